```python
import math
import jax
import jax.numpy as jnp
from jax import lax
import numpy as np

D_MODEL = 1024
BATCH = 16
SEQ = 256
DEPTH = 2
DEC_BATCH = 8
DEC_SEQ = 1024
PAST_LEN = 512

GRID_W = 64
N_EVEN = (DEPTH + 1) // 2
N_ODD = DEPTH // 2
RMS_EPS = 1e-6
NEG_INF = -1e30

SSD_INNER = D_MODEL
SSD_HEADDIM = 64
SSD_HEADS = SSD_INNER // SSD_HEADDIM
SSD_GROUPS = 2
SSD_STATE = 128
SSD_CONV = 5
SSD_CHUNK = 128
SSD_XBC = SSD_INNER + 2 * SSD_GROUPS * SSD_STATE

HY_CH = D_MODEL
HY_ORDER = 2
HY_SHORT = 3
HY_BANDS = 16
HY_FEAT = 1 + 2 * HY_BANDS
HY_HID = 64
HY_MIN_DECAY = abs(math.log(1e-2) / 1.5)
HY_MAX_DECAY = abs(math.log(1e-2) / 0.3)

EVEN_IN = SSD_INNER + SSD_XBC + SSD_HEADS + 3 * HY_CH
EVEN_OUT = SSD_INNER + HY_CH

HEAD_DIM = 64
NA_HEADS = D_MODEL // 128
NA_WIDTH = NA_HEADS * HEAD_DIM
NA_WIN_R = 8
NA_WIN_C = 16
NA_QCB = 16
NA_KCB = NA_QCB + NA_WIN_C

MLA_HEADS = D_MODEL // 128
MLA_Q_RANK = D_MODEL // 4
MLA_KV_RANK = D_MODEL // 8
MLA_NOPE = 64
MLA_ROPE = 32
MLA_V = 64
MLA_QK = MLA_NOPE + MLA_ROPE
ROPE_FREQS = MLA_ROPE // 4
ROPE_BASE = 10000.0

ODD_IN = 3 * NA_WIDTH + MLA_Q_RANK + MLA_KV_RANK + MLA_ROPE
ODD_OUT = NA_WIDTH + MLA_HEADS * MLA_V

MOE_GROUPS = 4
MOE_PER_GROUP = 8
MOE_EXPERTS = MOE_GROUPS * MOE_PER_GROUP
MOE_TOPK = 2
MOE_HIDDEN = D_MODEL // 4

ATTN_BLOCK = 128

kernel_name = 'hybrid_diffusion_prefix_step'


def rmsnorm(x, g):
    xf = x.astype(jnp.float32)
    y = xf * lax.rsqrt(jnp.mean(xf * xf, axis=-1, keepdims=True) + RMS_EPS)
    return (y * g.astype(jnp.float32)).astype(x.dtype)


def modulate(x, g, shift, scale):
    return rmsnorm(x, g) * (1.0 + scale[:, None]) + shift[:, None]


def dwconv_centred(x, w, b):
    k_w = w.shape[0]
    pad = k_w // 2
    L = x.shape[1]
    xp = jnp.pad(x, ((0, 0), (pad, pad), (0, 0)))
    y = b + xp[:, 0:L] * w[0]
    for k in range(1, k_w):
        y = y + xp[:, k:k + L] * w[k]
    return y


def merge_heads(o):
    b, h, L, d = o.shape
    return o.transpose(0, 2, 1, 3).reshape(b, L, h * d)


def softmax_attend(q, k, v, scale):
    b, h, Lq, dk = q.shape

    def attend(qb):
        s = jnp.einsum('bhqd,bhkd->bhqk', qb, k).astype(jnp.float32) * scale
        p = jax.nn.softmax(s, axis=-1).astype(v.dtype)
        return jnp.einsum('bhqk,bhkd->bhqd', p, v)

    if Lq <= ATTN_BLOCK or Lq % ATTN_BLOCK != 0:
        return attend(q)
    nb = Lq // ATTN_BLOCK
    qb = q.reshape(b, h, nb, ATTN_BLOCK, dk).transpose(2, 0, 1, 3, 4)
    out = lax.map(attend, qb)
    return out.transpose(1, 2, 0, 3, 4).reshape(b, h, Lq, v.shape[-1])


def axial_rope_tables(L):
    t = jnp.arange(L, dtype=jnp.int32)
    row = (t // GRID_W).astype(jnp.float32)
    col = (t % GRID_W).astype(jnp.float32)
    inv = ROPE_BASE ** (-jnp.arange(ROPE_FREQS, dtype=jnp.float32) / ROPE_FREQS)
    ang = jnp.stack([row[:, None] * inv, col[:, None] * inv], axis=1)
    return jnp.cos(ang), jnp.sin(ang)


def apply_axial_rope(x, cos, sin):
    shp = x.shape
    xr = x.astype(jnp.float32).reshape(shp[:-1] + (2, 2, ROPE_FREQS))
    x1, x2 = xr[..., 0, :], xr[..., 1, :]
    out = jnp.stack([x1 * cos - x2 * sin, x2 * cos + x1 * sin], axis=-2)
    return out.reshape(shp).astype(x.dtype)


def ssd_chunked(x, dt, Bm, Cm, A, init):
    b, L, H, P = x.shape
    G, N = Bm.shape[2], Bm.shape[3]
    T = SSD_CHUNK
    nc = L // T
    rep = H // G
    xdt = (x * dt[..., None]).reshape(b, nc, T, H, P)
    Bc = Bm.reshape(b, nc, T, G, N)
    Cc = Cm.reshape(b, nc, T, G, N)
    cs = jnp.cumsum((dt * A).reshape(b, nc, T, H), axis=2)
    causal = jnp.tril(jnp.ones((T, T), dtype=bool))[None, None, :, :, None]
    decay_in = jnp.exp(jnp.where(causal, cs[:, :, :, None, :] - cs[:, :, None, :, :], -jnp.inf))
    cb = jnp.repeat(jnp.einsum('bclgn,bcsgn->bclsg', Cc, Bc), rep, axis=-1)
    y_diag = jnp.einsum('bclsh,bcshp->bclhp', cb * decay_in, xdt)
    Bh = jnp.repeat(Bc, rep, axis=3)
    Ch = jnp.repeat(Cc, rep, axis=3)
    states = jnp.einsum('bclhn,bclhp->bchpn', Bh * jnp.exp(cs[:, :, -1:, :] - cs)[..., None], xdt)
    chunk_decay = jnp.exp(cs[:, :, -1, :])

    def step(carry, inp):
        st, dec = inp
        return carry * dec[:, :, None, None] + st, carry

    final, entering = lax.scan(step, init, (jnp.moveaxis(states, 1, 0), jnp.moveaxis(chunk_decay, 1, 0)))
    entering = jnp.moveaxis(entering, 0, 1)
    y_off = jnp.einsum('bclhn,bchpn->bclhp', Ch, entering) * jnp.exp(cs)[..., None]
    return (y_diag + y_off).reshape(b, L, H, P), final


def bidirectional_ssd(x, dt_raw, Bm, Cm, A_log, dt_bias, init):
    x, Bm, Cm = x.astype(jnp.float32), Bm.astype(jnp.float32), Cm.astype(jnp.float32)
    outs, finals = [], []
    for d in range(2):
        dt = jax.nn.softplus(dt_raw.astype(jnp.float32) + dt_bias[d].astype(jnp.float32))
        A = -jnp.exp(A_log[d].astype(jnp.float32))
        if d == 0:
            y, fin = ssd_chunked(x, dt, Bm, Cm, A, init[:, d].astype(jnp.float32))
        else:
            y, fin = ssd_chunked(jnp.flip(x, axis=1), jnp.flip(dt, axis=1), jnp.flip(Bm, axis=1),
                                 jnp.flip(Cm, axis=1), A, init[:, d].astype(jnp.float32))
            y = jnp.flip(y, axis=1)
        outs.append(y)
        finals.append(fin)
    return outs[0] + outs[1], jnp.stack(finals, axis=1)


def hyena_filters(L, w1, b1, w2, b2, w3, freq):
    t = jnp.linspace(0.0, 1.0, L, dtype=jnp.float32)[:, None]
    w = 2.0 * math.pi * jnp.arange(L, dtype=jnp.float32) / L
    bands = jnp.linspace(1e-4, HY_BANDS - 1, HY_BANDS, dtype=jnp.float32)
    ang = w[:, None] * bands[None]
    feat = jnp.concatenate([t, jnp.cos(ang), -jnp.sin(ang)], axis=-1)
    hdn = jnp.sin(freq[0] * (feat @ w1 + b1))
    hdn = jnp.sin(freq[1] * (hdn @ w2 + b2))
    h = (hdn @ w3).astype(jnp.float32).reshape(L, HY_ORDER, 2, HY_CH)
    deltas = jnp.linspace(HY_MIN_DECAY, HY_MAX_DECAY, HY_CH, dtype=jnp.float32)
    h = h * jnp.exp(-t * deltas)[:, None, None, :]
    fwd, bwd = h[:, :, 0], h[:, :, 1]
    return jnp.concatenate([fwd, jnp.zeros_like(fwd[:1]), jnp.flip(bwd[1:], axis=0)], axis=0)


def long_conv(u, h, bias):
    L = u.shape[1]
    uf = u.astype(jnp.float32)
    U = jnp.fft.rfft(uf, n=2 * L, axis=1)
    Hf = jnp.fft.rfft(h.astype(jnp.float32), axis=0)
    y = jnp.fft.irfft(U * Hf[None], n=2 * L, axis=1)[:, :L]
    return (y + uf * bias.astype(jnp.float32)).astype(u.dtype)


def even_mixer(h, init_state, w_in, conv_w, conv_b, A_log, dt_bias, d_skip, g_ssd,
               hy_conv_w, hy_conv_b, hy_w1, hy_b1, hy_w2, hy_b2, hy_w3, hy_freq, hy_bias, w_out):
    b, L, _ = h.shape
    proj = h @ w_in
    z, xbc, dt_raw, hy = jnp.split(proj, [SSD_INNER, SSD_INNER + SSD_XBC, SSD_INNER + SSD_XBC + SSD_HEADS], axis=-1)
    xbc = jax.nn.silu(dwconv_centred(xbc, conv_w, conv_b))
    xs, Bm, Cm = jnp.split(xbc, [SSD_INNER, SSD_INNER + SSD_GROUPS * SSD_STATE], axis=-1)
    xs = xs.reshape(b, L, SSD_HEADS, SSD_HEADDIM)
    Bm = Bm.reshape(b, L, SSD_GROUPS, SSD_STATE)
    Cm = Cm.reshape(b, L, SSD_GROUPS, SSD_STATE)
    y, final = bidirectional_ssd(xs, dt_raw, Bm, Cm, A_log, dt_bias, init_state)
    y = y + xs.astype(jnp.float32) * d_skip.astype(jnp.float32)[:, None]
    y = y.astype(h.dtype).reshape(b, L, SSD_INNER)
    y = rmsnorm(y * jax.nn.silu(z), g_ssd)
    parts = jnp.split(dwconv_centred(hy, hy_conv_w, hy_conv_b), 1 + HY_ORDER, axis=-1)
    filters = hyena_filters(L, hy_w1, hy_b1, hy_w2, hy_b2, hy_w3, hy_freq)
    u = parts[0]
    for o in range(HY_ORDER):
        u = parts[o + 1] * long_conv(u, filters[:, o], hy_bias[o])
    return jnp.concatenate([y, u], axis=-1) @ w_out, final


def odd_project(h, w_in, g_q, w_uq, g_kv):
    b, L, _ = h.shape
    proj = h @ w_in
    qkv, q_dn, kv_dn, k_rope = jnp.split(
        proj, [3 * NA_WIDTH, 3 * NA_WIDTH + MLA_Q_RANK, 3 * NA_WIDTH + MLA_Q_RANK + MLA_KV_RANK], axis=-1)
    qkv = qkv.reshape(b, L, 3, NA_HEADS, HEAD_DIM).transpose(2, 0, 3, 1, 4)
    q_mla = (rmsnorm(q_dn, g_q) @ w_uq).reshape(b, L, MLA_HEADS, MLA_QK).transpose(0, 2, 1, 3)
    ckv = rmsnorm(kv_dn, g_kv)
    return qkv[0], qkv[1], qkv[2], q_mla, ckv, k_rope


def mla_keys(ckv, k_rope, w_ukv):
    b, L, _ = ckv.shape
    kv = (ckv @ w_ukv).reshape(b, L, MLA_HEADS, MLA_NOPE + MLA_V).transpose(0, 2, 1, 3)
    k = jnp.concatenate([kv[..., :MLA_NOPE],
                         jnp.broadcast_to(k_rope[:, None].astype(kv.dtype), (b, MLA_HEADS, L, MLA_ROPE))], axis=-1)
    return k, kv[..., MLA_NOPE:]


def neighbourhood_attend(q, k, v, k_ctx, v_ctx, rel_bias):
    b, h, L, dh = q.shape
    rows = L // GRID_W
    kr = min(NA_WIN_R, rows)
    ncb = GRID_W // NA_QCB
    scale = dh ** -0.5
    r = jnp.arange(rows)
    r0 = jnp.clip(r - kr // 2, 0, rows - kr)
    qcol = jnp.arange(GRID_W).reshape(ncb, NA_QCB)
    c0 = jnp.clip(qcol - NA_WIN_C // 2, 0, GRID_W - NA_WIN_C)
    kb0 = jnp.clip(jnp.arange(ncb) * NA_QCB - NA_WIN_C // 2, 0, GRID_W - NA_KCB)
    col_idx = kb0[:, None] + jnp.arange(NA_KCB)[None]
    col_ok = (col_idx[:, None, :] >= c0[..., None]) & (col_idx[:, None, :] < c0[..., None] + NA_WIN_C)
    dr = r0[:, None] + jnp.arange(kr)[None] - r[:, None] + NA_WIN_R - 1
    dc = jnp.clip(col_idx[:, None, :] - qcol[..., None], -(NA_WIN_C - 1), NA_WIN_C - 1) + NA_WIN_C - 1
    bias = jnp.take(jnp.take(rel_bias.astype(jnp.float32), dr, axis=1), dc, axis=3)
    bias = jnp.where(col_ok[None, None, None], bias, NEG_INF).transpose(1, 0, 3, 4, 2, 5)
    k_grid = k.reshape(b, h, rows, GRID_W, dh)
    v_grid = v.reshape(b, h, rows, GRID_W, dh)
    q_rows = q.reshape(b, h, rows, ncb, NA_QCB, dh).transpose(2, 0, 1, 3, 4, 5)
    nwin = kr * NA_KCB

    def row_block(args):
        q_r, start, bias_r = args
        k_r = jnp.take(lax.dynamic_slice_in_dim(k_grid, start, kr, axis=2), col_idx, axis=3)
        v_r = jnp.take(lax.dynamic_slice_in_dim(v_grid, start, kr, axis=2), col_idx, axis=3)
        s_win = jnp.einsum('bhjqd,bhajkd->bhjqak', q_r, k_r).astype(jnp.float32) * scale + bias_r[None]
        s_ctx = jnp.einsum('bhjqd,bhcd->bhjqc', q_r, k_ctx).astype(jnp.float32) * scale
        s = jnp.concatenate([s_win.reshape(b, h, ncb, NA_QCB, nwin), s_ctx], axis=-1)
        p = jax.nn.softmax(s, axis=-1).astype(v.dtype)
        p_win = p[..., :nwin].reshape(b, h, ncb, NA_QCB, kr, NA_KCB)
        return (jnp.einsum('bhjqak,bhajkd->bhjqd', p_win, v_r)
                + jnp.einsum('bhjqc,bhcd->bhjqd', p[..., nwin:], v_ctx.astype(v.dtype)))

    out = lax.map(row_block, (q_rows, r0, bias))
    return out.transpose(1, 2, 0, 3, 4, 5).reshape(b, h, L, dh)


def odd_mixer_context(h, w_in, g_q, w_uq, g_kv, w_ukv, w_out):
    q_na, k_na, v_na, q_mla, ckv, k_rope = odd_project(h, w_in, g_q, w_uq, g_kv)
    o_na = softmax_attend(q_na, k_na, v_na, HEAD_DIM ** -0.5)
    k_m, v_m = mla_keys(ckv, k_rope, w_ukv)
    o_mla = softmax_attend(q_mla, k_m, v_m, MLA_QK ** -0.5)
    out = jnp.concatenate([merge_heads(o_na), merge_heads(o_mla)], axis=-1) @ w_out
    return out, k_na, v_na, ckv, k_rope


def odd_mixer_latent(h, k_na_c, v_na_c, ckv_c, krope_c, rel_bias, w_in, g_q, w_uq, g_kv, w_ukv, w_out):
    q_na, k_na, v_na, q_mla, ckv, k_rope = odd_project(h, w_in, g_q, w_uq, g_kv)
    o_na = neighbourhood_attend(q_na, k_na, v_na, k_na_c, v_na_c, rel_bias)
    cos, sin = axial_rope_tables(h.shape[1])
    q_mla = jnp.concatenate([q_mla[..., :MLA_NOPE], apply_axial_rope(q_mla[..., MLA_NOPE:], cos, sin)], axis=-1)
    k_lat, v_lat = mla_keys(ckv, apply_axial_rope(k_rope, cos, sin), w_ukv)
    k_ctx, v_ctx = mla_keys(ckv_c.astype(ckv.dtype), krope_c, w_ukv)
    o_mla = softmax_attend(q_mla, jnp.concatenate([k_lat, k_ctx], axis=2),
                           jnp.concatenate([v_lat, v_ctx], axis=2), MLA_QK ** -0.5)
    return jnp.concatenate([merge_heads(o_na), merge_heads(o_mla)], axis=-1) @ w_out


def hier_moe(h, w_gr, b_gr, w_er, b_er, w_gate, w_up, w_down):
    b, L, D = h.shape
    n_tok = b * L
    t = h.reshape(n_tok, D)
    g_prob = jax.nn.softmax((t @ w_gr).astype(jnp.float32) + b_gr.astype(jnp.float32), axis=-1)
    g_w, g_idx = lax.top_k(g_prob, 1)
    e_logits = ((t @ w_er).astype(jnp.float32) + b_er.astype(jnp.float32)).reshape(n_tok, MOE_GROUPS, MOE_PER_GROUP)
    e_prob = jax.nn.softmax(jnp.take_along_axis(e_logits, g_idx[:, :, None], axis=1)[:, 0], axis=-1)
    e_w, e_idx = lax.top_k(e_prob, MOE_TOPK)
    weights = g_w * e_w / jnp.sum(e_w, axis=-1, keepdims=True)
    expert = g_idx * MOE_PER_GROUP + e_idx
    combine = jnp.zeros((n_tok, MOE_EXPERTS), jnp.float32).at[jnp.arange(n_tok)[:, None], expert].add(weights)
    hid = jax.nn.silu(jnp.einsum('td,edf->tef', t, w_gate)) * jnp.einsum('td,edf->tef', t, w_up)
    out = jnp.einsum('tef,efd->td', hid * combine[..., None].astype(hid.dtype), w_down)
    return out.reshape(b, L, D).astype(h.dtype)


def setup_inputs(seed: int = 0) -> dict:
    key = jax.random.key(seed)
    keys = iter(jax.random.split(key, 64))

    def nrm(shape, scale):
        return jax.random.normal(next(keys), shape, jnp.float32) * scale

    def gain(shape):
        return 1.0 + nrm(shape, 0.01)

    dt0 = jnp.exp(jax.random.uniform(next(keys), (N_EVEN, 2, SSD_HEADS), jnp.float32,
                                     math.log(1e-3), math.log(1e-1)))
    a0 = jax.random.uniform(next(keys), (N_EVEN, 2, SSD_HEADS), jnp.float32, 1.0, 16.0)
    return {
        'x_prompt': nrm((BATCH, SEQ, D_MODEL), 1.0),
        'x_sample': nrm((DEC_BATCH, DEC_SEQ, D_MODEL), 1.0),
        'state_ssd': nrm((DEC_BATCH, N_EVEN, 2, SSD_HEADS, SSD_HEADDIM, SSD_STATE), 0.1),
        'cache_na_k': nrm((DEC_BATCH, N_ODD, NA_HEADS, PAST_LEN, HEAD_DIM), 1.0),
        'cache_na_v': nrm((DEC_BATCH, N_ODD, NA_HEADS, PAST_LEN, HEAD_DIM), 1.0),
        'cache_mla_ckv': nrm((DEC_BATCH, N_ODD, PAST_LEN, MLA_KV_RANK), 1.0),
        'cache_mla_krope': nrm((DEC_BATCH, N_ODD, PAST_LEN, MLA_ROPE), 1.0),
        'c': nrm((DEC_BATCH, D_MODEL), 1.0),
        'c_ctx': nrm((D_MODEL,), 1.0),
        'w_ada': nrm((DEPTH, D_MODEL, 6 * D_MODEL), 0.5 * D_MODEL ** -0.5),
        'b_ada': nrm((DEPTH, 6 * D_MODEL), 0.01),
        'norm_mix': gain((DEPTH, D_MODEL)),
        'norm_ffn': gain((DEPTH, D_MODEL)),
        'norm_final': gain((D_MODEL,)),
        'ev_w_in': nrm((N_EVEN, D_MODEL, EVEN_IN), D_MODEL ** -0.5),
        'ev_conv_w': nrm((N_EVEN, SSD_CONV, SSD_XBC), SSD_CONV ** -0.5),
        'ev_conv_b': nrm((N_EVEN, SSD_XBC), 0.02),
        'ssd_A_log': jnp.log(a0),
        'ssd_dt_bias': dt0 + jnp.log(-jnp.expm1(-dt0)),
        'ssd_d': gain((N_EVEN, SSD_HEADS)),
        'ssd_norm': gain((N_EVEN, SSD_INNER)),
        'hy_conv_w': nrm((N_EVEN, HY_SHORT, 3 * HY_CH), HY_SHORT ** -0.5),
        'hy_conv_b': nrm((N_EVEN, 3 * HY_CH), 0.02),
        'hy_w1': nrm((N_EVEN, HY_FEAT, HY_HID), HY_FEAT ** -0.5),
        'hy_b1': nrm((N_EVEN, HY_HID), 0.1),
        'hy_w2': nrm((N_EVEN, HY_HID, HY_HID), HY_HID ** -0.5),
        'hy_b2': nrm((N_EVEN, HY_HID), 0.1),
        'hy_w3': nrm((N_EVEN, HY_HID, HY_ORDER * 2 * HY_CH), 0.02),
        'hy_freq': 1.0 + nrm((N_EVEN, 2, HY_HID), 0.1),
        'hy_bias': nrm((N_EVEN, HY_ORDER, HY_CH), 0.5),
        'ev_w_out': nrm((N_EVEN, EVEN_OUT, D_MODEL), EVEN_OUT ** -0.5),
        'od_w_in': nrm((N_ODD, D_MODEL, ODD_IN), D_MODEL ** -0.5),
        'mla_q_norm': gain((N_ODD, MLA_Q_RANK)),
        'mla_w_uq': nrm((N_ODD, MLA_Q_RANK, MLA_HEADS * MLA_QK), MLA_Q_RANK ** -0.5),
        'mla_kv_norm': gain((N_ODD, MLA_KV_RANK)),
        'mla_w_ukv': nrm((N_ODD, MLA_KV_RANK, MLA_HEADS * (MLA_NOPE + MLA_V)), MLA_KV_RANK ** -0.5),
        'na_rel_bias': nrm((N_ODD, NA_HEADS, 2 * NA_WIN_R - 1, 2 * NA_WIN_C - 1), 0.5),
        'od_w_out': nrm((N_ODD, ODD_OUT, D_MODEL), ODD_OUT ** -0.5),
        'moe_w_gr': nrm((DEPTH, D_MODEL, MOE_GROUPS), D_MODEL ** -0.5),
        'moe_b_gr': nrm((DEPTH, MOE_GROUPS), 0.01),
        'moe_w_er': nrm((DEPTH, D_MODEL, MOE_EXPERTS), D_MODEL ** -0.5),
        'moe_b_er': nrm((DEPTH, MOE_EXPERTS), 0.01),
        'moe_w_gate': nrm((DEPTH, MOE_EXPERTS, D_MODEL, MOE_HIDDEN), D_MODEL ** -0.5),
        'moe_w_up': nrm((DEPTH, MOE_EXPERTS, D_MODEL, MOE_HIDDEN), D_MODEL ** -0.5),
        'moe_w_down': nrm((DEPTH, MOE_EXPERTS, MOE_HIDDEN, D_MODEL), MOE_HIDDEN ** -0.5),
    }


def reference(x_prompt, x_sample, state_ssd, cache_na_k, cache_na_v, cache_mla_ckv, cache_mla_krope, c, c_ctx,
              w_ada, b_ada, norm_mix, norm_ffn, norm_final, ev_w_in, ev_conv_w, ev_conv_b, ssd_A_log, ssd_dt_bias,
              ssd_d, ssd_norm, hy_conv_w, hy_conv_b, hy_w1, hy_b1, hy_w2, hy_b2, hy_w3, hy_freq, hy_bias, ev_w_out,
              od_w_in, mla_q_norm, mla_w_uq, mla_kv_norm, mla_w_ukv, na_rel_bias, od_w_out,
              moe_w_gr, moe_b_gr, moe_w_er, moe_b_er, moe_w_gate, moe_w_up, moe_w_down):
    xc, xl = x_prompt, x_sample
    new_ssd, new_na_k, new_na_v, new_ckv, new_krope = [], [], [], [], []
    for i in range(DEPTH):
        mod_c = jnp.split(jax.nn.silu(c_ctx)[None] @ w_ada[i] + b_ada[i], 6, axis=-1)
        mod_l = jnp.split(jax.nn.silu(c) @ w_ada[i] + b_ada[i], 6, axis=-1)
        hc = modulate(xc, norm_mix[i], mod_c[0], mod_c[1])
        hl = modulate(xl, norm_mix[i], mod_l[0], mod_l[1])
        if i % 2 == 0:
            e = i // 2
            ev = (ev_w_in[e], ev_conv_w[e], ev_conv_b[e], ssd_A_log[e], ssd_dt_bias[e], ssd_d[e], ssd_norm[e],
                  hy_conv_w[e], hy_conv_b[e], hy_w1[e], hy_b1[e], hy_w2[e], hy_b2[e], hy_w3[e], hy_freq[e],
                  hy_bias[e], ev_w_out[e])
            init_c = jnp.zeros((xc.shape[0], 2, SSD_HEADS, SSD_HEADDIM, SSD_STATE), jnp.float32)
            oc, fin_c = even_mixer(hc, init_c, *ev)
            ol, _ = even_mixer(hl, state_ssd[:, e], *ev)
            new_ssd.append(fin_c)
        else:
            o = i // 2
            od = (od_w_in[o], mla_q_norm[o], mla_w_uq[o], mla_kv_norm[o], mla_w_ukv[o], od_w_out[o])
            oc, k_c, v_c, ckv_c, kr_c = odd_mixer_context(hc, *od)
            ol = odd_mixer_latent(hl, cache_na_k[:, o], cache_na_v[:, o], cache_mla_ckv[:, o],
                                  cache_mla_krope[:, o], na_rel_bias[o], *od)
            new_na_k.append(k_c)
            new_na_v.append(v_c)
            new_ckv.append(ckv_c)
            new_krope.append(kr_c)
        xc = xc + mod_c[2][:, None] * oc
        xl = xl + mod_l[2][:, None] * ol
        moe = (moe_w_gr[i], moe_b_gr[i], moe_w_er[i], moe_b_er[i], moe_w_gate[i], moe_w_up[i], moe_w_down[i])
        xc = xc + mod_c[5][:, None] * hier_moe(modulate(xc, norm_ffn[i], mod_c[3], mod_c[4]), *moe)
        xl = xl + mod_l[5][:, None] * hier_moe(modulate(xl, norm_ffn[i], mod_l[3], mod_l[4]), *moe)
    y_prompt = rmsnorm(xc, norm_final)
    y_sample = rmsnorm(xl, norm_final)
    new_state_ssd = jnp.stack(new_ssd, axis=1)
    new_cache_na_k = jnp.stack(new_na_k, axis=1)
    new_cache_na_v = jnp.stack(new_na_v, axis=1)
    new_cache_mla_ckv = jnp.stack(new_ckv, axis=1)
    new_cache_mla_krope = jnp.stack(new_krope, axis=1)
    return (y_prompt, y_sample, new_state_ssd, new_cache_na_k, new_cache_na_v, new_cache_mla_ckv, new_cache_mla_krope)
```

```python
import functools
import math

import jax
import jax.numpy as jnp
from jax import lax
from jax.experimental import pallas as pl
from jax.experimental.pallas import tpu as pltpu

F32 = jnp.float32
BF16 = jnp.bfloat16
HIGHEST = lax.Precision.HIGHEST

D = 1024
N_CTX, L_CTX = 16, 256
N_LAT, L_LAT = 8, 1024
T_CTX = N_CTX * L_CTX
T_LAT = N_LAT * L_LAT
T_ALL = T_CTX + T_LAT
PAST = 512
GRID_W = 64
EPS = 1e-6
NEG = -1e30

SSD_H, SSD_P, SSD_N, SSD_G = 16, 64, 128, 2
SSD_XBC = D + 2 * SSD_G * SSD_N
SSD_K = 5
CHUNK = 128

HY_K = 3
HY_BANDS = 16
HY_FEAT = 1 + 2 * HY_BANDS
HY_HID = 64
HY_MIN_DECAY = abs(math.log(1e-2) / 1.5)
HY_MAX_DECAY = abs(math.log(1e-2) / 0.3)

NA_H, NA_D = 8, 64
NA_W = NA_H * NA_D
NA_WIN_R, NA_WIN_C = 8, 16
MLA_H, MLA_QR, MLA_KVR = 8, 256, 128
MLA_NOPE, MLA_ROPE, MLA_V = 64, 32, 64
MLA_QK = MLA_NOPE + MLA_ROPE
ROPE_F = MLA_ROPE // 4

MOE_G, MOE_PG, MOE_E, MOE_F = 4, 8, 32, 256

LANES = 128
SUBLANES = 8
VMEM_LIMIT = 56 * 1024 * 1024

TM = 256
N_TILES = T_ALL // TM
CTX_TILES = T_CTX // TM
LAT_TILES_PER_SEQ = L_LAT // TM
MOD_ROWS = 16


def _cparams(sem):
    return pltpu.CompilerParams(dimension_semantics=sem, vmem_limit_bytes=VMEM_LIMIT)


def _mod_row(i):
    return jnp.where(i < CTX_TILES, 0, 1 + (i - CTX_TILES) // LAT_TILES_PER_SEQ)


def _silu(x):
    return x * jax.nn.sigmoid(x)


def _rms(x):
    return x * lax.rsqrt(jnp.mean(x * x, axis=-1, keepdims=True) + EPS)


def _ada_kernel(c_ref, w_ref, b_ref, o_ref):
    c = c_ref[...]
    o_ref[0] = jnp.dot(_silu(c), w_ref[0], precision=HIGHEST, preferred_element_type=F32) + b_ref[0]


def ada_modulation(cvec, w_ada, b_ada):
    depth = w_ada.shape[0]
    out = pl.pallas_call(
        _ada_kernel,
        grid=(depth, 6),
        in_specs=[
            pl.BlockSpec((MOD_ROWS, D), lambda l, j: (0, 0)),
            pl.BlockSpec((1, D, D), lambda l, j: (l, 0, j)),
            pl.BlockSpec((1, 1, D), lambda l, j: (l, 0, j)),
        ],
        out_specs=pl.BlockSpec((1, MOD_ROWS, D), lambda l, j: (l, 0, j)),
        out_shape=jax.ShapeDtypeStruct((depth, MOD_ROWS, 6 * D), F32),
        compiler_params=_cparams(("arbitrary", "arbitrary")),
        name="ada",
    )(cvec, w_ada, b_ada.reshape(depth, 1, 6 * D))
    return out.reshape(depth, MOD_ROWS, 6, D)


PROJ_CHUNK = 512


def _modulated(x, g_ref, mod_ref, shift_row):
    h = _rms(x) * g_ref[...]
    return h * (1.0 + mod_ref[0, shift_row + 1:shift_row + 2, :]) + mod_ref[0, shift_row:shift_row + 1, :]


def _even_in_kernel(x_ref, mod_ref, g_ref, w_ref, z_ref, xbc_ref, hy_ref, dt_ref):
    hb = _modulated(x_ref[...], g_ref, mod_ref, 0).astype(BF16)
    col = 0
    for o_ref in (z_ref, xbc_ref, hy_ref, dt_ref):
        width = o_ref.shape[1]
        for c0 in range(0, width, PROJ_CHUNK):
            c1 = min(c0 + PROJ_CHUNK, width)
            o_ref[:, c0:c1] = jnp.dot(hb, w_ref[:, col + c0:col + c1], preferred_element_type=F32)
        col += width


def even_in_proj(x, mod, g, w_bf):
    widths = (D, SSD_XBC, 3 * D, LANES)
    return pl.pallas_call(
        _even_in_kernel,
        grid=(N_TILES,),
        in_specs=[
            pl.BlockSpec((TM, D), lambda i: (i, 0)),
            pl.BlockSpec((1, 6, D), lambda i: (_mod_row(i), 0, 0)),
            pl.BlockSpec((1, D), lambda i: (0, 0)),
            pl.BlockSpec(w_bf.shape, lambda i: (0, 0)),
        ],
        out_specs=[pl.BlockSpec((TM, w), lambda i: (i, 0)) for w in widths],
        out_shape=[jax.ShapeDtypeStruct((T_ALL, w), F32) for w in widths],
        compiler_params=_cparams(("arbitrary",)),
        name="even_in",
    )(x, mod, g, w_bf)


PAD = SUBLANES


def _ssd_kernel(*refs, L, has_init):
    if has_init:
        (xbc_ref, dt_ref, z_ref, init_ref, cw_ref, cb_ref, dtb_ref, alog_ref, dsk_ref, gs_ref,
         y_ref, xp_s, xc_s, ya_s, st_s) = refs
        fin_ref = None
    else:
        (xbc_ref, dt_ref, z_ref, cw_ref, cb_ref, dtb_ref, alog_ref, dsk_ref, gs_ref,
         y_ref, fin_ref, xp_s, xc_s, ya_s, st_s) = refs
        init_ref = None
    nc = L // CHUNK
    half = SSD_K // 2

    xp_s[0:PAD, :] = jnp.zeros((PAD, SSD_XBC), F32)
    xp_s[PAD + L:2 * PAD + L, :] = jnp.zeros((PAD, SSD_XBC), F32)
    xp_s[PAD:PAD + L, :] = xbc_ref[...]
    for c in range(nc):
        base = PAD + c * CHUNK - half
        acc = cb_ref[...] + xp_s[base:base + CHUNK, :] * cw_ref[0:1, :]
        for k in range(1, SSD_K):
            acc = acc + xp_s[base + k:base + k + CHUNK, :] * cw_ref[k:k + 1, :]
        xc_s[c * CHUNK:(c + 1) * CHUNK, :] = _silu(acc)

    row = lax.broadcasted_iota(jnp.int32, (CHUNK, CHUNK), 0)
    colm = lax.broadcasted_iota(jnp.int32, (CHUNK, CHUNK), 1)
    lane_lo = colm < SSD_P
    tri_lo = (colm <= row).astype(F32)
    tri_up = (colm >= row).astype(F32)

    for d in range(2):
        causal = (colm <= row) if d == 0 else (colm >= row)
        for j in range(SSD_H * SSD_P // CHUNK):
            if has_init:
                st_s[:, j * CHUNK:(j + 1) * CHUNK] = init_ref[0, d, j * CHUNK:(j + 1) * CHUNK, :].T
            else:
                st_s[:, j * CHUNK:(j + 1) * CHUNK] = jnp.zeros((CHUNK, CHUNK), F32)

        def chunk_body(ci, carry, d=d, causal=causal):
            c = ci if d == 0 else nc - 1 - ci
            r0 = pl.multiple_of(c * CHUNK, CHUNK)
            dt = jax.nn.softplus(dt_ref[pl.ds(r0, CHUNK), :] + dtb_ref[d:d + 1, :])
            a = dt * (-jnp.exp(alog_ref[d:d + 1, :]))
            tri = tri_lo if d == 0 else tri_up
            cs = jnp.dot(tri, a, precision=HIGHEST, preferred_element_type=F32)
            cs_t = jnp.dot(a.T, tri.T, precision=HIGHEST, preferred_element_type=F32)
            edge = cs[CHUNK - 1:CHUNK, :] if d == 0 else cs[0:1, :]
            ecs = jnp.exp(cs)
            dec = jnp.exp(edge - cs)
            cdec = jnp.exp(edge)
            for g in range(SSD_G):
                bm = xc_s[pl.ds(r0, CHUNK), D + g * SSD_N:D + (g + 1) * SSD_N]
                cm = xc_s[pl.ds(r0, CHUNK), D + (SSD_G + g) * SSD_N:D + (SSD_G + g + 1) * SSD_N]
                bm_b, cm_b = bm.astype(BF16), cm.astype(BF16)
                cb = lax.dot_general(cm_b, bm_b, (((1,), (1,)), ((), ())), preferred_element_type=F32)
                bm_t = bm.T.astype(BF16)
                pairs = SSD_H // SSD_G // 2
                for pp in range(pairs):
                    p = g * pairs + pp
                    h0, h1 = 2 * p, 2 * p + 1
                    cols = slice(p * CHUNK, (p + 1) * CHUNK)
                    xs = xc_s[pl.ds(r0, CHUNK), cols]
                    xdt = xs * jnp.where(lane_lo, dt[:, h0:h0 + 1], dt[:, h1:h1 + 1])
                    ms = []
                    for h in (h0, h1):
                        diff = cs[:, h:h + 1] - cs_t[h:h + 1, :]
                        ms.append(cb * jnp.exp(jnp.where(causal, diff, NEG)))
                    mcat = jnp.concatenate(ms, axis=1).astype(BF16)
                    xbd = jnp.concatenate([jnp.where(lane_lo, xdt, 0.0), jnp.where(lane_lo, 0.0, xdt)],
                                          axis=0).astype(BF16)
                    y_diag = jnp.dot(mcat, xbd, preferred_element_type=F32)
                    st = st_s[:, cols]
                    y_off = jnp.dot(cm_b, st.astype(BF16), preferred_element_type=F32)
                    y_off = y_off * jnp.where(lane_lo, ecs[:, h0:h0 + 1], ecs[:, h1:h1 + 1])
                    y = y_diag + y_off
                    if d == 0:
                        ya_s[pl.ds(r0, CHUNK), cols] = y
                    else:
                        ya_s[pl.ds(r0, CHUNK), cols] = ya_s[pl.ds(r0, CHUNK), cols] + y
                    xdd = (xdt * jnp.where(lane_lo, dec[:, h0:h0 + 1], dec[:, h1:h1 + 1])).astype(BF16)
                    snew = jnp.dot(bm_t, xdd, preferred_element_type=F32)
                    st_s[:, cols] = st * jnp.where(lane_lo[0:1, :], cdec[:, h0:h0 + 1], cdec[:, h1:h1 + 1]) + snew
            return carry

        lax.fori_loop(0, nc, chunk_body, 0)
        if fin_ref is not None:
            for j in range(SSD_H * SSD_P // CHUNK):
                fin_ref[0, d, j * CHUNK:(j + 1) * CHUNK, :] = st_s[:, j * CHUNK:(j + 1) * CHUNK].T

    def out_body(c, carry):
        r0 = pl.multiple_of(c * CHUNK, CHUNK)
        y = ya_s[pl.ds(r0, CHUNK), :] + xc_s[pl.ds(r0, CHUNK), 0:D] * dsk_ref[...]
        y = y * _silu(z_ref[pl.ds(r0, CHUNK), :])
        y_ref[pl.ds(r0, CHUNK), :] = (_rms(y) * gs_ref[...]).astype(y_ref.dtype)
        return carry

    lax.fori_loop(0, nc, out_body, 0)


def ssd_mixer(xbc, dtr, z, init, cw, cb, dtb, alog, dsk, gs, *, L, n_seq, row_off):
    blk0 = row_off // L
    has_init = init is not None
    seq = lambda w: pl.BlockSpec((L, w), lambda b: (blk0 + b, 0))
    full = lambda arr: pl.BlockSpec(arr.shape, lambda b: (0,) * arr.ndim)
    in_specs = [seq(SSD_XBC), seq(LANES), seq(D)]
    args = [xbc, dtr, z]
    if has_init:
        in_specs.append(pl.BlockSpec((1, 2, SSD_H * SSD_P, SSD_N), lambda b: (b, 0, 0, 0)))
        args.append(init)
    small = [cw, cb, dtb, alog, dsk, gs]
    in_specs += [full(a) for a in small]
    args += small
    out_specs = [pl.BlockSpec((L, D), lambda b: (b, 0))]
    out_shape = [jax.ShapeDtypeStruct((n_seq * L, D), BF16)]
    if not has_init:
        out_specs.append(pl.BlockSpec((1, 2, SSD_H * SSD_P, SSD_N), lambda b: (b, 0, 0, 0)))
        out_shape.append(jax.ShapeDtypeStruct((n_seq, 2, SSD_H * SSD_P, SSD_N), F32))
    return pl.pallas_call(
        functools.partial(_ssd_kernel, L=L, has_init=has_init),
        grid=(n_seq,),
        in_specs=in_specs,
        out_specs=out_specs,
        out_shape=out_shape,
        scratch_shapes=[
            pltpu.VMEM((L + 2 * PAD, SSD_XBC), F32),
            pltpu.VMEM((L, SSD_XBC), F32),
            pltpu.VMEM((L, D), F32),
            pltpu.VMEM((SSD_N, SSD_H * SSD_P), F32),
        ],
        compiler_params=_cparams(("arbitrary",)),
        name=f"ssd_{L}",
    )(*args)


HY_CB = 256


def dft_matrices(L):
    k = jnp.arange(L, dtype=jnp.int32)
    ang = ((k[:, None] * k[None, :]) % (2 * L)).astype(F32) * (math.pi / L)
    alt = jnp.where(k % 2 == 0, 1.0, -1.0).astype(F32)
    cosm = jnp.cos(ang)
    sinm = jnp.sin(ang).at[0].set(alt)
    fwd = jnp.concatenate([cosm, sinm], axis=0)
    wts = jnp.where(k == 0, 1.0, 2.0).astype(F32) / (2 * L)
    inv = jnp.concatenate([cosm * wts[None, :], sinm.T * wts[None, :]], axis=1)

    def split(m):
        hi = m.astype(BF16)
        return hi, (m - hi.astype(F32)).astype(BF16)

    return split(fwd) + split(inv)


def _dot3(a_hi, a_lo, b):
    b_hi = b.astype(BF16)
    b_lo = (b - b_hi.astype(F32)).astype(BF16)
    return (jnp.dot(a_hi, b_hi, preferred_element_type=F32) + jnp.dot(a_lo, b_hi, preferred_element_type=F32)
            + jnp.dot(a_hi, b_lo, preferred_element_type=F32))


def _const_spec(arr):
    return pl.BlockSpec(arr.shape, lambda *_: (0,) * arr.ndim, pipeline_mode=pl.Buffered(1))


def _hy_filter_kernel(feat_ref, w1_ref, b1_ref, w2_ref, b2_ref, fr_ref, w3_ref, dl_ref, fh_ref, fl_ref, o_ref, *, L):
    hp = functools.partial(jnp.dot, precision=HIGHEST, preferred_element_type=F32)
    hdn = jnp.sin(fr_ref[0:1, :] * (hp(feat_ref[...], w1_ref[...]) + b1_ref[...]))
    hdn = jnp.sin(fr_ref[1:2, :] * (hp(hdn, w2_ref[...]) + b2_ref[...]))
    rowi = lax.broadcasted_iota(jnp.int32, (L, 1), 0)
    t = rowi.astype(F32) * (1.0 / (L - 1))
    dec = jnp.exp(-t * dl_ref[...])
    first = rowi == 0
    for o in range(2):
        fwd = hp(hdn, w3_ref[2 * o]) * dec
        bwd = jnp.where(first, 0.0, hp(hdn, w3_ref[2 * o + 1]) * dec)
        ss = _dot3(fh_ref[...], fl_ref[...], fwd + bwd)
        sd = _dot3(fh_ref[...], fl_ref[...], fwd - bwd)
        hr = ss[0:L]
        o_ref[o, 0] = hr
        o_ref[o, 1] = jnp.where(first, 0.0, sd[L:2 * L])
        o_ref[o, 2] = jnp.where(first, ss[L:L + 1], hr)


def hyena_filter_spectra(feat, w1, b1, w2, b2, freq, w3r, deltas, f_hi, f_lo, *, L):
    full = lambda arr: pl.BlockSpec(arr.shape, lambda j: (0,) * arr.ndim)
    return pl.pallas_call(
        functools.partial(_hy_filter_kernel, L=L),
        grid=(D // HY_CB,),
        in_specs=[full(feat), full(w1), full(b1), full(w2), full(b2), full(freq),
                  pl.BlockSpec((4, HY_HID, HY_CB), lambda j: (0, 0, j)),
                  pl.BlockSpec((1, HY_CB), lambda j: (0, j)),
                  _const_spec(f_hi), _const_spec(f_lo)],
        out_specs=pl.BlockSpec((2, 3, L, HY_CB), lambda j: (0, 0, 0, j)),
        out_shape=jax.ShapeDtypeStruct((2, 3, L, D), F32),
        compiler_params=_cparams(("arbitrary",)),
        name=f"hy_filter_{L}",
    )(feat, w1, b1, w2, b2, freq, w3r, deltas, f_hi, f_lo)


def _hyena_kernel(p0_ref, p1_ref, p2_ref, w0_ref, w1_ref, w2_ref, b0_ref, b1_ref, b2_ref, h_ref, hb_ref,
                  fh_ref, fl_ref, gh_ref, gl_ref, o_ref, xp_s, *, L):
    xp_s[0:PAD, :] = jnp.zeros((PAD, HY_CB), F32)
    xp_s[PAD + L:2 * PAD + L, :] = jnp.zeros((PAD, HY_CB), F32)

    def conv(p_ref, w_ref, b_ref):
        xp_s[PAD:PAD + L, :] = p_ref[...]
        acc = b_ref[...] + xp_s[PAD - 1:PAD - 1 + L, :] * w_ref[0:1, :]
        for k in range(1, HY_K):
            acc = acc + xp_s[PAD - 1 + k:PAD - 1 + k + L, :] * w_ref[k:k + 1, :]
        return acc

    u = conv(p0_ref, w0_ref, b0_ref)
    for o, (p_ref, w_ref, b_ref) in enumerate(((p1_ref, w1_ref, b1_ref), (p2_ref, w2_ref, b2_ref))):
        spec = _dot3(fh_ref[...], fl_ref[...], u)
        ar, ai = spec[0:L], spec[L:2 * L]
        yr = ar * h_ref[o, 0] - ai * h_ref[o, 1]
        yn = ar * h_ref[o, 1] + ai * h_ref[o, 2]
        y = _dot3(gh_ref[...], gl_ref[...], jnp.concatenate([yr, yn], axis=0))
        u = conv(p_ref, w_ref, b_ref) * (y + u * hb_ref[o:o + 1, :])
    o_ref[...] = u.astype(o_ref.dtype)


def hyena_mixer(hy, conv_w, conv_b, spectra, hy_bias, mats, *, L, n_seq, row_off):
    blk0 = row_off // L
    nj = D // HY_CB
    part = lambda q: pl.BlockSpec((L, HY_CB), lambda j, b: (blk0 + b, q * nj + j))
    wpart = lambda q: pl.BlockSpec((HY_K, HY_CB), lambda j, b: (0, q * nj + j))
    bpart = lambda q: pl.BlockSpec((1, HY_CB), lambda j, b: (0, q * nj + j))
    return pl.pallas_call(
        functools.partial(_hyena_kernel, L=L),
        grid=(nj, n_seq),
        in_specs=[part(0), part(1), part(2), wpart(0), wpart(1), wpart(2), bpart(0), bpart(1), bpart(2),
                  pl.BlockSpec((2, 3, L, HY_CB), lambda j, b: (0, 0, 0, j)),
                  pl.BlockSpec((2, HY_CB), lambda j, b: (0, j))]
                 + [_const_spec(m) for m in mats],
        out_specs=pl.BlockSpec((L, HY_CB), lambda j, b: (b, j)),
        out_shape=jax.ShapeDtypeStruct((n_seq * L, D), BF16),
        scratch_shapes=[pltpu.VMEM((L + 2 * PAD, HY_CB), F32)],
        compiler_params=_cparams(("arbitrary", "arbitrary")),
        name=f"hyena_{L}",
    )(hy, hy, hy, conv_w, conv_w, conv_w, conv_b, conv_b, conv_b, spectra, hy_bias, *mats)


ROUTER_LANES = LANES
BIG_LANE = 1e9


def _first_max_lane(v, lanef):
    m = jnp.max(v, axis=-1, keepdims=True)
    return m, jnp.min(jnp.where(v == m, lanef, BIG_LANE), axis=-1, keepdims=True)


def _out_router_kernel(*refs, n_in):
    a_refs = refs[:n_in]
    w_ref, x_ref, mod_ref, gf_ref, wr_ref, br_ref, xo_ref, h2_ref, ids_ref, wts_ref = refs[n_in:]
    acc, k0 = None, 0
    for a_ref in a_refs:
        kk = a_ref.shape[1]
        part = jnp.dot(a_ref[...], w_ref[k0:k0 + kk, :], preferred_element_type=F32)
        acc = part if acc is None else acc + part
        k0 += kk
    xn = x_ref[...] + mod_ref[0, 2:3, :] * acc
    xo_ref[...] = xn
    h2 = _modulated(xn, gf_ref, mod_ref, 3)
    h2_ref[...] = h2

    logits = jnp.dot(h2, wr_ref[...], precision=HIGHEST, preferred_element_type=F32) + br_ref[...]
    lanef = lax.broadcasted_iota(jnp.int32, logits.shape, 1).astype(F32)
    gl = jnp.where(lanef < MOE_G, logits, NEG)
    gm, gi = _first_max_lane(gl, lanef)
    g_w = 1.0 / jnp.sum(jnp.exp(gl - gm), axis=-1, keepdims=True)
    lo = MOE_G + MOE_PG * gi
    el = jnp.where((lanef >= lo) & (lanef < lo + MOE_PG), logits, NEG)
    m1, e1 = _first_max_lane(el, lanef)
    m2, e2 = _first_max_lane(jnp.where(lanef == e1, NEG, el), lanef)
    p2 = jnp.exp(m2 - m1)
    w1 = g_w / (1.0 + p2)
    ids_ref[...] = jnp.where(lanef == 0, e1 - MOE_G, jnp.where(lanef == 1, e2 - MOE_G, 0.0)).astype(jnp.int32)
    wts_ref[...] = jnp.where(lanef == 0, w1, jnp.where(lanef == 1, w1 * p2, 0.0))


def out_proj_router(acts, w_bf, x, mod, gf, wr, br):
    tile = lambda w: pl.BlockSpec((TM, w), lambda i: (i, 0))
    full = lambda arr: pl.BlockSpec(arr.shape, lambda i: (0,) * arr.ndim)
    return pl.pallas_call(
        functools.partial(_out_router_kernel, n_in=len(acts)),
        grid=(N_TILES,),
        in_specs=[tile(a.shape[1]) for a in acts] + [full(w_bf), tile(D),
                  pl.BlockSpec((1, 6, D), lambda i: (_mod_row(i), 0, 0)), full(gf), full(wr), full(br)],
        out_specs=[tile(D), tile(D), tile(ROUTER_LANES), tile(ROUTER_LANES)],
        out_shape=[jax.ShapeDtypeStruct((T_ALL, D), F32), jax.ShapeDtypeStruct((T_ALL, D), F32),
                   jax.ShapeDtypeStruct((T_ALL, ROUTER_LANES), jnp.int32),
                   jax.ShapeDtypeStruct((T_ALL, ROUTER_LANES), F32)],
        compiler_params=_cparams(("arbitrary",)),
        name="out_router",
    )(*acts, w_bf, x, mod, gf, wr, br)


N_ASSIGN = 2 * T_ALL
MOE_TILES = N_ASSIGN // TM + MOE_E
N_SLOTS = MOE_TILES * TM


def route_tables(ids):
    flat = ids[:, :2].reshape(-1)
    onehot = (flat[:, None] == jnp.arange(MOE_E, dtype=jnp.int32)[None, :]).astype(jnp.int32)
    csum = jnp.cumsum(onehot, axis=0)
    rank = jnp.sum((csum - onehot) * onehot, axis=1)
    counts = csum[-1]
    padded = (counts + TM - 1) // TM * TM
    ends = jnp.cumsum(padded)
    dest = (ends - padded)[flat] + rank
    src = jnp.zeros((N_SLOTS,), jnp.int32).at[dest].set(jnp.arange(N_ASSIGN, dtype=jnp.int32) // 2)
    tile_expert = jnp.minimum(jnp.searchsorted(ends, jnp.arange(MOE_TILES, dtype=jnp.int32) * TM, side="right"),
                              MOE_E - 1).astype(jnp.int32)
    n_used = (ends[-1] // TM).astype(jnp.int32).reshape(1)
    return dest, src, tile_expert, n_used


def _row_copy(src_hbm, row, dst_ref, dst_row, sem):
    return pltpu.make_async_copy(src_hbm.at[pl.ds(row, 1), :], dst_ref.at[pl.ds(dst_row, 1), :], sem)


def _gather_kernel(nu_ref, src_ref, h_hbm, o_ref, sem):
    i = pl.program_id(0)

    @pl.when(i < nu_ref[0])
    def _():
        def start(r, c):
            _row_copy(h_hbm, src_ref[0, 0, r], o_ref, r, sem).start()
            return c

        def wait(r, c):
            _row_copy(h_hbm, 0, o_ref, r, sem).wait()
            return c

        lax.fori_loop(0, TM, start, 0)
        lax.fori_loop(0, TM, wait, 0)

    @pl.when(i >= nu_ref[0])
    def _():
        o_ref[...] = jnp.zeros(o_ref.shape, o_ref.dtype)


def gather_rows(h, src, n_used):
    return pl.pallas_call(
        _gather_kernel,
        grid_spec=pltpu.PrefetchScalarGridSpec(
            num_scalar_prefetch=1,
            grid=(MOE_TILES,),
            in_specs=[pl.BlockSpec((1, 1, TM), lambda i, nu: (i, 0, 0), memory_space=pltpu.SMEM),
                      pl.BlockSpec(memory_space=pl.ANY)],
            out_specs=pl.BlockSpec((TM, D), lambda i, nu: (i, 0)),
            scratch_shapes=[pltpu.SemaphoreType.DMA(())],
        ),
        out_shape=jax.ShapeDtypeStruct((N_SLOTS, D), F32),
        compiler_params=_cparams(("arbitrary",)),
        name="moe_gather",
    )(n_used, src.reshape(MOE_TILES, 1, TM), h)


def _experts_kernel(te_ref, nu_ref, x_ref, wg_ref, wu_ref, wd_ref, o_ref):
    i = pl.program_id(0)

    @pl.when(i < nu_ref[0])
    def _():
        x = x_ref[...].astype(BF16)
        g = jnp.dot(x, wg_ref[0, 0].astype(BF16), preferred_element_type=F32)
        u = jnp.dot(x, wu_ref[0, 0].astype(BF16), preferred_element_type=F32)
        hid = (_silu(g) * u).astype(BF16)
        o_ref[...] = jnp.dot(hid, wd_ref[0, 0].astype(BF16), preferred_element_type=F32)

    @pl.when(i >= nu_ref[0])
    def _():
        o_ref[...] = jnp.zeros(o_ref.shape, o_ref.dtype)


def grouped_experts(xs, w_gate, w_up, w_down, tile_expert, n_used, layer):
    wspec = lambda a, b: pl.BlockSpec((1, 1, a, b), lambda i, te, nu: (layer, te[i], 0, 0))
    return pl.pallas_call(
        _experts_kernel,
        grid_spec=pltpu.PrefetchScalarGridSpec(
            num_scalar_prefetch=2,
            grid=(MOE_TILES,),
            in_specs=[pl.BlockSpec((TM, D), lambda i, te, nu: (i, 0)),
                      wspec(D, MOE_F), wspec(D, MOE_F), wspec(MOE_F, D)],
            out_specs=pl.BlockSpec((TM, D), lambda i, te, nu: (i, 0)),
        ),
        out_shape=jax.ShapeDtypeStruct((N_SLOTS, D), F32),
        compiler_params=_cparams(("arbitrary",)),
        name="moe_experts",
    )(tile_expert, n_used, xs, w_gate, w_up, w_down)


def _combine_kernel(dest_ref, ys_hbm, x_ref, wts_ref, mod_ref, gfin_ref, o_ref, buf, sem, *, final):
    def start(r, c):
        for k in range(2):
            _row_copy(ys_hbm, dest_ref[0, 0, 2 * r + k], buf.at[k], r, sem).start()
        return c

    def wait(r, c):
        for k in range(2):
            _row_copy(ys_hbm, 0, buf.at[k], r, sem).wait()
        return c

    lax.fori_loop(0, TM, start, 0)
    lax.fori_loop(0, TM, wait, 0)
    moe = wts_ref[:, 0:1] * buf[0] + wts_ref[:, 1:2] * buf[1]
    xn = x_ref[...] + mod_ref[0, 5:6, :] * moe
    if final:
        xn = _rms(xn) * gfin_ref[...]
    o_ref[...] = xn


def moe_combine(ys, dest, x, wts, mod, gfin, *, final):
    tile = lambda w: pl.BlockSpec((TM, w), lambda i: (i, 0))
    return pl.pallas_call(
        functools.partial(_combine_kernel, final=final),
        grid=(N_TILES,),
        in_specs=[pl.BlockSpec((1, 1, 2 * TM), lambda i: (i, 0, 0), memory_space=pltpu.SMEM),
                  pl.BlockSpec(memory_space=pl.ANY), tile(D), tile(ROUTER_LANES),
                  pl.BlockSpec((1, 6, D), lambda i: (_mod_row(i), 0, 0)),
                  pl.BlockSpec((1, D), lambda i: (0, 0))],
        out_specs=tile(D),
        out_shape=jax.ShapeDtypeStruct((T_ALL, D), F32),
        scratch_shapes=[pltpu.VMEM((2, TM, D), F32), pltpu.SemaphoreType.DMA(())],
        compiler_params=_cparams(("arbitrary",)),
        name="moe_combine",
    )(dest.reshape(N_TILES, 1, 2 * TM), ys, x, wts, mod, gfin)


ODD_COLS = 2048
ROPE_Q = MLA_H * MLA_ROPE
ROPE_SHIFT = ROPE_F


def rope_tables():
    t = jnp.arange(L_LAT, dtype=jnp.int32)
    pos = jnp.stack([(t // GRID_W).astype(F32), (t % GRID_W).astype(F32)], axis=1)
    inv = 10000.0 ** (-jnp.arange(ROPE_F, dtype=F32) / ROPE_F)
    lane = jnp.arange(ROPE_Q, dtype=jnp.int32) % MLA_ROPE
    axis = lane // (2 * ROPE_F)
    first = (lane % (2 * ROPE_F)) < ROPE_F
    ang = pos[:, axis] * inv[lane % ROPE_F][None, :]
    cos, sin = jnp.cos(ang), jnp.sin(ang)
    tabs = [cos, jnp.where(first[None, :], -sin, 0.0), jnp.where(first[None, :], 0.0, sin)]
    ident = [jnp.ones((1, TM, ROPE_Q), F32), jnp.zeros((1, TM, ROPE_Q), F32), jnp.zeros((1, TM, ROPE_Q), F32)]
    return [jnp.concatenate([i, tb.reshape(LAT_TILES_PER_SEQ, TM, ROPE_Q)], axis=0) for i, tb in zip(ident, tabs)]


def _rope(x, c, a, b):
    n = x.shape[1]
    return x * c[:, :n] + pltpu.roll(x, n - ROPE_SHIFT, 1) * a[:, :n] + pltpu.roll(x, ROPE_SHIFT, 1) * b[:, :n]


def _odd_in_kernel(x_ref, mod_ref, g_ref, w_ref, gq_ref, wuq_ref, gkv_ref, wukv_ref, rc_ref, ra_ref, rb_ref,
                   qkv_ref, qm_ref, ckv_ref, kvu_ref, kr_ref):
    hb = _modulated(x_ref[...], g_ref, mod_ref, 0).astype(BF16)
    for c0 in range(0, 3 * NA_W, PROJ_CHUNK):
        qkv_ref[:, c0:c0 + PROJ_CHUNK] = jnp.dot(hb, w_ref[:, c0:c0 + PROJ_CHUNK], preferred_element_type=F32)
    rest = jnp.dot(hb, w_ref[:, 3 * NA_W:ODD_COLS], preferred_element_type=F32)
    rc, ra, rb = rc_ref[0], ra_ref[0], rb_ref[0]
    qd = (_rms(rest[:, 0:MLA_QR]) * gq_ref[...]).astype(BF16)
    qm = jnp.dot(qd, wuq_ref[...], preferred_element_type=F32)
    qm_ref[:, 0:MLA_H * MLA_NOPE] = qm[:, 0:MLA_H * MLA_NOPE]
    qm_ref[:, MLA_H * MLA_NOPE:] = _rope(qm[:, MLA_H * MLA_NOPE:], rc, ra, rb)
    ckv = _rms(rest[:, MLA_QR:MLA_QR + MLA_KVR]) * gkv_ref[...]
    ckv_ref[...] = ckv
    kvu_ref[...] = jnp.dot(ckv.astype(BF16), wukv_ref[...], preferred_element_type=F32)
    kr_ref[...] = _rope(rest[:, MLA_QR + MLA_KVR:], rc, ra, rb)


def odd_in_proj(x, mod, g, w_bf, gq, wuq_bf, gkv, wukv_bf, tabs):
    tile = lambda w: pl.BlockSpec((TM, w), lambda i: (i, 0))
    full = lambda arr: pl.BlockSpec(arr.shape, lambda i: (0,) * arr.ndim)
    tab = pl.BlockSpec((1, TM, ROPE_Q),
                       lambda i: (jnp.where(i < CTX_TILES, 0, 1 + (i - CTX_TILES) % LAT_TILES_PER_SEQ), 0, 0))
    widths = (3 * NA_W, MLA_H * MLA_QK, MLA_KVR, MLA_H * (MLA_NOPE + MLA_V), LANES)
    return pl.pallas_call(
        _odd_in_kernel,
        grid=(N_TILES,),
        in_specs=[tile(D), pl.BlockSpec((1, 6, D), lambda i: (_mod_row(i), 0, 0)), full(g), full(w_bf),
                  full(gq), full(wuq_bf), full(gkv), full(wukv_bf), tab, tab, tab],
        out_specs=[tile(w) for w in widths],
        out_shape=[jax.ShapeDtypeStruct((T_ALL, w), F32) for w in widths],
        compiler_params=_cparams(("arbitrary",)),
        name="odd_in",
    )(x, mod, g, w_bf, gq, wuq_bf, gkv, wukv_bf, *tabs)


NA_SCALE = NA_D ** -0.5
MLA_SCALE = MLA_QK ** -0.5
NT = (((1,), (1,)), ((), ()))


def _softmax_pv(scores, values):
    m = functools.reduce(jnp.maximum, [jnp.max(s, axis=-1, keepdims=True) for s in scores])
    ps = [jnp.exp(s - m) for s in scores]
    den = functools.reduce(jnp.add, [jnp.sum(p, axis=-1, keepdims=True) for p in ps])
    acc = functools.reduce(jnp.add, [jnp.dot(p.astype(BF16), v, preferred_element_type=F32) for p, v in zip(ps, values)])
    return acc / den


def _head(ref_or_val, h, width, base=0):
    return ref_or_val[:, base + h * width:base + (h + 1) * width]


def _mla_scores(qm, h, kn, kr):
    qn = _head(qm, h, MLA_NOPE).astype(BF16)
    qr = _head(qm, h, MLA_ROPE, MLA_H * MLA_NOPE).astype(BF16)
    return (lax.dot_general(qn, kn, NT, preferred_element_type=F32)
            + lax.dot_general(qr, kr, NT, preferred_element_type=F32)) * MLA_SCALE


def _attn_ctx_kernel(qkv_ref, qm_ref, kvu_ref, kr_ref, ona_ref, omla_ref):
    kr = kr_ref[:, 0:MLA_ROPE].astype(BF16)
    for h in range(NA_H):
        q = _head(qkv_ref, h, NA_D).astype(BF16)
        k = _head(qkv_ref, h, NA_D, NA_W).astype(BF16)
        v = _head(qkv_ref, h, NA_D, 2 * NA_W).astype(BF16)
        s = lax.dot_general(q, k, NT, preferred_element_type=F32) * NA_SCALE
        ona_ref[:, h * NA_D:(h + 1) * NA_D] = _softmax_pv([s], [v]).astype(ona_ref.dtype)
    for h in range(MLA_H):
        kn = _head(kvu_ref, h, MLA_NOPE).astype(BF16)
        v = _head(kvu_ref, h, MLA_V, MLA_H * MLA_NOPE).astype(BF16)
        s = _mla_scores(qm_ref, h, kn, kr)
        omla_ref[:, h * MLA_V:(h + 1) * MLA_V] = _softmax_pv([s], [v]).astype(omla_ref.dtype)


def attn_context(qkv, qm, kvu, kr):
    seq = lambda w: pl.BlockSpec((L_CTX, w), lambda b: (b, 0))
    return pl.pallas_call(
        _attn_ctx_kernel,
        grid=(N_CTX,),
        in_specs=[seq(3 * NA_W), seq(MLA_H * MLA_QK), seq(MLA_H * (MLA_NOPE + MLA_V)), seq(LANES)],
        out_specs=[seq(NA_W), seq(MLA_H * MLA_V)],
        out_shape=[jax.ShapeDtypeStruct((T_CTX, NA_W), BF16), jax.ShapeDtypeStruct((T_CTX, MLA_H * MLA_V), BF16)],
        compiler_params=_cparams(("arbitrary",)),
        name="attn_ctx",
    )(qkv, qm, kvu, kr)


def neighbourhood_bias(rel_bias):
    rows = L_LAT // GRID_W
    r = jnp.arange(rows)
    r0 = jnp.clip(r - NA_WIN_R // 2, 0, rows - NA_WIN_R)
    c = jnp.arange(GRID_W)
    c0 = jnp.clip(c - NA_WIN_C // 2, 0, GRID_W - NA_WIN_C)
    row_ok = (r[None, :] >= r0[:, None]) & (r[None, :] < r0[:, None] + NA_WIN_R)
    col_ok = (c[None, :] >= c0[:, None]) & (c[None, :] < c0[:, None] + NA_WIN_C)
    dr = jnp.clip(r[None, :] - r[:, None] + NA_WIN_R - 1, 0, 2 * NA_WIN_R - 2)
    dc = jnp.clip(c[None, :] - c[:, None], -(NA_WIN_C - 1), NA_WIN_C - 1) + NA_WIN_C - 1
    b = rel_bias.astype(F32)[:, dr[:, None, :, None], dc[None, :, None, :]]
    ok = row_ok[:, None, :, None] & col_ok[None, :, None, :]
    return jnp.where(ok[None], b, NEG).reshape(NA_H, L_LAT, L_LAT)


def _na_lat_kernel(q_ref, k_ref, v_ref, kc_ref, vc_ref, b_ref, o_ref):
    for h in range(NA_H):
        q = _head(q_ref, h, NA_D).astype(BF16)
        k = _head(k_ref, h, NA_D).astype(BF16)
        v = _head(v_ref, h, NA_D).astype(BF16)
        kc = kc_ref[0, 0, h].astype(BF16)
        vc = vc_ref[0, 0, h].astype(BF16)
        s1 = lax.dot_general(q, k, NT, preferred_element_type=F32) * NA_SCALE + b_ref[h]
        s2 = lax.dot_general(q, kc, NT, preferred_element_type=F32) * NA_SCALE
        o_ref[:, h * NA_D:(h + 1) * NA_D] = _softmax_pv([s1, s2], [v, vc]).astype(o_ref.dtype)


def attn_neighbourhood_latent(qkv, cache_k, cache_v, bias):
    nq = L_LAT // TM
    t0 = T_CTX // TM
    s0 = T_CTX // L_LAT
    cache = pl.BlockSpec((1, 1, NA_H, PAST, NA_D), lambda qt, b: (b, 0, 0, 0, 0))
    return pl.pallas_call(
        _na_lat_kernel,
        grid=(nq, N_LAT),
        in_specs=[pl.BlockSpec((TM, NA_W), lambda qt, b: (t0 + b * nq + qt, 0)),
                  pl.BlockSpec((L_LAT, NA_W), lambda qt, b: (s0 + b, 1)),
                  pl.BlockSpec((L_LAT, NA_W), lambda qt, b: (s0 + b, 2)),
                  cache, cache,
                  pl.BlockSpec((NA_H, TM, L_LAT), lambda qt, b: (0, qt, 0))],
        out_specs=pl.BlockSpec((TM, NA_W), lambda qt, b: (b * nq + qt, 0)),
        out_shape=jax.ShapeDtypeStruct((T_LAT, NA_W), BF16),
        compiler_params=_cparams(("arbitrary", "arbitrary")),
        name="attn_na_lat",
    )(qkv, qkv, qkv, cache_k, cache_v, bias)


def _mla_lat_kernel(qm_ref, kvu_ref, kr_ref, ckv_ref, krc_ref, wukv_ref, o_ref):
    kvc = jnp.dot(ckv_ref[0, 0].astype(BF16), wukv_ref[...], preferred_element_type=F32)
    kr = kr_ref[:, 0:MLA_ROPE].astype(BF16)
    krc = krc_ref[0, 0].astype(BF16)
    for h in range(MLA_H):
        kn = _head(kvu_ref, h, MLA_NOPE).astype(BF16)
        v = _head(kvu_ref, h, MLA_V, MLA_H * MLA_NOPE).astype(BF16)
        knc = _head(kvc, h, MLA_NOPE).astype(BF16)
        vc = _head(kvc, h, MLA_V, MLA_H * MLA_NOPE).astype(BF16)
        s1 = _mla_scores(qm_ref, h, kn, kr)
        s2 = _mla_scores(qm_ref, h, knc, krc)
        o_ref[:, h * MLA_V:(h + 1) * MLA_V] = _softmax_pv([s1, s2], [v, vc]).astype(o_ref.dtype)


def attn_mla_latent(qm, kvu, kr, cache_ckv, cache_krope, wukv_bf):
    nq = L_LAT // TM
    t0 = T_CTX // TM
    s0 = T_CTX // L_LAT
    return pl.pallas_call(
        _mla_lat_kernel,
        grid=(nq, N_LAT),
        in_specs=[pl.BlockSpec((TM, MLA_H * MLA_QK), lambda qt, b: (t0 + b * nq + qt, 0)),
                  pl.BlockSpec((L_LAT, MLA_H * (MLA_NOPE + MLA_V)), lambda qt, b: (s0 + b, 0)),
                  pl.BlockSpec((L_LAT, LANES), lambda qt, b: (s0 + b, 0)),
                  pl.BlockSpec((1, 1, PAST, MLA_KVR), lambda qt, b: (b, 0, 0, 0)),
                  pl.BlockSpec((1, 1, PAST, MLA_ROPE), lambda qt, b: (b, 0, 0, 0)),
                  pl.BlockSpec(wukv_bf.shape, lambda qt, b: (0, 0))],
        out_specs=pl.BlockSpec((TM, MLA_H * MLA_V), lambda qt, b: (b * nq + qt, 0)),
        out_shape=jax.ShapeDtypeStruct((T_LAT, MLA_H * MLA_V), BF16),
        compiler_params=_cparams(("arbitrary", "arbitrary")),
        name="attn_mla_lat",
    )(qm, kvu, kr, cache_ckv, cache_krope, wukv_bf)


def moe_block(h2, ids, wts, x, mod, gfin, w_gate, w_up, w_down, layer, *, final):
    dest, src, tile_expert, n_used = route_tables(ids)
    xs = gather_rows(h2, src, n_used)
    ys = grouped_experts(xs, w_gate, w_up, w_down, tile_expert, n_used, layer)
    return moe_combine(ys, dest, x, wts, mod, gfin, final=final)


def _pad_lanes(a):
    return jnp.pad(a, ((0, 0), (0, LANES - a.shape[1])))


def _hyena_features(L):
    t = jnp.linspace(0.0, 1.0, L, dtype=F32)[:, None]
    w = 2.0 * math.pi * jnp.arange(L, dtype=F32) / L
    bands = jnp.linspace(1e-4, HY_BANDS - 1, HY_BANDS, dtype=F32)
    ang = w[:, None] * bands[None]
    return _pad_lanes(jnp.concatenate([t, jnp.cos(ang), -jnp.sin(ang)], axis=-1))


def _router_params(w_gr, b_gr, w_er, b_er):
    wr = _pad_lanes(jnp.concatenate([w_gr, w_er], axis=1))
    br = _pad_lanes(jnp.concatenate([b_gr, b_er])[None])
    return wr, br


def _even_layer(x, mod, g_mix, state, w_in, conv_w, conv_b, a_log, dt_bias, d_skip, g_ssd, hy_conv_w, hy_conv_b,
                hy_w1, hy_b1, hy_w2, hy_b2, hy_w3, hy_freq, hy_bias):
    n0 = D + SSD_XBC
    w_bf = jnp.concatenate([w_in[:, :n0], w_in[:, n0 + SSD_H:], w_in[:, n0:n0 + SSD_H],
                            jnp.zeros((D, LANES - SSD_H), F32)], axis=1).astype(BF16)
    z, xbc, hy, dtr = even_in_proj(x, mod, g_mix, w_bf)
    small = (conv_w, conv_b[None], _pad_lanes(dt_bias), _pad_lanes(a_log), jnp.repeat(d_skip, SSD_P)[None], g_ssd[None])
    y_c, fin = ssd_mixer(xbc, dtr, z, None, *small, L=L_CTX, n_seq=N_CTX, row_off=0)
    (y_l,) = ssd_mixer(xbc, dtr, z, state.reshape(N_LAT, 2, SSD_H * SSD_P, SSD_N), *small,
                       L=L_LAT, n_seq=N_LAT, row_off=T_CTX)
    w1 = jnp.pad(hy_w1, ((0, LANES - HY_FEAT), (0, 0)))
    w3r = hy_w3.reshape(HY_HID, 4, D).transpose(1, 0, 2)
    deltas = jnp.linspace(HY_MIN_DECAY, HY_MAX_DECAY, D, dtype=F32)[None]
    us = []
    for L, n_seq, off in ((L_CTX, N_CTX, 0), (L_LAT, N_LAT, T_CTX)):
        mats = dft_matrices(L)
        spectra = hyena_filter_spectra(_hyena_features(L), w1, hy_b1[None], hy_w2, hy_b2[None], hy_freq, w3r, deltas,
                                       mats[0], mats[1], L=L)
        us.append(hyena_mixer(hy, hy_conv_w, hy_conv_b[None], spectra, hy_bias, mats, L=L, n_seq=n_seq, row_off=off))
    return jnp.concatenate([y_c, y_l], axis=0), jnp.concatenate(us, axis=0), fin


def _odd_layer(x, mod, g_mix, cache_k, cache_v, cache_ckv, cache_kr, rel_bias, w_in, g_q, w_uq, g_kv, w_ukv):
    w_bf = jnp.pad(w_in, ((0, 0), (0, ODD_COLS - w_in.shape[1]))).astype(BF16)
    wuq = w_uq.reshape(MLA_QR, MLA_H, MLA_QK)
    wuq_bf = jnp.concatenate([wuq[:, :, :MLA_NOPE].reshape(MLA_QR, -1), wuq[:, :, MLA_NOPE:].reshape(MLA_QR, -1)],
                             axis=1).astype(BF16)
    wukv = w_ukv.reshape(MLA_KVR, MLA_H, MLA_NOPE + MLA_V)
    wukv_bf = jnp.concatenate([wukv[:, :, :MLA_NOPE].reshape(MLA_KVR, -1), wukv[:, :, MLA_NOPE:].reshape(MLA_KVR, -1)],
                              axis=1).astype(BF16)
    qkv, qm, ckv, kvu, kr = odd_in_proj(x, mod, g_mix, w_bf, g_q[None], wuq_bf, g_kv[None], wukv_bf, rope_tables())
    ona_c, omla_c = attn_context(qkv, qm, kvu, kr)
    ona_l = attn_neighbourhood_latent(qkv, cache_k, cache_v, neighbourhood_bias(rel_bias))
    omla_l = attn_mla_latent(qm, kvu, kr, cache_ckv, cache_kr, wukv_bf)
    return jnp.concatenate([ona_c, ona_l], axis=0), jnp.concatenate([omla_c, omla_l], axis=0), qkv, ckv, kr


def kernel(x_prompt, x_sample, state_ssd, cache_na_k, cache_na_v, cache_mla_ckv, cache_mla_krope, c, c_ctx, w_ada, b_ada, norm_mix, norm_ffn, norm_final, ev_w_in, ev_conv_w, ev_conv_b, ssd_A_log, ssd_dt_bias, ssd_d, ssd_norm, hy_conv_w, hy_conv_b, hy_w1, hy_b1, hy_w2, hy_b2, hy_w3, hy_freq, hy_bias, ev_w_out, od_w_in, mla_q_norm, mla_w_uq, mla_kv_norm, mla_w_ukv, na_rel_bias, od_w_out, moe_w_gr, moe_b_gr, moe_w_er, moe_b_er, moe_w_gate, moe_w_up, moe_w_down):
    x = jnp.concatenate([x_prompt.reshape(T_CTX, D), x_sample.reshape(T_LAT, D)], axis=0)
    cvec = jnp.zeros((MOD_ROWS, D), F32).at[0].set(c_ctx).at[1:1 + N_LAT].set(c)
    mod = ada_modulation(cvec, w_ada, b_ada)
    gfin = norm_final[None]

    y, u, fin = _even_layer(x, mod[0], norm_mix[0][None], state_ssd[:, 0], ev_w_in[0], ev_conv_w[0], ev_conv_b[0],
                            ssd_A_log[0], ssd_dt_bias[0], ssd_d[0], ssd_norm[0], hy_conv_w[0], hy_conv_b[0],
                            hy_w1[0], hy_b1[0], hy_w2[0], hy_b2[0], hy_w3[0], hy_freq[0], hy_bias[0])
    wr, br = _router_params(moe_w_gr[0], moe_b_gr[0], moe_w_er[0], moe_b_er[0])
    xn, h2, ids, wts = out_proj_router([y, u], ev_w_out[0].astype(BF16), x, mod[0], norm_ffn[0][None], wr, br)
    x = moe_block(h2, ids, wts, xn, mod[0], gfin, moe_w_gate, moe_w_up, moe_w_down, 0, final=False)

    o_na, o_mla, qkv, ckv, kr = _odd_layer(x, mod[1], norm_mix[1][None], cache_na_k, cache_na_v, cache_mla_ckv,
                                           cache_mla_krope, na_rel_bias[0], od_w_in[0], mla_q_norm[0], mla_w_uq[0],
                                           mla_kv_norm[0], mla_w_ukv[0])
    wr, br = _router_params(moe_w_gr[1], moe_b_gr[1], moe_w_er[1], moe_b_er[1])
    xn, h2, ids, wts = out_proj_router([o_na, o_mla], od_w_out[0].astype(BF16), x, mod[1], norm_ffn[1][None], wr, br)
    out = moe_block(h2, ids, wts, xn, mod[1], gfin, moe_w_gate, moe_w_up, moe_w_down, 1, final=True)

    heads = lambda a: a.reshape(N_CTX, L_CTX, NA_H, NA_D).transpose(0, 2, 1, 3)[:, None]
    return (out[:T_CTX].reshape(N_CTX, L_CTX, D),
            out[T_CTX:].reshape(N_LAT, L_LAT, D),
            fin.reshape(N_CTX, 1, 2, SSD_H, SSD_P, SSD_N),
            heads(qkv[:T_CTX, NA_W:2 * NA_W]),
            heads(qkv[:T_CTX, 2 * NA_W:3 * NA_W]),
            ckv[:T_CTX].reshape(N_CTX, 1, L_CTX, MLA_KVR),
            kr[:T_CTX, :MLA_ROPE].reshape(N_CTX, 1, L_CTX, MLA_ROPE))
```

```python
import functools
import math

import numpy as np
import jax
import jax.numpy as jnp
from jax import lax
from jax.experimental import pallas as pl
from jax.experimental.pallas import tpu as pltpu

F32 = jnp.float32
BF16 = jnp.bfloat16
HIGHEST = lax.Precision.HIGHEST

D = 1024
N_CTX, L_CTX = 16, 256
N_LAT, L_LAT = 8, 1024
T_CTX = N_CTX * L_CTX
T_LAT = N_LAT * L_LAT
T_ALL = T_CTX + T_LAT
PAST = 512
GRID_W = 64
EPS = 1e-6
NEG = -1e30

SSD_H, SSD_P, SSD_N, SSD_G = 16, 64, 128, 2
SSD_XBC = D + 2 * SSD_G * SSD_N
SSD_K = 5
CHUNK = 128

HY_K = 3
HY_BANDS = 16
HY_FEAT = 1 + 2 * HY_BANDS
HY_HID = 64
HY_MIN_DECAY = abs(math.log(1e-2) / 1.5)
HY_MAX_DECAY = abs(math.log(1e-2) / 0.3)

NA_H, NA_D = 8, 64
NA_W = NA_H * NA_D
NA_WIN_R, NA_WIN_C = 8, 16
MLA_H, MLA_QR, MLA_KVR = 8, 256, 128
MLA_NOPE, MLA_ROPE, MLA_V = 64, 32, 64
MLA_QK = MLA_NOPE + MLA_ROPE
ROPE_F = MLA_ROPE // 4

MOE_G, MOE_PG, MOE_E, MOE_F = 4, 8, 32, 256

LANES = 128
SUBLANES = 8
VMEM_LIMIT = 56 * 1024 * 1024

TM = 256
N_TILES = T_ALL // TM
CTX_TILES = T_CTX // TM
LAT_TILES_PER_SEQ = L_LAT // TM
MOD_ROWS = 16


def _cparams(sem):
    return pltpu.CompilerParams(dimension_semantics=sem, vmem_limit_bytes=VMEM_LIMIT)


def _mod_row(i):
    return jnp.where(i < CTX_TILES, 0, 1 + (i - CTX_TILES) // LAT_TILES_PER_SEQ)


def _silu(x):
    return x * jax.nn.sigmoid(x)


def _rms(x):
    return x * lax.rsqrt(jnp.mean(x * x, axis=-1, keepdims=True) + EPS)


def _ada_kernel(c_ref, w_ref, b_ref, o_ref):
    c = c_ref[...]
    o_ref[0] = jnp.dot(_silu(c), w_ref[0], precision=HIGHEST, preferred_element_type=F32) + b_ref[0]


def ada_modulation(cvec, w_ada, b_ada):
    depth = w_ada.shape[0]
    out = pl.pallas_call(
        _ada_kernel,
        grid=(depth, 6),
        in_specs=[
            pl.BlockSpec((MOD_ROWS, D), lambda l, j: (0, 0)),
            pl.BlockSpec((1, D, D), lambda l, j: (l, 0, j)),
            pl.BlockSpec((1, 1, D), lambda l, j: (l, 0, j)),
        ],
        out_specs=pl.BlockSpec((1, MOD_ROWS, D), lambda l, j: (l, 0, j)),
        out_shape=jax.ShapeDtypeStruct((depth, MOD_ROWS, 6 * D), F32),
        compiler_params=_cparams(("arbitrary", "arbitrary")),
        name="ada",
    )(cvec, w_ada, b_ada.reshape(depth, 1, 6 * D))
    return out.reshape(depth, MOD_ROWS, 6, D)


PROJ_CHUNK = 512


def _modulated(x, g_ref, mod_ref, shift_row):
    h = _rms(x) * g_ref[...]
    return h * (1.0 + mod_ref[0, shift_row + 1:shift_row + 2, :]) + mod_ref[0, shift_row:shift_row + 1, :]


def _even_in_kernel(x_ref, mod_ref, g_ref, w_ref, z_ref, xbc_ref, hy_ref, dt_ref):
    hb = _modulated(x_ref[...], g_ref, mod_ref, 0).astype(BF16)
    col = 0
    for o_ref in (z_ref, xbc_ref, hy_ref, dt_ref):
        width = o_ref.shape[1]
        for c0 in range(0, width, PROJ_CHUNK):
            c1 = min(c0 + PROJ_CHUNK, width)
            o_ref[:, c0:c1] = jnp.dot(hb, w_ref[:, col + c0:col + c1], preferred_element_type=F32)
        col += width


def even_in_proj(x, mod, g, w_bf):
    widths = (D, SSD_XBC, 3 * D, LANES)
    return pl.pallas_call(
        _even_in_kernel,
        grid=(N_TILES,),
        in_specs=[
            pl.BlockSpec((TM, D), lambda i: (i, 0)),
            pl.BlockSpec((1, 6, D), lambda i: (_mod_row(i), 0, 0)),
            pl.BlockSpec((1, D), lambda i: (0, 0)),
            pl.BlockSpec(w_bf.shape, lambda i: (0, 0)),
        ],
        out_specs=[pl.BlockSpec((TM, w), lambda i: (i, 0)) for w in widths],
        out_shape=[jax.ShapeDtypeStruct((T_ALL, w), F32) for w in widths],
        compiler_params=_cparams(("arbitrary",)),
        name="even_in",
    )(x, mod, g, w_bf)


PAD = SUBLANES


def _ssd_kernel(*refs, L, has_init):
    if has_init:
        (xbc_ref, dt_ref, z_ref, init_ref, cw_ref, cb_ref, dtb_ref, alog_ref, dsk_ref, gs_ref,
         y_ref, xp_s, xc_s, ya_s, st_s) = refs
        fin_ref = None
    else:
        (xbc_ref, dt_ref, z_ref, cw_ref, cb_ref, dtb_ref, alog_ref, dsk_ref, gs_ref,
         y_ref, fin_ref, xp_s, xc_s, ya_s, st_s) = refs
        init_ref = None
    nc = L // CHUNK
    half = SSD_K // 2

    xp_s[0:PAD, :] = jnp.zeros((PAD, SSD_XBC), F32)
    xp_s[PAD + L:2 * PAD + L, :] = jnp.zeros((PAD, SSD_XBC), F32)
    xp_s[PAD:PAD + L, :] = xbc_ref[...]
    for c in range(nc):
        base = PAD + c * CHUNK - half
        acc = cb_ref[...] + xp_s[base:base + CHUNK, :] * cw_ref[0:1, :]
        for k in range(1, SSD_K):
            acc = acc + xp_s[base + k:base + k + CHUNK, :] * cw_ref[k:k + 1, :]
        xc_s[c * CHUNK:(c + 1) * CHUNK, :] = _silu(acc)

    row = lax.broadcasted_iota(jnp.int32, (CHUNK, CHUNK), 0)
    colm = lax.broadcasted_iota(jnp.int32, (CHUNK, CHUNK), 1)
    lane_lo = colm < SSD_P
    tri_lo = (colm <= row).astype(F32)
    tri_up = (colm >= row).astype(F32)

    for d in range(2):
        causal = (colm <= row) if d == 0 else (colm >= row)
        for j in range(SSD_H * SSD_P // CHUNK):
            if has_init:
                st_s[:, j * CHUNK:(j + 1) * CHUNK] = init_ref[0, d, j * CHUNK:(j + 1) * CHUNK, :].T
            else:
                st_s[:, j * CHUNK:(j + 1) * CHUNK] = jnp.zeros((CHUNK, CHUNK), F32)

        def chunk_body(ci, carry, d=d, causal=causal):
            c = ci if d == 0 else nc - 1 - ci
            r0 = pl.multiple_of(c * CHUNK, CHUNK)
            dt = jax.nn.softplus(dt_ref[pl.ds(r0, CHUNK), :] + dtb_ref[d:d + 1, :])
            a = dt * (-jnp.exp(alog_ref[d:d + 1, :]))
            tri = tri_lo if d == 0 else tri_up
            cs = jnp.dot(tri, a, precision=HIGHEST, preferred_element_type=F32)
            cs_t = jnp.dot(a.T, tri.T, precision=HIGHEST, preferred_element_type=F32)
            edge = cs[CHUNK - 1:CHUNK, :] if d == 0 else cs[0:1, :]
            ecs = jnp.exp(cs)
            dec = jnp.exp(edge - cs)
            cdec = jnp.exp(edge)
            for g in range(SSD_G):
                bm = xc_s[pl.ds(r0, CHUNK), D + g * SSD_N:D + (g + 1) * SSD_N]
                cm = xc_s[pl.ds(r0, CHUNK), D + (SSD_G + g) * SSD_N:D + (SSD_G + g + 1) * SSD_N]
                bm_b, cm_b = bm.astype(BF16), cm.astype(BF16)
                cb = lax.dot_general(cm_b, bm_b, (((1,), (1,)), ((), ())), preferred_element_type=F32)
                bm_t = bm.T.astype(BF16)
                pairs = SSD_H // SSD_G // 2
                for pp in range(pairs):
                    p = g * pairs + pp
                    h0, h1 = 2 * p, 2 * p + 1
                    cols = slice(p * CHUNK, (p + 1) * CHUNK)
                    xs = xc_s[pl.ds(r0, CHUNK), cols]
                    xdt = xs * jnp.where(lane_lo, dt[:, h0:h0 + 1], dt[:, h1:h1 + 1])
                    ms = []
                    for h in (h0, h1):
                        diff = cs[:, h:h + 1] - cs_t[h:h + 1, :]
                        ms.append(cb * jnp.exp(jnp.where(causal, diff, NEG)))
                    mcat = jnp.concatenate(ms, axis=1).astype(BF16)
                    xbd = jnp.concatenate([jnp.where(lane_lo, xdt, 0.0), jnp.where(lane_lo, 0.0, xdt)],
                                          axis=0).astype(BF16)
                    y_diag = jnp.dot(mcat, xbd, preferred_element_type=F32)
                    st = st_s[:, cols]
                    y_off = jnp.dot(cm_b, st.astype(BF16), preferred_element_type=F32)
                    y_off = y_off * jnp.where(lane_lo, ecs[:, h0:h0 + 1], ecs[:, h1:h1 + 1])
                    y = y_diag + y_off
                    if d == 0:
                        ya_s[pl.ds(r0, CHUNK), cols] = y
                    else:
                        ya_s[pl.ds(r0, CHUNK), cols] = ya_s[pl.ds(r0, CHUNK), cols] + y
                    xdd = (xdt * jnp.where(lane_lo, dec[:, h0:h0 + 1], dec[:, h1:h1 + 1])).astype(BF16)
                    snew = jnp.dot(bm_t, xdd, preferred_element_type=F32)
                    st_s[:, cols] = st * jnp.where(lane_lo[0:1, :], cdec[:, h0:h0 + 1], cdec[:, h1:h1 + 1]) + snew
            return carry

        lax.fori_loop(0, nc, chunk_body, 0)
        if fin_ref is not None:
            for j in range(SSD_H * SSD_P // CHUNK):
                fin_ref[0, d, j * CHUNK:(j + 1) * CHUNK, :] = st_s[:, j * CHUNK:(j + 1) * CHUNK].T

    def out_body(c, carry):
        r0 = pl.multiple_of(c * CHUNK, CHUNK)
        y = ya_s[pl.ds(r0, CHUNK), :] + xc_s[pl.ds(r0, CHUNK), 0:D] * dsk_ref[...]
        y = y * _silu(z_ref[pl.ds(r0, CHUNK), :])
        y_ref[pl.ds(r0, CHUNK), :] = (_rms(y) * gs_ref[...]).astype(y_ref.dtype)
        return carry

    lax.fori_loop(0, nc, out_body, 0)


def ssd_mixer(xbc, dtr, z, init, cw, cb, dtb, alog, dsk, gs, *, L, n_seq, row_off):
    blk0 = row_off // L
    has_init = init is not None
    seq = lambda w: pl.BlockSpec((L, w), lambda b: (blk0 + b, 0))
    full = lambda arr: pl.BlockSpec(arr.shape, lambda b: (0,) * arr.ndim)
    in_specs = [seq(SSD_XBC), seq(LANES), seq(D)]
    args = [xbc, dtr, z]
    if has_init:
        in_specs.append(pl.BlockSpec((1, 2, SSD_H * SSD_P, SSD_N), lambda b: (b, 0, 0, 0)))
        args.append(init)
    small = [cw, cb, dtb, alog, dsk, gs]
    in_specs += [full(a) for a in small]
    args += small
    out_specs = [pl.BlockSpec((L, D), lambda b: (b, 0))]
    out_shape = [jax.ShapeDtypeStruct((n_seq * L, D), BF16)]
    if not has_init:
        out_specs.append(pl.BlockSpec((1, 2, SSD_H * SSD_P, SSD_N), lambda b: (b, 0, 0, 0)))
        out_shape.append(jax.ShapeDtypeStruct((n_seq, 2, SSD_H * SSD_P, SSD_N), F32))
    return pl.pallas_call(
        functools.partial(_ssd_kernel, L=L, has_init=has_init),
        grid=(n_seq,),
        in_specs=in_specs,
        out_specs=out_specs,
        out_shape=out_shape,
        scratch_shapes=[
            pltpu.VMEM((L + 2 * PAD, SSD_XBC), F32),
            pltpu.VMEM((L, SSD_XBC), F32),
            pltpu.VMEM((L, D), F32),
            pltpu.VMEM((SSD_N, SSD_H * SSD_P), F32),
        ],
        compiler_params=_cparams(("arbitrary",)),
        name=f"ssd_{L}",
    )(*args)


HY_CB = 256


def dft_matrices(L):
    k = np.arange(L, dtype=np.int64)
    ang = ((k[:, None] * k[None, :]) % (2 * L)).astype(np.float64) * (math.pi / L)
    cosm = np.cos(ang)
    sinm = np.sin(ang)
    sinm[0] = np.where(k % 2 == 0, 1.0, -1.0)
    fwd = np.concatenate([cosm, sinm], axis=0).astype(np.float32)
    wts = np.where(k == 0, 1.0, 2.0) / (2 * L)
    inv = np.concatenate([cosm * wts[None, :], sinm.T * wts[None, :]], axis=1).astype(np.float32)
    fwd, inv = jnp.asarray(fwd), jnp.asarray(inv)
    f_hi = fwd.astype(BF16)
    f_lo = (fwd - f_hi.astype(F32)).astype(BF16)
    return f_hi, f_lo, inv.astype(BF16)


def _dot3(a_hi, a_lo, b):
    b_hi = b.astype(BF16)
    b_lo = (b - b_hi.astype(F32)).astype(BF16)
    return (jnp.dot(a_hi, b_hi, preferred_element_type=F32) + jnp.dot(a_lo, b_hi, preferred_element_type=F32)
            + jnp.dot(a_hi, b_lo, preferred_element_type=F32))


def _const_spec(arr):
    return pl.BlockSpec(arr.shape, lambda *_: (0,) * arr.ndim, pipeline_mode=pl.Buffered(1))


def _hy_filter_kernel(feat_ref, w1_ref, b1_ref, w2_ref, b2_ref, fr_ref, w3_ref, dl_ref, fh_ref, fl_ref, o_ref, *, L):
    hp = functools.partial(jnp.dot, precision=HIGHEST, preferred_element_type=F32)
    hdn = jnp.sin(fr_ref[0:1, :] * (hp(feat_ref[...], w1_ref[...]) + b1_ref[...]))
    hdn = jnp.sin(fr_ref[1:2, :] * (hp(hdn, w2_ref[...]) + b2_ref[...]))
    rowi = lax.broadcasted_iota(jnp.int32, (L, 1), 0)
    t = rowi.astype(F32) * (1.0 / (L - 1))
    dec = jnp.exp(-t * dl_ref[...])
    first = rowi == 0
    for o in range(2):
        fwd = hp(hdn, w3_ref[2 * o]) * dec
        bwd = jnp.where(first, 0.0, hp(hdn, w3_ref[2 * o + 1]) * dec)
        ss = _dot3(fh_ref[...], fl_ref[...], fwd + bwd)
        sd = _dot3(fh_ref[...], fl_ref[...], fwd - bwd)
        hr = ss[0:L]
        o_ref[o, 0] = hr
        o_ref[o, 1] = jnp.where(first, 0.0, sd[L:2 * L])
        o_ref[o, 2] = jnp.where(first, ss[L:L + 1], hr)


def hyena_filter_spectra(feat, w1, b1, w2, b2, freq, w3r, deltas, f_hi, f_lo, *, L):
    full = lambda arr: pl.BlockSpec(arr.shape, lambda j: (0,) * arr.ndim)
    return pl.pallas_call(
        functools.partial(_hy_filter_kernel, L=L),
        grid=(D // HY_CB,),
        in_specs=[full(feat), full(w1), full(b1), full(w2), full(b2), full(freq),
                  pl.BlockSpec((4, HY_HID, HY_CB), lambda j: (0, 0, j)),
                  pl.BlockSpec((1, HY_CB), lambda j: (0, j)),
                  _const_spec(f_hi), _const_spec(f_lo)],
        out_specs=pl.BlockSpec((2, 3, L, HY_CB), lambda j: (0, 0, 0, j)),
        out_shape=jax.ShapeDtypeStruct((2, 3, L, D), F32),
        compiler_params=_cparams(("arbitrary",)),
        name=f"hy_filter_{L}",
    )(feat, w1, b1, w2, b2, freq, w3r, deltas, f_hi, f_lo)


def _hyena_kernel(p0_ref, p1_ref, p2_ref, w0_ref, w1_ref, w2_ref, b0_ref, b1_ref, b2_ref, h_ref, hb_ref,
                  f_ref, g_ref, o_ref, xp_s, *, L):
    xp_s[0:PAD, :] = jnp.zeros((PAD, HY_CB), F32)
    xp_s[PAD + L:2 * PAD + L, :] = jnp.zeros((PAD, HY_CB), F32)

    def conv(p_ref, w_ref, b_ref):
        xp_s[PAD:PAD + L, :] = p_ref[...]
        acc = b_ref[...] + xp_s[PAD - 1:PAD - 1 + L, :] * w_ref[0:1, :]
        for k in range(1, HY_K):
            acc = acc + xp_s[PAD - 1 + k:PAD - 1 + k + L, :] * w_ref[k:k + 1, :]
        return acc

    u = conv(p0_ref, w0_ref, b0_ref)
    for o, (p_ref, w_ref, b_ref) in enumerate(((p1_ref, w1_ref, b1_ref), (p2_ref, w2_ref, b2_ref))):
        spec = jnp.dot(f_ref[...], u.astype(BF16), preferred_element_type=F32)
        ar, ai = spec[0:L], spec[L:2 * L]
        yr = (ar * h_ref[o, 0] - ai * h_ref[o, 1]).astype(BF16)
        yn = (ar * h_ref[o, 1] + ai * h_ref[o, 2]).astype(BF16)
        y = jnp.dot(g_ref[...], jnp.concatenate([yr, yn], axis=0), preferred_element_type=F32)
        u = conv(p_ref, w_ref, b_ref) * (y + u * hb_ref[o:o + 1, :])
    o_ref[...] = u.astype(o_ref.dtype)


def hyena_mixer(hy, conv_w, conv_b, spectra, hy_bias, mats, *, L, n_seq, row_off):
    blk0 = row_off // L
    nj = D // HY_CB
    part = lambda q: pl.BlockSpec((L, HY_CB), lambda j, b: (blk0 + b, q * nj + j))
    wpart = lambda q: pl.BlockSpec((HY_K, HY_CB), lambda j, b: (0, q * nj + j))
    bpart = lambda q: pl.BlockSpec((1, HY_CB), lambda j, b: (0, q * nj + j))
    return pl.pallas_call(
        functools.partial(_hyena_kernel, L=L),
        grid=(nj, n_seq),
        in_specs=[part(0), part(1), part(2), wpart(0), wpart(1), wpart(2), bpart(0), bpart(1), bpart(2),
                  pl.BlockSpec((2, 3, L, HY_CB), lambda j, b: (0, 0, 0, j)),
                  pl.BlockSpec((2, HY_CB), lambda j, b: (0, j))]
                 + [_const_spec(m) for m in mats],
        out_specs=pl.BlockSpec((L, HY_CB), lambda j, b: (b, j)),
        out_shape=jax.ShapeDtypeStruct((n_seq * L, D), BF16),
        scratch_shapes=[pltpu.VMEM((L + 2 * PAD, HY_CB), F32)],
        compiler_params=_cparams(("arbitrary", "arbitrary")),
        name=f"hyena_{L}",
    )(hy, hy, hy, conv_w, conv_w, conv_w, conv_b, conv_b, conv_b, spectra, hy_bias, *mats)


ROUTER_LANES = LANES
BIG_LANE = 1e9


def _first_max_lane(v, lanef):
    m = jnp.max(v, axis=-1, keepdims=True)
    return m, jnp.min(jnp.where(v == m, lanef, BIG_LANE), axis=-1, keepdims=True)


def _out_router_kernel(*refs, n_in):
    a_refs = refs[:n_in]
    w_ref, x_ref, mod_ref, gf_ref, wr_ref, br_ref, xo_ref, h2_ref, ids_ref, wts_ref = refs[n_in:]
    acc, k0 = None, 0
    for a_ref in a_refs:
        kk = a_ref.shape[1]
        part = jnp.dot(a_ref[...], w_ref[k0:k0 + kk, :], preferred_element_type=F32)
        acc = part if acc is None else acc + part
        k0 += kk
    xn = x_ref[...] + mod_ref[0, 2:3, :] * acc
    xo_ref[...] = xn
    h2 = _modulated(xn, gf_ref, mod_ref, 3)
    h2_ref[...] = h2

    logits = jnp.dot(h2, wr_ref[...], precision=HIGHEST, preferred_element_type=F32) + br_ref[...]
    lanef = lax.broadcasted_iota(jnp.int32, logits.shape, 1).astype(F32)
    gl = jnp.where(lanef < MOE_G, logits, NEG)
    gm, gi = _first_max_lane(gl, lanef)
    g_w = 1.0 / jnp.sum(jnp.exp(gl - gm), axis=-1, keepdims=True)
    lo = MOE_G + MOE_PG * gi
    el = jnp.where((lanef >= lo) & (lanef < lo + MOE_PG), logits, NEG)
    m1, e1 = _first_max_lane(el, lanef)
    m2, e2 = _first_max_lane(jnp.where(lanef == e1, NEG, el), lanef)
    p2 = jnp.exp(m2 - m1)
    w1 = g_w / (1.0 + p2)
    ids_ref[...] = jnp.where(lanef == 0, e1 - MOE_G, jnp.where(lanef == 1, e2 - MOE_G, 0.0)).astype(jnp.int32)
    wts_ref[...] = jnp.where(lanef == 0, w1, jnp.where(lanef == 1, w1 * p2, 0.0))


def out_proj_router(acts, w_bf, x, mod, gf, wr, br):
    tile = lambda w: pl.BlockSpec((TM, w), lambda i: (i, 0))
    full = lambda arr: pl.BlockSpec(arr.shape, lambda i: (0,) * arr.ndim)
    return pl.pallas_call(
        functools.partial(_out_router_kernel, n_in=len(acts)),
        grid=(N_TILES,),
        in_specs=[tile(a.shape[1]) for a in acts] + [full(w_bf), tile(D),
                  pl.BlockSpec((1, 6, D), lambda i: (_mod_row(i), 0, 0)), full(gf), full(wr), full(br)],
        out_specs=[tile(D), tile(D), tile(ROUTER_LANES), tile(ROUTER_LANES)],
        out_shape=[jax.ShapeDtypeStruct((T_ALL, D), F32), jax.ShapeDtypeStruct((T_ALL, D), F32),
                   jax.ShapeDtypeStruct((T_ALL, ROUTER_LANES), jnp.int32),
                   jax.ShapeDtypeStruct((T_ALL, ROUTER_LANES), F32)],
        compiler_params=_cparams(("arbitrary",)),
        name="out_router",
    )(*acts, w_bf, x, mod, gf, wr, br)


N_ASSIGN = 2 * T_ALL
MOE_TILES = N_ASSIGN // TM + MOE_E
N_SLOTS = MOE_TILES * TM


def route_tables(ids):
    flat = ids[:, :2].reshape(-1)
    onehot = (flat[:, None] == jnp.arange(MOE_E, dtype=jnp.int32)[None, :]).astype(jnp.int32)
    csum = jnp.cumsum(onehot, axis=0)
    rank = jnp.sum((csum - onehot) * onehot, axis=1)
    counts = csum[-1]
    padded = (counts + TM - 1) // TM * TM
    ends = jnp.cumsum(padded)
    dest = (ends - padded)[flat] + rank
    src = jnp.zeros((N_SLOTS,), jnp.int32).at[dest].set(jnp.arange(N_ASSIGN, dtype=jnp.int32) // 2)
    starts = jnp.arange(MOE_TILES, dtype=jnp.int32) * TM
    tile_expert = jnp.minimum(jnp.sum((ends[None, :] <= starts[:, None]).astype(jnp.int32), axis=1), MOE_E - 1)
    n_used = (ends[-1] // TM).astype(jnp.int32).reshape(1)
    return dest, src, tile_expert, n_used


DMA_UNROLL = 8


def _row_copy(src_hbm, row, dst_ref, dst_row, sem):
    return pltpu.make_async_copy(src_hbm.at[pl.ds(row, 1), :], dst_ref.at[pl.ds(dst_row, 1), :], sem)


def _gather_kernel(nu_ref, src_ref, h_hbm, o_ref, sem):
    i = pl.program_id(0)

    @pl.when(i < nu_ref[0])
    def _():
        def start(j, c):
            for u in range(DMA_UNROLL):
                r = j * DMA_UNROLL + u
                _row_copy(h_hbm, src_ref[0, 0, r], o_ref, r, sem).start(priority=u % 2)
            return c

        lax.fori_loop(0, TM // DMA_UNROLL, start, 0)
        pltpu.make_async_copy(h_hbm.at[pl.ds(0, TM), :], o_ref, sem).wait()

    @pl.when(i >= nu_ref[0])
    def _():
        o_ref[...] = jnp.zeros(o_ref.shape, o_ref.dtype)


def gather_rows(h, src, n_used):
    return pl.pallas_call(
        _gather_kernel,
        grid_spec=pltpu.PrefetchScalarGridSpec(
            num_scalar_prefetch=1,
            grid=(MOE_TILES,),
            in_specs=[pl.BlockSpec((1, 1, TM), lambda i, nu: (i, 0, 0), memory_space=pltpu.SMEM),
                      pl.BlockSpec(memory_space=pl.ANY)],
            out_specs=pl.BlockSpec((TM, D), lambda i, nu: (i, 0)),
            scratch_shapes=[pltpu.SemaphoreType.DMA(())],
        ),
        out_shape=jax.ShapeDtypeStruct((N_SLOTS, D), F32),
        compiler_params=_cparams(("arbitrary",)),
        name="moe_gather",
    )(n_used, src.reshape(MOE_TILES, 1, TM), h)


def _experts_kernel(te_ref, nu_ref, x_ref, wg_ref, wu_ref, wd_ref, o_ref):
    i = pl.program_id(0)

    @pl.when(i < nu_ref[0])
    def _():
        x = x_ref[...].astype(BF16)
        g = jnp.dot(x, wg_ref[0, 0].astype(BF16), preferred_element_type=F32)
        u = jnp.dot(x, wu_ref[0, 0].astype(BF16), preferred_element_type=F32)
        hid = (_silu(g) * u).astype(BF16)
        o_ref[...] = jnp.dot(hid, wd_ref[0, 0].astype(BF16), preferred_element_type=F32)

    @pl.when(i >= nu_ref[0])
    def _():
        o_ref[...] = jnp.zeros(o_ref.shape, o_ref.dtype)


def grouped_experts(xs, w_gate, w_up, w_down, tile_expert, n_used, layer):
    wspec = lambda a, b: pl.BlockSpec((1, 1, a, b), lambda i, te, nu: (layer, te[i], 0, 0))
    return pl.pallas_call(
        _experts_kernel,
        grid_spec=pltpu.PrefetchScalarGridSpec(
            num_scalar_prefetch=2,
            grid=(MOE_TILES,),
            in_specs=[pl.BlockSpec((TM, D), lambda i, te, nu: (i, 0)),
                      wspec(D, MOE_F), wspec(D, MOE_F), wspec(MOE_F, D)],
            out_specs=pl.BlockSpec((TM, D), lambda i, te, nu: (i, 0)),
        ),
        out_shape=jax.ShapeDtypeStruct((N_SLOTS, D), F32),
        compiler_params=_cparams(("arbitrary",)),
        name="moe_experts",
    )(tile_expert, n_used, xs, w_gate, w_up, w_down)


def _combine_kernel(dest_ref, ys_hbm, x_ref, wts_ref, mod_ref, gfin_ref, o_ref, buf, sem, *, final):
    def start(j, c):
        for u in range(DMA_UNROLL // 2):
            r = j * (DMA_UNROLL // 2) + u
            for k in range(2):
                _row_copy(ys_hbm, dest_ref[0, 0, 2 * r + k], buf.at[k], r, sem).start(priority=k)
        return c

    lax.fori_loop(0, 2 * TM // DMA_UNROLL, start, 0)
    for k in range(2):
        pltpu.make_async_copy(ys_hbm.at[pl.ds(0, TM), :], buf.at[k], sem).wait()
    moe = wts_ref[:, 0:1] * buf[0] + wts_ref[:, 1:2] * buf[1]
    xn = x_ref[...] + mod_ref[0, 5:6, :] * moe
    if final:
        xn = _rms(xn) * gfin_ref[...]
    o_ref[...] = xn


def moe_combine(ys, dest, x, wts, mod, gfin, *, final):
    tile = lambda w: pl.BlockSpec((TM, w), lambda i: (i, 0))
    return pl.pallas_call(
        functools.partial(_combine_kernel, final=final),
        grid=(N_TILES,),
        in_specs=[pl.BlockSpec((1, 1, 2 * TM), lambda i: (i, 0, 0), memory_space=pltpu.SMEM),
                  pl.BlockSpec(memory_space=pl.ANY), tile(D), tile(ROUTER_LANES),
                  pl.BlockSpec((1, 6, D), lambda i: (_mod_row(i), 0, 0)),
                  pl.BlockSpec((1, D), lambda i: (0, 0))],
        out_specs=tile(D),
        out_shape=jax.ShapeDtypeStruct((T_ALL, D), F32),
        scratch_shapes=[pltpu.VMEM((2, TM, D), F32), pltpu.SemaphoreType.DMA(())],
        compiler_params=_cparams(("arbitrary",)),
        name="moe_combine",
    )(dest.reshape(N_TILES, 1, 2 * TM), ys, x, wts, mod, gfin)


ODD_COLS = 2048
ROPE_Q = MLA_H * MLA_ROPE
ROPE_SHIFT = ROPE_F


def rope_tables():
    t = np.arange(L_LAT)
    pos = np.stack([t // GRID_W, t % GRID_W], axis=1).astype(np.float64)
    inv = 10000.0 ** (-np.arange(ROPE_F, dtype=np.float64) / ROPE_F)
    lane = np.arange(ROPE_Q) % MLA_ROPE
    axis = lane // (2 * ROPE_F)
    first = (lane % (2 * ROPE_F)) < ROPE_F
    ang = pos[:, axis] * inv[lane % ROPE_F][None, :]
    cos, sin = np.cos(ang), np.sin(ang)
    tabs = [cos, np.where(first[None, :], -sin, 0.0), np.where(first[None, :], 0.0, sin)]
    ident = [np.ones((1, TM, ROPE_Q)), np.zeros((1, TM, ROPE_Q)), np.zeros((1, TM, ROPE_Q))]
    return [jnp.asarray(np.concatenate([i, tb.reshape(LAT_TILES_PER_SEQ, TM, ROPE_Q)], axis=0).astype(np.float32))
            for i, tb in zip(ident, tabs)]


def _rope(x, c, a, b):
    n = x.shape[1]
    return x * c[:, :n] + pltpu.roll(x, n - ROPE_SHIFT, 1) * a[:, :n] + pltpu.roll(x, ROPE_SHIFT, 1) * b[:, :n]


def _odd_in_kernel(x_ref, mod_ref, g_ref, w_ref, gq_ref, wuq_ref, gkv_ref, wukv_ref, rc_ref, ra_ref, rb_ref,
                   qkv_ref, qm_ref, ckv_ref, kvu_ref, kr_ref):
    hb = _modulated(x_ref[...], g_ref, mod_ref, 0).astype(BF16)
    for c0 in range(0, 3 * NA_W, PROJ_CHUNK):
        qkv_ref[:, c0:c0 + PROJ_CHUNK] = jnp.dot(hb, w_ref[:, c0:c0 + PROJ_CHUNK], preferred_element_type=F32)
    rest = jnp.dot(hb, w_ref[:, 3 * NA_W:ODD_COLS], preferred_element_type=F32)
    rc, ra, rb = rc_ref[0], ra_ref[0], rb_ref[0]
    qd = (_rms(rest[:, 0:MLA_QR]) * gq_ref[...]).astype(BF16)
    qm = jnp.dot(qd, wuq_ref[...], preferred_element_type=F32)
    qm_ref[:, 0:MLA_H * MLA_NOPE] = qm[:, 0:MLA_H * MLA_NOPE]
    qm_ref[:, MLA_H * MLA_NOPE:] = _rope(qm[:, MLA_H * MLA_NOPE:], rc, ra, rb)
    ckv = _rms(rest[:, MLA_QR:MLA_QR + MLA_KVR]) * gkv_ref[...]
    ckv_ref[...] = ckv
    kvu_ref[...] = jnp.dot(ckv.astype(BF16), wukv_ref[...], preferred_element_type=F32)
    kr_ref[...] = _rope(rest[:, MLA_QR + MLA_KVR:], rc, ra, rb)


def odd_in_proj(x, mod, g, w_bf, gq, wuq_bf, gkv, wukv_bf, tabs):
    tile = lambda w: pl.BlockSpec((TM, w), lambda i: (i, 0))
    full = lambda arr: pl.BlockSpec(arr.shape, lambda i: (0,) * arr.ndim)
    tab = pl.BlockSpec((1, TM, ROPE_Q),
                       lambda i: (jnp.where(i < CTX_TILES, 0, 1 + (i - CTX_TILES) % LAT_TILES_PER_SEQ), 0, 0))
    widths = (3 * NA_W, MLA_H * MLA_QK, MLA_KVR, MLA_H * (MLA_NOPE + MLA_V), LANES)
    return pl.pallas_call(
        _odd_in_kernel,
        grid=(N_TILES,),
        in_specs=[tile(D), pl.BlockSpec((1, 6, D), lambda i: (_mod_row(i), 0, 0)), full(g), full(w_bf),
                  full(gq), full(wuq_bf), full(gkv), full(wukv_bf), tab, tab, tab],
        out_specs=[tile(w) for w in widths],
        out_shape=[jax.ShapeDtypeStruct((T_ALL, w), F32) for w in widths],
        compiler_params=_cparams(("arbitrary",)),
        name="odd_in",
    )(x, mod, g, w_bf, gq, wuq_bf, gkv, wukv_bf, *tabs)


NA_SCALE = NA_D ** -0.5
MLA_SCALE = MLA_QK ** -0.5
NT = (((1,), (1,)), ((), ()))


def _softmax_pv(scores, values):
    m = functools.reduce(jnp.maximum, [jnp.max(s, axis=-1, keepdims=True) for s in scores])
    ps = [jnp.exp(s - m) for s in scores]
    den = functools.reduce(jnp.add, [jnp.sum(p, axis=-1, keepdims=True) for p in ps])
    acc = functools.reduce(jnp.add, [jnp.dot(p.astype(BF16), v, preferred_element_type=F32) for p, v in zip(ps, values)])
    return acc / den


def _head(ref_or_val, h, width, base=0):
    return ref_or_val[:, base + h * width:base + (h + 1) * width]


def _mla_scores(qm, h, kn, kr):
    qn = _head(qm, h, MLA_NOPE).astype(BF16)
    qr = _head(qm, h, MLA_ROPE, MLA_H * MLA_NOPE).astype(BF16)
    return (lax.dot_general(qn, kn, NT, preferred_element_type=F32)
            + lax.dot_general(qr, kr, NT, preferred_element_type=F32)) * MLA_SCALE


def _attn_ctx_kernel(qkv_ref, qm_ref, kvu_ref, kr_ref, ona_ref, omla_ref):
    kr = kr_ref[:, 0:MLA_ROPE].astype(BF16)
    for h in range(NA_H):
        q = _head(qkv_ref, h, NA_D).astype(BF16)
        k = _head(qkv_ref, h, NA_D, NA_W).astype(BF16)
        v = _head(qkv_ref, h, NA_D, 2 * NA_W).astype(BF16)
        s = lax.dot_general(q, k, NT, preferred_element_type=F32) * NA_SCALE
        ona_ref[:, h * NA_D:(h + 1) * NA_D] = _softmax_pv([s], [v]).astype(ona_ref.dtype)
    for h in range(MLA_H):
        kn = _head(kvu_ref, h, MLA_NOPE).astype(BF16)
        v = _head(kvu_ref, h, MLA_V, MLA_H * MLA_NOPE).astype(BF16)
        s = _mla_scores(qm_ref, h, kn, kr)
        omla_ref[:, h * MLA_V:(h + 1) * MLA_V] = _softmax_pv([s], [v]).astype(omla_ref.dtype)


def attn_context(qkv, qm, kvu, kr):
    seq = lambda w: pl.BlockSpec((L_CTX, w), lambda b: (b, 0))
    return pl.pallas_call(
        _attn_ctx_kernel,
        grid=(N_CTX,),
        in_specs=[seq(3 * NA_W), seq(MLA_H * MLA_QK), seq(MLA_H * (MLA_NOPE + MLA_V)), seq(LANES)],
        out_specs=[seq(NA_W), seq(MLA_H * MLA_V)],
        out_shape=[jax.ShapeDtypeStruct((T_CTX, NA_W), BF16), jax.ShapeDtypeStruct((T_CTX, MLA_H * MLA_V), BF16)],
        compiler_params=_cparams(("arbitrary",)),
        name="attn_ctx",
    )(qkv, qm, kvu, kr)


def neighbourhood_bias(rel_bias):
    rows = L_LAT // GRID_W
    r = np.arange(rows)
    r0 = np.clip(r - NA_WIN_R // 2, 0, rows - NA_WIN_R)
    c = np.arange(GRID_W)
    c0 = np.clip(c - NA_WIN_C // 2, 0, GRID_W - NA_WIN_C)
    row_ok = (r[None, :] >= r0[:, None]) & (r[None, :] < r0[:, None] + NA_WIN_R)
    col_ok = (c[None, :] >= c0[:, None]) & (c[None, :] < c0[:, None] + NA_WIN_C)
    dr = np.clip(r[None, :] - r[:, None] + NA_WIN_R - 1, 0, 2 * NA_WIN_R - 2)
    dc = np.clip(c[None, :] - c[:, None], -(NA_WIN_C - 1), NA_WIN_C - 1) + NA_WIN_C - 1
    sel_r = (dr[:, :, None] == np.arange(2 * NA_WIN_R - 1)) & row_ok[:, :, None]
    sel_c = (dc[:, :, None] == np.arange(2 * NA_WIN_C - 1)) & col_ok[:, :, None]
    t = jnp.einsum("hdj,qcj->hdqc", rel_bias.astype(F32), jnp.asarray(sel_c.astype(np.float32)), precision=HIGHEST)
    b = jnp.einsum("rkd,hdqc->hrqkc", jnp.asarray(sel_r.astype(np.float32)), t, precision=HIGHEST)
    ok = jnp.asarray(row_ok[:, None, :, None] & col_ok[None, :, None, :])
    return jnp.where(ok[None], b, NEG).reshape(NA_H, L_LAT, L_LAT)


def _na_lat_kernel(q_ref, k_ref, v_ref, kc_ref, vc_ref, b_ref, o_ref):
    for h in range(NA_H):
        q = _head(q_ref, h, NA_D).astype(BF16)
        k = _head(k_ref, h, NA_D).astype(BF16)
        v = _head(v_ref, h, NA_D).astype(BF16)
        kc = kc_ref[0, 0, h].astype(BF16)
        vc = vc_ref[0, 0, h].astype(BF16)
        s1 = lax.dot_general(q, k, NT, preferred_element_type=F32) * NA_SCALE + b_ref[h]
        s2 = lax.dot_general(q, kc, NT, preferred_element_type=F32) * NA_SCALE
        o_ref[:, h * NA_D:(h + 1) * NA_D] = _softmax_pv([s1, s2], [v, vc]).astype(o_ref.dtype)


def attn_neighbourhood_latent(qkv, cache_k, cache_v, bias):
    nq = L_LAT // TM
    t0 = T_CTX // TM
    s0 = T_CTX // L_LAT
    cache = pl.BlockSpec((1, 1, NA_H, PAST, NA_D), lambda qt, b: (b, 0, 0, 0, 0))
    return pl.pallas_call(
        _na_lat_kernel,
        grid=(nq, N_LAT),
        in_specs=[pl.BlockSpec((TM, NA_W), lambda qt, b: (t0 + b * nq + qt, 0)),
                  pl.BlockSpec((L_LAT, NA_W), lambda qt, b: (s0 + b, 1)),
                  pl.BlockSpec((L_LAT, NA_W), lambda qt, b: (s0 + b, 2)),
                  cache, cache,
                  pl.BlockSpec((NA_H, TM, L_LAT), lambda qt, b: (0, qt, 0))],
        out_specs=pl.BlockSpec((TM, NA_W), lambda qt, b: (b * nq + qt, 0)),
        out_shape=jax.ShapeDtypeStruct((T_LAT, NA_W), BF16),
        compiler_params=_cparams(("arbitrary", "arbitrary")),
        name="attn_na_lat",
    )(qkv, qkv, qkv, cache_k, cache_v, bias)


def _mla_lat_kernel(qm_ref, kvu_ref, kr_ref, ckv_ref, krc_ref, wukv_ref, o_ref):
    kvc = jnp.dot(ckv_ref[0, 0].astype(BF16), wukv_ref[...], preferred_element_type=F32)
    kr = kr_ref[:, 0:MLA_ROPE].astype(BF16)
    krc = krc_ref[0, 0].astype(BF16)
    for h in range(MLA_H):
        kn = _head(kvu_ref, h, MLA_NOPE).astype(BF16)
        v = _head(kvu_ref, h, MLA_V, MLA_H * MLA_NOPE).astype(BF16)
        knc = _head(kvc, h, MLA_NOPE).astype(BF16)
        vc = _head(kvc, h, MLA_V, MLA_H * MLA_NOPE).astype(BF16)
        s1 = _mla_scores(qm_ref, h, kn, kr)
        s2 = _mla_scores(qm_ref, h, knc, krc)
        o_ref[:, h * MLA_V:(h + 1) * MLA_V] = _softmax_pv([s1, s2], [v, vc]).astype(o_ref.dtype)


def attn_mla_latent(qm, kvu, kr, cache_ckv, cache_krope, wukv_bf):
    nq = L_LAT // TM
    t0 = T_CTX // TM
    s0 = T_CTX // L_LAT
    return pl.pallas_call(
        _mla_lat_kernel,
        grid=(nq, N_LAT),
        in_specs=[pl.BlockSpec((TM, MLA_H * MLA_QK), lambda qt, b: (t0 + b * nq + qt, 0)),
                  pl.BlockSpec((L_LAT, MLA_H * (MLA_NOPE + MLA_V)), lambda qt, b: (s0 + b, 0)),
                  pl.BlockSpec((L_LAT, LANES), lambda qt, b: (s0 + b, 0)),
                  pl.BlockSpec((1, 1, PAST, MLA_KVR), lambda qt, b: (b, 0, 0, 0)),
                  pl.BlockSpec((1, 1, PAST, MLA_ROPE), lambda qt, b: (b, 0, 0, 0)),
                  pl.BlockSpec(wukv_bf.shape, lambda qt, b: (0, 0))],
        out_specs=pl.BlockSpec((TM, MLA_H * MLA_V), lambda qt, b: (b * nq + qt, 0)),
        out_shape=jax.ShapeDtypeStruct((T_LAT, MLA_H * MLA_V), BF16),
        compiler_params=_cparams(("arbitrary", "arbitrary")),
        name="attn_mla_lat",
    )(qm, kvu, kr, cache_ckv, cache_krope, wukv_bf)


def moe_block(h2, ids, wts, x, mod, gfin, w_gate, w_up, w_down, layer, *, final):
    dest, src, tile_expert, n_used = route_tables(ids)
    xs = gather_rows(h2, src, n_used)
    ys = grouped_experts(xs, w_gate, w_up, w_down, tile_expert, n_used, layer)
    return moe_combine(ys, dest, x, wts, mod, gfin, final=final)


def _pad_lanes(a):
    return jnp.pad(a, ((0, 0), (0, LANES - a.shape[1])))


def _hyena_features(L):
    t = np.linspace(0.0, 1.0, L)[:, None]
    w = 2.0 * math.pi * np.arange(L) / L
    bands = np.linspace(1e-4, HY_BANDS - 1, HY_BANDS)
    ang = w[:, None] * bands[None]
    feat = np.concatenate([t, np.cos(ang), -np.sin(ang)], axis=-1)
    return jnp.asarray(np.pad(feat, ((0, 0), (0, LANES - HY_FEAT))).astype(np.float32))


def _router_params(w_gr, b_gr, w_er, b_er):
    wr = _pad_lanes(jnp.concatenate([w_gr, w_er], axis=1))
    br = _pad_lanes(jnp.concatenate([b_gr, b_er])[None])
    return wr, br


def _even_layer(x, mod, g_mix, state, w_in, conv_w, conv_b, a_log, dt_bias, d_skip, g_ssd, hy_conv_w, hy_conv_b,
                hy_w1, hy_b1, hy_w2, hy_b2, hy_w3, hy_freq, hy_bias):
    n0 = D + SSD_XBC
    w_bf = jnp.concatenate([w_in[:, :n0], w_in[:, n0 + SSD_H:], w_in[:, n0:n0 + SSD_H],
                            jnp.zeros((D, LANES - SSD_H), F32)], axis=1).astype(BF16)
    z, xbc, hy, dtr = even_in_proj(x, mod, g_mix, w_bf)
    small = (conv_w, conv_b[None], _pad_lanes(dt_bias), _pad_lanes(a_log), jnp.repeat(d_skip, SSD_P)[None], g_ssd[None])
    y_c, fin = ssd_mixer(xbc, dtr, z, None, *small, L=L_CTX, n_seq=N_CTX, row_off=0)
    (y_l,) = ssd_mixer(xbc, dtr, z, state.reshape(N_LAT, 2, SSD_H * SSD_P, SSD_N), *small,
                       L=L_LAT, n_seq=N_LAT, row_off=T_CTX)
    w1 = jnp.pad(hy_w1, ((0, LANES - HY_FEAT), (0, 0)))
    w3r = hy_w3.reshape(HY_HID, 4, D).transpose(1, 0, 2)
    deltas = jnp.asarray(np.linspace(HY_MIN_DECAY, HY_MAX_DECAY, D).astype(np.float32))[None]
    us = []
    for L, n_seq, off in ((L_CTX, N_CTX, 0), (L_LAT, N_LAT, T_CTX)):
        f_hi, f_lo, g_hi = dft_matrices(L)
        spectra = hyena_filter_spectra(_hyena_features(L), w1, hy_b1[None], hy_w2, hy_b2[None], hy_freq, w3r, deltas,
                                       f_hi, f_lo, L=L)
        us.append(hyena_mixer(hy, hy_conv_w, hy_conv_b[None], spectra, hy_bias, (f_hi, g_hi),
                              L=L, n_seq=n_seq, row_off=off))
    return jnp.concatenate([y_c, y_l], axis=0), jnp.concatenate(us, axis=0), fin


def _odd_layer(x, mod, g_mix, cache_k, cache_v, cache_ckv, cache_kr, rel_bias, w_in, g_q, w_uq, g_kv, w_ukv):
    w_bf = jnp.pad(w_in, ((0, 0), (0, ODD_COLS - w_in.shape[1]))).astype(BF16)
    wuq = w_uq.reshape(MLA_QR, MLA_H, MLA_QK)
    wuq_bf = jnp.concatenate([wuq[:, :, :MLA_NOPE].reshape(MLA_QR, -1), wuq[:, :, MLA_NOPE:].reshape(MLA_QR, -1)],
                             axis=1).astype(BF16)
    wukv = w_ukv.reshape(MLA_KVR, MLA_H, MLA_NOPE + MLA_V)
    wukv_bf = jnp.concatenate([wukv[:, :, :MLA_NOPE].reshape(MLA_KVR, -1), wukv[:, :, MLA_NOPE:].reshape(MLA_KVR, -1)],
                              axis=1).astype(BF16)
    qkv, qm, ckv, kvu, kr = odd_in_proj(x, mod, g_mix, w_bf, g_q[None], wuq_bf, g_kv[None], wukv_bf, rope_tables())
    ona_c, omla_c = attn_context(qkv, qm, kvu, kr)
    ona_l = attn_neighbourhood_latent(qkv, cache_k, cache_v, neighbourhood_bias(rel_bias))
    omla_l = attn_mla_latent(qm, kvu, kr, cache_ckv, cache_kr, wukv_bf)
    return jnp.concatenate([ona_c, ona_l], axis=0), jnp.concatenate([omla_c, omla_l], axis=0), qkv, ckv, kr


def kernel(x_prompt, x_sample, state_ssd, cache_na_k, cache_na_v, cache_mla_ckv, cache_mla_krope, c, c_ctx, w_ada, b_ada, norm_mix, norm_ffn, norm_final, ev_w_in, ev_conv_w, ev_conv_b, ssd_A_log, ssd_dt_bias, ssd_d, ssd_norm, hy_conv_w, hy_conv_b, hy_w1, hy_b1, hy_w2, hy_b2, hy_w3, hy_freq, hy_bias, ev_w_out, od_w_in, mla_q_norm, mla_w_uq, mla_kv_norm, mla_w_ukv, na_rel_bias, od_w_out, moe_w_gr, moe_b_gr, moe_w_er, moe_b_er, moe_w_gate, moe_w_up, moe_w_down):
    x = jnp.concatenate([x_prompt.reshape(T_CTX, D), x_sample.reshape(T_LAT, D)], axis=0)
    cvec = jnp.zeros((MOD_ROWS, D), F32).at[0].set(c_ctx).at[1:1 + N_LAT].set(c)
    mod = ada_modulation(cvec, w_ada, b_ada)
    gfin = norm_final[None]

    y, u, fin = _even_layer(x, mod[0], norm_mix[0][None], state_ssd[:, 0], ev_w_in[0], ev_conv_w[0], ev_conv_b[0],
                            ssd_A_log[0], ssd_dt_bias[0], ssd_d[0], ssd_norm[0], hy_conv_w[0], hy_conv_b[0],
                            hy_w1[0], hy_b1[0], hy_w2[0], hy_b2[0], hy_w3[0], hy_freq[0], hy_bias[0])
    wr, br = _router_params(moe_w_gr[0], moe_b_gr[0], moe_w_er[0], moe_b_er[0])
    xn, h2, ids, wts = out_proj_router([y, u], ev_w_out[0].astype(BF16), x, mod[0], norm_ffn[0][None], wr, br)
    x = moe_block(h2, ids, wts, xn, mod[0], gfin, moe_w_gate, moe_w_up, moe_w_down, 0, final=False)

    o_na, o_mla, qkv, ckv, kr = _odd_layer(x, mod[1], norm_mix[1][None], cache_na_k, cache_na_v, cache_mla_ckv,
                                           cache_mla_krope, na_rel_bias[0], od_w_in[0], mla_q_norm[0], mla_w_uq[0],
                                           mla_kv_norm[0], mla_w_ukv[0])
    wr, br = _router_params(moe_w_gr[1], moe_b_gr[1], moe_w_er[1], moe_b_er[1])
    xn, h2, ids, wts = out_proj_router([o_na, o_mla], od_w_out[0].astype(BF16), x, mod[1], norm_ffn[1][None], wr, br)
    out = moe_block(h2, ids, wts, xn, mod[1], gfin, moe_w_gate, moe_w_up, moe_w_down, 1, final=True)

    heads = lambda a: a.reshape(N_CTX, L_CTX, NA_H, NA_D).transpose(0, 2, 1, 3)[:, None]
    return (out[:T_CTX].reshape(N_CTX, L_CTX, D),
            out[T_CTX:].reshape(N_LAT, L_LAT, D),
            fin.reshape(N_CTX, 1, 2, SSD_H, SSD_P, SSD_N),
            heads(qkv[:T_CTX, NA_W:2 * NA_W]),
            heads(qkv[:T_CTX, 2 * NA_W:3 * NA_W]),
            ckv[:T_CTX].reshape(N_CTX, 1, L_CTX, MLA_KVR),
            kr[:T_CTX, :MLA_ROPE].reshape(N_CTX, 1, L_CTX, MLA_ROPE))
```

```python
import functools
import math

import numpy as np
import jax
import jax.numpy as jnp
from jax import lax
from jax.experimental import pallas as pl
from jax.experimental.pallas import tpu as pltpu

F32 = jnp.float32
BF16 = jnp.bfloat16
HIGHEST = lax.Precision.HIGHEST

D = 1024
N_CTX, L_CTX = 16, 256
N_LAT, L_LAT = 8, 1024
T_CTX = N_CTX * L_CTX
T_LAT = N_LAT * L_LAT
T_ALL = T_CTX + T_LAT
PAST = 512
GRID_W = 64
EPS = 1e-6
NEG = -1e30

SSD_H, SSD_P, SSD_N, SSD_G = 16, 64, 128, 2
SSD_XBC = D + 2 * SSD_G * SSD_N
SSD_K = 5
CHUNK = 128

HY_K = 3
HY_BANDS = 16
HY_FEAT = 1 + 2 * HY_BANDS
HY_HID = 64
HY_MIN_DECAY = abs(math.log(1e-2) / 1.5)
HY_MAX_DECAY = abs(math.log(1e-2) / 0.3)

NA_H, NA_D = 8, 64
NA_W = NA_H * NA_D
NA_WIN_R, NA_WIN_C = 8, 16
MLA_H, MLA_QR, MLA_KVR = 8, 256, 128
MLA_NOPE, MLA_ROPE, MLA_V = 64, 32, 64
MLA_QK = MLA_NOPE + MLA_ROPE
ROPE_F = MLA_ROPE // 4

MOE_G, MOE_PG, MOE_E, MOE_F = 4, 8, 32, 256

LANES = 128
SUBLANES = 8
VMEM_LIMIT = 56 * 1024 * 1024

TM = 256
N_TILES = T_ALL // TM
CTX_TILES = T_CTX // TM
LAT_TILES_PER_SEQ = L_LAT // TM
MOD_ROWS = 16


def _cparams(sem):
    return pltpu.CompilerParams(dimension_semantics=sem, vmem_limit_bytes=VMEM_LIMIT)


def _mod_row(i):
    return jnp.where(i < CTX_TILES, 0, 1 + (i - CTX_TILES) // LAT_TILES_PER_SEQ)


def _silu(x):
    return x * jax.nn.sigmoid(x)


def _rms(x):
    return x * lax.rsqrt(jnp.mean(x * x, axis=-1, keepdims=True) + EPS)


def _ada_kernel(c_ref, w_ref, b_ref, o_ref):
    c = c_ref[...]
    o_ref[0] = jnp.dot(_silu(c), w_ref[0], precision=HIGHEST, preferred_element_type=F32) + b_ref[0]


def ada_modulation(cvec, w_ada, b_ada):
    depth = w_ada.shape[0]
    out = pl.pallas_call(
        _ada_kernel,
        grid=(depth, 6),
        in_specs=[
            pl.BlockSpec((MOD_ROWS, D), lambda l, j: (0, 0)),
            pl.BlockSpec((1, D, D), lambda l, j: (l, 0, j)),
            pl.BlockSpec((1, 1, D), lambda l, j: (l, 0, j)),
        ],
        out_specs=pl.BlockSpec((1, MOD_ROWS, D), lambda l, j: (l, 0, j)),
        out_shape=jax.ShapeDtypeStruct((depth, MOD_ROWS, 6 * D), F32),
        compiler_params=_cparams(("arbitrary", "arbitrary")),
        name="ada",
    )(cvec, w_ada, b_ada.reshape(depth, 1, 6 * D))
    return out.reshape(depth, MOD_ROWS, 6, D)


PROJ_CHUNK = 512


def _modulated(x, g_ref, mod_ref, shift_row):
    h = _rms(x) * g_ref[...]
    return h * (1.0 + mod_ref[0, shift_row + 1:shift_row + 2, :]) + mod_ref[0, shift_row:shift_row + 1, :]


def _even_in_kernel(x_ref, mod_ref, g_ref, w_ref, z_ref, xbc_ref, hy_ref, dt_ref):
    hb = _modulated(x_ref[...], g_ref, mod_ref, 0).astype(BF16)
    col = 0
    for o_ref in (z_ref, xbc_ref, hy_ref, dt_ref):
        width = o_ref.shape[1]
        for c0 in range(0, width, PROJ_CHUNK):
            c1 = min(c0 + PROJ_CHUNK, width)
            o_ref[:, c0:c1] = jnp.dot(hb, w_ref[:, col + c0:col + c1], preferred_element_type=F32)
        col += width


def even_in_proj(x, mod, g, w_bf):
    widths = (D, SSD_XBC, 3 * D, LANES)
    return pl.pallas_call(
        _even_in_kernel,
        grid=(N_TILES,),
        in_specs=[
            pl.BlockSpec((TM, D), lambda i: (i, 0)),
            pl.BlockSpec((1, 6, D), lambda i: (_mod_row(i), 0, 0)),
            pl.BlockSpec((1, D), lambda i: (0, 0)),
            pl.BlockSpec(w_bf.shape, lambda i: (0, 0)),
        ],
        out_specs=[pl.BlockSpec((TM, w), lambda i: (i, 0)) for w in widths],
        out_shape=[jax.ShapeDtypeStruct((T_ALL, w), F32) for w in widths],
        compiler_params=_cparams(("arbitrary",)),
        name="even_in",
    )(x, mod, g, w_bf)


PAD = SUBLANES


def _ssd_kernel(*refs, L, has_init):
    if has_init:
        (xbc_ref, dt_ref, z_ref, init_ref, cw_ref, cb_ref, dtb_ref, alog_ref, dsk_ref, gs_ref,
         y_ref, xp_s, xc_s, ya_s, st_s) = refs
        fin_ref = None
    else:
        (xbc_ref, dt_ref, z_ref, cw_ref, cb_ref, dtb_ref, alog_ref, dsk_ref, gs_ref,
         y_ref, fin_ref, xp_s, xc_s, ya_s, st_s) = refs
        init_ref = None
    nc = L // CHUNK
    half = SSD_K // 2

    xp_s[0:PAD, :] = jnp.zeros((PAD, SSD_XBC), F32)
    xp_s[PAD + L:2 * PAD + L, :] = jnp.zeros((PAD, SSD_XBC), F32)
    xp_s[PAD:PAD + L, :] = xbc_ref[...]
    for c in range(nc):
        base = PAD + c * CHUNK - half
        for j in range(SSD_XBC // LANES):
            cols = slice(j * LANES, (j + 1) * LANES)
            acc = cb_ref[:, cols] + xp_s[base:base + CHUNK, cols] * cw_ref[0:1, cols]
            for k in range(1, SSD_K):
                acc = acc + xp_s[base + k:base + k + CHUNK, cols] * cw_ref[k:k + 1, cols]
            xc_s[c * CHUNK:(c + 1) * CHUNK, cols] = _silu(acc)

    row = lax.broadcasted_iota(jnp.int32, (CHUNK, CHUNK), 0)
    colm = lax.broadcasted_iota(jnp.int32, (CHUNK, CHUNK), 1)
    lane_lo = colm < SSD_P
    tri_lo = (colm <= row).astype(F32)
    tri_up = (colm >= row).astype(F32)

    for d in range(2):
        causal = (colm <= row) if d == 0 else (colm >= row)
        for j in range(SSD_H * SSD_P // CHUNK):
            if has_init:
                st_s[:, j * CHUNK:(j + 1) * CHUNK] = init_ref[0, d, j * CHUNK:(j + 1) * CHUNK, :].T
            else:
                st_s[:, j * CHUNK:(j + 1) * CHUNK] = jnp.zeros((CHUNK, CHUNK), F32)

        def chunk_body(ci, carry, d=d, causal=causal):
            c = ci if d == 0 else nc - 1 - ci
            r0 = pl.multiple_of(c * CHUNK, CHUNK)
            dt = jax.nn.softplus(dt_ref[pl.ds(r0, CHUNK), :] + dtb_ref[d:d + 1, :])
            a = dt * (-jnp.exp(alog_ref[d:d + 1, :]))
            tri = tri_lo if d == 0 else tri_up
            cs = jnp.dot(tri, a, precision=HIGHEST, preferred_element_type=F32)
            cs_t = jnp.dot(a.T, tri.T, precision=HIGHEST, preferred_element_type=F32)
            edge = cs[CHUNK - 1:CHUNK, :] if d == 0 else cs[0:1, :]
            ecs = jnp.exp(cs)
            dec = jnp.exp(edge - cs)
            cdec = jnp.exp(edge)
            for g in range(SSD_G):
                bm = xc_s[pl.ds(r0, CHUNK), D + g * SSD_N:D + (g + 1) * SSD_N]
                cm = xc_s[pl.ds(r0, CHUNK), D + (SSD_G + g) * SSD_N:D + (SSD_G + g + 1) * SSD_N]
                bm_b, cm_b = bm.astype(BF16), cm.astype(BF16)
                cb = lax.dot_general(cm_b, bm_b, (((1,), (1,)), ((), ())), preferred_element_type=F32)
                bm_t = bm.T.astype(BF16)
                pairs = SSD_H // SSD_G // 2
                for pp in range(pairs):
                    p = g * pairs + pp
                    h0, h1 = 2 * p, 2 * p + 1
                    cols = slice(p * CHUNK, (p + 1) * CHUNK)
                    xs = xc_s[pl.ds(r0, CHUNK), cols]
                    xdt = xs * jnp.where(lane_lo, dt[:, h0:h0 + 1], dt[:, h1:h1 + 1])
                    ms = []
                    for h in (h0, h1):
                        diff = cs[:, h:h + 1] - cs_t[h:h + 1, :]
                        ms.append(cb * jnp.exp(jnp.where(causal, diff, NEG)))
                    mcat = jnp.concatenate(ms, axis=1).astype(BF16)
                    xbd = jnp.concatenate([jnp.where(lane_lo, xdt, 0.0), jnp.where(lane_lo, 0.0, xdt)],
                                          axis=0).astype(BF16)
                    y_diag = jnp.dot(mcat, xbd, preferred_element_type=F32)
                    st = st_s[:, cols]
                    y_off = jnp.dot(cm_b, st.astype(BF16), preferred_element_type=F32)
                    y_off = y_off * jnp.where(lane_lo, ecs[:, h0:h0 + 1], ecs[:, h1:h1 + 1])
                    y = y_diag + y_off
                    if d == 0:
                        ya_s[pl.ds(r0, CHUNK), cols] = y
                    else:
                        ya_s[pl.ds(r0, CHUNK), cols] = ya_s[pl.ds(r0, CHUNK), cols] + y
                    xdd = (xdt * jnp.where(lane_lo, dec[:, h0:h0 + 1], dec[:, h1:h1 + 1])).astype(BF16)
                    snew = jnp.dot(bm_t, xdd, preferred_element_type=F32)
                    st_s[:, cols] = st * jnp.where(lane_lo[0:1, :], cdec[:, h0:h0 + 1], cdec[:, h1:h1 + 1]) + snew
            return carry

        lax.fori_loop(0, nc, chunk_body, 0)
        if fin_ref is not None:
            for j in range(SSD_H * SSD_P // CHUNK):
                fin_ref[0, d, j * CHUNK:(j + 1) * CHUNK, :] = st_s[:, j * CHUNK:(j + 1) * CHUNK].T

    def out_body(c, carry):
        r0 = pl.multiple_of(c * CHUNK, CHUNK)
        y = ya_s[pl.ds(r0, CHUNK), :] + xc_s[pl.ds(r0, CHUNK), 0:D] * dsk_ref[...]
        y = y * _silu(z_ref[pl.ds(r0, CHUNK), :])
        y_ref[pl.ds(r0, CHUNK), :] = (_rms(y) * gs_ref[...]).astype(y_ref.dtype)
        return carry

    lax.fori_loop(0, nc, out_body, 0)


def ssd_mixer(xbc, dtr, z, init, cw, cb, dtb, alog, dsk, gs, *, L, n_seq, row_off):
    blk0 = row_off // L
    has_init = init is not None
    seq = lambda w: pl.BlockSpec((L, w), lambda b: (blk0 + b, 0))
    full = lambda arr: pl.BlockSpec(arr.shape, lambda b: (0,) * arr.ndim)
    in_specs = [seq(SSD_XBC), seq(LANES), seq(D)]
    args = [xbc, dtr, z]
    if has_init:
        in_specs.append(pl.BlockSpec((1, 2, SSD_H * SSD_P, SSD_N), lambda b: (b, 0, 0, 0)))
        args.append(init)
    small = [cw, cb, dtb, alog, dsk, gs]
    in_specs += [full(a) for a in small]
    args += small
    out_specs = [pl.BlockSpec((L, D), lambda b: (b, 0))]
    out_shape = [jax.ShapeDtypeStruct((n_seq * L, D), BF16)]
    if not has_init:
        out_specs.append(pl.BlockSpec((1, 2, SSD_H * SSD_P, SSD_N), lambda b: (b, 0, 0, 0)))
        out_shape.append(jax.ShapeDtypeStruct((n_seq, 2, SSD_H * SSD_P, SSD_N), F32))
    return pl.pallas_call(
        functools.partial(_ssd_kernel, L=L, has_init=has_init),
        grid=(n_seq,),
        in_specs=in_specs,
        out_specs=out_specs,
        out_shape=out_shape,
        scratch_shapes=[
            pltpu.VMEM((L + 2 * PAD, SSD_XBC), F32),
            pltpu.VMEM((L, SSD_XBC), F32),
            pltpu.VMEM((L, D), F32),
            pltpu.VMEM((SSD_N, SSD_H * SSD_P), F32),
        ],
        compiler_params=_cparams(("arbitrary",)),
        name=f"ssd_{L}",
    )(*args)


HY_CB = 256


def dft_matrices(L):
    k = np.arange(L, dtype=np.int64)
    ang = ((k[:, None] * k[None, :]) % (2 * L)).astype(np.float64) * (math.pi / L)
    cosm = np.cos(ang)
    sinm = np.sin(ang)
    sinm[0] = np.where(k % 2 == 0, 1.0, -1.0)
    fwd = np.concatenate([cosm, sinm], axis=0).astype(np.float32)
    wts = np.where(k == 0, 1.0, 2.0) / (2 * L)
    inv = np.concatenate([cosm * wts[None, :], sinm.T * wts[None, :]], axis=1).astype(np.float32)
    fwd, inv = jnp.asarray(fwd), jnp.asarray(inv)
    f_hi = fwd.astype(BF16)
    f_lo = (fwd - f_hi.astype(F32)).astype(BF16)
    return f_hi, f_lo, inv.astype(BF16)


def _dot3(a_hi, a_lo, b):
    b_hi = b.astype(BF16)
    b_lo = (b - b_hi.astype(F32)).astype(BF16)
    return (jnp.dot(a_hi, b_hi, preferred_element_type=F32) + jnp.dot(a_lo, b_hi, preferred_element_type=F32)
            + jnp.dot(a_hi, b_lo, preferred_element_type=F32))


def _const_spec(arr):
    return pl.BlockSpec(arr.shape, lambda *_: (0,) * arr.ndim, pipeline_mode=pl.Buffered(1))


def _hy_filter_kernel(feat_ref, w1_ref, b1_ref, w2_ref, b2_ref, fr_ref, w3_ref, dl_ref, fh_ref, fl_ref, o_ref, *, L):
    hp = functools.partial(jnp.dot, precision=HIGHEST, preferred_element_type=F32)
    hdn = jnp.sin(fr_ref[0:1, :] * (hp(feat_ref[...], w1_ref[...]) + b1_ref[...]))
    hdn = jnp.sin(fr_ref[1:2, :] * (hp(hdn, w2_ref[...]) + b2_ref[...]))
    rowi = lax.broadcasted_iota(jnp.int32, (L, 1), 0)
    t = rowi.astype(F32) * (1.0 / (L - 1))
    dec = jnp.exp(-t * dl_ref[...])
    first = rowi == 0
    for o in range(2):
        fwd = hp(hdn, w3_ref[2 * o]) * dec
        bwd = jnp.where(first, 0.0, hp(hdn, w3_ref[2 * o + 1]) * dec)
        ss = _dot3(fh_ref[...], fl_ref[...], fwd + bwd)
        sd = _dot3(fh_ref[...], fl_ref[...], fwd - bwd)
        hr = ss[0:L]
        o_ref[o, 0] = hr
        o_ref[o, 1] = jnp.where(first, 0.0, sd[L:2 * L])
        o_ref[o, 2] = jnp.where(first, ss[L:L + 1], hr)


def hyena_filter_spectra(feat, w1, b1, w2, b2, freq, w3r, deltas, f_hi, f_lo, *, L):
    full = lambda arr: pl.BlockSpec(arr.shape, lambda j: (0,) * arr.ndim)
    return pl.pallas_call(
        functools.partial(_hy_filter_kernel, L=L),
        grid=(D // HY_CB,),
        in_specs=[full(feat), full(w1), full(b1), full(w2), full(b2), full(freq),
                  pl.BlockSpec((4, HY_HID, HY_CB), lambda j: (0, 0, j)),
                  pl.BlockSpec((1, HY_CB), lambda j: (0, j)),
                  _const_spec(f_hi), _const_spec(f_lo)],
        out_specs=pl.BlockSpec((2, 3, L, HY_CB), lambda j: (0, 0, 0, j)),
        out_shape=jax.ShapeDtypeStruct((2, 3, L, D), F32),
        compiler_params=_cparams(("arbitrary",)),
        name=f"hy_filter_{L}",
    )(feat, w1, b1, w2, b2, freq, w3r, deltas, f_hi, f_lo)


def _hyena_kernel(p0_ref, p1_ref, p2_ref, w0_ref, w1_ref, w2_ref, b0_ref, b1_ref, b2_ref, h_ref, hb_ref,
                  f_ref, g_ref, o_ref, xp_s, *, L):
    xp_s[0:PAD, :] = jnp.zeros((PAD, HY_CB), F32)
    xp_s[PAD + L:2 * PAD + L, :] = jnp.zeros((PAD, HY_CB), F32)

    def conv(p_ref, w_ref, b_ref):
        xp_s[PAD:PAD + L, :] = p_ref[...]
        acc = b_ref[...] + xp_s[PAD - 1:PAD - 1 + L, :] * w_ref[0:1, :]
        for k in range(1, HY_K):
            acc = acc + xp_s[PAD - 1 + k:PAD - 1 + k + L, :] * w_ref[k:k + 1, :]
        return acc

    u = conv(p0_ref, w0_ref, b0_ref)
    for o, (p_ref, w_ref, b_ref) in enumerate(((p1_ref, w1_ref, b1_ref), (p2_ref, w2_ref, b2_ref))):
        spec = jnp.dot(f_ref[...], u.astype(BF16), preferred_element_type=F32)
        ar, ai = spec[0:L], spec[L:2 * L]
        yr = (ar * h_ref[o, 0] - ai * h_ref[o, 1]).astype(BF16)
        yn = (ar * h_ref[o, 1] + ai * h_ref[o, 2]).astype(BF16)
        y = jnp.dot(g_ref[...], jnp.concatenate([yr, yn], axis=0), preferred_element_type=F32)
        u = conv(p_ref, w_ref, b_ref) * (y + u * hb_ref[o:o + 1, :])
    o_ref[...] = u.astype(o_ref.dtype)


def hyena_mixer(hy, conv_w, conv_b, spectra, hy_bias, mats, *, L, n_seq, row_off):
    blk0 = row_off // L
    nj = D // HY_CB
    part = lambda q: pl.BlockSpec((L, HY_CB), lambda j, b: (blk0 + b, q * nj + j))
    wpart = lambda q: pl.BlockSpec((HY_K, HY_CB), lambda j, b: (0, q * nj + j))
    bpart = lambda q: pl.BlockSpec((1, HY_CB), lambda j, b: (0, q * nj + j))
    return pl.pallas_call(
        functools.partial(_hyena_kernel, L=L),
        grid=(nj, n_seq),
        in_specs=[part(0), part(1), part(2), wpart(0), wpart(1), wpart(2), bpart(0), bpart(1), bpart(2),
                  pl.BlockSpec((2, 3, L, HY_CB), lambda j, b: (0, 0, 0, j)),
                  pl.BlockSpec((2, HY_CB), lambda j, b: (0, j))]
                 + [_const_spec(m) for m in mats],
        out_specs=pl.BlockSpec((L, HY_CB), lambda j, b: (b, j)),
        out_shape=jax.ShapeDtypeStruct((n_seq * L, D), BF16),
        scratch_shapes=[pltpu.VMEM((L + 2 * PAD, HY_CB), F32)],
        compiler_params=_cparams(("arbitrary", "arbitrary")),
        name=f"hyena_{L}",
    )(hy, hy, hy, conv_w, conv_w, conv_w, conv_b, conv_b, conv_b, spectra, hy_bias, *mats)


ROUTER_LANES = LANES
BIG_LANE = 1e9


ROW_GROUP = D // LANES


def _store_row_groups(ref, val):
    n = val.shape[0]
    for s in range(ROW_GROUP):
        ref[pl.ds(s, n, stride=ROW_GROUP), :] = val[:, s * LANES:(s + 1) * LANES]


def _load_row_groups(ref, n, s):
    return ref[pl.ds(s, n, stride=ROW_GROUP), :]


def _first_max_lane(v, lanef):
    m = jnp.max(v, axis=-1, keepdims=True)
    return m, jnp.min(jnp.where(v == m, lanef, BIG_LANE), axis=-1, keepdims=True)


def _out_router_kernel(*refs, n_in):
    a_refs = refs[:n_in]
    w_ref, x_ref, mod_ref, gf_ref, wr_ref, br_ref, xo_ref, h2_ref, ids_ref, wts_ref = refs[n_in:]
    acc, k0 = None, 0
    for a_ref in a_refs:
        kk = a_ref.shape[1]
        part = jnp.dot(a_ref[...], w_ref[k0:k0 + kk, :], preferred_element_type=F32)
        acc = part if acc is None else acc + part
        k0 += kk
    xn = x_ref[...] + mod_ref[0, 2:3, :] * acc
    xo_ref[...] = xn
    h2 = _modulated(xn, gf_ref, mod_ref, 3)
    _store_row_groups(h2_ref, h2)

    h_hi = h2.astype(BF16)
    h_lo = (h2 - h_hi.astype(F32)).astype(BF16)
    logits = (jnp.dot(h_hi, wr_ref[0], preferred_element_type=F32) + jnp.dot(h_lo, wr_ref[0], preferred_element_type=F32)
              + jnp.dot(h_hi, wr_ref[1], preferred_element_type=F32) + br_ref[...])
    lanef = lax.broadcasted_iota(jnp.int32, logits.shape, 1).astype(F32)
    gl = jnp.where(lanef < MOE_G, logits, NEG)
    gm, gi = _first_max_lane(gl, lanef)
    g_w = 1.0 / jnp.sum(jnp.exp(gl - gm), axis=-1, keepdims=True)
    lo = MOE_G + MOE_PG * gi
    el = jnp.where((lanef >= lo) & (lanef < lo + MOE_PG), logits, NEG)
    m1, e1 = _first_max_lane(el, lanef)
    m2, e2 = _first_max_lane(jnp.where(lanef == e1, NEG, el), lanef)
    p2 = jnp.exp(m2 - m1)
    w1 = g_w / (1.0 + p2)
    ids_ref[...] = jnp.where(lanef == 0, e1 - MOE_G, jnp.where(lanef == 1, e2 - MOE_G, 0.0)).astype(jnp.int32)
    wts_ref[...] = jnp.where(lanef == 0, w1, jnp.where(lanef == 1, w1 * p2, 0.0))


def out_proj_router(acts, w_bf, x, mod, gf, wr, br):
    tile = lambda w: pl.BlockSpec((TM, w), lambda i: (i, 0))
    full = lambda arr: pl.BlockSpec(arr.shape, lambda i: (0,) * arr.ndim)
    return pl.pallas_call(
        functools.partial(_out_router_kernel, n_in=len(acts)),
        grid=(N_TILES,),
        in_specs=[tile(a.shape[1]) for a in acts] + [full(w_bf), tile(D),
                  pl.BlockSpec((1, 6, D), lambda i: (_mod_row(i), 0, 0)), full(gf), full(wr), full(br)],
        out_specs=[tile(D), pl.BlockSpec((TM * ROW_GROUP, LANES), lambda i: (i, 0)),
                   tile(ROUTER_LANES), tile(ROUTER_LANES)],
        out_shape=[jax.ShapeDtypeStruct((T_ALL, D), F32), jax.ShapeDtypeStruct((T_ALL * ROW_GROUP, LANES), F32),
                   jax.ShapeDtypeStruct((T_ALL, ROUTER_LANES), jnp.int32),
                   jax.ShapeDtypeStruct((T_ALL, ROUTER_LANES), F32)],
        compiler_params=_cparams(("arbitrary",)),
        name="out_router",
    )(*acts, w_bf, x, mod, gf, wr, br)


N_ASSIGN = 2 * T_ALL
MOE_TILES = N_ASSIGN // TM + MOE_E
N_SLOTS = MOE_TILES * TM


def route_tables(ids):
    flat = ids[:, :2].reshape(-1)
    onehot = (flat[:, None] == jnp.arange(MOE_E, dtype=jnp.int32)[None, :]).astype(jnp.int32)
    csum = jnp.cumsum(onehot, axis=0)
    rank = jnp.sum((csum - onehot) * onehot, axis=1)
    counts = csum[-1]
    padded = (counts + TM - 1) // TM * TM
    ends = jnp.cumsum(padded)
    dest = (ends - padded)[flat] + rank
    src = jnp.zeros((N_SLOTS,), jnp.int32).at[dest].set(jnp.arange(N_ASSIGN, dtype=jnp.int32) // 2)
    starts = jnp.arange(MOE_TILES, dtype=jnp.int32) * TM
    tile_expert = jnp.minimum(jnp.sum((ends[None, :] <= starts[:, None]).astype(jnp.int32), axis=1), MOE_E - 1)
    n_used = (ends[-1] // TM).astype(jnp.int32).reshape(1)
    return dest, src, tile_expert, n_used


DMA_UNROLL = 8


def _start_group_gather(src_hbm, idx_ref, n, dst_ref, sem):
    def body(j, c):
        for u in range(DMA_UNROLL):
            r = j * DMA_UNROLL + u
            src = pl.multiple_of(idx_ref[0, 0, r] * ROW_GROUP, ROW_GROUP)
            dst = pl.multiple_of(r * ROW_GROUP, ROW_GROUP)
            pltpu.make_async_copy(src_hbm.at[pl.ds(src, ROW_GROUP), :], dst_ref.at[pl.ds(dst, ROW_GROUP), :],
                                  sem).start(priority=u % 2)
        return c

    lax.fori_loop(0, n // DMA_UNROLL, body, 0)


def _wait_group_gather(src_hbm, dst_ref, sem):
    pltpu.make_async_copy(src_hbm.at[pl.ds(0, dst_ref.shape[0]), :], dst_ref, sem).wait()


def _experts_kernel(te_ref, nu_ref, cur_ref, nxt_ref, h_hbm, wg_ref, wu_ref, wd_ref, o_ref, xbuf, xcat, sem):
    i = pl.program_id(0)
    nu = nu_ref[0]
    slot = i % 2

    @pl.when((i == 0) & (nu > 0))
    def _():
        _start_group_gather(h_hbm, cur_ref, TM, xbuf.at[0], sem.at[0])

    @pl.when(i + 1 < nu)
    def _():
        _start_group_gather(h_hbm, nxt_ref, TM, xbuf.at[1 - slot], sem.at[1 - slot])

    @pl.when(i < nu)
    def _():
        _wait_group_gather(h_hbm, xbuf.at[slot], sem.at[slot])
        for s in range(ROW_GROUP):
            xcat[:, s * LANES:(s + 1) * LANES] = _load_row_groups(xbuf.at[slot], TM, s).astype(BF16)
        x = xcat[...]
        g = jnp.dot(x, wg_ref[0, 0].astype(BF16), preferred_element_type=F32)
        u = jnp.dot(x, wu_ref[0, 0].astype(BF16), preferred_element_type=F32)
        hid = (_silu(g) * u).astype(BF16)
        _store_row_groups(o_ref, jnp.dot(hid, wd_ref[0, 0].astype(BF16), preferred_element_type=F32))

    @pl.when(i >= nu)
    def _():
        o_ref[...] = jnp.zeros(o_ref.shape, o_ref.dtype)


def grouped_experts(h2r, src, w_gate, w_up, w_down, tile_expert, n_used, layer):
    wspec = lambda a, b: pl.BlockSpec((1, 1, a, b), lambda i, te, nu: (layer, te[i], 0, 0))
    idx = lambda f: pl.BlockSpec((1, 1, TM), lambda i, te, nu: (f(i), 0, 0), memory_space=pltpu.SMEM)
    src3 = src.reshape(MOE_TILES, 1, TM)
    return pl.pallas_call(
        _experts_kernel,
        grid_spec=pltpu.PrefetchScalarGridSpec(
            num_scalar_prefetch=2,
            grid=(MOE_TILES,),
            in_specs=[idx(lambda i: i), idx(lambda i: jnp.minimum(i + 1, MOE_TILES - 1)),
                      pl.BlockSpec(memory_space=pl.ANY),
                      wspec(D, MOE_F), wspec(D, MOE_F), wspec(MOE_F, D)],
            out_specs=pl.BlockSpec((TM * ROW_GROUP, LANES), lambda i, te, nu: (i, 0)),
            scratch_shapes=[pltpu.VMEM((2, TM * ROW_GROUP, LANES), F32), pltpu.VMEM((TM, D), BF16),
                            pltpu.SemaphoreType.DMA((2,))],
        ),
        out_shape=jax.ShapeDtypeStruct((N_SLOTS * ROW_GROUP, LANES), F32),
        compiler_params=_cparams(("arbitrary",)),
        name="moe_experts",
    )(tile_expert, n_used, src3, src3, h2r, w_gate, w_up, w_down)


def _combine_kernel(cur_ref, nxt_ref, ys_hbm, x_ref, wts_ref, mod_ref, gfin_ref, o_ref, buf, sem, *, final):
    i = pl.program_id(0)
    slot = i % 2

    @pl.when(i == 0)
    def _():
        _start_group_gather(ys_hbm, cur_ref, 2 * TM, buf.at[0], sem.at[0])

    @pl.when(i + 1 < N_TILES)
    def _():
        _start_group_gather(ys_hbm, nxt_ref, 2 * TM, buf.at[1 - slot], sem.at[1 - slot])

    _wait_group_gather(ys_hbm, buf.at[slot], sem.at[slot])
    w0, w1 = wts_ref[:, 0:1], wts_ref[:, 1:2]
    for s in range(ROW_GROUP):
        cols = slice(s * LANES, (s + 1) * LANES)
        y0 = buf[slot, pl.ds(s, TM, stride=2 * ROW_GROUP), :]
        y1 = buf[slot, pl.ds(ROW_GROUP + s, TM, stride=2 * ROW_GROUP), :]
        o_ref[:, cols] = x_ref[:, cols] + mod_ref[0, 5:6, cols] * (w0 * y0 + w1 * y1)
    if final:
        o_ref[...] = _rms(o_ref[...]) * gfin_ref[...]


def moe_combine(ys, dest, x, wts, mod, gfin, *, final):
    tile = lambda w: pl.BlockSpec((TM, w), lambda i: (i, 0))
    idx = lambda f: pl.BlockSpec((1, 1, 2 * TM), lambda i: (f(i), 0, 0), memory_space=pltpu.SMEM)
    dest3 = dest.reshape(N_TILES, 1, 2 * TM)
    return pl.pallas_call(
        functools.partial(_combine_kernel, final=final),
        grid=(N_TILES,),
        in_specs=[idx(lambda i: i), idx(lambda i: jnp.minimum(i + 1, N_TILES - 1)),
                  pl.BlockSpec(memory_space=pl.ANY), tile(D), tile(ROUTER_LANES),
                  pl.BlockSpec((1, 6, D), lambda i: (_mod_row(i), 0, 0)),
                  pl.BlockSpec((1, D), lambda i: (0, 0))],
        out_specs=tile(D),
        out_shape=jax.ShapeDtypeStruct((T_ALL, D), F32),
        scratch_shapes=[pltpu.VMEM((2, 2 * TM * ROW_GROUP, LANES), F32), pltpu.SemaphoreType.DMA((2,))],
        compiler_params=_cparams(("arbitrary",)),
        name="moe_combine",
    )(dest3, dest3, ys, x, wts, mod, gfin)


ODD_COLS = 2048
ROPE_Q = MLA_H * MLA_ROPE
ROPE_SHIFT = ROPE_F


def rope_tables():
    t = np.arange(L_LAT)
    pos = np.stack([t // GRID_W, t % GRID_W], axis=1).astype(np.float64)
    inv = 10000.0 ** (-np.arange(ROPE_F, dtype=np.float64) / ROPE_F)
    lane = np.arange(ROPE_Q) % MLA_ROPE
    axis = lane // (2 * ROPE_F)
    first = (lane % (2 * ROPE_F)) < ROPE_F
    ang = pos[:, axis] * inv[lane % ROPE_F][None, :]
    cos, sin = np.cos(ang), np.sin(ang)
    tabs = [cos, np.where(first[None, :], -sin, 0.0), np.where(first[None, :], 0.0, sin)]
    ident = [np.ones((1, TM, ROPE_Q)), np.zeros((1, TM, ROPE_Q)), np.zeros((1, TM, ROPE_Q))]
    return [jnp.asarray(np.concatenate([i, tb.reshape(LAT_TILES_PER_SEQ, TM, ROPE_Q)], axis=0).astype(np.float32))
            for i, tb in zip(ident, tabs)]


def _rope(x, c, a, b):
    n = x.shape[1]
    return x * c[:, :n] + pltpu.roll(x, n - ROPE_SHIFT, 1) * a[:, :n] + pltpu.roll(x, ROPE_SHIFT, 1) * b[:, :n]


def _odd_in_kernel(x_ref, mod_ref, g_ref, w_ref, gq_ref, wuq_ref, gkv_ref, wukv_ref, rc_ref, ra_ref, rb_ref,
                   qkv_ref, qm_ref, ckv_ref, kvu_ref, kr_ref):
    hb = _modulated(x_ref[...], g_ref, mod_ref, 0).astype(BF16)
    for c0 in range(0, 3 * NA_W, PROJ_CHUNK):
        qkv_ref[:, c0:c0 + PROJ_CHUNK] = jnp.dot(hb, w_ref[:, c0:c0 + PROJ_CHUNK], preferred_element_type=F32)
    rest = jnp.dot(hb, w_ref[:, 3 * NA_W:ODD_COLS], preferred_element_type=F32)
    rc, ra, rb = rc_ref[0], ra_ref[0], rb_ref[0]
    qd = (_rms(rest[:, 0:MLA_QR]) * gq_ref[...]).astype(BF16)
    qm = jnp.dot(qd, wuq_ref[...], preferred_element_type=F32)
    qm_ref[:, 0:MLA_H * MLA_NOPE] = qm[:, 0:MLA_H * MLA_NOPE]
    qm_ref[:, MLA_H * MLA_NOPE:] = _rope(qm[:, MLA_H * MLA_NOPE:], rc, ra, rb)
    ckv = _rms(rest[:, MLA_QR:MLA_QR + MLA_KVR]) * gkv_ref[...]
    ckv_ref[...] = ckv
    kvu_ref[...] = jnp.dot(ckv.astype(BF16), wukv_ref[...], preferred_element_type=F32)
    kr_ref[...] = _rope(rest[:, MLA_QR + MLA_KVR:], rc, ra, rb)


def odd_in_proj(x, mod, g, w_bf, gq, wuq_bf, gkv, wukv_bf, tabs):
    tile = lambda w: pl.BlockSpec((TM, w), lambda i: (i, 0))
    full = lambda arr: pl.BlockSpec(arr.shape, lambda i: (0,) * arr.ndim)
    tab = pl.BlockSpec((1, TM, ROPE_Q),
                       lambda i: (jnp.where(i < CTX_TILES, 0, 1 + (i - CTX_TILES) % LAT_TILES_PER_SEQ), 0, 0))
    widths = (3 * NA_W, MLA_H * MLA_QK, MLA_KVR, MLA_H * (MLA_NOPE + MLA_V), LANES)
    return pl.pallas_call(
        _odd_in_kernel,
        grid=(N_TILES,),
        in_specs=[tile(D), pl.BlockSpec((1, 6, D), lambda i: (_mod_row(i), 0, 0)), full(g), full(w_bf),
                  full(gq), full(wuq_bf), full(gkv), full(wukv_bf), tab, tab, tab],
        out_specs=[tile(w) for w in widths],
        out_shape=[jax.ShapeDtypeStruct((T_ALL, w), F32) for w in widths],
        compiler_params=_cparams(("arbitrary",)),
        name="odd_in",
    )(x, mod, g, w_bf, gq, wuq_bf, gkv, wukv_bf, *tabs)


LOG2E = math.log2(math.e)
NA_QSCALE = NA_D ** -0.5 * LOG2E
MLA_QSCALE = MLA_QK ** -0.5 * LOG2E
NT = (((1,), (1,)), ((), ()))


def _softmax_pv(scores, values):
    m = functools.reduce(jnp.maximum, [jnp.max(s, axis=-1, keepdims=True) for s in scores])
    ps = [jnp.exp2(s - m) for s in scores]
    den = functools.reduce(jnp.add, [jnp.sum(p, axis=-1, keepdims=True) for p in ps])
    acc = functools.reduce(jnp.add, [jnp.dot(p.astype(BF16), v, preferred_element_type=F32) for p, v in zip(ps, values)])
    return acc / den


def _head(ref_or_val, h, width, base=0):
    return ref_or_val[:, base + h * width:base + (h + 1) * width]


def _mla_scores(qm, h, kn, kr):
    qn = (_head(qm, h, MLA_NOPE) * MLA_QSCALE).astype(BF16)
    qr = (_head(qm, h, MLA_ROPE, MLA_H * MLA_NOPE) * MLA_QSCALE).astype(BF16)
    return (lax.dot_general(qn, kn, NT, preferred_element_type=F32)
            + lax.dot_general(qr, kr, NT, preferred_element_type=F32))


def _attn_ctx_kernel(qkv_ref, qm_ref, kvu_ref, kr_ref, ona_ref, omla_ref):
    kr = kr_ref[:, 0:MLA_ROPE].astype(BF16)
    for h in range(NA_H):
        q = (_head(qkv_ref, h, NA_D) * NA_QSCALE).astype(BF16)
        k = _head(qkv_ref, h, NA_D, NA_W).astype(BF16)
        v = _head(qkv_ref, h, NA_D, 2 * NA_W).astype(BF16)
        s = lax.dot_general(q, k, NT, preferred_element_type=F32)
        ona_ref[:, h * NA_D:(h + 1) * NA_D] = _softmax_pv([s], [v]).astype(ona_ref.dtype)
    for h in range(MLA_H):
        kn = _head(kvu_ref, h, MLA_NOPE).astype(BF16)
        v = _head(kvu_ref, h, MLA_V, MLA_H * MLA_NOPE).astype(BF16)
        s = _mla_scores(qm_ref, h, kn, kr)
        omla_ref[:, h * MLA_V:(h + 1) * MLA_V] = _softmax_pv([s], [v]).astype(omla_ref.dtype)


def attn_context(qkv, qm, kvu, kr):
    seq = lambda w: pl.BlockSpec((L_CTX, w), lambda b: (b, 0))
    return pl.pallas_call(
        _attn_ctx_kernel,
        grid=(N_CTX,),
        in_specs=[seq(3 * NA_W), seq(MLA_H * MLA_QK), seq(MLA_H * (MLA_NOPE + MLA_V)), seq(LANES)],
        out_specs=[seq(NA_W), seq(MLA_H * MLA_V)],
        out_shape=[jax.ShapeDtypeStruct((T_CTX, NA_W), BF16), jax.ShapeDtypeStruct((T_CTX, MLA_H * MLA_V), BF16)],
        compiler_params=_cparams(("arbitrary",)),
        name="attn_ctx",
    )(qkv, qm, kvu, kr)


def neighbourhood_bias(rel_bias):
    rows = L_LAT // GRID_W
    r = np.arange(rows)
    r0 = np.clip(r - NA_WIN_R // 2, 0, rows - NA_WIN_R)
    c = np.arange(GRID_W)
    c0 = np.clip(c - NA_WIN_C // 2, 0, GRID_W - NA_WIN_C)
    row_ok = (r[None, :] >= r0[:, None]) & (r[None, :] < r0[:, None] + NA_WIN_R)
    col_ok = (c[None, :] >= c0[:, None]) & (c[None, :] < c0[:, None] + NA_WIN_C)
    dr = np.clip(r[None, :] - r[:, None] + NA_WIN_R - 1, 0, 2 * NA_WIN_R - 2)
    dc = np.clip(c[None, :] - c[:, None], -(NA_WIN_C - 1), NA_WIN_C - 1) + NA_WIN_C - 1
    sel_r = (dr[:, :, None] == np.arange(2 * NA_WIN_R - 1)) & row_ok[:, :, None]
    sel_c = (dc[:, :, None] == np.arange(2 * NA_WIN_C - 1)) & col_ok[:, :, None]
    t = jnp.einsum("hdj,qcj->hdqc", rel_bias.astype(F32), jnp.asarray(sel_c.astype(np.float32)), precision=HIGHEST)
    b = jnp.einsum("rkd,hdqc->hrqkc", jnp.asarray(sel_r.astype(np.float32)), t, precision=HIGHEST)
    ok = jnp.asarray(row_ok[:, None, :, None] & col_ok[None, :, None, :])
    return jnp.where(ok[None], b * LOG2E, NEG).reshape(NA_H, L_LAT, L_LAT)


def _na_lat_kernel(q_ref, k_ref, v_ref, kc_ref, vc_ref, b_ref, o_ref):
    for h in range(NA_H):
        q = (_head(q_ref, h, NA_D) * NA_QSCALE).astype(BF16)
        k = _head(k_ref, h, NA_D).astype(BF16)
        v = _head(v_ref, h, NA_D).astype(BF16)
        kc = kc_ref[0, 0, h].astype(BF16)
        vc = vc_ref[0, 0, h].astype(BF16)
        s1 = lax.dot_general(q, k, NT, preferred_element_type=F32) + b_ref[h]
        s2 = lax.dot_general(q, kc, NT, preferred_element_type=F32)
        o_ref[:, h * NA_D:(h + 1) * NA_D] = _softmax_pv([s1, s2], [v, vc]).astype(o_ref.dtype)


def attn_neighbourhood_latent(qkv, cache_k, cache_v, bias):
    nq = L_LAT // TM
    t0 = T_CTX // TM
    s0 = T_CTX // L_LAT
    cache = pl.BlockSpec((1, 1, NA_H, PAST, NA_D), lambda qt, b: (b, 0, 0, 0, 0))
    return pl.pallas_call(
        _na_lat_kernel,
        grid=(nq, N_LAT),
        in_specs=[pl.BlockSpec((TM, NA_W), lambda qt, b: (t0 + b * nq + qt, 0)),
                  pl.BlockSpec((L_LAT, NA_W), lambda qt, b: (s0 + b, 1)),
                  pl.BlockSpec((L_LAT, NA_W), lambda qt, b: (s0 + b, 2)),
                  cache, cache,
                  pl.BlockSpec((NA_H, TM, L_LAT), lambda qt, b: (0, qt, 0))],
        out_specs=pl.BlockSpec((TM, NA_W), lambda qt, b: (b * nq + qt, 0)),
        out_shape=jax.ShapeDtypeStruct((T_LAT, NA_W), BF16),
        compiler_params=_cparams(("arbitrary", "arbitrary")),
        name="attn_na_lat",
    )(qkv, qkv, qkv, cache_k, cache_v, bias)


def _mla_lat_kernel(qm_ref, kvu_ref, kr_ref, ckv_ref, krc_ref, wukv_ref, o_ref):
    kvc = jnp.dot(ckv_ref[0, 0].astype(BF16), wukv_ref[...], preferred_element_type=F32)
    kr = kr_ref[:, 0:MLA_ROPE].astype(BF16)
    krc = krc_ref[0, 0].astype(BF16)
    for h in range(MLA_H):
        kn = _head(kvu_ref, h, MLA_NOPE).astype(BF16)
        v = _head(kvu_ref, h, MLA_V, MLA_H * MLA_NOPE).astype(BF16)
        knc = _head(kvc, h, MLA_NOPE).astype(BF16)
        vc = _head(kvc, h, MLA_V, MLA_H * MLA_NOPE).astype(BF16)
        s1 = _mla_scores(qm_ref, h, kn, kr)
        s2 = _mla_scores(qm_ref, h, knc, krc)
        o_ref[:, h * MLA_V:(h + 1) * MLA_V] = _softmax_pv([s1, s2], [v, vc]).astype(o_ref.dtype)


def attn_mla_latent(qm, kvu, kr, cache_ckv, cache_krope, wukv_bf):
    nq = L_LAT // TM
    t0 = T_CTX // TM
    s0 = T_CTX // L_LAT
    return pl.pallas_call(
        _mla_lat_kernel,
        grid=(nq, N_LAT),
        in_specs=[pl.BlockSpec((TM, MLA_H * MLA_QK), lambda qt, b: (t0 + b * nq + qt, 0)),
                  pl.BlockSpec((L_LAT, MLA_H * (MLA_NOPE + MLA_V)), lambda qt, b: (s0 + b, 0)),
                  pl.BlockSpec((L_LAT, LANES), lambda qt, b: (s0 + b, 0)),
                  pl.BlockSpec((1, 1, PAST, MLA_KVR), lambda qt, b: (b, 0, 0, 0)),
                  pl.BlockSpec((1, 1, PAST, MLA_ROPE), lambda qt, b: (b, 0, 0, 0)),
                  pl.BlockSpec(wukv_bf.shape, lambda qt, b: (0, 0))],
        out_specs=pl.BlockSpec((TM, MLA_H * MLA_V), lambda qt, b: (b * nq + qt, 0)),
        out_shape=jax.ShapeDtypeStruct((T_LAT, MLA_H * MLA_V), BF16),
        compiler_params=_cparams(("arbitrary", "arbitrary")),
        name="attn_mla_lat",
    )(qm, kvu, kr, cache_ckv, cache_krope, wukv_bf)


def moe_block(h2, ids, wts, x, mod, gfin, w_gate, w_up, w_down, layer, *, final):
    dest, src, tile_expert, n_used = route_tables(ids)
    ys = grouped_experts(h2, src, w_gate, w_up, w_down, tile_expert, n_used, layer)
    return moe_combine(ys, dest, x, wts, mod, gfin, final=final)


def _pad_lanes(a):
    return jnp.pad(a, ((0, 0), (0, LANES - a.shape[1])))


def _hyena_features(L):
    t = np.linspace(0.0, 1.0, L)[:, None]
    w = 2.0 * math.pi * np.arange(L) / L
    bands = np.linspace(1e-4, HY_BANDS - 1, HY_BANDS)
    ang = w[:, None] * bands[None]
    feat = np.concatenate([t, np.cos(ang), -np.sin(ang)], axis=-1)
    return jnp.asarray(np.pad(feat, ((0, 0), (0, LANES - HY_FEAT))).astype(np.float32))


def _router_params(w_gr, b_gr, w_er, b_er):
    wr = _pad_lanes(jnp.concatenate([w_gr, w_er], axis=1))
    br = _pad_lanes(jnp.concatenate([b_gr, b_er])[None])
    wr_hi = wr.astype(BF16)
    wr_lo = (wr - wr_hi.astype(F32)).astype(BF16)
    return jnp.stack([wr_hi, wr_lo]), br


def _even_layer(x, mod, g_mix, state, w_in, conv_w, conv_b, a_log, dt_bias, d_skip, g_ssd, hy_conv_w, hy_conv_b,
                hy_w1, hy_b1, hy_w2, hy_b2, hy_w3, hy_freq, hy_bias):
    n0 = D + SSD_XBC
    w_bf = jnp.concatenate([w_in[:, :n0], w_in[:, n0 + SSD_H:], w_in[:, n0:n0 + SSD_H],
                            jnp.zeros((D, LANES - SSD_H), F32)], axis=1).astype(BF16)
    z, xbc, hy, dtr = even_in_proj(x, mod, g_mix, w_bf)
    small = (conv_w, conv_b[None], _pad_lanes(dt_bias), _pad_lanes(a_log), jnp.repeat(d_skip, SSD_P)[None], g_ssd[None])
    y_c, fin = ssd_mixer(xbc, dtr, z, None, *small, L=L_CTX, n_seq=N_CTX, row_off=0)
    (y_l,) = ssd_mixer(xbc, dtr, z, state.reshape(N_LAT, 2, SSD_H * SSD_P, SSD_N), *small,
                       L=L_LAT, n_seq=N_LAT, row_off=T_CTX)
    w1 = jnp.pad(hy_w1, ((0, LANES - HY_FEAT), (0, 0)))
    w3r = hy_w3.reshape(HY_HID, 4, D).transpose(1, 0, 2)
    deltas = jnp.asarray(np.linspace(HY_MIN_DECAY, HY_MAX_DECAY, D).astype(np.float32))[None]
    us = []
    for L, n_seq, off in ((L_CTX, N_CTX, 0), (L_LAT, N_LAT, T_CTX)):
        f_hi, f_lo, g_hi = dft_matrices(L)
        spectra = hyena_filter_spectra(_hyena_features(L), w1, hy_b1[None], hy_w2, hy_b2[None], hy_freq, w3r, deltas,
                                       f_hi, f_lo, L=L)
        us.append(hyena_mixer(hy, hy_conv_w, hy_conv_b[None], spectra, hy_bias, (f_hi, g_hi),
                              L=L, n_seq=n_seq, row_off=off))
    return jnp.concatenate([y_c, y_l], axis=0), jnp.concatenate(us, axis=0), fin


def _odd_layer(x, mod, g_mix, cache_k, cache_v, cache_ckv, cache_kr, rel_bias, w_in, g_q, w_uq, g_kv, w_ukv):
    w_bf = jnp.pad(w_in, ((0, 0), (0, ODD_COLS - w_in.shape[1]))).astype(BF16)
    wuq = w_uq.reshape(MLA_QR, MLA_H, MLA_QK)
    wuq_bf = jnp.concatenate([wuq[:, :, :MLA_NOPE].reshape(MLA_QR, -1), wuq[:, :, MLA_NOPE:].reshape(MLA_QR, -1)],
                             axis=1).astype(BF16)
    wukv = w_ukv.reshape(MLA_KVR, MLA_H, MLA_NOPE + MLA_V)
    wukv_bf = jnp.concatenate([wukv[:, :, :MLA_NOPE].reshape(MLA_KVR, -1), wukv[:, :, MLA_NOPE:].reshape(MLA_KVR, -1)],
                              axis=1).astype(BF16)
    qkv, qm, ckv, kvu, kr = odd_in_proj(x, mod, g_mix, w_bf, g_q[None], wuq_bf, g_kv[None], wukv_bf, rope_tables())
    ona_c, omla_c = attn_context(qkv, qm, kvu, kr)
    ona_l = attn_neighbourhood_latent(qkv, cache_k, cache_v, neighbourhood_bias(rel_bias))
    omla_l = attn_mla_latent(qm, kvu, kr, cache_ckv, cache_kr, wukv_bf)
    return jnp.concatenate([ona_c, ona_l], axis=0), jnp.concatenate([omla_c, omla_l], axis=0), qkv, ckv, kr


def kernel(x_prompt, x_sample, state_ssd, cache_na_k, cache_na_v, cache_mla_ckv, cache_mla_krope, c, c_ctx, w_ada, b_ada, norm_mix, norm_ffn, norm_final, ev_w_in, ev_conv_w, ev_conv_b, ssd_A_log, ssd_dt_bias, ssd_d, ssd_norm, hy_conv_w, hy_conv_b, hy_w1, hy_b1, hy_w2, hy_b2, hy_w3, hy_freq, hy_bias, ev_w_out, od_w_in, mla_q_norm, mla_w_uq, mla_kv_norm, mla_w_ukv, na_rel_bias, od_w_out, moe_w_gr, moe_b_gr, moe_w_er, moe_b_er, moe_w_gate, moe_w_up, moe_w_down):
    x = jnp.concatenate([x_prompt.reshape(T_CTX, D), x_sample.reshape(T_LAT, D)], axis=0)
    cvec = jnp.zeros((MOD_ROWS, D), F32).at[0].set(c_ctx).at[1:1 + N_LAT].set(c)
    mod = ada_modulation(cvec, w_ada, b_ada)
    gfin = norm_final[None]

    y, u, fin = _even_layer(x, mod[0], norm_mix[0][None], state_ssd[:, 0], ev_w_in[0], ev_conv_w[0], ev_conv_b[0],
                            ssd_A_log[0], ssd_dt_bias[0], ssd_d[0], ssd_norm[0], hy_conv_w[0], hy_conv_b[0],
                            hy_w1[0], hy_b1[0], hy_w2[0], hy_b2[0], hy_w3[0], hy_freq[0], hy_bias[0])
    wr, br = _router_params(moe_w_gr[0], moe_b_gr[0], moe_w_er[0], moe_b_er[0])
    xn, h2, ids, wts = out_proj_router([y, u], ev_w_out[0].astype(BF16), x, mod[0], norm_ffn[0][None], wr, br)
    x = moe_block(h2, ids, wts, xn, mod[0], gfin, moe_w_gate, moe_w_up, moe_w_down, 0, final=False)

    o_na, o_mla, qkv, ckv, kr = _odd_layer(x, mod[1], norm_mix[1][None], cache_na_k, cache_na_v, cache_mla_ckv,
                                           cache_mla_krope, na_rel_bias[0], od_w_in[0], mla_q_norm[0], mla_w_uq[0],
                                           mla_kv_norm[0], mla_w_ukv[0])
    wr, br = _router_params(moe_w_gr[1], moe_b_gr[1], moe_w_er[1], moe_b_er[1])
    xn, h2, ids, wts = out_proj_router([o_na, o_mla], od_w_out[0].astype(BF16), x, mod[1], norm_ffn[1][None], wr, br)
    out = moe_block(h2, ids, wts, xn, mod[1], gfin, moe_w_gate, moe_w_up, moe_w_down, 1, final=True)

    heads = lambda a: a.reshape(N_CTX, L_CTX, NA_H, NA_D).transpose(0, 2, 1, 3)[:, None]
    return (out[:T_CTX].reshape(N_CTX, L_CTX, D),
            out[T_CTX:].reshape(N_LAT, L_LAT, D),
            fin.reshape(N_CTX, 1, 2, SSD_H, SSD_P, SSD_N),
            heads(qkv[:T_CTX, NA_W:2 * NA_W]),
            heads(qkv[:T_CTX, 2 * NA_W:3 * NA_W]),
            ckv[:T_CTX].reshape(N_CTX, 1, L_CTX, MLA_KVR),
            kr[:T_CTX, :MLA_ROPE].reshape(N_CTX, 1, L_CTX, MLA_ROPE))
```

```python
import functools
import math

import numpy as np
import jax
import jax.numpy as jnp
from jax import lax
from jax.experimental import pallas as pl
from jax.experimental.pallas import tpu as pltpu

F32 = jnp.float32
BF16 = jnp.bfloat16
HIGHEST = lax.Precision.HIGHEST

D = 1024
N_CTX, L_CTX = 16, 256
N_LAT, L_LAT = 8, 1024
T_CTX = N_CTX * L_CTX
T_LAT = N_LAT * L_LAT
T_ALL = T_CTX + T_LAT
PAST = 512
GRID_W = 64
EPS = 1e-6
NEG = -1e30

SSD_H, SSD_P, SSD_N, SSD_G = 16, 64, 128, 2
SSD_XBC = D + 2 * SSD_G * SSD_N
SSD_K = 5
CHUNK = 128

HY_K = 3
HY_BANDS = 16
HY_FEAT = 1 + 2 * HY_BANDS
HY_HID = 64
HY_MIN_DECAY = abs(math.log(1e-2) / 1.5)
HY_MAX_DECAY = abs(math.log(1e-2) / 0.3)

NA_H, NA_D = 8, 64
NA_W = NA_H * NA_D
NA_WIN_R, NA_WIN_C = 8, 16
MLA_H, MLA_QR, MLA_KVR = 8, 256, 128
MLA_NOPE, MLA_ROPE, MLA_V = 64, 32, 64
MLA_QK = MLA_NOPE + MLA_ROPE
ROPE_F = MLA_ROPE // 4

MOE_G, MOE_PG, MOE_E, MOE_F = 4, 8, 32, 256

LANES = 128
SUBLANES = 8
VMEM_LIMIT = 56 * 1024 * 1024

TM = 256
N_TILES = T_ALL // TM
CTX_TILES = T_CTX // TM
LAT_TILES_PER_SEQ = L_LAT // TM
MOD_ROWS = 16


def _cparams(sem):
    return pltpu.CompilerParams(dimension_semantics=sem, vmem_limit_bytes=VMEM_LIMIT)


def _mod_row(i):
    return jnp.where(i < CTX_TILES, 0, 1 + (i - CTX_TILES) // LAT_TILES_PER_SEQ)


def _silu(x):
    return x * jax.nn.sigmoid(x)


def _rms(x):
    return x * lax.rsqrt(jnp.mean(x * x, axis=-1, keepdims=True) + EPS)


def _ada_kernel(c_ref, w_ref, b_ref, o_ref):
    c = c_ref[...]
    o_ref[0] = jnp.dot(_silu(c), w_ref[0], precision=HIGHEST, preferred_element_type=F32) + b_ref[0]


def ada_modulation(cvec, w_ada, b_ada):
    depth = w_ada.shape[0]
    out = pl.pallas_call(
        _ada_kernel,
        grid=(depth, 6),
        in_specs=[
            pl.BlockSpec((MOD_ROWS, D), lambda l, j: (0, 0)),
            pl.BlockSpec((1, D, D), lambda l, j: (l, 0, j)),
            pl.BlockSpec((1, 1, D), lambda l, j: (l, 0, j)),
        ],
        out_specs=pl.BlockSpec((1, MOD_ROWS, D), lambda l, j: (l, 0, j)),
        out_shape=jax.ShapeDtypeStruct((depth, MOD_ROWS, 6 * D), F32),
        compiler_params=_cparams(("arbitrary", "arbitrary")),
        name="ada",
    )(cvec, w_ada, b_ada.reshape(depth, 1, 6 * D))
    return out.reshape(depth, MOD_ROWS, 6, D)


PROJ_CHUNK = 512


def _modulated(x, g_ref, mod_ref, shift_row):
    h = _rms(x) * g_ref[...]
    return h * (1.0 + mod_ref[0, shift_row + 1:shift_row + 2, :]) + mod_ref[0, shift_row:shift_row + 1, :]


def _even_in_kernel(x_ref, mod_ref, g_ref, w_ref, z_ref, xbc_ref, hy_ref, dt_ref):
    hb = _modulated(x_ref[...], g_ref, mod_ref, 0).astype(BF16)
    col = 0
    for o_ref in (z_ref, xbc_ref, hy_ref, dt_ref):
        width = o_ref.shape[1]
        for c0 in range(0, width, PROJ_CHUNK):
            c1 = min(c0 + PROJ_CHUNK, width)
            o_ref[:, c0:c1] = jnp.dot(hb, w_ref[:, col + c0:col + c1], preferred_element_type=F32)
        col += width


def even_in_proj(x, mod, g, w_bf):
    widths = (D, SSD_XBC, 3 * D, LANES)
    return pl.pallas_call(
        _even_in_kernel,
        grid=(N_TILES,),
        in_specs=[
            pl.BlockSpec((TM, D), lambda i: (i, 0)),
            pl.BlockSpec((1, 6, D), lambda i: (_mod_row(i), 0, 0)),
            pl.BlockSpec((1, D), lambda i: (0, 0)),
            pl.BlockSpec(w_bf.shape, lambda i: (0, 0)),
        ],
        out_specs=[pl.BlockSpec((TM, w), lambda i: (i, 0)) for w in widths],
        out_shape=[jax.ShapeDtypeStruct((T_ALL, w), F32) for w in widths],
        compiler_params=_cparams(("arbitrary",)),
        name="even_in",
    )(x, mod, g, w_bf)


PAD = SUBLANES


def _ssd_kernel(*refs, L, has_init):
    if has_init:
        (xbc_ref, dt_ref, z_ref, init_ref, cw_ref, cb_ref, dtb_ref, alog_ref, dsk_ref, gs_ref,
         y_ref, xp_s, xc_s, ya_s, st_s) = refs
        fin_ref = None
    else:
        (xbc_ref, dt_ref, z_ref, cw_ref, cb_ref, dtb_ref, alog_ref, dsk_ref, gs_ref,
         y_ref, fin_ref, xp_s, xc_s, ya_s, st_s) = refs
        init_ref = None
    nc = L // CHUNK
    half = SSD_K // 2

    xp_s[0:PAD, :] = jnp.zeros((PAD, SSD_XBC), F32)
    xp_s[PAD + L:2 * PAD + L, :] = jnp.zeros((PAD, SSD_XBC), F32)
    xp_s[PAD:PAD + L, :] = xbc_ref[...]
    for c in range(nc):
        base = PAD + c * CHUNK - half
        for j in range(SSD_XBC // LANES):
            cols = slice(j * LANES, (j + 1) * LANES)
            acc = cb_ref[:, cols] + xp_s[base:base + CHUNK, cols] * cw_ref[0:1, cols]
            for k in range(1, SSD_K):
                acc = acc + xp_s[base + k:base + k + CHUNK, cols] * cw_ref[k:k + 1, cols]
            xc_s[c * CHUNK:(c + 1) * CHUNK, cols] = _silu(acc)

    row = lax.broadcasted_iota(jnp.int32, (CHUNK, CHUNK), 0)
    colm = lax.broadcasted_iota(jnp.int32, (CHUNK, CHUNK), 1)
    lane_lo = colm < SSD_P
    tri_lo = (colm <= row).astype(F32)
    tri_up = (colm >= row).astype(F32)

    for d in range(2):
        causal = (colm <= row) if d == 0 else (colm >= row)
        for j in range(SSD_H * SSD_P // CHUNK):
            if has_init:
                st_s[:, j * CHUNK:(j + 1) * CHUNK] = init_ref[0, d, j * CHUNK:(j + 1) * CHUNK, :].T
            else:
                st_s[:, j * CHUNK:(j + 1) * CHUNK] = jnp.zeros((CHUNK, CHUNK), F32)

        def chunk_body(ci, carry, d=d, causal=causal):
            c = ci if d == 0 else nc - 1 - ci
            r0 = pl.multiple_of(c * CHUNK, CHUNK)
            dt = jax.nn.softplus(dt_ref[pl.ds(r0, CHUNK), :] + dtb_ref[d:d + 1, :])
            a = dt * (-jnp.exp(alog_ref[d:d + 1, :]))
            tri = tri_lo if d == 0 else tri_up
            cs = jnp.dot(tri, a, precision=HIGHEST, preferred_element_type=F32)
            cs_t = jnp.dot(a.T, tri.T, precision=HIGHEST, preferred_element_type=F32)
            edge = cs[CHUNK - 1:CHUNK, :] if d == 0 else cs[0:1, :]
            ecs = jnp.exp(cs)
            dec = jnp.exp(edge - cs)
            cdec = jnp.exp(edge)
            for g in range(SSD_G):
                bm = xc_s[pl.ds(r0, CHUNK), D + g * SSD_N:D + (g + 1) * SSD_N]
                cm = xc_s[pl.ds(r0, CHUNK), D + (SSD_G + g) * SSD_N:D + (SSD_G + g + 1) * SSD_N]
                bm_b, cm_b = bm.astype(BF16), cm.astype(BF16)
                cb = lax.dot_general(cm_b, bm_b, (((1,), (1,)), ((), ())), preferred_element_type=F32)
                bm_t = bm.T.astype(BF16)
                pairs = SSD_H // SSD_G // 2
                for pp in range(pairs):
                    p = g * pairs + pp
                    h0, h1 = 2 * p, 2 * p + 1
                    cols = slice(p * CHUNK, (p + 1) * CHUNK)
                    xs = xc_s[pl.ds(r0, CHUNK), cols]
                    xdt = xs * jnp.where(lane_lo, dt[:, h0:h0 + 1], dt[:, h1:h1 + 1])
                    ms = []
                    for h in (h0, h1):
                        diff = cs[:, h:h + 1] - cs_t[h:h + 1, :]
                        ms.append(cb * jnp.exp(jnp.where(causal, diff, NEG)))
                    mcat = jnp.concatenate(ms, axis=1).astype(BF16)
                    xbd = jnp.concatenate([jnp.where(lane_lo, xdt, 0.0), jnp.where(lane_lo, 0.0, xdt)],
                                          axis=0).astype(BF16)
                    y_diag = jnp.dot(mcat, xbd, preferred_element_type=F32)
                    st = st_s[:, cols]
                    y_off = jnp.dot(cm_b, st.astype(BF16), preferred_element_type=F32)
                    y_off = y_off * jnp.where(lane_lo, ecs[:, h0:h0 + 1], ecs[:, h1:h1 + 1])
                    y = y_diag + y_off
                    if d == 0:
                        ya_s[pl.ds(r0, CHUNK), cols] = y
                    else:
                        ya_s[pl.ds(r0, CHUNK), cols] = ya_s[pl.ds(r0, CHUNK), cols] + y
                    xdd = (xdt * jnp.where(lane_lo, dec[:, h0:h0 + 1], dec[:, h1:h1 + 1])).astype(BF16)
                    snew = jnp.dot(bm_t, xdd, preferred_element_type=F32)
                    st_s[:, cols] = st * jnp.where(lane_lo[0:1, :], cdec[:, h0:h0 + 1], cdec[:, h1:h1 + 1]) + snew
            return carry

        lax.fori_loop(0, nc, chunk_body, 0)
        if fin_ref is not None:
            for j in range(SSD_H * SSD_P // CHUNK):
                fin_ref[0, d, j * CHUNK:(j + 1) * CHUNK, :] = st_s[:, j * CHUNK:(j + 1) * CHUNK].T

    def out_body(c, carry):
        r0 = pl.multiple_of(c * CHUNK, CHUNK)
        y = ya_s[pl.ds(r0, CHUNK), :] + xc_s[pl.ds(r0, CHUNK), 0:D] * dsk_ref[...]
        y = y * _silu(z_ref[pl.ds(r0, CHUNK), :])
        y_ref[pl.ds(r0, CHUNK), :] = (_rms(y) * gs_ref[...]).astype(y_ref.dtype)
        return carry

    lax.fori_loop(0, nc, out_body, 0)


def ssd_mixer(xbc, dtr, z, init, cw, cb, dtb, alog, dsk, gs, *, L, n_seq, row_off):
    blk0 = row_off // L
    has_init = init is not None
    seq = lambda w: pl.BlockSpec((L, w), lambda b: (blk0 + b, 0))
    full = lambda arr: pl.BlockSpec(arr.shape, lambda b: (0,) * arr.ndim)
    in_specs = [seq(SSD_XBC), seq(LANES), seq(D)]
    args = [xbc, dtr, z]
    if has_init:
        in_specs.append(pl.BlockSpec((1, 2, SSD_H * SSD_P, SSD_N), lambda b: (b, 0, 0, 0)))
        args.append(init)
    small = [cw, cb, dtb, alog, dsk, gs]
    in_specs += [full(a) for a in small]
    args += small
    out_specs = [pl.BlockSpec((L, D), lambda b: (b, 0))]
    out_shape = [jax.ShapeDtypeStruct((n_seq * L, D), BF16)]
    if not has_init:
        out_specs.append(pl.BlockSpec((1, 2, SSD_H * SSD_P, SSD_N), lambda b: (b, 0, 0, 0)))
        out_shape.append(jax.ShapeDtypeStruct((n_seq, 2, SSD_H * SSD_P, SSD_N), F32))
    return pl.pallas_call(
        functools.partial(_ssd_kernel, L=L, has_init=has_init),
        grid=(n_seq,),
        in_specs=in_specs,
        out_specs=out_specs,
        out_shape=out_shape,
        scratch_shapes=[
            pltpu.VMEM((L + 2 * PAD, SSD_XBC), F32),
            pltpu.VMEM((L, SSD_XBC), F32),
            pltpu.VMEM((L, D), F32),
            pltpu.VMEM((SSD_N, SSD_H * SSD_P), F32),
        ],
        compiler_params=_cparams(("arbitrary",)),
        name=f"ssd_{L}",
    )(*args)


HY_CB = 256


def dft_matrices(L):
    k = np.arange(L, dtype=np.int64)
    ang = ((k[:, None] * k[None, :]) % (2 * L)).astype(np.float64) * (math.pi / L)
    cosm = np.cos(ang)
    sinm = np.sin(ang)
    sinm[0] = np.where(k % 2 == 0, 1.0, -1.0)
    fwd = np.concatenate([cosm, sinm], axis=0).astype(np.float32)
    wts = np.where(k == 0, 1.0, 2.0) / (2 * L)
    inv = np.concatenate([cosm * wts[None, :], sinm.T * wts[None, :]], axis=1).astype(np.float32)
    fwd, inv = jnp.asarray(fwd), jnp.asarray(inv)
    f_hi = fwd.astype(BF16)
    f_lo = (fwd - f_hi.astype(F32)).astype(BF16)
    return f_hi, f_lo, inv.astype(BF16)


def _dot3(a_hi, a_lo, b):
    b_hi = b.astype(BF16)
    b_lo = (b - b_hi.astype(F32)).astype(BF16)
    return (jnp.dot(a_hi, b_hi, preferred_element_type=F32) + jnp.dot(a_lo, b_hi, preferred_element_type=F32)
            + jnp.dot(a_hi, b_lo, preferred_element_type=F32))


def _const_spec(arr):
    return pl.BlockSpec(arr.shape, lambda *_: (0,) * arr.ndim, pipeline_mode=pl.Buffered(1))


def _hy_filter_kernel(feat_ref, w1_ref, b1_ref, w2_ref, b2_ref, fr_ref, w3_ref, dl_ref, fh_ref, fl_ref, o_ref, *, L):
    hp = functools.partial(jnp.dot, precision=HIGHEST, preferred_element_type=F32)
    hdn = jnp.sin(fr_ref[0:1, :] * (hp(feat_ref[...], w1_ref[...]) + b1_ref[...]))
    hdn = jnp.sin(fr_ref[1:2, :] * (hp(hdn, w2_ref[...]) + b2_ref[...]))
    rowi = lax.broadcasted_iota(jnp.int32, (L, 1), 0)
    t = rowi.astype(F32) * (1.0 / (L - 1))
    dec = jnp.exp(-t * dl_ref[...])
    first = rowi == 0
    for o in range(2):
        fwd = hp(hdn, w3_ref[2 * o]) * dec
        bwd = jnp.where(first, 0.0, hp(hdn, w3_ref[2 * o + 1]) * dec)
        ss = _dot3(fh_ref[...], fl_ref[...], fwd + bwd)
        sd = _dot3(fh_ref[...], fl_ref[...], fwd - bwd)
        hr = ss[0:L]
        o_ref[o, 0] = hr
        o_ref[o, 1] = jnp.where(first, 0.0, sd[L:2 * L])
        o_ref[o, 2] = jnp.where(first, ss[L:L + 1], hr)


def hyena_filter_spectra(feat, w1, b1, w2, b2, freq, w3r, deltas, f_hi, f_lo, *, L):
    full = lambda arr: pl.BlockSpec(arr.shape, lambda j: (0,) * arr.ndim)
    return pl.pallas_call(
        functools.partial(_hy_filter_kernel, L=L),
        grid=(D // HY_CB,),
        in_specs=[full(feat), full(w1), full(b1), full(w2), full(b2), full(freq),
                  pl.BlockSpec((4, HY_HID, HY_CB), lambda j: (0, 0, j)),
                  pl.BlockSpec((1, HY_CB), lambda j: (0, j)),
                  _const_spec(f_hi), _const_spec(f_lo)],
        out_specs=pl.BlockSpec((2, 3, L, HY_CB), lambda j: (0, 0, 0, j)),
        out_shape=jax.ShapeDtypeStruct((2, 3, L, D), F32),
        compiler_params=_cparams(("arbitrary",)),
        name=f"hy_filter_{L}",
    )(feat, w1, b1, w2, b2, freq, w3r, deltas, f_hi, f_lo)


def _hyena_kernel(p0_ref, p1_ref, p2_ref, w0_ref, w1_ref, w2_ref, b0_ref, b1_ref, b2_ref, h_ref, hb_ref,
                  f_ref, g_ref, o_ref, xp_s, *, L):
    xp_s[0:PAD, :] = jnp.zeros((PAD, HY_CB), F32)
    xp_s[PAD + L:2 * PAD + L, :] = jnp.zeros((PAD, HY_CB), F32)

    def conv(p_ref, w_ref, b_ref):
        xp_s[PAD:PAD + L, :] = p_ref[...]
        acc = b_ref[...] + xp_s[PAD - 1:PAD - 1 + L, :] * w_ref[0:1, :]
        for k in range(1, HY_K):
            acc = acc + xp_s[PAD - 1 + k:PAD - 1 + k + L, :] * w_ref[k:k + 1, :]
        return acc

    u = conv(p0_ref, w0_ref, b0_ref)
    for o, (p_ref, w_ref, b_ref) in enumerate(((p1_ref, w1_ref, b1_ref), (p2_ref, w2_ref, b2_ref))):
        spec = jnp.dot(f_ref[...], u.astype(BF16), preferred_element_type=F32)
        ar, ai = spec[0:L], spec[L:2 * L]
        yr = (ar * h_ref[o, 0] - ai * h_ref[o, 1]).astype(BF16)
        yn = (ar * h_ref[o, 1] + ai * h_ref[o, 2]).astype(BF16)
        y = jnp.dot(g_ref[...], jnp.concatenate([yr, yn], axis=0), preferred_element_type=F32)
        u = conv(p_ref, w_ref, b_ref) * (y + u * hb_ref[o:o + 1, :])
    o_ref[...] = u.astype(o_ref.dtype)


def hyena_mixer(hy, conv_w, conv_b, spectra, hy_bias, mats, *, L, n_seq, row_off):
    blk0 = row_off // L
    nj = D // HY_CB
    part = lambda q: pl.BlockSpec((L, HY_CB), lambda j, b: (blk0 + b, q * nj + j))
    wpart = lambda q: pl.BlockSpec((HY_K, HY_CB), lambda j, b: (0, q * nj + j))
    bpart = lambda q: pl.BlockSpec((1, HY_CB), lambda j, b: (0, q * nj + j))
    return pl.pallas_call(
        functools.partial(_hyena_kernel, L=L),
        grid=(nj, n_seq),
        in_specs=[part(0), part(1), part(2), wpart(0), wpart(1), wpart(2), bpart(0), bpart(1), bpart(2),
                  pl.BlockSpec((2, 3, L, HY_CB), lambda j, b: (0, 0, 0, j)),
                  pl.BlockSpec((2, HY_CB), lambda j, b: (0, j))]
                 + [_const_spec(m) for m in mats],
        out_specs=pl.BlockSpec((L, HY_CB), lambda j, b: (b, j)),
        out_shape=jax.ShapeDtypeStruct((n_seq * L, D), BF16),
        scratch_shapes=[pltpu.VMEM((L + 2 * PAD, HY_CB), F32)],
        compiler_params=_cparams(("arbitrary", "arbitrary")),
        name=f"hyena_{L}",
    )(hy, hy, hy, conv_w, conv_w, conv_w, conv_b, conv_b, conv_b, spectra, hy_bias, *mats)


ROUTER_LANES = LANES
BIG_LANE = 1e9


ROW_GROUP = D // LANES


def _store_row_groups(ref, val):
    n = val.shape[0]
    for s in range(ROW_GROUP):
        ref[pl.ds(s, n, stride=ROW_GROUP), :] = val[:, s * LANES:(s + 1) * LANES]


def _load_row_groups(ref, n, s):
    return ref[pl.ds(s, n, stride=ROW_GROUP), :]


def _first_max_lane(v, lanef):
    m = jnp.max(v, axis=-1, keepdims=True)
    return m, jnp.min(jnp.where(v == m, lanef, BIG_LANE), axis=-1, keepdims=True)


def _out_router_kernel(*refs, n_in):
    a_refs = refs[:2 * n_in]
    w_ref, x_ref, mod_ref, gf_ref, wr_ref, br_ref, xo_ref, h2_ref, ids_ref, wts_ref, cnt_ref = refs[2 * n_in:]
    is_ctx = pl.program_id(0) < CTX_TILES
    acc, k0 = None, 0
    for ac_ref, al_ref in zip(a_refs[0::2], a_refs[1::2]):
        kk = ac_ref.shape[1]
        a = jnp.where(is_ctx, ac_ref[...], al_ref[...])
        part = jnp.dot(a, w_ref[k0:k0 + kk, :], preferred_element_type=F32)
        acc = part if acc is None else acc + part
        k0 += kk
    xn = x_ref[...] + mod_ref[0, 2:3, :] * acc
    xo_ref[...] = xn
    h2 = _modulated(xn, gf_ref, mod_ref, 3)
    h2_ref[...] = h2

    h_hi = h2.astype(BF16)
    h_lo = (h2 - h_hi.astype(F32)).astype(BF16)
    logits = (jnp.dot(h_hi, wr_ref[0], preferred_element_type=F32) + jnp.dot(h_lo, wr_ref[0], preferred_element_type=F32)
              + jnp.dot(h_hi, wr_ref[1], preferred_element_type=F32) + br_ref[...])
    lanef = lax.broadcasted_iota(jnp.int32, logits.shape, 1).astype(F32)
    gl = jnp.where(lanef < MOE_G, logits, NEG)
    gm, gi = _first_max_lane(gl, lanef)
    g_w = 1.0 / jnp.sum(jnp.exp(gl - gm), axis=-1, keepdims=True)
    lo = MOE_G + MOE_PG * gi
    el = jnp.where((lanef >= lo) & (lanef < lo + MOE_PG), logits, NEG)
    m1, e1 = _first_max_lane(el, lanef)
    m2, e2 = _first_max_lane(jnp.where(lanef == e1, NEG, el), lanef)
    p2 = jnp.exp(m2 - m1)
    w1 = g_w / (1.0 + p2)
    ids_ref[...] = jnp.where(lanef == 0, e1 - MOE_G, jnp.where(lanef == 1, e2 - MOE_G, 0.0)).astype(jnp.int32)
    wts_ref[...] = jnp.where(lanef == 0, w1, jnp.where(lanef == 1, w1 * p2, 0.0))
    chosen = ((lanef == e1 - MOE_G) | (lanef == e2 - MOE_G)).astype(F32)
    cnt_ref[0] = jnp.sum(chosen, axis=0, keepdims=True).astype(jnp.int32)


def out_proj_router(acts, w_bf, x, mod, gf, wr, br):
    tile = lambda w: pl.BlockSpec((TM, w), lambda i: (i, 0))
    full = lambda arr: pl.BlockSpec(arr.shape, lambda i: (0,) * arr.ndim)
    pair = lambda w: [pl.BlockSpec((TM, w), lambda i: (jnp.minimum(i, CTX_TILES - 1), 0)),
                      pl.BlockSpec((TM, w), lambda i: (jnp.maximum(i - CTX_TILES, 0), 0))]
    return pl.pallas_call(
        functools.partial(_out_router_kernel, n_in=len(acts)),
        grid=(N_TILES,),
        in_specs=[s for a in acts for s in pair(a[0].shape[1])] + [full(w_bf), tile(D),
                  pl.BlockSpec((1, 6, D), lambda i: (_mod_row(i), 0, 0)), full(gf), full(wr), full(br)],
        out_specs=[tile(D), tile(D), tile(ROUTER_LANES), tile(ROUTER_LANES),
                   pl.BlockSpec((1, 1, ROUTER_LANES), lambda i: (i, 0, 0))],
        out_shape=[jax.ShapeDtypeStruct((T_ALL, D), F32), jax.ShapeDtypeStruct((T_ALL, D), F32),
                   jax.ShapeDtypeStruct((T_ALL, ROUTER_LANES), jnp.int32),
                   jax.ShapeDtypeStruct((T_ALL, ROUTER_LANES), F32),
                   jax.ShapeDtypeStruct((N_TILES, 1, ROUTER_LANES), jnp.int32)],
        compiler_params=_cparams(("arbitrary",)),
        name="out_router",
    )(*[part for a in acts for part in a], w_bf, x, mod, gf, wr, br)


N_ASSIGN = 2 * T_ALL
MOE_TILES = N_ASSIGN // TM + MOE_E
N_SLOTS = MOE_TILES * TM


def route_tables(cnt3):
    cnt = cnt3[:, 0, :MOE_E]
    total = jnp.sum(cnt, axis=0)
    padded = (total + TM - 1) // TM * TM
    ends = jnp.cumsum(padded)
    gdst = (ends - padded)[None, :] + jnp.cumsum(cnt, axis=0) - cnt
    loc = jnp.cumsum(cnt, axis=1) - cnt
    starts = jnp.arange(MOE_TILES, dtype=jnp.int32) * TM
    tile_expert = jnp.minimum(jnp.sum((ends[None, :] <= starts[:, None]).astype(jnp.int32), axis=1), MOE_E - 1)
    n_used = (ends[-1] // TM).astype(jnp.int32).reshape(1)
    return cnt, loc, gdst, tile_expert, n_used


RUN_BITS = (2 * TM).bit_length()


def _dispatch_kernel(cnt_s, loc_s, gdst_s, h_ref, ids_ref, gcol_ref, xs_in, xs_ref, dest_ref, srt, sem):
    del xs_in
    i = pl.program_id(0)
    slot = i % 2
    n_rows = 2 * TM

    idt = ids_ref[...].astype(F32).T
    sub = lax.broadcasted_iota(jnp.int32, (LANES, TM), 0).astype(F32)
    m0 = (sub == idt[0:1, :]).astype(F32)
    m1 = (sub == idt[1:2, :]).astype(F32)
    mt = (m0 + m1).astype(BF16)
    tr = lax.broadcasted_iota(jnp.int32, (TM, TM), 0)
    tc = lax.broadcasted_iota(jnp.int32, (TM, TM), 1)
    earlier = jnp.dot(mt, (tr < tc).astype(BF16), preferred_element_type=F32)
    er = lax.broadcasted_iota(jnp.int32, (LANES, LANES), 0)
    ec = lax.broadcasted_iota(jnp.int32, (LANES, LANES), 1)
    below = jnp.dot((ec < er).astype(BF16), mt, preferred_element_type=F32)
    local = jnp.sum(below, axis=1, keepdims=True) + earlier
    glob = gcol_ref[0] + earlier
    pos0 = jnp.sum(m0 * local, axis=0, keepdims=True)
    pos1 = jnp.sum(m1 * local, axis=0, keepdims=True)
    dest_ref[0] = jnp.concatenate([jnp.sum(m0 * glob, axis=0, keepdims=True),
                                   jnp.sum(m1 * glob, axis=0, keepdims=True)], axis=0).astype(jnp.int32)

    srow = lax.broadcasted_iota(jnp.int32, (n_rows, TM), 0).astype(F32)
    perm = jnp.where((srow == pos0) | (srow == pos1), 1.0, 0.0).astype(BF16)
    _store_row_groups(srt.at[slot], jnp.dot(perm, h_ref[...].astype(BF16), preferred_element_type=F32))

    for e in range(MOE_E):
        n, s0, d0 = cnt_s[0, 0, e], loc_s[0, 0, e], gdst_s[0, 0, e]
        for b in reversed(range(RUN_BITS)):
            size = 1 << b
            off = (n >> (b + 1)) << (b + 1)

            @pl.when(((n >> b) & 1) == 1)
            def _(size=size, off=off, s0=s0, d0=d0):
                src = pl.multiple_of((s0 + off) * ROW_GROUP, ROW_GROUP)
                dst = pl.multiple_of((d0 + off) * ROW_GROUP, ROW_GROUP)
                pltpu.make_async_copy(srt.at[slot, pl.ds(src, size * ROW_GROUP), :],
                                      xs_ref.at[pl.ds(dst, size * ROW_GROUP), :], sem.at[slot]).start()

    def wait(s):
        pltpu.make_async_copy(srt.at[s], xs_ref.at[pl.ds(0, n_rows * ROW_GROUP), :], sem.at[s]).wait()

    @pl.when(i > 0)
    def _():
        wait(1 - slot)

    @pl.when(i == N_TILES - 1)
    def _():
        wait(slot)


def dispatch_rows(h2, ids, cnt, loc, gdst):
    tab = lambda: pl.BlockSpec((1, 1, MOE_E), lambda i: (i, 0, 0), memory_space=pltpu.SMEM)
    gcol = jnp.pad(gdst.astype(F32), ((0, 0), (0, LANES - MOE_E)))[:, :, None]
    return pl.pallas_call(
        _dispatch_kernel,
        grid=(N_TILES,),
        in_specs=[tab(), tab(), tab(),
                  pl.BlockSpec((TM, D), lambda i: (i, 0)), pl.BlockSpec((TM, ROUTER_LANES), lambda i: (i, 0)),
                  pl.BlockSpec((1, LANES, 1), lambda i: (i, 0, 0)),
                  pl.BlockSpec(memory_space=pl.ANY)],
        out_specs=[pl.BlockSpec(memory_space=pl.ANY), pl.BlockSpec((1, 2, TM), lambda i: (i, 0, 0))],
        out_shape=[jax.ShapeDtypeStruct((N_SLOTS * ROW_GROUP, LANES), F32),
                   jax.ShapeDtypeStruct((N_TILES, 2, TM), jnp.int32)],
        scratch_shapes=[pltpu.VMEM((2, 2 * TM * ROW_GROUP, LANES), F32), pltpu.SemaphoreType.DMA((2,))],
        input_output_aliases={6: 0},
        compiler_params=_cparams(("arbitrary",)),
        name="moe_dispatch",
    )(cnt.reshape(N_TILES, 1, MOE_E), loc.reshape(N_TILES, 1, MOE_E), gdst.reshape(N_TILES, 1, MOE_E),
      h2, ids, gcol, jnp.zeros((N_SLOTS * ROW_GROUP, LANES), F32))


DMA_UNROLL = 8


def _start_group_gather(src_hbm, idx_ref, n, dst_ref, sem):
    def body(j, c):
        for u in range(DMA_UNROLL):
            r = j * DMA_UNROLL + u
            src = pl.multiple_of(idx_ref[0, 0, r] * ROW_GROUP, ROW_GROUP)
            dst = pl.multiple_of(r * ROW_GROUP, ROW_GROUP)
            pltpu.make_async_copy(src_hbm.at[pl.ds(src, ROW_GROUP), :], dst_ref.at[pl.ds(dst, ROW_GROUP), :],
                                  sem).start(priority=u % 2)
        return c

    lax.fori_loop(0, n // DMA_UNROLL, body, 0)


def _wait_group_gather(src_hbm, dst_ref, sem):
    pltpu.make_async_copy(src_hbm.at[pl.ds(0, dst_ref.shape[0]), :], dst_ref, sem).wait()


def _experts_kernel(te_ref, nu_ref, x_ref, wg_ref, wu_ref, wd_ref, o_ref, xcat):
    i = pl.program_id(0)

    @pl.when(i < nu_ref[0])
    def _():
        for s in range(ROW_GROUP):
            xcat[:, s * LANES:(s + 1) * LANES] = _load_row_groups(x_ref, TM, s).astype(BF16)
        x = xcat[...]
        g = jnp.dot(x, wg_ref[0, 0].astype(BF16), preferred_element_type=F32)
        u = jnp.dot(x, wu_ref[0, 0].astype(BF16), preferred_element_type=F32)
        hid = (_silu(g) * u).astype(BF16)
        _store_row_groups(o_ref, jnp.dot(hid, wd_ref[0, 0].astype(BF16), preferred_element_type=F32))

    @pl.when(i >= nu_ref[0])
    def _():
        o_ref[...] = jnp.zeros(o_ref.shape, o_ref.dtype)


def grouped_experts(xs, w_gate, w_up, w_down, tile_expert, n_used, layer):
    wspec = lambda a, b: pl.BlockSpec((1, 1, a, b), lambda i, te, nu: (layer, te[i], 0, 0))
    return pl.pallas_call(
        _experts_kernel,
        grid_spec=pltpu.PrefetchScalarGridSpec(
            num_scalar_prefetch=2,
            grid=(MOE_TILES,),
            in_specs=[pl.BlockSpec((TM * ROW_GROUP, LANES), lambda i, te, nu: (i, 0)),
                      wspec(D, MOE_F), wspec(D, MOE_F), wspec(MOE_F, D)],
            out_specs=pl.BlockSpec((TM * ROW_GROUP, LANES), lambda i, te, nu: (i, 0)),
            scratch_shapes=[pltpu.VMEM((TM, D), BF16)],
        ),
        out_shape=jax.ShapeDtypeStruct((N_SLOTS * ROW_GROUP, LANES), F32),
        compiler_params=_cparams(("arbitrary",)),
        name="moe_experts",
    )(tile_expert, n_used, xs, w_gate, w_up, w_down)


def _combine_kernel(cur_ref, nxt_ref, ys_hbm, x_ref, wts_ref, mod_ref, gfin_ref, o_ref, buf, sem, *, final):
    i = pl.program_id(0)
    slot = i % 2

    @pl.when(i == 0)
    def _():
        _start_group_gather(ys_hbm, cur_ref, 2 * TM, buf.at[0], sem.at[0])

    @pl.when(i + 1 < N_TILES)
    def _():
        _start_group_gather(ys_hbm, nxt_ref, 2 * TM, buf.at[1 - slot], sem.at[1 - slot])

    _wait_group_gather(ys_hbm, buf.at[slot], sem.at[slot])
    w0, w1 = wts_ref[:, 0:1], wts_ref[:, 1:2]
    for s in range(ROW_GROUP):
        cols = slice(s * LANES, (s + 1) * LANES)
        y0 = buf[slot, pl.ds(s, TM, stride=ROW_GROUP), :]
        y1 = buf[slot, pl.ds(TM * ROW_GROUP + s, TM, stride=ROW_GROUP), :]
        o_ref[:, cols] = x_ref[:, cols] + mod_ref[0, 5:6, cols] * (w0 * y0 + w1 * y1)
    if final:
        o_ref[...] = _rms(o_ref[...]) * gfin_ref[...]


def moe_combine(ys, dest, x, wts, mod, gfin, *, final):
    tile = lambda w: pl.BlockSpec((TM, w), lambda i: (i, 0))
    idx = lambda f: pl.BlockSpec((1, 1, 2 * TM), lambda i: (f(i), 0, 0), memory_space=pltpu.SMEM)
    dest3 = dest.reshape(N_TILES, 1, 2 * TM)
    return pl.pallas_call(
        functools.partial(_combine_kernel, final=final),
        grid=(N_TILES,),
        in_specs=[idx(lambda i: i), idx(lambda i: jnp.minimum(i + 1, N_TILES - 1)),
                  pl.BlockSpec(memory_space=pl.ANY), tile(D), tile(ROUTER_LANES),
                  pl.BlockSpec((1, 6, D), lambda i: (_mod_row(i), 0, 0)),
                  pl.BlockSpec((1, D), lambda i: (0, 0))],
        out_specs=tile(D),
        out_shape=jax.ShapeDtypeStruct((T_ALL, D), F32),
        scratch_shapes=[pltpu.VMEM((2, 2 * TM * ROW_GROUP, LANES), F32), pltpu.SemaphoreType.DMA((2,))],
        compiler_params=_cparams(("arbitrary",)),
        name="moe_combine",
    )(dest3, dest3, ys, x, wts, mod, gfin)


ODD_COLS = 2048
ROPE_Q = MLA_H * MLA_ROPE
ROPE_SHIFT = ROPE_F


def rope_tables():
    t = np.arange(L_LAT)
    pos = np.stack([t // GRID_W, t % GRID_W], axis=1).astype(np.float64)
    inv = 10000.0 ** (-np.arange(ROPE_F, dtype=np.float64) / ROPE_F)
    lane = np.arange(ROPE_Q) % MLA_ROPE
    axis = lane // (2 * ROPE_F)
    first = (lane % (2 * ROPE_F)) < ROPE_F
    ang = pos[:, axis] * inv[lane % ROPE_F][None, :]
    cos, sin = np.cos(ang), np.sin(ang)
    tabs = [cos, np.where(first[None, :], -sin, 0.0), np.where(first[None, :], 0.0, sin)]
    ident = [np.ones((1, TM, ROPE_Q)), np.zeros((1, TM, ROPE_Q)), np.zeros((1, TM, ROPE_Q))]
    return [jnp.asarray(np.concatenate([i, tb.reshape(LAT_TILES_PER_SEQ, TM, ROPE_Q)], axis=0).astype(np.float32))
            for i, tb in zip(ident, tabs)]


def _rope(x, c, a, b):
    n = x.shape[1]
    return x * c[:, :n] + pltpu.roll(x, n - ROPE_SHIFT, 1) * a[:, :n] + pltpu.roll(x, ROPE_SHIFT, 1) * b[:, :n]


def _odd_in_kernel(x_ref, mod_ref, g_ref, w_ref, gq_ref, wuq_ref, gkv_ref, wukv_ref, rc_ref, ra_ref, rb_ref,
                   qkv_ref, qm_ref, ckv_ref, kvu_ref, kr_ref):
    hb = _modulated(x_ref[...], g_ref, mod_ref, 0).astype(BF16)
    for c0 in range(0, 3 * NA_W, PROJ_CHUNK):
        qkv_ref[:, c0:c0 + PROJ_CHUNK] = jnp.dot(hb, w_ref[:, c0:c0 + PROJ_CHUNK], preferred_element_type=F32)
    rest = jnp.dot(hb, w_ref[:, 3 * NA_W:ODD_COLS], preferred_element_type=F32)
    rc, ra, rb = rc_ref[0], ra_ref[0], rb_ref[0]
    qd = (_rms(rest[:, 0:MLA_QR]) * gq_ref[...]).astype(BF16)
    qm = jnp.dot(qd, wuq_ref[...], preferred_element_type=F32)
    qm_ref[:, 0:MLA_H * MLA_NOPE] = qm[:, 0:MLA_H * MLA_NOPE]
    qm_ref[:, MLA_H * MLA_NOPE:] = _rope(qm[:, MLA_H * MLA_NOPE:], rc, ra, rb)
    ckv = _rms(rest[:, MLA_QR:MLA_QR + MLA_KVR]) * gkv_ref[...]
    ckv_ref[...] = ckv
    kvu_ref[...] = jnp.dot(ckv.astype(BF16), wukv_ref[...], preferred_element_type=F32)
    kr_ref[...] = _rope(rest[:, MLA_QR + MLA_KVR:], rc, ra, rb)


def odd_in_proj(x, mod, g, w_bf, gq, wuq_bf, gkv, wukv_bf, tabs):
    tile = lambda w: pl.BlockSpec((TM, w), lambda i: (i, 0))
    full = lambda arr: pl.BlockSpec(arr.shape, lambda i: (0,) * arr.ndim)
    tab = pl.BlockSpec((1, TM, ROPE_Q),
                       lambda i: (jnp.where(i < CTX_TILES, 0, 1 + (i - CTX_TILES) % LAT_TILES_PER_SEQ), 0, 0))
    widths = (3 * NA_W, MLA_H * MLA_QK, MLA_KVR, MLA_H * (MLA_NOPE + MLA_V), LANES)
    return pl.pallas_call(
        _odd_in_kernel,
        grid=(N_TILES,),
        in_specs=[tile(D), pl.BlockSpec((1, 6, D), lambda i: (_mod_row(i), 0, 0)), full(g), full(w_bf),
                  full(gq), full(wuq_bf), full(gkv), full(wukv_bf), tab, tab, tab],
        out_specs=[tile(w) for w in widths],
        out_shape=[jax.ShapeDtypeStruct((T_ALL, w), F32) for w in widths],
        compiler_params=_cparams(("arbitrary",)),
        name="odd_in",
    )(x, mod, g, w_bf, gq, wuq_bf, gkv, wukv_bf, *tabs)


LOG2E = math.log2(math.e)
NA_QSCALE = NA_D ** -0.5 * LOG2E
MLA_QSCALE = MLA_QK ** -0.5 * LOG2E
NT = (((1,), (1,)), ((), ()))


def _softmax_pv(scores, values):
    m = functools.reduce(jnp.maximum, [jnp.max(s, axis=-1, keepdims=True) for s in scores])
    ps = [jnp.exp2(s - m) for s in scores]
    den = functools.reduce(jnp.add, [jnp.sum(p, axis=-1, keepdims=True) for p in ps])
    acc = functools.reduce(jnp.add, [jnp.dot(p.astype(BF16), v, preferred_element_type=F32) for p, v in zip(ps, values)])
    return acc / den


def _head(ref_or_val, h, width, base=0):
    return ref_or_val[:, base + h * width:base + (h + 1) * width]


def _mla_scores(qm, h, kn, kr):
    qn = (_head(qm, h, MLA_NOPE) * MLA_QSCALE).astype(BF16)
    qr = (_head(qm, h, MLA_ROPE, MLA_H * MLA_NOPE) * MLA_QSCALE).astype(BF16)
    return (lax.dot_general(qn, kn, NT, preferred_element_type=F32)
            + lax.dot_general(qr, kr, NT, preferred_element_type=F32))


def _attn_ctx_kernel(qkv_ref, qm_ref, kvu_ref, kr_ref, ona_ref, omla_ref):
    kr = kr_ref[:, 0:MLA_ROPE].astype(BF16)
    for h in range(NA_H):
        q = (_head(qkv_ref, h, NA_D) * NA_QSCALE).astype(BF16)
        k = _head(qkv_ref, h, NA_D, NA_W).astype(BF16)
        v = _head(qkv_ref, h, NA_D, 2 * NA_W).astype(BF16)
        s = lax.dot_general(q, k, NT, preferred_element_type=F32)
        ona_ref[:, h * NA_D:(h + 1) * NA_D] = _softmax_pv([s], [v]).astype(ona_ref.dtype)
    for h in range(MLA_H):
        kn = _head(kvu_ref, h, MLA_NOPE).astype(BF16)
        v = _head(kvu_ref, h, MLA_V, MLA_H * MLA_NOPE).astype(BF16)
        s = _mla_scores(qm_ref, h, kn, kr)
        omla_ref[:, h * MLA_V:(h + 1) * MLA_V] = _softmax_pv([s], [v]).astype(omla_ref.dtype)


def attn_context(qkv, qm, kvu, kr):
    seq = lambda w: pl.BlockSpec((L_CTX, w), lambda b: (b, 0))
    return pl.pallas_call(
        _attn_ctx_kernel,
        grid=(N_CTX,),
        in_specs=[seq(3 * NA_W), seq(MLA_H * MLA_QK), seq(MLA_H * (MLA_NOPE + MLA_V)), seq(LANES)],
        out_specs=[seq(NA_W), seq(MLA_H * MLA_V)],
        out_shape=[jax.ShapeDtypeStruct((T_CTX, NA_W), BF16), jax.ShapeDtypeStruct((T_CTX, MLA_H * MLA_V), BF16)],
        compiler_params=_cparams(("arbitrary",)),
        name="attn_ctx",
    )(qkv, qm, kvu, kr)


N_DR = 2 * NA_WIN_R - 1
GRID_ROWS = L_LAT // GRID_W


def _na_bias_kernel(t_ref, o_ref):
    r = pl.program_id(1)
    r0 = jnp.clip(r - NA_WIN_R // 2, 0, GRID_ROWS - NA_WIN_R)
    for kr in range(GRID_ROWS):
        in_window = (kr >= r0) & (kr < r0 + NA_WIN_R)
        dr = jnp.clip(kr - r + NA_WIN_R - 1, 0, N_DR - 1)
        o_ref[0, :, kr * GRID_W:(kr + 1) * GRID_W] = jnp.where(in_window, t_ref[0, dr], NEG)


def neighbourhood_bias(rel_bias):
    c = np.arange(GRID_W)
    c0 = np.clip(c - NA_WIN_C // 2, 0, GRID_W - NA_WIN_C)
    col_ok = (c[None, :] >= c0[:, None]) & (c[None, :] < c0[:, None] + NA_WIN_C)
    dc = np.clip(c[None, :] - c[:, None], -(NA_WIN_C - 1), NA_WIN_C - 1) + NA_WIN_C - 1
    sel_c = (dc[:, :, None] == np.arange(2 * NA_WIN_C - 1)).astype(np.float32)
    t = jnp.einsum("hdj,qcj->hdqc", rel_bias.astype(F32), jnp.asarray(sel_c), precision=HIGHEST)
    t = jnp.where(jnp.asarray(col_ok)[None, None], t * LOG2E, NEG)
    return pl.pallas_call(
        _na_bias_kernel,
        grid=(NA_H, GRID_ROWS),
        in_specs=[pl.BlockSpec((1, N_DR, GRID_W, GRID_W), lambda h, r: (h, 0, 0, 0))],
        out_specs=pl.BlockSpec((1, GRID_W, L_LAT), lambda h, r: (h, r, 0)),
        out_shape=jax.ShapeDtypeStruct((NA_H, L_LAT, L_LAT), F32),
        compiler_params=_cparams(("arbitrary", "arbitrary")),
        name="na_bias",
    )(t)


def _na_lat_kernel(q_ref, k_ref, v_ref, kc_ref, vc_ref, b_ref, o_ref):
    for h in range(NA_H):
        q = (_head(q_ref, h, NA_D) * NA_QSCALE).astype(BF16)
        k = _head(k_ref, h, NA_D).astype(BF16)
        v = _head(v_ref, h, NA_D).astype(BF16)
        kc = kc_ref[0, 0, h].astype(BF16)
        vc = vc_ref[0, 0, h].astype(BF16)
        s1 = lax.dot_general(q, k, NT, preferred_element_type=F32) + b_ref[h]
        s2 = lax.dot_general(q, kc, NT, preferred_element_type=F32)
        o_ref[:, h * NA_D:(h + 1) * NA_D] = _softmax_pv([s1, s2], [v, vc]).astype(o_ref.dtype)


def attn_neighbourhood_latent(qkv, cache_k, cache_v, bias):
    nq = L_LAT // TM
    t0 = T_CTX // TM
    s0 = T_CTX // L_LAT
    cache = pl.BlockSpec((1, 1, NA_H, PAST, NA_D), lambda qt, b: (b, 0, 0, 0, 0))
    return pl.pallas_call(
        _na_lat_kernel,
        grid=(nq, N_LAT),
        in_specs=[pl.BlockSpec((TM, NA_W), lambda qt, b: (t0 + b * nq + qt, 0)),
                  pl.BlockSpec((L_LAT, NA_W), lambda qt, b: (s0 + b, 1)),
                  pl.BlockSpec((L_LAT, NA_W), lambda qt, b: (s0 + b, 2)),
                  cache, cache,
                  pl.BlockSpec((NA_H, TM, L_LAT), lambda qt, b: (0, qt, 0))],
        out_specs=pl.BlockSpec((TM, NA_W), lambda qt, b: (b * nq + qt, 0)),
        out_shape=jax.ShapeDtypeStruct((T_LAT, NA_W), BF16),
        compiler_params=_cparams(("arbitrary", "arbitrary")),
        name="attn_na_lat",
    )(qkv, qkv, qkv, cache_k, cache_v, bias)


def _mla_lat_kernel(qm_ref, kvu_ref, kr_ref, ckv_ref, krc_ref, wukv_ref, o_ref):
    kvc = jnp.dot(ckv_ref[0, 0].astype(BF16), wukv_ref[...], preferred_element_type=F32)
    kr = kr_ref[:, 0:MLA_ROPE].astype(BF16)
    krc = krc_ref[0, 0].astype(BF16)
    for h in range(MLA_H):
        kn = _head(kvu_ref, h, MLA_NOPE).astype(BF16)
        v = _head(kvu_ref, h, MLA_V, MLA_H * MLA_NOPE).astype(BF16)
        knc = _head(kvc, h, MLA_NOPE).astype(BF16)
        vc = _head(kvc, h, MLA_V, MLA_H * MLA_NOPE).astype(BF16)
        s1 = _mla_scores(qm_ref, h, kn, kr)
        s2 = _mla_scores(qm_ref, h, knc, krc)
        o_ref[:, h * MLA_V:(h + 1) * MLA_V] = _softmax_pv([s1, s2], [v, vc]).astype(o_ref.dtype)


def attn_mla_latent(qm, kvu, kr, cache_ckv, cache_krope, wukv_bf):
    nq = L_LAT // TM
    t0 = T_CTX // TM
    s0 = T_CTX // L_LAT
    return pl.pallas_call(
        _mla_lat_kernel,
        grid=(nq, N_LAT),
        in_specs=[pl.BlockSpec((TM, MLA_H * MLA_QK), lambda qt, b: (t0 + b * nq + qt, 0)),
                  pl.BlockSpec((L_LAT, MLA_H * (MLA_NOPE + MLA_V)), lambda qt, b: (s0 + b, 0)),
                  pl.BlockSpec((L_LAT, LANES), lambda qt, b: (s0 + b, 0)),
                  pl.BlockSpec((1, 1, PAST, MLA_KVR), lambda qt, b: (b, 0, 0, 0)),
                  pl.BlockSpec((1, 1, PAST, MLA_ROPE), lambda qt, b: (b, 0, 0, 0)),
                  pl.BlockSpec(wukv_bf.shape, lambda qt, b: (0, 0))],
        out_specs=pl.BlockSpec((TM, MLA_H * MLA_V), lambda qt, b: (b * nq + qt, 0)),
        out_shape=jax.ShapeDtypeStruct((T_LAT, MLA_H * MLA_V), BF16),
        compiler_params=_cparams(("arbitrary", "arbitrary")),
        name="attn_mla_lat",
    )(qm, kvu, kr, cache_ckv, cache_krope, wukv_bf)


def moe_block(h2, ids, wts, cnt3, x, mod, gfin, w_gate, w_up, w_down, layer, *, final):
    cnt, loc, gdst, tile_expert, n_used = route_tables(cnt3)
    xs, dest = dispatch_rows(h2, ids, cnt, loc, gdst)
    ys = grouped_experts(xs, w_gate, w_up, w_down, tile_expert, n_used, layer)
    return moe_combine(ys, dest, x, wts, mod, gfin, final=final)


def _pad_lanes(a):
    return jnp.pad(a, ((0, 0), (0, LANES - a.shape[1])))


def _hyena_features(L):
    t = np.linspace(0.0, 1.0, L)[:, None]
    w = 2.0 * math.pi * np.arange(L) / L
    bands = np.linspace(1e-4, HY_BANDS - 1, HY_BANDS)
    ang = w[:, None] * bands[None]
    feat = np.concatenate([t, np.cos(ang), -np.sin(ang)], axis=-1)
    return jnp.asarray(np.pad(feat, ((0, 0), (0, LANES - HY_FEAT))).astype(np.float32))


def _router_params(w_gr, b_gr, w_er, b_er):
    wr = _pad_lanes(jnp.concatenate([w_gr, w_er], axis=1))
    br = _pad_lanes(jnp.concatenate([b_gr, b_er])[None])
    wr_hi = wr.astype(BF16)
    wr_lo = (wr - wr_hi.astype(F32)).astype(BF16)
    return jnp.stack([wr_hi, wr_lo]), br


def _even_layer(x, mod, g_mix, state, w_in, conv_w, conv_b, a_log, dt_bias, d_skip, g_ssd, hy_conv_w, hy_conv_b,
                hy_w1, hy_b1, hy_w2, hy_b2, hy_w3, hy_freq, hy_bias):
    n0 = D + SSD_XBC
    w_bf = jnp.concatenate([w_in[:, :n0], w_in[:, n0 + SSD_H:], w_in[:, n0:n0 + SSD_H],
                            jnp.zeros((D, LANES - SSD_H), F32)], axis=1).astype(BF16)
    z, xbc, hy, dtr = even_in_proj(x, mod, g_mix, w_bf)
    small = (conv_w, conv_b[None], _pad_lanes(dt_bias), _pad_lanes(a_log), jnp.repeat(d_skip, SSD_P)[None], g_ssd[None])
    y_c, fin = ssd_mixer(xbc, dtr, z, None, *small, L=L_CTX, n_seq=N_CTX, row_off=0)
    (y_l,) = ssd_mixer(xbc, dtr, z, state.reshape(N_LAT, 2, SSD_H * SSD_P, SSD_N), *small,
                       L=L_LAT, n_seq=N_LAT, row_off=T_CTX)
    w1 = jnp.pad(hy_w1, ((0, LANES - HY_FEAT), (0, 0)))
    w3r = hy_w3.reshape(HY_HID, 4, D).transpose(1, 0, 2)
    deltas = jnp.asarray(np.linspace(HY_MIN_DECAY, HY_MAX_DECAY, D).astype(np.float32))[None]
    us = []
    for L, n_seq, off in ((L_CTX, N_CTX, 0), (L_LAT, N_LAT, T_CTX)):
        f_hi, f_lo, g_hi = dft_matrices(L)
        spectra = hyena_filter_spectra(_hyena_features(L), w1, hy_b1[None], hy_w2, hy_b2[None], hy_freq, w3r, deltas,
                                       f_hi, f_lo, L=L)
        us.append(hyena_mixer(hy, hy_conv_w, hy_conv_b[None], spectra, hy_bias, (f_hi, g_hi),
                              L=L, n_seq=n_seq, row_off=off))
    return (y_c, y_l), tuple(us), fin


def _odd_layer(x, mod, g_mix, cache_k, cache_v, cache_ckv, cache_kr, rel_bias, w_in, g_q, w_uq, g_kv, w_ukv):
    w_bf = jnp.pad(w_in, ((0, 0), (0, ODD_COLS - w_in.shape[1]))).astype(BF16)
    wuq = w_uq.reshape(MLA_QR, MLA_H, MLA_QK)
    wuq_bf = jnp.concatenate([wuq[:, :, :MLA_NOPE].reshape(MLA_QR, -1), wuq[:, :, MLA_NOPE:].reshape(MLA_QR, -1)],
                             axis=1).astype(BF16)
    wukv = w_ukv.reshape(MLA_KVR, MLA_H, MLA_NOPE + MLA_V)
    wukv_bf = jnp.concatenate([wukv[:, :, :MLA_NOPE].reshape(MLA_KVR, -1), wukv[:, :, MLA_NOPE:].reshape(MLA_KVR, -1)],
                              axis=1).astype(BF16)
    qkv, qm, ckv, kvu, kr = odd_in_proj(x, mod, g_mix, w_bf, g_q[None], wuq_bf, g_kv[None], wukv_bf, rope_tables())
    ona_c, omla_c = attn_context(qkv, qm, kvu, kr)
    ona_l = attn_neighbourhood_latent(qkv, cache_k, cache_v, neighbourhood_bias(rel_bias))
    omla_l = attn_mla_latent(qm, kvu, kr, cache_ckv, cache_kr, wukv_bf)
    return (ona_c, ona_l), (omla_c, omla_l), qkv, ckv, kr


def kernel(x_prompt, x_sample, state_ssd, cache_na_k, cache_na_v, cache_mla_ckv, cache_mla_krope, c, c_ctx, w_ada, b_ada, norm_mix, norm_ffn, norm_final, ev_w_in, ev_conv_w, ev_conv_b, ssd_A_log, ssd_dt_bias, ssd_d, ssd_norm, hy_conv_w, hy_conv_b, hy_w1, hy_b1, hy_w2, hy_b2, hy_w3, hy_freq, hy_bias, ev_w_out, od_w_in, mla_q_norm, mla_w_uq, mla_kv_norm, mla_w_ukv, na_rel_bias, od_w_out, moe_w_gr, moe_b_gr, moe_w_er, moe_b_er, moe_w_gate, moe_w_up, moe_w_down):
    x = jnp.concatenate([x_prompt.reshape(T_CTX, D), x_sample.reshape(T_LAT, D)], axis=0)
    cvec = jnp.zeros((MOD_ROWS, D), F32).at[0].set(c_ctx).at[1:1 + N_LAT].set(c)
    mod = ada_modulation(cvec, w_ada, b_ada)
    gfin = norm_final[None]

    y, u, fin = _even_layer(x, mod[0], norm_mix[0][None], state_ssd[:, 0], ev_w_in[0], ev_conv_w[0], ev_conv_b[0],
                            ssd_A_log[0], ssd_dt_bias[0], ssd_d[0], ssd_norm[0], hy_conv_w[0], hy_conv_b[0],
                            hy_w1[0], hy_b1[0], hy_w2[0], hy_b2[0], hy_w3[0], hy_freq[0], hy_bias[0])
    wr, br = _router_params(moe_w_gr[0], moe_b_gr[0], moe_w_er[0], moe_b_er[0])
    xn, h2, ids, wts, cnt3 = out_proj_router([y, u], ev_w_out[0].astype(BF16), x, mod[0], norm_ffn[0][None], wr, br)
    x = moe_block(h2, ids, wts, cnt3, xn, mod[0], gfin, moe_w_gate, moe_w_up, moe_w_down, 0, final=False)

    o_na, o_mla, qkv, ckv, kr = _odd_layer(x, mod[1], norm_mix[1][None], cache_na_k, cache_na_v, cache_mla_ckv,
                                           cache_mla_krope, na_rel_bias[0], od_w_in[0], mla_q_norm[0], mla_w_uq[0],
                                           mla_kv_norm[0], mla_w_ukv[0])
    wr, br = _router_params(moe_w_gr[1], moe_b_gr[1], moe_w_er[1], moe_b_er[1])
    xn, h2, ids, wts, cnt3 = out_proj_router([o_na, o_mla], od_w_out[0].astype(BF16), x, mod[1], norm_ffn[1][None], wr, br)
    out = moe_block(h2, ids, wts, cnt3, xn, mod[1], gfin, moe_w_gate, moe_w_up, moe_w_down, 1, final=True)

    heads = lambda a: a.reshape(N_CTX, L_CTX, NA_H, NA_D).transpose(0, 2, 1, 3)[:, None]
    return (out[:T_CTX].reshape(N_CTX, L_CTX, D),
            out[T_CTX:].reshape(N_LAT, L_LAT, D),
            fin.reshape(N_CTX, 1, 2, SSD_H, SSD_P, SSD_N),
            heads(qkv[:T_CTX, NA_W:2 * NA_W]),
            heads(qkv[:T_CTX, 2 * NA_W:3 * NA_W]),
            ckv[:T_CTX].reshape(N_CTX, 1, L_CTX, MLA_KVR),
            kr[:T_CTX, :MLA_ROPE].reshape(N_CTX, 1, L_CTX, MLA_ROPE))
```

```python
import functools
import math

import numpy as np
import jax
import jax.numpy as jnp
from jax import lax
from jax.experimental import pallas as pl
from jax.experimental.pallas import tpu as pltpu

F32 = jnp.float32
BF16 = jnp.bfloat16
HIGHEST = lax.Precision.HIGHEST

D = 1024
N_CTX, L_CTX = 16, 256
N_LAT, L_LAT = 8, 1024
T_CTX = N_CTX * L_CTX
T_LAT = N_LAT * L_LAT
T_ALL = T_CTX + T_LAT
PAST = 512
GRID_W = 64
EPS = 1e-6
NEG = -1e30

SSD_H, SSD_P, SSD_N, SSD_G = 16, 64, 128, 2
SSD_XBC = D + 2 * SSD_G * SSD_N
SSD_K = 5
CHUNK = 128

HY_K = 3
HY_BANDS = 16
HY_FEAT = 1 + 2 * HY_BANDS
HY_HID = 64
HY_MIN_DECAY = abs(math.log(1e-2) / 1.5)
HY_MAX_DECAY = abs(math.log(1e-2) / 0.3)

NA_H, NA_D = 8, 64
NA_W = NA_H * NA_D
NA_WIN_R, NA_WIN_C = 8, 16
MLA_H, MLA_QR, MLA_KVR = 8, 256, 128
MLA_NOPE, MLA_ROPE, MLA_V = 64, 32, 64
MLA_QK = MLA_NOPE + MLA_ROPE
ROPE_F = MLA_ROPE // 4

MOE_G, MOE_PG, MOE_E, MOE_F = 4, 8, 32, 256

LANES = 128
SUBLANES = 8
VMEM_LIMIT = 56 * 1024 * 1024

TM = 256
N_TILES = T_ALL // TM
CTX_TILES = T_CTX // TM
LAT_TILES_PER_SEQ = L_LAT // TM
MOD_ROWS = 16


def _cparams(sem):
    return pltpu.CompilerParams(dimension_semantics=sem, vmem_limit_bytes=VMEM_LIMIT)


def _mod_row(i):
    return jnp.where(i < CTX_TILES, 0, 1 + (i - CTX_TILES) // LAT_TILES_PER_SEQ)


def _silu(x):
    return x * jax.nn.sigmoid(x)


def _rms(x):
    return x * lax.rsqrt(jnp.mean(x * x, axis=-1, keepdims=True) + EPS)


def _ada_kernel(c_ref, w_ref, b_ref, o_ref):
    c = c_ref[...]
    o_ref[0] = jnp.dot(_silu(c), w_ref[0], precision=HIGHEST, preferred_element_type=F32) + b_ref[0]


def ada_modulation(cvec, w_ada, b_ada):
    depth = w_ada.shape[0]
    out = pl.pallas_call(
        _ada_kernel,
        grid=(depth, 6),
        in_specs=[
            pl.BlockSpec((MOD_ROWS, D), lambda l, j: (0, 0)),
            pl.BlockSpec((1, D, D), lambda l, j: (l, 0, j)),
            pl.BlockSpec((1, 1, D), lambda l, j: (l, 0, j)),
        ],
        out_specs=pl.BlockSpec((1, MOD_ROWS, D), lambda l, j: (l, 0, j)),
        out_shape=jax.ShapeDtypeStruct((depth, MOD_ROWS, 6 * D), F32),
        compiler_params=_cparams(("arbitrary", "arbitrary")),
        name="ada",
    )(cvec, w_ada, b_ada.reshape(depth, 1, 6 * D))
    return out.reshape(depth, MOD_ROWS, 6, D)


PROJ_CHUNK = 512


def _modulated(x, g_ref, mod_ref, shift_row):
    h = _rms(x) * g_ref[...]
    return h * (1.0 + mod_ref[0, shift_row + 1:shift_row + 2, :]) + mod_ref[0, shift_row:shift_row + 1, :]


def _even_in_kernel(x_ref, mod_ref, g_ref, w_ref, z_ref, xbc_ref, hy_ref, dt_ref):
    hb = _modulated(x_ref[...], g_ref, mod_ref, 0).astype(BF16)
    col = 0
    for o_ref in (z_ref, xbc_ref, hy_ref, dt_ref):
        width = o_ref.shape[1]
        for c0 in range(0, width, PROJ_CHUNK):
            c1 = min(c0 + PROJ_CHUNK, width)
            o_ref[:, c0:c1] = jnp.dot(hb, w_ref[:, col + c0:col + c1], preferred_element_type=F32)
        col += width


def even_in_proj(x, mod, g, w_bf):
    widths = (D, SSD_XBC, 3 * D, LANES)
    return pl.pallas_call(
        _even_in_kernel,
        grid=(N_TILES,),
        in_specs=[
            pl.BlockSpec((TM, D), lambda i: (i, 0)),
            pl.BlockSpec((1, 6, D), lambda i: (_mod_row(i), 0, 0)),
            pl.BlockSpec((1, D), lambda i: (0, 0)),
            pl.BlockSpec(w_bf.shape, lambda i: (0, 0)),
        ],
        out_specs=[pl.BlockSpec((TM, w), lambda i: (i, 0)) for w in widths],
        out_shape=[jax.ShapeDtypeStruct((T_ALL, w), F32) for w in widths],
        compiler_params=_cparams(("arbitrary",)),
        name="even_in",
    )(x, mod, g, w_bf)


PAD = SUBLANES


def _expand_heads(q, e_ref):
    hi = q.astype(BF16)
    r1 = q - hi.astype(F32)
    mid = r1.astype(BF16)
    lo = (r1 - mid.astype(F32)).astype(BF16)
    e = e_ref[...]
    return (jnp.dot(hi, e, preferred_element_type=F32) + jnp.dot(mid, e, preferred_element_type=F32)
            + jnp.dot(lo, e, preferred_element_type=F32))


def _head_expanders():
    h = np.arange(LANES)[:, None]
    ep = (np.arange(SSD_H * SSD_P)[None, :] // SSD_P == h).astype(np.float32)
    eh = (np.arange(SSD_H * CHUNK)[None, :] // CHUNK == h).astype(np.float32)
    return jnp.asarray(ep).astype(BF16), jnp.asarray(eh).astype(BF16)


def _ssd_kernel(*refs, L, has_init):
    if has_init:
        (xbc_ref, dt_ref, z_ref, init_ref, cw_ref, cb_ref, dtb_ref, alog_ref, dsk_ref, gs_ref, ep_ref, eh_ref,
         y_ref, xp_s, xc_s, ya_s, st_s) = refs
        fin_ref = None
    else:
        (xbc_ref, dt_ref, z_ref, cw_ref, cb_ref, dtb_ref, alog_ref, dsk_ref, gs_ref, ep_ref, eh_ref,
         y_ref, fin_ref, xp_s, xc_s, ya_s, st_s) = refs
        init_ref = None
    nc = L // CHUNK
    half = SSD_K // 2

    xp_s[0:PAD, :] = jnp.zeros((PAD, SSD_XBC), F32)
    xp_s[PAD + L:2 * PAD + L, :] = jnp.zeros((PAD, SSD_XBC), F32)
    xp_s[PAD:PAD + L, :] = xbc_ref[...]
    for c in range(nc):
        base = PAD + c * CHUNK - half
        for j in range(SSD_XBC // LANES):
            cols = slice(j * LANES, (j + 1) * LANES)
            acc = cb_ref[:, cols] + xp_s[base:base + CHUNK, cols] * cw_ref[0:1, cols]
            for k in range(1, SSD_K):
                acc = acc + xp_s[base + k:base + k + CHUNK, cols] * cw_ref[k:k + 1, cols]
            xc_s[c * CHUNK:(c + 1) * CHUNK, cols] = _silu(acc)

    row = lax.broadcasted_iota(jnp.int32, (CHUNK, CHUNK), 0)
    colm = lax.broadcasted_iota(jnp.int32, (CHUNK, CHUNK), 1)
    lane_lo = colm < SSD_P
    tri_lo = (colm <= row).astype(F32)
    tri_up = (colm >= row).astype(F32)

    for d in range(2):
        causal = (colm <= row) if d == 0 else (colm >= row)
        for j in range(SSD_H * SSD_P // CHUNK):
            if has_init:
                st_s[:, j * CHUNK:(j + 1) * CHUNK] = init_ref[0, d, j * CHUNK:(j + 1) * CHUNK, :].T
            else:
                st_s[:, j * CHUNK:(j + 1) * CHUNK] = jnp.zeros((CHUNK, CHUNK), F32)

        def chunk_body(ci, carry, d=d, causal=causal):
            c = ci if d == 0 else nc - 1 - ci
            r0 = pl.multiple_of(c * CHUNK, CHUNK)
            dt = jax.nn.softplus(dt_ref[pl.ds(r0, CHUNK), :] + dtb_ref[d:d + 1, :])
            a = dt * (-jnp.exp(alog_ref[d:d + 1, :]))
            tri = tri_lo if d == 0 else tri_up
            cs = jnp.dot(tri, a, precision=HIGHEST, preferred_element_type=F32)
            cs_t = jnp.dot(a.T, tri.T, precision=HIGHEST, preferred_element_type=F32)
            edge = cs[CHUNK - 1:CHUNK, :] if d == 0 else cs[0:1, :]
            csx = _expand_heads(cs, eh_ref)
            dtx = _expand_heads(dt, ep_ref)
            ecsx = _expand_heads(jnp.exp(cs), ep_ref)
            decx = _expand_heads(jnp.exp(edge - cs), ep_ref)
            cdecx = _expand_heads(jnp.broadcast_to(jnp.exp(edge), (SUBLANES, LANES)), ep_ref)[0:1, :]
            for g in range(SSD_G):
                bm = xc_s[pl.ds(r0, CHUNK), D + g * SSD_N:D + (g + 1) * SSD_N]
                cm = xc_s[pl.ds(r0, CHUNK), D + (SSD_G + g) * SSD_N:D + (SSD_G + g + 1) * SSD_N]
                bm_b, cm_b = bm.astype(BF16), cm.astype(BF16)
                cb = lax.dot_general(cm_b, bm_b, (((1,), (1,)), ((), ())), preferred_element_type=F32)
                bm_t = bm.T.astype(BF16)
                pairs = SSD_H // SSD_G // 2
                for pp in range(pairs):
                    p = g * pairs + pp
                    h0, h1 = 2 * p, 2 * p + 1
                    cols = slice(p * CHUNK, (p + 1) * CHUNK)
                    xs = xc_s[pl.ds(r0, CHUNK), cols]
                    xdt = xs * dtx[:, cols]
                    ms = []
                    for h in (h0, h1):
                        diff = csx[:, h * CHUNK:(h + 1) * CHUNK] - cs_t[h:h + 1, :]
                        ms.append(cb * jnp.exp(jnp.where(causal, diff, NEG)))
                    mcat = jnp.concatenate(ms, axis=1).astype(BF16)
                    xbd = jnp.concatenate([jnp.where(lane_lo, xdt, 0.0), jnp.where(lane_lo, 0.0, xdt)],
                                          axis=0).astype(BF16)
                    y_diag = jnp.dot(mcat, xbd, preferred_element_type=F32)
                    st = st_s[:, cols]
                    y_off = jnp.dot(cm_b, st.astype(BF16), preferred_element_type=F32)
                    y_off = y_off * ecsx[:, cols]
                    y = y_diag + y_off
                    if d == 0:
                        ya_s[pl.ds(r0, CHUNK), cols] = y
                    else:
                        ya_s[pl.ds(r0, CHUNK), cols] = ya_s[pl.ds(r0, CHUNK), cols] + y
                    xdd = (xdt * decx[:, cols]).astype(BF16)
                    snew = jnp.dot(bm_t, xdd, preferred_element_type=F32)
                    st_s[:, cols] = st * cdecx[:, cols] + snew
            return carry

        lax.fori_loop(0, nc, chunk_body, 0)
        if fin_ref is not None:
            for j in range(SSD_H * SSD_P // CHUNK):
                fin_ref[0, d, j * CHUNK:(j + 1) * CHUNK, :] = st_s[:, j * CHUNK:(j + 1) * CHUNK].T

    def out_body(c, carry):
        r0 = pl.multiple_of(c * CHUNK, CHUNK)
        y = ya_s[pl.ds(r0, CHUNK), :] + xc_s[pl.ds(r0, CHUNK), 0:D] * dsk_ref[...]
        y = y * _silu(z_ref[pl.ds(r0, CHUNK), :])
        y_ref[pl.ds(r0, CHUNK), :] = (_rms(y) * gs_ref[...]).astype(y_ref.dtype)
        return carry

    lax.fori_loop(0, nc, out_body, 0)


def ssd_mixer(xbc, dtr, z, init, cw, cb, dtb, alog, dsk, gs, *, L, n_seq, row_off):
    blk0 = row_off // L
    has_init = init is not None
    seq = lambda w: pl.BlockSpec((L, w), lambda b: (blk0 + b, 0))
    full = lambda arr: pl.BlockSpec(arr.shape, lambda b: (0,) * arr.ndim)
    in_specs = [seq(SSD_XBC), seq(LANES), seq(D)]
    args = [xbc, dtr, z]
    if has_init:
        in_specs.append(pl.BlockSpec((1, 2, SSD_H * SSD_P, SSD_N), lambda b: (b, 0, 0, 0)))
        args.append(init)
    small = [cw, cb, dtb, alog, dsk, gs, *_head_expanders()]
    in_specs += [full(a) for a in small]
    args += small
    out_specs = [pl.BlockSpec((L, D), lambda b: (b, 0))]
    out_shape = [jax.ShapeDtypeStruct((n_seq * L, D), BF16)]
    if not has_init:
        out_specs.append(pl.BlockSpec((1, 2, SSD_H * SSD_P, SSD_N), lambda b: (b, 0, 0, 0)))
        out_shape.append(jax.ShapeDtypeStruct((n_seq, 2, SSD_H * SSD_P, SSD_N), F32))
    return pl.pallas_call(
        functools.partial(_ssd_kernel, L=L, has_init=has_init),
        grid=(n_seq,),
        in_specs=in_specs,
        out_specs=out_specs,
        out_shape=out_shape,
        scratch_shapes=[
            pltpu.VMEM((L + 2 * PAD, SSD_XBC), F32),
            pltpu.VMEM((L, SSD_XBC), F32),
            pltpu.VMEM((L, D), F32),
            pltpu.VMEM((SSD_N, SSD_H * SSD_P), F32),
        ],
        compiler_params=_cparams(("arbitrary",)),
        name=f"ssd_{L}",
    )(*args)


HY_CB = 256


def dft_matrices(L):
    k = np.arange(L, dtype=np.int64)
    ang = ((k[:, None] * k[None, :]) % (2 * L)).astype(np.float64) * (math.pi / L)
    cosm = np.cos(ang)
    sinm = np.sin(ang)
    sinm[0] = np.where(k % 2 == 0, 1.0, -1.0)
    fwd = np.concatenate([cosm, sinm], axis=0).astype(np.float32)
    wts = np.where(k == 0, 1.0, 2.0) / (2 * L)
    inv = np.concatenate([cosm * wts[None, :], sinm.T * wts[None, :]], axis=1).astype(np.float32)
    return jnp.asarray(fwd).astype(BF16), jnp.asarray(inv).astype(BF16)


def _const_spec(arr):
    return pl.BlockSpec(arr.shape, lambda *_: (0,) * arr.ndim, pipeline_mode=pl.Buffered(1))


def _hy_filter_kernel(feat_ref, w1_ref, b1_ref, w2_ref, b2_ref, fr_ref, w3_ref, dl_ref, f_ref, o_ref, *, L):
    hp = functools.partial(jnp.dot, precision=HIGHEST, preferred_element_type=F32)
    hdn = jnp.sin(fr_ref[0:1, :] * (hp(feat_ref[...], w1_ref[...]) + b1_ref[...]))
    hdn = jnp.sin(fr_ref[1:2, :] * (hp(hdn, w2_ref[...]) + b2_ref[...]))
    rowi = lax.broadcasted_iota(jnp.int32, (L, 1), 0)
    t = rowi.astype(F32) * (1.0 / (L - 1))
    dec = jnp.exp(-t * dl_ref[...])
    first = rowi == 0
    for o in range(2):
        fwd = hp(hdn, w3_ref[2 * o]) * dec
        bwd = jnp.where(first, 0.0, hp(hdn, w3_ref[2 * o + 1]) * dec)
        ss = jnp.dot(f_ref[...], (fwd + bwd).astype(BF16), preferred_element_type=F32)
        sd = jnp.dot(f_ref[...], (fwd - bwd).astype(BF16), preferred_element_type=F32)
        hr = ss[0:L]
        o_ref[o, 0] = hr
        o_ref[o, 1] = jnp.where(first, 0.0, sd[L:2 * L])
        o_ref[o, 2] = jnp.where(first, ss[L:L + 1], hr)


def hyena_filter_spectra(feat, w1, b1, w2, b2, freq, w3r, deltas, f_bf, *, L):
    full = lambda arr: pl.BlockSpec(arr.shape, lambda j: (0,) * arr.ndim)
    return pl.pallas_call(
        functools.partial(_hy_filter_kernel, L=L),
        grid=(D // HY_CB,),
        in_specs=[full(feat), full(w1), full(b1), full(w2), full(b2), full(freq),
                  pl.BlockSpec((4, HY_HID, HY_CB), lambda j: (0, 0, j)),
                  pl.BlockSpec((1, HY_CB), lambda j: (0, j)),
                  _const_spec(f_bf)],
        out_specs=pl.BlockSpec((2, 3, L, HY_CB), lambda j: (0, 0, 0, j)),
        out_shape=jax.ShapeDtypeStruct((2, 3, L, D), F32),
        compiler_params=_cparams(("arbitrary",)),
        name=f"hy_filter_{L}",
    )(feat, w1, b1, w2, b2, freq, w3r, deltas, f_bf)


def _hyena_kernel(p0_ref, p1_ref, p2_ref, w0_ref, w1_ref, w2_ref, b0_ref, b1_ref, b2_ref, h_ref, hb_ref,
                  f_ref, g_ref, o_ref, xp_s, *, L):
    xp_s[0:PAD, :] = jnp.zeros((PAD, HY_CB), F32)
    xp_s[PAD + L:2 * PAD + L, :] = jnp.zeros((PAD, HY_CB), F32)

    def conv(p_ref, w_ref, b_ref):
        xp_s[PAD:PAD + L, :] = p_ref[...]
        acc = b_ref[...] + xp_s[PAD - 1:PAD - 1 + L, :] * w_ref[0:1, :]
        for k in range(1, HY_K):
            acc = acc + xp_s[PAD - 1 + k:PAD - 1 + k + L, :] * w_ref[k:k + 1, :]
        return acc

    u = conv(p0_ref, w0_ref, b0_ref)
    for o, (p_ref, w_ref, b_ref) in enumerate(((p1_ref, w1_ref, b1_ref), (p2_ref, w2_ref, b2_ref))):
        spec = jnp.dot(f_ref[...], u.astype(BF16), preferred_element_type=F32)
        ar, ai = spec[0:L], spec[L:2 * L]
        yr = (ar * h_ref[o, 0] - ai * h_ref[o, 1]).astype(BF16)
        yn = (ar * h_ref[o, 1] + ai * h_ref[o, 2]).astype(BF16)
        y = jnp.dot(g_ref[...], jnp.concatenate([yr, yn], axis=0), preferred_element_type=F32)
        u = conv(p_ref, w_ref, b_ref) * (y + u * hb_ref[o:o + 1, :])
    o_ref[...] = u.astype(o_ref.dtype)


def hyena_mixer(hy, conv_w, conv_b, spectra, hy_bias, mats, *, L, n_seq, row_off):
    blk0 = row_off // L
    nj = D // HY_CB
    part = lambda q: pl.BlockSpec((L, HY_CB), lambda j, b: (blk0 + b, q * nj + j))
    wpart = lambda q: pl.BlockSpec((HY_K, HY_CB), lambda j, b: (0, q * nj + j))
    bpart = lambda q: pl.BlockSpec((1, HY_CB), lambda j, b: (0, q * nj + j))
    return pl.pallas_call(
        functools.partial(_hyena_kernel, L=L),
        grid=(nj, n_seq),
        in_specs=[part(0), part(1), part(2), wpart(0), wpart(1), wpart(2), bpart(0), bpart(1), bpart(2),
                  pl.BlockSpec((2, 3, L, HY_CB), lambda j, b: (0, 0, 0, j)),
                  pl.BlockSpec((2, HY_CB), lambda j, b: (0, j))]
                 + [_const_spec(m) for m in mats],
        out_specs=pl.BlockSpec((L, HY_CB), lambda j, b: (b, j)),
        out_shape=jax.ShapeDtypeStruct((n_seq * L, D), BF16),
        scratch_shapes=[pltpu.VMEM((L + 2 * PAD, HY_CB), F32)],
        compiler_params=_cparams(("arbitrary", "arbitrary")),
        name=f"hyena_{L}",
    )(hy, hy, hy, conv_w, conv_w, conv_w, conv_b, conv_b, conv_b, spectra, hy_bias, *mats)


ROUTER_LANES = LANES
BIG_LANE = 1e9


ROW_GROUP = D // LANES


def _store_row_groups(ref, val):
    n = val.shape[0]
    for s in range(ROW_GROUP):
        ref[pl.ds(s, n, stride=ROW_GROUP), :] = val[:, s * LANES:(s + 1) * LANES]


def _load_row_groups(ref, n, s):
    return ref[pl.ds(s, n, stride=ROW_GROUP), :]


def _first_max_lane(v, lanef):
    m = jnp.max(v, axis=-1, keepdims=True)
    return m, jnp.min(jnp.where(v == m, lanef, BIG_LANE), axis=-1, keepdims=True)


def _out_router_kernel(*refs, n_in):
    a_refs = refs[:2 * n_in]
    w_ref, x_ref, mod_ref, gf_ref, wr_ref, br_ref, xo_ref, h2_ref, ids_ref, wts_ref, cnt_ref = refs[2 * n_in:]
    is_ctx = pl.program_id(0) < CTX_TILES
    acc, k0 = None, 0
    for ac_ref, al_ref in zip(a_refs[0::2], a_refs[1::2]):
        kk = ac_ref.shape[1]
        a = jnp.where(is_ctx, ac_ref[...], al_ref[...])
        part = jnp.dot(a, w_ref[k0:k0 + kk, :], preferred_element_type=F32)
        acc = part if acc is None else acc + part
        k0 += kk
    xn = x_ref[...] + mod_ref[0, 2:3, :] * acc
    xo_ref[...] = xn
    h2 = _modulated(xn, gf_ref, mod_ref, 3)
    h2_ref[...] = h2

    h_hi = h2.astype(BF16)
    h_lo = (h2 - h_hi.astype(F32)).astype(BF16)
    logits = (jnp.dot(h_hi, wr_ref[0], preferred_element_type=F32) + jnp.dot(h_lo, wr_ref[0], preferred_element_type=F32)
              + jnp.dot(h_hi, wr_ref[1], preferred_element_type=F32) + br_ref[...])
    lanef = lax.broadcasted_iota(jnp.int32, logits.shape, 1).astype(F32)
    gl = jnp.where(lanef < MOE_G, logits, NEG)
    gm, gi = _first_max_lane(gl, lanef)
    g_w = 1.0 / jnp.sum(jnp.exp(gl - gm), axis=-1, keepdims=True)
    lo = MOE_G + MOE_PG * gi
    el = jnp.where((lanef >= lo) & (lanef < lo + MOE_PG), logits, NEG)
    m1, e1 = _first_max_lane(el, lanef)
    m2, e2 = _first_max_lane(jnp.where(lanef == e1, NEG, el), lanef)
    p2 = jnp.exp(m2 - m1)
    w1 = g_w / (1.0 + p2)
    ids_ref[...] = jnp.where(lanef == 0, e1 - MOE_G, jnp.where(lanef == 1, e2 - MOE_G, 0.0)).astype(jnp.int32)
    wts_ref[...] = jnp.where(lanef == 0, w1, jnp.where(lanef == 1, w1 * p2, 0.0))
    chosen = ((lanef == e1 - MOE_G) | (lanef == e2 - MOE_G)).astype(F32)
    cnt_ref[0] = jnp.sum(chosen, axis=0, keepdims=True).astype(jnp.int32)


def out_proj_router(acts, w_bf, x, mod, gf, wr, br):
    tile = lambda w: pl.BlockSpec((TM, w), lambda i: (i, 0))
    full = lambda arr: pl.BlockSpec(arr.shape, lambda i: (0,) * arr.ndim)
    pair = lambda w: [pl.BlockSpec((TM, w), lambda i: (jnp.minimum(i, CTX_TILES - 1), 0)),
                      pl.BlockSpec((TM, w), lambda i: (jnp.maximum(i - CTX_TILES, 0), 0))]
    return pl.pallas_call(
        functools.partial(_out_router_kernel, n_in=len(acts)),
        grid=(N_TILES,),
        in_specs=[s for a in acts for s in pair(a[0].shape[1])] + [full(w_bf), tile(D),
                  pl.BlockSpec((1, 6, D), lambda i: (_mod_row(i), 0, 0)), full(gf), full(wr), full(br)],
        out_specs=[tile(D), tile(D), tile(ROUTER_LANES), tile(ROUTER_LANES),
                   pl.BlockSpec((1, 1, ROUTER_LANES), lambda i: (i, 0, 0))],
        out_shape=[jax.ShapeDtypeStruct((T_ALL, D), F32), jax.ShapeDtypeStruct((T_ALL, D), F32),
                   jax.ShapeDtypeStruct((T_ALL, ROUTER_LANES), jnp.int32),
                   jax.ShapeDtypeStruct((T_ALL, ROUTER_LANES), F32),
                   jax.ShapeDtypeStruct((N_TILES, 1, ROUTER_LANES), jnp.int32)],
        compiler_params=_cparams(("arbitrary",)),
        name="out_router",
    )(*[part for a in acts for part in a], w_bf, x, mod, gf, wr, br)


N_ASSIGN = 2 * T_ALL
MOE_TILES = N_ASSIGN // TM + MOE_E
N_SLOTS = MOE_TILES * TM


def route_tables(cnt3):
    cnt = cnt3[:, 0, :MOE_E]
    total = jnp.sum(cnt, axis=0)
    padded = (total + TM - 1) // TM * TM
    ends = jnp.cumsum(padded)
    gdst = (ends - padded)[None, :] + jnp.cumsum(cnt, axis=0) - cnt
    loc = jnp.cumsum(cnt, axis=1) - cnt
    starts = jnp.arange(MOE_TILES, dtype=jnp.int32) * TM
    tile_expert = jnp.minimum(jnp.sum((ends[None, :] <= starts[:, None]).astype(jnp.int32), axis=1), MOE_E - 1)
    n_used = (ends[-1] // TM).astype(jnp.int32).reshape(1)
    return cnt, loc, gdst, ends, tile_expert, n_used


RUN_BITS = (2 * TM).bit_length()


def _dispatch_kernel(cnt_s, loc_s, gdst_s, ends_s, h_ref, ids_ref, gcol_ref, xs_ref, dest_ref, srt, zbuf, sem, zsem):
    i = pl.program_id(0)
    slot = i % 2
    n_rows = 2 * TM

    @pl.when(i == 0)
    def _():
        zbuf[...] = jnp.zeros(zbuf.shape, zbuf.dtype)
        n_used = ends_s[MOE_E - 1] // TM
        for phase in ("start", "wait"):
            def tail(t, c, phase=phase):
                dst = pl.multiple_of(t * (TM * ROW_GROUP), TM * ROW_GROUP)
                cp = pltpu.make_async_copy(zbuf, xs_ref.at[pl.ds(dst, TM * ROW_GROUP), :], zsem)
                cp.start() if phase == "start" else cp.wait()
                return c

            lax.fori_loop(n_used, MOE_TILES, tail, 0)
            for e in range(MOE_E):
                end = ends_s[e]
                prev = ends_s[e - 1] if e > 0 else 0

                @pl.when(end > prev)
                def _(end=end, phase=phase):
                    dst = pl.multiple_of((end - TM) * ROW_GROUP, TM * ROW_GROUP)
                    cp = pltpu.make_async_copy(zbuf, xs_ref.at[pl.ds(dst, TM * ROW_GROUP), :], zsem)
                    cp.start() if phase == "start" else cp.wait()

    idt = ids_ref[...].astype(F32).T
    sub = lax.broadcasted_iota(jnp.int32, (LANES, TM), 0).astype(F32)
    m0 = (sub == idt[0:1, :]).astype(F32)
    m1 = (sub == idt[1:2, :]).astype(F32)
    mt = (m0 + m1).astype(BF16)
    tr = lax.broadcasted_iota(jnp.int32, (TM, TM), 0)
    tc = lax.broadcasted_iota(jnp.int32, (TM, TM), 1)
    earlier = jnp.dot(mt, (tr < tc).astype(BF16), preferred_element_type=F32)
    er = lax.broadcasted_iota(jnp.int32, (LANES, LANES), 0)
    ec = lax.broadcasted_iota(jnp.int32, (LANES, LANES), 1)
    below = jnp.dot((ec < er).astype(BF16), mt, preferred_element_type=F32)
    local = jnp.sum(below, axis=1, keepdims=True) + earlier
    glob = gcol_ref[0] + earlier
    pos0 = jnp.sum(m0 * local, axis=0, keepdims=True)
    pos1 = jnp.sum(m1 * local, axis=0, keepdims=True)
    dest_ref[0] = jnp.concatenate([jnp.sum(m0 * glob, axis=0, keepdims=True),
                                   jnp.sum(m1 * glob, axis=0, keepdims=True)], axis=0).astype(jnp.int32)

    srow = lax.broadcasted_iota(jnp.int32, (n_rows, TM), 0).astype(F32)
    perm = jnp.where((srow == pos0) | (srow == pos1), 1.0, 0.0).astype(BF16)
    _store_row_groups(srt.at[slot], jnp.dot(perm, h_ref[...].astype(BF16), preferred_element_type=F32))

    for e in range(MOE_E):
        n, s0, d0 = cnt_s[0, 0, e], loc_s[0, 0, e], gdst_s[0, 0, e]
        for b in reversed(range(RUN_BITS)):
            size = 1 << b
            off = (n >> (b + 1)) << (b + 1)

            @pl.when(((n >> b) & 1) == 1)
            def _(size=size, off=off, s0=s0, d0=d0):
                src = pl.multiple_of((s0 + off) * ROW_GROUP, ROW_GROUP)
                dst = pl.multiple_of((d0 + off) * ROW_GROUP, ROW_GROUP)
                pltpu.make_async_copy(srt.at[slot, pl.ds(src, size * ROW_GROUP), :],
                                      xs_ref.at[pl.ds(dst, size * ROW_GROUP), :], sem.at[slot]).start()

    def wait(s):
        pltpu.make_async_copy(srt.at[s], xs_ref.at[pl.ds(0, n_rows * ROW_GROUP), :], sem.at[s]).wait()

    @pl.when(i > 0)
    def _():
        wait(1 - slot)

    @pl.when(i == N_TILES - 1)
    def _():
        wait(slot)


def dispatch_rows(h2, ids, cnt, loc, gdst, ends):
    tab = lambda: pl.BlockSpec((1, 1, MOE_E), lambda i: (i, 0, 0), memory_space=pltpu.SMEM)
    gcol = jnp.pad(gdst.astype(F32), ((0, 0), (0, LANES - MOE_E)))[:, :, None]
    return pl.pallas_call(
        _dispatch_kernel,
        grid=(N_TILES,),
        in_specs=[tab(), tab(), tab(), pl.BlockSpec(memory_space=pltpu.SMEM),
                  pl.BlockSpec((TM, D), lambda i: (i, 0)), pl.BlockSpec((TM, ROUTER_LANES), lambda i: (i, 0)),
                  pl.BlockSpec((1, LANES, 1), lambda i: (i, 0, 0))],
        out_specs=[pl.BlockSpec(memory_space=pl.ANY), pl.BlockSpec((1, 2, TM), lambda i: (i, 0, 0))],
        out_shape=[jax.ShapeDtypeStruct((N_SLOTS * ROW_GROUP, LANES), F32),
                   jax.ShapeDtypeStruct((N_TILES, 2, TM), jnp.int32)],
        scratch_shapes=[pltpu.VMEM((2, 2 * TM * ROW_GROUP, LANES), F32), pltpu.VMEM((TM * ROW_GROUP, LANES), F32),
                        pltpu.SemaphoreType.DMA((2,)), pltpu.SemaphoreType.DMA(())],
        compiler_params=_cparams(("arbitrary",)),
        name="moe_dispatch",
    )(cnt.reshape(N_TILES, 1, MOE_E), loc.reshape(N_TILES, 1, MOE_E), gdst.reshape(N_TILES, 1, MOE_E),
      ends, h2, ids, gcol)


DMA_UNROLL = 8


def _start_group_gather(src_hbm, idx_ref, n, dst_ref, sem):
    def body(j, c):
        for u in range(DMA_UNROLL):
            r = j * DMA_UNROLL + u
            src = pl.multiple_of(idx_ref[0, 0, r] * ROW_GROUP, ROW_GROUP)
            dst = pl.multiple_of(r * ROW_GROUP, ROW_GROUP)
            pltpu.make_async_copy(src_hbm.at[pl.ds(src, ROW_GROUP), :], dst_ref.at[pl.ds(dst, ROW_GROUP), :],
                                  sem).start(priority=u % 2)
        return c

    lax.fori_loop(0, n // DMA_UNROLL, body, 0)


def _wait_group_gather(src_hbm, dst_ref, sem):
    pltpu.make_async_copy(src_hbm.at[pl.ds(0, dst_ref.shape[0]), :], dst_ref, sem).wait()


def _experts_kernel(te_ref, nu_ref, x_ref, wg_ref, wu_ref, wd_ref, o_ref, xcat):
    i = pl.program_id(0)

    @pl.when(i < nu_ref[0])
    def _():
        for s in range(ROW_GROUP):
            xcat[:, s * LANES:(s + 1) * LANES] = _load_row_groups(x_ref, TM, s).astype(BF16)
        x = xcat[...]
        g = jnp.dot(x, wg_ref[0, 0].astype(BF16), preferred_element_type=F32)
        u = jnp.dot(x, wu_ref[0, 0].astype(BF16), preferred_element_type=F32)
        hid = (_silu(g) * u).astype(BF16)
        _store_row_groups(o_ref, jnp.dot(hid, wd_ref[0, 0].astype(BF16), preferred_element_type=F32))

    @pl.when(i >= nu_ref[0])
    def _():
        o_ref[...] = jnp.zeros(o_ref.shape, o_ref.dtype)


def grouped_experts(xs, w_gate, w_up, w_down, tile_expert, n_used, layer):
    wspec = lambda a, b: pl.BlockSpec((1, 1, a, b), lambda i, te, nu: (layer, te[i], 0, 0))
    return pl.pallas_call(
        _experts_kernel,
        grid_spec=pltpu.PrefetchScalarGridSpec(
            num_scalar_prefetch=2,
            grid=(MOE_TILES,),
            in_specs=[pl.BlockSpec((TM * ROW_GROUP, LANES), lambda i, te, nu: (jnp.minimum(i, nu[0] - 1), 0)),
                      wspec(D, MOE_F), wspec(D, MOE_F), wspec(MOE_F, D)],
            out_specs=pl.BlockSpec((TM * ROW_GROUP, LANES), lambda i, te, nu: (i, 0)),
            scratch_shapes=[pltpu.VMEM((TM, D), BF16)],
        ),
        out_shape=jax.ShapeDtypeStruct((N_SLOTS * ROW_GROUP, LANES), F32),
        compiler_params=_cparams(("arbitrary",)),
        name="moe_experts",
    )(tile_expert, n_used, xs, w_gate, w_up, w_down)


def _combine_kernel(cur_ref, nxt_ref, ys_hbm, x_ref, wts_ref, mod_ref, gfin_ref, o_ref, buf, sem, *, final):
    i = pl.program_id(0)
    slot = i % 2

    @pl.when(i == 0)
    def _():
        _start_group_gather(ys_hbm, cur_ref, 2 * TM, buf.at[0], sem.at[0])

    @pl.when(i + 1 < N_TILES)
    def _():
        _start_group_gather(ys_hbm, nxt_ref, 2 * TM, buf.at[1 - slot], sem.at[1 - slot])

    _wait_group_gather(ys_hbm, buf.at[slot], sem.at[slot])
    w0, w1 = wts_ref[:, 0:1], wts_ref[:, 1:2]
    for s in range(ROW_GROUP):
        cols = slice(s * LANES, (s + 1) * LANES)
        y0 = buf[slot, pl.ds(s, TM, stride=ROW_GROUP), :]
        y1 = buf[slot, pl.ds(TM * ROW_GROUP + s, TM, stride=ROW_GROUP), :]
        o_ref[:, cols] = x_ref[:, cols] + mod_ref[0, 5:6, cols] * (w0 * y0 + w1 * y1)
    if final:
        o_ref[...] = _rms(o_ref[...]) * gfin_ref[...]


def moe_combine(ys, dest, x, wts, mod, gfin, *, final):
    tile = lambda w: pl.BlockSpec((TM, w), lambda i: (i, 0))
    idx = lambda f: pl.BlockSpec((1, 1, 2 * TM), lambda i: (f(i), 0, 0), memory_space=pltpu.SMEM)
    dest3 = dest.reshape(N_TILES, 1, 2 * TM)
    return pl.pallas_call(
        functools.partial(_combine_kernel, final=final),
        grid=(N_TILES,),
        in_specs=[idx(lambda i: i), idx(lambda i: jnp.minimum(i + 1, N_TILES - 1)),
                  pl.BlockSpec(memory_space=pl.ANY), tile(D), tile(ROUTER_LANES),
                  pl.BlockSpec((1, 6, D), lambda i: (_mod_row(i), 0, 0)),
                  pl.BlockSpec((1, D), lambda i: (0, 0))],
        out_specs=tile(D),
        out_shape=jax.ShapeDtypeStruct((T_ALL, D), F32),
        scratch_shapes=[pltpu.VMEM((2, 2 * TM * ROW_GROUP, LANES), F32), pltpu.SemaphoreType.DMA((2,))],
        compiler_params=_cparams(("arbitrary",)),
        name="moe_combine",
    )(dest3, dest3, ys, x, wts, mod, gfin)


ODD_COLS = 2048
ROPE_Q = MLA_H * MLA_ROPE
ROPE_SHIFT = ROPE_F


def rope_tables():
    t = np.arange(L_LAT)
    pos = np.stack([t // GRID_W, t % GRID_W], axis=1).astype(np.float64)
    inv = 10000.0 ** (-np.arange(ROPE_F, dtype=np.float64) / ROPE_F)
    lane = np.arange(ROPE_Q) % MLA_ROPE
    axis = lane // (2 * ROPE_F)
    first = (lane % (2 * ROPE_F)) < ROPE_F
    ang = pos[:, axis] * inv[lane % ROPE_F][None, :]
    cos, sin = np.cos(ang), np.sin(ang)
    tabs = [cos, np.where(first[None, :], -sin, 0.0), np.where(first[None, :], 0.0, sin)]
    ident = [np.ones((1, TM, ROPE_Q)), np.zeros((1, TM, ROPE_Q)), np.zeros((1, TM, ROPE_Q))]
    return [jnp.asarray(np.concatenate([i, tb.reshape(LAT_TILES_PER_SEQ, TM, ROPE_Q)], axis=0).astype(np.float32))
            for i, tb in zip(ident, tabs)]


def _rope(x, c, a, b):
    n = x.shape[1]
    return x * c[:, :n] + pltpu.roll(x, n - ROPE_SHIFT, 1) * a[:, :n] + pltpu.roll(x, ROPE_SHIFT, 1) * b[:, :n]


def _odd_in_kernel(x_ref, mod_ref, g_ref, w_ref, gq_ref, wuq_ref, gkv_ref, wukv_ref, rc_ref, ra_ref, rb_ref,
                   qkv_ref, qm_ref, ckv_ref, kvu_ref, kr_ref):
    hb = _modulated(x_ref[...], g_ref, mod_ref, 0).astype(BF16)
    for c0 in range(0, 3 * NA_W, PROJ_CHUNK):
        qkv_ref[:, c0:c0 + PROJ_CHUNK] = jnp.dot(hb, w_ref[:, c0:c0 + PROJ_CHUNK], preferred_element_type=F32)
    rest = jnp.dot(hb, w_ref[:, 3 * NA_W:ODD_COLS], preferred_element_type=F32)
    rc, ra, rb = rc_ref[0], ra_ref[0], rb_ref[0]
    qd = (_rms(rest[:, 0:MLA_QR]) * gq_ref[...]).astype(BF16)
    qm = jnp.dot(qd, wuq_ref[...], preferred_element_type=F32)
    qm_ref[:, 0:MLA_H * MLA_NOPE] = qm[:, 0:MLA_H * MLA_NOPE]
    qm_ref[:, MLA_H * MLA_NOPE:] = _rope(qm[:, MLA_H * MLA_NOPE:], rc, ra, rb)
    ckv = _rms(rest[:, MLA_QR:MLA_QR + MLA_KVR]) * gkv_ref[...]
    ckv_ref[...] = ckv
    kvu_ref[...] = jnp.dot(ckv.astype(BF16), wukv_ref[...], preferred_element_type=F32)
    kr_ref[...] = _rope(rest[:, MLA_QR + MLA_KVR:], rc, ra, rb)


def odd_in_proj(x, mod, g, w_bf, gq, wuq_bf, gkv, wukv_bf, tabs):
    tile = lambda w: pl.BlockSpec((TM, w), lambda i: (i, 0))
    full = lambda arr: pl.BlockSpec(arr.shape, lambda i: (0,) * arr.ndim)
    tab = pl.BlockSpec((1, TM, ROPE_Q),
                       lambda i: (jnp.where(i < CTX_TILES, 0, 1 + (i - CTX_TILES) % LAT_TILES_PER_SEQ), 0, 0))
    widths = (3 * NA_W, MLA_H * MLA_QK, MLA_KVR, MLA_H * (MLA_NOPE + MLA_V), LANES)
    return pl.pallas_call(
        _odd_in_kernel,
        grid=(N_TILES,),
        in_specs=[tile(D), pl.BlockSpec((1, 6, D), lambda i: (_mod_row(i), 0, 0)), full(g), full(w_bf),
                  full(gq), full(wuq_bf), full(gkv), full(wukv_bf), tab, tab, tab],
        out_specs=[tile(w) for w in widths],
        out_shape=[jax.ShapeDtypeStruct((T_ALL, w), F32) for w in widths],
        compiler_params=_cparams(("arbitrary",)),
        name="odd_in",
    )(x, mod, g, w_bf, gq, wuq_bf, gkv, wukv_bf, *tabs)


LOG2E = math.log2(math.e)
NA_QSCALE = NA_D ** -0.5 * LOG2E
MLA_QSCALE = MLA_QK ** -0.5 * LOG2E
NT = (((1,), (1,)), ((), ()))


def _softmax_pv(scores, values):
    m = functools.reduce(jnp.maximum, [jnp.max(s, axis=-1, keepdims=True) for s in scores])
    ps = [jnp.exp2(s - m) for s in scores]
    den = functools.reduce(jnp.add, [jnp.sum(p, axis=-1, keepdims=True) for p in ps])
    acc = functools.reduce(jnp.add, [jnp.dot(p.astype(BF16), v, preferred_element_type=F32) for p, v in zip(ps, values)])
    return acc / den


def _head(ref_or_val, h, width, base=0):
    return ref_or_val[:, base + h * width:base + (h + 1) * width]


def _mla_scores(qm, h, kn, kr):
    qn = (_head(qm, h, MLA_NOPE) * MLA_QSCALE).astype(BF16)
    qr = (_head(qm, h, MLA_ROPE, MLA_H * MLA_NOPE) * MLA_QSCALE).astype(BF16)
    return (lax.dot_general(qn, kn, NT, preferred_element_type=F32)
            + lax.dot_general(qr, kr, NT, preferred_element_type=F32))


def _attn_ctx_kernel(qkv_ref, qm_ref, kvu_ref, kr_ref, ona_ref, omla_ref):
    kr = kr_ref[:, 0:MLA_ROPE].astype(BF16)
    for h in range(NA_H):
        q = (_head(qkv_ref, h, NA_D) * NA_QSCALE).astype(BF16)
        k = _head(qkv_ref, h, NA_D, NA_W).astype(BF16)
        v = _head(qkv_ref, h, NA_D, 2 * NA_W).astype(BF16)
        s = lax.dot_general(q, k, NT, preferred_element_type=F32)
        ona_ref[:, h * NA_D:(h + 1) * NA_D] = _softmax_pv([s], [v]).astype(ona_ref.dtype)
    for h in range(MLA_H):
        kn = _head(kvu_ref, h, MLA_NOPE).astype(BF16)
        v = _head(kvu_ref, h, MLA_V, MLA_H * MLA_NOPE).astype(BF16)
        s = _mla_scores(qm_ref, h, kn, kr)
        omla_ref[:, h * MLA_V:(h + 1) * MLA_V] = _softmax_pv([s], [v]).astype(omla_ref.dtype)


def attn_context(qkv, qm, kvu, kr):
    seq = lambda w: pl.BlockSpec((L_CTX, w), lambda b: (b, 0))
    return pl.pallas_call(
        _attn_ctx_kernel,
        grid=(N_CTX,),
        in_specs=[seq(3 * NA_W), seq(MLA_H * MLA_QK), seq(MLA_H * (MLA_NOPE + MLA_V)), seq(LANES)],
        out_specs=[seq(NA_W), seq(MLA_H * MLA_V)],
        out_shape=[jax.ShapeDtypeStruct((T_CTX, NA_W), BF16), jax.ShapeDtypeStruct((T_CTX, MLA_H * MLA_V), BF16)],
        compiler_params=_cparams(("arbitrary",)),
        name="attn_ctx",
    )(qkv, qm, kvu, kr)


N_DR = 2 * NA_WIN_R - 1
GRID_ROWS = L_LAT // GRID_W


def _na_bias_kernel(t_ref, o_ref):
    r = pl.program_id(1)
    r0 = jnp.clip(r - NA_WIN_R // 2, 0, GRID_ROWS - NA_WIN_R)
    for kr in range(GRID_ROWS):
        in_window = (kr >= r0) & (kr < r0 + NA_WIN_R)
        dr = jnp.clip(kr - r + NA_WIN_R - 1, 0, N_DR - 1)
        o_ref[0, :, kr * GRID_W:(kr + 1) * GRID_W] = jnp.where(in_window, t_ref[0, dr], NEG)


def neighbourhood_bias(rel_bias):
    c = np.arange(GRID_W)
    c0 = np.clip(c - NA_WIN_C // 2, 0, GRID_W - NA_WIN_C)
    col_ok = (c[None, :] >= c0[:, None]) & (c[None, :] < c0[:, None] + NA_WIN_C)
    dc = np.clip(c[None, :] - c[:, None], -(NA_WIN_C - 1), NA_WIN_C - 1) + NA_WIN_C - 1
    sel_c = (dc[:, :, None] == np.arange(2 * NA_WIN_C - 1)).astype(np.float32)
    t = jnp.einsum("hdj,qcj->hdqc", rel_bias.astype(F32), jnp.asarray(sel_c), precision=HIGHEST)
    t = jnp.where(jnp.asarray(col_ok)[None, None], t * LOG2E, NEG)
    return pl.pallas_call(
        _na_bias_kernel,
        grid=(NA_H, GRID_ROWS),
        in_specs=[pl.BlockSpec((1, N_DR, GRID_W, GRID_W), lambda h, r: (h, 0, 0, 0))],
        out_specs=pl.BlockSpec((1, GRID_W, L_LAT), lambda h, r: (h, r, 0)),
        out_shape=jax.ShapeDtypeStruct((NA_H, L_LAT, L_LAT), F32),
        compiler_params=_cparams(("arbitrary", "arbitrary")),
        name="na_bias",
    )(t)


def _na_lat_kernel(q_ref, k_ref, v_ref, kc_ref, vc_ref, b_ref, o_ref):
    for h in range(NA_H):
        q = (_head(q_ref, h, NA_D) * NA_QSCALE).astype(BF16)
        k = _head(k_ref, h, NA_D).astype(BF16)
        v = _head(v_ref, h, NA_D).astype(BF16)
        kc = kc_ref[0, 0, h].astype(BF16)
        vc = vc_ref[0, 0, h].astype(BF16)
        s1 = lax.dot_general(q, k, NT, preferred_element_type=F32) + b_ref[h]
        s2 = lax.dot_general(q, kc, NT, preferred_element_type=F32)
        o_ref[:, h * NA_D:(h + 1) * NA_D] = _softmax_pv([s1, s2], [v, vc]).astype(o_ref.dtype)


def attn_neighbourhood_latent(qkv, cache_k, cache_v, bias):
    nq = L_LAT // TM
    t0 = T_CTX // TM
    s0 = T_CTX // L_LAT
    cache = pl.BlockSpec((1, 1, NA_H, PAST, NA_D), lambda qt, b: (b, 0, 0, 0, 0))
    return pl.pallas_call(
        _na_lat_kernel,
        grid=(nq, N_LAT),
        in_specs=[pl.BlockSpec((TM, NA_W), lambda qt, b: (t0 + b * nq + qt, 0)),
                  pl.BlockSpec((L_LAT, NA_W), lambda qt, b: (s0 + b, 1)),
                  pl.BlockSpec((L_LAT, NA_W), lambda qt, b: (s0 + b, 2)),
                  cache, cache,
                  pl.BlockSpec((NA_H, TM, L_LAT), lambda qt, b: (0, qt, 0))],
        out_specs=pl.BlockSpec((TM, NA_W), lambda qt, b: (b * nq + qt, 0)),
        out_shape=jax.ShapeDtypeStruct((T_LAT, NA_W), BF16),
        compiler_params=_cparams(("arbitrary", "arbitrary")),
        name="attn_na_lat",
    )(qkv, qkv, qkv, cache_k, cache_v, bias)


def _mla_lat_kernel(qm_ref, kvu_ref, kr_ref, ckv_ref, krc_ref, wukv_ref, o_ref):
    kvc = jnp.dot(ckv_ref[0, 0].astype(BF16), wukv_ref[...], preferred_element_type=F32)
    kr = kr_ref[:, 0:MLA_ROPE].astype(BF16)
    krc = krc_ref[0, 0].astype(BF16)
    for h in range(MLA_H):
        kn = _head(kvu_ref, h, MLA_NOPE).astype(BF16)
        v = _head(kvu_ref, h, MLA_V, MLA_H * MLA_NOPE).astype(BF16)
        knc = _head(kvc, h, MLA_NOPE).astype(BF16)
        vc = _head(kvc, h, MLA_V, MLA_H * MLA_NOPE).astype(BF16)
        s1 = _mla_scores(qm_ref, h, kn, kr)
        s2 = _mla_scores(qm_ref, h, knc, krc)
        o_ref[:, h * MLA_V:(h + 1) * MLA_V] = _softmax_pv([s1, s2], [v, vc]).astype(o_ref.dtype)


def attn_mla_latent(qm, kvu, kr, cache_ckv, cache_krope, wukv_bf):
    nq = L_LAT // TM
    t0 = T_CTX // TM
    s0 = T_CTX // L_LAT
    return pl.pallas_call(
        _mla_lat_kernel,
        grid=(nq, N_LAT),
        in_specs=[pl.BlockSpec((TM, MLA_H * MLA_QK), lambda qt, b: (t0 + b * nq + qt, 0)),
                  pl.BlockSpec((L_LAT, MLA_H * (MLA_NOPE + MLA_V)), lambda qt, b: (s0 + b, 0)),
                  pl.BlockSpec((L_LAT, LANES), lambda qt, b: (s0 + b, 0)),
                  pl.BlockSpec((1, 1, PAST, MLA_KVR), lambda qt, b: (b, 0, 0, 0)),
                  pl.BlockSpec((1, 1, PAST, MLA_ROPE), lambda qt, b: (b, 0, 0, 0)),
                  pl.BlockSpec(wukv_bf.shape, lambda qt, b: (0, 0))],
        out_specs=pl.BlockSpec((TM, MLA_H * MLA_V), lambda qt, b: (b * nq + qt, 0)),
        out_shape=jax.ShapeDtypeStruct((T_LAT, MLA_H * MLA_V), BF16),
        compiler_params=_cparams(("arbitrary", "arbitrary")),
        name="attn_mla_lat",
    )(qm, kvu, kr, cache_ckv, cache_krope, wukv_bf)


def moe_block(h2, ids, wts, cnt3, x, mod, gfin, w_gate, w_up, w_down, layer, *, final):
    cnt, loc, gdst, ends, tile_expert, n_used = route_tables(cnt3)
    xs, dest = dispatch_rows(h2, ids, cnt, loc, gdst, ends)
    ys = grouped_experts(xs, w_gate, w_up, w_down, tile_expert, n_used, layer)
    return moe_combine(ys, dest, x, wts, mod, gfin, final=final)


def _pad_lanes(a):
    return jnp.pad(a, ((0, 0), (0, LANES - a.shape[1])))


def _hyena_features(L):
    t = np.linspace(0.0, 1.0, L)[:, None]
    w = 2.0 * math.pi * np.arange(L) / L
    bands = np.linspace(1e-4, HY_BANDS - 1, HY_BANDS)
    ang = w[:, None] * bands[None]
    feat = np.concatenate([t, np.cos(ang), -np.sin(ang)], axis=-1)
    return jnp.asarray(np.pad(feat, ((0, 0), (0, LANES - HY_FEAT))).astype(np.float32))


def _router_params(w_gr, b_gr, w_er, b_er):
    wr = _pad_lanes(jnp.concatenate([w_gr, w_er], axis=1))
    br = _pad_lanes(jnp.concatenate([b_gr, b_er])[None])
    wr_hi = wr.astype(BF16)
    wr_lo = (wr - wr_hi.astype(F32)).astype(BF16)
    return jnp.stack([wr_hi, wr_lo]), br


def _even_layer(x, mod, g_mix, state, w_in, conv_w, conv_b, a_log, dt_bias, d_skip, g_ssd, hy_conv_w, hy_conv_b,
                hy_w1, hy_b1, hy_w2, hy_b2, hy_w3, hy_freq, hy_bias):
    n0 = D + SSD_XBC
    w_bf = jnp.concatenate([w_in[:, :n0], w_in[:, n0 + SSD_H:], w_in[:, n0:n0 + SSD_H],
                            jnp.zeros((D, LANES - SSD_H), F32)], axis=1).astype(BF16)
    z, xbc, hy, dtr = even_in_proj(x, mod, g_mix, w_bf)
    small = (conv_w, conv_b[None], _pad_lanes(dt_bias), _pad_lanes(a_log), jnp.repeat(d_skip, SSD_P)[None], g_ssd[None])
    y_c, fin = ssd_mixer(xbc, dtr, z, None, *small, L=L_CTX, n_seq=N_CTX, row_off=0)
    (y_l,) = ssd_mixer(xbc, dtr, z, state.reshape(N_LAT, 2, SSD_H * SSD_P, SSD_N), *small,
                       L=L_LAT, n_seq=N_LAT, row_off=T_CTX)
    w1 = jnp.pad(hy_w1, ((0, LANES - HY_FEAT), (0, 0)))
    w3r = hy_w3.reshape(HY_HID, 4, D).transpose(1, 0, 2)
    deltas = jnp.asarray(np.linspace(HY_MIN_DECAY, HY_MAX_DECAY, D).astype(np.float32))[None]
    us = []
    for L, n_seq, off in ((L_CTX, N_CTX, 0), (L_LAT, N_LAT, T_CTX)):
        f_bf, g_bf = dft_matrices(L)
        spectra = hyena_filter_spectra(_hyena_features(L), w1, hy_b1[None], hy_w2, hy_b2[None], hy_freq, w3r, deltas,
                                       f_bf, L=L)
        us.append(hyena_mixer(hy, hy_conv_w, hy_conv_b[None], spectra, hy_bias, (f_bf, g_bf),
                              L=L, n_seq=n_seq, row_off=off))
    return (y_c, y_l), tuple(us), fin


def _odd_layer(x, mod, g_mix, cache_k, cache_v, cache_ckv, cache_kr, rel_bias, w_in, g_q, w_uq, g_kv, w_ukv):
    w_bf = jnp.pad(w_in, ((0, 0), (0, ODD_COLS - w_in.shape[1]))).astype(BF16)
    wuq = w_uq.reshape(MLA_QR, MLA_H, MLA_QK)
    wuq_bf = jnp.concatenate([wuq[:, :, :MLA_NOPE].reshape(MLA_QR, -1), wuq[:, :, MLA_NOPE:].reshape(MLA_QR, -1)],
                             axis=1).astype(BF16)
    wukv = w_ukv.reshape(MLA_KVR, MLA_H, MLA_NOPE + MLA_V)
    wukv_bf = jnp.concatenate([wukv[:, :, :MLA_NOPE].reshape(MLA_KVR, -1), wukv[:, :, MLA_NOPE:].reshape(MLA_KVR, -1)],
                              axis=1).astype(BF16)
    qkv, qm, ckv, kvu, kr = odd_in_proj(x, mod, g_mix, w_bf, g_q[None], wuq_bf, g_kv[None], wukv_bf, rope_tables())
    ona_c, omla_c = attn_context(qkv, qm, kvu, kr)
    ona_l = attn_neighbourhood_latent(qkv, cache_k, cache_v, neighbourhood_bias(rel_bias))
    omla_l = attn_mla_latent(qm, kvu, kr, cache_ckv, cache_kr, wukv_bf)
    return (ona_c, ona_l), (omla_c, omla_l), qkv, ckv, kr


def kernel(x_prompt, x_sample, state_ssd, cache_na_k, cache_na_v, cache_mla_ckv, cache_mla_krope, c, c_ctx, w_ada, b_ada, norm_mix, norm_ffn, norm_final, ev_w_in, ev_conv_w, ev_conv_b, ssd_A_log, ssd_dt_bias, ssd_d, ssd_norm, hy_conv_w, hy_conv_b, hy_w1, hy_b1, hy_w2, hy_b2, hy_w3, hy_freq, hy_bias, ev_w_out, od_w_in, mla_q_norm, mla_w_uq, mla_kv_norm, mla_w_ukv, na_rel_bias, od_w_out, moe_w_gr, moe_b_gr, moe_w_er, moe_b_er, moe_w_gate, moe_w_up, moe_w_down):
    x = jnp.concatenate([x_prompt.reshape(T_CTX, D), x_sample.reshape(T_LAT, D)], axis=0)
    cvec = jnp.zeros((MOD_ROWS, D), F32).at[0].set(c_ctx).at[1:1 + N_LAT].set(c)
    mod = ada_modulation(cvec, w_ada, b_ada)
    gfin = norm_final[None]

    y, u, fin = _even_layer(x, mod[0], norm_mix[0][None], state_ssd[:, 0], ev_w_in[0], ev_conv_w[0], ev_conv_b[0],
                            ssd_A_log[0], ssd_dt_bias[0], ssd_d[0], ssd_norm[0], hy_conv_w[0], hy_conv_b[0],
                            hy_w1[0], hy_b1[0], hy_w2[0], hy_b2[0], hy_w3[0], hy_freq[0], hy_bias[0])
    wr, br = _router_params(moe_w_gr[0], moe_b_gr[0], moe_w_er[0], moe_b_er[0])
    xn, h2, ids, wts, cnt3 = out_proj_router([y, u], ev_w_out[0].astype(BF16), x, mod[0], norm_ffn[0][None], wr, br)
    x = moe_block(h2, ids, wts, cnt3, xn, mod[0], gfin, moe_w_gate, moe_w_up, moe_w_down, 0, final=False)

    o_na, o_mla, qkv, ckv, kr = _odd_layer(x, mod[1], norm_mix[1][None], cache_na_k, cache_na_v, cache_mla_ckv,
                                           cache_mla_krope, na_rel_bias[0], od_w_in[0], mla_q_norm[0], mla_w_uq[0],
                                           mla_kv_norm[0], mla_w_ukv[0])
    wr, br = _router_params(moe_w_gr[1], moe_b_gr[1], moe_w_er[1], moe_b_er[1])
    xn, h2, ids, wts, cnt3 = out_proj_router([o_na, o_mla], od_w_out[0].astype(BF16), x, mod[1], norm_ffn[1][None], wr, br)
    out = moe_block(h2, ids, wts, cnt3, xn, mod[1], gfin, moe_w_gate, moe_w_up, moe_w_down, 1, final=True)

    heads = lambda a: a.reshape(N_CTX, L_CTX, NA_H, NA_D).transpose(0, 2, 1, 3)[:, None]
    return (out[:T_CTX].reshape(N_CTX, L_CTX, D),
            out[T_CTX:].reshape(N_LAT, L_LAT, D),
            fin.reshape(N_CTX, 1, 2, SSD_H, SSD_P, SSD_N),
            heads(qkv[:T_CTX, NA_W:2 * NA_W]),
            heads(qkv[:T_CTX, 2 * NA_W:3 * NA_W]),
            ckv[:T_CTX].reshape(N_CTX, 1, L_CTX, MLA_KVR),
            kr[:T_CTX, :MLA_ROPE].reshape(N_CTX, 1, L_CTX, MLA_ROPE))
```

```python
import functools
import math

import numpy as np
import jax
import jax.numpy as jnp
from jax import lax
from jax.experimental import pallas as pl
from jax.experimental.pallas import tpu as pltpu

F32 = jnp.float32
BF16 = jnp.bfloat16
HIGHEST = lax.Precision.HIGHEST

D = 1024
N_CTX, L_CTX = 16, 256
N_LAT, L_LAT = 8, 1024
T_CTX = N_CTX * L_CTX
T_LAT = N_LAT * L_LAT
T_ALL = T_CTX + T_LAT
PAST = 512
GRID_W = 64
EPS = 1e-6
NEG = -1e30

SSD_H, SSD_P, SSD_N, SSD_G = 16, 64, 128, 2
SSD_XBC = D + 2 * SSD_G * SSD_N
SSD_K = 5
CHUNK = 128

HY_K = 3
HY_BANDS = 16
HY_FEAT = 1 + 2 * HY_BANDS
HY_HID = 64
HY_MIN_DECAY = abs(math.log(1e-2) / 1.5)
HY_MAX_DECAY = abs(math.log(1e-2) / 0.3)

NA_H, NA_D = 8, 64
NA_W = NA_H * NA_D
NA_WIN_R, NA_WIN_C = 8, 16
MLA_H, MLA_QR, MLA_KVR = 8, 256, 128
MLA_NOPE, MLA_ROPE, MLA_V = 64, 32, 64
MLA_QK = MLA_NOPE + MLA_ROPE
ROPE_F = MLA_ROPE // 4

MOE_G, MOE_PG, MOE_E, MOE_F = 4, 8, 32, 256

LANES = 128
SUBLANES = 8
VMEM_LIMIT = 56 * 1024 * 1024

TM = 256
N_TILES = T_ALL // TM
CTX_TILES = T_CTX // TM
LAT_TILES_PER_SEQ = L_LAT // TM
MOD_ROWS = 16


def _cparams(sem):
    return pltpu.CompilerParams(dimension_semantics=sem, vmem_limit_bytes=VMEM_LIMIT)


def _mod_row(i):
    return jnp.where(i < CTX_TILES, 0, 1 + (i - CTX_TILES) // LAT_TILES_PER_SEQ)


def _silu(x):
    return x * jax.nn.sigmoid(x)


def _rms(x):
    return x * lax.rsqrt(jnp.mean(x * x, axis=-1, keepdims=True) + EPS)


def _ada_kernel(c_ref, w_ref, b_ref, o_ref):
    c = c_ref[...]
    o_ref[0] = jnp.dot(_silu(c), w_ref[0], precision=HIGHEST, preferred_element_type=F32) + b_ref[0]


def ada_modulation(cvec, w_ada, b_ada):
    depth = w_ada.shape[0]
    out = pl.pallas_call(
        _ada_kernel,
        grid=(depth, 6),
        in_specs=[
            pl.BlockSpec((MOD_ROWS, D), lambda l, j: (0, 0)),
            pl.BlockSpec((1, D, D), lambda l, j: (l, 0, j)),
            pl.BlockSpec((1, 1, D), lambda l, j: (l, 0, j)),
        ],
        out_specs=pl.BlockSpec((1, MOD_ROWS, D), lambda l, j: (l, 0, j)),
        out_shape=jax.ShapeDtypeStruct((depth, MOD_ROWS, 6 * D), F32),
        compiler_params=_cparams(("arbitrary", "arbitrary")),
        name="ada",
    )(cvec, w_ada, b_ada.reshape(depth, 1, 6 * D))
    return out.reshape(depth, MOD_ROWS, 6, D)


PROJ_CHUNK = 512


def _modulated(x, g_ref, mod_ref, shift_row):
    h = _rms(x) * g_ref[...]
    return h * (1.0 + mod_ref[0, shift_row + 1:shift_row + 2, :]) + mod_ref[0, shift_row:shift_row + 1, :]


def _even_in_kernel(xc_ref, xl_ref, mod_ref, g_ref, w_ref, z_ref, xbc_ref, hy_ref, dt_ref):
    x = jnp.where(pl.program_id(0) < CTX_TILES, xc_ref[...], xl_ref[...])
    hb = _modulated(x, g_ref, mod_ref, 0).astype(BF16)
    col = 0
    for o_ref in (z_ref, xbc_ref, hy_ref, dt_ref):
        width = o_ref.shape[1]
        for c0 in range(0, width, PROJ_CHUNK):
            c1 = min(c0 + PROJ_CHUNK, width)
            o_ref[:, c0:c1] = jnp.dot(hb, w_ref[:, col + c0:col + c1], preferred_element_type=F32)
        col += width


def _pair_specs(width):
    return [pl.BlockSpec((TM, width), lambda i: (jnp.minimum(i, CTX_TILES - 1), 0)),
            pl.BlockSpec((TM, width), lambda i: (jnp.maximum(i - CTX_TILES, 0), 0))]


def even_in_proj(x_pair, mod, g, w_bf):
    widths = (D, SSD_XBC, 3 * D, LANES)
    return pl.pallas_call(
        _even_in_kernel,
        grid=(N_TILES,),
        in_specs=_pair_specs(D) + [
            pl.BlockSpec((1, 6, D), lambda i: (_mod_row(i), 0, 0)),
            pl.BlockSpec((1, D), lambda i: (0, 0)),
            pl.BlockSpec(w_bf.shape, lambda i: (0, 0)),
        ],
        out_specs=[pl.BlockSpec((TM, w), lambda i: (i, 0)) for w in widths],
        out_shape=[jax.ShapeDtypeStruct((T_ALL, w), F32) for w in widths],
        compiler_params=_cparams(("arbitrary",)),
        name="even_in",
    )(*x_pair, mod, g, w_bf)


PAD = SUBLANES


def _ssd_kernel(*refs, L, has_init):
    if has_init:
        (xbc_ref, dt_ref, z_ref, init_ref, cw_ref, cb_ref, dtb_ref, alog_ref, dsk_ref, gs_ref,
         y_ref, xp_s, xc_s, ya_s, st_s) = refs
        fin_ref = None
    else:
        (xbc_ref, dt_ref, z_ref, cw_ref, cb_ref, dtb_ref, alog_ref, dsk_ref, gs_ref,
         y_ref, fin_ref, xp_s, xc_s, ya_s, st_s) = refs
        init_ref = None
    nc = L // CHUNK
    half = SSD_K // 2

    xp_s[0:PAD, :] = jnp.zeros((PAD, SSD_XBC), F32)
    xp_s[PAD + L:2 * PAD + L, :] = jnp.zeros((PAD, SSD_XBC), F32)
    xp_s[PAD:PAD + L, :] = xbc_ref[...]
    for c in range(nc):
        base = PAD + c * CHUNK - half
        for j in range(SSD_XBC // LANES):
            cols = slice(j * LANES, (j + 1) * LANES)
            acc = cb_ref[:, cols] + xp_s[base:base + CHUNK, cols] * cw_ref[0:1, cols]
            for k in range(1, SSD_K):
                acc = acc + xp_s[base + k:base + k + CHUNK, cols] * cw_ref[k:k + 1, cols]
            xc_s[c * CHUNK:(c + 1) * CHUNK, cols] = _silu(acc)

    row = lax.broadcasted_iota(jnp.int32, (CHUNK, CHUNK), 0)
    colm = lax.broadcasted_iota(jnp.int32, (CHUNK, CHUNK), 1)
    lane_lo = colm < SSD_P
    tri_lo = (colm <= row).astype(F32)
    tri_up = (colm >= row).astype(F32)

    for d in range(2):
        causal = (colm <= row) if d == 0 else (colm >= row)
        for j in range(SSD_H * SSD_P // CHUNK):
            if has_init:
                st_s[:, j * CHUNK:(j + 1) * CHUNK] = init_ref[0, d, j * CHUNK:(j + 1) * CHUNK, :].T
            else:
                st_s[:, j * CHUNK:(j + 1) * CHUNK] = jnp.zeros((CHUNK, CHUNK), F32)

        def chunk_body(ci, carry, d=d, causal=causal):
            c = ci if d == 0 else nc - 1 - ci
            r0 = pl.multiple_of(c * CHUNK, CHUNK)
            dt = jax.nn.softplus(dt_ref[pl.ds(r0, CHUNK), :] + dtb_ref[d:d + 1, :])
            a = dt * (-jnp.exp(alog_ref[d:d + 1, :]))
            tri = tri_lo if d == 0 else tri_up
            cs = jnp.dot(tri, a, precision=HIGHEST, preferred_element_type=F32)
            cs_t = jnp.dot(a.T, tri.T, precision=HIGHEST, preferred_element_type=F32)
            edge = cs[CHUNK - 1:CHUNK, :] if d == 0 else cs[0:1, :]
            ecs = jnp.exp(cs)
            dec = jnp.exp(edge - cs)
            cdec = jnp.exp(edge)
            for g in range(SSD_G):
                bm = xc_s[pl.ds(r0, CHUNK), D + g * SSD_N:D + (g + 1) * SSD_N]
                cm = xc_s[pl.ds(r0, CHUNK), D + (SSD_G + g) * SSD_N:D + (SSD_G + g + 1) * SSD_N]
                bm_b, cm_b = bm.astype(BF16), cm.astype(BF16)
                cb = lax.dot_general(cm_b, bm_b, (((1,), (1,)), ((), ())), preferred_element_type=F32)
                bm_t = bm.T.astype(BF16)
                pairs = SSD_H // SSD_G // 2
                for pp in range(pairs):
                    p = g * pairs + pp
                    h0, h1 = 2 * p, 2 * p + 1
                    cols = slice(p * CHUNK, (p + 1) * CHUNK)
                    xs = xc_s[pl.ds(r0, CHUNK), cols]
                    xdt = xs * jnp.where(lane_lo, dt[:, h0:h0 + 1], dt[:, h1:h1 + 1])
                    ms = []
                    for h in (h0, h1):
                        diff = cs[:, h:h + 1] - cs_t[h:h + 1, :]
                        ms.append(cb * jnp.exp(jnp.where(causal, diff, NEG)))
                    mcat = jnp.concatenate(ms, axis=1).astype(BF16)
                    xbd = jnp.concatenate([jnp.where(lane_lo, xdt, 0.0), jnp.where(lane_lo, 0.0, xdt)],
                                          axis=0).astype(BF16)
                    y_diag = jnp.dot(mcat, xbd, preferred_element_type=F32)
                    st = st_s[:, cols]
                    y_off = jnp.dot(cm_b, st.astype(BF16), preferred_element_type=F32)
                    y_off = y_off * jnp.where(lane_lo, ecs[:, h0:h0 + 1], ecs[:, h1:h1 + 1])
                    y = y_diag + y_off
                    if d == 0:
                        ya_s[pl.ds(r0, CHUNK), cols] = y
                    else:
                        ya_s[pl.ds(r0, CHUNK), cols] = ya_s[pl.ds(r0, CHUNK), cols] + y
                    xdd = (xdt * jnp.where(lane_lo, dec[:, h0:h0 + 1], dec[:, h1:h1 + 1])).astype(BF16)
                    snew = jnp.dot(bm_t, xdd, preferred_element_type=F32)
                    st_s[:, cols] = st * jnp.where(lane_lo[0:1, :], cdec[:, h0:h0 + 1], cdec[:, h1:h1 + 1]) + snew
            return carry

        lax.fori_loop(0, nc, chunk_body, 0)
        if fin_ref is not None:
            for j in range(SSD_H * SSD_P // CHUNK):
                fin_ref[0, d, j * CHUNK:(j + 1) * CHUNK, :] = st_s[:, j * CHUNK:(j + 1) * CHUNK].T

    def out_body(c, carry):
        r0 = pl.multiple_of(c * CHUNK, CHUNK)
        y = ya_s[pl.ds(r0, CHUNK), :] + xc_s[pl.ds(r0, CHUNK), 0:D] * dsk_ref[...]
        y = y * _silu(z_ref[pl.ds(r0, CHUNK), :])
        y_ref[pl.ds(r0, CHUNK), :] = (_rms(y) * gs_ref[...]).astype(y_ref.dtype)
        return carry

    lax.fori_loop(0, nc, out_body, 0)


def ssd_mixer(xbc, dtr, z, init, cw, cb, dtb, alog, dsk, gs, *, L, n_seq, row_off):
    blk0 = row_off // L
    has_init = init is not None
    seq = lambda w: pl.BlockSpec((L, w), lambda b: (blk0 + b, 0))
    full = lambda arr: pl.BlockSpec(arr.shape, lambda b: (0,) * arr.ndim)
    in_specs = [seq(SSD_XBC), seq(LANES), seq(D)]
    args = [xbc, dtr, z]
    if has_init:
        in_specs.append(pl.BlockSpec((1, 2, SSD_H * SSD_P, SSD_N), lambda b: (b, 0, 0, 0)))
        args.append(init)
    small = [cw, cb, dtb, alog, dsk, gs]
    in_specs += [full(a) for a in small]
    args += small
    out_specs = [pl.BlockSpec((L, D), lambda b: (b, 0))]
    out_shape = [jax.ShapeDtypeStruct((n_seq * L, D), BF16)]
    if not has_init:
        out_specs.append(pl.BlockSpec((1, 2, SSD_H * SSD_P, SSD_N), lambda b: (b, 0, 0, 0)))
        out_shape.append(jax.ShapeDtypeStruct((n_seq, 2, SSD_H * SSD_P, SSD_N), F32))
    return pl.pallas_call(
        functools.partial(_ssd_kernel, L=L, has_init=has_init),
        grid=(n_seq,),
        in_specs=in_specs,
        out_specs=out_specs,
        out_shape=out_shape,
        scratch_shapes=[
            pltpu.VMEM((L + 2 * PAD, SSD_XBC), F32),
            pltpu.VMEM((L, SSD_XBC), F32),
            pltpu.VMEM((L, D), F32),
            pltpu.VMEM((SSD_N, SSD_H * SSD_P), F32),
        ],
        compiler_params=_cparams(("arbitrary",)),
        name=f"ssd_{L}",
    )(*args)


HY_CB = 256


def dft_matrix(L):
    k = np.arange(L, dtype=np.int64)
    ang = ((k[:, None] * k[None, :]) % (2 * L)).astype(np.float64) * (math.pi / L)
    cosm = np.cos(ang)
    sinm = np.sin(ang)
    sinm[0] = np.where(k % 2 == 0, 1.0, -1.0)
    return jnp.asarray(np.concatenate([cosm, sinm], axis=0).astype(np.float32)).astype(BF16)


def _const_spec(arr):
    return pl.BlockSpec(arr.shape, lambda *_: (0,) * arr.ndim, pipeline_mode=pl.Buffered(1))


def _hy_filter_kernel(feat_ref, w1_ref, b1_ref, w2_ref, b2_ref, fr_ref, w3_ref, dl_ref, f_ref, o_ref, *, L):
    hp = functools.partial(jnp.dot, precision=HIGHEST, preferred_element_type=F32)
    hdn = jnp.sin(fr_ref[0:1, :] * (hp(feat_ref[...], w1_ref[...]) + b1_ref[...]))
    hdn = jnp.sin(fr_ref[1:2, :] * (hp(hdn, w2_ref[...]) + b2_ref[...]))
    rowi = lax.broadcasted_iota(jnp.int32, (L, 1), 0)
    t = rowi.astype(F32) * (1.0 / (L - 1))
    dec = jnp.exp(-t * dl_ref[...])
    first = rowi == 0
    for o in range(2):
        fwd = hp(hdn, w3_ref[2 * o]) * dec
        bwd = jnp.where(first, 0.0, hp(hdn, w3_ref[2 * o + 1]) * dec)
        ss = jnp.dot(f_ref[...], (fwd + bwd).astype(BF16), preferred_element_type=F32)
        sd = jnp.dot(f_ref[...], (fwd - bwd).astype(BF16), preferred_element_type=F32)
        hr = ss[0:L]
        o_ref[o, 0] = hr
        o_ref[o, 1] = jnp.where(first, 0.0, sd[L:2 * L])
        o_ref[o, 2] = jnp.where(first, ss[L:L + 1], hr)


def hyena_filter_spectra(feat, w1, b1, w2, b2, freq, w3r, deltas, f_bf, *, L):
    full = lambda arr: pl.BlockSpec(arr.shape, lambda j: (0,) * arr.ndim)
    return pl.pallas_call(
        functools.partial(_hy_filter_kernel, L=L),
        grid=(D // HY_CB,),
        in_specs=[full(feat), full(w1), full(b1), full(w2), full(b2), full(freq),
                  pl.BlockSpec((4, HY_HID, HY_CB), lambda j: (0, 0, j)),
                  pl.BlockSpec((1, HY_CB), lambda j: (0, j)),
                  _const_spec(f_bf)],
        out_specs=pl.BlockSpec((2, 3, L, HY_CB), lambda j: (0, 0, 0, j)),
        out_shape=jax.ShapeDtypeStruct((2, 3, L, D), F32),
        compiler_params=_cparams(("arbitrary",)),
        name=f"hy_filter_{L}",
    )(feat, w1, b1, w2, b2, freq, w3r, deltas, f_bf)


def split_dft_matrices(L):
    H = L // 2
    k = np.arange(H, dtype=np.int64)[:, None]
    m = np.arange(H, dtype=np.int64)[None, :]
    alt = np.where(m % 2 == 0, 1.0, -1.0)
    ang_e = ((k * m) % L).astype(np.float64) * (2 * math.pi / L)
    ang_o = ((k * (2 * m + 1)) % (2 * L)).astype(np.float64) * (math.pi / L)
    ce, se, co, so = np.cos(ang_e), np.sin(ang_e), np.cos(ang_o), np.sin(ang_o)
    se[0], so[0] = alt[0], alt[0]
    w = np.where(k == 0, 1.0, 2.0) / (2 * L)
    fe = np.concatenate([ce, se], axis=0)
    fo = np.concatenate([co, so], axis=0)
    ge = np.concatenate([(ce * w).T, se.T / L], axis=1)
    go = np.concatenate([(co * w).T, so.T / L], axis=1)
    return tuple(jnp.asarray(a.astype(np.float32)).astype(BF16) for a in (fe, fo, ge, go))


def split_spectra(spec, L):
    H = L // 2
    hr, hn, hd = spec[:, 0], spec[:, 1], spec[:, 2]
    hbr = jnp.concatenate([hd[:, 0:1], jnp.flip(hr[:, H + 1:], axis=1)], axis=1)
    hbn = jnp.concatenate([jnp.zeros_like(hn[:, 0:1]), jnp.flip(hn[:, H + 1:], axis=1)], axis=1)
    return jnp.stack([hr[:, :H], hn[:, :H], hbr, hbn], axis=1), jnp.stack([hr[:, H], hn[:, H]], axis=1)


def _store_lane_blocks(ref, val):
    for c in range(ref.shape[0]):
        ref[c] = val[:, c * LANES:(c + 1) * LANES]


def _load_parity(ref, parity, n):
    return jnp.concatenate([ref[c, pl.ds(parity, n, stride=2), :] for c in range(ref.shape[0])], axis=1)


def _hyena_kernel(p0_ref, p1_ref, p2_ref, w0_ref, w1_ref, w2_ref, b0_ref, b1_ref, b2_ref, h_ref, hm_ref, hb_ref,
                  fe_ref, fo_ref, ge_ref, go_ref, o_ref, xp_s, u_s, y_s, *, L):
    H = L // 2
    xp_s[0:PAD, :] = jnp.zeros((PAD, HY_CB), F32)
    xp_s[PAD + L:2 * PAD + L, :] = jnp.zeros((PAD, HY_CB), F32)
    first = lax.broadcasted_iota(jnp.int32, (H, 1), 0) == 0

    def conv(p_ref, w_ref, b_ref):
        xp_s[PAD:PAD + L, :] = p_ref[...]
        acc = b_ref[...] + xp_s[PAD - 1:PAD - 1 + L, :] * w_ref[0:1, :]
        for k in range(1, HY_K):
            acc = acc + xp_s[PAD - 1 + k:PAD - 1 + k + L, :] * w_ref[k:k + 1, :]
        return acc

    u = conv(p0_ref, w0_ref, b0_ref)
    for o, (p_ref, w_ref, b_ref) in enumerate(((p1_ref, w1_ref, b1_ref), (p2_ref, w2_ref, b2_ref))):
        _store_lane_blocks(u_s, u)
        se = jnp.dot(fe_ref[...], _load_parity(u_s, 0, H).astype(BF16), preferred_element_type=F32)
        so = jnp.dot(fo_ref[...], _load_parity(u_s, 1, H).astype(BF16), preferred_element_type=F32)
        e, es, od, os_ = se[0:H], se[H:L], so[0:H], so[H:L]
        b0, b1 = e + od, e - od
        b2 = jnp.where(first, es, es + os_)
        b3 = jnp.where(first, os_, os_ - es)
        har, han, hbr, hbn = h_ref[o, 0], h_ref[o, 1], h_ref[o, 2], h_ref[o, 3]
        hmr, hmn = hm_ref[o, 0:1, :], hm_ref[o, 1:2, :]
        y0 = b0 * har - b2 * han
        y1 = b1 * hbr - b3 * hbn
        y2 = b0 * han + b2 * har
        y3 = b1 * hbn + b3 * hbr
        mid_r = b2[0:1] * hmr - b3[0:1] * hmn
        mid_n = b2[0:1] * hmn + b3[0:1] * hmr
        de = jnp.where(first, mid_r, y2 - y3)
        do = jnp.where(first, mid_n, y2 + y3)
        ye = jnp.dot(ge_ref[...], jnp.concatenate([y0 + y1, de], axis=0).astype(BF16), preferred_element_type=F32)
        yo = jnp.dot(go_ref[...], jnp.concatenate([y0 - y1, do], axis=0).astype(BF16), preferred_element_type=F32)
        for c in range(HY_CB // LANES):
            y_s[c, pl.ds(0, H, stride=2), :] = ye[:, c * LANES:(c + 1) * LANES]
            y_s[c, pl.ds(1, H, stride=2), :] = yo[:, c * LANES:(c + 1) * LANES]
        y = jnp.concatenate([y_s[c] for c in range(HY_CB // LANES)], axis=1)
        u = conv(p_ref, w_ref, b_ref) * (y + u * hb_ref[o:o + 1, :])
    o_ref[...] = u.astype(o_ref.dtype)


def hyena_mixer(hy, conv_w, conv_b, spectra, hy_bias, *, L, n_seq, row_off):
    blk0 = row_off // L
    nj = D // HY_CB
    H = L // 2
    h4, hm = split_spectra(spectra, L)
    part = lambda q: pl.BlockSpec((L, HY_CB), lambda j, b: (blk0 + b, q * nj + j))
    wpart = lambda q: pl.BlockSpec((HY_K, HY_CB), lambda j, b: (0, q * nj + j))
    bpart = lambda q: pl.BlockSpec((1, HY_CB), lambda j, b: (0, q * nj + j))
    mats = split_dft_matrices(L)
    return pl.pallas_call(
        functools.partial(_hyena_kernel, L=L),
        grid=(nj, n_seq),
        in_specs=[part(0), part(1), part(2), wpart(0), wpart(1), wpart(2), bpart(0), bpart(1), bpart(2),
                  pl.BlockSpec((2, 4, H, HY_CB), lambda j, b: (0, 0, 0, j)),
                  pl.BlockSpec((2, 2, HY_CB), lambda j, b: (0, 0, j)),
                  pl.BlockSpec((2, HY_CB), lambda j, b: (0, j))]
                 + [_const_spec(m) for m in mats],
        out_specs=pl.BlockSpec((L, HY_CB), lambda j, b: (b, j)),
        out_shape=jax.ShapeDtypeStruct((n_seq * L, D), BF16),
        scratch_shapes=[pltpu.VMEM((L + 2 * PAD, HY_CB), F32), pltpu.VMEM((HY_CB // LANES, L, LANES), F32),
                        pltpu.VMEM((HY_CB // LANES, L, LANES), F32)],
        compiler_params=_cparams(("arbitrary", "arbitrary")),
        name=f"hyena_{L}",
    )(hy, hy, hy, conv_w, conv_w, conv_w, conv_b, conv_b, conv_b, h4, hm, hy_bias, *mats)


ROUTER_LANES = LANES
BIG_LANE = 1e9


ROW_GROUP = D // LANES


def _store_row_groups(ref, val):
    n = val.shape[0]
    for s in range(ROW_GROUP):
        ref[pl.ds(s, n, stride=ROW_GROUP), :] = val[:, s * LANES:(s + 1) * LANES]


def _load_row_groups(ref, n, s):
    return ref[pl.ds(s, n, stride=ROW_GROUP), :]


def _first_max_lane(v, lanef):
    m = jnp.max(v, axis=-1, keepdims=True)
    return m, jnp.min(jnp.where(v == m, lanef, BIG_LANE), axis=-1, keepdims=True)


def _out_router_kernel(*refs, n_in, x_is_pair):
    a_refs = refs[:2 * n_in]
    refs = refs[2 * n_in:]
    is_ctx = pl.program_id(0) < CTX_TILES
    if x_is_pair:
        x = jnp.where(is_ctx, refs[0][...], refs[1][...])
        refs = refs[2:]
    else:
        x = refs[0][...]
        refs = refs[1:]
    w_ref, mod_ref, gf_ref, wr_ref, br_ref, xo_ref, h2_ref, ids_ref, wts_ref, cnt_ref = refs
    acc, k0 = None, 0
    for ac_ref, al_ref in zip(a_refs[0::2], a_refs[1::2]):
        kk = ac_ref.shape[1]
        a = jnp.where(is_ctx, ac_ref[...], al_ref[...])
        part = jnp.dot(a, w_ref[k0:k0 + kk, :], preferred_element_type=F32)
        acc = part if acc is None else acc + part
        k0 += kk
    xn = x + mod_ref[0, 2:3, :] * acc
    xo_ref[...] = xn
    h2 = _modulated(xn, gf_ref, mod_ref, 3)
    h2_ref[...] = h2

    h_hi = h2.astype(BF16)
    h_lo = (h2 - h_hi.astype(F32)).astype(BF16)
    logits = (jnp.dot(h_hi, wr_ref[0], preferred_element_type=F32) + jnp.dot(h_lo, wr_ref[0], preferred_element_type=F32)
              + jnp.dot(h_hi, wr_ref[1], preferred_element_type=F32) + br_ref[...])
    lanef = lax.broadcasted_iota(jnp.int32, logits.shape, 1).astype(F32)
    gl = jnp.where(lanef < MOE_G, logits, NEG)
    gm, gi = _first_max_lane(gl, lanef)
    g_w = 1.0 / jnp.sum(jnp.exp(gl - gm), axis=-1, keepdims=True)
    lo = MOE_G + MOE_PG * gi
    el = jnp.where((lanef >= lo) & (lanef < lo + MOE_PG), logits, NEG)
    m1, e1 = _first_max_lane(el, lanef)
    m2, e2 = _first_max_lane(jnp.where(lanef == e1, NEG, el), lanef)
    p2 = jnp.exp(m2 - m1)
    w1 = g_w / (1.0 + p2)
    ids_ref[...] = jnp.where(lanef == 0, e1 - MOE_G, jnp.where(lanef == 1, e2 - MOE_G, 0.0)).astype(jnp.int32)
    wts_ref[...] = jnp.where(lanef == 0, w1, jnp.where(lanef == 1, w1 * p2, 0.0))
    chosen = ((lanef == e1 - MOE_G) | (lanef == e2 - MOE_G)).astype(F32)
    cnt_ref[0] = jnp.sum(chosen, axis=0, keepdims=True).astype(jnp.int32)


def out_proj_router(acts, w_bf, x, mod, gf, wr, br):
    tile = lambda w: pl.BlockSpec((TM, w), lambda i: (i, 0))
    full = lambda arr: pl.BlockSpec(arr.shape, lambda i: (0,) * arr.ndim)
    x_is_pair = isinstance(x, tuple)
    xs = x if x_is_pair else (x,)
    return pl.pallas_call(
        functools.partial(_out_router_kernel, n_in=len(acts), x_is_pair=x_is_pair),
        grid=(N_TILES,),
        in_specs=[s for a in acts for s in _pair_specs(a[0].shape[1])]
                 + (_pair_specs(D) if x_is_pair else [tile(D)])
                 + [full(w_bf), pl.BlockSpec((1, 6, D), lambda i: (_mod_row(i), 0, 0)), full(gf), full(wr), full(br)],
        out_specs=[tile(D), tile(D), tile(ROUTER_LANES), tile(ROUTER_LANES),
                   pl.BlockSpec((1, 1, ROUTER_LANES), lambda i: (i, 0, 0))],
        out_shape=[jax.ShapeDtypeStruct((T_ALL, D), F32), jax.ShapeDtypeStruct((T_ALL, D), F32),
                   jax.ShapeDtypeStruct((T_ALL, ROUTER_LANES), jnp.int32),
                   jax.ShapeDtypeStruct((T_ALL, ROUTER_LANES), F32),
                   jax.ShapeDtypeStruct((N_TILES, 1, ROUTER_LANES), jnp.int32)],
        compiler_params=_cparams(("arbitrary",)),
        name="out_router",
    )(*[part for a in acts for part in a], *xs, w_bf, mod, gf, wr, br)


N_ASSIGN = 2 * T_ALL
MOE_TILES = N_ASSIGN // TM + MOE_E
N_SLOTS = MOE_TILES * TM


def route_tables(cnt3):
    cnt = cnt3[:, 0, :MOE_E]
    total = jnp.sum(cnt, axis=0)
    padded = (total + TM - 1) // TM * TM
    ends = jnp.cumsum(padded)
    gdst = (ends - padded)[None, :] + jnp.cumsum(cnt, axis=0) - cnt
    loc = jnp.cumsum(cnt, axis=1) - cnt
    starts = jnp.arange(MOE_TILES, dtype=jnp.int32) * TM
    tile_expert = jnp.minimum(jnp.sum((ends[None, :] <= starts[:, None]).astype(jnp.int32), axis=1), MOE_E - 1)
    n_used = (ends[-1] // TM).astype(jnp.int32).reshape(1)
    return cnt, loc, gdst, ends, tile_expert, n_used


RUN_BITS = (2 * TM).bit_length()


def _dispatch_kernel(cnt_s, loc_s, gdst_s, ends_s, h_ref, ids_ref, gcol_ref, xs_ref, dest_ref, srt, zbuf, sem, zsem):
    i = pl.program_id(0)
    slot = i % 2
    n_rows = 2 * TM

    @pl.when(i == 0)
    def _():
        zbuf[...] = jnp.zeros(zbuf.shape, zbuf.dtype)
        n_used = ends_s[MOE_E - 1] // TM
        for phase in ("start", "wait"):
            def tail(t, c, phase=phase):
                dst = pl.multiple_of(t * (TM * ROW_GROUP), TM * ROW_GROUP)
                cp = pltpu.make_async_copy(zbuf, xs_ref.at[pl.ds(dst, TM * ROW_GROUP), :], zsem)
                cp.start() if phase == "start" else cp.wait()
                return c

            lax.fori_loop(n_used, MOE_TILES, tail, 0)
            for e in range(MOE_E):
                end = ends_s[e]
                prev = ends_s[e - 1] if e > 0 else 0

                @pl.when(end > prev)
                def _(end=end, phase=phase):
                    dst = pl.multiple_of((end - TM) * ROW_GROUP, TM * ROW_GROUP)
                    cp = pltpu.make_async_copy(zbuf, xs_ref.at[pl.ds(dst, TM * ROW_GROUP), :], zsem)
                    cp.start() if phase == "start" else cp.wait()

    idt = ids_ref[...].astype(F32).T
    sub = lax.broadcasted_iota(jnp.int32, (LANES, TM), 0).astype(F32)
    m0 = (sub == idt[0:1, :]).astype(F32)
    m1 = (sub == idt[1:2, :]).astype(F32)
    mt = (m0 + m1).astype(BF16)
    tr = lax.broadcasted_iota(jnp.int32, (TM, TM), 0)
    tc = lax.broadcasted_iota(jnp.int32, (TM, TM), 1)
    earlier = jnp.dot(mt, (tr < tc).astype(BF16), preferred_element_type=F32)
    er = lax.broadcasted_iota(jnp.int32, (LANES, LANES), 0)
    ec = lax.broadcasted_iota(jnp.int32, (LANES, LANES), 1)
    below = jnp.dot((ec < er).astype(BF16), mt, preferred_element_type=F32)
    local = jnp.sum(below, axis=1, keepdims=True) + earlier
    glob = gcol_ref[0] + earlier
    pos0 = jnp.sum(m0 * local, axis=0, keepdims=True)
    pos1 = jnp.sum(m1 * local, axis=0, keepdims=True)
    dest_ref[0] = jnp.concatenate([jnp.sum(m0 * glob, axis=0, keepdims=True),
                                   jnp.sum(m1 * glob, axis=0, keepdims=True)], axis=0).astype(jnp.int32)

    srow = lax.broadcasted_iota(jnp.int32, (n_rows, TM), 0).astype(F32)
    perm = jnp.where((srow == pos0) | (srow == pos1), 1.0, 0.0).astype(BF16)
    _store_row_groups(srt.at[slot], jnp.dot(perm, h_ref[...].astype(BF16), preferred_element_type=F32))

    for e in range(MOE_E):
        n, s0, d0 = cnt_s[0, 0, e], loc_s[0, 0, e], gdst_s[0, 0, e]
        for b in reversed(range(RUN_BITS)):
            size = 1 << b
            off = (n >> (b + 1)) << (b + 1)

            @pl.when(((n >> b) & 1) == 1)
            def _(size=size, off=off, s0=s0, d0=d0):
                src = pl.multiple_of((s0 + off) * ROW_GROUP, ROW_GROUP)
                dst = pl.multiple_of((d0 + off) * ROW_GROUP, ROW_GROUP)
                pltpu.make_async_copy(srt.at[slot, pl.ds(src, size * ROW_GROUP), :],
                                      xs_ref.at[pl.ds(dst, size * ROW_GROUP), :], sem.at[slot]).start()

    def wait(s):
        pltpu.make_async_copy(srt.at[s], xs_ref.at[pl.ds(0, n_rows * ROW_GROUP), :], sem.at[s]).wait()

    @pl.when(i > 0)
    def _():
        wait(1 - slot)

    @pl.when(i == N_TILES - 1)
    def _():
        wait(slot)


def dispatch_rows(h2, ids, cnt, loc, gdst, ends):
    tab = lambda: pl.BlockSpec((1, 1, MOE_E), lambda i: (i, 0, 0), memory_space=pltpu.SMEM)
    gcol = jnp.pad(gdst.astype(F32), ((0, 0), (0, LANES - MOE_E)))[:, :, None]
    return pl.pallas_call(
        _dispatch_kernel,
        grid=(N_TILES,),
        in_specs=[tab(), tab(), tab(), pl.BlockSpec(memory_space=pltpu.SMEM),
                  pl.BlockSpec((TM, D), lambda i: (i, 0)), pl.BlockSpec((TM, ROUTER_LANES), lambda i: (i, 0)),
                  pl.BlockSpec((1, LANES, 1), lambda i: (i, 0, 0))],
        out_specs=[pl.BlockSpec(memory_space=pl.ANY), pl.BlockSpec((1, 2, TM), lambda i: (i, 0, 0))],
        out_shape=[jax.ShapeDtypeStruct((N_SLOTS * ROW_GROUP, LANES), F32),
                   jax.ShapeDtypeStruct((N_TILES, 2, TM), jnp.int32)],
        scratch_shapes=[pltpu.VMEM((2, 2 * TM * ROW_GROUP, LANES), F32), pltpu.VMEM((TM * ROW_GROUP, LANES), F32),
                        pltpu.SemaphoreType.DMA((2,)), pltpu.SemaphoreType.DMA(())],
        compiler_params=_cparams(("arbitrary",)),
        name="moe_dispatch",
    )(cnt.reshape(N_TILES, 1, MOE_E), loc.reshape(N_TILES, 1, MOE_E), gdst.reshape(N_TILES, 1, MOE_E),
      ends, h2, ids, gcol)


DMA_UNROLL = 8


def _start_group_gather(src_hbm, idx_ref, n, dst_ref, sem):
    def body(j, c):
        for u in range(DMA_UNROLL):
            r = j * DMA_UNROLL + u
            src = pl.multiple_of(idx_ref[0, 0, r] * ROW_GROUP, ROW_GROUP)
            dst = pl.multiple_of(r * ROW_GROUP, ROW_GROUP)
            pltpu.make_async_copy(src_hbm.at[pl.ds(src, ROW_GROUP), :], dst_ref.at[pl.ds(dst, ROW_GROUP), :],
                                  sem).start(priority=u % 2)
        return c

    lax.fori_loop(0, n // DMA_UNROLL, body, 0)


def _wait_group_gather(src_hbm, dst_ref, sem):
    pltpu.make_async_copy(src_hbm.at[pl.ds(0, dst_ref.shape[0]), :], dst_ref, sem).wait()


def _experts_kernel(te_ref, nu_ref, x_ref, wg_ref, wu_ref, wd_ref, o_ref, xcat):
    i = pl.program_id(0)

    @pl.when(i < nu_ref[0])
    def _():
        for s in range(ROW_GROUP):
            xcat[:, s * LANES:(s + 1) * LANES] = _load_row_groups(x_ref, TM, s).astype(BF16)
        x = xcat[...]
        g = jnp.dot(x, wg_ref[0, 0].astype(BF16), preferred_element_type=F32)
        u = jnp.dot(x, wu_ref[0, 0].astype(BF16), preferred_element_type=F32)
        hid = (_silu(g) * u).astype(BF16)
        _store_row_groups(o_ref, jnp.dot(hid, wd_ref[0, 0].astype(BF16), preferred_element_type=F32))

    @pl.when(i >= nu_ref[0])
    def _():
        o_ref[...] = jnp.zeros(o_ref.shape, o_ref.dtype)


def grouped_experts(xs, w_gate, w_up, w_down, tile_expert, n_used, layer):
    wspec = lambda a, b: pl.BlockSpec((1, 1, a, b), lambda i, te, nu: (layer, te[i], 0, 0))
    return pl.pallas_call(
        _experts_kernel,
        grid_spec=pltpu.PrefetchScalarGridSpec(
            num_scalar_prefetch=2,
            grid=(MOE_TILES,),
            in_specs=[pl.BlockSpec((TM * ROW_GROUP, LANES), lambda i, te, nu: (jnp.minimum(i, nu[0] - 1), 0)),
                      wspec(D, MOE_F), wspec(D, MOE_F), wspec(MOE_F, D)],
            out_specs=pl.BlockSpec((TM * ROW_GROUP, LANES), lambda i, te, nu: (i, 0)),
            scratch_shapes=[pltpu.VMEM((TM, D), BF16)],
        ),
        out_shape=jax.ShapeDtypeStruct((N_SLOTS * ROW_GROUP, LANES), F32),
        compiler_params=_cparams(("arbitrary",)),
        name="moe_experts",
    )(tile_expert, n_used, xs, w_gate, w_up, w_down)


def _combine_kernel(cur_ref, nxt_ref, ys_hbm, x_ref, wts_ref, mod_ref, gfin_ref, o_ref, buf, sem, *, final):
    i = pl.program_id(0)
    slot = i % 2

    @pl.when(i == 0)
    def _():
        _start_group_gather(ys_hbm, cur_ref, 2 * TM, buf.at[0], sem.at[0])

    @pl.when(i + 1 < N_TILES)
    def _():
        _start_group_gather(ys_hbm, nxt_ref, 2 * TM, buf.at[1 - slot], sem.at[1 - slot])

    _wait_group_gather(ys_hbm, buf.at[slot], sem.at[slot])
    w0, w1 = wts_ref[:, 0:1], wts_ref[:, 1:2]
    for s in range(ROW_GROUP):
        cols = slice(s * LANES, (s + 1) * LANES)
        y0 = buf[slot, pl.ds(s, TM, stride=ROW_GROUP), :]
        y1 = buf[slot, pl.ds(TM * ROW_GROUP + s, TM, stride=ROW_GROUP), :]
        o_ref[:, cols] = x_ref[:, cols] + mod_ref[0, 5:6, cols] * (w0 * y0 + w1 * y1)
    if final:
        o_ref[...] = _rms(o_ref[...]) * gfin_ref[...]


def moe_combine(ys, dest, x, wts, mod, gfin, *, final):
    tile = lambda w: pl.BlockSpec((TM, w), lambda i: (i, 0))
    idx = lambda f: pl.BlockSpec((1, 1, 2 * TM), lambda i: (f(i), 0, 0), memory_space=pltpu.SMEM)
    dest3 = dest.reshape(N_TILES, 1, 2 * TM)
    return pl.pallas_call(
        functools.partial(_combine_kernel, final=final),
        grid=(N_TILES,),
        in_specs=[idx(lambda i: i), idx(lambda i: jnp.minimum(i + 1, N_TILES - 1)),
                  pl.BlockSpec(memory_space=pl.ANY), tile(D), tile(ROUTER_LANES),
                  pl.BlockSpec((1, 6, D), lambda i: (_mod_row(i), 0, 0)),
                  pl.BlockSpec((1, D), lambda i: (0, 0))],
        out_specs=tile(D),
        out_shape=jax.ShapeDtypeStruct((T_ALL, D), F32),
        scratch_shapes=[pltpu.VMEM((2, 2 * TM * ROW_GROUP, LANES), F32), pltpu.SemaphoreType.DMA((2,))],
        compiler_params=_cparams(("arbitrary",)),
        name="moe_combine",
    )(dest3, dest3, ys, x, wts, mod, gfin)


ODD_COLS = 2048
ROPE_Q = MLA_H * MLA_ROPE
ROPE_SHIFT = ROPE_F


def rope_tables():
    t = np.arange(L_LAT)
    pos = np.stack([t // GRID_W, t % GRID_W], axis=1).astype(np.float64)
    inv = 10000.0 ** (-np.arange(ROPE_F, dtype=np.float64) / ROPE_F)
    lane = np.arange(ROPE_Q) % MLA_ROPE
    axis = lane // (2 * ROPE_F)
    first = (lane % (2 * ROPE_F)) < ROPE_F
    ang = pos[:, axis] * inv[lane % ROPE_F][None, :]
    cos, sin = np.cos(ang), np.sin(ang)
    tabs = [cos, np.where(first[None, :], -sin, 0.0), np.where(first[None, :], 0.0, sin)]
    ident = [np.ones((1, TM, ROPE_Q)), np.zeros((1, TM, ROPE_Q)), np.zeros((1, TM, ROPE_Q))]
    return [jnp.asarray(np.concatenate([i, tb.reshape(LAT_TILES_PER_SEQ, TM, ROPE_Q)], axis=0).astype(np.float32))
            for i, tb in zip(ident, tabs)]


def _rope(x, c, a, b):
    n = x.shape[1]
    return x * c[:, :n] + pltpu.roll(x, n - ROPE_SHIFT, 1) * a[:, :n] + pltpu.roll(x, ROPE_SHIFT, 1) * b[:, :n]


def _odd_in_kernel(x_ref, mod_ref, g_ref, w_ref, gq_ref, wuq_ref, gkv_ref, wukv_ref, rc_ref, ra_ref, rb_ref,
                   qkv_ref, qm_ref, ckv_ref, kvu_ref, kr_ref):
    hb = _modulated(x_ref[...], g_ref, mod_ref, 0).astype(BF16)
    for c0 in range(0, 3 * NA_W, PROJ_CHUNK):
        qkv_ref[:, c0:c0 + PROJ_CHUNK] = jnp.dot(hb, w_ref[:, c0:c0 + PROJ_CHUNK], preferred_element_type=F32)
    rest = jnp.dot(hb, w_ref[:, 3 * NA_W:ODD_COLS], preferred_element_type=F32)
    rc, ra, rb = rc_ref[0], ra_ref[0], rb_ref[0]
    qd = (_rms(rest[:, 0:MLA_QR]) * gq_ref[...]).astype(BF16)
    qm = jnp.dot(qd, wuq_ref[...], preferred_element_type=F32)
    qm_ref[:, 0:MLA_H * MLA_NOPE] = qm[:, 0:MLA_H * MLA_NOPE]
    qm_ref[:, MLA_H * MLA_NOPE:] = _rope(qm[:, MLA_H * MLA_NOPE:], rc, ra, rb)
    ckv = _rms(rest[:, MLA_QR:MLA_QR + MLA_KVR]) * gkv_ref[...]
    ckv_ref[...] = ckv
    kvu_ref[...] = jnp.dot(ckv.astype(BF16), wukv_ref[...], preferred_element_type=F32)
    kr_ref[...] = _rope(rest[:, MLA_QR + MLA_KVR:], rc, ra, rb)


def odd_in_proj(x, mod, g, w_bf, gq, wuq_bf, gkv, wukv_bf, tabs):
    tile = lambda w: pl.BlockSpec((TM, w), lambda i: (i, 0))
    full = lambda arr: pl.BlockSpec(arr.shape, lambda i: (0,) * arr.ndim)
    tab = pl.BlockSpec((1, TM, ROPE_Q),
                       lambda i: (jnp.where(i < CTX_TILES, 0, 1 + (i - CTX_TILES) % LAT_TILES_PER_SEQ), 0, 0))
    widths = (3 * NA_W, MLA_H * MLA_QK, MLA_KVR, MLA_H * (MLA_NOPE + MLA_V), LANES)
    return pl.pallas_call(
        _odd_in_kernel,
        grid=(N_TILES,),
        in_specs=[tile(D), pl.BlockSpec((1, 6, D), lambda i: (_mod_row(i), 0, 0)), full(g), full(w_bf),
                  full(gq), full(wuq_bf), full(gkv), full(wukv_bf), tab, tab, tab],
        out_specs=[tile(w) for w in widths],
        out_shape=[jax.ShapeDtypeStruct((T_ALL, w), F32) for w in widths],
        compiler_params=_cparams(("arbitrary",)),
        name="odd_in",
    )(x, mod, g, w_bf, gq, wuq_bf, gkv, wukv_bf, *tabs)


LOG2E = math.log2(math.e)
NA_QSCALE = NA_D ** -0.5 * LOG2E
MLA_QSCALE = MLA_QK ** -0.5 * LOG2E
NT = (((1,), (1,)), ((), ()))


def _softmax_pv(scores, values):
    m = functools.reduce(jnp.maximum, [jnp.max(s, axis=-1, keepdims=True) for s in scores])
    ps = [jnp.exp2(s - m) for s in scores]
    den = functools.reduce(jnp.add, [jnp.sum(p, axis=-1, keepdims=True) for p in ps])
    acc = functools.reduce(jnp.add, [jnp.dot(p.astype(BF16), v, preferred_element_type=F32) for p, v in zip(ps, values)])
    return acc / den


def _head(ref_or_val, h, width, base=0):
    return ref_or_val[:, base + h * width:base + (h + 1) * width]


def _mla_scores(qm, h, kn, kr):
    qn = (_head(qm, h, MLA_NOPE) * MLA_QSCALE).astype(BF16)
    qr = (_head(qm, h, MLA_ROPE, MLA_H * MLA_NOPE) * MLA_QSCALE).astype(BF16)
    return (lax.dot_general(qn, kn, NT, preferred_element_type=F32)
            + lax.dot_general(qr, kr, NT, preferred_element_type=F32))


def _attn_ctx_kernel(qkv_ref, qm_ref, kvu_ref, kr_ref, ona_ref, omla_ref):
    kr = kr_ref[:, 0:MLA_ROPE].astype(BF16)
    for h in range(NA_H):
        q = (_head(qkv_ref, h, NA_D) * NA_QSCALE).astype(BF16)
        k = _head(qkv_ref, h, NA_D, NA_W).astype(BF16)
        v = _head(qkv_ref, h, NA_D, 2 * NA_W).astype(BF16)
        s = lax.dot_general(q, k, NT, preferred_element_type=F32)
        ona_ref[:, h * NA_D:(h + 1) * NA_D] = _softmax_pv([s], [v]).astype(ona_ref.dtype)
    for h in range(MLA_H):
        kn = _head(kvu_ref, h, MLA_NOPE).astype(BF16)
        v = _head(kvu_ref, h, MLA_V, MLA_H * MLA_NOPE).astype(BF16)
        s = _mla_scores(qm_ref, h, kn, kr)
        omla_ref[:, h * MLA_V:(h + 1) * MLA_V] = _softmax_pv([s], [v]).astype(omla_ref.dtype)


def attn_context(qkv, qm, kvu, kr):
    seq = lambda w: pl.BlockSpec((L_CTX, w), lambda b: (b, 0))
    return pl.pallas_call(
        _attn_ctx_kernel,
        grid=(N_CTX,),
        in_specs=[seq(3 * NA_W), seq(MLA_H * MLA_QK), seq(MLA_H * (MLA_NOPE + MLA_V)), seq(LANES)],
        out_specs=[seq(NA_W), seq(MLA_H * MLA_V)],
        out_shape=[jax.ShapeDtypeStruct((T_CTX, NA_W), BF16), jax.ShapeDtypeStruct((T_CTX, MLA_H * MLA_V), BF16)],
        compiler_params=_cparams(("arbitrary",)),
        name="attn_ctx",
    )(qkv, qm, kvu, kr)


N_DR = 2 * NA_WIN_R - 1
GRID_ROWS = L_LAT // GRID_W


def _na_bias_kernel(t_ref, o_ref):
    neg = jnp.full((GRID_W, GRID_W), NEG, F32)
    for r in range(GRID_ROWS):
        r0 = min(max(r - NA_WIN_R // 2, 0), GRID_ROWS - NA_WIN_R)
        for kr in range(GRID_ROWS):
            in_window = r0 <= kr < r0 + NA_WIN_R
            blk = t_ref[0, kr - r + NA_WIN_R - 1] if in_window else neg
            o_ref[0, r * GRID_W:(r + 1) * GRID_W, kr * GRID_W:(kr + 1) * GRID_W] = blk


def neighbourhood_bias(rel_bias):
    c = np.arange(GRID_W)
    c0 = np.clip(c - NA_WIN_C // 2, 0, GRID_W - NA_WIN_C)
    col_ok = (c[None, :] >= c0[:, None]) & (c[None, :] < c0[:, None] + NA_WIN_C)
    dc = np.clip(c[None, :] - c[:, None], -(NA_WIN_C - 1), NA_WIN_C - 1) + NA_WIN_C - 1
    sel_c = (dc[:, :, None] == np.arange(2 * NA_WIN_C - 1)).astype(np.float32)
    t = jnp.einsum("hdj,qcj->hdqc", rel_bias.astype(F32), jnp.asarray(sel_c), precision=HIGHEST)
    t = jnp.where(jnp.asarray(col_ok)[None, None], t * LOG2E, NEG)
    return pl.pallas_call(
        _na_bias_kernel,
        grid=(NA_H,),
        in_specs=[pl.BlockSpec((1, N_DR, GRID_W, GRID_W), lambda h: (h, 0, 0, 0))],
        out_specs=pl.BlockSpec((1, L_LAT, L_LAT), lambda h: (h, 0, 0)),
        out_shape=jax.ShapeDtypeStruct((NA_H, L_LAT, L_LAT), F32),
        compiler_params=_cparams(("arbitrary",)),
        name="na_bias",
    )(t)


def _na_lat_kernel(q_ref, k_ref, v_ref, kc_ref, vc_ref, b_ref, o_ref):
    for h in range(NA_H):
        q = (_head(q_ref, h, NA_D) * NA_QSCALE).astype(BF16)
        k = _head(k_ref, h, NA_D).astype(BF16)
        v = _head(v_ref, h, NA_D).astype(BF16)
        kc = kc_ref[0, 0, h].astype(BF16)
        vc = vc_ref[0, 0, h].astype(BF16)
        s1 = lax.dot_general(q, k, NT, preferred_element_type=F32) + b_ref[h]
        s2 = lax.dot_general(q, kc, NT, preferred_element_type=F32)
        o_ref[:, h * NA_D:(h + 1) * NA_D] = _softmax_pv([s1, s2], [v, vc]).astype(o_ref.dtype)


def attn_neighbourhood_latent(qkv, cache_k, cache_v, bias):
    nq = L_LAT // TM
    t0 = T_CTX // TM
    s0 = T_CTX // L_LAT
    cache = pl.BlockSpec((1, 1, NA_H, PAST, NA_D), lambda qt, b: (b, 0, 0, 0, 0))
    return pl.pallas_call(
        _na_lat_kernel,
        grid=(nq, N_LAT),
        in_specs=[pl.BlockSpec((TM, NA_W), lambda qt, b: (t0 + b * nq + qt, 0)),
                  pl.BlockSpec((L_LAT, NA_W), lambda qt, b: (s0 + b, 1)),
                  pl.BlockSpec((L_LAT, NA_W), lambda qt, b: (s0 + b, 2)),
                  cache, cache,
                  pl.BlockSpec((NA_H, TM, L_LAT), lambda qt, b: (0, qt, 0))],
        out_specs=pl.BlockSpec((TM, NA_W), lambda qt, b: (b * nq + qt, 0)),
        out_shape=jax.ShapeDtypeStruct((T_LAT, NA_W), BF16),
        compiler_params=_cparams(("arbitrary", "arbitrary")),
        name="attn_na_lat",
    )(qkv, qkv, qkv, cache_k, cache_v, bias)


def _mla_lat_kernel(qm_ref, kvu_ref, kr_ref, ckv_ref, krc_ref, wukv_ref, o_ref):
    kvc = jnp.dot(ckv_ref[0, 0].astype(BF16), wukv_ref[...], preferred_element_type=F32)
    kr = kr_ref[:, 0:MLA_ROPE].astype(BF16)
    krc = krc_ref[0, 0].astype(BF16)
    for h in range(MLA_H):
        kn = _head(kvu_ref, h, MLA_NOPE).astype(BF16)
        v = _head(kvu_ref, h, MLA_V, MLA_H * MLA_NOPE).astype(BF16)
        knc = _head(kvc, h, MLA_NOPE).astype(BF16)
        vc = _head(kvc, h, MLA_V, MLA_H * MLA_NOPE).astype(BF16)
        s1 = _mla_scores(qm_ref, h, kn, kr)
        s2 = _mla_scores(qm_ref, h, knc, krc)
        o_ref[:, h * MLA_V:(h + 1) * MLA_V] = _softmax_pv([s1, s2], [v, vc]).astype(o_ref.dtype)


def attn_mla_latent(qm, kvu, kr, cache_ckv, cache_krope, wukv_bf):
    nq = L_LAT // TM
    t0 = T_CTX // TM
    s0 = T_CTX // L_LAT
    return pl.pallas_call(
        _mla_lat_kernel,
        grid=(nq, N_LAT),
        in_specs=[pl.BlockSpec((TM, MLA_H * MLA_QK), lambda qt, b: (t0 + b * nq + qt, 0)),
                  pl.BlockSpec((L_LAT, MLA_H * (MLA_NOPE + MLA_V)), lambda qt, b: (s0 + b, 0)),
                  pl.BlockSpec((L_LAT, LANES), lambda qt, b: (s0 + b, 0)),
                  pl.BlockSpec((1, 1, PAST, MLA_KVR), lambda qt, b: (b, 0, 0, 0)),
                  pl.BlockSpec((1, 1, PAST, MLA_ROPE), lambda qt, b: (b, 0, 0, 0)),
                  pl.BlockSpec(wukv_bf.shape, lambda qt, b: (0, 0))],
        out_specs=pl.BlockSpec((TM, MLA_H * MLA_V), lambda qt, b: (b * nq + qt, 0)),
        out_shape=jax.ShapeDtypeStruct((T_LAT, MLA_H * MLA_V), BF16),
        compiler_params=_cparams(("arbitrary", "arbitrary")),
        name="attn_mla_lat",
    )(qm, kvu, kr, cache_ckv, cache_krope, wukv_bf)


def moe_block(h2, ids, wts, cnt3, x, mod, gfin, w_gate, w_up, w_down, layer, *, final):
    cnt, loc, gdst, ends, tile_expert, n_used = route_tables(cnt3)
    xs, dest = dispatch_rows(h2, ids, cnt, loc, gdst, ends)
    ys = grouped_experts(xs, w_gate, w_up, w_down, tile_expert, n_used, layer)
    return moe_combine(ys, dest, x, wts, mod, gfin, final=final)


def _pad_lanes(a):
    return jnp.pad(a, ((0, 0), (0, LANES - a.shape[1])))


def _hyena_features(L):
    t = np.linspace(0.0, 1.0, L)[:, None]
    w = 2.0 * math.pi * np.arange(L) / L
    bands = np.linspace(1e-4, HY_BANDS - 1, HY_BANDS)
    ang = w[:, None] * bands[None]
    feat = np.concatenate([t, np.cos(ang), -np.sin(ang)], axis=-1)
    return jnp.asarray(np.pad(feat, ((0, 0), (0, LANES - HY_FEAT))).astype(np.float32))


def _router_params(w_gr, b_gr, w_er, b_er):
    wr = _pad_lanes(jnp.concatenate([w_gr, w_er], axis=1))
    br = _pad_lanes(jnp.concatenate([b_gr, b_er])[None])
    wr_hi = wr.astype(BF16)
    wr_lo = (wr - wr_hi.astype(F32)).astype(BF16)
    return jnp.stack([wr_hi, wr_lo]), br


def _even_layer(x, mod, g_mix, state, w_in, conv_w, conv_b, a_log, dt_bias, d_skip, g_ssd, hy_conv_w, hy_conv_b,
                hy_w1, hy_b1, hy_w2, hy_b2, hy_w3, hy_freq, hy_bias):
    n0 = D + SSD_XBC
    w_bf = jnp.concatenate([w_in[:, :n0], w_in[:, n0 + SSD_H:], w_in[:, n0:n0 + SSD_H],
                            jnp.zeros((D, LANES - SSD_H), F32)], axis=1).astype(BF16)
    z, xbc, hy, dtr = even_in_proj(x, mod, g_mix, w_bf)
    small = (conv_w, conv_b[None], _pad_lanes(dt_bias), _pad_lanes(a_log), jnp.repeat(d_skip, SSD_P)[None], g_ssd[None])
    y_c, fin = ssd_mixer(xbc, dtr, z, None, *small, L=L_CTX, n_seq=N_CTX, row_off=0)
    (y_l,) = ssd_mixer(xbc, dtr, z, state.reshape(N_LAT, 2, SSD_H * SSD_P, SSD_N), *small,
                       L=L_LAT, n_seq=N_LAT, row_off=T_CTX)
    w1 = jnp.pad(hy_w1, ((0, LANES - HY_FEAT), (0, 0)))
    w3r = hy_w3.reshape(HY_HID, 4, D).transpose(1, 0, 2)
    deltas = jnp.asarray(np.linspace(HY_MIN_DECAY, HY_MAX_DECAY, D).astype(np.float32))[None]
    us = []
    for L, n_seq, off in ((L_CTX, N_CTX, 0), (L_LAT, N_LAT, T_CTX)):
        spectra = hyena_filter_spectra(_hyena_features(L), w1, hy_b1[None], hy_w2, hy_b2[None], hy_freq, w3r, deltas,
                                       dft_matrix(L), L=L)
        us.append(hyena_mixer(hy, hy_conv_w, hy_conv_b[None], spectra, hy_bias, L=L, n_seq=n_seq, row_off=off))
    return (y_c, y_l), tuple(us), fin


def _odd_layer(x, mod, g_mix, cache_k, cache_v, cache_ckv, cache_kr, rel_bias, w_in, g_q, w_uq, g_kv, w_ukv):
    w_bf = jnp.pad(w_in, ((0, 0), (0, ODD_COLS - w_in.shape[1]))).astype(BF16)
    wuq = w_uq.reshape(MLA_QR, MLA_H, MLA_QK)
    wuq_bf = jnp.concatenate([wuq[:, :, :MLA_NOPE].reshape(MLA_QR, -1), wuq[:, :, MLA_NOPE:].reshape(MLA_QR, -1)],
                             axis=1).astype(BF16)
    wukv = w_ukv.reshape(MLA_KVR, MLA_H, MLA_NOPE + MLA_V)
    wukv_bf = jnp.concatenate([wukv[:, :, :MLA_NOPE].reshape(MLA_KVR, -1), wukv[:, :, MLA_NOPE:].reshape(MLA_KVR, -1)],
                              axis=1).astype(BF16)
    qkv, qm, ckv, kvu, kr = odd_in_proj(x, mod, g_mix, w_bf, g_q[None], wuq_bf, g_kv[None], wukv_bf, rope_tables())
    ona_c, omla_c = attn_context(qkv, qm, kvu, kr)
    ona_l = attn_neighbourhood_latent(qkv, cache_k, cache_v, neighbourhood_bias(rel_bias))
    omla_l = attn_mla_latent(qm, kvu, kr, cache_ckv, cache_kr, wukv_bf)
    return (ona_c, ona_l), (omla_c, omla_l), qkv, ckv, kr


def kernel(x_prompt, x_sample, state_ssd, cache_na_k, cache_na_v, cache_mla_ckv, cache_mla_krope, c, c_ctx, w_ada, b_ada, norm_mix, norm_ffn, norm_final, ev_w_in, ev_conv_w, ev_conv_b, ssd_A_log, ssd_dt_bias, ssd_d, ssd_norm, hy_conv_w, hy_conv_b, hy_w1, hy_b1, hy_w2, hy_b2, hy_w3, hy_freq, hy_bias, ev_w_out, od_w_in, mla_q_norm, mla_w_uq, mla_kv_norm, mla_w_ukv, na_rel_bias, od_w_out, moe_w_gr, moe_b_gr, moe_w_er, moe_b_er, moe_w_gate, moe_w_up, moe_w_down):
    x = (x_prompt.reshape(T_CTX, D), x_sample.reshape(T_LAT, D))
    cvec = jnp.zeros((MOD_ROWS, D), F32).at[0].set(c_ctx).at[1:1 + N_LAT].set(c)
    mod = ada_modulation(cvec, w_ada, b_ada)
    gfin = norm_final[None]

    y, u, fin = _even_layer(x, mod[0], norm_mix[0][None], state_ssd[:, 0], ev_w_in[0], ev_conv_w[0], ev_conv_b[0],
                            ssd_A_log[0], ssd_dt_bias[0], ssd_d[0], ssd_norm[0], hy_conv_w[0], hy_conv_b[0],
                            hy_w1[0], hy_b1[0], hy_w2[0], hy_b2[0], hy_w3[0], hy_freq[0], hy_bias[0])
    wr, br = _router_params(moe_w_gr[0], moe_b_gr[0], moe_w_er[0], moe_b_er[0])
    xn, h2, ids, wts, cnt3 = out_proj_router([y, u], ev_w_out[0].astype(BF16), x, mod[0], norm_ffn[0][None], wr, br)
    x = moe_block(h2, ids, wts, cnt3, xn, mod[0], gfin, moe_w_gate, moe_w_up, moe_w_down, 0, final=False)

    o_na, o_mla, qkv, ckv, kr = _odd_layer(x, mod[1], norm_mix[1][None], cache_na_k, cache_na_v, cache_mla_ckv,
                                           cache_mla_krope, na_rel_bias[0], od_w_in[0], mla_q_norm[0], mla_w_uq[0],
                                           mla_kv_norm[0], mla_w_ukv[0])
    wr, br = _router_params(moe_w_gr[1], moe_b_gr[1], moe_w_er[1], moe_b_er[1])
    xn, h2, ids, wts, cnt3 = out_proj_router([o_na, o_mla], od_w_out[0].astype(BF16), x, mod[1], norm_ffn[1][None], wr, br)
    out = moe_block(h2, ids, wts, cnt3, xn, mod[1], gfin, moe_w_gate, moe_w_up, moe_w_down, 1, final=True)

    heads = lambda a: a.reshape(N_CTX, L_CTX, NA_H, NA_D).transpose(0, 2, 1, 3)[:, None]
    return (out[:T_CTX].reshape(N_CTX, L_CTX, D),
            out[T_CTX:].reshape(N_LAT, L_LAT, D),
            fin.reshape(N_CTX, 1, 2, SSD_H, SSD_P, SSD_N),
            heads(qkv[:T_CTX, NA_W:2 * NA_W]),
            heads(qkv[:T_CTX, 2 * NA_W:3 * NA_W]),
            ckv[:T_CTX].reshape(N_CTX, 1, L_CTX, MLA_KVR),
            kr[:T_CTX, :MLA_ROPE].reshape(N_CTX, 1, L_CTX, MLA_ROPE))
```

```python
import functools
import math

import numpy as np
import jax
import jax.numpy as jnp
from jax import lax
from jax.experimental import pallas as pl
from jax.experimental.pallas import tpu as pltpu

F32 = jnp.float32
BF16 = jnp.bfloat16
HIGHEST = lax.Precision.HIGHEST

D = 1024
N_CTX, L_CTX = 16, 256
N_LAT, L_LAT = 8, 1024
T_CTX = N_CTX * L_CTX
T_LAT = N_LAT * L_LAT
T_ALL = T_CTX + T_LAT
PAST = 512
GRID_W = 64
EPS = 1e-6
NEG = -1e30

SSD_H, SSD_P, SSD_N, SSD_G = 16, 64, 128, 2
SSD_XBC = D + 2 * SSD_G * SSD_N
SSD_K = 5
CHUNK = 128

HY_K = 3
HY_BANDS = 16
HY_FEAT = 1 + 2 * HY_BANDS
HY_HID = 64
HY_MIN_DECAY = abs(math.log(1e-2) / 1.5)
HY_MAX_DECAY = abs(math.log(1e-2) / 0.3)

NA_H, NA_D = 8, 64
NA_W = NA_H * NA_D
NA_WIN_R, NA_WIN_C = 8, 16
MLA_H, MLA_QR, MLA_KVR = 8, 256, 128
MLA_NOPE, MLA_ROPE, MLA_V = 64, 32, 64
MLA_QK = MLA_NOPE + MLA_ROPE
ROPE_F = MLA_ROPE // 4

MOE_G, MOE_PG, MOE_E, MOE_F = 4, 8, 32, 256

LANES = 128
SUBLANES = 8
VMEM_LIMIT = 56 * 1024 * 1024

TM = 256
N_TILES = T_ALL // TM
CTX_TILES = T_CTX // TM
LAT_TILES_PER_SEQ = L_LAT // TM
MOD_ROWS = 16


def _cparams(sem):
    return pltpu.CompilerParams(dimension_semantics=sem, vmem_limit_bytes=VMEM_LIMIT)


def _mod_row(i):
    return jnp.where(i < CTX_TILES, 0, 1 + (i - CTX_TILES) // LAT_TILES_PER_SEQ)


def _silu(x):
    return x * jax.nn.sigmoid(x)


def _rms(x):
    return x * lax.rsqrt(jnp.mean(x * x, axis=-1, keepdims=True) + EPS)


def _ada_kernel(c_ref, w_ref, b_ref, o_ref):
    c = c_ref[...]
    o_ref[0] = jnp.dot(_silu(c), w_ref[0], precision=HIGHEST, preferred_element_type=F32) + b_ref[0]


def ada_modulation(cvec, w_ada, b_ada):
    depth = w_ada.shape[0]
    out = pl.pallas_call(
        _ada_kernel,
        grid=(depth, 6),
        in_specs=[
            pl.BlockSpec((MOD_ROWS, D), lambda l, j: (0, 0)),
            pl.BlockSpec((1, D, D), lambda l, j: (l, 0, j)),
            pl.BlockSpec((1, 1, D), lambda l, j: (l, 0, j)),
        ],
        out_specs=pl.BlockSpec((1, MOD_ROWS, D), lambda l, j: (l, 0, j)),
        out_shape=jax.ShapeDtypeStruct((depth, MOD_ROWS, 6 * D), F32),
        compiler_params=_cparams(("arbitrary", "arbitrary")),
        name="ada",
    )(cvec, w_ada, b_ada.reshape(depth, 1, 6 * D))
    return out.reshape(depth, MOD_ROWS, 6, D)


PROJ_CHUNK = 512


def _modulated(x, g_ref, mod_ref, shift_row):
    h = _rms(x) * g_ref[...]
    return h * (1.0 + mod_ref[0, shift_row + 1:shift_row + 2, :]) + mod_ref[0, shift_row:shift_row + 1, :]


def _even_in_kernel(xc_ref, xl_ref, mod_ref, g_ref, w_ref, z_ref, xbc_ref, hy_ref, dt_ref):
    x = jnp.where(pl.program_id(0) < CTX_TILES, xc_ref[...], xl_ref[...])
    hb = _modulated(x, g_ref, mod_ref, 0).astype(BF16)
    col = 0
    for o_ref in (z_ref, xbc_ref, hy_ref, dt_ref):
        width = o_ref.shape[1]
        for c0 in range(0, width, PROJ_CHUNK):
            c1 = min(c0 + PROJ_CHUNK, width)
            o_ref[:, c0:c1] = jnp.dot(hb, w_ref[:, col + c0:col + c1], preferred_element_type=F32)
        col += width


def _pair_specs(width):
    return [pl.BlockSpec((TM, width), lambda i: (jnp.minimum(i, CTX_TILES - 1), 0)),
            pl.BlockSpec((TM, width), lambda i: (jnp.maximum(i - CTX_TILES, 0), 0))]


def even_in_proj(x_pair, mod, g, w_bf):
    widths = (D, SSD_XBC, 3 * D, LANES)
    return pl.pallas_call(
        _even_in_kernel,
        grid=(N_TILES,),
        in_specs=_pair_specs(D) + [
            pl.BlockSpec((1, 6, D), lambda i: (_mod_row(i), 0, 0)),
            pl.BlockSpec((1, D), lambda i: (0, 0)),
            pl.BlockSpec(w_bf.shape, lambda i: (0, 0)),
        ],
        out_specs=[pl.BlockSpec((TM, w), lambda i: (i, 0)) for w in widths],
        out_shape=[jax.ShapeDtypeStruct((T_ALL, w), F32) for w in widths],
        compiler_params=_cparams(("arbitrary",)),
        name="even_in",
    )(*x_pair, mod, g, w_bf)


PAD = SUBLANES


def _ssd_kernel(*refs, L, has_init):
    if has_init:
        (xbc_ref, dt_ref, z_ref, init_ref, cw_ref, cb_ref, dtb_ref, alog_ref, dsk_ref, gs_ref,
         y_ref, xp_s, xc_s, ya_s, st_s) = refs
        fin_ref = None
    else:
        (xbc_ref, dt_ref, z_ref, cw_ref, cb_ref, dtb_ref, alog_ref, dsk_ref, gs_ref,
         y_ref, fin_ref, xp_s, xc_s, ya_s, st_s) = refs
        init_ref = None
    nc = L // CHUNK
    half = SSD_K // 2

    xp_s[0:PAD, :] = jnp.zeros((PAD, SSD_XBC), F32)
    xp_s[PAD + L:2 * PAD + L, :] = jnp.zeros((PAD, SSD_XBC), F32)
    xp_s[PAD:PAD + L, :] = xbc_ref[...]
    for c in range(nc):
        base = PAD + c * CHUNK - half
        for j in range(SSD_XBC // LANES):
            cols = slice(j * LANES, (j + 1) * LANES)
            acc = cb_ref[:, cols] + xp_s[base:base + CHUNK, cols] * cw_ref[0:1, cols]
            for k in range(1, SSD_K):
                acc = acc + xp_s[base + k:base + k + CHUNK, cols] * cw_ref[k:k + 1, cols]
            xc_s[c * CHUNK:(c + 1) * CHUNK, cols] = _silu(acc)

    row = lax.broadcasted_iota(jnp.int32, (CHUNK, CHUNK), 0)
    colm = lax.broadcasted_iota(jnp.int32, (CHUNK, CHUNK), 1)
    lane_lo = colm < SSD_P
    tri_lo = (colm <= row).astype(F32)
    tri_up = (colm >= row).astype(F32)

    for d in range(2):
        causal = (colm <= row) if d == 0 else (colm >= row)
        for j in range(SSD_H * SSD_P // CHUNK):
            if has_init:
                st_s[:, j * CHUNK:(j + 1) * CHUNK] = init_ref[0, d, j * CHUNK:(j + 1) * CHUNK, :].T
            else:
                st_s[:, j * CHUNK:(j + 1) * CHUNK] = jnp.zeros((CHUNK, CHUNK), F32)

        def chunk_body(ci, carry, d=d, causal=causal):
            c = ci if d == 0 else nc - 1 - ci
            r0 = pl.multiple_of(c * CHUNK, CHUNK)
            dt = jax.nn.softplus(dt_ref[pl.ds(r0, CHUNK), :] + dtb_ref[d:d + 1, :])
            a = dt * (-jnp.exp(alog_ref[d:d + 1, :]))
            tri = tri_lo if d == 0 else tri_up
            cs = jnp.dot(tri, a, precision=HIGHEST, preferred_element_type=F32)
            cs_t = jnp.dot(a.T, tri.T, precision=HIGHEST, preferred_element_type=F32)
            edge = cs[CHUNK - 1:CHUNK, :] if d == 0 else cs[0:1, :]
            ecs = jnp.exp(cs)
            dec = jnp.exp(edge - cs)
            cdec = jnp.exp(edge)
            for g in range(SSD_G):
                bm = xc_s[pl.ds(r0, CHUNK), D + g * SSD_N:D + (g + 1) * SSD_N]
                cm = xc_s[pl.ds(r0, CHUNK), D + (SSD_G + g) * SSD_N:D + (SSD_G + g + 1) * SSD_N]
                bm_b, cm_b = bm.astype(BF16), cm.astype(BF16)
                cb = lax.dot_general(cm_b, bm_b, (((1,), (1,)), ((), ())), preferred_element_type=F32)
                bm_t = bm.T.astype(BF16)
                pairs = SSD_H // SSD_G // 2
                for pp in range(pairs):
                    p = g * pairs + pp
                    h0, h1 = 2 * p, 2 * p + 1
                    cols = slice(p * CHUNK, (p + 1) * CHUNK)
                    xs = xc_s[pl.ds(r0, CHUNK), cols]
                    xdt = xs * jnp.where(lane_lo, dt[:, h0:h0 + 1], dt[:, h1:h1 + 1])
                    ms = []
                    for h in (h0, h1):
                        diff = cs[:, h:h + 1] - cs_t[h:h + 1, :]
                        ms.append(cb * jnp.exp(jnp.where(causal, diff, NEG)))
                    mcat = jnp.concatenate(ms, axis=1).astype(BF16)
                    xbd = jnp.concatenate([jnp.where(lane_lo, xdt, 0.0), jnp.where(lane_lo, 0.0, xdt)],
                                          axis=0).astype(BF16)
                    y_diag = jnp.dot(mcat, xbd, preferred_element_type=F32)
                    st = st_s[:, cols]
                    y_off = jnp.dot(cm_b, st.astype(BF16), preferred_element_type=F32)
                    y_off = y_off * jnp.where(lane_lo, ecs[:, h0:h0 + 1], ecs[:, h1:h1 + 1])
                    y = y_diag + y_off
                    if d == 0:
                        ya_s[pl.ds(r0, CHUNK), cols] = y
                    else:
                        ya_s[pl.ds(r0, CHUNK), cols] = ya_s[pl.ds(r0, CHUNK), cols] + y
                    xdd = (xdt * jnp.where(lane_lo, dec[:, h0:h0 + 1], dec[:, h1:h1 + 1])).astype(BF16)
                    snew = jnp.dot(bm_t, xdd, preferred_element_type=F32)
                    st_s[:, cols] = st * jnp.where(lane_lo[0:1, :], cdec[:, h0:h0 + 1], cdec[:, h1:h1 + 1]) + snew
            return carry

        lax.fori_loop(0, nc, chunk_body, 0)
        if fin_ref is not None:
            for j in range(SSD_H * SSD_P // CHUNK):
                fin_ref[0, d, j * CHUNK:(j + 1) * CHUNK, :] = st_s[:, j * CHUNK:(j + 1) * CHUNK].T

    def out_body(c, carry):
        r0 = pl.multiple_of(c * CHUNK, CHUNK)
        y = ya_s[pl.ds(r0, CHUNK), :] + xc_s[pl.ds(r0, CHUNK), 0:D] * dsk_ref[...]
        y = y * _silu(z_ref[pl.ds(r0, CHUNK), :])
        y_ref[pl.ds(r0, CHUNK), :] = (_rms(y) * gs_ref[...]).astype(y_ref.dtype)
        return carry

    lax.fori_loop(0, nc, out_body, 0)


def ssd_mixer(xbc, dtr, z, init, cw, cb, dtb, alog, dsk, gs, *, L, n_seq, row_off):
    blk0 = row_off // L
    has_init = init is not None
    seq = lambda w: pl.BlockSpec((L, w), lambda b: (blk0 + b, 0))
    full = lambda arr: pl.BlockSpec(arr.shape, lambda b: (0,) * arr.ndim)
    in_specs = [seq(SSD_XBC), seq(LANES), seq(D)]
    args = [xbc, dtr, z]
    if has_init:
        in_specs.append(pl.BlockSpec((1, 2, SSD_H * SSD_P, SSD_N), lambda b: (b, 0, 0, 0)))
        args.append(init)
    small = [cw, cb, dtb, alog, dsk, gs]
    in_specs += [full(a) for a in small]
    args += small
    out_specs = [pl.BlockSpec((L, D), lambda b: (b, 0))]
    out_shape = [jax.ShapeDtypeStruct((n_seq * L, D), BF16)]
    if not has_init:
        out_specs.append(pl.BlockSpec((1, 2, SSD_H * SSD_P, SSD_N), lambda b: (b, 0, 0, 0)))
        out_shape.append(jax.ShapeDtypeStruct((n_seq, 2, SSD_H * SSD_P, SSD_N), F32))
    return pl.pallas_call(
        functools.partial(_ssd_kernel, L=L, has_init=has_init),
        grid=(n_seq,),
        in_specs=in_specs,
        out_specs=out_specs,
        out_shape=out_shape,
        scratch_shapes=[
            pltpu.VMEM((L + 2 * PAD, SSD_XBC), F32),
            pltpu.VMEM((L, SSD_XBC), F32),
            pltpu.VMEM((L, D), F32),
            pltpu.VMEM((SSD_N, SSD_H * SSD_P), F32),
        ],
        compiler_params=_cparams(("arbitrary",)),
        name=f"ssd_{L}",
    )(*args)


HY_CB = 256


def filter_dft_matrices(L):
    H = L // 2
    s = np.arange(L, dtype=np.int64)[None, :]
    k = np.arange(H, dtype=np.int64)[:, None]
    ang = lambda kk: ((kk * s) % (2 * L)).astype(np.float64) * (math.pi / L)
    ca, cb = np.cos(ang(k)), np.cos(ang(L - k))
    sa, sb = np.sin(ang(k)), np.sin(ang(L - k))
    cb[0] = np.where(s[0] % 2 == 0, 1.0, -1.0)
    sa[0], sb[0] = 0.0, 0.0
    fm = np.zeros((2 * SUBLANES, L))
    fm[0], fm[1] = np.cos(ang(H))[0], np.sin(ang(H))[0]
    mats = (np.concatenate([ca, cb], axis=0), np.concatenate([sa, sb], axis=0), fm)
    return tuple(jnp.asarray(m.astype(np.float32)).astype(BF16) for m in mats)


def _const_spec(arr):
    return pl.BlockSpec(arr.shape, lambda *_: (0,) * arr.ndim, pipeline_mode=pl.Buffered(1))


def _hy_filter_kernel(feat_ref, w1_ref, b1_ref, w2_ref, b2_ref, fr_ref, w3_ref, dl_ref, fs_ref, fd_ref, fm_ref,
                      h_ref, hm_ref, *, L):
    H = L // 2
    hp = functools.partial(jnp.dot, precision=HIGHEST, preferred_element_type=F32)
    hdn = jnp.sin(fr_ref[0:1, :] * (hp(feat_ref[...], w1_ref[...]) + b1_ref[...]))
    hdn = jnp.sin(fr_ref[1:2, :] * (hp(hdn, w2_ref[...]) + b2_ref[...]))
    rowi = lax.broadcasted_iota(jnp.int32, (L, 1), 0)
    t = rowi.astype(F32) * (1.0 / (L - 1))
    dec = jnp.exp(-t * dl_ref[...])
    first = rowi == 0
    for o in range(2):
        fwd = hp(hdn, w3_ref[2 * o]) * dec
        bwd = jnp.where(first, 0.0, hp(hdn, w3_ref[2 * o + 1]) * dec)
        hs, hd = (fwd + bwd).astype(BF16), (fwd - bwd).astype(BF16)
        ss = jnp.dot(fs_ref[...], hs, preferred_element_type=F32)
        sd = jnp.dot(fd_ref[...], hd, preferred_element_type=F32)
        h_ref[o, 0] = ss[0:H]
        h_ref[o, 1] = sd[0:H]
        h_ref[o, 2] = ss[H:L]
        h_ref[o, 3] = sd[H:L]
        mid_r = jnp.dot(fm_ref[...], hs, preferred_element_type=F32)
        mid_n = jnp.dot(fm_ref[...], hd, preferred_element_type=F32)
        hm_ref[o] = jnp.concatenate([mid_r[0:1], mid_n[1:2], jnp.zeros((SUBLANES - 2, mid_r.shape[1]), F32)], axis=0)


def hyena_filter_spectra(feat, w1, b1, w2, b2, freq, w3r, deltas, *, L):
    full = lambda arr: pl.BlockSpec(arr.shape, lambda j: (0,) * arr.ndim)
    mats = filter_dft_matrices(L)
    return pl.pallas_call(
        functools.partial(_hy_filter_kernel, L=L),
        grid=(D // HY_CB,),
        in_specs=[full(feat), full(w1), full(b1), full(w2), full(b2), full(freq),
                  pl.BlockSpec((4, HY_HID, HY_CB), lambda j: (0, 0, j)),
                  pl.BlockSpec((1, HY_CB), lambda j: (0, j))] + [_const_spec(m) for m in mats],
        out_specs=[pl.BlockSpec((2, 4, L // 2, HY_CB), lambda j: (0, 0, 0, j)),
                   pl.BlockSpec((2, SUBLANES, HY_CB), lambda j: (0, 0, j))],
        out_shape=[jax.ShapeDtypeStruct((2, 4, L // 2, D), F32), jax.ShapeDtypeStruct((2, SUBLANES, D), F32)],
        compiler_params=_cparams(("arbitrary",)),
        name=f"hy_filter_{L}",
    )(feat, w1, b1, w2, b2, freq, w3r, deltas, *mats)


def split_dft_matrices(L):
    H = L // 2
    k = np.arange(H, dtype=np.int64)[:, None]
    m = np.arange(H, dtype=np.int64)[None, :]
    alt = np.where(m % 2 == 0, 1.0, -1.0)
    ang_e = ((k * m) % L).astype(np.float64) * (2 * math.pi / L)
    ang_o = ((k * (2 * m + 1)) % (2 * L)).astype(np.float64) * (math.pi / L)
    ce, se, co, so = np.cos(ang_e), np.sin(ang_e), np.cos(ang_o), np.sin(ang_o)
    se[0], so[0] = alt[0], alt[0]
    w = np.where(k == 0, 1.0, 2.0) / (2 * L)
    fe = np.concatenate([ce, se], axis=0)
    fo = np.concatenate([co, so], axis=0)
    ge = np.concatenate([(ce * w).T, se.T / L], axis=1)
    go = np.concatenate([(co * w).T, so.T / L], axis=1)
    return tuple(jnp.asarray(a.astype(np.float32)).astype(BF16) for a in (fe, fo, ge, go))


def _store_lane_blocks(ref, val):
    for c in range(ref.shape[0]):
        ref[c] = val[:, c * LANES:(c + 1) * LANES]


def _load_parity(ref, parity, n):
    return jnp.concatenate([ref[c, pl.ds(parity, n, stride=2), :] for c in range(ref.shape[0])], axis=1)


def _hyena_kernel(p0_ref, p1_ref, p2_ref, w0_ref, w1_ref, w2_ref, b0_ref, b1_ref, b2_ref, h_ref, hm_ref, hb_ref,
                  fe_ref, fo_ref, ge_ref, go_ref, o_ref, xp_s, u_s, y_s, *, L, cb):
    H = L // 2
    xp_s[0:PAD, :] = jnp.zeros((PAD, cb), F32)
    xp_s[PAD + L:2 * PAD + L, :] = jnp.zeros((PAD, cb), F32)
    first = lax.broadcasted_iota(jnp.int32, (H, 1), 0) == 0

    def conv(p_ref, w_ref, b_ref):
        xp_s[PAD:PAD + L, :] = p_ref[...]
        acc = b_ref[...] + xp_s[PAD - 1:PAD - 1 + L, :] * w_ref[0:1, :]
        for k in range(1, HY_K):
            acc = acc + xp_s[PAD - 1 + k:PAD - 1 + k + L, :] * w_ref[k:k + 1, :]
        return acc

    u = conv(p0_ref, w0_ref, b0_ref)
    for o, (p_ref, w_ref, b_ref) in enumerate(((p1_ref, w1_ref, b1_ref), (p2_ref, w2_ref, b2_ref))):
        _store_lane_blocks(u_s, u)
        se = jnp.dot(fe_ref[...], _load_parity(u_s, 0, H).astype(BF16), preferred_element_type=F32)
        so = jnp.dot(fo_ref[...], _load_parity(u_s, 1, H).astype(BF16), preferred_element_type=F32)
        e, es, od, os_ = se[0:H], se[H:L], so[0:H], so[H:L]
        b0, b1 = e + od, e - od
        b2 = jnp.where(first, es, es + os_)
        b3 = jnp.where(first, os_, os_ - es)
        har, han, hbr, hbn = h_ref[o, 0], h_ref[o, 1], h_ref[o, 2], h_ref[o, 3]
        hmr, hmn = hm_ref[o, 0:1, :], hm_ref[o, 1:2, :]
        y0 = b0 * har - b2 * han
        y1 = b1 * hbr - b3 * hbn
        y2 = b0 * han + b2 * har
        y3 = b1 * hbn + b3 * hbr
        mid_r = b2[0:1] * hmr - b3[0:1] * hmn
        mid_n = b2[0:1] * hmn + b3[0:1] * hmr
        de = jnp.where(first, mid_r, y2 - y3)
        do = jnp.where(first, mid_n, y2 + y3)
        ye = jnp.dot(ge_ref[...], jnp.concatenate([y0 + y1, de], axis=0).astype(BF16), preferred_element_type=F32)
        yo = jnp.dot(go_ref[...], jnp.concatenate([y0 - y1, do], axis=0).astype(BF16), preferred_element_type=F32)
        for c in range(cb // LANES):
            y_s[c, pl.ds(0, H, stride=2), :] = ye[:, c * LANES:(c + 1) * LANES]
            y_s[c, pl.ds(1, H, stride=2), :] = yo[:, c * LANES:(c + 1) * LANES]
        y = jnp.concatenate([y_s[c] for c in range(cb // LANES)], axis=1)
        u = conv(p_ref, w_ref, b_ref) * (y + u * hb_ref[o:o + 1, :])
    o_ref[...] = u.astype(o_ref.dtype)


def hyena_mixer(hy, conv_w, conv_b, h4, hm, hy_bias, *, L, n_seq, row_off):
    blk0 = row_off // L
    cb = min(D, HY_CB * (L_LAT // L))
    nj = D // cb
    H = L // 2
    part = lambda q: pl.BlockSpec((L, cb), lambda j, b: (blk0 + b, q * nj + j))
    wpart = lambda q: pl.BlockSpec((HY_K, cb), lambda j, b: (0, q * nj + j))
    bpart = lambda q: pl.BlockSpec((1, cb), lambda j, b: (0, q * nj + j))
    mats = split_dft_matrices(L)
    return pl.pallas_call(
        functools.partial(_hyena_kernel, L=L, cb=cb),
        grid=(nj, n_seq),
        in_specs=[part(0), part(1), part(2), wpart(0), wpart(1), wpart(2), bpart(0), bpart(1), bpart(2),
                  pl.BlockSpec((2, 4, H, cb), lambda j, b: (0, 0, 0, j)),
                  pl.BlockSpec((2, SUBLANES, cb), lambda j, b: (0, 0, j)),
                  pl.BlockSpec((2, cb), lambda j, b: (0, j))]
                 + [_const_spec(m) for m in mats],
        out_specs=pl.BlockSpec((L, cb), lambda j, b: (b, j)),
        out_shape=jax.ShapeDtypeStruct((n_seq * L, D), BF16),
        scratch_shapes=[pltpu.VMEM((L + 2 * PAD, cb), F32), pltpu.VMEM((cb // LANES, L, LANES), F32),
                        pltpu.VMEM((cb // LANES, L, LANES), F32)],
        compiler_params=_cparams(("arbitrary", "arbitrary")),
        name=f"hyena_{L}",
    )(hy, hy, hy, conv_w, conv_w, conv_w, conv_b, conv_b, conv_b, h4, hm, hy_bias, *mats)


ROUTER_LANES = LANES
BIG_LANE = 1e9


ROW_GROUP = D // LANES


def _store_row_groups(ref, val):
    n = val.shape[0]
    for s in range(ROW_GROUP):
        ref[pl.ds(s, n, stride=ROW_GROUP), :] = val[:, s * LANES:(s + 1) * LANES]


def _load_row_groups(ref, n, s):
    return ref[pl.ds(s, n, stride=ROW_GROUP), :]


def _first_max_lane(v, lanef):
    m = jnp.max(v, axis=-1, keepdims=True)
    return m, jnp.min(jnp.where(v == m, lanef, BIG_LANE), axis=-1, keepdims=True)


def _out_router_kernel(*refs, n_in, x_is_pair):
    a_refs = refs[:2 * n_in]
    refs = refs[2 * n_in:]
    is_ctx = pl.program_id(0) < CTX_TILES
    if x_is_pair:
        x = jnp.where(is_ctx, refs[0][...], refs[1][...])
        refs = refs[2:]
    else:
        x = refs[0][...]
        refs = refs[1:]
    w_ref, mod_ref, gf_ref, wr_ref, br_ref, xo_ref, h2_ref, ids_ref, wts_ref, cnt_ref = refs
    acc, k0 = None, 0
    for ac_ref, al_ref in zip(a_refs[0::2], a_refs[1::2]):
        kk = ac_ref.shape[1]
        a = jnp.where(is_ctx, ac_ref[...], al_ref[...])
        part = jnp.dot(a, w_ref[k0:k0 + kk, :], preferred_element_type=F32)
        acc = part if acc is None else acc + part
        k0 += kk
    xn = x + mod_ref[0, 2:3, :] * acc
    xo_ref[...] = xn
    h2 = _modulated(xn, gf_ref, mod_ref, 3)
    h2_ref[...] = h2

    h_hi = h2.astype(BF16)
    h_lo = (h2 - h_hi.astype(F32)).astype(BF16)
    logits = (jnp.dot(h_hi, wr_ref[0], preferred_element_type=F32) + jnp.dot(h_lo, wr_ref[0], preferred_element_type=F32)
              + jnp.dot(h_hi, wr_ref[1], preferred_element_type=F32) + br_ref[...])
    lanef = lax.broadcasted_iota(jnp.int32, logits.shape, 1).astype(F32)
    gl = jnp.where(lanef < MOE_G, logits, NEG)
    gm, gi = _first_max_lane(gl, lanef)
    g_w = 1.0 / jnp.sum(jnp.exp(gl - gm), axis=-1, keepdims=True)
    lo = MOE_G + MOE_PG * gi
    el = jnp.where((lanef >= lo) & (lanef < lo + MOE_PG), logits, NEG)
    m1, e1 = _first_max_lane(el, lanef)
    m2, e2 = _first_max_lane(jnp.where(lanef == e1, NEG, el), lanef)
    p2 = jnp.exp(m2 - m1)
    w1 = g_w / (1.0 + p2)
    ids_ref[...] = jnp.where(lanef == 0, e1 - MOE_G, jnp.where(lanef == 1, e2 - MOE_G, 0.0)).astype(jnp.int32)
    wts_ref[...] = jnp.where(lanef == 0, w1, jnp.where(lanef == 1, w1 * p2, 0.0))
    chosen = ((lanef == e1 - MOE_G) | (lanef == e2 - MOE_G)).astype(F32)
    cnt_ref[0] = jnp.sum(chosen, axis=0, keepdims=True).astype(jnp.int32)


def out_proj_router(acts, w_bf, x, mod, gf, wr, br):
    tile = lambda w: pl.BlockSpec((TM, w), lambda i: (i, 0))
    full = lambda arr: pl.BlockSpec(arr.shape, lambda i: (0,) * arr.ndim)
    x_is_pair = isinstance(x, tuple)
    xs = x if x_is_pair else (x,)
    return pl.pallas_call(
        functools.partial(_out_router_kernel, n_in=len(acts), x_is_pair=x_is_pair),
        grid=(N_TILES,),
        in_specs=[s for a in acts for s in _pair_specs(a[0].shape[1])]
                 + (_pair_specs(D) if x_is_pair else [tile(D)])
                 + [full(w_bf), pl.BlockSpec((1, 6, D), lambda i: (_mod_row(i), 0, 0)), full(gf), full(wr), full(br)],
        out_specs=[tile(D), tile(D), tile(ROUTER_LANES), tile(ROUTER_LANES),
                   pl.BlockSpec((1, 1, ROUTER_LANES), lambda i: (i, 0, 0))],
        out_shape=[jax.ShapeDtypeStruct((T_ALL, D), F32), jax.ShapeDtypeStruct((T_ALL, D), F32),
                   jax.ShapeDtypeStruct((T_ALL, ROUTER_LANES), jnp.int32),
                   jax.ShapeDtypeStruct((T_ALL, ROUTER_LANES), F32),
                   jax.ShapeDtypeStruct((N_TILES, 1, ROUTER_LANES), jnp.int32)],
        compiler_params=_cparams(("arbitrary",)),
        name="out_router",
    )(*[part for a in acts for part in a], *xs, w_bf, mod, gf, wr, br)


N_ASSIGN = 2 * T_ALL
MOE_TILES = N_ASSIGN // TM + MOE_E
N_SLOTS = MOE_TILES * TM


def route_tables(cnt3):
    cnt = cnt3[:, 0, :MOE_E]
    total = jnp.sum(cnt, axis=0)
    padded = (total + TM - 1) // TM * TM
    ends = jnp.cumsum(padded)
    gdst = (ends - padded)[None, :] + jnp.cumsum(cnt, axis=0) - cnt
    loc = jnp.cumsum(cnt, axis=1) - cnt
    starts = jnp.arange(MOE_TILES, dtype=jnp.int32) * TM
    tile_expert = jnp.minimum(jnp.sum((ends[None, :] <= starts[:, None]).astype(jnp.int32), axis=1), MOE_E - 1)
    n_used = (ends[-1] // TM).astype(jnp.int32).reshape(1)
    return cnt, loc, gdst, ends, tile_expert, n_used


RUN_BITS = (2 * TM).bit_length()


def _dispatch_kernel(cnt_s, loc_s, gdst_s, ends_s, h_ref, ids_ref, gcol_ref, xs_ref, dest_ref, srt, zbuf, sem, zsem):
    i = pl.program_id(0)
    slot = i % 2
    n_rows = 2 * TM

    @pl.when(i == 0)
    def _():
        zbuf[...] = jnp.zeros(zbuf.shape, zbuf.dtype)
        n_used = ends_s[MOE_E - 1] // TM
        for phase in ("start", "wait"):
            def tail(t, c, phase=phase):
                dst = pl.multiple_of(t * (TM * ROW_GROUP), TM * ROW_GROUP)
                cp = pltpu.make_async_copy(zbuf, xs_ref.at[pl.ds(dst, TM * ROW_GROUP), :], zsem)
                cp.start() if phase == "start" else cp.wait()
                return c

            lax.fori_loop(n_used, MOE_TILES, tail, 0)
            for e in range(MOE_E):
                end = ends_s[e]
                prev = ends_s[e - 1] if e > 0 else 0

                @pl.when(end > prev)
                def _(end=end, phase=phase):
                    dst = pl.multiple_of((end - TM) * ROW_GROUP, TM * ROW_GROUP)
                    cp = pltpu.make_async_copy(zbuf, xs_ref.at[pl.ds(dst, TM * ROW_GROUP), :], zsem)
                    cp.start() if phase == "start" else cp.wait()

    idt = ids_ref[...].astype(F32).T
    sub = lax.broadcasted_iota(jnp.int32, (LANES, TM), 0).astype(F32)
    m0 = (sub == idt[0:1, :]).astype(F32)
    m1 = (sub == idt[1:2, :]).astype(F32)
    mt = (m0 + m1).astype(BF16)
    tr = lax.broadcasted_iota(jnp.int32, (TM, TM), 0)
    tc = lax.broadcasted_iota(jnp.int32, (TM, TM), 1)
    earlier = jnp.dot(mt, (tr < tc).astype(BF16), preferred_element_type=F32)
    er = lax.broadcasted_iota(jnp.int32, (LANES, LANES), 0)
    ec = lax.broadcasted_iota(jnp.int32, (LANES, LANES), 1)
    below = jnp.dot((ec < er).astype(BF16), mt, preferred_element_type=F32)
    local = jnp.sum(below, axis=1, keepdims=True) + earlier
    glob = gcol_ref[0] + earlier
    pos0 = jnp.sum(m0 * local, axis=0, keepdims=True)
    pos1 = jnp.sum(m1 * local, axis=0, keepdims=True)
    dest_ref[0] = jnp.concatenate([jnp.sum(m0 * glob, axis=0, keepdims=True),
                                   jnp.sum(m1 * glob, axis=0, keepdims=True)], axis=0).astype(jnp.int32)

    srow = lax.broadcasted_iota(jnp.int32, (n_rows, TM), 0).astype(F32)
    perm = jnp.where((srow == pos0) | (srow == pos1), 1.0, 0.0).astype(BF16)
    _store_row_groups(srt.at[slot], jnp.dot(perm, h_ref[...].astype(BF16), preferred_element_type=F32))

    for e in range(MOE_E):
        n, s0, d0 = cnt_s[0, 0, e], loc_s[0, 0, e], gdst_s[0, 0, e]
        for b in reversed(range(RUN_BITS)):
            size = 1 << b
            off = (n >> (b + 1)) << (b + 1)

            @pl.when(((n >> b) & 1) == 1)
            def _(size=size, off=off, s0=s0, d0=d0):
                src = pl.multiple_of((s0 + off) * ROW_GROUP, ROW_GROUP)
                dst = pl.multiple_of((d0 + off) * ROW_GROUP, ROW_GROUP)
                pltpu.make_async_copy(srt.at[slot, pl.ds(src, size * ROW_GROUP), :],
                                      xs_ref.at[pl.ds(dst, size * ROW_GROUP), :], sem.at[slot]).start()

    def wait(s):
        pltpu.make_async_copy(srt.at[s], xs_ref.at[pl.ds(0, n_rows * ROW_GROUP), :], sem.at[s]).wait()

    @pl.when(i > 0)
    def _():
        wait(1 - slot)

    @pl.when(i == N_TILES - 1)
    def _():
        wait(slot)


def dispatch_rows(h2, ids, cnt, loc, gdst, ends):
    tab = lambda: pl.BlockSpec((1, 1, MOE_E), lambda i: (i, 0, 0), memory_space=pltpu.SMEM)
    gcol = jnp.pad(gdst.astype(F32), ((0, 0), (0, LANES - MOE_E)))[:, :, None]
    return pl.pallas_call(
        _dispatch_kernel,
        grid=(N_TILES,),
        in_specs=[tab(), tab(), tab(), pl.BlockSpec(memory_space=pltpu.SMEM),
                  pl.BlockSpec((TM, D), lambda i: (i, 0)), pl.BlockSpec((TM, ROUTER_LANES), lambda i: (i, 0)),
                  pl.BlockSpec((1, LANES, 1), lambda i: (i, 0, 0))],
        out_specs=[pl.BlockSpec(memory_space=pl.ANY), pl.BlockSpec((1, 2, TM), lambda i: (i, 0, 0))],
        out_shape=[jax.ShapeDtypeStruct((N_SLOTS * ROW_GROUP, LANES), F32),
                   jax.ShapeDtypeStruct((N_TILES, 2, TM), jnp.int32)],
        scratch_shapes=[pltpu.VMEM((2, 2 * TM * ROW_GROUP, LANES), F32), pltpu.VMEM((TM * ROW_GROUP, LANES), F32),
                        pltpu.SemaphoreType.DMA((2,)), pltpu.SemaphoreType.DMA(())],
        compiler_params=_cparams(("arbitrary",)),
        name="moe_dispatch",
    )(cnt.reshape(N_TILES, 1, MOE_E), loc.reshape(N_TILES, 1, MOE_E), gdst.reshape(N_TILES, 1, MOE_E),
      ends, h2, ids, gcol)


DMA_UNROLL = 8


def _start_group_gather(src_hbm, idx_ref, n, dst_ref, sem):
    def body(j, c):
        for u in range(DMA_UNROLL):
            r = j * DMA_UNROLL + u
            src = pl.multiple_of(idx_ref[0, 0, r] * ROW_GROUP, ROW_GROUP)
            dst = pl.multiple_of(r * ROW_GROUP, ROW_GROUP)
            pltpu.make_async_copy(src_hbm.at[pl.ds(src, ROW_GROUP), :], dst_ref.at[pl.ds(dst, ROW_GROUP), :],
                                  sem).start(priority=u % 2)
        return c

    lax.fori_loop(0, n // DMA_UNROLL, body, 0)


def _wait_group_gather(src_hbm, dst_ref, sem):
    pltpu.make_async_copy(src_hbm.at[pl.ds(0, dst_ref.shape[0]), :], dst_ref, sem).wait()


def _experts_kernel(te_ref, nu_ref, x_ref, wg_ref, wu_ref, wd_ref, o_ref, xcat):
    i = pl.program_id(0)

    @pl.when(i < nu_ref[0])
    def _():
        for s in range(ROW_GROUP):
            xcat[:, s * LANES:(s + 1) * LANES] = _load_row_groups(x_ref, TM, s).astype(BF16)
        x = xcat[...]
        g = jnp.dot(x, wg_ref[0, 0].astype(BF16), preferred_element_type=F32)
        u = jnp.dot(x, wu_ref[0, 0].astype(BF16), preferred_element_type=F32)
        hid = (_silu(g) * u).astype(BF16)
        _store_row_groups(o_ref, jnp.dot(hid, wd_ref[0, 0].astype(BF16), preferred_element_type=F32))

    @pl.when(i >= nu_ref[0])
    def _():
        o_ref[...] = jnp.zeros(o_ref.shape, o_ref.dtype)


def grouped_experts(xs, w_gate, w_up, w_down, tile_expert, n_used, layer):
    wspec = lambda a, b: pl.BlockSpec((1, 1, a, b), lambda i, te, nu: (layer, te[i], 0, 0))
    return pl.pallas_call(
        _experts_kernel,
        grid_spec=pltpu.PrefetchScalarGridSpec(
            num_scalar_prefetch=2,
            grid=(MOE_TILES,),
            in_specs=[pl.BlockSpec((TM * ROW_GROUP, LANES), lambda i, te, nu: (jnp.minimum(i, nu[0] - 1), 0)),
                      wspec(D, MOE_F), wspec(D, MOE_F), wspec(MOE_F, D)],
            out_specs=pl.BlockSpec((TM * ROW_GROUP, LANES), lambda i, te, nu: (i, 0)),
            scratch_shapes=[pltpu.VMEM((TM, D), BF16)],
        ),
        out_shape=jax.ShapeDtypeStruct((N_SLOTS * ROW_GROUP, LANES), F32),
        compiler_params=_cparams(("arbitrary",)),
        name="moe_experts",
    )(tile_expert, n_used, xs, w_gate, w_up, w_down)


def _combine_kernel(cur_ref, nxt_ref, ys_hbm, x_ref, wts_ref, mod_ref, gfin_ref, o_ref, buf, sem, *, final):
    i = pl.program_id(0)
    slot = i % 2

    @pl.when(i == 0)
    def _():
        _start_group_gather(ys_hbm, cur_ref, 2 * TM, buf.at[0], sem.at[0])

    @pl.when(i + 1 < N_TILES)
    def _():
        _start_group_gather(ys_hbm, nxt_ref, 2 * TM, buf.at[1 - slot], sem.at[1 - slot])

    _wait_group_gather(ys_hbm, buf.at[slot], sem.at[slot])
    w0, w1 = wts_ref[:, 0:1], wts_ref[:, 1:2]
    for s in range(ROW_GROUP):
        cols = slice(s * LANES, (s + 1) * LANES)
        y0 = buf[slot, pl.ds(s, TM, stride=ROW_GROUP), :]
        y1 = buf[slot, pl.ds(TM * ROW_GROUP + s, TM, stride=ROW_GROUP), :]
        o_ref[:, cols] = x_ref[:, cols] + mod_ref[0, 5:6, cols] * (w0 * y0 + w1 * y1)
    if final:
        o_ref[...] = _rms(o_ref[...]) * gfin_ref[...]


def moe_combine(ys, dest, x, wts, mod, gfin, *, final):
    tile = lambda w: pl.BlockSpec((TM, w), lambda i: (i, 0))
    idx = lambda f: pl.BlockSpec((1, 1, 2 * TM), lambda i: (f(i), 0, 0), memory_space=pltpu.SMEM)
    dest3 = dest.reshape(N_TILES, 1, 2 * TM)
    return pl.pallas_call(
        functools.partial(_combine_kernel, final=final),
        grid=(N_TILES,),
        in_specs=[idx(lambda i: i), idx(lambda i: jnp.minimum(i + 1, N_TILES - 1)),
                  pl.BlockSpec(memory_space=pl.ANY), tile(D), tile(ROUTER_LANES),
                  pl.BlockSpec((1, 6, D), lambda i: (_mod_row(i), 0, 0)),
                  pl.BlockSpec((1, D), lambda i: (0, 0))],
        out_specs=tile(D),
        out_shape=jax.ShapeDtypeStruct((T_ALL, D), F32),
        scratch_shapes=[pltpu.VMEM((2, 2 * TM * ROW_GROUP, LANES), F32), pltpu.SemaphoreType.DMA((2,))],
        compiler_params=_cparams(("arbitrary",)),
        name="moe_combine",
    )(dest3, dest3, ys, x, wts, mod, gfin)


ODD_COLS = 2048
ROPE_Q = MLA_H * MLA_ROPE
ROPE_SHIFT = ROPE_F


def rope_tables():
    t = np.arange(L_LAT)
    pos = np.stack([t // GRID_W, t % GRID_W], axis=1).astype(np.float64)
    inv = 10000.0 ** (-np.arange(ROPE_F, dtype=np.float64) / ROPE_F)
    lane = np.arange(ROPE_Q) % MLA_ROPE
    axis = lane // (2 * ROPE_F)
    first = (lane % (2 * ROPE_F)) < ROPE_F
    ang = pos[:, axis] * inv[lane % ROPE_F][None, :]
    cos, sin = np.cos(ang), np.sin(ang)
    tabs = [cos, np.where(first[None, :], -sin, 0.0), np.where(first[None, :], 0.0, sin)]
    ident = [np.ones((1, TM, ROPE_Q)), np.zeros((1, TM, ROPE_Q)), np.zeros((1, TM, ROPE_Q))]
    return [jnp.asarray(np.concatenate([i, tb.reshape(LAT_TILES_PER_SEQ, TM, ROPE_Q)], axis=0).astype(np.float32))
            for i, tb in zip(ident, tabs)]


def _rope(x, c, a, b):
    n = x.shape[1]
    return x * c[:, :n] + pltpu.roll(x, n - ROPE_SHIFT, 1) * a[:, :n] + pltpu.roll(x, ROPE_SHIFT, 1) * b[:, :n]


def _odd_in_kernel(x_ref, mod_ref, g_ref, w_ref, gq_ref, wuq_ref, gkv_ref, wukv_ref, rc_ref, ra_ref, rb_ref,
                   qkv_ref, qm_ref, ckv_ref, kvu_ref, kr_ref):
    hb = _modulated(x_ref[...], g_ref, mod_ref, 0).astype(BF16)
    for c0 in range(0, 3 * NA_W, PROJ_CHUNK):
        qkv_ref[:, c0:c0 + PROJ_CHUNK] = jnp.dot(hb, w_ref[:, c0:c0 + PROJ_CHUNK], preferred_element_type=F32)
    rest = jnp.dot(hb, w_ref[:, 3 * NA_W:ODD_COLS], preferred_element_type=F32)
    rc, ra, rb = rc_ref[0], ra_ref[0], rb_ref[0]
    qd = (_rms(rest[:, 0:MLA_QR]) * gq_ref[...]).astype(BF16)
    qm = jnp.dot(qd, wuq_ref[...], preferred_element_type=F32)
    qm_ref[:, 0:MLA_H * MLA_NOPE] = qm[:, 0:MLA_H * MLA_NOPE]
    qm_ref[:, MLA_H * MLA_NOPE:] = _rope(qm[:, MLA_H * MLA_NOPE:], rc, ra, rb)
    ckv = _rms(rest[:, MLA_QR:MLA_QR + MLA_KVR]) * gkv_ref[...]
    ckv_ref[...] = ckv
    kvu_ref[...] = jnp.dot(ckv.astype(BF16), wukv_ref[...], preferred_element_type=F32)
    kr_ref[...] = _rope(rest[:, MLA_QR + MLA_KVR:], rc, ra, rb)


def odd_in_proj(x, mod, g, w_bf, gq, wuq_bf, gkv, wukv_bf, tabs):
    tile = lambda w: pl.BlockSpec((TM, w), lambda i: (i, 0))
    full = lambda arr: pl.BlockSpec(arr.shape, lambda i: (0,) * arr.ndim)
    tab = pl.BlockSpec((1, TM, ROPE_Q),
                       lambda i: (jnp.where(i < CTX_TILES, 0, 1 + (i - CTX_TILES) % LAT_TILES_PER_SEQ), 0, 0))
    widths = (3 * NA_W, MLA_H * MLA_QK, MLA_KVR, MLA_H * (MLA_NOPE + MLA_V), LANES)
    return pl.pallas_call(
        _odd_in_kernel,
        grid=(N_TILES,),
        in_specs=[tile(D), pl.BlockSpec((1, 6, D), lambda i: (_mod_row(i), 0, 0)), full(g), full(w_bf),
                  full(gq), full(wuq_bf), full(gkv), full(wukv_bf), tab, tab, tab],
        out_specs=[tile(w) for w in widths],
        out_shape=[jax.ShapeDtypeStruct((T_ALL, w), F32) for w in widths],
        compiler_params=_cparams(("arbitrary",)),
        name="odd_in",
    )(x, mod, g, w_bf, gq, wuq_bf, gkv, wukv_bf, *tabs)


LOG2E = math.log2(math.e)
NA_QSCALE = NA_D ** -0.5 * LOG2E
MLA_QSCALE = MLA_QK ** -0.5 * LOG2E
NT = (((1,), (1,)), ((), ()))


def _softmax_pv(scores, values):
    m = functools.reduce(jnp.maximum, [jnp.max(s, axis=-1, keepdims=True) for s in scores])
    ps = [jnp.exp2(s - m) for s in scores]
    den = functools.reduce(jnp.add, [jnp.sum(p, axis=-1, keepdims=True) for p in ps])
    acc = functools.reduce(jnp.add, [jnp.dot(p.astype(BF16), v, preferred_element_type=F32) for p, v in zip(ps, values)])
    return acc / den


def _pair(ref_or_val, p, base=0):
    return ref_or_val[:, base + p * LANES:base + (p + 1) * LANES]


def _low_half():
    return lax.broadcasted_iota(jnp.int32, (1, LANES), 1) < NA_D


def _rope_key_forms(kr):
    return kr.astype(BF16), pltpu.roll(kr, LANES // 2, 1).astype(BF16)


def _mla_pair(qm, p, sources, lo):
    outs = []
    for e in range(2):
        h = 2 * p + e
        qn = qm[:, h * MLA_NOPE:(h + 1) * MLA_NOPE] * MLA_QSCALE
        qr = qm[:, MLA_H * MLA_NOPE + h * MLA_ROPE:MLA_H * MLA_NOPE + (h + 1) * MLA_ROPE] * MLA_QSCALE
        z = jnp.zeros((qn.shape[0], LANES - MLA_QK), F32)
        qcat = jnp.concatenate([qn, qr, z] if e == 0 else [qr, z, qn], axis=1).astype(BF16)
        scores = []
        for kb, kr_lo, kr_hi, _ in sources:
            kcat = jnp.where(lo, kb, kr_hi) if e == 0 else jnp.where(lo, kr_lo, kb)
            scores.append(lax.dot_general(qcat, kcat, NT, preferred_element_type=F32))
        outs.append(_softmax_pv(scores, [src[3] for src in sources]))
    return jnp.where(lo, outs[0], outs[1])


def _attn_ctx_kernel(qkv_ref, qm_ref, kvu_ref, kr_ref, ona_ref, omla_ref):
    lo = _low_half()
    for p in range(NA_H // 2):
        qb = _pair(qkv_ref, p) * NA_QSCALE
        kb = _pair(qkv_ref, p, NA_W).astype(BF16)
        vb = _pair(qkv_ref, p, 2 * NA_W).astype(BF16)
        outs = []
        for e in range(2):
            q = jnp.where(lo if e == 0 else jnp.logical_not(lo), qb, 0.0).astype(BF16)
            outs.append(_softmax_pv([lax.dot_general(q, kb, NT, preferred_element_type=F32)], [vb]))
        ona_ref[:, p * LANES:(p + 1) * LANES] = jnp.where(lo, outs[0], outs[1]).astype(ona_ref.dtype)
    kr_lo, kr_hi = _rope_key_forms(kr_ref[...])
    for p in range(MLA_H // 2):
        src = (_pair(kvu_ref, p).astype(BF16), kr_lo, kr_hi, _pair(kvu_ref, p, MLA_H * MLA_NOPE).astype(BF16))
        omla_ref[:, p * LANES:(p + 1) * LANES] = _mla_pair(qm_ref, p, [src], lo).astype(omla_ref.dtype)


def attn_context(qkv, qm, kvu, kr):
    seq = lambda w: pl.BlockSpec((L_CTX, w), lambda b: (b, 0))
    return pl.pallas_call(
        _attn_ctx_kernel,
        grid=(N_CTX,),
        in_specs=[seq(3 * NA_W), seq(MLA_H * MLA_QK), seq(MLA_H * (MLA_NOPE + MLA_V)), seq(LANES)],
        out_specs=[seq(NA_W), seq(MLA_H * MLA_V)],
        out_shape=[jax.ShapeDtypeStruct((T_CTX, NA_W), BF16), jax.ShapeDtypeStruct((T_CTX, MLA_H * MLA_V), BF16)],
        compiler_params=_cparams(("arbitrary",)),
        name="attn_ctx",
    )(qkv, qm, kvu, kr)


N_DR = 2 * NA_WIN_R - 1
GRID_ROWS = L_LAT // GRID_W


def _na_bias_kernel(t_ref, o_ref):
    neg = jnp.full((GRID_W, GRID_W), NEG, F32)
    for r in range(GRID_ROWS):
        r0 = min(max(r - NA_WIN_R // 2, 0), GRID_ROWS - NA_WIN_R)
        for kr in range(GRID_ROWS):
            in_window = r0 <= kr < r0 + NA_WIN_R
            blk = t_ref[0, kr - r + NA_WIN_R - 1] if in_window else neg
            o_ref[0, r * GRID_W:(r + 1) * GRID_W, kr * GRID_W:(kr + 1) * GRID_W] = blk


def neighbourhood_bias(rel_bias):
    c = np.arange(GRID_W)
    c0 = np.clip(c - NA_WIN_C // 2, 0, GRID_W - NA_WIN_C)
    col_ok = (c[None, :] >= c0[:, None]) & (c[None, :] < c0[:, None] + NA_WIN_C)
    dc = np.clip(c[None, :] - c[:, None], -(NA_WIN_C - 1), NA_WIN_C - 1) + NA_WIN_C - 1
    sel_c = (dc[:, :, None] == np.arange(2 * NA_WIN_C - 1)).astype(np.float32)
    t = jnp.einsum("hdj,qcj->hdqc", rel_bias.astype(F32), jnp.asarray(sel_c), precision=HIGHEST)
    t = jnp.where(jnp.asarray(col_ok)[None, None], t * LOG2E, NEG)
    return pl.pallas_call(
        _na_bias_kernel,
        grid=(NA_H,),
        in_specs=[pl.BlockSpec((1, N_DR, GRID_W, GRID_W), lambda h: (h, 0, 0, 0))],
        out_specs=pl.BlockSpec((1, L_LAT, L_LAT), lambda h: (h, 0, 0)),
        out_shape=jax.ShapeDtypeStruct((NA_H, L_LAT, L_LAT), F32),
        compiler_params=_cparams(("arbitrary",)),
        name="na_bias",
    )(t)


def _na_lat_kernel(q_ref, k_ref, v_ref, kc_ref, vc_ref, b_ref, o_ref):
    lo = _low_half()
    for p in range(NA_H // 2):
        qb = _pair(q_ref, p) * NA_QSCALE
        kb = _pair(k_ref, p).astype(BF16)
        vb = _pair(v_ref, p).astype(BF16)
        outs = []
        for e in range(2):
            h = 2 * p + e
            half = slice(e * NA_D, (e + 1) * NA_D)
            q = jnp.where(lo if e == 0 else jnp.logical_not(lo), qb, 0.0).astype(BF16)
            s1 = lax.dot_general(q, kb, NT, preferred_element_type=F32) + b_ref[h]
            s2 = lax.dot_general(qb[:, half].astype(BF16), kc_ref[0, 0, h].astype(BF16), NT, preferred_element_type=F32)
            m = jnp.maximum(jnp.max(s1, axis=-1, keepdims=True), jnp.max(s2, axis=-1, keepdims=True))
            p1, p2 = jnp.exp2(s1 - m), jnp.exp2(s2 - m)
            den = jnp.sum(p1, axis=-1, keepdims=True) + jnp.sum(p2, axis=-1, keepdims=True)
            a1 = jnp.dot(p1.astype(BF16), vb, preferred_element_type=F32)
            a2 = jnp.dot(p2.astype(BF16), vc_ref[0, 0, h].astype(BF16), preferred_element_type=F32)
            outs.append((a1[:, half] + a2) / den)
        o_ref[:, p * LANES:(p + 1) * LANES] = jnp.concatenate(outs, axis=1).astype(o_ref.dtype)


def attn_neighbourhood_latent(qkv, cache_k, cache_v, bias):
    nq = L_LAT // TM
    t0 = T_CTX // TM
    s0 = T_CTX // L_LAT
    cache = pl.BlockSpec((1, 1, NA_H, PAST, NA_D), lambda qt, b: (b, 0, 0, 0, 0))
    return pl.pallas_call(
        _na_lat_kernel,
        grid=(nq, N_LAT),
        in_specs=[pl.BlockSpec((TM, NA_W), lambda qt, b: (t0 + b * nq + qt, 0)),
                  pl.BlockSpec((L_LAT, NA_W), lambda qt, b: (s0 + b, 1)),
                  pl.BlockSpec((L_LAT, NA_W), lambda qt, b: (s0 + b, 2)),
                  cache, cache,
                  pl.BlockSpec((NA_H, TM, L_LAT), lambda qt, b: (0, qt, 0))],
        out_specs=pl.BlockSpec((TM, NA_W), lambda qt, b: (b * nq + qt, 0)),
        out_shape=jax.ShapeDtypeStruct((T_LAT, NA_W), BF16),
        compiler_params=_cparams(("arbitrary", "arbitrary")),
        name="attn_na_lat",
    )(qkv, qkv, qkv, cache_k, cache_v, bias)


def _mla_lat_kernel(qm_ref, kvu_ref, kr_ref, ckv_ref, krc_ref, wukv_ref, o_ref):
    lo = _low_half()
    kvc = jnp.dot(ckv_ref[0, 0].astype(BF16), wukv_ref[...], preferred_element_type=F32)
    kr_lo, kr_hi = _rope_key_forms(kr_ref[...])
    krc = jnp.concatenate([krc_ref[0, 0], jnp.zeros((PAST, LANES - MLA_ROPE), F32)], axis=1)
    krc_lo, krc_hi = _rope_key_forms(krc)
    vbase = MLA_H * MLA_NOPE
    for p in range(MLA_H // 2):
        lat = (_pair(kvu_ref, p).astype(BF16), kr_lo, kr_hi, _pair(kvu_ref, p, vbase).astype(BF16))
        ctx = (_pair(kvc, p).astype(BF16), krc_lo, krc_hi, _pair(kvc, p, vbase).astype(BF16))
        o_ref[:, p * LANES:(p + 1) * LANES] = _mla_pair(qm_ref, p, [lat, ctx], lo).astype(o_ref.dtype)


def attn_mla_latent(qm, kvu, kr, cache_ckv, cache_krope, wukv_bf):
    nq = L_LAT // TM
    t0 = T_CTX // TM
    s0 = T_CTX // L_LAT
    return pl.pallas_call(
        _mla_lat_kernel,
        grid=(nq, N_LAT),
        in_specs=[pl.BlockSpec((TM, MLA_H * MLA_QK), lambda qt, b: (t0 + b * nq + qt, 0)),
                  pl.BlockSpec((L_LAT, MLA_H * (MLA_NOPE + MLA_V)), lambda qt, b: (s0 + b, 0)),
                  pl.BlockSpec((L_LAT, LANES), lambda qt, b: (s0 + b, 0)),
                  pl.BlockSpec((1, 1, PAST, MLA_KVR), lambda qt, b: (b, 0, 0, 0)),
                  pl.BlockSpec((1, 1, PAST, MLA_ROPE), lambda qt, b: (b, 0, 0, 0)),
                  pl.BlockSpec(wukv_bf.shape, lambda qt, b: (0, 0))],
        out_specs=pl.BlockSpec((TM, MLA_H * MLA_V), lambda qt, b: (b * nq + qt, 0)),
        out_shape=jax.ShapeDtypeStruct((T_LAT, MLA_H * MLA_V), BF16),
        compiler_params=_cparams(("arbitrary", "arbitrary")),
        name="attn_mla_lat",
    )(qm, kvu, kr, cache_ckv, cache_krope, wukv_bf)


def moe_block(h2, ids, wts, cnt3, x, mod, gfin, w_gate, w_up, w_down, layer, *, final):
    cnt, loc, gdst, ends, tile_expert, n_used = route_tables(cnt3)
    xs, dest = dispatch_rows(h2, ids, cnt, loc, gdst, ends)
    ys = grouped_experts(xs, w_gate, w_up, w_down, tile_expert, n_used, layer)
    return moe_combine(ys, dest, x, wts, mod, gfin, final=final)


def _pad_lanes(a):
    return jnp.pad(a, ((0, 0), (0, LANES - a.shape[1])))


def _hyena_features(L):
    t = np.linspace(0.0, 1.0, L)[:, None]
    w = 2.0 * math.pi * np.arange(L) / L
    bands = np.linspace(1e-4, HY_BANDS - 1, HY_BANDS)
    ang = w[:, None] * bands[None]
    feat = np.concatenate([t, np.cos(ang), -np.sin(ang)], axis=-1)
    return jnp.asarray(np.pad(feat, ((0, 0), (0, LANES - HY_FEAT))).astype(np.float32))


def _router_params(w_gr, b_gr, w_er, b_er):
    wr = _pad_lanes(jnp.concatenate([w_gr, w_er], axis=1))
    br = _pad_lanes(jnp.concatenate([b_gr, b_er])[None])
    wr_hi = wr.astype(BF16)
    wr_lo = (wr - wr_hi.astype(F32)).astype(BF16)
    return jnp.stack([wr_hi, wr_lo]), br


def _even_layer(x, mod, g_mix, state, w_in, conv_w, conv_b, a_log, dt_bias, d_skip, g_ssd, hy_conv_w, hy_conv_b,
                hy_w1, hy_b1, hy_w2, hy_b2, hy_w3, hy_freq, hy_bias):
    n0 = D + SSD_XBC
    w_bf = jnp.concatenate([w_in[:, :n0], w_in[:, n0 + SSD_H:], w_in[:, n0:n0 + SSD_H],
                            jnp.zeros((D, LANES - SSD_H), F32)], axis=1).astype(BF16)
    z, xbc, hy, dtr = even_in_proj(x, mod, g_mix, w_bf)
    small = (conv_w, conv_b[None], _pad_lanes(dt_bias), _pad_lanes(a_log), jnp.repeat(d_skip, SSD_P)[None], g_ssd[None])
    y_c, fin = ssd_mixer(xbc, dtr, z, None, *small, L=L_CTX, n_seq=N_CTX, row_off=0)
    (y_l,) = ssd_mixer(xbc, dtr, z, state.reshape(N_LAT, 2, SSD_H * SSD_P, SSD_N), *small,
                       L=L_LAT, n_seq=N_LAT, row_off=T_CTX)
    w1 = jnp.pad(hy_w1, ((0, LANES - HY_FEAT), (0, 0)))
    w3r = hy_w3.reshape(HY_HID, 4, D).transpose(1, 0, 2)
    deltas = jnp.asarray(np.linspace(HY_MIN_DECAY, HY_MAX_DECAY, D).astype(np.float32))[None]
    us = []
    for L, n_seq, off in ((L_CTX, N_CTX, 0), (L_LAT, N_LAT, T_CTX)):
        h4, hm = hyena_filter_spectra(_hyena_features(L), w1, hy_b1[None], hy_w2, hy_b2[None], hy_freq, w3r, deltas, L=L)
        us.append(hyena_mixer(hy, hy_conv_w, hy_conv_b[None], h4, hm, hy_bias, L=L, n_seq=n_seq, row_off=off))
    return (y_c, y_l), tuple(us), fin


def _odd_layer(x, mod, g_mix, cache_k, cache_v, cache_ckv, cache_kr, rel_bias, w_in, g_q, w_uq, g_kv, w_ukv):
    w_bf = jnp.pad(w_in, ((0, 0), (0, ODD_COLS - w_in.shape[1]))).astype(BF16)
    wuq = w_uq.reshape(MLA_QR, MLA_H, MLA_QK)
    wuq_bf = jnp.concatenate([wuq[:, :, :MLA_NOPE].reshape(MLA_QR, -1), wuq[:, :, MLA_NOPE:].reshape(MLA_QR, -1)],
                             axis=1).astype(BF16)
    wukv = w_ukv.reshape(MLA_KVR, MLA_H, MLA_NOPE + MLA_V)
    wukv_bf = jnp.concatenate([wukv[:, :, :MLA_NOPE].reshape(MLA_KVR, -1), wukv[:, :, MLA_NOPE:].reshape(MLA_KVR, -1)],
                              axis=1).astype(BF16)
    qkv, qm, ckv, kvu, kr = odd_in_proj(x, mod, g_mix, w_bf, g_q[None], wuq_bf, g_kv[None], wukv_bf, rope_tables())
    ona_c, omla_c = attn_context(qkv, qm, kvu, kr)
    ona_l = attn_neighbourhood_latent(qkv, cache_k, cache_v, neighbourhood_bias(rel_bias))
    omla_l = attn_mla_latent(qm, kvu, kr, cache_ckv, cache_kr, wukv_bf)
    return (ona_c, ona_l), (omla_c, omla_l), qkv, ckv, kr


def kernel(x_prompt, x_sample, state_ssd, cache_na_k, cache_na_v, cache_mla_ckv, cache_mla_krope, c, c_ctx, w_ada, b_ada, norm_mix, norm_ffn, norm_final, ev_w_in, ev_conv_w, ev_conv_b, ssd_A_log, ssd_dt_bias, ssd_d, ssd_norm, hy_conv_w, hy_conv_b, hy_w1, hy_b1, hy_w2, hy_b2, hy_w3, hy_freq, hy_bias, ev_w_out, od_w_in, mla_q_norm, mla_w_uq, mla_kv_norm, mla_w_ukv, na_rel_bias, od_w_out, moe_w_gr, moe_b_gr, moe_w_er, moe_b_er, moe_w_gate, moe_w_up, moe_w_down):
    x = (x_prompt.reshape(T_CTX, D), x_sample.reshape(T_LAT, D))
    cvec = jnp.zeros((MOD_ROWS, D), F32).at[0].set(c_ctx).at[1:1 + N_LAT].set(c)
    mod = ada_modulation(cvec, w_ada, b_ada)
    gfin = norm_final[None]

    y, u, fin = _even_layer(x, mod[0], norm_mix[0][None], state_ssd[:, 0], ev_w_in[0], ev_conv_w[0], ev_conv_b[0],
                            ssd_A_log[0], ssd_dt_bias[0], ssd_d[0], ssd_norm[0], hy_conv_w[0], hy_conv_b[0],
                            hy_w1[0], hy_b1[0], hy_w2[0], hy_b2[0], hy_w3[0], hy_freq[0], hy_bias[0])
    wr, br = _router_params(moe_w_gr[0], moe_b_gr[0], moe_w_er[0], moe_b_er[0])
    xn, h2, ids, wts, cnt3 = out_proj_router([y, u], ev_w_out[0].astype(BF16), x, mod[0], norm_ffn[0][None], wr, br)
    x = moe_block(h2, ids, wts, cnt3, xn, mod[0], gfin, moe_w_gate, moe_w_up, moe_w_down, 0, final=False)

    o_na, o_mla, qkv, ckv, kr = _odd_layer(x, mod[1], norm_mix[1][None], cache_na_k, cache_na_v, cache_mla_ckv,
                                           cache_mla_krope, na_rel_bias[0], od_w_in[0], mla_q_norm[0], mla_w_uq[0],
                                           mla_kv_norm[0], mla_w_ukv[0])
    wr, br = _router_params(moe_w_gr[1], moe_b_gr[1], moe_w_er[1], moe_b_er[1])
    xn, h2, ids, wts, cnt3 = out_proj_router([o_na, o_mla], od_w_out[0].astype(BF16), x, mod[1], norm_ffn[1][None], wr, br)
    out = moe_block(h2, ids, wts, cnt3, xn, mod[1], gfin, moe_w_gate, moe_w_up, moe_w_down, 1, final=True)

    heads = lambda a: a.reshape(N_CTX, L_CTX, NA_H, NA_D).transpose(0, 2, 1, 3)[:, None]
    return (out[:T_CTX].reshape(N_CTX, L_CTX, D),
            out[T_CTX:].reshape(N_LAT, L_LAT, D),
            fin.reshape(N_CTX, 1, 2, SSD_H, SSD_P, SSD_N),
            heads(qkv[:T_CTX, NA_W:2 * NA_W]),
            heads(qkv[:T_CTX, 2 * NA_W:3 * NA_W]),
            ckv[:T_CTX].reshape(N_CTX, 1, L_CTX, MLA_KVR),
            kr[:T_CTX, :MLA_ROPE].reshape(N_CTX, 1, L_CTX, MLA_ROPE))
```

```python
import functools
import math

import numpy as np
import jax
import jax.numpy as jnp
from jax import lax
from jax.experimental import pallas as pl
from jax.experimental.pallas import tpu as pltpu

F32 = jnp.float32
BF16 = jnp.bfloat16
HIGHEST = lax.Precision.HIGHEST

D = 1024
N_CTX, L_CTX = 16, 256
N_LAT, L_LAT = 8, 1024
T_CTX = N_CTX * L_CTX
T_LAT = N_LAT * L_LAT
T_ALL = T_CTX + T_LAT
PAST = 512
GRID_W = 64
EPS = 1e-6
NEG = -1e30

SSD_H, SSD_P, SSD_N, SSD_G = 16, 64, 128, 2
SSD_XBC = D + 2 * SSD_G * SSD_N
SSD_K = 5
CHUNK = 128

HY_K = 3
HY_BANDS = 16
HY_FEAT = 1 + 2 * HY_BANDS
HY_HID = 64
HY_MIN_DECAY = abs(math.log(1e-2) / 1.5)
HY_MAX_DECAY = abs(math.log(1e-2) / 0.3)

NA_H, NA_D = 8, 64
NA_W = NA_H * NA_D
NA_WIN_R, NA_WIN_C = 8, 16
MLA_H, MLA_QR, MLA_KVR = 8, 256, 128
MLA_NOPE, MLA_ROPE, MLA_V = 64, 32, 64
MLA_QK = MLA_NOPE + MLA_ROPE
ROPE_F = MLA_ROPE // 4

MOE_G, MOE_PG, MOE_E, MOE_F = 4, 8, 32, 256

LANES = 128
SUBLANES = 8
VMEM_LIMIT = 56 * 1024 * 1024

TM = 256
N_TILES = T_ALL // TM
CTX_TILES = T_CTX // TM
LAT_TILES_PER_SEQ = L_LAT // TM
MOD_ROWS = 16


def _cparams(sem):
    return pltpu.CompilerParams(dimension_semantics=sem, vmem_limit_bytes=VMEM_LIMIT)


def _mod_row(i):
    return jnp.where(i < CTX_TILES, 0, 1 + (i - CTX_TILES) // LAT_TILES_PER_SEQ)


def _silu(x):
    return x * jax.nn.sigmoid(x)


def _rms(x):
    return x * lax.rsqrt(jnp.mean(x * x, axis=-1, keepdims=True) + EPS)


def _ada_kernel(c_ref, w_ref, b_ref, o_ref):
    c = c_ref[...]
    o_ref[0] = jnp.dot(_silu(c), w_ref[0], precision=HIGHEST, preferred_element_type=F32) + b_ref[0]


def ada_modulation(cvec, w_ada, b_ada):
    depth = w_ada.shape[0]
    out = pl.pallas_call(
        _ada_kernel,
        grid=(depth, 6),
        in_specs=[
            pl.BlockSpec((MOD_ROWS, D), lambda l, j: (0, 0)),
            pl.BlockSpec((1, D, D), lambda l, j: (l, 0, j)),
            pl.BlockSpec((1, 1, D), lambda l, j: (l, 0, j)),
        ],
        out_specs=pl.BlockSpec((1, MOD_ROWS, D), lambda l, j: (l, 0, j)),
        out_shape=jax.ShapeDtypeStruct((depth, MOD_ROWS, 6 * D), F32),
        compiler_params=_cparams(("arbitrary", "arbitrary")),
        name="ada",
    )(cvec, w_ada, b_ada.reshape(depth, 1, 6 * D))
    return out.reshape(depth, MOD_ROWS, 6, D)


PROJ_CHUNK = 512


def _modulated(x, g_ref, mod_ref, shift_row):
    h = _rms(x) * g_ref[...]
    return h * (1.0 + mod_ref[0, shift_row + 1:shift_row + 2, :]) + mod_ref[0, shift_row:shift_row + 1, :]


def _even_in_kernel(xc_ref, xl_ref, mod_ref, g_ref, w_ref, z_ref, xbc_ref, hy_ref, dt_ref):
    x = jnp.where(pl.program_id(0) < CTX_TILES, xc_ref[...], xl_ref[...])
    hb = _modulated(x, g_ref, mod_ref, 0).astype(BF16)
    col = 0
    for o_ref in (z_ref, xbc_ref, hy_ref, dt_ref):
        width = o_ref.shape[1]
        for c0 in range(0, width, PROJ_CHUNK):
            c1 = min(c0 + PROJ_CHUNK, width)
            o_ref[:, c0:c1] = jnp.dot(hb, w_ref[:, col + c0:col + c1], preferred_element_type=F32)
        col += width


def _pair_specs(width):
    return [pl.BlockSpec((TM, width), lambda i: (jnp.minimum(i, CTX_TILES - 1), 0)),
            pl.BlockSpec((TM, width), lambda i: (jnp.maximum(i - CTX_TILES, 0), 0))]


def even_in_proj(x_pair, mod, g, w_bf):
    widths = (D, SSD_XBC, 3 * D, LANES)
    return pl.pallas_call(
        _even_in_kernel,
        grid=(N_TILES,),
        in_specs=_pair_specs(D) + [
            pl.BlockSpec((1, 6, D), lambda i: (_mod_row(i), 0, 0)),
            pl.BlockSpec((1, D), lambda i: (0, 0)),
            pl.BlockSpec(w_bf.shape, lambda i: (0, 0)),
        ],
        out_specs=[pl.BlockSpec((TM, w), lambda i: (i, 0)) for w in widths],
        out_shape=[jax.ShapeDtypeStruct((T_ALL, w), F32) for w in widths],
        compiler_params=_cparams(("arbitrary",)),
        name="even_in",
    )(*x_pair, mod, g, w_bf)


PAD = SUBLANES


def _ssd_kernel(*refs, L, has_init):
    if has_init:
        (xbc_ref, dt_ref, z_ref, init_ref, cw_ref, cb_ref, dtb_ref, alog_ref, dsk_ref, gs_ref,
         y_ref, xp_s, xc_s, ya_s, st_s) = refs
        fin_ref = None
    else:
        (xbc_ref, dt_ref, z_ref, cw_ref, cb_ref, dtb_ref, alog_ref, dsk_ref, gs_ref,
         y_ref, fin_ref, xp_s, xc_s, ya_s, st_s) = refs
        init_ref = None
    nc = L // CHUNK
    half = SSD_K // 2

    xp_s[0:PAD, :] = jnp.zeros((PAD, SSD_XBC), F32)
    xp_s[PAD + L:2 * PAD + L, :] = jnp.zeros((PAD, SSD_XBC), F32)
    xp_s[PAD:PAD + L, :] = xbc_ref[...]
    for c in range(nc):
        base = PAD + c * CHUNK - half
        for j in range(SSD_XBC // LANES):
            cols = slice(j * LANES, (j + 1) * LANES)
            acc = cb_ref[:, cols] + xp_s[base:base + CHUNK, cols] * cw_ref[0:1, cols]
            for k in range(1, SSD_K):
                acc = acc + xp_s[base + k:base + k + CHUNK, cols] * cw_ref[k:k + 1, cols]
            xc_s[c * CHUNK:(c + 1) * CHUNK, cols] = _silu(acc)

    row = lax.broadcasted_iota(jnp.int32, (CHUNK, CHUNK), 0)
    colm = lax.broadcasted_iota(jnp.int32, (CHUNK, CHUNK), 1)
    lane_lo = colm < SSD_P
    tri_lo = (colm <= row).astype(F32)
    tri_up = (colm >= row).astype(F32)

    for d in range(2):
        causal = (colm <= row) if d == 0 else (colm >= row)
        for j in range(SSD_H * SSD_P // CHUNK):
            if has_init:
                st_s[:, j * CHUNK:(j + 1) * CHUNK] = init_ref[0, d, j * CHUNK:(j + 1) * CHUNK, :].T
            else:
                st_s[:, j * CHUNK:(j + 1) * CHUNK] = jnp.zeros((CHUNK, CHUNK), F32)

        def chunk_body(ci, carry, d=d, causal=causal):
            c = ci if d == 0 else nc - 1 - ci
            r0 = pl.multiple_of(c * CHUNK, CHUNK)
            dt = jax.nn.softplus(dt_ref[pl.ds(r0, CHUNK), :] + dtb_ref[d:d + 1, :])
            a = dt * (-jnp.exp(alog_ref[d:d + 1, :]))
            tri = tri_lo if d == 0 else tri_up
            cs = jnp.dot(tri, a, precision=HIGHEST, preferred_element_type=F32)
            cs_t = jnp.dot(a.T, tri.T, precision=HIGHEST, preferred_element_type=F32)
            edge = cs[CHUNK - 1:CHUNK, :] if d == 0 else cs[0:1, :]
            ecs = jnp.exp(cs)
            dec = jnp.exp(edge - cs)
            cdec = jnp.exp(edge)
            for g in range(SSD_G):
                bm = xc_s[pl.ds(r0, CHUNK), D + g * SSD_N:D + (g + 1) * SSD_N]
                cm = xc_s[pl.ds(r0, CHUNK), D + (SSD_G + g) * SSD_N:D + (SSD_G + g + 1) * SSD_N]
                bm_b, cm_b = bm.astype(BF16), cm.astype(BF16)
                cb = lax.dot_general(cm_b, bm_b, (((1,), (1,)), ((), ())), preferred_element_type=F32)
                bm_t = bm.T.astype(BF16)
                pairs = SSD_H // SSD_G // 2
                for pp in range(pairs):
                    p = g * pairs + pp
                    h0, h1 = 2 * p, 2 * p + 1
                    cols = slice(p * CHUNK, (p + 1) * CHUNK)
                    xs = xc_s[pl.ds(r0, CHUNK), cols]
                    xdt = xs * jnp.where(lane_lo, dt[:, h0:h0 + 1], dt[:, h1:h1 + 1])
                    ms = []
                    for h in (h0, h1):
                        diff = cs[:, h:h + 1] - cs_t[h:h + 1, :]
                        ms.append(cb * jnp.exp(jnp.where(causal, diff, NEG)))
                    mcat = jnp.concatenate(ms, axis=1).astype(BF16)
                    xbd = jnp.concatenate([jnp.where(lane_lo, xdt, 0.0), jnp.where(lane_lo, 0.0, xdt)],
                                          axis=0).astype(BF16)
                    y_diag = jnp.dot(mcat, xbd, preferred_element_type=F32)
                    st = st_s[:, cols]
                    y_off = jnp.dot(cm_b, st.astype(BF16), preferred_element_type=F32)
                    y_off = y_off * jnp.where(lane_lo, ecs[:, h0:h0 + 1], ecs[:, h1:h1 + 1])
                    y = y_diag + y_off
                    if d == 0:
                        ya_s[pl.ds(r0, CHUNK), cols] = y
                    else:
                        ya_s[pl.ds(r0, CHUNK), cols] = ya_s[pl.ds(r0, CHUNK), cols] + y
                    xdd = (xdt * jnp.where(lane_lo, dec[:, h0:h0 + 1], dec[:, h1:h1 + 1])).astype(BF16)
                    snew = jnp.dot(bm_t, xdd, preferred_element_type=F32)
                    st_s[:, cols] = st * jnp.where(lane_lo[0:1, :], cdec[:, h0:h0 + 1], cdec[:, h1:h1 + 1]) + snew
            return carry

        lax.fori_loop(0, nc, chunk_body, 0)
        if fin_ref is not None:
            for j in range(SSD_H * SSD_P // CHUNK):
                fin_ref[0, d, j * CHUNK:(j + 1) * CHUNK, :] = st_s[:, j * CHUNK:(j + 1) * CHUNK].T

    def out_body(c, carry):
        r0 = pl.multiple_of(c * CHUNK, CHUNK)
        y = ya_s[pl.ds(r0, CHUNK), :] + xc_s[pl.ds(r0, CHUNK), 0:D] * dsk_ref[...]
        y = y * _silu(z_ref[pl.ds(r0, CHUNK), :])
        y_ref[pl.ds(r0, CHUNK), :] = (_rms(y) * gs_ref[...]).astype(y_ref.dtype)
        return carry

    lax.fori_loop(0, nc, out_body, 0)


def ssd_mixer(xbc, dtr, z, init, cw, cb, dtb, alog, dsk, gs, *, L, n_seq, row_off):
    blk0 = row_off // L
    has_init = init is not None
    seq = lambda w: pl.BlockSpec((L, w), lambda b: (blk0 + b, 0))
    full = lambda arr: pl.BlockSpec(arr.shape, lambda b: (0,) * arr.ndim)
    in_specs = [seq(SSD_XBC), seq(LANES), seq(D)]
    args = [xbc, dtr, z]
    if has_init:
        in_specs.append(pl.BlockSpec((1, 2, SSD_H * SSD_P, SSD_N), lambda b: (b, 0, 0, 0)))
        args.append(init)
    small = [cw, cb, dtb, alog, dsk, gs]
    in_specs += [full(a) for a in small]
    args += small
    out_specs = [pl.BlockSpec((L, D), lambda b: (b, 0))]
    out_shape = [jax.ShapeDtypeStruct((n_seq * L, D), BF16)]
    if not has_init:
        out_specs.append(pl.BlockSpec((1, 2, SSD_H * SSD_P, SSD_N), lambda b: (b, 0, 0, 0)))
        out_shape.append(jax.ShapeDtypeStruct((n_seq, 2, SSD_H * SSD_P, SSD_N), F32))
    return pl.pallas_call(
        functools.partial(_ssd_kernel, L=L, has_init=has_init),
        grid=(n_seq,),
        in_specs=in_specs,
        out_specs=out_specs,
        out_shape=out_shape,
        scratch_shapes=[
            pltpu.VMEM((L + 2 * PAD, SSD_XBC), F32),
            pltpu.VMEM((L, SSD_XBC), F32),
            pltpu.VMEM((L, D), F32),
            pltpu.VMEM((SSD_N, SSD_H * SSD_P), F32),
        ],
        compiler_params=_cparams(("arbitrary",)),
        name=f"ssd_{L}",
    )(*args)


HY_CB = 256


def filter_dft_matrices(L):
    H = L // 2
    s = np.arange(L, dtype=np.int64)[None, :]
    k = np.arange(H, dtype=np.int64)[:, None]
    ang = lambda kk: ((kk * s) % (2 * L)).astype(np.float64) * (math.pi / L)
    ca, cb = np.cos(ang(k)), np.cos(ang(L - k))
    sa, sb = np.sin(ang(k)), np.sin(ang(L - k))
    cb[0] = np.where(s[0] % 2 == 0, 1.0, -1.0)
    sa[0], sb[0] = 0.0, 0.0
    fm = np.zeros((2 * SUBLANES, L))
    fm[0], fm[1] = np.cos(ang(H))[0], np.sin(ang(H))[0]
    mats = (np.concatenate([ca, cb], axis=0), np.concatenate([sa, sb], axis=0), fm)
    return tuple(jnp.asarray(m.astype(np.float32)).astype(BF16) for m in mats)


def _const_spec(arr):
    return pl.BlockSpec(arr.shape, lambda *_: (0,) * arr.ndim, pipeline_mode=pl.Buffered(1))


def _hy_filter_kernel(feat_ref, w1_ref, b1_ref, w2_ref, b2_ref, fr_ref, w3_ref, dl_ref, fs_ref, fd_ref, fm_ref,
                      h_ref, hm_ref, *, L):
    H = L // 2
    hp = functools.partial(jnp.dot, precision=HIGHEST, preferred_element_type=F32)
    hdn = jnp.sin(fr_ref[0:1, :] * (hp(feat_ref[...], w1_ref[...]) + b1_ref[...]))
    hdn = jnp.sin(fr_ref[1:2, :] * (hp(hdn, w2_ref[...]) + b2_ref[...]))
    rowi = lax.broadcasted_iota(jnp.int32, (L, 1), 0)
    t = rowi.astype(F32) * (1.0 / (L - 1))
    dec = jnp.exp(-t * dl_ref[...])
    first = rowi == 0
    for o in range(2):
        fwd = hp(hdn, w3_ref[2 * o]) * dec
        bwd = jnp.where(first, 0.0, hp(hdn, w3_ref[2 * o + 1]) * dec)
        hs, hd = (fwd + bwd).astype(BF16), (fwd - bwd).astype(BF16)
        ss = jnp.dot(fs_ref[...], hs, preferred_element_type=F32)
        sd = jnp.dot(fd_ref[...], hd, preferred_element_type=F32)
        h_ref[o, 0] = ss[0:H]
        h_ref[o, 1] = sd[0:H]
        h_ref[o, 2] = ss[H:L]
        h_ref[o, 3] = sd[H:L]
        mid_r = jnp.dot(fm_ref[...], hs, preferred_element_type=F32)
        mid_n = jnp.dot(fm_ref[...], hd, preferred_element_type=F32)
        hm_ref[o] = jnp.concatenate([mid_r[0:1], mid_n[1:2], jnp.zeros((SUBLANES - 2, mid_r.shape[1]), F32)], axis=0)


def hyena_filter_spectra(feat, w1, b1, w2, b2, freq, w3r, deltas, *, L):
    full = lambda arr: pl.BlockSpec(arr.shape, lambda j: (0,) * arr.ndim)
    mats = filter_dft_matrices(L)
    return pl.pallas_call(
        functools.partial(_hy_filter_kernel, L=L),
        grid=(D // HY_CB,),
        in_specs=[full(feat), full(w1), full(b1), full(w2), full(b2), full(freq),
                  pl.BlockSpec((4, HY_HID, HY_CB), lambda j: (0, 0, j)),
                  pl.BlockSpec((1, HY_CB), lambda j: (0, j))] + [_const_spec(m) for m in mats],
        out_specs=[pl.BlockSpec((2, 4, L // 2, HY_CB), lambda j: (0, 0, 0, j)),
                   pl.BlockSpec((2, SUBLANES, HY_CB), lambda j: (0, 0, j))],
        out_shape=[jax.ShapeDtypeStruct((2, 4, L // 2, D), F32), jax.ShapeDtypeStruct((2, SUBLANES, D), F32)],
        compiler_params=_cparams(("arbitrary",)),
        name=f"hy_filter_{L}",
    )(feat, w1, b1, w2, b2, freq, w3r, deltas, *mats)


def split_dft_matrices(L):
    H = L // 2
    k = np.arange(H, dtype=np.int64)[:, None]
    m = np.arange(H, dtype=np.int64)[None, :]
    alt = np.where(m % 2 == 0, 1.0, -1.0)
    ang_e = ((k * m) % L).astype(np.float64) * (2 * math.pi / L)
    ang_o = ((k * (2 * m + 1)) % (2 * L)).astype(np.float64) * (math.pi / L)
    ce, se, co, so = np.cos(ang_e), np.sin(ang_e), np.cos(ang_o), np.sin(ang_o)
    se[0], so[0] = alt[0], alt[0]
    w = np.where(k == 0, 1.0, 2.0) / (2 * L)
    fe = np.concatenate([ce, se], axis=0)
    fo = np.concatenate([co, so], axis=0)
    ge = np.concatenate([(ce * w).T, se.T / L], axis=1)
    go = np.concatenate([(co * w).T, so.T / L], axis=1)
    return tuple(jnp.asarray(a.astype(np.float32)).astype(BF16) for a in (fe, fo, ge, go))


def _store_lane_blocks(ref, val):
    for c in range(ref.shape[0]):
        ref[c] = val[:, c * LANES:(c + 1) * LANES]


def _load_parity(ref, parity, n):
    return jnp.concatenate([ref[c, pl.ds(parity, n, stride=2), :] for c in range(ref.shape[0])], axis=1)


def _hyena_kernel(p0_ref, p1_ref, p2_ref, w0_ref, w1_ref, w2_ref, b0_ref, b1_ref, b2_ref, h_ref, hm_ref, hb_ref,
                  fe_ref, fo_ref, ge_ref, go_ref, o_ref, xp_s, u_s, y_s, *, L, cb):
    H = L // 2
    xp_s[0:PAD, :] = jnp.zeros((PAD, cb), F32)
    xp_s[PAD + L:2 * PAD + L, :] = jnp.zeros((PAD, cb), F32)
    first = lax.broadcasted_iota(jnp.int32, (H, 1), 0) == 0

    def conv(p_ref, w_ref, b_ref):
        xp_s[PAD:PAD + L, :] = p_ref[...]
        acc = b_ref[...] + xp_s[PAD - 1:PAD - 1 + L, :] * w_ref[0:1, :]
        for k in range(1, HY_K):
            acc = acc + xp_s[PAD - 1 + k:PAD - 1 + k + L, :] * w_ref[k:k + 1, :]
        return acc

    u = conv(p0_ref, w0_ref, b0_ref)
    for o, (p_ref, w_ref, b_ref) in enumerate(((p1_ref, w1_ref, b1_ref), (p2_ref, w2_ref, b2_ref))):
        _store_lane_blocks(u_s, u)
        se = jnp.dot(fe_ref[...], _load_parity(u_s, 0, H).astype(BF16), preferred_element_type=F32)
        so = jnp.dot(fo_ref[...], _load_parity(u_s, 1, H).astype(BF16), preferred_element_type=F32)
        e, es, od, os_ = se[0:H], se[H:L], so[0:H], so[H:L]
        b0, b1 = e + od, e - od
        b2 = jnp.where(first, es, es + os_)
        b3 = jnp.where(first, os_, os_ - es)
        har, han, hbr, hbn = h_ref[o, 0], h_ref[o, 1], h_ref[o, 2], h_ref[o, 3]
        hmr, hmn = hm_ref[o, 0:1, :], hm_ref[o, 1:2, :]
        y0 = b0 * har - b2 * han
        y1 = b1 * hbr - b3 * hbn
        y2 = b0 * han + b2 * har
        y3 = b1 * hbn + b3 * hbr
        mid_r = b2[0:1] * hmr - b3[0:1] * hmn
        mid_n = b2[0:1] * hmn + b3[0:1] * hmr
        de = jnp.where(first, mid_r, y2 - y3)
        do = jnp.where(first, mid_n, y2 + y3)
        ye = jnp.dot(ge_ref[...], jnp.concatenate([y0 + y1, de], axis=0).astype(BF16), preferred_element_type=F32)
        yo = jnp.dot(go_ref[...], jnp.concatenate([y0 - y1, do], axis=0).astype(BF16), preferred_element_type=F32)
        for c in range(cb // LANES):
            y_s[c, pl.ds(0, H, stride=2), :] = ye[:, c * LANES:(c + 1) * LANES]
            y_s[c, pl.ds(1, H, stride=2), :] = yo[:, c * LANES:(c + 1) * LANES]
        y = jnp.concatenate([y_s[c] for c in range(cb // LANES)], axis=1)
        u = conv(p_ref, w_ref, b_ref) * (y + u * hb_ref[o:o + 1, :])
    o_ref[...] = u.astype(o_ref.dtype)


def hyena_mixer(hy, conv_w, conv_b, h4, hm, hy_bias, *, L, n_seq, row_off):
    blk0 = row_off // L
    cb = min(D, HY_CB * (L_LAT // L))
    nj = D // cb
    H = L // 2
    part = lambda q: pl.BlockSpec((L, cb), lambda j, b: (blk0 + b, q * nj + j))
    wpart = lambda q: pl.BlockSpec((HY_K, cb), lambda j, b: (0, q * nj + j))
    bpart = lambda q: pl.BlockSpec((1, cb), lambda j, b: (0, q * nj + j))
    mats = split_dft_matrices(L)
    return pl.pallas_call(
        functools.partial(_hyena_kernel, L=L, cb=cb),
        grid=(nj, n_seq),
        in_specs=[part(0), part(1), part(2), wpart(0), wpart(1), wpart(2), bpart(0), bpart(1), bpart(2),
                  pl.BlockSpec((2, 4, H, cb), lambda j, b: (0, 0, 0, j)),
                  pl.BlockSpec((2, SUBLANES, cb), lambda j, b: (0, 0, j)),
                  pl.BlockSpec((2, cb), lambda j, b: (0, j))]
                 + [_const_spec(m) for m in mats],
        out_specs=pl.BlockSpec((L, cb), lambda j, b: (b, j)),
        out_shape=jax.ShapeDtypeStruct((n_seq * L, D), BF16),
        scratch_shapes=[pltpu.VMEM((L + 2 * PAD, cb), F32), pltpu.VMEM((cb // LANES, L, LANES), F32),
                        pltpu.VMEM((cb // LANES, L, LANES), F32)],
        compiler_params=_cparams(("arbitrary", "arbitrary")),
        name=f"hyena_{L}",
    )(hy, hy, hy, conv_w, conv_w, conv_w, conv_b, conv_b, conv_b, h4, hm, hy_bias, *mats)


ROUTER_LANES = LANES
BIG_LANE = 1e9


ROW_GROUP = D // LANES


def _store_row_groups(ref, val):
    n = val.shape[0]
    for s in range(ROW_GROUP):
        ref[pl.ds(s, n, stride=ROW_GROUP), :] = val[:, s * LANES:(s + 1) * LANES]


def _load_row_groups(ref, n, s):
    return ref[pl.ds(s, n, stride=ROW_GROUP), :]


def _first_max_lane(v, lanef):
    m = jnp.max(v, axis=-1, keepdims=True)
    return m, jnp.min(jnp.where(v == m, lanef, BIG_LANE), axis=-1, keepdims=True)


def _out_router_kernel(*refs, n_in, x_is_pair):
    a_refs = refs[:2 * n_in]
    refs = refs[2 * n_in:]
    is_ctx = pl.program_id(0) < CTX_TILES
    if x_is_pair:
        x = jnp.where(is_ctx, refs[0][...], refs[1][...])
        refs = refs[2:]
    else:
        x = refs[0][...]
        refs = refs[1:]
    w_ref, mod_ref, gf_ref, wr_ref, br_ref, xo_ref, h2_ref, ids_ref, wts_ref, cnt_ref = refs
    acc, k0 = None, 0
    for ac_ref, al_ref in zip(a_refs[0::2], a_refs[1::2]):
        kk = ac_ref.shape[1]
        a = jnp.where(is_ctx, ac_ref[...], al_ref[...])
        part = jnp.dot(a, w_ref[k0:k0 + kk, :], preferred_element_type=F32)
        acc = part if acc is None else acc + part
        k0 += kk
    xn = x + mod_ref[0, 2:3, :] * acc
    xo_ref[...] = xn
    h2 = _modulated(xn, gf_ref, mod_ref, 3)
    h2_ref[...] = h2

    h_hi = h2.astype(BF16)
    h_lo = (h2 - h_hi.astype(F32)).astype(BF16)
    logits = (jnp.dot(h_hi, wr_ref[0], preferred_element_type=F32) + jnp.dot(h_lo, wr_ref[0], preferred_element_type=F32)
              + jnp.dot(h_hi, wr_ref[1], preferred_element_type=F32) + br_ref[...])
    lanef = lax.broadcasted_iota(jnp.int32, logits.shape, 1).astype(F32)
    gl = jnp.where(lanef < MOE_G, logits, NEG)
    gm, gi = _first_max_lane(gl, lanef)
    g_w = 1.0 / jnp.sum(jnp.exp(gl - gm), axis=-1, keepdims=True)
    lo = MOE_G + MOE_PG * gi
    el = jnp.where((lanef >= lo) & (lanef < lo + MOE_PG), logits, NEG)
    m1, e1 = _first_max_lane(el, lanef)
    m2, e2 = _first_max_lane(jnp.where(lanef == e1, NEG, el), lanef)
    p2 = jnp.exp(m2 - m1)
    w1 = g_w / (1.0 + p2)
    ids_ref[...] = jnp.where(lanef == 0, e1 - MOE_G, jnp.where(lanef == 1, e2 - MOE_G, 0.0)).astype(jnp.int32)
    wts_ref[...] = jnp.where(lanef == 0, w1, jnp.where(lanef == 1, w1 * p2, 0.0))
    chosen = ((lanef == e1 - MOE_G) | (lanef == e2 - MOE_G)).astype(F32)
    cnt_ref[0] = jnp.sum(chosen, axis=0, keepdims=True).astype(jnp.int32)


def out_proj_router(acts, w_bf, x, mod, gf, wr, br):
    tile = lambda w: pl.BlockSpec((TM, w), lambda i: (i, 0))
    full = lambda arr: pl.BlockSpec(arr.shape, lambda i: (0,) * arr.ndim)
    x_is_pair = isinstance(x, tuple)
    xs = x if x_is_pair else (x,)
    return pl.pallas_call(
        functools.partial(_out_router_kernel, n_in=len(acts), x_is_pair=x_is_pair),
        grid=(N_TILES,),
        in_specs=[s for a in acts for s in _pair_specs(a[0].shape[1])]
                 + (_pair_specs(D) if x_is_pair else [tile(D)])
                 + [full(w_bf), pl.BlockSpec((1, 6, D), lambda i: (_mod_row(i), 0, 0)), full(gf), full(wr), full(br)],
        out_specs=[tile(D), tile(D), tile(ROUTER_LANES), tile(ROUTER_LANES),
                   pl.BlockSpec((1, 1, ROUTER_LANES), lambda i: (i, 0, 0))],
        out_shape=[jax.ShapeDtypeStruct((T_ALL, D), F32), jax.ShapeDtypeStruct((T_ALL, D), F32),
                   jax.ShapeDtypeStruct((T_ALL, ROUTER_LANES), jnp.int32),
                   jax.ShapeDtypeStruct((T_ALL, ROUTER_LANES), F32),
                   jax.ShapeDtypeStruct((N_TILES, 1, ROUTER_LANES), jnp.int32)],
        compiler_params=_cparams(("arbitrary",)),
        name="out_router",
    )(*[part for a in acts for part in a], *xs, w_bf, mod, gf, wr, br)


N_ASSIGN = 2 * T_ALL
MOE_TILES = N_ASSIGN // TM + MOE_E
N_SLOTS = MOE_TILES * TM


def route_tables(cnt3):
    cnt = cnt3[:, 0, :MOE_E]
    total = jnp.sum(cnt, axis=0)
    padded = (total + TM - 1) // TM * TM
    ends = jnp.cumsum(padded)
    gdst = (ends - padded)[None, :] + jnp.cumsum(cnt, axis=0) - cnt
    loc = jnp.cumsum(cnt, axis=1) - cnt
    starts = jnp.arange(MOE_TILES, dtype=jnp.int32) * TM
    tile_expert = jnp.minimum(jnp.sum((ends[None, :] <= starts[:, None]).astype(jnp.int32), axis=1), MOE_E - 1)
    n_used = (ends[-1] // TM).astype(jnp.int32).reshape(1)
    return cnt, loc, gdst, ends, tile_expert, n_used


RUN_BITS = (2 * TM).bit_length()
RUN_SMALL_BITS = 6


def _dispatch_kernel(cnt_s, loc_s, gdst_s, ends_s, h_ref, ids_ref, gcol_ref, xs_ref, dest_ref, srt, zbuf, sem, zsem):
    i = pl.program_id(0)
    slot = i % 2
    n_rows = 2 * TM

    @pl.when(i == 0)
    def _():
        zbuf[...] = jnp.zeros(zbuf.shape, zbuf.dtype)
        n_used = ends_s[MOE_E - 1] // TM
        for phase in ("start", "wait"):
            def tail(t, c, phase=phase):
                dst = pl.multiple_of(t * (TM * ROW_GROUP), TM * ROW_GROUP)
                cp = pltpu.make_async_copy(zbuf, xs_ref.at[pl.ds(dst, TM * ROW_GROUP), :], zsem)
                cp.start() if phase == "start" else cp.wait()
                return c

            lax.fori_loop(n_used, MOE_TILES, tail, 0)
            for e in range(MOE_E):
                end = ends_s[e]
                prev = ends_s[e - 1] if e > 0 else 0

                @pl.when(end > prev)
                def _(end=end, phase=phase):
                    dst = pl.multiple_of((end - TM) * ROW_GROUP, TM * ROW_GROUP)
                    cp = pltpu.make_async_copy(zbuf, xs_ref.at[pl.ds(dst, TM * ROW_GROUP), :], zsem)
                    cp.start() if phase == "start" else cp.wait()

    idt = ids_ref[...].astype(F32).T
    sub = lax.broadcasted_iota(jnp.int32, (LANES, TM), 0).astype(F32)
    m0 = (sub == idt[0:1, :]).astype(F32)
    m1 = (sub == idt[1:2, :]).astype(F32)
    mt = (m0 + m1).astype(BF16)
    tr = lax.broadcasted_iota(jnp.int32, (TM, TM), 0)
    tc = lax.broadcasted_iota(jnp.int32, (TM, TM), 1)
    earlier = jnp.dot(mt, (tr < tc).astype(BF16), preferred_element_type=F32)
    er = lax.broadcasted_iota(jnp.int32, (LANES, LANES), 0)
    ec = lax.broadcasted_iota(jnp.int32, (LANES, LANES), 1)
    below = jnp.dot((ec < er).astype(BF16), mt, preferred_element_type=F32)
    local = jnp.sum(below, axis=1, keepdims=True) + earlier
    glob = gcol_ref[0] + earlier
    pos0 = jnp.sum(m0 * local, axis=0, keepdims=True)
    pos1 = jnp.sum(m1 * local, axis=0, keepdims=True)
    dest_ref[0] = jnp.concatenate([jnp.sum(m0 * glob, axis=0, keepdims=True),
                                   jnp.sum(m1 * glob, axis=0, keepdims=True)], axis=0).astype(jnp.int32)

    srow = lax.broadcasted_iota(jnp.int32, (n_rows, TM), 0).astype(F32)
    perm = jnp.where((srow == pos0) | (srow == pos1), 1.0, 0.0).astype(BF16)
    _store_row_groups(srt.at[slot], jnp.dot(perm, h_ref[...].astype(BF16), preferred_element_type=F32))

    def run_pieces(n, s0, d0, bits):
        for b in bits:
            size = 1 << b
            off = (n >> (b + 1)) << (b + 1)

            @pl.when(((n >> b) & 1) == 1)
            def _(size=size, off=off):
                src = pl.multiple_of((s0 + off) * ROW_GROUP, ROW_GROUP)
                dst = pl.multiple_of((d0 + off) * ROW_GROUP, ROW_GROUP)
                pltpu.make_async_copy(srt.at[slot, pl.ds(src, size * ROW_GROUP), :],
                                      xs_ref.at[pl.ds(dst, size * ROW_GROUP), :], sem.at[slot]).start()

    for e in range(MOE_E):
        n, s0, d0 = cnt_s[0, 0, e], loc_s[0, 0, e], gdst_s[0, 0, e]

        @pl.when(n >= (1 << RUN_SMALL_BITS))
        def _(n=n, s0=s0, d0=d0):
            run_pieces(n, s0, d0, reversed(range(RUN_SMALL_BITS, RUN_BITS)))

        run_pieces(n, s0, d0, reversed(range(RUN_SMALL_BITS)))

    def wait(s):
        pltpu.make_async_copy(srt.at[s], xs_ref.at[pl.ds(0, n_rows * ROW_GROUP), :], sem.at[s]).wait()

    @pl.when(i > 0)
    def _():
        wait(1 - slot)

    @pl.when(i == N_TILES - 1)
    def _():
        wait(slot)


def dispatch_rows(h2, ids, cnt, loc, gdst, ends):
    tab = lambda: pl.BlockSpec((1, 1, MOE_E), lambda i: (i, 0, 0), memory_space=pltpu.SMEM)
    gcol = jnp.pad(gdst.astype(F32), ((0, 0), (0, LANES - MOE_E)))[:, :, None]
    return pl.pallas_call(
        _dispatch_kernel,
        grid=(N_TILES,),
        in_specs=[tab(), tab(), tab(), pl.BlockSpec(memory_space=pltpu.SMEM),
                  pl.BlockSpec((TM, D), lambda i: (i, 0)), pl.BlockSpec((TM, ROUTER_LANES), lambda i: (i, 0)),
                  pl.BlockSpec((1, LANES, 1), lambda i: (i, 0, 0))],
        out_specs=[pl.BlockSpec(memory_space=pl.ANY), pl.BlockSpec((1, 2, TM), lambda i: (i, 0, 0))],
        out_shape=[jax.ShapeDtypeStruct((N_SLOTS * ROW_GROUP, LANES), F32),
                   jax.ShapeDtypeStruct((N_TILES, 2, TM), jnp.int32)],
        scratch_shapes=[pltpu.VMEM((2, 2 * TM * ROW_GROUP, LANES), F32), pltpu.VMEM((TM * ROW_GROUP, LANES), F32),
                        pltpu.SemaphoreType.DMA((2,)), pltpu.SemaphoreType.DMA(())],
        compiler_params=_cparams(("arbitrary",)),
        name="moe_dispatch",
    )(cnt.reshape(N_TILES, 1, MOE_E), loc.reshape(N_TILES, 1, MOE_E), gdst.reshape(N_TILES, 1, MOE_E),
      ends, h2, ids, gcol)


DMA_UNROLL = 8


def _start_group_gather(src_hbm, idx_ref, n, dst_ref, sem):
    def body(j, c):
        for u in range(DMA_UNROLL):
            r = j * DMA_UNROLL + u
            src = pl.multiple_of(idx_ref[0, 0, r] * ROW_GROUP, ROW_GROUP)
            dst = pl.multiple_of(r * ROW_GROUP, ROW_GROUP)
            pltpu.make_async_copy(src_hbm.at[pl.ds(src, ROW_GROUP), :], dst_ref.at[pl.ds(dst, ROW_GROUP), :],
                                  sem).start(priority=u % 2)
        return c

    lax.fori_loop(0, n // DMA_UNROLL, body, 0)


def _wait_group_gather(src_hbm, dst_ref, sem):
    pltpu.make_async_copy(src_hbm.at[pl.ds(0, dst_ref.shape[0]), :], dst_ref, sem).wait()


def _experts_kernel(te_ref, nu_ref, x_ref, wg_ref, wu_ref, wd_ref, o_ref, xcat):
    i = pl.program_id(0)

    @pl.when(i < nu_ref[0])
    def _():
        for s in range(ROW_GROUP):
            xcat[:, s * LANES:(s + 1) * LANES] = _load_row_groups(x_ref, TM, s).astype(BF16)
        x = xcat[...]
        g = jnp.dot(x, wg_ref[0, 0].astype(BF16), preferred_element_type=F32)
        u = jnp.dot(x, wu_ref[0, 0].astype(BF16), preferred_element_type=F32)
        hid = (_silu(g) * u).astype(BF16)
        _store_row_groups(o_ref, jnp.dot(hid, wd_ref[0, 0].astype(BF16), preferred_element_type=F32))

    @pl.when(i >= nu_ref[0])
    def _():
        o_ref[...] = jnp.zeros(o_ref.shape, o_ref.dtype)


def grouped_experts(xs, w_gate, w_up, w_down, tile_expert, n_used, layer):
    wspec = lambda a, b: pl.BlockSpec((1, 1, a, b), lambda i, te, nu: (layer, te[i], 0, 0))
    return pl.pallas_call(
        _experts_kernel,
        grid_spec=pltpu.PrefetchScalarGridSpec(
            num_scalar_prefetch=2,
            grid=(MOE_TILES,),
            in_specs=[pl.BlockSpec((TM * ROW_GROUP, LANES), lambda i, te, nu: (jnp.minimum(i, nu[0] - 1), 0)),
                      wspec(D, MOE_F), wspec(D, MOE_F), wspec(MOE_F, D)],
            out_specs=pl.BlockSpec((TM * ROW_GROUP, LANES), lambda i, te, nu: (i, 0)),
            scratch_shapes=[pltpu.VMEM((TM, D), BF16)],
        ),
        out_shape=jax.ShapeDtypeStruct((N_SLOTS * ROW_GROUP, LANES), F32),
        compiler_params=_cparams(("arbitrary",)),
        name="moe_experts",
    )(tile_expert, n_used, xs, w_gate, w_up, w_down)


def _combine_kernel(cur_ref, nxt_ref, ys_hbm, x_ref, wts_ref, mod_ref, gfin_ref, o_ref, buf, sem, *, final):
    i = pl.program_id(0)
    slot = i % 2

    @pl.when(i == 0)
    def _():
        _start_group_gather(ys_hbm, cur_ref, 2 * TM, buf.at[0], sem.at[0])

    @pl.when(i + 1 < N_TILES)
    def _():
        _start_group_gather(ys_hbm, nxt_ref, 2 * TM, buf.at[1 - slot], sem.at[1 - slot])

    _wait_group_gather(ys_hbm, buf.at[slot], sem.at[slot])
    w0, w1 = wts_ref[:, 0:1], wts_ref[:, 1:2]
    for s in range(ROW_GROUP):
        cols = slice(s * LANES, (s + 1) * LANES)
        y0 = buf[slot, pl.ds(s, TM, stride=ROW_GROUP), :]
        y1 = buf[slot, pl.ds(TM * ROW_GROUP + s, TM, stride=ROW_GROUP), :]
        o_ref[:, cols] = x_ref[:, cols] + mod_ref[0, 5:6, cols] * (w0 * y0 + w1 * y1)
    if final:
        o_ref[...] = _rms(o_ref[...]) * gfin_ref[...]


def moe_combine(ys, dest, x, wts, mod, gfin, *, final):
    tile = lambda w: pl.BlockSpec((TM, w), lambda i: (i, 0))
    idx = lambda f: pl.BlockSpec((1, 1, 2 * TM), lambda i: (f(i), 0, 0), memory_space=pltpu.SMEM)
    dest3 = dest.reshape(N_TILES, 1, 2 * TM)
    return pl.pallas_call(
        functools.partial(_combine_kernel, final=final),
        grid=(N_TILES,),
        in_specs=[idx(lambda i: i), idx(lambda i: jnp.minimum(i + 1, N_TILES - 1)),
                  pl.BlockSpec(memory_space=pl.ANY), tile(D), tile(ROUTER_LANES),
                  pl.BlockSpec((1, 6, D), lambda i: (_mod_row(i), 0, 0)),
                  pl.BlockSpec((1, D), lambda i: (0, 0))],
        out_specs=tile(D),
        out_shape=jax.ShapeDtypeStruct((T_ALL, D), F32),
        scratch_shapes=[pltpu.VMEM((2, 2 * TM * ROW_GROUP, LANES), F32), pltpu.SemaphoreType.DMA((2,))],
        compiler_params=_cparams(("arbitrary",)),
        name="moe_combine",
    )(dest3, dest3, ys, x, wts, mod, gfin)


ODD_COLS = 2048
ROPE_Q = MLA_H * MLA_ROPE
ROPE_SHIFT = ROPE_F


def rope_tables():
    t = np.arange(L_LAT)
    pos = np.stack([t // GRID_W, t % GRID_W], axis=1).astype(np.float64)
    inv = 10000.0 ** (-np.arange(ROPE_F, dtype=np.float64) / ROPE_F)
    lane = np.arange(ROPE_Q) % MLA_ROPE
    axis = lane // (2 * ROPE_F)
    first = (lane % (2 * ROPE_F)) < ROPE_F
    ang = pos[:, axis] * inv[lane % ROPE_F][None, :]
    cos, sin = np.cos(ang), np.sin(ang)
    tabs = [cos, np.where(first[None, :], -sin, 0.0), np.where(first[None, :], 0.0, sin)]
    ident = [np.ones((1, TM, ROPE_Q)), np.zeros((1, TM, ROPE_Q)), np.zeros((1, TM, ROPE_Q))]
    return [jnp.asarray(np.concatenate([i, tb.reshape(LAT_TILES_PER_SEQ, TM, ROPE_Q)], axis=0).astype(np.float32))
            for i, tb in zip(ident, tabs)]


def _rope(x, c, a, b):
    n = x.shape[1]
    return x * c[:, :n] + pltpu.roll(x, n - ROPE_SHIFT, 1) * a[:, :n] + pltpu.roll(x, ROPE_SHIFT, 1) * b[:, :n]


def _odd_in_kernel(x_ref, mod_ref, g_ref, w_ref, gq_ref, wuq_ref, gkv_ref, wukv_ref, rc_ref, ra_ref, rb_ref,
                   qkv_ref, qm_ref, ckv_ref, kvu_ref, kr_ref, knew_ref, vnew_ref):
    hb = _modulated(x_ref[...], g_ref, mod_ref, 0).astype(BF16)
    for c0 in range(0, 3 * NA_W, PROJ_CHUNK):
        qkv_ref[:, c0:c0 + PROJ_CHUNK] = jnp.dot(hb, w_ref[:, c0:c0 + PROJ_CHUNK], preferred_element_type=F32)

    @pl.when(pl.program_id(0) < CTX_TILES)
    def _():
        for h in range(NA_H):
            knew_ref[0, h] = qkv_ref[:, NA_W + h * NA_D:NA_W + (h + 1) * NA_D]
            vnew_ref[0, h] = qkv_ref[:, 2 * NA_W + h * NA_D:2 * NA_W + (h + 1) * NA_D]

    rest = jnp.dot(hb, w_ref[:, 3 * NA_W:ODD_COLS], preferred_element_type=F32)
    rc, ra, rb = rc_ref[0], ra_ref[0], rb_ref[0]
    qd = (_rms(rest[:, 0:MLA_QR]) * gq_ref[...]).astype(BF16)
    qm = jnp.dot(qd, wuq_ref[...], preferred_element_type=F32)
    qm_ref[:, 0:MLA_H * MLA_NOPE] = qm[:, 0:MLA_H * MLA_NOPE]
    qm_ref[:, MLA_H * MLA_NOPE:] = _rope(qm[:, MLA_H * MLA_NOPE:], rc, ra, rb)
    ckv = _rms(rest[:, MLA_QR:MLA_QR + MLA_KVR]) * gkv_ref[...]
    ckv_ref[...] = ckv
    kvu_ref[...] = jnp.dot(ckv.astype(BF16), wukv_ref[...], preferred_element_type=F32)
    kr_ref[...] = _rope(rest[:, MLA_QR + MLA_KVR:], rc, ra, rb)


def odd_in_proj(x, mod, g, w_bf, gq, wuq_bf, gkv, wukv_bf, tabs):
    tile = lambda w: pl.BlockSpec((TM, w), lambda i: (i, 0))
    full = lambda arr: pl.BlockSpec(arr.shape, lambda i: (0,) * arr.ndim)
    tab = pl.BlockSpec((1, TM, ROPE_Q),
                       lambda i: (jnp.where(i < CTX_TILES, 0, 1 + (i - CTX_TILES) % LAT_TILES_PER_SEQ), 0, 0))
    widths = (3 * NA_W, MLA_H * MLA_QK, MLA_KVR, MLA_H * (MLA_NOPE + MLA_V), LANES)
    cache = pl.BlockSpec((1, NA_H, L_CTX, NA_D), lambda i: (jnp.minimum(i, CTX_TILES - 1), 0, 0, 0))
    cache_shape = jax.ShapeDtypeStruct((N_CTX, NA_H, L_CTX, NA_D), F32)
    return pl.pallas_call(
        _odd_in_kernel,
        grid=(N_TILES,),
        in_specs=[tile(D), pl.BlockSpec((1, 6, D), lambda i: (_mod_row(i), 0, 0)), full(g), full(w_bf),
                  full(gq), full(wuq_bf), full(gkv), full(wukv_bf), tab, tab, tab],
        out_specs=[tile(w) for w in widths] + [cache, cache],
        out_shape=[jax.ShapeDtypeStruct((T_ALL, w), F32) for w in widths] + [cache_shape, cache_shape],
        compiler_params=_cparams(("arbitrary",)),
        name="odd_in",
    )(x, mod, g, w_bf, gq, wuq_bf, gkv, wukv_bf, *tabs)


LOG2E = math.log2(math.e)
NA_QSCALE = NA_D ** -0.5 * LOG2E
MLA_QSCALE = MLA_QK ** -0.5 * LOG2E
NT = (((1,), (1,)), ((), ()))


def _softmax_pv(scores, values):
    m = functools.reduce(jnp.maximum, [jnp.max(s, axis=-1, keepdims=True) for s in scores])
    ps = [jnp.exp2(s - m) for s in scores]
    den = functools.reduce(jnp.add, [jnp.sum(p, axis=-1, keepdims=True) for p in ps])
    acc = functools.reduce(jnp.add, [jnp.dot(p.astype(BF16), v, preferred_element_type=F32) for p, v in zip(ps, values)])
    return acc / den


def _pair(ref_or_val, p, base=0):
    return ref_or_val[:, base + p * LANES:base + (p + 1) * LANES]


def _low_half():
    return lax.broadcasted_iota(jnp.int32, (1, LANES), 1) < NA_D


def _rope_key_forms(kr):
    return kr.astype(BF16), pltpu.roll(kr, LANES // 2, 1).astype(BF16)


def _mla_pair(qm, p, sources, lo):
    outs = []
    for e in range(2):
        h = 2 * p + e
        qn = qm[:, h * MLA_NOPE:(h + 1) * MLA_NOPE] * MLA_QSCALE
        qr = qm[:, MLA_H * MLA_NOPE + h * MLA_ROPE:MLA_H * MLA_NOPE + (h + 1) * MLA_ROPE] * MLA_QSCALE
        z = jnp.zeros((qn.shape[0], LANES - MLA_QK), F32)
        qcat = jnp.concatenate([qn, qr, z] if e == 0 else [qr, z, qn], axis=1).astype(BF16)
        scores = []
        for kb, kr_lo, kr_hi, _ in sources:
            kcat = jnp.where(lo, kb, kr_hi) if e == 0 else jnp.where(lo, kr_lo, kb)
            scores.append(lax.dot_general(qcat, kcat, NT, preferred_element_type=F32))
        outs.append(_softmax_pv(scores, [src[3] for src in sources]))
    return jnp.where(lo, outs[0], outs[1])


def _attn_ctx_kernel(qkv_ref, qm_ref, kvu_ref, kr_ref, ona_ref, omla_ref):
    lo = _low_half()
    for p in range(NA_H // 2):
        qb = _pair(qkv_ref, p) * NA_QSCALE
        kb = _pair(qkv_ref, p, NA_W).astype(BF16)
        vb = _pair(qkv_ref, p, 2 * NA_W).astype(BF16)
        outs = []
        for e in range(2):
            q = jnp.where(lo if e == 0 else jnp.logical_not(lo), qb, 0.0).astype(BF16)
            outs.append(_softmax_pv([lax.dot_general(q, kb, NT, preferred_element_type=F32)], [vb]))
        ona_ref[:, p * LANES:(p + 1) * LANES] = jnp.where(lo, outs[0], outs[1]).astype(ona_ref.dtype)
    kr_lo, kr_hi = _rope_key_forms(kr_ref[...])
    for p in range(MLA_H // 2):
        src = (_pair(kvu_ref, p).astype(BF16), kr_lo, kr_hi, _pair(kvu_ref, p, MLA_H * MLA_NOPE).astype(BF16))
        omla_ref[:, p * LANES:(p + 1) * LANES] = _mla_pair(qm_ref, p, [src], lo).astype(omla_ref.dtype)


def attn_context(qkv, qm, kvu, kr):
    seq = lambda w: pl.BlockSpec((L_CTX, w), lambda b: (b, 0))
    return pl.pallas_call(
        _attn_ctx_kernel,
        grid=(N_CTX,),
        in_specs=[seq(3 * NA_W), seq(MLA_H * MLA_QK), seq(MLA_H * (MLA_NOPE + MLA_V)), seq(LANES)],
        out_specs=[seq(NA_W), seq(MLA_H * MLA_V)],
        out_shape=[jax.ShapeDtypeStruct((T_CTX, NA_W), BF16), jax.ShapeDtypeStruct((T_CTX, MLA_H * MLA_V), BF16)],
        compiler_params=_cparams(("arbitrary",)),
        name="attn_ctx",
    )(qkv, qm, kvu, kr)


N_DR = 2 * NA_WIN_R - 1
GRID_ROWS = L_LAT // GRID_W


def _na_bias_kernel(t_ref, o_ref):
    neg = jnp.full((GRID_W, GRID_W), NEG, F32)
    for r in range(GRID_ROWS):
        r0 = min(max(r - NA_WIN_R // 2, 0), GRID_ROWS - NA_WIN_R)
        for kr in range(GRID_ROWS):
            in_window = r0 <= kr < r0 + NA_WIN_R
            blk = t_ref[0, kr - r + NA_WIN_R - 1] if in_window else neg
            o_ref[0, r * GRID_W:(r + 1) * GRID_W, kr * GRID_W:(kr + 1) * GRID_W] = blk


def neighbourhood_bias(rel_bias):
    c = np.arange(GRID_W)
    c0 = np.clip(c - NA_WIN_C // 2, 0, GRID_W - NA_WIN_C)
    col_ok = (c[None, :] >= c0[:, None]) & (c[None, :] < c0[:, None] + NA_WIN_C)
    dc = np.clip(c[None, :] - c[:, None], -(NA_WIN_C - 1), NA_WIN_C - 1) + NA_WIN_C - 1
    sel_c = (dc[:, :, None] == np.arange(2 * NA_WIN_C - 1)).astype(np.float32)
    t = jnp.einsum("hdj,qcj->hdqc", rel_bias.astype(F32), jnp.asarray(sel_c), precision=HIGHEST)
    t = jnp.where(jnp.asarray(col_ok)[None, None], t * LOG2E, NEG)
    return pl.pallas_call(
        _na_bias_kernel,
        grid=(NA_H,),
        in_specs=[pl.BlockSpec((1, N_DR, GRID_W, GRID_W), lambda h: (h, 0, 0, 0))],
        out_specs=pl.BlockSpec((1, L_LAT, L_LAT), lambda h: (h, 0, 0)),
        out_shape=jax.ShapeDtypeStruct((NA_H, L_LAT, L_LAT), F32),
        compiler_params=_cparams(("arbitrary",)),
        name="na_bias",
    )(t)


def _na_lat_kernel(q_ref, k_ref, v_ref, kc_ref, vc_ref, b_ref, o_ref):
    lo = _low_half()
    for p in range(NA_H // 2):
        qb = _pair(q_ref, p) * NA_QSCALE
        kb = _pair(k_ref, p).astype(BF16)
        vb = _pair(v_ref, p).astype(BF16)
        outs = []
        for e in range(2):
            h = 2 * p + e
            half = slice(e * NA_D, (e + 1) * NA_D)
            q = jnp.where(lo if e == 0 else jnp.logical_not(lo), qb, 0.0).astype(BF16)
            s1 = lax.dot_general(q, kb, NT, preferred_element_type=F32) + b_ref[h]
            s2 = lax.dot_general(qb[:, half].astype(BF16), kc_ref[0, 0, h].astype(BF16), NT, preferred_element_type=F32)
            m = jnp.maximum(jnp.max(s1, axis=-1, keepdims=True), jnp.max(s2, axis=-1, keepdims=True))
            p1, p2 = jnp.exp2(s1 - m), jnp.exp2(s2 - m)
            den = jnp.sum(p1, axis=-1, keepdims=True) + jnp.sum(p2, axis=-1, keepdims=True)
            a1 = jnp.dot(p1.astype(BF16), vb, preferred_element_type=F32)
            a2 = jnp.dot(p2.astype(BF16), vc_ref[0, 0, h].astype(BF16), preferred_element_type=F32)
            outs.append((a1[:, half] + a2) / den)
        o_ref[:, p * LANES:(p + 1) * LANES] = jnp.concatenate(outs, axis=1).astype(o_ref.dtype)


def attn_neighbourhood_latent(qkv, cache_k, cache_v, bias):
    nq = L_LAT // TM
    t0 = T_CTX // TM
    s0 = T_CTX // L_LAT
    cache = pl.BlockSpec((1, 1, NA_H, PAST, NA_D), lambda qt, b: (b, 0, 0, 0, 0))
    return pl.pallas_call(
        _na_lat_kernel,
        grid=(nq, N_LAT),
        in_specs=[pl.BlockSpec((TM, NA_W), lambda qt, b: (t0 + b * nq + qt, 0)),
                  pl.BlockSpec((L_LAT, NA_W), lambda qt, b: (s0 + b, 1)),
                  pl.BlockSpec((L_LAT, NA_W), lambda qt, b: (s0 + b, 2)),
                  cache, cache,
                  pl.BlockSpec((NA_H, TM, L_LAT), lambda qt, b: (0, qt, 0))],
        out_specs=pl.BlockSpec((TM, NA_W), lambda qt, b: (b * nq + qt, 0)),
        out_shape=jax.ShapeDtypeStruct((T_LAT, NA_W), BF16),
        compiler_params=_cparams(("arbitrary", "arbitrary")),
        name="attn_na_lat",
    )(qkv, qkv, qkv, cache_k, cache_v, bias)


def _mla_lat_kernel(qm_ref, kvu_ref, kr_ref, ckv_ref, krc_ref, wukv_ref, o_ref):
    lo = _low_half()
    kvc = jnp.dot(ckv_ref[0, 0].astype(BF16), wukv_ref[...], preferred_element_type=F32)
    kr_lo, kr_hi = _rope_key_forms(kr_ref[...])
    krc = jnp.concatenate([krc_ref[0, 0], jnp.zeros((PAST, LANES - MLA_ROPE), F32)], axis=1)
    krc_lo, krc_hi = _rope_key_forms(krc)
    vbase = MLA_H * MLA_NOPE
    for p in range(MLA_H // 2):
        lat = (_pair(kvu_ref, p).astype(BF16), kr_lo, kr_hi, _pair(kvu_ref, p, vbase).astype(BF16))
        ctx = (_pair(kvc, p).astype(BF16), krc_lo, krc_hi, _pair(kvc, p, vbase).astype(BF16))
        o_ref[:, p * LANES:(p + 1) * LANES] = _mla_pair(qm_ref, p, [lat, ctx], lo).astype(o_ref.dtype)


def attn_mla_latent(qm, kvu, kr, cache_ckv, cache_krope, wukv_bf):
    nq = L_LAT // TM
    t0 = T_CTX // TM
    s0 = T_CTX // L_LAT
    return pl.pallas_call(
        _mla_lat_kernel,
        grid=(nq, N_LAT),
        in_specs=[pl.BlockSpec((TM, MLA_H * MLA_QK), lambda qt, b: (t0 + b * nq + qt, 0)),
                  pl.BlockSpec((L_LAT, MLA_H * (MLA_NOPE + MLA_V)), lambda qt, b: (s0 + b, 0)),
                  pl.BlockSpec((L_LAT, LANES), lambda qt, b: (s0 + b, 0)),
                  pl.BlockSpec((1, 1, PAST, MLA_KVR), lambda qt, b: (b, 0, 0, 0)),
                  pl.BlockSpec((1, 1, PAST, MLA_ROPE), lambda qt, b: (b, 0, 0, 0)),
                  pl.BlockSpec(wukv_bf.shape, lambda qt, b: (0, 0))],
        out_specs=pl.BlockSpec((TM, MLA_H * MLA_V), lambda qt, b: (b * nq + qt, 0)),
        out_shape=jax.ShapeDtypeStruct((T_LAT, MLA_H * MLA_V), BF16),
        compiler_params=_cparams(("arbitrary", "arbitrary")),
        name="attn_mla_lat",
    )(qm, kvu, kr, cache_ckv, cache_krope, wukv_bf)


def moe_block(h2, ids, wts, cnt3, x, mod, gfin, w_gate, w_up, w_down, layer, *, final):
    cnt, loc, gdst, ends, tile_expert, n_used = route_tables(cnt3)
    xs, dest = dispatch_rows(h2, ids, cnt, loc, gdst, ends)
    ys = grouped_experts(xs, w_gate, w_up, w_down, tile_expert, n_used, layer)
    return moe_combine(ys, dest, x, wts, mod, gfin, final=final)


def _pad_lanes(a):
    return jnp.pad(a, ((0, 0), (0, LANES - a.shape[1])))


def _hyena_features(L):
    t = np.linspace(0.0, 1.0, L)[:, None]
    w = 2.0 * math.pi * np.arange(L) / L
    bands = np.linspace(1e-4, HY_BANDS - 1, HY_BANDS)
    ang = w[:, None] * bands[None]
    feat = np.concatenate([t, np.cos(ang), -np.sin(ang)], axis=-1)
    return jnp.asarray(np.pad(feat, ((0, 0), (0, LANES - HY_FEAT))).astype(np.float32))


def _router_params(w_gr, b_gr, w_er, b_er):
    wr = _pad_lanes(jnp.concatenate([w_gr, w_er], axis=1))
    br = _pad_lanes(jnp.concatenate([b_gr, b_er])[None])
    wr_hi = wr.astype(BF16)
    wr_lo = (wr - wr_hi.astype(F32)).astype(BF16)
    return jnp.stack([wr_hi, wr_lo]), br


def _even_layer(x, mod, g_mix, state, w_in, conv_w, conv_b, a_log, dt_bias, d_skip, g_ssd, hy_conv_w, hy_conv_b,
                hy_w1, hy_b1, hy_w2, hy_b2, hy_w3, hy_freq, hy_bias):
    n0 = D + SSD_XBC
    w_bf = jnp.concatenate([w_in[:, :n0], w_in[:, n0 + SSD_H:], w_in[:, n0:n0 + SSD_H],
                            jnp.zeros((D, LANES - SSD_H), F32)], axis=1).astype(BF16)
    z, xbc, hy, dtr = even_in_proj(x, mod, g_mix, w_bf)
    small = (conv_w, conv_b[None], _pad_lanes(dt_bias), _pad_lanes(a_log), jnp.repeat(d_skip, SSD_P)[None], g_ssd[None])
    y_c, fin = ssd_mixer(xbc, dtr, z, None, *small, L=L_CTX, n_seq=N_CTX, row_off=0)
    (y_l,) = ssd_mixer(xbc, dtr, z, state.reshape(N_LAT, 2, SSD_H * SSD_P, SSD_N), *small,
                       L=L_LAT, n_seq=N_LAT, row_off=T_CTX)
    w1 = jnp.pad(hy_w1, ((0, LANES - HY_FEAT), (0, 0)))
    w3r = hy_w3.reshape(HY_HID, 4, D).transpose(1, 0, 2)
    deltas = jnp.asarray(np.linspace(HY_MIN_DECAY, HY_MAX_DECAY, D).astype(np.float32))[None]
    us = []
    for L, n_seq, off in ((L_CTX, N_CTX, 0), (L_LAT, N_LAT, T_CTX)):
        h4, hm = hyena_filter_spectra(_hyena_features(L), w1, hy_b1[None], hy_w2, hy_b2[None], hy_freq, w3r, deltas, L=L)
        us.append(hyena_mixer(hy, hy_conv_w, hy_conv_b[None], h4, hm, hy_bias, L=L, n_seq=n_seq, row_off=off))
    return (y_c, y_l), tuple(us), fin


def _odd_layer(x, mod, g_mix, cache_k, cache_v, cache_ckv, cache_kr, rel_bias, w_in, g_q, w_uq, g_kv, w_ukv):
    w_bf = jnp.pad(w_in, ((0, 0), (0, ODD_COLS - w_in.shape[1]))).astype(BF16)
    wuq = w_uq.reshape(MLA_QR, MLA_H, MLA_QK)
    wuq_bf = jnp.concatenate([wuq[:, :, :MLA_NOPE].reshape(MLA_QR, -1), wuq[:, :, MLA_NOPE:].reshape(MLA_QR, -1)],
                             axis=1).astype(BF16)
    wukv = w_ukv.reshape(MLA_KVR, MLA_H, MLA_NOPE + MLA_V)
    wukv_bf = jnp.concatenate([wukv[:, :, :MLA_NOPE].reshape(MLA_KVR, -1), wukv[:, :, MLA_NOPE:].reshape(MLA_KVR, -1)],
                              axis=1).astype(BF16)
    qkv, qm, ckv, kvu, kr, k_new, v_new = odd_in_proj(x, mod, g_mix, w_bf, g_q[None], wuq_bf, g_kv[None], wukv_bf,
                                                      rope_tables())
    ona_c, omla_c = attn_context(qkv, qm, kvu, kr)
    ona_l = attn_neighbourhood_latent(qkv, cache_k, cache_v, neighbourhood_bias(rel_bias))
    omla_l = attn_mla_latent(qm, kvu, kr, cache_ckv, cache_kr, wukv_bf)
    return (ona_c, ona_l), (omla_c, omla_l), k_new, v_new, ckv, kr


def kernel(x_prompt, x_sample, state_ssd, cache_na_k, cache_na_v, cache_mla_ckv, cache_mla_krope, c, c_ctx, w_ada, b_ada, norm_mix, norm_ffn, norm_final, ev_w_in, ev_conv_w, ev_conv_b, ssd_A_log, ssd_dt_bias, ssd_d, ssd_norm, hy_conv_w, hy_conv_b, hy_w1, hy_b1, hy_w2, hy_b2, hy_w3, hy_freq, hy_bias, ev_w_out, od_w_in, mla_q_norm, mla_w_uq, mla_kv_norm, mla_w_ukv, na_rel_bias, od_w_out, moe_w_gr, moe_b_gr, moe_w_er, moe_b_er, moe_w_gate, moe_w_up, moe_w_down):
    x = (x_prompt.reshape(T_CTX, D), x_sample.reshape(T_LAT, D))
    cvec = jnp.zeros((MOD_ROWS, D), F32).at[0].set(c_ctx).at[1:1 + N_LAT].set(c)
    mod = ada_modulation(cvec, w_ada, b_ada)
    gfin = norm_final[None]

    y, u, fin = _even_layer(x, mod[0], norm_mix[0][None], state_ssd[:, 0], ev_w_in[0], ev_conv_w[0], ev_conv_b[0],
                            ssd_A_log[0], ssd_dt_bias[0], ssd_d[0], ssd_norm[0], hy_conv_w[0], hy_conv_b[0],
                            hy_w1[0], hy_b1[0], hy_w2[0], hy_b2[0], hy_w3[0], hy_freq[0], hy_bias[0])
    wr, br = _router_params(moe_w_gr[0], moe_b_gr[0], moe_w_er[0], moe_b_er[0])
    xn, h2, ids, wts, cnt3 = out_proj_router([y, u], ev_w_out[0].astype(BF16), x, mod[0], norm_ffn[0][None], wr, br)
    x = moe_block(h2, ids, wts, cnt3, xn, mod[0], gfin, moe_w_gate, moe_w_up, moe_w_down, 0, final=False)

    o_na, o_mla, k_new, v_new, ckv, kr = _odd_layer(x, mod[1], norm_mix[1][None], cache_na_k, cache_na_v, cache_mla_ckv,
                                           cache_mla_krope, na_rel_bias[0], od_w_in[0], mla_q_norm[0], mla_w_uq[0],
                                           mla_kv_norm[0], mla_w_ukv[0])
    wr, br = _router_params(moe_w_gr[1], moe_b_gr[1], moe_w_er[1], moe_b_er[1])
    xn, h2, ids, wts, cnt3 = out_proj_router([o_na, o_mla], od_w_out[0].astype(BF16), x, mod[1], norm_ffn[1][None], wr, br)
    out = moe_block(h2, ids, wts, cnt3, xn, mod[1], gfin, moe_w_gate, moe_w_up, moe_w_down, 1, final=True)

    return (out[:T_CTX].reshape(N_CTX, L_CTX, D),
            out[T_CTX:].reshape(N_LAT, L_LAT, D),
            fin.reshape(N_CTX, 1, 2, SSD_H, SSD_P, SSD_N),
            k_new[:, None],
            v_new[:, None],
            ckv[:T_CTX].reshape(N_CTX, 1, L_CTX, MLA_KVR),
            kr[:T_CTX, :MLA_ROPE].reshape(N_CTX, 1, L_CTX, MLA_ROPE))
```

```python
import functools
import math

import numpy as np
import jax
import jax.numpy as jnp
from jax import lax
from jax.experimental import pallas as pl
from jax.experimental.pallas import tpu as pltpu

F32 = jnp.float32
BF16 = jnp.bfloat16
HIGHEST = lax.Precision.HIGHEST

D = 1024
N_CTX, L_CTX = 16, 256
N_LAT, L_LAT = 8, 1024
T_CTX = N_CTX * L_CTX
T_LAT = N_LAT * L_LAT
T_ALL = T_CTX + T_LAT
PAST = 512
GRID_W = 64
EPS = 1e-6
NEG = -1e30

SSD_H, SSD_P, SSD_N, SSD_G = 16, 64, 128, 2
SSD_XBC = D + 2 * SSD_G * SSD_N
SSD_K = 5
CHUNK = 128

HY_K = 3
HY_BANDS = 16
HY_FEAT = 1 + 2 * HY_BANDS
HY_HID = 64
HY_MIN_DECAY = abs(math.log(1e-2) / 1.5)
HY_MAX_DECAY = abs(math.log(1e-2) / 0.3)

NA_H, NA_D = 8, 64
NA_W = NA_H * NA_D
NA_WIN_R, NA_WIN_C = 8, 16
MLA_H, MLA_QR, MLA_KVR = 8, 256, 128
MLA_NOPE, MLA_ROPE, MLA_V = 64, 32, 64
MLA_QK = MLA_NOPE + MLA_ROPE
ROPE_F = MLA_ROPE // 4

MOE_G, MOE_PG, MOE_E, MOE_F = 4, 8, 32, 256

LANES = 128
SUBLANES = 8
VMEM_LIMIT = 56 * 1024 * 1024

TM = 256
N_TILES = T_ALL // TM
CTX_TILES = T_CTX // TM
LAT_TILES_PER_SEQ = L_LAT // TM
MOD_ROWS = 16


def _cparams(sem):
    return pltpu.CompilerParams(dimension_semantics=sem, vmem_limit_bytes=VMEM_LIMIT)


def _mod_row(i):
    return jnp.where(i < CTX_TILES, 0, 1 + (i - CTX_TILES) // LAT_TILES_PER_SEQ)


def _silu(x):
    return x * jax.nn.sigmoid(x)


def _rms(x):
    return x * lax.rsqrt(jnp.mean(x * x, axis=-1, keepdims=True) + EPS)


def _ada_kernel(c_ref, w_ref, b_ref, o_ref):
    c = c_ref[...]
    o_ref[0] = jnp.dot(_silu(c), w_ref[0], precision=HIGHEST, preferred_element_type=F32) + b_ref[0]


def ada_modulation(cvec, w_ada, b_ada):
    depth = w_ada.shape[0]
    out = pl.pallas_call(
        _ada_kernel,
        grid=(depth, 6),
        in_specs=[
            pl.BlockSpec((MOD_ROWS, D), lambda l, j: (0, 0)),
            pl.BlockSpec((1, D, D), lambda l, j: (l, 0, j)),
            pl.BlockSpec((1, 1, D), lambda l, j: (l, 0, j)),
        ],
        out_specs=pl.BlockSpec((1, MOD_ROWS, D), lambda l, j: (l, 0, j)),
        out_shape=jax.ShapeDtypeStruct((depth, MOD_ROWS, 6 * D), F32),
        compiler_params=_cparams(("arbitrary", "arbitrary")),
        name="ada",
    )(cvec, w_ada, b_ada.reshape(depth, 1, 6 * D))
    return out.reshape(depth, MOD_ROWS, 6, D)


PROJ_CHUNK = 512


def _modulated(x, g_ref, mod_ref, shift_row):
    h = _rms(x) * g_ref[...]
    return h * (1.0 + mod_ref[0, shift_row + 1:shift_row + 2, :]) + mod_ref[0, shift_row:shift_row + 1, :]


IN_TM = 512
IN_CTX_TILES = T_CTX // IN_TM


def _even_in_kernel(xc_ref, xl_ref, mod_ref, g_ref, w_ref, z_ref, xbc_ref, hy_ref, dt_ref):
    x = jnp.where(pl.program_id(0) < IN_CTX_TILES, xc_ref[...], xl_ref[...])
    hb = _modulated(x, g_ref, mod_ref, 0).astype(BF16)
    col = 0
    for o_ref in (z_ref, xbc_ref, hy_ref, dt_ref):
        width = o_ref.shape[1]
        for c0 in range(0, width, PROJ_CHUNK):
            c1 = min(c0 + PROJ_CHUNK, width)
            o_ref[:, c0:c1] = jnp.dot(hb, w_ref[:, col + c0:col + c1], preferred_element_type=F32)
        col += width


def _pair_specs(width):
    return [pl.BlockSpec((TM, width), lambda i: (jnp.minimum(i, CTX_TILES - 1), 0)),
            pl.BlockSpec((TM, width), lambda i: (jnp.maximum(i - CTX_TILES, 0), 0))]


def even_in_proj(x_pair, mod, g, w_bf):
    widths = (D, SSD_XBC, 3 * D, LANES)
    mod_row = lambda i: jnp.where(i < IN_CTX_TILES, 0, 1 + (i - IN_CTX_TILES) // (L_LAT // IN_TM))
    return pl.pallas_call(
        _even_in_kernel,
        grid=(T_ALL // IN_TM,),
        in_specs=[
            pl.BlockSpec((IN_TM, D), lambda i: (jnp.minimum(i, IN_CTX_TILES - 1), 0)),
            pl.BlockSpec((IN_TM, D), lambda i: (jnp.maximum(i - IN_CTX_TILES, 0), 0)),
            pl.BlockSpec((1, 6, D), lambda i: (mod_row(i), 0, 0)),
            pl.BlockSpec((1, D), lambda i: (0, 0)),
            _const_spec(w_bf),
        ],
        out_specs=[pl.BlockSpec((IN_TM, w), lambda i: (i, 0)) for w in widths],
        out_shape=[jax.ShapeDtypeStruct((T_ALL, w), F32) for w in widths],
        compiler_params=_cparams(("arbitrary",)),
        name="even_in",
    )(*x_pair, mod, g, w_bf)


PAD = SUBLANES


def _ssd_kernel(*refs, L, has_init):
    if has_init:
        (xbc_ref, dt_ref, z_ref, init_ref, cw_ref, cb_ref, dtb_ref, alog_ref, dsk_ref, gs_ref,
         y_ref, xp_s, xc_s, ya_s, st_s) = refs
        fin_ref = None
    else:
        (xbc_ref, dt_ref, z_ref, cw_ref, cb_ref, dtb_ref, alog_ref, dsk_ref, gs_ref,
         y_ref, fin_ref, xp_s, xc_s, ya_s, st_s) = refs
        init_ref = None
    nc = L // CHUNK
    half = SSD_K // 2

    xp_s[0:PAD, :] = jnp.zeros((PAD, SSD_XBC), F32)
    xp_s[PAD + L:2 * PAD + L, :] = jnp.zeros((PAD, SSD_XBC), F32)
    xp_s[PAD:PAD + L, :] = xbc_ref[...]
    for c in range(nc):
        base = PAD + c * CHUNK - half
        for j in range(SSD_XBC // LANES):
            cols = slice(j * LANES, (j + 1) * LANES)
            acc = cb_ref[:, cols] + xp_s[base:base + CHUNK, cols] * cw_ref[0:1, cols]
            for k in range(1, SSD_K):
                acc = acc + xp_s[base + k:base + k + CHUNK, cols] * cw_ref[k:k + 1, cols]
            xc_s[c * CHUNK:(c + 1) * CHUNK, cols] = _silu(acc)

    row = lax.broadcasted_iota(jnp.int32, (CHUNK, CHUNK), 0)
    colm = lax.broadcasted_iota(jnp.int32, (CHUNK, CHUNK), 1)
    lane_lo = colm < SSD_P
    tri_lo = (colm <= row).astype(F32)
    tri_up = (colm >= row).astype(F32)

    for d in range(2):
        causal = (colm <= row) if d == 0 else (colm >= row)
        for j in range(SSD_H * SSD_P // CHUNK):
            if has_init:
                st_s[:, j * CHUNK:(j + 1) * CHUNK] = init_ref[0, d, j * CHUNK:(j + 1) * CHUNK, :].T
            else:
                st_s[:, j * CHUNK:(j + 1) * CHUNK] = jnp.zeros((CHUNK, CHUNK), F32)

        def chunk_body(ci, carry, d=d, causal=causal):
            c = ci if d == 0 else nc - 1 - ci
            r0 = pl.multiple_of(c * CHUNK, CHUNK)
            dt = jax.nn.softplus(dt_ref[pl.ds(r0, CHUNK), :] + dtb_ref[d:d + 1, :])
            a = dt * (-jnp.exp(alog_ref[d:d + 1, :]))
            tri = tri_lo if d == 0 else tri_up
            cs = jnp.dot(tri, a, precision=HIGHEST, preferred_element_type=F32)
            cs_t = jnp.dot(a.T, tri.T, precision=HIGHEST, preferred_element_type=F32)
            edge = cs[CHUNK - 1:CHUNK, :] if d == 0 else cs[0:1, :]
            ecs = jnp.exp(cs)
            dec = jnp.exp(edge - cs)
            cdec = jnp.exp(edge)
            for g in range(SSD_G):
                bm = xc_s[pl.ds(r0, CHUNK), D + g * SSD_N:D + (g + 1) * SSD_N]
                cm = xc_s[pl.ds(r0, CHUNK), D + (SSD_G + g) * SSD_N:D + (SSD_G + g + 1) * SSD_N]
                bm_b, cm_b = bm.astype(BF16), cm.astype(BF16)
                cb = lax.dot_general(cm_b, bm_b, (((1,), (1,)), ((), ())), preferred_element_type=F32)
                bm_t = bm.T.astype(BF16)
                pairs = SSD_H // SSD_G // 2
                for pp in range(pairs):
                    p = g * pairs + pp
                    h0, h1 = 2 * p, 2 * p + 1
                    cols = slice(p * CHUNK, (p + 1) * CHUNK)
                    xs = xc_s[pl.ds(r0, CHUNK), cols]
                    xdt = xs * jnp.where(lane_lo, dt[:, h0:h0 + 1], dt[:, h1:h1 + 1])
                    ms = []
                    for h in (h0, h1):
                        diff = cs[:, h:h + 1] - cs_t[h:h + 1, :]
                        ms.append(cb * jnp.exp(jnp.where(causal, diff, NEG)))
                    mcat = jnp.concatenate(ms, axis=1).astype(BF16)
                    xbd = jnp.concatenate([jnp.where(lane_lo, xdt, 0.0), jnp.where(lane_lo, 0.0, xdt)],
                                          axis=0).astype(BF16)
                    y_diag = jnp.dot(mcat, xbd, preferred_element_type=F32)
                    st = st_s[:, cols]
                    y_off = jnp.dot(cm_b, st.astype(BF16), preferred_element_type=F32)
                    y_off = y_off * jnp.where(lane_lo, ecs[:, h0:h0 + 1], ecs[:, h1:h1 + 1])
                    y = y_diag + y_off
                    if d == 0:
                        ya_s[pl.ds(r0, CHUNK), cols] = y
                    else:
                        ya_s[pl.ds(r0, CHUNK), cols] = ya_s[pl.ds(r0, CHUNK), cols] + y
                    xdd = (xdt * jnp.where(lane_lo, dec[:, h0:h0 + 1], dec[:, h1:h1 + 1])).astype(BF16)
                    snew = jnp.dot(bm_t, xdd, preferred_element_type=F32)
                    st_s[:, cols] = st * jnp.where(lane_lo[0:1, :], cdec[:, h0:h0 + 1], cdec[:, h1:h1 + 1]) + snew
            return carry

        lax.fori_loop(0, nc, chunk_body, 0)
        if fin_ref is not None:
            for j in range(SSD_H * SSD_P // CHUNK):
                fin_ref[0, d, j * CHUNK:(j + 1) * CHUNK, :] = st_s[:, j * CHUNK:(j + 1) * CHUNK].T

    def out_body(c, carry):
        r0 = pl.multiple_of(c * CHUNK, CHUNK)
        y = ya_s[pl.ds(r0, CHUNK), :] + xc_s[pl.ds(r0, CHUNK), 0:D] * dsk_ref[...]
        y = y * _silu(z_ref[pl.ds(r0, CHUNK), :])
        y_ref[pl.ds(r0, CHUNK), :] = (_rms(y) * gs_ref[...]).astype(y_ref.dtype)
        return carry

    lax.fori_loop(0, nc, out_body, 0)


def ssd_mixer(xbc, dtr, z, init, cw, cb, dtb, alog, dsk, gs, *, L, n_seq, row_off):
    blk0 = row_off // L
    has_init = init is not None
    seq = lambda w: pl.BlockSpec((L, w), lambda b: (blk0 + b, 0))
    full = lambda arr: pl.BlockSpec(arr.shape, lambda b: (0,) * arr.ndim)
    in_specs = [seq(SSD_XBC), seq(LANES), seq(D)]
    args = [xbc, dtr, z]
    if has_init:
        in_specs.append(pl.BlockSpec((1, 2, SSD_H * SSD_P, SSD_N), lambda b: (b, 0, 0, 0)))
        args.append(init)
    small = [cw, cb, dtb, alog, dsk, gs]
    in_specs += [full(a) for a in small]
    args += small
    out_specs = [pl.BlockSpec((L, D), lambda b: (b, 0))]
    out_shape = [jax.ShapeDtypeStruct((n_seq * L, D), BF16)]
    if not has_init:
        out_specs.append(pl.BlockSpec((1, 2, SSD_H * SSD_P, SSD_N), lambda b: (b, 0, 0, 0)))
        out_shape.append(jax.ShapeDtypeStruct((n_seq, 2, SSD_H * SSD_P, SSD_N), F32))
    return pl.pallas_call(
        functools.partial(_ssd_kernel, L=L, has_init=has_init),
        grid=(n_seq,),
        in_specs=in_specs,
        out_specs=out_specs,
        out_shape=out_shape,
        scratch_shapes=[
            pltpu.VMEM((L + 2 * PAD, SSD_XBC), F32),
            pltpu.VMEM((L, SSD_XBC), F32),
            pltpu.VMEM((L, D), F32),
            pltpu.VMEM((SSD_N, SSD_H * SSD_P), F32),
        ],
        compiler_params=_cparams(("arbitrary",)),
        name=f"ssd_{L}",
    )(*args)


HY_CB = 256


def filter_dft_matrices(L):
    H = L // 2
    s = np.arange(L, dtype=np.int64)[None, :]
    k = np.arange(H, dtype=np.int64)[:, None]
    ang = lambda kk: ((kk * s) % (2 * L)).astype(np.float64) * (math.pi / L)
    ca, cb = np.cos(ang(k)), np.cos(ang(L - k))
    sa, sb = np.sin(ang(k)), np.sin(ang(L - k))
    cb[0] = np.where(s[0] % 2 == 0, 1.0, -1.0)
    sa[0], sb[0] = 0.0, 0.0
    fm = np.zeros((2 * SUBLANES, L))
    fm[0], fm[1] = np.cos(ang(H))[0], np.sin(ang(H))[0]
    mats = (np.concatenate([ca, cb], axis=0), np.concatenate([sa, sb], axis=0), fm)
    return tuple(jnp.asarray(m.astype(np.float32)).astype(BF16) for m in mats)


def _const_spec(arr):
    return pl.BlockSpec(arr.shape, lambda *_: (0,) * arr.ndim, pipeline_mode=pl.Buffered(1))


def _hy_filter_kernel(feat_ref, w1_ref, b1_ref, w2_ref, b2_ref, fr_ref, w3_ref, dl_ref, fs_ref, fd_ref, fm_ref,
                      h_ref, hm_ref, *, L):
    H = L // 2
    hp = functools.partial(jnp.dot, precision=HIGHEST, preferred_element_type=F32)
    hdn = jnp.sin(fr_ref[0:1, :] * (hp(feat_ref[...], w1_ref[...]) + b1_ref[...]))
    hdn = jnp.sin(fr_ref[1:2, :] * (hp(hdn, w2_ref[...]) + b2_ref[...]))
    rowi = lax.broadcasted_iota(jnp.int32, (L, 1), 0)
    t = rowi.astype(F32) * (1.0 / (L - 1))
    dec = jnp.exp(-t * dl_ref[...])
    first = rowi == 0
    for o in range(2):
        fwd = hp(hdn, w3_ref[2 * o]) * dec
        bwd = jnp.where(first, 0.0, hp(hdn, w3_ref[2 * o + 1]) * dec)
        hs, hd = (fwd + bwd).astype(BF16), (fwd - bwd).astype(BF16)
        ss = jnp.dot(fs_ref[...], hs, preferred_element_type=F32)
        sd = jnp.dot(fd_ref[...], hd, preferred_element_type=F32)
        h_ref[o, 0] = ss[0:H]
        h_ref[o, 1] = sd[0:H]
        h_ref[o, 2] = ss[H:L]
        h_ref[o, 3] = sd[H:L]
        mid_r = jnp.dot(fm_ref[...], hs, preferred_element_type=F32)
        mid_n = jnp.dot(fm_ref[...], hd, preferred_element_type=F32)
        hm_ref[o] = jnp.concatenate([mid_r[0:1], mid_n[1:2], jnp.zeros((SUBLANES - 2, mid_r.shape[1]), F32)], axis=0)


def hyena_filter_spectra(feat, w1, b1, w2, b2, freq, w3r, deltas, *, L):
    full = lambda arr: pl.BlockSpec(arr.shape, lambda j: (0,) * arr.ndim)
    mats = filter_dft_matrices(L)
    return pl.pallas_call(
        functools.partial(_hy_filter_kernel, L=L),
        grid=(D // HY_CB,),
        in_specs=[full(feat), full(w1), full(b1), full(w2), full(b2), full(freq),
                  pl.BlockSpec((4, HY_HID, HY_CB), lambda j: (0, 0, j)),
                  pl.BlockSpec((1, HY_CB), lambda j: (0, j))] + [_const_spec(m) for m in mats],
        out_specs=[pl.BlockSpec((2, 4, L // 2, HY_CB), lambda j: (0, 0, 0, j)),
                   pl.BlockSpec((2, SUBLANES, HY_CB), lambda j: (0, 0, j))],
        out_shape=[jax.ShapeDtypeStruct((2, 4, L // 2, D), F32), jax.ShapeDtypeStruct((2, SUBLANES, D), F32)],
        compiler_params=_cparams(("arbitrary",)),
        name=f"hy_filter_{L}",
    )(feat, w1, b1, w2, b2, freq, w3r, deltas, *mats)


def split_dft_matrices(L):
    H = L // 2
    k = np.arange(H, dtype=np.int64)[:, None]
    m = np.arange(H, dtype=np.int64)[None, :]
    alt = np.where(m % 2 == 0, 1.0, -1.0)
    ang_e = ((k * m) % L).astype(np.float64) * (2 * math.pi / L)
    ang_o = ((k * (2 * m + 1)) % (2 * L)).astype(np.float64) * (math.pi / L)
    ce, se, co, so = np.cos(ang_e), np.sin(ang_e), np.cos(ang_o), np.sin(ang_o)
    se[0], so[0] = alt[0], alt[0]
    w = np.where(k == 0, 1.0, 2.0) / (2 * L)
    fe = np.concatenate([ce, se], axis=0)
    fo = np.concatenate([co, so], axis=0)
    ge = np.concatenate([(ce * w).T, se.T / L], axis=1)
    go = np.concatenate([(co * w).T, so.T / L], axis=1)
    return tuple(jnp.asarray(a.astype(np.float32)).astype(BF16) for a in (fe, fo, ge, go))


def _store_lane_blocks(ref, val):
    for c in range(ref.shape[0]):
        ref[c] = val[:, c * LANES:(c + 1) * LANES]


def _load_parity(ref, parity, n):
    return jnp.concatenate([ref[c, pl.ds(parity, n, stride=2), :] for c in range(ref.shape[0])], axis=1)


def _hyena_kernel(p0_ref, p1_ref, p2_ref, w0_ref, w1_ref, w2_ref, b0_ref, b1_ref, b2_ref, h_ref, hm_ref, hb_ref,
                  fe_ref, fo_ref, ge_ref, go_ref, o_ref, xp_s, u_s, y_s, *, L, cb):
    H = L // 2
    xp_s[0:PAD, :] = jnp.zeros((PAD, cb), F32)
    xp_s[PAD + L:2 * PAD + L, :] = jnp.zeros((PAD, cb), F32)
    first = lax.broadcasted_iota(jnp.int32, (H, 1), 0) == 0

    def conv(p_ref, w_ref, b_ref):
        xp_s[PAD:PAD + L, :] = p_ref[...]
        acc = b_ref[...] + xp_s[PAD - 1:PAD - 1 + L, :] * w_ref[0:1, :]
        for k in range(1, HY_K):
            acc = acc + xp_s[PAD - 1 + k:PAD - 1 + k + L, :] * w_ref[k:k + 1, :]
        return acc

    u = conv(p0_ref, w0_ref, b0_ref)
    for o, (p_ref, w_ref, b_ref) in enumerate(((p1_ref, w1_ref, b1_ref), (p2_ref, w2_ref, b2_ref))):
        _store_lane_blocks(u_s, u)
        se = jnp.dot(fe_ref[...], _load_parity(u_s, 0, H).astype(BF16), preferred_element_type=F32)
        so = jnp.dot(fo_ref[...], _load_parity(u_s, 1, H).astype(BF16), preferred_element_type=F32)
        e, es, od, os_ = se[0:H], se[H:L], so[0:H], so[H:L]
        b0, b1 = e + od, e - od
        b2 = jnp.where(first, es, es + os_)
        b3 = jnp.where(first, os_, os_ - es)
        har, han, hbr, hbn = h_ref[o, 0], h_ref[o, 1], h_ref[o, 2], h_ref[o, 3]
        hmr, hmn = hm_ref[o, 0:1, :], hm_ref[o, 1:2, :]
        y0 = b0 * har - b2 * han
        y1 = b1 * hbr - b3 * hbn
        y2 = b0 * han + b2 * har
        y3 = b1 * hbn + b3 * hbr
        mid_r = b2[0:1] * hmr - b3[0:1] * hmn
        mid_n = b2[0:1] * hmn + b3[0:1] * hmr
        de = jnp.where(first, mid_r, y2 - y3)
        do = jnp.where(first, mid_n, y2 + y3)
        ye = jnp.dot(ge_ref[...], jnp.concatenate([y0 + y1, de], axis=0).astype(BF16), preferred_element_type=F32)
        yo = jnp.dot(go_ref[...], jnp.concatenate([y0 - y1, do], axis=0).astype(BF16), preferred_element_type=F32)
        for c in range(cb // LANES):
            y_s[c, pl.ds(0, H, stride=2), :] = ye[:, c * LANES:(c + 1) * LANES]
            y_s[c, pl.ds(1, H, stride=2), :] = yo[:, c * LANES:(c + 1) * LANES]
        y = jnp.concatenate([y_s[c] for c in range(cb // LANES)], axis=1)
        u = conv(p_ref, w_ref, b_ref) * (y + u * hb_ref[o:o + 1, :])
    o_ref[...] = u.astype(o_ref.dtype)


def hyena_mixer(hy, conv_w, conv_b, h4, hm, hy_bias, *, L, n_seq, row_off):
    blk0 = row_off // L
    cb = min(D, HY_CB * (L_LAT // L))
    nj = D // cb
    H = L // 2
    part = lambda q: pl.BlockSpec((L, cb), lambda j, b: (blk0 + b, q * nj + j))
    wpart = lambda q: pl.BlockSpec((HY_K, cb), lambda j, b: (0, q * nj + j))
    bpart = lambda q: pl.BlockSpec((1, cb), lambda j, b: (0, q * nj + j))
    mats = split_dft_matrices(L)
    return pl.pallas_call(
        functools.partial(_hyena_kernel, L=L, cb=cb),
        grid=(nj, n_seq),
        in_specs=[part(0), part(1), part(2), wpart(0), wpart(1), wpart(2), bpart(0), bpart(1), bpart(2),
                  pl.BlockSpec((2, 4, H, cb), lambda j, b: (0, 0, 0, j)),
                  pl.BlockSpec((2, SUBLANES, cb), lambda j, b: (0, 0, j)),
                  pl.BlockSpec((2, cb), lambda j, b: (0, j))]
                 + [_const_spec(m) for m in mats],
        out_specs=pl.BlockSpec((L, cb), lambda j, b: (b, j)),
        out_shape=jax.ShapeDtypeStruct((n_seq * L, D), BF16),
        scratch_shapes=[pltpu.VMEM((L + 2 * PAD, cb), F32), pltpu.VMEM((cb // LANES, L, LANES), F32),
                        pltpu.VMEM((cb // LANES, L, LANES), F32)],
        compiler_params=_cparams(("arbitrary", "arbitrary")),
        name=f"hyena_{L}",
    )(hy, hy, hy, conv_w, conv_w, conv_w, conv_b, conv_b, conv_b, h4, hm, hy_bias, *mats)


ROUTER_LANES = LANES
BIG_LANE = 1e9


ROW_GROUP = D // LANES


def _store_row_groups(ref, val):
    n = val.shape[0]
    for s in range(ROW_GROUP):
        ref[pl.ds(s, n, stride=ROW_GROUP), :] = val[:, s * LANES:(s + 1) * LANES]


def _load_row_groups(ref, n, s):
    return ref[pl.ds(s, n, stride=ROW_GROUP), :]


def _first_max_lane(v, lanef):
    m = jnp.max(v, axis=-1, keepdims=True)
    return m, jnp.min(jnp.where(v == m, lanef, BIG_LANE), axis=-1, keepdims=True)


def _out_router_kernel(*refs, n_in, x_is_pair):
    a_refs = refs[:2 * n_in]
    refs = refs[2 * n_in:]
    is_ctx = pl.program_id(0) < CTX_TILES
    if x_is_pair:
        x = jnp.where(is_ctx, refs[0][...], refs[1][...])
        refs = refs[2:]
    else:
        x = refs[0][...]
        refs = refs[1:]
    w_ref, mod_ref, gf_ref, wr_ref, br_ref, xo_ref, h2_ref, ids_ref, wts_ref, cnt_ref = refs
    acc, k0 = None, 0
    for ac_ref, al_ref in zip(a_refs[0::2], a_refs[1::2]):
        kk = ac_ref.shape[1]
        a = jnp.where(is_ctx, ac_ref[...], al_ref[...])
        part = jnp.dot(a, w_ref[k0:k0 + kk, :], preferred_element_type=F32)
        acc = part if acc is None else acc + part
        k0 += kk
    xn = x + mod_ref[0, 2:3, :] * acc
    xo_ref[...] = xn
    h2 = _modulated(xn, gf_ref, mod_ref, 3)
    h2_ref[...] = h2

    h_hi = h2.astype(BF16)
    h_lo = (h2 - h_hi.astype(F32)).astype(BF16)
    logits = (jnp.dot(h_hi, wr_ref[0], preferred_element_type=F32) + jnp.dot(h_lo, wr_ref[0], preferred_element_type=F32)
              + jnp.dot(h_hi, wr_ref[1], preferred_element_type=F32) + br_ref[...])
    lanef = lax.broadcasted_iota(jnp.int32, logits.shape, 1).astype(F32)
    gl = jnp.where(lanef < MOE_G, logits, NEG)
    gm, gi = _first_max_lane(gl, lanef)
    g_w = 1.0 / jnp.sum(jnp.exp(gl - gm), axis=-1, keepdims=True)
    lo = MOE_G + MOE_PG * gi
    el = jnp.where((lanef >= lo) & (lanef < lo + MOE_PG), logits, NEG)
    m1, e1 = _first_max_lane(el, lanef)
    m2, e2 = _first_max_lane(jnp.where(lanef == e1, NEG, el), lanef)
    p2 = jnp.exp(m2 - m1)
    w1 = g_w / (1.0 + p2)
    ids_ref[...] = jnp.where(lanef == 0, e1 - MOE_G, jnp.where(lanef == 1, e2 - MOE_G, 0.0)).astype(jnp.int32)
    wts_ref[...] = jnp.where(lanef == 0, w1, jnp.where(lanef == 1, w1 * p2, 0.0))
    chosen = ((lanef == e1 - MOE_G) | (lanef == e2 - MOE_G)).astype(F32)
    cnt_ref[0] = jnp.sum(chosen, axis=0, keepdims=True).astype(jnp.int32)


def out_proj_router(acts, w_bf, x, mod, gf, wr, br):
    tile = lambda w: pl.BlockSpec((TM, w), lambda i: (i, 0))
    full = lambda arr: pl.BlockSpec(arr.shape, lambda i: (0,) * arr.ndim)
    x_is_pair = isinstance(x, tuple)
    xs = x if x_is_pair else (x,)
    return pl.pallas_call(
        functools.partial(_out_router_kernel, n_in=len(acts), x_is_pair=x_is_pair),
        grid=(N_TILES,),
        in_specs=[s for a in acts for s in _pair_specs(a[0].shape[1])]
                 + (_pair_specs(D) if x_is_pair else [tile(D)])
                 + [full(w_bf), pl.BlockSpec((1, 6, D), lambda i: (_mod_row(i), 0, 0)), full(gf), full(wr), full(br)],
        out_specs=[tile(D), tile(D), tile(ROUTER_LANES), tile(ROUTER_LANES),
                   pl.BlockSpec((1, 1, ROUTER_LANES), lambda i: (i, 0, 0))],
        out_shape=[jax.ShapeDtypeStruct((T_ALL, D), F32), jax.ShapeDtypeStruct((T_ALL, D), F32),
                   jax.ShapeDtypeStruct((T_ALL, ROUTER_LANES), jnp.int32),
                   jax.ShapeDtypeStruct((T_ALL, ROUTER_LANES), F32),
                   jax.ShapeDtypeStruct((N_TILES, 1, ROUTER_LANES), jnp.int32)],
        compiler_params=_cparams(("arbitrary",)),
        name="out_router",
    )(*[part for a in acts for part in a], *xs, w_bf, mod, gf, wr, br)


N_ASSIGN = 2 * T_ALL
MOE_TILES = N_ASSIGN // TM + MOE_E
N_SLOTS = MOE_TILES * TM


def route_tables(cnt3):
    cnt = cnt3[:, 0, :MOE_E]
    total = jnp.sum(cnt, axis=0)
    padded = (total + TM - 1) // TM * TM
    ends = jnp.cumsum(padded)
    gdst = (ends - padded)[None, :] + jnp.cumsum(cnt, axis=0) - cnt
    loc = jnp.cumsum(cnt, axis=1) - cnt
    starts = jnp.arange(MOE_TILES, dtype=jnp.int32) * TM
    tile_expert = jnp.minimum(jnp.sum((ends[None, :] <= starts[:, None]).astype(jnp.int32), axis=1), MOE_E - 1)
    n_used = (ends[-1] // TM).astype(jnp.int32).reshape(1)
    return cnt, loc, gdst, ends, tile_expert, n_used


RUN_BITS = (2 * TM).bit_length()
RUN_SMALL_BITS = 6


def _dispatch_kernel(cnt_s, loc_s, gdst_s, ends_s, h_ref, ids_ref, gcol_ref, xs_ref, dest_ref, srt, zbuf, sem, zsem):
    i = pl.program_id(0)
    slot = i % 2
    n_rows = 2 * TM

    @pl.when(i == 0)
    def _():
        zbuf[...] = jnp.zeros(zbuf.shape, zbuf.dtype)
        n_used = ends_s[MOE_E - 1] // TM
        for phase in ("start", "wait"):
            def tail(t, c, phase=phase):
                dst = pl.multiple_of(t * (TM * ROW_GROUP), TM * ROW_GROUP)
                cp = pltpu.make_async_copy(zbuf, xs_ref.at[pl.ds(dst, TM * ROW_GROUP), :], zsem)
                cp.start() if phase == "start" else cp.wait()
                return c

            lax.fori_loop(n_used, MOE_TILES, tail, 0)
            for e in range(MOE_E):
                end = ends_s[e]
                prev = ends_s[e - 1] if e > 0 else 0

                @pl.when(end > prev)
                def _(end=end, phase=phase):
                    dst = pl.multiple_of((end - TM) * ROW_GROUP, TM * ROW_GROUP)
                    cp = pltpu.make_async_copy(zbuf, xs_ref.at[pl.ds(dst, TM * ROW_GROUP), :], zsem)
                    cp.start() if phase == "start" else cp.wait()

    idt = ids_ref[...].astype(F32).T
    sub = lax.broadcasted_iota(jnp.int32, (LANES, TM), 0).astype(F32)
    m0 = (sub == idt[0:1, :]).astype(F32)
    m1 = (sub == idt[1:2, :]).astype(F32)
    mt = (m0 + m1).astype(BF16)
    tr = lax.broadcasted_iota(jnp.int32, (TM, TM), 0)
    tc = lax.broadcasted_iota(jnp.int32, (TM, TM), 1)
    earlier = jnp.dot(mt, (tr < tc).astype(BF16), preferred_element_type=F32)
    er = lax.broadcasted_iota(jnp.int32, (LANES, LANES), 0)
    ec = lax.broadcasted_iota(jnp.int32, (LANES, LANES), 1)
    below = jnp.dot((ec < er).astype(BF16), mt, preferred_element_type=F32)
    local = jnp.sum(below, axis=1, keepdims=True) + earlier
    glob = gcol_ref[0] + earlier
    pos0 = jnp.sum(m0 * local, axis=0, keepdims=True)
    pos1 = jnp.sum(m1 * local, axis=0, keepdims=True)
    dest_ref[0] = jnp.concatenate([jnp.sum(m0 * glob, axis=0, keepdims=True),
                                   jnp.sum(m1 * glob, axis=0, keepdims=True)], axis=0).astype(jnp.int32)

    srow = lax.broadcasted_iota(jnp.int32, (n_rows, TM), 0).astype(F32)
    perm = jnp.where((srow == pos0) | (srow == pos1), 1.0, 0.0).astype(BF16)
    _store_row_groups(srt.at[slot], jnp.dot(perm, h_ref[...].astype(BF16), preferred_element_type=F32))

    def run_pieces(n, s0, d0, bits):
        for b in bits:
            size = 1 << b
            off = (n >> (b + 1)) << (b + 1)

            @pl.when(((n >> b) & 1) == 1)
            def _(size=size, off=off):
                src = pl.multiple_of((s0 + off) * ROW_GROUP, ROW_GROUP)
                dst = pl.multiple_of((d0 + off) * ROW_GROUP, ROW_GROUP)
                pltpu.make_async_copy(srt.at[slot, pl.ds(src, size * ROW_GROUP), :],
                                      xs_ref.at[pl.ds(dst, size * ROW_GROUP), :], sem.at[slot]).start()

    for e in range(MOE_E):
        n, s0, d0 = cnt_s[0, 0, e], loc_s[0, 0, e], gdst_s[0, 0, e]

        @pl.when(n >= (1 << RUN_SMALL_BITS))
        def _(n=n, s0=s0, d0=d0):
            run_pieces(n, s0, d0, reversed(range(RUN_SMALL_BITS, RUN_BITS)))

        run_pieces(n, s0, d0, reversed(range(RUN_SMALL_BITS)))

    def wait(s):
        pltpu.make_async_copy(srt.at[s], xs_ref.at[pl.ds(0, n_rows * ROW_GROUP), :], sem.at[s]).wait()

    @pl.when(i > 0)
    def _():
        wait(1 - slot)

    @pl.when(i == N_TILES - 1)
    def _():
        wait(slot)


def dispatch_rows(h2, ids, cnt, loc, gdst, ends):
    tab = lambda: pl.BlockSpec((1, 1, MOE_E), lambda i: (i, 0, 0), memory_space=pltpu.SMEM)
    gcol = jnp.pad(gdst.astype(F32), ((0, 0), (0, LANES - MOE_E)))[:, :, None]
    return pl.pallas_call(
        _dispatch_kernel,
        grid=(N_TILES,),
        in_specs=[tab(), tab(), tab(), pl.BlockSpec(memory_space=pltpu.SMEM),
                  pl.BlockSpec((TM, D), lambda i: (i, 0)), pl.BlockSpec((TM, ROUTER_LANES), lambda i: (i, 0)),
                  pl.BlockSpec((1, LANES, 1), lambda i: (i, 0, 0))],
        out_specs=[pl.BlockSpec(memory_space=pl.ANY), pl.BlockSpec((1, 2, TM), lambda i: (i, 0, 0))],
        out_shape=[jax.ShapeDtypeStruct((N_SLOTS * ROW_GROUP, LANES), F32),
                   jax.ShapeDtypeStruct((N_TILES, 2, TM), jnp.int32)],
        scratch_shapes=[pltpu.VMEM((2, 2 * TM * ROW_GROUP, LANES), F32), pltpu.VMEM((TM * ROW_GROUP, LANES), F32),
                        pltpu.SemaphoreType.DMA((2,)), pltpu.SemaphoreType.DMA(())],
        compiler_params=_cparams(("arbitrary",)),
        name="moe_dispatch",
    )(cnt.reshape(N_TILES, 1, MOE_E), loc.reshape(N_TILES, 1, MOE_E), gdst.reshape(N_TILES, 1, MOE_E),
      ends, h2, ids, gcol)


DMA_UNROLL = 8


def _start_group_gather(src_hbm, idx_ref, n, dst_ref, sem):
    def body(j, c):
        for u in range(DMA_UNROLL):
            r = j * DMA_UNROLL + u
            src = pl.multiple_of(idx_ref[0, 0, r] * ROW_GROUP, ROW_GROUP)
            dst = pl.multiple_of(r * ROW_GROUP, ROW_GROUP)
            pltpu.make_async_copy(src_hbm.at[pl.ds(src, ROW_GROUP), :], dst_ref.at[pl.ds(dst, ROW_GROUP), :],
                                  sem).start(priority=u % 2)
        return c

    lax.fori_loop(0, n // DMA_UNROLL, body, 0)


def _wait_group_gather(src_hbm, dst_ref, sem):
    pltpu.make_async_copy(src_hbm.at[pl.ds(0, dst_ref.shape[0]), :], dst_ref, sem).wait()


def _experts_kernel(te_ref, nu_ref, x_ref, wg_ref, wu_ref, wd_ref, o_ref, xcat):
    i = pl.program_id(0)

    @pl.when(i < nu_ref[0])
    def _():
        for s in range(ROW_GROUP):
            xcat[:, s * LANES:(s + 1) * LANES] = _load_row_groups(x_ref, TM, s).astype(BF16)
        x = xcat[...]
        g = jnp.dot(x, wg_ref[0, 0].astype(BF16), preferred_element_type=F32)
        u = jnp.dot(x, wu_ref[0, 0].astype(BF16), preferred_element_type=F32)
        hid = (_silu(g) * u).astype(BF16)
        _store_row_groups(o_ref, jnp.dot(hid, wd_ref[0, 0].astype(BF16), preferred_element_type=F32))

    @pl.when(i >= nu_ref[0])
    def _():
        o_ref[...] = jnp.zeros(o_ref.shape, o_ref.dtype)


def grouped_experts(xs, w_gate, w_up, w_down, tile_expert, n_used, layer):
    wspec = lambda a, b: pl.BlockSpec((1, 1, a, b), lambda i, te, nu: (layer, te[i], 0, 0))
    return pl.pallas_call(
        _experts_kernel,
        grid_spec=pltpu.PrefetchScalarGridSpec(
            num_scalar_prefetch=2,
            grid=(MOE_TILES,),
            in_specs=[pl.BlockSpec((TM * ROW_GROUP, LANES), lambda i, te, nu: (jnp.minimum(i, nu[0] - 1), 0)),
                      wspec(D, MOE_F), wspec(D, MOE_F), wspec(MOE_F, D)],
            out_specs=pl.BlockSpec((TM * ROW_GROUP, LANES), lambda i, te, nu: (i, 0)),
            scratch_shapes=[pltpu.VMEM((TM, D), BF16)],
        ),
        out_shape=jax.ShapeDtypeStruct((N_SLOTS * ROW_GROUP, LANES), F32),
        compiler_params=_cparams(("arbitrary",)),
        name="moe_experts",
    )(tile_expert, n_used, xs, w_gate, w_up, w_down)


def _combine_kernel(cur_ref, nxt_ref, ys_hbm, x_ref, wts_ref, mod_ref, gfin_ref, *rest, final):
    *o_refs, buf, sem = rest
    i = pl.program_id(0)
    slot = i % 2

    @pl.when(i == 0)
    def _():
        _start_group_gather(ys_hbm, cur_ref, 2 * TM, buf.at[0], sem.at[0])

    @pl.when(i + 1 < N_TILES)
    def _():
        _start_group_gather(ys_hbm, nxt_ref, 2 * TM, buf.at[1 - slot], sem.at[1 - slot])

    _wait_group_gather(ys_hbm, buf.at[slot], sem.at[slot])
    w0, w1 = wts_ref[:, 0:1], wts_ref[:, 1:2]

    def finish(o_ref):
        for s in range(ROW_GROUP):
            cols = slice(s * LANES, (s + 1) * LANES)
            y0 = buf[slot, pl.ds(s, TM, stride=ROW_GROUP), :]
            y1 = buf[slot, pl.ds(TM * ROW_GROUP + s, TM, stride=ROW_GROUP), :]
            o_ref[:, cols] = x_ref[:, cols] + mod_ref[0, 5:6, cols] * (w0 * y0 + w1 * y1)
        if final:
            o_ref[...] = _rms(o_ref[...]) * gfin_ref[...]

    if final:
        pl.when(i < CTX_TILES)(lambda: finish(o_refs[0]))
        pl.when(i >= CTX_TILES)(lambda: finish(o_refs[1]))
    else:
        finish(o_refs[0])


def moe_combine(ys, dest, x, wts, mod, gfin, *, final):
    tile = lambda w: pl.BlockSpec((TM, w), lambda i: (i, 0))
    if final:
        out_specs = _pair_specs(D)
        out_shape = [jax.ShapeDtypeStruct((T_CTX, D), F32), jax.ShapeDtypeStruct((T_LAT, D), F32)]
    else:
        out_specs = [tile(D)]
        out_shape = [jax.ShapeDtypeStruct((T_ALL, D), F32)]
    idx = lambda f: pl.BlockSpec((1, 1, 2 * TM), lambda i: (f(i), 0, 0), memory_space=pltpu.SMEM)
    dest3 = dest.reshape(N_TILES, 1, 2 * TM)
    out = pl.pallas_call(
        functools.partial(_combine_kernel, final=final),
        grid=(N_TILES,),
        in_specs=[idx(lambda i: i), idx(lambda i: jnp.minimum(i + 1, N_TILES - 1)),
                  pl.BlockSpec(memory_space=pl.ANY), tile(D), tile(ROUTER_LANES),
                  pl.BlockSpec((1, 6, D), lambda i: (_mod_row(i), 0, 0)),
                  pl.BlockSpec((1, D), lambda i: (0, 0))],
        out_specs=out_specs,
        out_shape=out_shape,
        scratch_shapes=[pltpu.VMEM((2, 2 * TM * ROW_GROUP, LANES), F32), pltpu.SemaphoreType.DMA((2,))],
        compiler_params=_cparams(("arbitrary",)),
        name="moe_combine",
    )(dest3, dest3, ys, x, wts, mod, gfin)
    return out if final else out[0]


ODD_COLS = 2048
ROPE_Q = MLA_H * MLA_ROPE
ROPE_SHIFT = ROPE_F


def rope_tables():
    t = np.arange(L_LAT)
    pos = np.stack([t // GRID_W, t % GRID_W], axis=1).astype(np.float64)
    inv = 10000.0 ** (-np.arange(ROPE_F, dtype=np.float64) / ROPE_F)
    lane = np.arange(ROPE_Q) % MLA_ROPE
    axis = lane // (2 * ROPE_F)
    first = (lane % (2 * ROPE_F)) < ROPE_F
    ang = pos[:, axis] * inv[lane % ROPE_F][None, :]
    cos, sin = np.cos(ang), np.sin(ang)
    tabs = [cos, np.where(first[None, :], -sin, 0.0), np.where(first[None, :], 0.0, sin)]
    ident = [np.ones((1, TM, ROPE_Q)), np.zeros((1, TM, ROPE_Q)), np.zeros((1, TM, ROPE_Q))]
    return [jnp.asarray(np.concatenate([i, tb.reshape(LAT_TILES_PER_SEQ, TM, ROPE_Q)], axis=0).astype(np.float32))
            for i, tb in zip(ident, tabs)]


def _rope(x, c, a, b):
    n = x.shape[1]
    return x * c[:, :n] + pltpu.roll(x, n - ROPE_SHIFT, 1) * a[:, :n] + pltpu.roll(x, ROPE_SHIFT, 1) * b[:, :n]


def _odd_in_kernel(x_ref, mod_ref, g_ref, w_ref, gq_ref, wuq_ref, gkv_ref, wukv_ref, rc_ref, ra_ref, rb_ref,
                   qkv_ref, qm_ref, ckv_ref, kvu_ref, kr_ref, knew_ref, vnew_ref):
    hb = _modulated(x_ref[...], g_ref, mod_ref, 0).astype(BF16)
    for c0 in range(0, 3 * NA_W, PROJ_CHUNK):
        qkv_ref[:, c0:c0 + PROJ_CHUNK] = jnp.dot(hb, w_ref[:, c0:c0 + PROJ_CHUNK], preferred_element_type=F32)

    @pl.when(pl.program_id(0) < CTX_TILES)
    def _():
        for h in range(NA_H):
            knew_ref[0, h] = qkv_ref[:, NA_W + h * NA_D:NA_W + (h + 1) * NA_D]
            vnew_ref[0, h] = qkv_ref[:, 2 * NA_W + h * NA_D:2 * NA_W + (h + 1) * NA_D]

    rest = jnp.dot(hb, w_ref[:, 3 * NA_W:ODD_COLS], preferred_element_type=F32)
    rc, ra, rb = rc_ref[0], ra_ref[0], rb_ref[0]
    qd = (_rms(rest[:, 0:MLA_QR]) * gq_ref[...]).astype(BF16)
    qm = jnp.dot(qd, wuq_ref[...], preferred_element_type=F32)
    qm_ref[:, 0:MLA_H * MLA_NOPE] = qm[:, 0:MLA_H * MLA_NOPE]
    qm_ref[:, MLA_H * MLA_NOPE:] = _rope(qm[:, MLA_H * MLA_NOPE:], rc, ra, rb)
    ckv = _rms(rest[:, MLA_QR:MLA_QR + MLA_KVR]) * gkv_ref[...]
    ckv_ref[...] = ckv
    kvu_ref[...] = jnp.dot(ckv.astype(BF16), wukv_ref[...], preferred_element_type=F32)
    kr_ref[...] = _rope(rest[:, MLA_QR + MLA_KVR:], rc, ra, rb)


def odd_in_proj(x, mod, g, w_bf, gq, wuq_bf, gkv, wukv_bf, tabs):
    tile = lambda w: pl.BlockSpec((TM, w), lambda i: (i, 0))
    full = lambda arr: pl.BlockSpec(arr.shape, lambda i: (0,) * arr.ndim)
    tab = pl.BlockSpec((1, TM, ROPE_Q),
                       lambda i: (jnp.where(i < CTX_TILES, 0, 1 + (i - CTX_TILES) % LAT_TILES_PER_SEQ), 0, 0))
    widths = (3 * NA_W, MLA_H * MLA_QK, MLA_KVR, MLA_H * (MLA_NOPE + MLA_V), LANES)
    cache = pl.BlockSpec((1, NA_H, L_CTX, NA_D), lambda i: (jnp.minimum(i, CTX_TILES - 1), 0, 0, 0))
    cache_shape = jax.ShapeDtypeStruct((N_CTX, NA_H, L_CTX, NA_D), F32)
    return pl.pallas_call(
        _odd_in_kernel,
        grid=(N_TILES,),
        in_specs=[tile(D), pl.BlockSpec((1, 6, D), lambda i: (_mod_row(i), 0, 0)), full(g), full(w_bf),
                  full(gq), full(wuq_bf), full(gkv), full(wukv_bf), tab, tab, tab],
        out_specs=[tile(w) for w in widths] + [cache, cache],
        out_shape=[jax.ShapeDtypeStruct((T_ALL, w), F32) for w in widths] + [cache_shape, cache_shape],
        compiler_params=_cparams(("arbitrary",)),
        name="odd_in",
    )(x, mod, g, w_bf, gq, wuq_bf, gkv, wukv_bf, *tabs)


LOG2E = math.log2(math.e)
NA_QSCALE = NA_D ** -0.5 * LOG2E
MLA_QSCALE = MLA_QK ** -0.5 * LOG2E
NT = (((1,), (1,)), ((), ()))


def _softmax_pv(scores, values):
    m = functools.reduce(jnp.maximum, [jnp.max(s, axis=-1, keepdims=True) for s in scores])
    ps = [jnp.exp2(s - m) for s in scores]
    den = functools.reduce(jnp.add, [jnp.sum(p, axis=-1, keepdims=True) for p in ps])
    acc = functools.reduce(jnp.add, [jnp.dot(p.astype(BF16), v, preferred_element_type=F32) for p, v in zip(ps, values)])
    return acc / den


def _pair(ref_or_val, p, base=0):
    return ref_or_val[:, base + p * LANES:base + (p + 1) * LANES]


def _low_half():
    return lax.broadcasted_iota(jnp.int32, (1, LANES), 1) < NA_D


def _rope_key_forms(kr):
    return kr.astype(BF16), pltpu.roll(kr, LANES // 2, 1).astype(BF16)


def _mla_pair(qm, p, sources, lo):
    outs = []
    for e in range(2):
        h = 2 * p + e
        qn = qm[:, h * MLA_NOPE:(h + 1) * MLA_NOPE] * MLA_QSCALE
        qr = qm[:, MLA_H * MLA_NOPE + h * MLA_ROPE:MLA_H * MLA_NOPE + (h + 1) * MLA_ROPE] * MLA_QSCALE
        z = jnp.zeros((qn.shape[0], LANES - MLA_QK), F32)
        qcat = jnp.concatenate([qn, qr, z] if e == 0 else [qr, z, qn], axis=1).astype(BF16)
        scores = []
        for kb, kr_lo, kr_hi, _ in sources:
            kcat = jnp.where(lo, kb, kr_hi) if e == 0 else jnp.where(lo, kr_lo, kb)
            scores.append(lax.dot_general(qcat, kcat, NT, preferred_element_type=F32))
        outs.append(_softmax_pv(scores, [src[3] for src in sources]))
    return jnp.where(lo, outs[0], outs[1])


def _attn_ctx_kernel(qkv_ref, qm_ref, kvu_ref, kr_ref, ona_ref, omla_ref):
    lo = _low_half()
    for p in range(NA_H // 2):
        qb = _pair(qkv_ref, p) * NA_QSCALE
        kb = _pair(qkv_ref, p, NA_W).astype(BF16)
        vb = _pair(qkv_ref, p, 2 * NA_W).astype(BF16)
        outs = []
        for e in range(2):
            q = jnp.where(lo if e == 0 else jnp.logical_not(lo), qb, 0.0).astype(BF16)
            outs.append(_softmax_pv([lax.dot_general(q, kb, NT, preferred_element_type=F32)], [vb]))
        ona_ref[:, p * LANES:(p + 1) * LANES] = jnp.where(lo, outs[0], outs[1]).astype(ona_ref.dtype)
    kr_lo, kr_hi = _rope_key_forms(kr_ref[...])
    for p in range(MLA_H // 2):
        src = (_pair(kvu_ref, p).astype(BF16), kr_lo, kr_hi, _pair(kvu_ref, p, MLA_H * MLA_NOPE).astype(BF16))
        omla_ref[:, p * LANES:(p + 1) * LANES] = _mla_pair(qm_ref, p, [src], lo).astype(omla_ref.dtype)


def attn_context(qkv, qm, kvu, kr):
    seq = lambda w: pl.BlockSpec((L_CTX, w), lambda b: (b, 0))
    return pl.pallas_call(
        _attn_ctx_kernel,
        grid=(N_CTX,),
        in_specs=[seq(3 * NA_W), seq(MLA_H * MLA_QK), seq(MLA_H * (MLA_NOPE + MLA_V)), seq(LANES)],
        out_specs=[seq(NA_W), seq(MLA_H * MLA_V)],
        out_shape=[jax.ShapeDtypeStruct((T_CTX, NA_W), BF16), jax.ShapeDtypeStruct((T_CTX, MLA_H * MLA_V), BF16)],
        compiler_params=_cparams(("arbitrary",)),
        name="attn_ctx",
    )(qkv, qm, kvu, kr)


N_DR = 2 * NA_WIN_R - 1
GRID_ROWS = L_LAT // GRID_W


def _na_bias_kernel(t_ref, o_ref):
    neg = jnp.full((GRID_W, GRID_W), NEG, F32)
    for r in range(GRID_ROWS):
        r0 = min(max(r - NA_WIN_R // 2, 0), GRID_ROWS - NA_WIN_R)
        for kr in range(GRID_ROWS):
            in_window = r0 <= kr < r0 + NA_WIN_R
            blk = t_ref[0, kr - r + NA_WIN_R - 1] if in_window else neg
            o_ref[0, r * GRID_W:(r + 1) * GRID_W, kr * GRID_W:(kr + 1) * GRID_W] = blk


def neighbourhood_bias(rel_bias):
    c = np.arange(GRID_W)
    c0 = np.clip(c - NA_WIN_C // 2, 0, GRID_W - NA_WIN_C)
    col_ok = (c[None, :] >= c0[:, None]) & (c[None, :] < c0[:, None] + NA_WIN_C)
    dc = np.clip(c[None, :] - c[:, None], -(NA_WIN_C - 1), NA_WIN_C - 1) + NA_WIN_C - 1
    sel_c = (dc[:, :, None] == np.arange(2 * NA_WIN_C - 1)).astype(np.float32)
    t = jnp.einsum("hdj,qcj->hdqc", rel_bias.astype(F32), jnp.asarray(sel_c), precision=HIGHEST)
    t = jnp.where(jnp.asarray(col_ok)[None, None], t * LOG2E, NEG)
    return pl.pallas_call(
        _na_bias_kernel,
        grid=(NA_H,),
        in_specs=[pl.BlockSpec((1, N_DR, GRID_W, GRID_W), lambda h: (h, 0, 0, 0))],
        out_specs=pl.BlockSpec((1, L_LAT, L_LAT), lambda h: (h, 0, 0)),
        out_shape=jax.ShapeDtypeStruct((NA_H, L_LAT, L_LAT), F32),
        compiler_params=_cparams(("arbitrary",)),
        name="na_bias",
    )(t)


def _na_lat_kernel(q_ref, k_ref, v_ref, kc_ref, vc_ref, b_ref, o_ref):
    lo = _low_half()
    for p in range(NA_H // 2):
        qb = _pair(q_ref, p) * NA_QSCALE
        kb = _pair(k_ref, p).astype(BF16)
        vb = _pair(v_ref, p).astype(BF16)
        outs = []
        for e in range(2):
            h = 2 * p + e
            half = slice(e * NA_D, (e + 1) * NA_D)
            q = jnp.where(lo if e == 0 else jnp.logical_not(lo), qb, 0.0).astype(BF16)
            s1 = lax.dot_general(q, kb, NT, preferred_element_type=F32) + b_ref[h]
            s2 = lax.dot_general(qb[:, half].astype(BF16), kc_ref[0, 0, h].astype(BF16), NT, preferred_element_type=F32)
            m = jnp.maximum(jnp.max(s1, axis=-1, keepdims=True), jnp.max(s2, axis=-1, keepdims=True))
            p1, p2 = jnp.exp2(s1 - m), jnp.exp2(s2 - m)
            den = jnp.sum(p1, axis=-1, keepdims=True) + jnp.sum(p2, axis=-1, keepdims=True)
            a1 = jnp.dot(p1.astype(BF16), vb, preferred_element_type=F32)
            a2 = jnp.dot(p2.astype(BF16), vc_ref[0, 0, h].astype(BF16), preferred_element_type=F32)
            outs.append((a1[:, half] + a2) / den)
        o_ref[:, p * LANES:(p + 1) * LANES] = jnp.concatenate(outs, axis=1).astype(o_ref.dtype)


def attn_neighbourhood_latent(qkv, cache_k, cache_v, bias):
    nq = L_LAT // TM
    t0 = T_CTX // TM
    s0 = T_CTX // L_LAT
    cache = pl.BlockSpec((1, 1, NA_H, PAST, NA_D), lambda qt, b: (b, 0, 0, 0, 0))
    return pl.pallas_call(
        _na_lat_kernel,
        grid=(nq, N_LAT),
        in_specs=[pl.BlockSpec((TM, NA_W), lambda qt, b: (t0 + b * nq + qt, 0)),
                  pl.BlockSpec((L_LAT, NA_W), lambda qt, b: (s0 + b, 1)),
                  pl.BlockSpec((L_LAT, NA_W), lambda qt, b: (s0 + b, 2)),
                  cache, cache,
                  pl.BlockSpec((NA_H, TM, L_LAT), lambda qt, b: (0, qt, 0))],
        out_specs=pl.BlockSpec((TM, NA_W), lambda qt, b: (b * nq + qt, 0)),
        out_shape=jax.ShapeDtypeStruct((T_LAT, NA_W), BF16),
        compiler_params=_cparams(("arbitrary", "arbitrary")),
        name="attn_na_lat",
    )(qkv, qkv, qkv, cache_k, cache_v, bias)


def _mla_lat_kernel(qm_ref, kvu_ref, kr_ref, ckv_ref, krc_ref, wukv_ref, o_ref):
    lo = _low_half()
    kvc = jnp.dot(ckv_ref[0, 0].astype(BF16), wukv_ref[...], preferred_element_type=F32)
    kr_lo, kr_hi = _rope_key_forms(kr_ref[...])
    krc = jnp.concatenate([krc_ref[0, 0], jnp.zeros((PAST, LANES - MLA_ROPE), F32)], axis=1)
    krc_lo, krc_hi = _rope_key_forms(krc)
    vbase = MLA_H * MLA_NOPE
    for p in range(MLA_H // 2):
        lat = (_pair(kvu_ref, p).astype(BF16), kr_lo, kr_hi, _pair(kvu_ref, p, vbase).astype(BF16))
        ctx = (_pair(kvc, p).astype(BF16), krc_lo, krc_hi, _pair(kvc, p, vbase).astype(BF16))
        o_ref[:, p * LANES:(p + 1) * LANES] = _mla_pair(qm_ref, p, [lat, ctx], lo).astype(o_ref.dtype)


def attn_mla_latent(qm, kvu, kr, cache_ckv, cache_krope, wukv_bf):
    nq = L_LAT // TM
    t0 = T_CTX // TM
    s0 = T_CTX // L_LAT
    return pl.pallas_call(
        _mla_lat_kernel,
        grid=(nq, N_LAT),
        in_specs=[pl.BlockSpec((TM, MLA_H * MLA_QK), lambda qt, b: (t0 + b * nq + qt, 0)),
                  pl.BlockSpec((L_LAT, MLA_H * (MLA_NOPE + MLA_V)), lambda qt, b: (s0 + b, 0)),
                  pl.BlockSpec((L_LAT, LANES), lambda qt, b: (s0 + b, 0)),
                  pl.BlockSpec((1, 1, PAST, MLA_KVR), lambda qt, b: (b, 0, 0, 0)),
                  pl.BlockSpec((1, 1, PAST, MLA_ROPE), lambda qt, b: (b, 0, 0, 0)),
                  pl.BlockSpec(wukv_bf.shape, lambda qt, b: (0, 0))],
        out_specs=pl.BlockSpec((TM, MLA_H * MLA_V), lambda qt, b: (b * nq + qt, 0)),
        out_shape=jax.ShapeDtypeStruct((T_LAT, MLA_H * MLA_V), BF16),
        compiler_params=_cparams(("arbitrary", "arbitrary")),
        name="attn_mla_lat",
    )(qm, kvu, kr, cache_ckv, cache_krope, wukv_bf)


def moe_block(h2, ids, wts, cnt3, x, mod, gfin, w_gate, w_up, w_down, layer, *, final):
    cnt, loc, gdst, ends, tile_expert, n_used = route_tables(cnt3)
    xs, dest = dispatch_rows(h2, ids, cnt, loc, gdst, ends)
    ys = grouped_experts(xs, w_gate, w_up, w_down, tile_expert, n_used, layer)
    return moe_combine(ys, dest, x, wts, mod, gfin, final=final)


def _pad_lanes(a):
    return jnp.pad(a, ((0, 0), (0, LANES - a.shape[1])))


def _hyena_features(L):
    t = np.linspace(0.0, 1.0, L)[:, None]
    w = 2.0 * math.pi * np.arange(L) / L
    bands = np.linspace(1e-4, HY_BANDS - 1, HY_BANDS)
    ang = w[:, None] * bands[None]
    feat = np.concatenate([t, np.cos(ang), -np.sin(ang)], axis=-1)
    return jnp.asarray(np.pad(feat, ((0, 0), (0, LANES - HY_FEAT))).astype(np.float32))


def _router_params(w_gr, b_gr, w_er, b_er):
    wr = _pad_lanes(jnp.concatenate([w_gr, w_er], axis=1))
    br = _pad_lanes(jnp.concatenate([b_gr, b_er])[None])
    wr_hi = wr.astype(BF16)
    wr_lo = (wr - wr_hi.astype(F32)).astype(BF16)
    return jnp.stack([wr_hi, wr_lo]), br


def _even_layer(x, mod, g_mix, state, w_in, conv_w, conv_b, a_log, dt_bias, d_skip, g_ssd, hy_conv_w, hy_conv_b,
                hy_w1, hy_b1, hy_w2, hy_b2, hy_w3, hy_freq, hy_bias):
    n0 = D + SSD_XBC
    w_bf = jnp.concatenate([w_in[:, :n0], w_in[:, n0 + SSD_H:], w_in[:, n0:n0 + SSD_H],
                            jnp.zeros((D, LANES - SSD_H), F32)], axis=1).astype(BF16)
    z, xbc, hy, dtr = even_in_proj(x, mod, g_mix, w_bf)
    small = (conv_w, conv_b[None], _pad_lanes(dt_bias), _pad_lanes(a_log), jnp.repeat(d_skip, SSD_P)[None], g_ssd[None])
    y_c, fin = ssd_mixer(xbc, dtr, z, None, *small, L=L_CTX, n_seq=N_CTX, row_off=0)
    (y_l,) = ssd_mixer(xbc, dtr, z, state.reshape(N_LAT, 2, SSD_H * SSD_P, SSD_N), *small,
                       L=L_LAT, n_seq=N_LAT, row_off=T_CTX)
    w1 = jnp.pad(hy_w1, ((0, LANES - HY_FEAT), (0, 0)))
    w3r = hy_w3.reshape(HY_HID, 4, D).transpose(1, 0, 2)
    deltas = jnp.asarray(np.linspace(HY_MIN_DECAY, HY_MAX_DECAY, D).astype(np.float32))[None]
    us = []
    for L, n_seq, off in ((L_CTX, N_CTX, 0), (L_LAT, N_LAT, T_CTX)):
        h4, hm = hyena_filter_spectra(_hyena_features(L), w1, hy_b1[None], hy_w2, hy_b2[None], hy_freq, w3r, deltas, L=L)
        us.append(hyena_mixer(hy, hy_conv_w, hy_conv_b[None], h4, hm, hy_bias, L=L, n_seq=n_seq, row_off=off))
    return (y_c, y_l), tuple(us), fin


def _odd_layer(x, mod, g_mix, cache_k, cache_v, cache_ckv, cache_kr, rel_bias, w_in, g_q, w_uq, g_kv, w_ukv):
    w_bf = jnp.pad(w_in, ((0, 0), (0, ODD_COLS - w_in.shape[1]))).astype(BF16)
    wuq = w_uq.reshape(MLA_QR, MLA_H, MLA_QK)
    wuq_bf = jnp.concatenate([wuq[:, :, :MLA_NOPE].reshape(MLA_QR, -1), wuq[:, :, MLA_NOPE:].reshape(MLA_QR, -1)],
                             axis=1).astype(BF16)
    wukv = w_ukv.reshape(MLA_KVR, MLA_H, MLA_NOPE + MLA_V)
    wukv_bf = jnp.concatenate([wukv[:, :, :MLA_NOPE].reshape(MLA_KVR, -1), wukv[:, :, MLA_NOPE:].reshape(MLA_KVR, -1)],
                              axis=1).astype(BF16)
    qkv, qm, ckv, kvu, kr, k_new, v_new = odd_in_proj(x, mod, g_mix, w_bf, g_q[None], wuq_bf, g_kv[None], wukv_bf,
                                                      rope_tables())
    ona_c, omla_c = attn_context(qkv, qm, kvu, kr)
    ona_l = attn_neighbourhood_latent(qkv, cache_k, cache_v, neighbourhood_bias(rel_bias))
    omla_l = attn_mla_latent(qm, kvu, kr, cache_ckv, cache_kr, wukv_bf)
    return (ona_c, ona_l), (omla_c, omla_l), k_new, v_new, ckv, kr


def kernel(x_prompt, x_sample, state_ssd, cache_na_k, cache_na_v, cache_mla_ckv, cache_mla_krope, c, c_ctx, w_ada, b_ada, norm_mix, norm_ffn, norm_final, ev_w_in, ev_conv_w, ev_conv_b, ssd_A_log, ssd_dt_bias, ssd_d, ssd_norm, hy_conv_w, hy_conv_b, hy_w1, hy_b1, hy_w2, hy_b2, hy_w3, hy_freq, hy_bias, ev_w_out, od_w_in, mla_q_norm, mla_w_uq, mla_kv_norm, mla_w_ukv, na_rel_bias, od_w_out, moe_w_gr, moe_b_gr, moe_w_er, moe_b_er, moe_w_gate, moe_w_up, moe_w_down):
    x = (x_prompt.reshape(T_CTX, D), x_sample.reshape(T_LAT, D))
    cvec = jnp.zeros((MOD_ROWS, D), F32).at[0].set(c_ctx).at[1:1 + N_LAT].set(c)
    mod = ada_modulation(cvec, w_ada, b_ada)
    gfin = norm_final[None]

    y, u, fin = _even_layer(x, mod[0], norm_mix[0][None], state_ssd[:, 0], ev_w_in[0], ev_conv_w[0], ev_conv_b[0],
                            ssd_A_log[0], ssd_dt_bias[0], ssd_d[0], ssd_norm[0], hy_conv_w[0], hy_conv_b[0],
                            hy_w1[0], hy_b1[0], hy_w2[0], hy_b2[0], hy_w3[0], hy_freq[0], hy_bias[0])
    wr, br = _router_params(moe_w_gr[0], moe_b_gr[0], moe_w_er[0], moe_b_er[0])
    xn, h2, ids, wts, cnt3 = out_proj_router([y, u], ev_w_out[0].astype(BF16), x, mod[0], norm_ffn[0][None], wr, br)
    x = moe_block(h2, ids, wts, cnt3, xn, mod[0], gfin, moe_w_gate, moe_w_up, moe_w_down, 0, final=False)

    o_na, o_mla, k_new, v_new, ckv, kr = _odd_layer(x, mod[1], norm_mix[1][None], cache_na_k, cache_na_v, cache_mla_ckv,
                                           cache_mla_krope, na_rel_bias[0], od_w_in[0], mla_q_norm[0], mla_w_uq[0],
                                           mla_kv_norm[0], mla_w_ukv[0])
    wr, br = _router_params(moe_w_gr[1], moe_b_gr[1], moe_w_er[1], moe_b_er[1])
    xn, h2, ids, wts, cnt3 = out_proj_router([o_na, o_mla], od_w_out[0].astype(BF16), x, mod[1], norm_ffn[1][None], wr, br)
    y_c, y_l = moe_block(h2, ids, wts, cnt3, xn, mod[1], gfin, moe_w_gate, moe_w_up, moe_w_down, 1, final=True)

    return (y_c.reshape(N_CTX, L_CTX, D),
            y_l.reshape(N_LAT, L_LAT, D),
            fin.reshape(N_CTX, 1, 2, SSD_H, SSD_P, SSD_N),
            k_new[:, None],
            v_new[:, None],
            ckv[:T_CTX].reshape(N_CTX, 1, L_CTX, MLA_KVR),
            kr[:T_CTX, :MLA_ROPE].reshape(N_CTX, 1, L_CTX, MLA_ROPE))
```

```python
import functools
import math

import numpy as np
import jax
import jax.numpy as jnp
from jax import lax
from jax.experimental import pallas as pl
from jax.experimental.pallas import tpu as pltpu

F32 = jnp.float32
BF16 = jnp.bfloat16
HIGHEST = lax.Precision.HIGHEST

D = 1024
N_CTX, L_CTX = 16, 256
N_LAT, L_LAT = 8, 1024
T_CTX = N_CTX * L_CTX
T_LAT = N_LAT * L_LAT
T_ALL = T_CTX + T_LAT
PAST = 512
GRID_W = 64
EPS = 1e-6
NEG = -1e30

SSD_H, SSD_P, SSD_N, SSD_G = 16, 64, 128, 2
SSD_XBC = D + 2 * SSD_G * SSD_N
SSD_K = 5
CHUNK = 128

HY_K = 3
HY_BANDS = 16
HY_FEAT = 1 + 2 * HY_BANDS
HY_HID = 64
HY_MIN_DECAY = abs(math.log(1e-2) / 1.5)
HY_MAX_DECAY = abs(math.log(1e-2) / 0.3)

NA_H, NA_D = 8, 64
NA_W = NA_H * NA_D
NA_WIN_R, NA_WIN_C = 8, 16
MLA_H, MLA_QR, MLA_KVR = 8, 256, 128
MLA_NOPE, MLA_ROPE, MLA_V = 64, 32, 64
MLA_QK = MLA_NOPE + MLA_ROPE
ROPE_F = MLA_ROPE // 4

MOE_G, MOE_PG, MOE_E, MOE_F = 4, 8, 32, 256

LANES = 128
SUBLANES = 8
VMEM_LIMIT = 56 * 1024 * 1024

TM = 256
N_TILES = T_ALL // TM
CTX_TILES = T_CTX // TM
LAT_TILES_PER_SEQ = L_LAT // TM
MOD_ROWS = 16


def _cparams(sem):
    return pltpu.CompilerParams(dimension_semantics=sem, vmem_limit_bytes=VMEM_LIMIT)


def _mod_row(i):
    return jnp.where(i < CTX_TILES, 0, 1 + (i - CTX_TILES) // LAT_TILES_PER_SEQ)


def _silu(x):
    return x * jax.nn.sigmoid(x)


def _rms(x):
    return x * lax.rsqrt(jnp.mean(x * x, axis=-1, keepdims=True) + EPS)


def _ada_kernel(c_ref, w_ref, b_ref, o_ref):
    c = c_ref[...]
    o_ref[0] = jnp.dot(_silu(c), w_ref[0], precision=HIGHEST, preferred_element_type=F32) + b_ref[0]


def ada_modulation(cvec, w_ada, b_ada):
    depth = w_ada.shape[0]
    out = pl.pallas_call(
        _ada_kernel,
        grid=(depth, 6),
        in_specs=[
            pl.BlockSpec((MOD_ROWS, D), lambda l, j: (0, 0)),
            pl.BlockSpec((1, D, D), lambda l, j: (l, 0, j)),
            pl.BlockSpec((1, 1, D), lambda l, j: (l, 0, j)),
        ],
        out_specs=pl.BlockSpec((1, MOD_ROWS, D), lambda l, j: (l, 0, j)),
        out_shape=jax.ShapeDtypeStruct((depth, MOD_ROWS, 6 * D), F32),
        compiler_params=_cparams(("arbitrary", "arbitrary")),
        name="ada",
    )(cvec, w_ada, b_ada.reshape(depth, 1, 6 * D))
    return out.reshape(depth, MOD_ROWS, 6, D)


PROJ_CHUNK = 512


def _modulated(x, g_ref, mod_ref, shift_row):
    h = _rms(x) * g_ref[...]
    return h * (1.0 + mod_ref[0, shift_row + 1:shift_row + 2, :]) + mod_ref[0, shift_row:shift_row + 1, :]


IN_TM = 512
IN_CTX_TILES = T_CTX // IN_TM


def _even_in_kernel(xc_ref, xl_ref, mod_ref, g_ref, w_ref, z_ref, xbc_ref, hy_ref, dt_ref):
    x = jnp.where(pl.program_id(0) < IN_CTX_TILES, xc_ref[...], xl_ref[...])
    hb = _modulated(x, g_ref, mod_ref, 0).astype(BF16)
    col = 0
    for o_ref in (z_ref, xbc_ref, hy_ref, dt_ref):
        width = o_ref.shape[1]
        for c0 in range(0, width, PROJ_CHUNK):
            c1 = min(c0 + PROJ_CHUNK, width)
            o_ref[:, c0:c1] = jnp.dot(hb, w_ref[:, col + c0:col + c1], preferred_element_type=F32)
        col += width


def _pair_specs(width):
    return [pl.BlockSpec((TM, width), lambda i: (jnp.minimum(i, CTX_TILES - 1), 0)),
            pl.BlockSpec((TM, width), lambda i: (jnp.maximum(i - CTX_TILES, 0), 0))]


def even_in_proj(x_pair, mod, g, w_bf):
    widths = (D, SSD_XBC, 3 * D, LANES)
    mod_row = lambda i: jnp.where(i < IN_CTX_TILES, 0, 1 + (i - IN_CTX_TILES) // (L_LAT // IN_TM))
    return pl.pallas_call(
        _even_in_kernel,
        grid=(T_ALL // IN_TM,),
        in_specs=[
            pl.BlockSpec((IN_TM, D), lambda i: (jnp.minimum(i, IN_CTX_TILES - 1), 0)),
            pl.BlockSpec((IN_TM, D), lambda i: (jnp.maximum(i - IN_CTX_TILES, 0), 0)),
            pl.BlockSpec((1, 6, D), lambda i: (mod_row(i), 0, 0)),
            pl.BlockSpec((1, D), lambda i: (0, 0)),
            _const_spec(w_bf),
        ],
        out_specs=[pl.BlockSpec((IN_TM, w), lambda i: (i, 0)) for w in widths],
        out_shape=[jax.ShapeDtypeStruct((T_ALL, w), F32) for w in widths],
        compiler_params=_cparams(("arbitrary",)),
        name="even_in",
    )(*x_pair, mod, g, w_bf)


PAD = SUBLANES


def _ssd_kernel(*refs, L, has_init):
    if has_init:
        (xbc_ref, dt_ref, z_ref, init_ref, cw_ref, cb_ref, dtb_ref, alog_ref, dsk_ref, gs_ref,
         y_ref, xp_s, xc_s, ya_s, st_s) = refs
        fin_ref = None
    else:
        (xbc_ref, dt_ref, z_ref, cw_ref, cb_ref, dtb_ref, alog_ref, dsk_ref, gs_ref,
         y_ref, fin_ref, xp_s, xc_s, ya_s, st_s) = refs
        init_ref = None
    nc = L // CHUNK
    half = SSD_K // 2

    xp_s[0:PAD, :] = jnp.zeros((PAD, SSD_XBC), F32)
    xp_s[PAD + L:2 * PAD + L, :] = jnp.zeros((PAD, SSD_XBC), F32)
    xp_s[PAD:PAD + L, :] = xbc_ref[...]
    for c in range(nc):
        base = PAD + c * CHUNK - half
        for j in range(SSD_XBC // LANES):
            cols = slice(j * LANES, (j + 1) * LANES)
            acc = cb_ref[:, cols] + xp_s[base:base + CHUNK, cols] * cw_ref[0:1, cols]
            for k in range(1, SSD_K):
                acc = acc + xp_s[base + k:base + k + CHUNK, cols] * cw_ref[k:k + 1, cols]
            xc_s[c * CHUNK:(c + 1) * CHUNK, cols] = _silu(acc)

    row = lax.broadcasted_iota(jnp.int32, (CHUNK, CHUNK), 0)
    colm = lax.broadcasted_iota(jnp.int32, (CHUNK, CHUNK), 1)
    lane_lo = colm < SSD_P
    tri_lo = (colm <= row).astype(F32)
    tri_up = (colm >= row).astype(F32)

    for d in range(2):
        causal = (colm <= row) if d == 0 else (colm >= row)
        for j in range(SSD_H * SSD_P // CHUNK):
            if has_init:
                st_s[:, j * CHUNK:(j + 1) * CHUNK] = init_ref[0, d, j * CHUNK:(j + 1) * CHUNK, :].T
            else:
                st_s[:, j * CHUNK:(j + 1) * CHUNK] = jnp.zeros((CHUNK, CHUNK), F32)

        def chunk_body(ci, carry, d=d, causal=causal):
            c = ci if d == 0 else nc - 1 - ci
            r0 = pl.multiple_of(c * CHUNK, CHUNK)
            dt = jax.nn.softplus(dt_ref[pl.ds(r0, CHUNK), :] + dtb_ref[d:d + 1, :])
            a = dt * (-jnp.exp(alog_ref[d:d + 1, :]))
            tri = tri_lo if d == 0 else tri_up
            cs = jnp.dot(tri, a, precision=HIGHEST, preferred_element_type=F32)
            cs_t = jnp.dot(a.T, tri.T, precision=HIGHEST, preferred_element_type=F32)
            edge = cs[CHUNK - 1:CHUNK, :] if d == 0 else cs[0:1, :]
            ecs = jnp.exp(cs)
            dec = jnp.exp(edge - cs)
            cdec = jnp.exp(edge)
            for g in range(SSD_G):
                bm = xc_s[pl.ds(r0, CHUNK), D + g * SSD_N:D + (g + 1) * SSD_N]
                cm = xc_s[pl.ds(r0, CHUNK), D + (SSD_G + g) * SSD_N:D + (SSD_G + g + 1) * SSD_N]
                bm_b, cm_b = bm.astype(BF16), cm.astype(BF16)
                cb = lax.dot_general(cm_b, bm_b, (((1,), (1,)), ((), ())), preferred_element_type=F32)
                bm_t = bm.T.astype(BF16)
                pairs = SSD_H // SSD_G // 2
                for pp in range(pairs):
                    p = g * pairs + pp
                    h0, h1 = 2 * p, 2 * p + 1
                    cols = slice(p * CHUNK, (p + 1) * CHUNK)
                    xs = xc_s[pl.ds(r0, CHUNK), cols]
                    xdt = xs * jnp.where(lane_lo, dt[:, h0:h0 + 1], dt[:, h1:h1 + 1])
                    ms = []
                    for h in (h0, h1):
                        diff = cs[:, h:h + 1] - cs_t[h:h + 1, :]
                        ms.append(cb * jnp.exp(jnp.where(causal, diff, NEG)))
                    mcat = jnp.concatenate(ms, axis=1).astype(BF16)
                    xbd = jnp.concatenate([jnp.where(lane_lo, xdt, 0.0), jnp.where(lane_lo, 0.0, xdt)],
                                          axis=0).astype(BF16)
                    y_diag = jnp.dot(mcat, xbd, preferred_element_type=F32)
                    st = st_s[:, cols]
                    y_off = jnp.dot(cm_b, st.astype(BF16), preferred_element_type=F32)
                    y_off = y_off * jnp.where(lane_lo, ecs[:, h0:h0 + 1], ecs[:, h1:h1 + 1])
                    y = y_diag + y_off
                    if d == 0:
                        ya_s[pl.ds(r0, CHUNK), cols] = y
                    else:
                        ya_s[pl.ds(r0, CHUNK), cols] = ya_s[pl.ds(r0, CHUNK), cols] + y
                    xdd = (xdt * jnp.where(lane_lo, dec[:, h0:h0 + 1], dec[:, h1:h1 + 1])).astype(BF16)
                    snew = jnp.dot(bm_t, xdd, preferred_element_type=F32)
                    st_s[:, cols] = st * jnp.where(lane_lo[0:1, :], cdec[:, h0:h0 + 1], cdec[:, h1:h1 + 1]) + snew
            return carry

        lax.fori_loop(0, nc, chunk_body, 0)
        if fin_ref is not None:
            for j in range(SSD_H * SSD_P // CHUNK):
                fin_ref[0, d, j * CHUNK:(j + 1) * CHUNK, :] = st_s[:, j * CHUNK:(j + 1) * CHUNK].T

    def out_body(c, carry):
        r0 = pl.multiple_of(c * CHUNK, CHUNK)
        y = ya_s[pl.ds(r0, CHUNK), :] + xc_s[pl.ds(r0, CHUNK), 0:D] * dsk_ref[...]
        y = y * _silu(z_ref[pl.ds(r0, CHUNK), :])
        y_ref[pl.ds(r0, CHUNK), :] = (_rms(y) * gs_ref[...]).astype(y_ref.dtype)
        return carry

    lax.fori_loop(0, nc, out_body, 0)


def ssd_mixer(xbc, dtr, z, init, cw, cb, dtb, alog, dsk, gs, *, L, n_seq, row_off):
    blk0 = row_off // L
    has_init = init is not None
    seq = lambda w: pl.BlockSpec((L, w), lambda b: (blk0 + b, 0))
    full = lambda arr: pl.BlockSpec(arr.shape, lambda b: (0,) * arr.ndim)
    in_specs = [seq(SSD_XBC), seq(LANES), seq(D)]
    args = [xbc, dtr, z]
    if has_init:
        in_specs.append(pl.BlockSpec((1, 2, SSD_H * SSD_P, SSD_N), lambda b: (b, 0, 0, 0)))
        args.append(init)
    small = [cw, cb, dtb, alog, dsk, gs]
    in_specs += [full(a) for a in small]
    args += small
    out_specs = [pl.BlockSpec((L, D), lambda b: (b, 0))]
    out_shape = [jax.ShapeDtypeStruct((n_seq * L, D), BF16)]
    if not has_init:
        out_specs.append(pl.BlockSpec((1, 2, SSD_H * SSD_P, SSD_N), lambda b: (b, 0, 0, 0)))
        out_shape.append(jax.ShapeDtypeStruct((n_seq, 2, SSD_H * SSD_P, SSD_N), F32))
    return pl.pallas_call(
        functools.partial(_ssd_kernel, L=L, has_init=has_init),
        grid=(n_seq,),
        in_specs=in_specs,
        out_specs=out_specs,
        out_shape=out_shape,
        scratch_shapes=[
            pltpu.VMEM((L + 2 * PAD, SSD_XBC), F32),
            pltpu.VMEM((L, SSD_XBC), F32),
            pltpu.VMEM((L, D), F32),
            pltpu.VMEM((SSD_N, SSD_H * SSD_P), F32),
        ],
        compiler_params=_cparams(("arbitrary",)),
        name=f"ssd_{L}",
    )(*args)


HY_CB = 256


def filter_dft_matrices(L):
    H = L // 2
    s = np.arange(L, dtype=np.int64)[None, :]
    k = np.arange(H, dtype=np.int64)[:, None]
    ang = lambda kk: ((kk * s) % (2 * L)).astype(np.float64) * (math.pi / L)
    ca, cb = np.cos(ang(k)), np.cos(ang(L - k))
    sa, sb = np.sin(ang(k)), np.sin(ang(L - k))
    cb[0] = np.where(s[0] % 2 == 0, 1.0, -1.0)
    sa[0], sb[0] = 0.0, 0.0
    fm = np.zeros((2 * SUBLANES, L))
    fm[0], fm[1] = np.cos(ang(H))[0], np.sin(ang(H))[0]
    mats = (np.concatenate([ca, cb], axis=0), np.concatenate([sa, sb], axis=0), fm)
    return tuple(jnp.asarray(m.astype(np.float32)).astype(BF16) for m in mats)


def _const_spec(arr):
    return pl.BlockSpec(arr.shape, lambda *_: (0,) * arr.ndim, pipeline_mode=pl.Buffered(1))


def _hy_filter_kernel(feat_ref, w1_ref, b1_ref, w2_ref, b2_ref, fr_ref, w3_ref, dl_ref, fs_ref, fd_ref, fm_ref,
                      h_ref, hm_ref, *, L):
    H = L // 2
    hp = functools.partial(jnp.dot, precision=HIGHEST, preferred_element_type=F32)
    hdn = jnp.sin(fr_ref[0:1, :] * (hp(feat_ref[...], w1_ref[...]) + b1_ref[...]))
    hdn = jnp.sin(fr_ref[1:2, :] * (hp(hdn, w2_ref[...]) + b2_ref[...]))
    rowi = lax.broadcasted_iota(jnp.int32, (L, 1), 0)
    t = rowi.astype(F32) * (1.0 / (L - 1))
    dec = jnp.exp(-t * dl_ref[...])
    first = rowi == 0
    for o in range(2):
        fwd = hp(hdn, w3_ref[2 * o]) * dec
        bwd = jnp.where(first, 0.0, hp(hdn, w3_ref[2 * o + 1]) * dec)
        hs, hd = (fwd + bwd).astype(BF16), (fwd - bwd).astype(BF16)
        ss = jnp.dot(fs_ref[...], hs, preferred_element_type=F32)
        sd = jnp.dot(fd_ref[...], hd, preferred_element_type=F32)
        h_ref[o, 0] = ss[0:H]
        h_ref[o, 1] = sd[0:H]
        h_ref[o, 2] = ss[H:L]
        h_ref[o, 3] = sd[H:L]
        mid_r = jnp.dot(fm_ref[...], hs, preferred_element_type=F32)
        mid_n = jnp.dot(fm_ref[...], hd, preferred_element_type=F32)
        hm_ref[o] = jnp.concatenate([mid_r[0:1], mid_n[1:2], jnp.zeros((SUBLANES - 2, mid_r.shape[1]), F32)], axis=0)


def hyena_filter_spectra(feat, w1, b1, w2, b2, freq, w3r, deltas, *, L):
    full = lambda arr: pl.BlockSpec(arr.shape, lambda j: (0,) * arr.ndim)
    mats = filter_dft_matrices(L)
    return pl.pallas_call(
        functools.partial(_hy_filter_kernel, L=L),
        grid=(D // HY_CB,),
        in_specs=[full(feat), full(w1), full(b1), full(w2), full(b2), full(freq),
                  pl.BlockSpec((4, HY_HID, HY_CB), lambda j: (0, 0, j)),
                  pl.BlockSpec((1, HY_CB), lambda j: (0, j))] + [_const_spec(m) for m in mats],
        out_specs=[pl.BlockSpec((2, 4, L // 2, HY_CB), lambda j: (0, 0, 0, j)),
                   pl.BlockSpec((2, SUBLANES, HY_CB), lambda j: (0, 0, j))],
        out_shape=[jax.ShapeDtypeStruct((2, 4, L // 2, D), F32), jax.ShapeDtypeStruct((2, SUBLANES, D), F32)],
        compiler_params=_cparams(("arbitrary",)),
        name=f"hy_filter_{L}",
    )(feat, w1, b1, w2, b2, freq, w3r, deltas, *mats)


def split_dft_matrices(L):
    H = L // 2
    k = np.arange(H, dtype=np.int64)[:, None]
    m = np.arange(H, dtype=np.int64)[None, :]
    alt = np.where(m % 2 == 0, 1.0, -1.0)
    ang_e = ((k * m) % L).astype(np.float64) * (2 * math.pi / L)
    ang_o = ((k * (2 * m + 1)) % (2 * L)).astype(np.float64) * (math.pi / L)
    ce, se, co, so = np.cos(ang_e), np.sin(ang_e), np.cos(ang_o), np.sin(ang_o)
    se[0], so[0] = alt[0], alt[0]
    w = np.where(k == 0, 1.0, 2.0) / (2 * L)
    fe = np.concatenate([ce, se], axis=0)
    fo = np.concatenate([co, so], axis=0)
    ge = np.concatenate([(ce * w).T, se.T / L], axis=1)
    go = np.concatenate([(co * w).T, so.T / L], axis=1)
    return tuple(jnp.asarray(a.astype(np.float32)).astype(BF16) for a in (fe, fo, ge, go))


def _store_lane_blocks(ref, val):
    for c in range(ref.shape[0]):
        ref[c] = val[:, c * LANES:(c + 1) * LANES]


def _load_parity(ref, parity, n):
    return jnp.concatenate([ref[c, pl.ds(parity, n, stride=2), :] for c in range(ref.shape[0])], axis=1)


def _hyena_kernel(p0_ref, p1_ref, p2_ref, w0_ref, w1_ref, w2_ref, b0_ref, b1_ref, b2_ref, h_ref, hm_ref, hb_ref,
                  fe_ref, fo_ref, ge_ref, go_ref, o_ref, xp_s, u_s, y_s, *, L, cb):
    H = L // 2
    xp_s[0:PAD, :] = jnp.zeros((PAD, cb), F32)
    xp_s[PAD + L:2 * PAD + L, :] = jnp.zeros((PAD, cb), F32)
    first = lax.broadcasted_iota(jnp.int32, (H, 1), 0) == 0

    def conv(p_ref, w_ref, b_ref):
        xp_s[PAD:PAD + L, :] = p_ref[...]
        acc = b_ref[...] + xp_s[PAD - 1:PAD - 1 + L, :] * w_ref[0:1, :]
        for k in range(1, HY_K):
            acc = acc + xp_s[PAD - 1 + k:PAD - 1 + k + L, :] * w_ref[k:k + 1, :]
        return acc

    u = conv(p0_ref, w0_ref, b0_ref)
    for o, (p_ref, w_ref, b_ref) in enumerate(((p1_ref, w1_ref, b1_ref), (p2_ref, w2_ref, b2_ref))):
        _store_lane_blocks(u_s, u)
        se = jnp.dot(fe_ref[...], _load_parity(u_s, 0, H).astype(BF16), preferred_element_type=F32)
        so = jnp.dot(fo_ref[...], _load_parity(u_s, 1, H).astype(BF16), preferred_element_type=F32)
        e, es, od, os_ = se[0:H], se[H:L], so[0:H], so[H:L]
        b0, b1 = e + od, e - od
        b2 = jnp.where(first, es, es + os_)
        b3 = jnp.where(first, os_, os_ - es)
        har, han, hbr, hbn = h_ref[o, 0], h_ref[o, 1], h_ref[o, 2], h_ref[o, 3]
        hmr, hmn = hm_ref[o, 0:1, :], hm_ref[o, 1:2, :]
        y0 = b0 * har - b2 * han
        y1 = b1 * hbr - b3 * hbn
        y2 = b0 * han + b2 * har
        y3 = b1 * hbn + b3 * hbr
        mid_r = b2[0:1] * hmr - b3[0:1] * hmn
        mid_n = b2[0:1] * hmn + b3[0:1] * hmr
        de = jnp.where(first, mid_r, y2 - y3)
        do = jnp.where(first, mid_n, y2 + y3)
        ye = jnp.dot(ge_ref[...], jnp.concatenate([y0 + y1, de], axis=0).astype(BF16), preferred_element_type=F32)
        yo = jnp.dot(go_ref[...], jnp.concatenate([y0 - y1, do], axis=0).astype(BF16), preferred_element_type=F32)
        for c in range(cb // LANES):
            y_s[c, pl.ds(0, H, stride=2), :] = ye[:, c * LANES:(c + 1) * LANES]
            y_s[c, pl.ds(1, H, stride=2), :] = yo[:, c * LANES:(c + 1) * LANES]
        y = jnp.concatenate([y_s[c] for c in range(cb // LANES)], axis=1)
        u = conv(p_ref, w_ref, b_ref) * (y + u * hb_ref[o:o + 1, :])
    o_ref[...] = u.astype(o_ref.dtype)


def hyena_mixer(hy, conv_w, conv_b, h4, hm, hy_bias, *, L, n_seq, row_off):
    blk0 = row_off // L
    cb = min(D, HY_CB * (L_LAT // L))
    nj = D // cb
    H = L // 2
    part = lambda q: pl.BlockSpec((L, cb), lambda j, b: (blk0 + b, q * nj + j))
    wpart = lambda q: pl.BlockSpec((HY_K, cb), lambda j, b: (0, q * nj + j))
    bpart = lambda q: pl.BlockSpec((1, cb), lambda j, b: (0, q * nj + j))
    mats = split_dft_matrices(L)
    return pl.pallas_call(
        functools.partial(_hyena_kernel, L=L, cb=cb),
        grid=(nj, n_seq),
        in_specs=[part(0), part(1), part(2), wpart(0), wpart(1), wpart(2), bpart(0), bpart(1), bpart(2),
                  pl.BlockSpec((2, 4, H, cb), lambda j, b: (0, 0, 0, j)),
                  pl.BlockSpec((2, SUBLANES, cb), lambda j, b: (0, 0, j)),
                  pl.BlockSpec((2, cb), lambda j, b: (0, j))]
                 + [_const_spec(m) for m in mats],
        out_specs=pl.BlockSpec((L, cb), lambda j, b: (b, j)),
        out_shape=jax.ShapeDtypeStruct((n_seq * L, D), BF16),
        scratch_shapes=[pltpu.VMEM((L + 2 * PAD, cb), F32), pltpu.VMEM((cb // LANES, L, LANES), F32),
                        pltpu.VMEM((cb // LANES, L, LANES), F32)],
        compiler_params=_cparams(("arbitrary", "arbitrary")),
        name=f"hyena_{L}",
    )(hy, hy, hy, conv_w, conv_w, conv_w, conv_b, conv_b, conv_b, h4, hm, hy_bias, *mats)


ROUTER_LANES = LANES
BIG_LANE = 1e9


ROW_GROUP = D // LANES


def _store_row_groups(ref, val):
    n = val.shape[0]
    for s in range(ROW_GROUP):
        ref[pl.ds(s, n, stride=ROW_GROUP), :] = val[:, s * LANES:(s + 1) * LANES]


def _load_row_groups(ref, n, s):
    return ref[pl.ds(s, n, stride=ROW_GROUP), :]


def _first_max_lane(v, lanef):
    m = jnp.max(v, axis=-1, keepdims=True)
    return m, jnp.min(jnp.where(v == m, lanef, BIG_LANE), axis=-1, keepdims=True)


def _out_router_kernel(*refs, n_in, x_is_pair):
    a_refs = refs[:2 * n_in]
    refs = refs[2 * n_in:]
    is_ctx = pl.program_id(0) < CTX_TILES
    if x_is_pair:
        x = jnp.where(is_ctx, refs[0][...], refs[1][...])
        refs = refs[2:]
    else:
        x = refs[0][...]
        refs = refs[1:]
    w_ref, mod_ref, gf_ref, wr_ref, br_ref, xo_ref, h2_ref, ids_ref, wts_ref, cnt_ref = refs
    acc, k0 = None, 0
    for ac_ref, al_ref in zip(a_refs[0::2], a_refs[1::2]):
        kk = ac_ref.shape[1]
        a = jnp.where(is_ctx, ac_ref[...], al_ref[...])
        part = jnp.dot(a, w_ref[k0:k0 + kk, :], preferred_element_type=F32)
        acc = part if acc is None else acc + part
        k0 += kk
    xn = x + mod_ref[0, 2:3, :] * acc
    xo_ref[...] = xn
    h2 = _modulated(xn, gf_ref, mod_ref, 3)
    h2_ref[...] = h2

    h_hi = h2.astype(BF16)
    h_lo = (h2 - h_hi.astype(F32)).astype(BF16)
    logits = (jnp.dot(h_hi, wr_ref[0], preferred_element_type=F32) + jnp.dot(h_lo, wr_ref[0], preferred_element_type=F32)
              + jnp.dot(h_hi, wr_ref[1], preferred_element_type=F32) + br_ref[...])
    lanef = lax.broadcasted_iota(jnp.int32, logits.shape, 1).astype(F32)
    gl = jnp.where(lanef < MOE_G, logits, NEG)
    gm, gi = _first_max_lane(gl, lanef)
    g_w = 1.0 / jnp.sum(jnp.exp(gl - gm), axis=-1, keepdims=True)
    lo = MOE_G + MOE_PG * gi
    el = jnp.where((lanef >= lo) & (lanef < lo + MOE_PG), logits, NEG)
    m1, e1 = _first_max_lane(el, lanef)
    m2, e2 = _first_max_lane(jnp.where(lanef == e1, NEG, el), lanef)
    p2 = jnp.exp(m2 - m1)
    w1 = g_w / (1.0 + p2)
    ids_ref[...] = jnp.where(lanef == 0, e1 - MOE_G, jnp.where(lanef == 1, e2 - MOE_G, 0.0)).astype(jnp.int32)
    wts_ref[...] = jnp.where(lanef == 0, w1, jnp.where(lanef == 1, w1 * p2, 0.0))
    chosen = ((lanef == e1 - MOE_G) | (lanef == e2 - MOE_G)).astype(F32)
    cnt_ref[0] = jnp.sum(chosen, axis=0, keepdims=True).astype(jnp.int32)


def out_proj_router(acts, w_bf, x, mod, gf, wr, br):
    tile = lambda w: pl.BlockSpec((TM, w), lambda i: (i, 0))
    full = lambda arr: pl.BlockSpec(arr.shape, lambda i: (0,) * arr.ndim)
    x_is_pair = isinstance(x, tuple)
    xs = x if x_is_pair else (x,)
    return pl.pallas_call(
        functools.partial(_out_router_kernel, n_in=len(acts), x_is_pair=x_is_pair),
        grid=(N_TILES,),
        in_specs=[s for a in acts for s in _pair_specs(a[0].shape[1])]
                 + (_pair_specs(D) if x_is_pair else [tile(D)])
                 + [full(w_bf), pl.BlockSpec((1, 6, D), lambda i: (_mod_row(i), 0, 0)), full(gf), full(wr), full(br)],
        out_specs=[tile(D), tile(D), tile(ROUTER_LANES), tile(ROUTER_LANES),
                   pl.BlockSpec((1, 1, ROUTER_LANES), lambda i: (i, 0, 0))],
        out_shape=[jax.ShapeDtypeStruct((T_ALL, D), F32), jax.ShapeDtypeStruct((T_ALL, D), F32),
                   jax.ShapeDtypeStruct((T_ALL, ROUTER_LANES), jnp.int32),
                   jax.ShapeDtypeStruct((T_ALL, ROUTER_LANES), F32),
                   jax.ShapeDtypeStruct((N_TILES, 1, ROUTER_LANES), jnp.int32)],
        compiler_params=_cparams(("arbitrary",)),
        name="out_router",
    )(*[part for a in acts for part in a], *xs, w_bf, mod, gf, wr, br)


N_ASSIGN = 2 * T_ALL
MOE_TILES = N_ASSIGN // TM + MOE_E
N_SLOTS = MOE_TILES * TM


def route_tables(cnt3):
    cnt = cnt3[:, 0, :MOE_E]
    total = jnp.sum(cnt, axis=0)
    padded = (total + TM - 1) // TM * TM
    ends = jnp.cumsum(padded)
    gdst = (ends - padded)[None, :] + jnp.cumsum(cnt, axis=0) - cnt
    loc = jnp.cumsum(cnt, axis=1) - cnt
    starts = jnp.arange(MOE_TILES, dtype=jnp.int32) * TM
    tile_expert = jnp.minimum(jnp.sum((ends[None, :] <= starts[:, None]).astype(jnp.int32), axis=1), MOE_E - 1)
    n_used = (ends[-1] // TM).astype(jnp.int32).reshape(1)
    return cnt, loc, gdst, ends, tile_expert, n_used


RUN_BITS = (2 * TM).bit_length()
RUN_SMALL_BITS = 6


def _dispatch_kernel(cnt_s, loc_s, gdst_s, ends_s, h_ref, ids_ref, gcol_ref, xs_ref, dest_ref, srt, zbuf, sem, zsem):
    i = pl.program_id(0)
    slot = i % 2
    n_rows = 2 * TM

    @pl.when(i == 0)
    def _():
        zbuf[...] = jnp.zeros(zbuf.shape, zbuf.dtype)
        n_used = ends_s[MOE_E - 1] // TM
        for phase in ("start", "wait"):
            def tail(t, c, phase=phase):
                dst = pl.multiple_of(t * (TM * ROW_GROUP), TM * ROW_GROUP)
                cp = pltpu.make_async_copy(zbuf, xs_ref.at[pl.ds(dst, TM * ROW_GROUP), :], zsem)
                cp.start() if phase == "start" else cp.wait()
                return c

            lax.fori_loop(n_used, MOE_TILES, tail, 0)
            for e in range(MOE_E):
                end = ends_s[e]
                prev = ends_s[e - 1] if e > 0 else 0

                @pl.when(end > prev)
                def _(end=end, phase=phase):
                    dst = pl.multiple_of((end - TM) * ROW_GROUP, TM * ROW_GROUP)
                    cp = pltpu.make_async_copy(zbuf, xs_ref.at[pl.ds(dst, TM * ROW_GROUP), :], zsem)
                    cp.start() if phase == "start" else cp.wait()

    idt = ids_ref[...].astype(F32).T
    sub = lax.broadcasted_iota(jnp.int32, (LANES, TM), 0).astype(F32)
    m0 = (sub == idt[0:1, :]).astype(F32)
    m1 = (sub == idt[1:2, :]).astype(F32)
    mt = (m0 + m1).astype(BF16)
    tr = lax.broadcasted_iota(jnp.int32, (TM, TM), 0)
    tc = lax.broadcasted_iota(jnp.int32, (TM, TM), 1)
    earlier = jnp.dot(mt, (tr < tc).astype(BF16), preferred_element_type=F32)
    er = lax.broadcasted_iota(jnp.int32, (LANES, LANES), 0)
    ec = lax.broadcasted_iota(jnp.int32, (LANES, LANES), 1)
    below = jnp.dot((ec < er).astype(BF16), mt, preferred_element_type=F32)
    local = jnp.sum(below, axis=1, keepdims=True) + earlier
    glob = gcol_ref[0] + earlier
    pos0 = jnp.sum(m0 * local, axis=0, keepdims=True)
    pos1 = jnp.sum(m1 * local, axis=0, keepdims=True)
    dest_ref[0] = jnp.concatenate([jnp.sum(m0 * glob, axis=0, keepdims=True),
                                   jnp.sum(m1 * glob, axis=0, keepdims=True)], axis=0).astype(jnp.int32)

    srow = lax.broadcasted_iota(jnp.int32, (n_rows, TM), 0).astype(F32)
    perm = jnp.where((srow == pos0) | (srow == pos1), 1.0, 0.0).astype(BF16)
    _store_row_groups(srt.at[slot], jnp.dot(perm, h_ref[...].astype(BF16), preferred_element_type=F32))

    def run_pieces(n, s0, d0, bits):
        for b in bits:
            size = 1 << b
            off = (n >> (b + 1)) << (b + 1)

            @pl.when(((n >> b) & 1) == 1)
            def _(size=size, off=off):
                src = pl.multiple_of((s0 + off) * ROW_GROUP, ROW_GROUP)
                dst = pl.multiple_of((d0 + off) * ROW_GROUP, ROW_GROUP)
                pltpu.make_async_copy(srt.at[slot, pl.ds(src, size * ROW_GROUP), :],
                                      xs_ref.at[pl.ds(dst, size * ROW_GROUP), :], sem.at[slot]).start()

    for e in range(MOE_E):
        n, s0, d0 = cnt_s[0, 0, e], loc_s[0, 0, e], gdst_s[0, 0, e]

        @pl.when(n >= (1 << RUN_SMALL_BITS))
        def _(n=n, s0=s0, d0=d0):
            run_pieces(n, s0, d0, reversed(range(RUN_SMALL_BITS, RUN_BITS)))

        run_pieces(n, s0, d0, reversed(range(RUN_SMALL_BITS)))

    def wait(s):
        pltpu.make_async_copy(srt.at[s], xs_ref.at[pl.ds(0, n_rows * ROW_GROUP), :], sem.at[s]).wait()

    @pl.when(i > 0)
    def _():
        wait(1 - slot)

    @pl.when(i == N_TILES - 1)
    def _():
        wait(slot)


def dispatch_rows(h2, ids, cnt, loc, gdst, ends):
    tab = lambda: pl.BlockSpec((1, 1, MOE_E), lambda i: (i, 0, 0), memory_space=pltpu.SMEM)
    gcol = jnp.pad(gdst.astype(F32), ((0, 0), (0, LANES - MOE_E)))[:, :, None]
    return pl.pallas_call(
        _dispatch_kernel,
        grid=(N_TILES,),
        in_specs=[tab(), tab(), tab(), pl.BlockSpec(memory_space=pltpu.SMEM),
                  pl.BlockSpec((TM, D), lambda i: (i, 0)), pl.BlockSpec((TM, ROUTER_LANES), lambda i: (i, 0)),
                  pl.BlockSpec((1, LANES, 1), lambda i: (i, 0, 0))],
        out_specs=[pl.BlockSpec(memory_space=pl.ANY), pl.BlockSpec((1, 2, TM), lambda i: (i, 0, 0))],
        out_shape=[jax.ShapeDtypeStruct((N_SLOTS * ROW_GROUP, LANES), F32),
                   jax.ShapeDtypeStruct((N_TILES, 2, TM), jnp.int32)],
        scratch_shapes=[pltpu.VMEM((2, 2 * TM * ROW_GROUP, LANES), F32), pltpu.VMEM((TM * ROW_GROUP, LANES), F32),
                        pltpu.SemaphoreType.DMA((2,)), pltpu.SemaphoreType.DMA(())],
        compiler_params=_cparams(("arbitrary",)),
        name="moe_dispatch",
    )(cnt.reshape(N_TILES, 1, MOE_E), loc.reshape(N_TILES, 1, MOE_E), gdst.reshape(N_TILES, 1, MOE_E),
      ends, h2, ids, gcol)


DMA_UNROLL = 8


def _start_group_gather(src_hbm, idx_ref, n, dst_ref, sem):
    def body(j, c):
        for u in range(DMA_UNROLL):
            r = j * DMA_UNROLL + u
            src = pl.multiple_of(idx_ref[0, 0, r] * ROW_GROUP, ROW_GROUP)
            dst = pl.multiple_of(r * ROW_GROUP, ROW_GROUP)
            pltpu.make_async_copy(src_hbm.at[pl.ds(src, ROW_GROUP), :], dst_ref.at[pl.ds(dst, ROW_GROUP), :],
                                  sem).start(priority=u % 2)
        return c

    lax.fori_loop(0, n // DMA_UNROLL, body, 0)


def _wait_group_gather(src_hbm, dst_ref, sem):
    pltpu.make_async_copy(src_hbm.at[pl.ds(0, dst_ref.shape[0]), :], dst_ref, sem).wait()


def _experts_kernel(te_ref, nu_ref, x_ref, wg_ref, wu_ref, wd_ref, o_ref, xcat):
    i = pl.program_id(0)

    @pl.when(i < nu_ref[0])
    def _():
        for s in range(ROW_GROUP):
            xcat[:, s * LANES:(s + 1) * LANES] = _load_row_groups(x_ref, TM, s).astype(BF16)
        x = xcat[...]
        g = jnp.dot(x, wg_ref[0, 0].astype(BF16), preferred_element_type=F32)
        u = jnp.dot(x, wu_ref[0, 0].astype(BF16), preferred_element_type=F32)
        hid = (_silu(g) * u).astype(BF16)
        _store_row_groups(o_ref, jnp.dot(hid, wd_ref[0, 0].astype(BF16), preferred_element_type=F32))

    @pl.when(i >= nu_ref[0])
    def _():
        o_ref[...] = jnp.zeros(o_ref.shape, o_ref.dtype)


def grouped_experts(xs, w_gate, w_up, w_down, tile_expert, n_used, layer):
    wspec = lambda a, b: pl.BlockSpec((1, 1, a, b), lambda i, te, nu: (layer, te[i], 0, 0))
    return pl.pallas_call(
        _experts_kernel,
        grid_spec=pltpu.PrefetchScalarGridSpec(
            num_scalar_prefetch=2,
            grid=(MOE_TILES,),
            in_specs=[pl.BlockSpec((TM * ROW_GROUP, LANES), lambda i, te, nu: (jnp.minimum(i, nu[0] - 1), 0)),
                      wspec(D, MOE_F), wspec(D, MOE_F), wspec(MOE_F, D)],
            out_specs=pl.BlockSpec((TM * ROW_GROUP, LANES), lambda i, te, nu: (i, 0)),
            scratch_shapes=[pltpu.VMEM((TM, D), BF16)],
        ),
        out_shape=jax.ShapeDtypeStruct((N_SLOTS * ROW_GROUP, LANES), F32),
        compiler_params=_cparams(("arbitrary",)),
        name="moe_experts",
    )(tile_expert, n_used, xs, w_gate, w_up, w_down)


def _combine_kernel(cur_ref, nxt_ref, ys_hbm, x_ref, wts_ref, mod_ref, gfin_ref, *rest, final):
    *o_refs, buf, sem = rest
    i = pl.program_id(0)
    slot = i % 2

    @pl.when(i == 0)
    def _():
        _start_group_gather(ys_hbm, cur_ref, 2 * TM, buf.at[0], sem.at[0])

    @pl.when(i + 1 < N_TILES)
    def _():
        _start_group_gather(ys_hbm, nxt_ref, 2 * TM, buf.at[1 - slot], sem.at[1 - slot])

    _wait_group_gather(ys_hbm, buf.at[slot], sem.at[slot])
    w0, w1 = wts_ref[:, 0:1], wts_ref[:, 1:2]

    def finish(o_ref):
        for s in range(ROW_GROUP):
            cols = slice(s * LANES, (s + 1) * LANES)
            y0 = buf[slot, pl.ds(s, TM, stride=ROW_GROUP), :]
            y1 = buf[slot, pl.ds(TM * ROW_GROUP + s, TM, stride=ROW_GROUP), :]
            o_ref[:, cols] = x_ref[:, cols] + mod_ref[0, 5:6, cols] * (w0 * y0 + w1 * y1)
        if final:
            o_ref[...] = _rms(o_ref[...]) * gfin_ref[...]

    if final:
        pl.when(i < CTX_TILES)(lambda: finish(o_refs[0]))
        pl.when(i >= CTX_TILES)(lambda: finish(o_refs[1]))
    else:
        finish(o_refs[0])


def moe_combine(ys, dest, x, wts, mod, gfin, *, final):
    tile = lambda w: pl.BlockSpec((TM, w), lambda i: (i, 0))
    if final:
        out_specs = _pair_specs(D)
        out_shape = [jax.ShapeDtypeStruct((T_CTX, D), F32), jax.ShapeDtypeStruct((T_LAT, D), F32)]
    else:
        out_specs = [tile(D)]
        out_shape = [jax.ShapeDtypeStruct((T_ALL, D), F32)]
    idx = lambda f: pl.BlockSpec((1, 1, 2 * TM), lambda i: (f(i), 0, 0), memory_space=pltpu.SMEM)
    dest3 = dest.reshape(N_TILES, 1, 2 * TM)
    out = pl.pallas_call(
        functools.partial(_combine_kernel, final=final),
        grid=(N_TILES,),
        in_specs=[idx(lambda i: i), idx(lambda i: jnp.minimum(i + 1, N_TILES - 1)),
                  pl.BlockSpec(memory_space=pl.ANY), tile(D), tile(ROUTER_LANES),
                  pl.BlockSpec((1, 6, D), lambda i: (_mod_row(i), 0, 0)),
                  pl.BlockSpec((1, D), lambda i: (0, 0))],
        out_specs=out_specs,
        out_shape=out_shape,
        scratch_shapes=[pltpu.VMEM((2, 2 * TM * ROW_GROUP, LANES), F32), pltpu.SemaphoreType.DMA((2,))],
        compiler_params=_cparams(("arbitrary",)),
        name="moe_combine",
    )(dest3, dest3, ys, x, wts, mod, gfin)
    return out if final else out[0]


ODD_COLS = 2048
ROPE_Q = MLA_H * MLA_ROPE
ROPE_SHIFT = ROPE_F


def rope_tables():
    t = np.arange(L_LAT)
    pos = np.stack([t // GRID_W, t % GRID_W], axis=1).astype(np.float64)
    inv = 10000.0 ** (-np.arange(ROPE_F, dtype=np.float64) / ROPE_F)
    lane = np.arange(ROPE_Q) % MLA_ROPE
    axis = lane // (2 * ROPE_F)
    first = (lane % (2 * ROPE_F)) < ROPE_F
    ang = pos[:, axis] * inv[lane % ROPE_F][None, :]
    cos, sin = np.cos(ang), np.sin(ang)
    tabs = [cos, np.where(first[None, :], -sin, 0.0), np.where(first[None, :], 0.0, sin)]
    ident = [np.ones((1, TM, ROPE_Q)), np.zeros((1, TM, ROPE_Q)), np.zeros((1, TM, ROPE_Q))]
    return [jnp.asarray(np.concatenate([i, tb.reshape(LAT_TILES_PER_SEQ, TM, ROPE_Q)], axis=0).astype(np.float32))
            for i, tb in zip(ident, tabs)]


def _rope(x, c, a, b):
    n = x.shape[1]
    return x * c[:, :n] + pltpu.roll(x, n - ROPE_SHIFT, 1) * a[:, :n] + pltpu.roll(x, ROPE_SHIFT, 1) * b[:, :n]


def _odd_in_kernel(x_ref, mod_ref, g_ref, w_ref, gq_ref, wuq_ref, gkv_ref, wukv_ref, rc_ref, ra_ref, rb_ref,
                   q_ref, kv_ref, qm_ref, ckv_ref, kvu_ref, kr_ref, knew_ref, vnew_ref):
    hb = _modulated(x_ref[...], g_ref, mod_ref, 0).astype(BF16)
    q_ref[...] = jnp.dot(hb, w_ref[:, 0:NA_W], preferred_element_type=F32)
    k = jnp.dot(hb, w_ref[:, NA_W:2 * NA_W], preferred_element_type=F32)
    v = jnp.dot(hb, w_ref[:, 2 * NA_W:3 * NA_W], preferred_element_type=F32)
    kv_ref[:, 0:NA_W] = k.astype(kv_ref.dtype)
    kv_ref[:, NA_W:2 * NA_W] = v.astype(kv_ref.dtype)

    @pl.when(pl.program_id(0) < CTX_TILES)
    def _():
        for h in range(NA_H):
            knew_ref[0, h] = k[:, h * NA_D:(h + 1) * NA_D]
            vnew_ref[0, h] = v[:, h * NA_D:(h + 1) * NA_D]

    rest = jnp.dot(hb, w_ref[:, 3 * NA_W:ODD_COLS], preferred_element_type=F32)
    rc, ra, rb = rc_ref[0], ra_ref[0], rb_ref[0]
    qd = (_rms(rest[:, 0:MLA_QR]) * gq_ref[...]).astype(BF16)
    qm = jnp.dot(qd, wuq_ref[...], preferred_element_type=F32)
    qm_ref[:, 0:MLA_H * MLA_NOPE] = qm[:, 0:MLA_H * MLA_NOPE]
    qm_ref[:, MLA_H * MLA_NOPE:] = _rope(qm[:, MLA_H * MLA_NOPE:], rc, ra, rb)
    ckv = _rms(rest[:, MLA_QR:MLA_QR + MLA_KVR]) * gkv_ref[...]
    ckv_ref[...] = ckv
    kvu_ref[...] = jnp.dot(ckv.astype(BF16), wukv_ref[...],
                           preferred_element_type=F32).astype(kvu_ref.dtype)
    kr_ref[...] = _rope(rest[:, MLA_QR + MLA_KVR:], rc, ra, rb)


def odd_in_proj(x, mod, g, w_bf, gq, wuq_bf, gkv, wukv_bf, tabs):
    tile = lambda w: pl.BlockSpec((TM, w), lambda i: (i, 0))
    full = lambda arr: pl.BlockSpec(arr.shape, lambda i: (0,) * arr.ndim)
    tab = pl.BlockSpec((1, TM, ROPE_Q),
                       lambda i: (jnp.where(i < CTX_TILES, 0, 1 + (i - CTX_TILES) % LAT_TILES_PER_SEQ), 0, 0))
    outs = ((NA_W, F32), (2 * NA_W, BF16), (MLA_H * MLA_QK, F32), (MLA_KVR, F32),
            (MLA_H * (MLA_NOPE + MLA_V), BF16), (LANES, F32))
    cache = pl.BlockSpec((1, NA_H, L_CTX, NA_D), lambda i: (jnp.minimum(i, CTX_TILES - 1), 0, 0, 0))
    cache_shape = jax.ShapeDtypeStruct((N_CTX, NA_H, L_CTX, NA_D), F32)
    return pl.pallas_call(
        _odd_in_kernel,
        grid=(N_TILES,),
        in_specs=[tile(D), pl.BlockSpec((1, 6, D), lambda i: (_mod_row(i), 0, 0)), full(g), full(w_bf),
                  full(gq), full(wuq_bf), full(gkv), full(wukv_bf), tab, tab, tab],
        out_specs=[tile(w) for w, _ in outs] + [cache, cache],
        out_shape=[jax.ShapeDtypeStruct((T_ALL, w), dt) for w, dt in outs] + [cache_shape, cache_shape],
        compiler_params=_cparams(("arbitrary",)),
        name="odd_in",
    )(x, mod, g, w_bf, gq, wuq_bf, gkv, wukv_bf, *tabs)


LOG2E = math.log2(math.e)
NA_QSCALE = NA_D ** -0.5 * LOG2E
MLA_QSCALE = MLA_QK ** -0.5 * LOG2E
NT = (((1,), (1,)), ((), ()))


def _softmax_pv(scores, values):
    m = functools.reduce(jnp.maximum, [jnp.max(s, axis=-1, keepdims=True) for s in scores])
    ps = [jnp.exp2(s - m) for s in scores]
    den = functools.reduce(jnp.add, [jnp.sum(p, axis=-1, keepdims=True) for p in ps])
    acc = functools.reduce(jnp.add, [jnp.dot(p.astype(BF16), v, preferred_element_type=F32) for p, v in zip(ps, values)])
    return acc / den


def _pair(ref_or_val, p, base=0):
    return ref_or_val[:, base + p * LANES:base + (p + 1) * LANES]


def _low_half():
    return lax.broadcasted_iota(jnp.int32, (1, LANES), 1) < NA_D


def _rope_key_forms(kr):
    return kr.astype(BF16), pltpu.roll(kr, LANES // 2, 1).astype(BF16)


def _mla_pair(qm, p, sources, lo):
    outs = []
    for e in range(2):
        h = 2 * p + e
        qn = qm[:, h * MLA_NOPE:(h + 1) * MLA_NOPE] * MLA_QSCALE
        qr = qm[:, MLA_H * MLA_NOPE + h * MLA_ROPE:MLA_H * MLA_NOPE + (h + 1) * MLA_ROPE] * MLA_QSCALE
        z = jnp.zeros((qn.shape[0], LANES - MLA_QK), F32)
        qcat = jnp.concatenate([qn, qr, z] if e == 0 else [qr, z, qn], axis=1).astype(BF16)
        scores = []
        for kb, kr_lo, kr_hi, _ in sources:
            kcat = jnp.where(lo, kb, kr_hi) if e == 0 else jnp.where(lo, kr_lo, kb)
            scores.append(lax.dot_general(qcat, kcat, NT, preferred_element_type=F32))
        outs.append(_softmax_pv(scores, [src[3] for src in sources]))
    return jnp.where(lo, outs[0], outs[1])


def _attn_ctx_kernel(q_ref, kv_ref, qm_ref, kvu_ref, kr_ref, ona_ref, omla_ref):
    lo = _low_half()
    for p in range(NA_H // 2):
        qb = _pair(q_ref, p) * NA_QSCALE
        kb = _pair(kv_ref, p).astype(BF16)
        vb = _pair(kv_ref, p, NA_W).astype(BF16)
        outs = []
        for e in range(2):
            q = jnp.where(lo if e == 0 else jnp.logical_not(lo), qb, 0.0).astype(BF16)
            outs.append(_softmax_pv([lax.dot_general(q, kb, NT, preferred_element_type=F32)], [vb]))
        ona_ref[:, p * LANES:(p + 1) * LANES] = jnp.where(lo, outs[0], outs[1]).astype(ona_ref.dtype)
    kr_lo, kr_hi = _rope_key_forms(kr_ref[...])
    for p in range(MLA_H // 2):
        src = (_pair(kvu_ref, p).astype(BF16), kr_lo, kr_hi, _pair(kvu_ref, p, MLA_H * MLA_NOPE).astype(BF16))
        omla_ref[:, p * LANES:(p + 1) * LANES] = _mla_pair(qm_ref, p, [src], lo).astype(omla_ref.dtype)


def attn_context(q, kv, qm, kvu, kr):
    seq = lambda w: pl.BlockSpec((L_CTX, w), lambda b: (b, 0))
    return pl.pallas_call(
        _attn_ctx_kernel,
        grid=(N_CTX,),
        in_specs=[seq(NA_W), seq(2 * NA_W), seq(MLA_H * MLA_QK), seq(MLA_H * (MLA_NOPE + MLA_V)), seq(LANES)],
        out_specs=[seq(NA_W), seq(MLA_H * MLA_V)],
        out_shape=[jax.ShapeDtypeStruct((T_CTX, NA_W), BF16), jax.ShapeDtypeStruct((T_CTX, MLA_H * MLA_V), BF16)],
        compiler_params=_cparams(("arbitrary",)),
        name="attn_ctx",
    )(q, kv, qm, kvu, kr)


N_DR = 2 * NA_WIN_R - 1
GRID_ROWS = L_LAT // GRID_W


def _na_bias_kernel(t_ref, o_ref):
    neg = jnp.full((GRID_W, GRID_W), NEG, F32)
    for r in range(GRID_ROWS):
        r0 = min(max(r - NA_WIN_R // 2, 0), GRID_ROWS - NA_WIN_R)
        for kr in range(GRID_ROWS):
            in_window = r0 <= kr < r0 + NA_WIN_R
            blk = t_ref[0, kr - r + NA_WIN_R - 1] if in_window else neg
            o_ref[0, r * GRID_W:(r + 1) * GRID_W, kr * GRID_W:(kr + 1) * GRID_W] = blk


def neighbourhood_bias(rel_bias):
    c = np.arange(GRID_W)
    c0 = np.clip(c - NA_WIN_C // 2, 0, GRID_W - NA_WIN_C)
    col_ok = (c[None, :] >= c0[:, None]) & (c[None, :] < c0[:, None] + NA_WIN_C)
    dc = np.clip(c[None, :] - c[:, None], -(NA_WIN_C - 1), NA_WIN_C - 1) + NA_WIN_C - 1
    sel_c = (dc[:, :, None] == np.arange(2 * NA_WIN_C - 1)).astype(np.float32)
    t = jnp.einsum("hdj,qcj->hdqc", rel_bias.astype(F32), jnp.asarray(sel_c), precision=HIGHEST)
    t = jnp.where(jnp.asarray(col_ok)[None, None], t * LOG2E, NEG)
    return pl.pallas_call(
        _na_bias_kernel,
        grid=(NA_H,),
        in_specs=[pl.BlockSpec((1, N_DR, GRID_W, GRID_W), lambda h: (h, 0, 0, 0))],
        out_specs=pl.BlockSpec((1, L_LAT, L_LAT), lambda h: (h, 0, 0)),
        out_shape=jax.ShapeDtypeStruct((NA_H, L_LAT, L_LAT), F32),
        compiler_params=_cparams(("arbitrary",)),
        name="na_bias",
    )(t)


def _na_lat_kernel(q_ref, k_ref, v_ref, kc_ref, vc_ref, b_ref, o_ref):
    lo = _low_half()
    for p in range(NA_H // 2):
        qb = _pair(q_ref, p) * NA_QSCALE
        kb = _pair(k_ref, p).astype(BF16)
        vb = _pair(v_ref, p).astype(BF16)
        outs = []
        for e in range(2):
            h = 2 * p + e
            half = slice(e * NA_D, (e + 1) * NA_D)
            q = jnp.where(lo if e == 0 else jnp.logical_not(lo), qb, 0.0).astype(BF16)
            s1 = lax.dot_general(q, kb, NT, preferred_element_type=F32) + b_ref[h]
            s2 = lax.dot_general(qb[:, half].astype(BF16), kc_ref[0, 0, h].astype(BF16), NT, preferred_element_type=F32)
            m = jnp.maximum(jnp.max(s1, axis=-1, keepdims=True), jnp.max(s2, axis=-1, keepdims=True))
            p1, p2 = jnp.exp2(s1 - m), jnp.exp2(s2 - m)
            den = jnp.sum(p1, axis=-1, keepdims=True) + jnp.sum(p2, axis=-1, keepdims=True)
            a1 = jnp.dot(p1.astype(BF16), vb, preferred_element_type=F32)
            a2 = jnp.dot(p2.astype(BF16), vc_ref[0, 0, h].astype(BF16), preferred_element_type=F32)
            outs.append((a1[:, half] + a2) / den)
        o_ref[:, p * LANES:(p + 1) * LANES] = jnp.concatenate(outs, axis=1).astype(o_ref.dtype)


def attn_neighbourhood_latent(q, kv, cache_k, cache_v, bias):
    nq = L_LAT // TM
    t0 = T_CTX // TM
    s0 = T_CTX // L_LAT
    cache = pl.BlockSpec((1, 1, NA_H, PAST, NA_D), lambda qt, b: (b, 0, 0, 0, 0))
    return pl.pallas_call(
        _na_lat_kernel,
        grid=(nq, N_LAT),
        in_specs=[pl.BlockSpec((TM, NA_W), lambda qt, b: (t0 + b * nq + qt, 0)),
                  pl.BlockSpec((L_LAT, NA_W), lambda qt, b: (s0 + b, 0)),
                  pl.BlockSpec((L_LAT, NA_W), lambda qt, b: (s0 + b, 1)),
                  cache, cache,
                  pl.BlockSpec((NA_H, TM, L_LAT), lambda qt, b: (0, qt, 0))],
        out_specs=pl.BlockSpec((TM, NA_W), lambda qt, b: (b * nq + qt, 0)),
        out_shape=jax.ShapeDtypeStruct((T_LAT, NA_W), BF16),
        compiler_params=_cparams(("arbitrary", "arbitrary")),
        name="attn_na_lat",
    )(q, kv, kv, cache_k, cache_v, bias)


def _mla_lat_kernel(qm_ref, kvu_ref, kr_ref, ckv_ref, krc_ref, wukv_ref, o_ref):
    lo = _low_half()
    kvc = jnp.dot(ckv_ref[0, 0].astype(BF16), wukv_ref[...], preferred_element_type=F32)
    kr_lo, kr_hi = _rope_key_forms(kr_ref[...])
    krc = jnp.concatenate([krc_ref[0, 0], jnp.zeros((PAST, LANES - MLA_ROPE), F32)], axis=1)
    krc_lo, krc_hi = _rope_key_forms(krc)
    vbase = MLA_H * MLA_NOPE
    for p in range(MLA_H // 2):
        lat = (_pair(kvu_ref, p).astype(BF16), kr_lo, kr_hi, _pair(kvu_ref, p, vbase).astype(BF16))
        ctx = (_pair(kvc, p).astype(BF16), krc_lo, krc_hi, _pair(kvc, p, vbase).astype(BF16))
        o_ref[:, p * LANES:(p + 1) * LANES] = _mla_pair(qm_ref, p, [lat, ctx], lo).astype(o_ref.dtype)


def attn_mla_latent(qm, kvu, kr, cache_ckv, cache_krope, wukv_bf):
    nq = L_LAT // TM
    t0 = T_CTX // TM
    s0 = T_CTX // L_LAT
    return pl.pallas_call(
        _mla_lat_kernel,
        grid=(nq, N_LAT),
        in_specs=[pl.BlockSpec((TM, MLA_H * MLA_QK), lambda qt, b: (t0 + b * nq + qt, 0)),
                  pl.BlockSpec((L_LAT, MLA_H * (MLA_NOPE + MLA_V)), lambda qt, b: (s0 + b, 0)),
                  pl.BlockSpec((L_LAT, LANES), lambda qt, b: (s0 + b, 0)),
                  pl.BlockSpec((1, 1, PAST, MLA_KVR), lambda qt, b: (b, 0, 0, 0)),
                  pl.BlockSpec((1, 1, PAST, MLA_ROPE), lambda qt, b: (b, 0, 0, 0)),
                  pl.BlockSpec(wukv_bf.shape, lambda qt, b: (0, 0))],
        out_specs=pl.BlockSpec((TM, MLA_H * MLA_V), lambda qt, b: (b * nq + qt, 0)),
        out_shape=jax.ShapeDtypeStruct((T_LAT, MLA_H * MLA_V), BF16),
        compiler_params=_cparams(("arbitrary", "arbitrary")),
        name="attn_mla_lat",
    )(qm, kvu, kr, cache_ckv, cache_krope, wukv_bf)


def moe_block(h2, ids, wts, cnt3, x, mod, gfin, w_gate, w_up, w_down, layer, *, final):
    cnt, loc, gdst, ends, tile_expert, n_used = route_tables(cnt3)
    xs, dest = dispatch_rows(h2, ids, cnt, loc, gdst, ends)
    ys = grouped_experts(xs, w_gate, w_up, w_down, tile_expert, n_used, layer)
    return moe_combine(ys, dest, x, wts, mod, gfin, final=final)


def _pad_lanes(a):
    return jnp.pad(a, ((0, 0), (0, LANES - a.shape[1])))


def _hyena_features(L):
    t = np.linspace(0.0, 1.0, L)[:, None]
    w = 2.0 * math.pi * np.arange(L) / L
    bands = np.linspace(1e-4, HY_BANDS - 1, HY_BANDS)
    ang = w[:, None] * bands[None]
    feat = np.concatenate([t, np.cos(ang), -np.sin(ang)], axis=-1)
    return jnp.asarray(np.pad(feat, ((0, 0), (0, LANES - HY_FEAT))).astype(np.float32))


def _router_params(w_gr, b_gr, w_er, b_er):
    wr = _pad_lanes(jnp.concatenate([w_gr, w_er], axis=1))
    br = _pad_lanes(jnp.concatenate([b_gr, b_er])[None])
    wr_hi = wr.astype(BF16)
    wr_lo = (wr - wr_hi.astype(F32)).astype(BF16)
    return jnp.stack([wr_hi, wr_lo]), br


def _even_layer(x, mod, g_mix, state, w_in, conv_w, conv_b, a_log, dt_bias, d_skip, g_ssd, hy_conv_w, hy_conv_b,
                hy_w1, hy_b1, hy_w2, hy_b2, hy_w3, hy_freq, hy_bias):
    n0 = D + SSD_XBC
    w_bf = jnp.concatenate([w_in[:, :n0], w_in[:, n0 + SSD_H:], w_in[:, n0:n0 + SSD_H],
                            jnp.zeros((D, LANES - SSD_H), F32)], axis=1).astype(BF16)
    z, xbc, hy, dtr = even_in_proj(x, mod, g_mix, w_bf)
    small = (conv_w, conv_b[None], _pad_lanes(dt_bias), _pad_lanes(a_log), jnp.repeat(d_skip, SSD_P)[None], g_ssd[None])
    y_c, fin = ssd_mixer(xbc, dtr, z, None, *small, L=L_CTX, n_seq=N_CTX, row_off=0)
    (y_l,) = ssd_mixer(xbc, dtr, z, state.reshape(N_LAT, 2, SSD_H * SSD_P, SSD_N), *small,
                       L=L_LAT, n_seq=N_LAT, row_off=T_CTX)
    w1 = jnp.pad(hy_w1, ((0, LANES - HY_FEAT), (0, 0)))
    w3r = hy_w3.reshape(HY_HID, 4, D).transpose(1, 0, 2)
    deltas = jnp.asarray(np.linspace(HY_MIN_DECAY, HY_MAX_DECAY, D).astype(np.float32))[None]
    us = []
    for L, n_seq, off in ((L_CTX, N_CTX, 0), (L_LAT, N_LAT, T_CTX)):
        h4, hm = hyena_filter_spectra(_hyena_features(L), w1, hy_b1[None], hy_w2, hy_b2[None], hy_freq, w3r, deltas, L=L)
        us.append(hyena_mixer(hy, hy_conv_w, hy_conv_b[None], h4, hm, hy_bias, L=L, n_seq=n_seq, row_off=off))
    return (y_c, y_l), tuple(us), fin


def _odd_layer(x, mod, g_mix, cache_k, cache_v, cache_ckv, cache_kr, rel_bias, w_in, g_q, w_uq, g_kv, w_ukv):
    w_bf = jnp.pad(w_in, ((0, 0), (0, ODD_COLS - w_in.shape[1]))).astype(BF16)
    wuq = w_uq.reshape(MLA_QR, MLA_H, MLA_QK)
    wuq_bf = jnp.concatenate([wuq[:, :, :MLA_NOPE].reshape(MLA_QR, -1), wuq[:, :, MLA_NOPE:].reshape(MLA_QR, -1)],
                             axis=1).astype(BF16)
    wukv = w_ukv.reshape(MLA_KVR, MLA_H, MLA_NOPE + MLA_V)
    wukv_bf = jnp.concatenate([wukv[:, :, :MLA_NOPE].reshape(MLA_KVR, -1), wukv[:, :, MLA_NOPE:].reshape(MLA_KVR, -1)],
                              axis=1).astype(BF16)
    q, kv, qm, ckv, kvu, kr, k_new, v_new = odd_in_proj(x, mod, g_mix, w_bf, g_q[None], wuq_bf, g_kv[None], wukv_bf,
                                                        rope_tables())
    ona_c, omla_c = attn_context(q, kv, qm, kvu, kr)
    ona_l = attn_neighbourhood_latent(q, kv, cache_k, cache_v, neighbourhood_bias(rel_bias))
    omla_l = attn_mla_latent(qm, kvu, kr, cache_ckv, cache_kr, wukv_bf)
    return (ona_c, ona_l), (omla_c, omla_l), k_new, v_new, ckv, kr


def kernel(x_prompt, x_sample, state_ssd, cache_na_k, cache_na_v, cache_mla_ckv, cache_mla_krope, c, c_ctx, w_ada, b_ada, norm_mix, norm_ffn, norm_final, ev_w_in, ev_conv_w, ev_conv_b, ssd_A_log, ssd_dt_bias, ssd_d, ssd_norm, hy_conv_w, hy_conv_b, hy_w1, hy_b1, hy_w2, hy_b2, hy_w3, hy_freq, hy_bias, ev_w_out, od_w_in, mla_q_norm, mla_w_uq, mla_kv_norm, mla_w_ukv, na_rel_bias, od_w_out, moe_w_gr, moe_b_gr, moe_w_er, moe_b_er, moe_w_gate, moe_w_up, moe_w_down):
    x = (x_prompt.reshape(T_CTX, D), x_sample.reshape(T_LAT, D))
    cvec = jnp.zeros((MOD_ROWS, D), F32).at[0].set(c_ctx).at[1:1 + N_LAT].set(c)
    mod = ada_modulation(cvec, w_ada, b_ada)
    gfin = norm_final[None]

    y, u, fin = _even_layer(x, mod[0], norm_mix[0][None], state_ssd[:, 0], ev_w_in[0], ev_conv_w[0], ev_conv_b[0],
                            ssd_A_log[0], ssd_dt_bias[0], ssd_d[0], ssd_norm[0], hy_conv_w[0], hy_conv_b[0],
                            hy_w1[0], hy_b1[0], hy_w2[0], hy_b2[0], hy_w3[0], hy_freq[0], hy_bias[0])
    wr, br = _router_params(moe_w_gr[0], moe_b_gr[0], moe_w_er[0], moe_b_er[0])
    xn, h2, ids, wts, cnt3 = out_proj_router([y, u], ev_w_out[0].astype(BF16), x, mod[0], norm_ffn[0][None], wr, br)
    x = moe_block(h2, ids, wts, cnt3, xn, mod[0], gfin, moe_w_gate, moe_w_up, moe_w_down, 0, final=False)

    o_na, o_mla, k_new, v_new, ckv, kr = _odd_layer(x, mod[1], norm_mix[1][None], cache_na_k, cache_na_v, cache_mla_ckv,
                                           cache_mla_krope, na_rel_bias[0], od_w_in[0], mla_q_norm[0], mla_w_uq[0],
                                           mla_kv_norm[0], mla_w_ukv[0])
    wr, br = _router_params(moe_w_gr[1], moe_b_gr[1], moe_w_er[1], moe_b_er[1])
    xn, h2, ids, wts, cnt3 = out_proj_router([o_na, o_mla], od_w_out[0].astype(BF16), x, mod[1], norm_ffn[1][None], wr, br)
    y_c, y_l = moe_block(h2, ids, wts, cnt3, xn, mod[1], gfin, moe_w_gate, moe_w_up, moe_w_down, 1, final=True)

    return (y_c.reshape(N_CTX, L_CTX, D),
            y_l.reshape(N_LAT, L_LAT, D),
            fin.reshape(N_CTX, 1, 2, SSD_H, SSD_P, SSD_N),
            k_new[:, None],
            v_new[:, None],
            ckv[:T_CTX].reshape(N_CTX, 1, L_CTX, MLA_KVR),
            kr[:T_CTX, :MLA_ROPE].reshape(N_CTX, 1, L_CTX, MLA_ROPE))
```

```python
import functools
import math

import numpy as np
import jax
import jax.numpy as jnp
from jax import lax
from jax.experimental import pallas as pl
from jax.experimental.pallas import tpu as pltpu

F32 = jnp.float32
BF16 = jnp.bfloat16
HIGHEST = lax.Precision.HIGHEST

D = 1024
N_CTX, L_CTX = 16, 256
N_LAT, L_LAT = 8, 1024
T_CTX = N_CTX * L_CTX
T_LAT = N_LAT * L_LAT
T_ALL = T_CTX + T_LAT
PAST = 512
GRID_W = 64
EPS = 1e-6
NEG = -1e30

SSD_H, SSD_P, SSD_N, SSD_G = 16, 64, 128, 2
SSD_XBC = D + 2 * SSD_G * SSD_N
SSD_K = 5
CHUNK = 128

HY_K = 3
HY_BANDS = 16
HY_FEAT = 1 + 2 * HY_BANDS
HY_HID = 64
HY_MIN_DECAY = abs(math.log(1e-2) / 1.5)
HY_MAX_DECAY = abs(math.log(1e-2) / 0.3)

NA_H, NA_D = 8, 64
NA_W = NA_H * NA_D
NA_WIN_R, NA_WIN_C = 8, 16
MLA_H, MLA_QR, MLA_KVR = 8, 256, 128
MLA_NOPE, MLA_ROPE, MLA_V = 64, 32, 64
MLA_QK = MLA_NOPE + MLA_ROPE
ROPE_F = MLA_ROPE // 4

MOE_G, MOE_PG, MOE_E, MOE_F = 4, 8, 32, 256

LANES = 128
SUBLANES = 8
VMEM_LIMIT = 56 * 1024 * 1024

TM = 256
N_TILES = T_ALL // TM
CTX_TILES = T_CTX // TM
LAT_TILES_PER_SEQ = L_LAT // TM
MOD_ROWS = 16


def _cparams(sem):
    return pltpu.CompilerParams(dimension_semantics=sem, vmem_limit_bytes=VMEM_LIMIT)


def _mod_row(i):
    return jnp.where(i < CTX_TILES, 0, 1 + (i - CTX_TILES) // LAT_TILES_PER_SEQ)


def _silu(x):
    return x * jax.nn.sigmoid(x)


def _rms(x):
    return x * lax.rsqrt(jnp.mean(x * x, axis=-1, keepdims=True) + EPS)


def _ada_kernel(c_ref, w_ref, b_ref, o_ref):
    c = c_ref[...]
    o_ref[0] = jnp.dot(_silu(c), w_ref[0], precision=HIGHEST, preferred_element_type=F32) + b_ref[0]


def ada_modulation(cvec, w_ada, b_ada):
    depth = w_ada.shape[0]
    out = pl.pallas_call(
        _ada_kernel,
        grid=(depth, 6),
        in_specs=[
            pl.BlockSpec((MOD_ROWS, D), lambda l, j: (0, 0)),
            pl.BlockSpec((1, D, D), lambda l, j: (l, 0, j)),
            pl.BlockSpec((1, 1, D), lambda l, j: (l, 0, j)),
        ],
        out_specs=pl.BlockSpec((1, MOD_ROWS, D), lambda l, j: (l, 0, j)),
        out_shape=jax.ShapeDtypeStruct((depth, MOD_ROWS, 6 * D), F32),
        compiler_params=_cparams(("arbitrary", "arbitrary")),
        name="ada",
    )(cvec, w_ada, b_ada.reshape(depth, 1, 6 * D))
    return out.reshape(depth, MOD_ROWS, 6, D)


PROJ_CHUNK = 512


def _modulated(x, g_ref, mod_ref, shift_row):
    h = _rms(x) * g_ref[...]
    return h * (1.0 + mod_ref[0, shift_row + 1:shift_row + 2, :]) + mod_ref[0, shift_row:shift_row + 1, :]


IN_TM = 512
IN_CTX_TILES = T_CTX // IN_TM


def _even_in_kernel(xc_ref, xl_ref, mod_ref, g_ref, w_ref, z_ref, xbc_ref, hy_ref, dt_ref):
    x = jnp.where(pl.program_id(0) < IN_CTX_TILES, xc_ref[...], xl_ref[...])
    hb = _modulated(x, g_ref, mod_ref, 0).astype(BF16)
    col = 0
    for o_ref in (z_ref, xbc_ref, hy_ref, dt_ref):
        width = o_ref.shape[1]
        for c0 in range(0, width, PROJ_CHUNK):
            c1 = min(c0 + PROJ_CHUNK, width)
            o_ref[:, c0:c1] = jnp.dot(hb, w_ref[:, col + c0:col + c1], preferred_element_type=F32)
        col += width


def _pair_specs(width):
    return [pl.BlockSpec((TM, width), lambda i: (jnp.minimum(i, CTX_TILES - 1), 0)),
            pl.BlockSpec((TM, width), lambda i: (jnp.maximum(i - CTX_TILES, 0), 0))]


def even_in_proj(x_pair, mod, g, w_bf):
    widths = (D, SSD_XBC, 3 * D, LANES)
    mod_row = lambda i: jnp.where(i < IN_CTX_TILES, 0, 1 + (i - IN_CTX_TILES) // (L_LAT // IN_TM))
    return pl.pallas_call(
        _even_in_kernel,
        grid=(T_ALL // IN_TM,),
        in_specs=[
            pl.BlockSpec((IN_TM, D), lambda i: (jnp.minimum(i, IN_CTX_TILES - 1), 0)),
            pl.BlockSpec((IN_TM, D), lambda i: (jnp.maximum(i - IN_CTX_TILES, 0), 0)),
            pl.BlockSpec((1, 6, D), lambda i: (mod_row(i), 0, 0)),
            pl.BlockSpec((1, D), lambda i: (0, 0)),
            _const_spec(w_bf),
        ],
        out_specs=[pl.BlockSpec((IN_TM, w), lambda i: (i, 0)) for w in widths],
        out_shape=[jax.ShapeDtypeStruct((T_ALL, w), F32) for w in widths],
        compiler_params=_cparams(("arbitrary",)),
        name="even_in",
    )(*x_pair, mod, g, w_bf)


PAD = SUBLANES


def _ssd_kernel(*refs, L, has_init):
    if has_init:
        (xbc_ref, dt_ref, z_ref, init_ref, cw_ref, cb_ref, dtb_ref, alog_ref, dsk_ref, gs_ref,
         y_ref, xp_s, xc_s, ya_s, st_s) = refs
        fin_ref = None
    else:
        (xbc_ref, dt_ref, z_ref, cw_ref, cb_ref, dtb_ref, alog_ref, dsk_ref, gs_ref,
         y_ref, fin_ref, xp_s, xc_s, ya_s, st_s) = refs
        init_ref = None
    nc = L // CHUNK
    half = SSD_K // 2

    xp_s[0:PAD, :] = jnp.zeros((PAD, SSD_XBC), F32)
    xp_s[PAD + L:2 * PAD + L, :] = jnp.zeros((PAD, SSD_XBC), F32)
    xp_s[PAD:PAD + L, :] = xbc_ref[...]
    for c in range(nc):
        base = PAD + c * CHUNK - half
        for j in range(SSD_XBC // LANES):
            cols = slice(j * LANES, (j + 1) * LANES)
            acc = cb_ref[:, cols] + xp_s[base:base + CHUNK, cols] * cw_ref[0:1, cols]
            for k in range(1, SSD_K):
                acc = acc + xp_s[base + k:base + k + CHUNK, cols] * cw_ref[k:k + 1, cols]
            xc_s[c * CHUNK:(c + 1) * CHUNK, cols] = _silu(acc)

    row = lax.broadcasted_iota(jnp.int32, (CHUNK, CHUNK), 0)
    colm = lax.broadcasted_iota(jnp.int32, (CHUNK, CHUNK), 1)
    lane_lo = colm < SSD_P
    tri_lo = (colm <= row).astype(F32)
    tri_up = (colm >= row).astype(F32)

    for d in range(2):
        causal = (colm <= row) if d == 0 else (colm >= row)
        for j in range(SSD_H * SSD_P // CHUNK):
            if has_init:
                st_s[:, j * CHUNK:(j + 1) * CHUNK] = init_ref[0, d, j * CHUNK:(j + 1) * CHUNK, :].T
            else:
                st_s[:, j * CHUNK:(j + 1) * CHUNK] = jnp.zeros((CHUNK, CHUNK), F32)

        def chunk_body(ci, carry, d=d, causal=causal):
            c = ci if d == 0 else nc - 1 - ci
            r0 = pl.multiple_of(c * CHUNK, CHUNK)
            dt = jax.nn.softplus(dt_ref[pl.ds(r0, CHUNK), :] + dtb_ref[d:d + 1, :])
            a = dt * (-jnp.exp(alog_ref[d:d + 1, :]))
            tri = tri_lo if d == 0 else tri_up
            cs = jnp.dot(tri, a, precision=HIGHEST, preferred_element_type=F32)
            cs_t = jnp.dot(a.T, tri.T, precision=HIGHEST, preferred_element_type=F32)
            edge = cs[CHUNK - 1:CHUNK, :] if d == 0 else cs[0:1, :]
            for g in range(SSD_G):
                bm = xc_s[pl.ds(r0, CHUNK), D + g * SSD_N:D + (g + 1) * SSD_N]
                cm = xc_s[pl.ds(r0, CHUNK), D + (SSD_G + g) * SSD_N:D + (SSD_G + g + 1) * SSD_N]
                bm_b, cm_b = bm.astype(BF16), cm.astype(BF16)
                cb = lax.dot_general(cm_b, bm_b, (((1,), (1,)), ((), ())), preferred_element_type=F32)
                bm_t = bm.T.astype(BF16)
                pairs = SSD_H // SSD_G // 2
                for pp in range(pairs):
                    p = g * pairs + pp
                    h0, h1 = 2 * p, 2 * p + 1
                    cols = slice(p * CHUNK, (p + 1) * CHUNK)
                    xs = xc_s[pl.ds(r0, CHUNK), cols]
                    xdt = xs * jnp.where(lane_lo, dt[:, h0:h0 + 1], dt[:, h1:h1 + 1])
                    cs_b = [jnp.broadcast_to(cs[:, h:h + 1], (CHUNK, CHUNK)) for h in (h0, h1)]
                    ms = [cb * jnp.exp(jnp.where(causal, cs_b[e] - cs_t[h:h + 1, :], NEG)) for e, h in enumerate((h0, h1))]
                    cs_p = jnp.where(lane_lo, cs_b[0], cs_b[1])
                    edge_p = jnp.where(lane_lo[0:1, :], edge[:, h0:h0 + 1], edge[:, h1:h1 + 1])
                    mcat = jnp.concatenate(ms, axis=1).astype(BF16)
                    xbd = jnp.concatenate([jnp.where(lane_lo, xdt, 0.0), jnp.where(lane_lo, 0.0, xdt)],
                                          axis=0).astype(BF16)
                    y_diag = jnp.dot(mcat, xbd, preferred_element_type=F32)
                    st = st_s[:, cols]
                    y_off = jnp.dot(cm_b, st.astype(BF16), preferred_element_type=F32)
                    y_off = y_off * jnp.exp(cs_p)
                    y = y_diag + y_off
                    if d == 0:
                        ya_s[pl.ds(r0, CHUNK), cols] = y
                    else:
                        ya_s[pl.ds(r0, CHUNK), cols] = ya_s[pl.ds(r0, CHUNK), cols] + y
                    xdd = (xdt * jnp.exp(edge_p - cs_p)).astype(BF16)
                    snew = jnp.dot(bm_t, xdd, preferred_element_type=F32)
                    st_s[:, cols] = st * jnp.exp(edge_p) + snew
            return carry

        lax.fori_loop(0, nc, chunk_body, 0)
        if fin_ref is not None:
            for j in range(SSD_H * SSD_P // CHUNK):
                fin_ref[0, d, j * CHUNK:(j + 1) * CHUNK, :] = st_s[:, j * CHUNK:(j + 1) * CHUNK].T

    def out_body(c, carry):
        r0 = pl.multiple_of(c * CHUNK, CHUNK)
        y = ya_s[pl.ds(r0, CHUNK), :] + xc_s[pl.ds(r0, CHUNK), 0:D] * dsk_ref[...]
        y = y * _silu(z_ref[pl.ds(r0, CHUNK), :])
        y_ref[pl.ds(r0, CHUNK), :] = (_rms(y) * gs_ref[...]).astype(y_ref.dtype)
        return carry

    lax.fori_loop(0, nc, out_body, 0)


def ssd_mixer(xbc, dtr, z, init, cw, cb, dtb, alog, dsk, gs, *, L, n_seq, row_off):
    blk0 = row_off // L
    has_init = init is not None
    seq = lambda w: pl.BlockSpec((L, w), lambda b: (blk0 + b, 0))
    full = lambda arr: pl.BlockSpec(arr.shape, lambda b: (0,) * arr.ndim)
    in_specs = [seq(SSD_XBC), seq(LANES), seq(D)]
    args = [xbc, dtr, z]
    if has_init:
        in_specs.append(pl.BlockSpec((1, 2, SSD_H * SSD_P, SSD_N), lambda b: (b, 0, 0, 0)))
        args.append(init)
    small = [cw, cb, dtb, alog, dsk, gs]
    in_specs += [full(a) for a in small]
    args += small
    out_specs = [pl.BlockSpec((L, D), lambda b: (b, 0))]
    out_shape = [jax.ShapeDtypeStruct((n_seq * L, D), BF16)]
    if not has_init:
        out_specs.append(pl.BlockSpec((1, 2, SSD_H * SSD_P, SSD_N), lambda b: (b, 0, 0, 0)))
        out_shape.append(jax.ShapeDtypeStruct((n_seq, 2, SSD_H * SSD_P, SSD_N), F32))
    return pl.pallas_call(
        functools.partial(_ssd_kernel, L=L, has_init=has_init),
        grid=(n_seq,),
        in_specs=in_specs,
        out_specs=out_specs,
        out_shape=out_shape,
        scratch_shapes=[
            pltpu.VMEM((L + 2 * PAD, SSD_XBC), F32),
            pltpu.VMEM((L, SSD_XBC), F32),
            pltpu.VMEM((L, D), F32),
            pltpu.VMEM((SSD_N, SSD_H * SSD_P), F32),
        ],
        compiler_params=_cparams(("arbitrary",)),
        name=f"ssd_{L}",
    )(*args)


HY_CB = 256


def filter_dft_matrices(L):
    H = L // 2
    s = np.arange(L, dtype=np.int64)[None, :]
    k = np.arange(H, dtype=np.int64)[:, None]
    ang = lambda kk: ((kk * s) % (2 * L)).astype(np.float64) * (math.pi / L)
    ca, cb = np.cos(ang(k)), np.cos(ang(L - k))
    sa, sb = np.sin(ang(k)), np.sin(ang(L - k))
    cb[0] = np.where(s[0] % 2 == 0, 1.0, -1.0)
    sa[0], sb[0] = 0.0, 0.0
    fm = np.zeros((2 * SUBLANES, L))
    fm[0], fm[1] = np.cos(ang(H))[0], np.sin(ang(H))[0]
    mats = (np.concatenate([ca, cb], axis=0), np.concatenate([sa, sb], axis=0), fm)
    return tuple(jnp.asarray(m.astype(np.float32)).astype(BF16) for m in mats)


def _const_spec(arr):
    return pl.BlockSpec(arr.shape, lambda *_: (0,) * arr.ndim, pipeline_mode=pl.Buffered(1))


def _hy_filter_kernel(feat_ref, w1_ref, b1_ref, w2_ref, b2_ref, fr_ref, w3_ref, dl_ref, fs_ref, fd_ref, fm_ref,
                      h_ref, hm_ref, *, L):
    H = L // 2
    hp = functools.partial(jnp.dot, precision=HIGHEST, preferred_element_type=F32)
    hdn = jnp.sin(fr_ref[0:1, :] * (hp(feat_ref[...], w1_ref[...]) + b1_ref[...]))
    hdn = jnp.sin(fr_ref[1:2, :] * (hp(hdn, w2_ref[...]) + b2_ref[...]))
    rowi = lax.broadcasted_iota(jnp.int32, (L, 1), 0)
    t = rowi.astype(F32) * (1.0 / (L - 1))
    dec = jnp.exp(-t * dl_ref[...])
    first = rowi == 0
    for o in range(2):
        fwd = hp(hdn, w3_ref[2 * o]) * dec
        bwd = jnp.where(first, 0.0, hp(hdn, w3_ref[2 * o + 1]) * dec)
        hs, hd = (fwd + bwd).astype(BF16), (fwd - bwd).astype(BF16)
        ss = jnp.dot(fs_ref[...], hs, preferred_element_type=F32)
        sd = jnp.dot(fd_ref[...], hd, preferred_element_type=F32)
        h_ref[o, 0] = ss[0:H]
        h_ref[o, 1] = sd[0:H]
        h_ref[o, 2] = ss[H:L]
        h_ref[o, 3] = sd[H:L]
        mid_r = jnp.dot(fm_ref[...], hs, preferred_element_type=F32)
        mid_n = jnp.dot(fm_ref[...], hd, preferred_element_type=F32)
        hm_ref[o] = jnp.concatenate([mid_r[0:1], mid_n[1:2], jnp.zeros((SUBLANES - 2, mid_r.shape[1]), F32)], axis=0)


def hyena_filter_spectra(feat, w1, b1, w2, b2, freq, w3r, deltas, *, L):
    full = lambda arr: pl.BlockSpec(arr.shape, lambda j: (0,) * arr.ndim)
    mats = filter_dft_matrices(L)
    return pl.pallas_call(
        functools.partial(_hy_filter_kernel, L=L),
        grid=(D // HY_CB,),
        in_specs=[full(feat), full(w1), full(b1), full(w2), full(b2), full(freq),
                  pl.BlockSpec((4, HY_HID, HY_CB), lambda j: (0, 0, j)),
                  pl.BlockSpec((1, HY_CB), lambda j: (0, j))] + [_const_spec(m) for m in mats],
        out_specs=[pl.BlockSpec((2, 4, L // 2, HY_CB), lambda j: (0, 0, 0, j)),
                   pl.BlockSpec((2, SUBLANES, HY_CB), lambda j: (0, 0, j))],
        out_shape=[jax.ShapeDtypeStruct((2, 4, L // 2, D), F32), jax.ShapeDtypeStruct((2, SUBLANES, D), F32)],
        compiler_params=_cparams(("arbitrary",)),
        name=f"hy_filter_{L}",
    )(feat, w1, b1, w2, b2, freq, w3r, deltas, *mats)


def split_dft_matrices(L):
    H = L // 2
    k = np.arange(H, dtype=np.int64)[:, None]
    m = np.arange(H, dtype=np.int64)[None, :]
    alt = np.where(m % 2 == 0, 1.0, -1.0)
    ang_e = ((k * m) % L).astype(np.float64) * (2 * math.pi / L)
    ang_o = ((k * (2 * m + 1)) % (2 * L)).astype(np.float64) * (math.pi / L)
    ce, se, co, so = np.cos(ang_e), np.sin(ang_e), np.cos(ang_o), np.sin(ang_o)
    se[0], so[0] = alt[0], alt[0]
    w = np.where(k == 0, 1.0, 2.0) / (2 * L)
    fe = np.concatenate([ce, se], axis=0)
    fo = np.concatenate([co, so], axis=0)
    ge = np.concatenate([(ce * w).T, se.T / L], axis=1)
    go = np.concatenate([(co * w).T, so.T / L], axis=1)
    return tuple(jnp.asarray(a.astype(np.float32)).astype(BF16) for a in (fe, fo, ge, go))


def _store_lane_blocks(ref, val):
    for c in range(ref.shape[0]):
        ref[c] = val[:, c * LANES:(c + 1) * LANES]


def _load_parity(ref, parity, n):
    return jnp.concatenate([ref[c, pl.ds(parity, n, stride=2), :] for c in range(ref.shape[0])], axis=1)


def _hyena_kernel(p0_ref, p1_ref, p2_ref, w0_ref, w1_ref, w2_ref, b0_ref, b1_ref, b2_ref, h_ref, hm_ref, hb_ref,
                  fe_ref, fo_ref, ge_ref, go_ref, o_ref, xp_s, u_s, y_s, *, L, cb):
    H = L // 2
    xp_s[0:PAD, :] = jnp.zeros((PAD, cb), F32)
    xp_s[PAD + L:2 * PAD + L, :] = jnp.zeros((PAD, cb), F32)
    first = lax.broadcasted_iota(jnp.int32, (H, 1), 0) == 0

    def conv(p_ref, w_ref, b_ref):
        xp_s[PAD:PAD + L, :] = p_ref[...]
        acc = b_ref[...] + xp_s[PAD - 1:PAD - 1 + L, :] * w_ref[0:1, :]
        for k in range(1, HY_K):
            acc = acc + xp_s[PAD - 1 + k:PAD - 1 + k + L, :] * w_ref[k:k + 1, :]
        return acc

    u = conv(p0_ref, w0_ref, b0_ref)
    for o, (p_ref, w_ref, b_ref) in enumerate(((p1_ref, w1_ref, b1_ref), (p2_ref, w2_ref, b2_ref))):
        _store_lane_blocks(u_s, u)
        se = jnp.dot(fe_ref[...], _load_parity(u_s, 0, H).astype(BF16), preferred_element_type=F32)
        so = jnp.dot(fo_ref[...], _load_parity(u_s, 1, H).astype(BF16), preferred_element_type=F32)
        e, es, od, os_ = se[0:H], se[H:L], so[0:H], so[H:L]
        b0, b1 = e + od, e - od
        b2 = jnp.where(first, es, es + os_)
        b3 = jnp.where(first, os_, os_ - es)
        har, han, hbr, hbn = h_ref[o, 0], h_ref[o, 1], h_ref[o, 2], h_ref[o, 3]
        hmr, hmn = hm_ref[o, 0:1, :], hm_ref[o, 1:2, :]
        y0 = b0 * har - b2 * han
        y1 = b1 * hbr - b3 * hbn
        y2 = b0 * han + b2 * har
        y3 = b1 * hbn + b3 * hbr
        mid_r = b2[0:1] * hmr - b3[0:1] * hmn
        mid_n = b2[0:1] * hmn + b3[0:1] * hmr
        de = jnp.where(first, mid_r, y2 - y3)
        do = jnp.where(first, mid_n, y2 + y3)
        ye = jnp.dot(ge_ref[...], jnp.concatenate([y0 + y1, de], axis=0).astype(BF16), preferred_element_type=F32)
        yo = jnp.dot(go_ref[...], jnp.concatenate([y0 - y1, do], axis=0).astype(BF16), preferred_element_type=F32)
        for c in range(cb // LANES):
            y_s[c, pl.ds(0, H, stride=2), :] = ye[:, c * LANES:(c + 1) * LANES]
            y_s[c, pl.ds(1, H, stride=2), :] = yo[:, c * LANES:(c + 1) * LANES]
        y = jnp.concatenate([y_s[c] for c in range(cb // LANES)], axis=1)
        u = conv(p_ref, w_ref, b_ref) * (y + u * hb_ref[o:o + 1, :])
    o_ref[...] = u.astype(o_ref.dtype)


def hyena_mixer(hy, conv_w, conv_b, h4, hm, hy_bias, *, L, n_seq, row_off):
    blk0 = row_off // L
    cb = min(D, HY_CB * (L_LAT // L))
    nj = D // cb
    H = L // 2
    part = lambda q: pl.BlockSpec((L, cb), lambda j, b: (blk0 + b, q * nj + j))
    wpart = lambda q: pl.BlockSpec((HY_K, cb), lambda j, b: (0, q * nj + j))
    bpart = lambda q: pl.BlockSpec((1, cb), lambda j, b: (0, q * nj + j))
    mats = split_dft_matrices(L)
    return pl.pallas_call(
        functools.partial(_hyena_kernel, L=L, cb=cb),
        grid=(nj, n_seq),
        in_specs=[part(0), part(1), part(2), wpart(0), wpart(1), wpart(2), bpart(0), bpart(1), bpart(2),
                  pl.BlockSpec((2, 4, H, cb), lambda j, b: (0, 0, 0, j)),
                  pl.BlockSpec((2, SUBLANES, cb), lambda j, b: (0, 0, j)),
                  pl.BlockSpec((2, cb), lambda j, b: (0, j))]
                 + [_const_spec(m) for m in mats],
        out_specs=pl.BlockSpec((L, cb), lambda j, b: (b, j)),
        out_shape=jax.ShapeDtypeStruct((n_seq * L, D), BF16),
        scratch_shapes=[pltpu.VMEM((L + 2 * PAD, cb), F32), pltpu.VMEM((cb // LANES, L, LANES), F32),
                        pltpu.VMEM((cb // LANES, L, LANES), F32)],
        compiler_params=_cparams(("arbitrary", "arbitrary")),
        name=f"hyena_{L}",
    )(hy, hy, hy, conv_w, conv_w, conv_w, conv_b, conv_b, conv_b, h4, hm, hy_bias, *mats)


ROUTER_LANES = LANES
BIG_LANE = 1e9


ROW_GROUP = D // LANES


def _store_row_groups(ref, val):
    n = val.shape[0]
    for s in range(ROW_GROUP):
        ref[pl.ds(s, n, stride=ROW_GROUP), :] = val[:, s * LANES:(s + 1) * LANES]


def _load_row_groups(ref, n, s):
    return ref[pl.ds(s, n, stride=ROW_GROUP), :]


def _first_max_lane(v, lanef):
    m = jnp.max(v, axis=-1, keepdims=True)
    return m, jnp.min(jnp.where(v == m, lanef, BIG_LANE), axis=-1, keepdims=True)


def _out_router_kernel(*refs, n_in, x_is_pair):
    a_refs = refs[:2 * n_in]
    refs = refs[2 * n_in:]
    is_ctx = pl.program_id(0) < CTX_TILES
    if x_is_pair:
        x = jnp.where(is_ctx, refs[0][...], refs[1][...])
        refs = refs[2:]
    else:
        x = refs[0][...]
        refs = refs[1:]
    w_ref, mod_ref, gf_ref, wr_ref, br_ref, xo_ref, h2_ref, ids_ref, wts_ref, cnt_ref = refs
    acc, k0 = None, 0
    for ac_ref, al_ref in zip(a_refs[0::2], a_refs[1::2]):
        kk = ac_ref.shape[1]
        a = jnp.where(is_ctx, ac_ref[...], al_ref[...])
        part = jnp.dot(a, w_ref[k0:k0 + kk, :], preferred_element_type=F32)
        acc = part if acc is None else acc + part
        k0 += kk
    xn = x + mod_ref[0, 2:3, :] * acc
    xo_ref[...] = xn
    h2 = _modulated(xn, gf_ref, mod_ref, 3)
    h2_ref[...] = h2

    h_hi = h2.astype(BF16)
    h_lo = (h2 - h_hi.astype(F32)).astype(BF16)
    logits = (jnp.dot(h_hi, wr_ref[0], preferred_element_type=F32) + jnp.dot(h_lo, wr_ref[0], preferred_element_type=F32)
              + jnp.dot(h_hi, wr_ref[1], preferred_element_type=F32) + br_ref[...])
    lanef = lax.broadcasted_iota(jnp.int32, logits.shape, 1).astype(F32)
    gl = jnp.where(lanef < MOE_G, logits, NEG)
    gm, gi = _first_max_lane(gl, lanef)
    g_w = 1.0 / jnp.sum(jnp.exp(gl - gm), axis=-1, keepdims=True)
    lo = MOE_G + MOE_PG * gi
    el = jnp.where((lanef >= lo) & (lanef < lo + MOE_PG), logits, NEG)
    m1, e1 = _first_max_lane(el, lanef)
    m2, e2 = _first_max_lane(jnp.where(lanef == e1, NEG, el), lanef)
    p2 = jnp.exp(m2 - m1)
    w1 = g_w / (1.0 + p2)
    ids_ref[...] = jnp.where(lanef == 0, e1 - MOE_G, jnp.where(lanef == 1, e2 - MOE_G, 0.0)).astype(jnp.int32)
    wts_ref[...] = jnp.where(lanef == 0, w1, jnp.where(lanef == 1, w1 * p2, 0.0))
    chosen = ((lanef == e1 - MOE_G) | (lanef == e2 - MOE_G)).astype(F32)
    cnt_ref[0] = jnp.sum(chosen, axis=0, keepdims=True).astype(jnp.int32)


def out_proj_router(acts, w_bf, x, mod, gf, wr, br):
    tile = lambda w: pl.BlockSpec((TM, w), lambda i: (i, 0))
    full = lambda arr: pl.BlockSpec(arr.shape, lambda i: (0,) * arr.ndim)
    x_is_pair = isinstance(x, tuple)
    xs = x if x_is_pair else (x,)
    return pl.pallas_call(
        functools.partial(_out_router_kernel, n_in=len(acts), x_is_pair=x_is_pair),
        grid=(N_TILES,),
        in_specs=[s for a in acts for s in _pair_specs(a[0].shape[1])]
                 + (_pair_specs(D) if x_is_pair else [tile(D)])
                 + [full(w_bf), pl.BlockSpec((1, 6, D), lambda i: (_mod_row(i), 0, 0)), full(gf), full(wr), full(br)],
        out_specs=[tile(D), tile(D), tile(ROUTER_LANES), tile(ROUTER_LANES),
                   pl.BlockSpec((1, 1, ROUTER_LANES), lambda i: (i, 0, 0))],
        out_shape=[jax.ShapeDtypeStruct((T_ALL, D), F32), jax.ShapeDtypeStruct((T_ALL, D), F32),
                   jax.ShapeDtypeStruct((T_ALL, ROUTER_LANES), jnp.int32),
                   jax.ShapeDtypeStruct((T_ALL, ROUTER_LANES), F32),
                   jax.ShapeDtypeStruct((N_TILES, 1, ROUTER_LANES), jnp.int32)],
        compiler_params=_cparams(("arbitrary",)),
        name="out_router",
    )(*[part for a in acts for part in a], *xs, w_bf, mod, gf, wr, br)


N_ASSIGN = 2 * T_ALL
MOE_TILES = N_ASSIGN // TM + MOE_E
N_SLOTS = MOE_TILES * TM


def route_tables(cnt3):
    cnt = cnt3[:, 0, :MOE_E]
    total = jnp.sum(cnt, axis=0)
    padded = (total + TM - 1) // TM * TM
    ends = jnp.cumsum(padded)
    gdst = (ends - padded)[None, :] + jnp.cumsum(cnt, axis=0) - cnt
    loc = jnp.cumsum(cnt, axis=1) - cnt
    starts = jnp.arange(MOE_TILES, dtype=jnp.int32) * TM
    tile_expert = jnp.minimum(jnp.sum((ends[None, :] <= starts[:, None]).astype(jnp.int32), axis=1), MOE_E - 1)
    n_used = (ends[-1] // TM).astype(jnp.int32).reshape(1)
    return cnt, loc, gdst, ends, tile_expert, n_used


RUN_BITS = (2 * TM).bit_length()
RUN_SMALL_BITS = 6


def _dispatch_kernel(cnt_s, loc_s, gdst_s, ends_s, h_ref, ids_ref, gcol_ref, xs_ref, dest_ref, srt, zbuf, sem, zsem):
    i = pl.program_id(0)
    slot = i % 2
    n_rows = 2 * TM

    @pl.when(i == 0)
    def _():
        zbuf[...] = jnp.zeros(zbuf.shape, zbuf.dtype)
        n_used = ends_s[MOE_E - 1] // TM
        for phase in ("start", "wait"):
            def tail(t, c, phase=phase):
                dst = pl.multiple_of(t * (TM * ROW_GROUP), TM * ROW_GROUP)
                cp = pltpu.make_async_copy(zbuf, xs_ref.at[pl.ds(dst, TM * ROW_GROUP), :], zsem)
                cp.start() if phase == "start" else cp.wait()
                return c

            lax.fori_loop(n_used, MOE_TILES, tail, 0)
            for e in range(MOE_E):
                end = ends_s[e]
                prev = ends_s[e - 1] if e > 0 else 0

                @pl.when(end > prev)
                def _(end=end, phase=phase):
                    dst = pl.multiple_of((end - TM) * ROW_GROUP, TM * ROW_GROUP)
                    cp = pltpu.make_async_copy(zbuf, xs_ref.at[pl.ds(dst, TM * ROW_GROUP), :], zsem)
                    cp.start() if phase == "start" else cp.wait()

    idt = ids_ref[...].astype(F32).T
    sub = lax.broadcasted_iota(jnp.int32, (LANES, TM), 0).astype(F32)
    m0 = (sub == idt[0:1, :]).astype(F32)
    m1 = (sub == idt[1:2, :]).astype(F32)
    mt = (m0 + m1).astype(BF16)
    tr = lax.broadcasted_iota(jnp.int32, (TM, TM), 0)
    tc = lax.broadcasted_iota(jnp.int32, (TM, TM), 1)
    earlier = jnp.dot(mt, (tr < tc).astype(BF16), preferred_element_type=F32)
    er = lax.broadcasted_iota(jnp.int32, (LANES, LANES), 0)
    ec = lax.broadcasted_iota(jnp.int32, (LANES, LANES), 1)
    below = jnp.dot((ec < er).astype(BF16), mt, preferred_element_type=F32)
    local = jnp.sum(below, axis=1, keepdims=True) + earlier
    glob = gcol_ref[0] + earlier
    pos0 = jnp.sum(m0 * local, axis=0, keepdims=True)
    pos1 = jnp.sum(m1 * local, axis=0, keepdims=True)
    dest_ref[0] = jnp.concatenate([jnp.sum(m0 * glob, axis=0, keepdims=True),
                                   jnp.sum(m1 * glob, axis=0, keepdims=True)], axis=0).astype(jnp.int32)

    srow = lax.broadcasted_iota(jnp.int32, (n_rows, TM), 0).astype(F32)
    perm = jnp.where((srow == pos0) | (srow == pos1), 1.0, 0.0).astype(BF16)
    _store_row_groups(srt.at[slot], jnp.dot(perm, h_ref[...].astype(BF16), preferred_element_type=F32))

    def run_pieces(n, s0, d0, bits):
        for b in bits:
            size = 1 << b
            off = (n >> (b + 1)) << (b + 1)

            @pl.when(((n >> b) & 1) == 1)
            def _(size=size, off=off):
                src = pl.multiple_of((s0 + off) * ROW_GROUP, ROW_GROUP)
                dst = pl.multiple_of((d0 + off) * ROW_GROUP, ROW_GROUP)
                pltpu.make_async_copy(srt.at[slot, pl.ds(src, size * ROW_GROUP), :],
                                      xs_ref.at[pl.ds(dst, size * ROW_GROUP), :], sem.at[slot]).start()

    for e in range(MOE_E):
        n, s0, d0 = cnt_s[0, 0, e], loc_s[0, 0, e], gdst_s[0, 0, e]

        @pl.when(n >= (1 << RUN_SMALL_BITS))
        def _(n=n, s0=s0, d0=d0):
            run_pieces(n, s0, d0, reversed(range(RUN_SMALL_BITS, RUN_BITS)))

        run_pieces(n, s0, d0, reversed(range(RUN_SMALL_BITS)))

    def wait(s):
        pltpu.make_async_copy(srt.at[s], xs_ref.at[pl.ds(0, n_rows * ROW_GROUP), :], sem.at[s]).wait()

    @pl.when(i > 0)
    def _():
        wait(1 - slot)

    @pl.when(i == N_TILES - 1)
    def _():
        wait(slot)


def dispatch_rows(h2, ids, cnt, loc, gdst, ends):
    tab = lambda: pl.BlockSpec((1, 1, MOE_E), lambda i: (i, 0, 0), memory_space=pltpu.SMEM)
    gcol = jnp.pad(gdst.astype(F32), ((0, 0), (0, LANES - MOE_E)))[:, :, None]
    return pl.pallas_call(
        _dispatch_kernel,
        grid=(N_TILES,),
        in_specs=[tab(), tab(), tab(), pl.BlockSpec(memory_space=pltpu.SMEM),
                  pl.BlockSpec((TM, D), lambda i: (i, 0)), pl.BlockSpec((TM, ROUTER_LANES), lambda i: (i, 0)),
                  pl.BlockSpec((1, LANES, 1), lambda i: (i, 0, 0))],
        out_specs=[pl.BlockSpec(memory_space=pl.ANY), pl.BlockSpec((1, 2, TM), lambda i: (i, 0, 0))],
        out_shape=[jax.ShapeDtypeStruct((N_SLOTS * ROW_GROUP, LANES), F32),
                   jax.ShapeDtypeStruct((N_TILES, 2, TM), jnp.int32)],
        scratch_shapes=[pltpu.VMEM((2, 2 * TM * ROW_GROUP, LANES), F32), pltpu.VMEM((TM * ROW_GROUP, LANES), F32),
                        pltpu.SemaphoreType.DMA((2,)), pltpu.SemaphoreType.DMA(())],
        compiler_params=_cparams(("arbitrary",)),
        name="moe_dispatch",
    )(cnt.reshape(N_TILES, 1, MOE_E), loc.reshape(N_TILES, 1, MOE_E), gdst.reshape(N_TILES, 1, MOE_E),
      ends, h2, ids, gcol)


DMA_UNROLL = 8


def _start_group_gather(src_hbm, idx_ref, n, dst_ref, sem):
    def body(j, c):
        for u in range(DMA_UNROLL):
            r = j * DMA_UNROLL + u
            src = pl.multiple_of(idx_ref[0, 0, r] * ROW_GROUP, ROW_GROUP)
            dst = pl.multiple_of(r * ROW_GROUP, ROW_GROUP)
            pltpu.make_async_copy(src_hbm.at[pl.ds(src, ROW_GROUP), :], dst_ref.at[pl.ds(dst, ROW_GROUP), :],
                                  sem).start(priority=u % 2)
        return c

    lax.fori_loop(0, n // DMA_UNROLL, body, 0)


def _wait_group_gather(src_hbm, dst_ref, sem):
    pltpu.make_async_copy(src_hbm.at[pl.ds(0, dst_ref.shape[0]), :], dst_ref, sem).wait()


def _experts_kernel(te_ref, nu_ref, x_ref, wg_ref, wu_ref, wd_ref, o_ref, xcat):
    i = pl.program_id(0)

    @pl.when(i < nu_ref[0])
    def _():
        for s in range(ROW_GROUP):
            xcat[:, s * LANES:(s + 1) * LANES] = _load_row_groups(x_ref, TM, s).astype(BF16)
        x = xcat[...]
        g = jnp.dot(x, wg_ref[0, 0].astype(BF16), preferred_element_type=F32)
        u = jnp.dot(x, wu_ref[0, 0].astype(BF16), preferred_element_type=F32)
        hid = (_silu(g) * u).astype(BF16)
        _store_row_groups(o_ref, jnp.dot(hid, wd_ref[0, 0].astype(BF16), preferred_element_type=F32))

    @pl.when(i >= nu_ref[0])
    def _():
        o_ref[...] = jnp.zeros(o_ref.shape, o_ref.dtype)


def grouped_experts(xs, w_gate, w_up, w_down, tile_expert, n_used, layer):
    wspec = lambda a, b: pl.BlockSpec((1, 1, a, b), lambda i, te, nu: (layer, te[i], 0, 0))
    return pl.pallas_call(
        _experts_kernel,
        grid_spec=pltpu.PrefetchScalarGridSpec(
            num_scalar_prefetch=2,
            grid=(MOE_TILES,),
            in_specs=[pl.BlockSpec((TM * ROW_GROUP, LANES), lambda i, te, nu: (jnp.minimum(i, nu[0] - 1), 0)),
                      wspec(D, MOE_F), wspec(D, MOE_F), wspec(MOE_F, D)],
            out_specs=pl.BlockSpec((TM * ROW_GROUP, LANES), lambda i, te, nu: (i, 0)),
            scratch_shapes=[pltpu.VMEM((TM, D), BF16)],
        ),
        out_shape=jax.ShapeDtypeStruct((N_SLOTS * ROW_GROUP, LANES), F32),
        compiler_params=_cparams(("arbitrary",)),
        name="moe_experts",
    )(tile_expert, n_used, xs, w_gate, w_up, w_down)


def _combine_kernel(cur_ref, nxt_ref, ys_hbm, x_ref, wts_ref, mod_ref, gfin_ref, *rest, final):
    *o_refs, buf, sem = rest
    i = pl.program_id(0)
    slot = i % 2

    @pl.when(i == 0)
    def _():
        _start_group_gather(ys_hbm, cur_ref, 2 * TM, buf.at[0], sem.at[0])

    @pl.when(i + 1 < N_TILES)
    def _():
        _start_group_gather(ys_hbm, nxt_ref, 2 * TM, buf.at[1 - slot], sem.at[1 - slot])

    _wait_group_gather(ys_hbm, buf.at[slot], sem.at[slot])
    w0, w1 = wts_ref[:, 0:1], wts_ref[:, 1:2]

    def finish(o_ref):
        for s in range(ROW_GROUP):
            cols = slice(s * LANES, (s + 1) * LANES)
            y0 = buf[slot, pl.ds(s, TM, stride=ROW_GROUP), :]
            y1 = buf[slot, pl.ds(TM * ROW_GROUP + s, TM, stride=ROW_GROUP), :]
            o_ref[:, cols] = x_ref[:, cols] + mod_ref[0, 5:6, cols] * (w0 * y0 + w1 * y1)
        if final:
            o_ref[...] = _rms(o_ref[...]) * gfin_ref[...]

    if final:
        pl.when(i < CTX_TILES)(lambda: finish(o_refs[0]))
        pl.when(i >= CTX_TILES)(lambda: finish(o_refs[1]))
    else:
        finish(o_refs[0])


def moe_combine(ys, dest, x, wts, mod, gfin, *, final):
    tile = lambda w: pl.BlockSpec((TM, w), lambda i: (i, 0))
    if final:
        out_specs = _pair_specs(D)
        out_shape = [jax.ShapeDtypeStruct((T_CTX, D), F32), jax.ShapeDtypeStruct((T_LAT, D), F32)]
    else:
        out_specs = [tile(D)]
        out_shape = [jax.ShapeDtypeStruct((T_ALL, D), F32)]
    idx = lambda f: pl.BlockSpec((1, 1, 2 * TM), lambda i: (f(i), 0, 0), memory_space=pltpu.SMEM)
    dest3 = dest.reshape(N_TILES, 1, 2 * TM)
    out = pl.pallas_call(
        functools.partial(_combine_kernel, final=final),
        grid=(N_TILES,),
        in_specs=[idx(lambda i: i), idx(lambda i: jnp.minimum(i + 1, N_TILES - 1)),
                  pl.BlockSpec(memory_space=pl.ANY), tile(D), tile(ROUTER_LANES),
                  pl.BlockSpec((1, 6, D), lambda i: (_mod_row(i), 0, 0)),
                  pl.BlockSpec((1, D), lambda i: (0, 0))],
        out_specs=out_specs,
        out_shape=out_shape,
        scratch_shapes=[pltpu.VMEM((2, 2 * TM * ROW_GROUP, LANES), F32), pltpu.SemaphoreType.DMA((2,))],
        compiler_params=_cparams(("arbitrary",)),
        name="moe_combine",
    )(dest3, dest3, ys, x, wts, mod, gfin)
    return out if final else out[0]


ODD_COLS = 2048
ROPE_Q = MLA_H * MLA_ROPE
ROPE_SHIFT = ROPE_F


def rope_tables():
    t = np.arange(L_LAT)
    pos = np.stack([t // GRID_W, t % GRID_W], axis=1).astype(np.float64)
    inv = 10000.0 ** (-np.arange(ROPE_F, dtype=np.float64) / ROPE_F)
    lane = np.arange(ROPE_Q) % MLA_ROPE
    axis = lane // (2 * ROPE_F)
    first = (lane % (2 * ROPE_F)) < ROPE_F
    ang = pos[:, axis] * inv[lane % ROPE_F][None, :]
    cos, sin = np.cos(ang), np.sin(ang)
    tabs = [cos, np.where(first[None, :], -sin, 0.0), np.where(first[None, :], 0.0, sin)]
    ident = [np.ones((1, TM, ROPE_Q)), np.zeros((1, TM, ROPE_Q)), np.zeros((1, TM, ROPE_Q))]
    return [jnp.asarray(np.concatenate([i, tb.reshape(LAT_TILES_PER_SEQ, TM, ROPE_Q)], axis=0).astype(np.float32))
            for i, tb in zip(ident, tabs)]


def _rope(x, c, a, b):
    n = x.shape[1]
    return x * c[:, :n] + pltpu.roll(x, n - ROPE_SHIFT, 1) * a[:, :n] + pltpu.roll(x, ROPE_SHIFT, 1) * b[:, :n]


def _odd_in_kernel(x_ref, mod_ref, g_ref, w_ref, gq_ref, wuq_ref, gkv_ref, wukv_ref, rc_ref, ra_ref, rb_ref,
                   q_ref, kv_ref, qm_ref, ckv_ref, kvu_ref, kr_ref, knew_ref, vnew_ref):
    hb = _modulated(x_ref[...], g_ref, mod_ref, 0).astype(BF16)
    q_ref[...] = jnp.dot(hb, w_ref[:, 0:NA_W], preferred_element_type=F32)
    k = jnp.dot(hb, w_ref[:, NA_W:2 * NA_W], preferred_element_type=F32)
    v = jnp.dot(hb, w_ref[:, 2 * NA_W:3 * NA_W], preferred_element_type=F32)
    kv_ref[:, 0:NA_W] = k.astype(kv_ref.dtype)
    kv_ref[:, NA_W:2 * NA_W] = v.astype(kv_ref.dtype)

    @pl.when(pl.program_id(0) < CTX_TILES)
    def _():
        for h in range(NA_H):
            knew_ref[0, h] = k[:, h * NA_D:(h + 1) * NA_D]
            vnew_ref[0, h] = v[:, h * NA_D:(h + 1) * NA_D]

    rest = jnp.dot(hb, w_ref[:, 3 * NA_W:ODD_COLS], preferred_element_type=F32)
    rc, ra, rb = rc_ref[0], ra_ref[0], rb_ref[0]
    qd = (_rms(rest[:, 0:MLA_QR]) * gq_ref[...]).astype(BF16)
    qm = jnp.dot(qd, wuq_ref[...], preferred_element_type=F32)
    qm_ref[:, 0:MLA_H * MLA_NOPE] = qm[:, 0:MLA_H * MLA_NOPE]
    qm_ref[:, MLA_H * MLA_NOPE:] = _rope(qm[:, MLA_H * MLA_NOPE:], rc, ra, rb)
    ckv = _rms(rest[:, MLA_QR:MLA_QR + MLA_KVR]) * gkv_ref[...]
    ckv_ref[...] = ckv
    kvu_ref[...] = jnp.dot(ckv.astype(BF16), wukv_ref[...],
                           preferred_element_type=F32).astype(kvu_ref.dtype)
    kr_ref[...] = _rope(rest[:, MLA_QR + MLA_KVR:], rc, ra, rb)


def odd_in_proj(x, mod, g, w_bf, gq, wuq_bf, gkv, wukv_bf, tabs):
    tile = lambda w: pl.BlockSpec((TM, w), lambda i: (i, 0))
    full = lambda arr: pl.BlockSpec(arr.shape, lambda i: (0,) * arr.ndim)
    tab = pl.BlockSpec((1, TM, ROPE_Q),
                       lambda i: (jnp.where(i < CTX_TILES, 0, 1 + (i - CTX_TILES) % LAT_TILES_PER_SEQ), 0, 0))
    outs = ((NA_W, F32), (2 * NA_W, BF16), (MLA_H * MLA_QK, F32), (MLA_KVR, F32),
            (MLA_H * (MLA_NOPE + MLA_V), BF16), (LANES, F32))
    cache = pl.BlockSpec((1, NA_H, L_CTX, NA_D), lambda i: (jnp.minimum(i, CTX_TILES - 1), 0, 0, 0))
    cache_shape = jax.ShapeDtypeStruct((N_CTX, NA_H, L_CTX, NA_D), F32)
    return pl.pallas_call(
        _odd_in_kernel,
        grid=(N_TILES,),
        in_specs=[tile(D), pl.BlockSpec((1, 6, D), lambda i: (_mod_row(i), 0, 0)), full(g), full(w_bf),
                  full(gq), full(wuq_bf), full(gkv), full(wukv_bf), tab, tab, tab],
        out_specs=[tile(w) for w, _ in outs] + [cache, cache],
        out_shape=[jax.ShapeDtypeStruct((T_ALL, w), dt) for w, dt in outs] + [cache_shape, cache_shape],
        compiler_params=_cparams(("arbitrary",)),
        name="odd_in",
    )(x, mod, g, w_bf, gq, wuq_bf, gkv, wukv_bf, *tabs)


LOG2E = math.log2(math.e)
NA_QSCALE = NA_D ** -0.5 * LOG2E
MLA_QSCALE = MLA_QK ** -0.5 * LOG2E
NT = (((1,), (1,)), ((), ()))


def _softmax_pv(scores, values):
    m = functools.reduce(jnp.maximum, [jnp.max(s, axis=-1, keepdims=True) for s in scores])
    ps = [jnp.exp2(s - m) for s in scores]
    den = functools.reduce(jnp.add, [jnp.sum(p, axis=-1, keepdims=True) for p in ps])
    acc = functools.reduce(jnp.add, [jnp.dot(p.astype(BF16), v, preferred_element_type=F32) for p, v in zip(ps, values)])
    return acc / den


def _pair(ref_or_val, p, base=0):
    return ref_or_val[:, base + p * LANES:base + (p + 1) * LANES]


def _low_half():
    return lax.broadcasted_iota(jnp.int32, (1, LANES), 1) < NA_D


def _rope_key_forms(kr):
    return kr.astype(BF16), pltpu.roll(kr, LANES // 2, 1).astype(BF16)


def _mla_pair(qm, p, sources, lo):
    outs = []
    for e in range(2):
        h = 2 * p + e
        qn = qm[:, h * MLA_NOPE:(h + 1) * MLA_NOPE] * MLA_QSCALE
        qr = qm[:, MLA_H * MLA_NOPE + h * MLA_ROPE:MLA_H * MLA_NOPE + (h + 1) * MLA_ROPE] * MLA_QSCALE
        z = jnp.zeros((qn.shape[0], LANES - MLA_QK), F32)
        qcat = jnp.concatenate([qn, qr, z] if e == 0 else [qr, z, qn], axis=1).astype(BF16)
        scores = []
        for kb, kr_lo, kr_hi, _ in sources:
            kcat = jnp.where(lo, kb, kr_hi) if e == 0 else jnp.where(lo, kr_lo, kb)
            scores.append(lax.dot_general(qcat, kcat, NT, preferred_element_type=F32))
        outs.append(_softmax_pv(scores, [src[3] for src in sources]))
    return jnp.where(lo, outs[0], outs[1])


def _attn_ctx_kernel(q_ref, kv_ref, qm_ref, kvu_ref, kr_ref, ona_ref, omla_ref):
    lo = _low_half()
    for p in range(NA_H // 2):
        qb = _pair(q_ref, p) * NA_QSCALE
        kb = _pair(kv_ref, p).astype(BF16)
        vb = _pair(kv_ref, p, NA_W).astype(BF16)
        outs = []
        for e in range(2):
            q = jnp.where(lo if e == 0 else jnp.logical_not(lo), qb, 0.0).astype(BF16)
            outs.append(_softmax_pv([lax.dot_general(q, kb, NT, preferred_element_type=F32)], [vb]))
        ona_ref[:, p * LANES:(p + 1) * LANES] = jnp.where(lo, outs[0], outs[1]).astype(ona_ref.dtype)
    kr_lo, kr_hi = _rope_key_forms(kr_ref[...])
    for p in range(MLA_H // 2):
        src = (_pair(kvu_ref, p).astype(BF16), kr_lo, kr_hi, _pair(kvu_ref, p, MLA_H * MLA_NOPE).astype(BF16))
        omla_ref[:, p * LANES:(p + 1) * LANES] = _mla_pair(qm_ref, p, [src], lo).astype(omla_ref.dtype)


def attn_context(q, kv, qm, kvu, kr):
    seq = lambda w: pl.BlockSpec((L_CTX, w), lambda b: (b, 0))
    return pl.pallas_call(
        _attn_ctx_kernel,
        grid=(N_CTX,),
        in_specs=[seq(NA_W), seq(2 * NA_W), seq(MLA_H * MLA_QK), seq(MLA_H * (MLA_NOPE + MLA_V)), seq(LANES)],
        out_specs=[seq(NA_W), seq(MLA_H * MLA_V)],
        out_shape=[jax.ShapeDtypeStruct((T_CTX, NA_W), BF16), jax.ShapeDtypeStruct((T_CTX, MLA_H * MLA_V), BF16)],
        compiler_params=_cparams(("arbitrary",)),
        name="attn_ctx",
    )(q, kv, qm, kvu, kr)


N_DR = 2 * NA_WIN_R - 1
GRID_ROWS = L_LAT // GRID_W


def _na_bias_kernel(t_ref, o_ref):
    neg = jnp.full((GRID_W, GRID_W), NEG, F32)
    for r in range(GRID_ROWS):
        r0 = min(max(r - NA_WIN_R // 2, 0), GRID_ROWS - NA_WIN_R)
        for kr in range(GRID_ROWS):
            in_window = r0 <= kr < r0 + NA_WIN_R
            blk = t_ref[0, kr - r + NA_WIN_R - 1] if in_window else neg
            o_ref[0, r * GRID_W:(r + 1) * GRID_W, kr * GRID_W:(kr + 1) * GRID_W] = blk


def neighbourhood_bias(rel_bias):
    c = np.arange(GRID_W)
    c0 = np.clip(c - NA_WIN_C // 2, 0, GRID_W - NA_WIN_C)
    col_ok = (c[None, :] >= c0[:, None]) & (c[None, :] < c0[:, None] + NA_WIN_C)
    dc = np.clip(c[None, :] - c[:, None], -(NA_WIN_C - 1), NA_WIN_C - 1) + NA_WIN_C - 1
    sel_c = (dc[:, :, None] == np.arange(2 * NA_WIN_C - 1)).astype(np.float32)
    t = jnp.einsum("hdj,qcj->hdqc", rel_bias.astype(F32), jnp.asarray(sel_c), precision=HIGHEST)
    t = jnp.where(jnp.asarray(col_ok)[None, None], t * LOG2E, NEG)
    return pl.pallas_call(
        _na_bias_kernel,
        grid=(NA_H,),
        in_specs=[pl.BlockSpec((1, N_DR, GRID_W, GRID_W), lambda h: (h, 0, 0, 0))],
        out_specs=pl.BlockSpec((1, L_LAT, L_LAT), lambda h: (h, 0, 0)),
        out_shape=jax.ShapeDtypeStruct((NA_H, L_LAT, L_LAT), F32),
        compiler_params=_cparams(("arbitrary",)),
        name="na_bias",
    )(t)


def _na_lat_kernel(q_ref, k_ref, v_ref, kc_ref, vc_ref, b_ref, o_ref):
    lo = _low_half()
    for p in range(NA_H // 2):
        qb = _pair(q_ref, p) * NA_QSCALE
        kb = _pair(k_ref, p).astype(BF16)
        vb = _pair(v_ref, p).astype(BF16)
        outs = []
        for e in range(2):
            h = 2 * p + e
            half = slice(e * NA_D, (e + 1) * NA_D)
            q = jnp.where(lo if e == 0 else jnp.logical_not(lo), qb, 0.0).astype(BF16)
            s1 = lax.dot_general(q, kb, NT, preferred_element_type=F32) + b_ref[h]
            s2 = lax.dot_general(qb[:, half].astype(BF16), kc_ref[0, 0, h].astype(BF16), NT, preferred_element_type=F32)
            m = jnp.maximum(jnp.max(s1, axis=-1, keepdims=True), jnp.max(s2, axis=-1, keepdims=True))
            p1, p2 = jnp.exp2(s1 - m), jnp.exp2(s2 - m)
            den = jnp.sum(p1, axis=-1, keepdims=True) + jnp.sum(p2, axis=-1, keepdims=True)
            a1 = jnp.dot(p1.astype(BF16), vb, preferred_element_type=F32)
            a2 = jnp.dot(p2.astype(BF16), vc_ref[0, 0, h].astype(BF16), preferred_element_type=F32)
            outs.append((a1[:, half] + a2) / den)
        o_ref[:, p * LANES:(p + 1) * LANES] = jnp.concatenate(outs, axis=1).astype(o_ref.dtype)


def attn_neighbourhood_latent(q, kv, cache_k, cache_v, bias):
    nq = L_LAT // TM
    t0 = T_CTX // TM
    s0 = T_CTX // L_LAT
    cache = pl.BlockSpec((1, 1, NA_H, PAST, NA_D), lambda qt, b: (b, 0, 0, 0, 0))
    return pl.pallas_call(
        _na_lat_kernel,
        grid=(nq, N_LAT),
        in_specs=[pl.BlockSpec((TM, NA_W), lambda qt, b: (t0 + b * nq + qt, 0)),
                  pl.BlockSpec((L_LAT, NA_W), lambda qt, b: (s0 + b, 0)),
                  pl.BlockSpec((L_LAT, NA_W), lambda qt, b: (s0 + b, 1)),
                  cache, cache,
                  pl.BlockSpec((NA_H, TM, L_LAT), lambda qt, b: (0, qt, 0))],
        out_specs=pl.BlockSpec((TM, NA_W), lambda qt, b: (b * nq + qt, 0)),
        out_shape=jax.ShapeDtypeStruct((T_LAT, NA_W), BF16),
        compiler_params=_cparams(("arbitrary", "arbitrary")),
        name="attn_na_lat",
    )(q, kv, kv, cache_k, cache_v, bias)


def _mla_lat_kernel(qm_ref, kvu_ref, kr_ref, ckv_ref, krc_ref, wukv_ref, o_ref):
    lo = _low_half()
    kvc = jnp.dot(ckv_ref[0, 0].astype(BF16), wukv_ref[...], preferred_element_type=F32)
    kr_lo, kr_hi = _rope_key_forms(kr_ref[...])
    krc = jnp.concatenate([krc_ref[0, 0], jnp.zeros((PAST, LANES - MLA_ROPE), F32)], axis=1)
    krc_lo, krc_hi = _rope_key_forms(krc)
    vbase = MLA_H * MLA_NOPE
    for p in range(MLA_H // 2):
        lat = (_pair(kvu_ref, p).astype(BF16), kr_lo, kr_hi, _pair(kvu_ref, p, vbase).astype(BF16))
        ctx = (_pair(kvc, p).astype(BF16), krc_lo, krc_hi, _pair(kvc, p, vbase).astype(BF16))
        o_ref[:, p * LANES:(p + 1) * LANES] = _mla_pair(qm_ref, p, [lat, ctx], lo).astype(o_ref.dtype)


def attn_mla_latent(qm, kvu, kr, cache_ckv, cache_krope, wukv_bf):
    nq = L_LAT // TM
    t0 = T_CTX // TM
    s0 = T_CTX // L_LAT
    return pl.pallas_call(
        _mla_lat_kernel,
        grid=(nq, N_LAT),
        in_specs=[pl.BlockSpec((TM, MLA_H * MLA_QK), lambda qt, b: (t0 + b * nq + qt, 0)),
                  pl.BlockSpec((L_LAT, MLA_H * (MLA_NOPE + MLA_V)), lambda qt, b: (s0 + b, 0)),
                  pl.BlockSpec((L_LAT, LANES), lambda qt, b: (s0 + b, 0)),
                  pl.BlockSpec((1, 1, PAST, MLA_KVR), lambda qt, b: (b, 0, 0, 0)),
                  pl.BlockSpec((1, 1, PAST, MLA_ROPE), lambda qt, b: (b, 0, 0, 0)),
                  pl.BlockSpec(wukv_bf.shape, lambda qt, b: (0, 0))],
        out_specs=pl.BlockSpec((TM, MLA_H * MLA_V), lambda qt, b: (b * nq + qt, 0)),
        out_shape=jax.ShapeDtypeStruct((T_LAT, MLA_H * MLA_V), BF16),
        compiler_params=_cparams(("arbitrary", "arbitrary")),
        name="attn_mla_lat",
    )(qm, kvu, kr, cache_ckv, cache_krope, wukv_bf)


def moe_block(h2, ids, wts, cnt3, x, mod, gfin, w_gate, w_up, w_down, layer, *, final):
    cnt, loc, gdst, ends, tile_expert, n_used = route_tables(cnt3)
    xs, dest = dispatch_rows(h2, ids, cnt, loc, gdst, ends)
    ys = grouped_experts(xs, w_gate, w_up, w_down, tile_expert, n_used, layer)
    return moe_combine(ys, dest, x, wts, mod, gfin, final=final)


def _pad_lanes(a):
    return jnp.pad(a, ((0, 0), (0, LANES - a.shape[1])))


def _hyena_features(L):
    t = np.linspace(0.0, 1.0, L)[:, None]
    w = 2.0 * math.pi * np.arange(L) / L
    bands = np.linspace(1e-4, HY_BANDS - 1, HY_BANDS)
    ang = w[:, None] * bands[None]
    feat = np.concatenate([t, np.cos(ang), -np.sin(ang)], axis=-1)
    return jnp.asarray(np.pad(feat, ((0, 0), (0, LANES - HY_FEAT))).astype(np.float32))


def _router_params(w_gr, b_gr, w_er, b_er):
    wr = _pad_lanes(jnp.concatenate([w_gr, w_er], axis=1))
    br = _pad_lanes(jnp.concatenate([b_gr, b_er])[None])
    wr_hi = wr.astype(BF16)
    wr_lo = (wr - wr_hi.astype(F32)).astype(BF16)
    return jnp.stack([wr_hi, wr_lo]), br


def _even_layer(x, mod, g_mix, state, w_in, conv_w, conv_b, a_log, dt_bias, d_skip, g_ssd, hy_conv_w, hy_conv_b,
                hy_w1, hy_b1, hy_w2, hy_b2, hy_w3, hy_freq, hy_bias):
    n0 = D + SSD_XBC
    w_bf = jnp.concatenate([w_in[:, :n0], w_in[:, n0 + SSD_H:], w_in[:, n0:n0 + SSD_H],
                            jnp.zeros((D, LANES - SSD_H), F32)], axis=1).astype(BF16)
    z, xbc, hy, dtr = even_in_proj(x, mod, g_mix, w_bf)
    small = (conv_w, conv_b[None], _pad_lanes(dt_bias), _pad_lanes(a_log), jnp.repeat(d_skip, SSD_P)[None], g_ssd[None])
    y_c, fin = ssd_mixer(xbc, dtr, z, None, *small, L=L_CTX, n_seq=N_CTX, row_off=0)
    (y_l,) = ssd_mixer(xbc, dtr, z, state.reshape(N_LAT, 2, SSD_H * SSD_P, SSD_N), *small,
                       L=L_LAT, n_seq=N_LAT, row_off=T_CTX)
    w1 = jnp.pad(hy_w1, ((0, LANES - HY_FEAT), (0, 0)))
    w3r = hy_w3.reshape(HY_HID, 4, D).transpose(1, 0, 2)
    deltas = jnp.asarray(np.linspace(HY_MIN_DECAY, HY_MAX_DECAY, D).astype(np.float32))[None]
    us = []
    for L, n_seq, off in ((L_CTX, N_CTX, 0), (L_LAT, N_LAT, T_CTX)):
        h4, hm = hyena_filter_spectra(_hyena_features(L), w1, hy_b1[None], hy_w2, hy_b2[None], hy_freq, w3r, deltas, L=L)
        us.append(hyena_mixer(hy, hy_conv_w, hy_conv_b[None], h4, hm, hy_bias, L=L, n_seq=n_seq, row_off=off))
    return (y_c, y_l), tuple(us), fin


def _odd_layer(x, mod, g_mix, cache_k, cache_v, cache_ckv, cache_kr, rel_bias, w_in, g_q, w_uq, g_kv, w_ukv):
    w_bf = jnp.pad(w_in, ((0, 0), (0, ODD_COLS - w_in.shape[1]))).astype(BF16)
    wuq = w_uq.reshape(MLA_QR, MLA_H, MLA_QK)
    wuq_bf = jnp.concatenate([wuq[:, :, :MLA_NOPE].reshape(MLA_QR, -1), wuq[:, :, MLA_NOPE:].reshape(MLA_QR, -1)],
                             axis=1).astype(BF16)
    wukv = w_ukv.reshape(MLA_KVR, MLA_H, MLA_NOPE + MLA_V)
    wukv_bf = jnp.concatenate([wukv[:, :, :MLA_NOPE].reshape(MLA_KVR, -1), wukv[:, :, MLA_NOPE:].reshape(MLA_KVR, -1)],
                              axis=1).astype(BF16)
    q, kv, qm, ckv, kvu, kr, k_new, v_new = odd_in_proj(x, mod, g_mix, w_bf, g_q[None], wuq_bf, g_kv[None], wukv_bf,
                                                        rope_tables())
    ona_c, omla_c = attn_context(q, kv, qm, kvu, kr)
    ona_l = attn_neighbourhood_latent(q, kv, cache_k, cache_v, neighbourhood_bias(rel_bias))
    omla_l = attn_mla_latent(qm, kvu, kr, cache_ckv, cache_kr, wukv_bf)
    return (ona_c, ona_l), (omla_c, omla_l), k_new, v_new, ckv, kr


def kernel(x_prompt, x_sample, state_ssd, cache_na_k, cache_na_v, cache_mla_ckv, cache_mla_krope, c, c_ctx, w_ada, b_ada, norm_mix, norm_ffn, norm_final, ev_w_in, ev_conv_w, ev_conv_b, ssd_A_log, ssd_dt_bias, ssd_d, ssd_norm, hy_conv_w, hy_conv_b, hy_w1, hy_b1, hy_w2, hy_b2, hy_w3, hy_freq, hy_bias, ev_w_out, od_w_in, mla_q_norm, mla_w_uq, mla_kv_norm, mla_w_ukv, na_rel_bias, od_w_out, moe_w_gr, moe_b_gr, moe_w_er, moe_b_er, moe_w_gate, moe_w_up, moe_w_down):
    x = (x_prompt.reshape(T_CTX, D), x_sample.reshape(T_LAT, D))
    cvec = jnp.zeros((MOD_ROWS, D), F32).at[0].set(c_ctx).at[1:1 + N_LAT].set(c)
    mod = ada_modulation(cvec, w_ada, b_ada)
    gfin = norm_final[None]

    y, u, fin = _even_layer(x, mod[0], norm_mix[0][None], state_ssd[:, 0], ev_w_in[0], ev_conv_w[0], ev_conv_b[0],
                            ssd_A_log[0], ssd_dt_bias[0], ssd_d[0], ssd_norm[0], hy_conv_w[0], hy_conv_b[0],
                            hy_w1[0], hy_b1[0], hy_w2[0], hy_b2[0], hy_w3[0], hy_freq[0], hy_bias[0])
    wr, br = _router_params(moe_w_gr[0], moe_b_gr[0], moe_w_er[0], moe_b_er[0])
    xn, h2, ids, wts, cnt3 = out_proj_router([y, u], ev_w_out[0].astype(BF16), x, mod[0], norm_ffn[0][None], wr, br)
    x = moe_block(h2, ids, wts, cnt3, xn, mod[0], gfin, moe_w_gate, moe_w_up, moe_w_down, 0, final=False)

    o_na, o_mla, k_new, v_new, ckv, kr = _odd_layer(x, mod[1], norm_mix[1][None], cache_na_k, cache_na_v, cache_mla_ckv,
                                           cache_mla_krope, na_rel_bias[0], od_w_in[0], mla_q_norm[0], mla_w_uq[0],
                                           mla_kv_norm[0], mla_w_ukv[0])
    wr, br = _router_params(moe_w_gr[1], moe_b_gr[1], moe_w_er[1], moe_b_er[1])
    xn, h2, ids, wts, cnt3 = out_proj_router([o_na, o_mla], od_w_out[0].astype(BF16), x, mod[1], norm_ffn[1][None], wr, br)
    y_c, y_l = moe_block(h2, ids, wts, cnt3, xn, mod[1], gfin, moe_w_gate, moe_w_up, moe_w_down, 1, final=True)

    return (y_c.reshape(N_CTX, L_CTX, D),
            y_l.reshape(N_LAT, L_LAT, D),
            fin.reshape(N_CTX, 1, 2, SSD_H, SSD_P, SSD_N),
            k_new[:, None],
            v_new[:, None],
            ckv[:T_CTX].reshape(N_CTX, 1, L_CTX, MLA_KVR),
            kr[:T_CTX, :MLA_ROPE].reshape(N_CTX, 1, L_CTX, MLA_ROPE))
```

```python
import functools
import math

import numpy as np
import jax
import jax.numpy as jnp
from jax import lax
from jax.experimental import pallas as pl
from jax.experimental.pallas import tpu as pltpu

F32 = jnp.float32
BF16 = jnp.bfloat16
HIGHEST = lax.Precision.HIGHEST

D = 1024
N_CTX, L_CTX = 16, 256
N_LAT, L_LAT = 8, 1024
T_CTX = N_CTX * L_CTX
T_LAT = N_LAT * L_LAT
T_ALL = T_CTX + T_LAT
PAST = 512
GRID_W = 64
EPS = 1e-6
NEG = -1e30

SSD_H, SSD_P, SSD_N, SSD_G = 16, 64, 128, 2
SSD_XBC = D + 2 * SSD_G * SSD_N
SSD_K = 5
CHUNK = 128

HY_K = 3
HY_BANDS = 16
HY_FEAT = 1 + 2 * HY_BANDS
HY_HID = 64
HY_MIN_DECAY = abs(math.log(1e-2) / 1.5)
HY_MAX_DECAY = abs(math.log(1e-2) / 0.3)

NA_H, NA_D = 8, 64
NA_W = NA_H * NA_D
NA_WIN_R, NA_WIN_C = 8, 16
MLA_H, MLA_QR, MLA_KVR = 8, 256, 128
MLA_NOPE, MLA_ROPE, MLA_V = 64, 32, 64
MLA_QK = MLA_NOPE + MLA_ROPE
ROPE_F = MLA_ROPE // 4

MOE_G, MOE_PG, MOE_E, MOE_F = 4, 8, 32, 256

LANES = 128
SUBLANES = 8
VMEM_LIMIT = 56 * 1024 * 1024

TM = 256
N_TILES = T_ALL // TM
CTX_TILES = T_CTX // TM
LAT_TILES_PER_SEQ = L_LAT // TM
MOD_ROWS = 16


def _cparams(sem):
    return pltpu.CompilerParams(dimension_semantics=sem, vmem_limit_bytes=VMEM_LIMIT)


def _mod_row(i):
    return jnp.where(i < CTX_TILES, 0, 1 + (i - CTX_TILES) // LAT_TILES_PER_SEQ)


def _silu(x):
    return x * jax.nn.sigmoid(x)


def _rms(x):
    return x * lax.rsqrt(jnp.mean(x * x, axis=-1, keepdims=True) + EPS)


def _ada_kernel(c_ref, w_ref, b_ref, o_ref):
    c = c_ref[...]
    o_ref[0] = jnp.dot(_silu(c), w_ref[0], precision=HIGHEST, preferred_element_type=F32) + b_ref[0]


def ada_modulation(cvec, w_ada, b_ada):
    depth = w_ada.shape[0]
    out = pl.pallas_call(
        _ada_kernel,
        grid=(depth, 6),
        in_specs=[
            pl.BlockSpec((MOD_ROWS, D), lambda l, j: (0, 0)),
            pl.BlockSpec((1, D, D), lambda l, j: (l, 0, j)),
            pl.BlockSpec((1, 1, D), lambda l, j: (l, 0, j)),
        ],
        out_specs=pl.BlockSpec((1, MOD_ROWS, D), lambda l, j: (l, 0, j)),
        out_shape=jax.ShapeDtypeStruct((depth, MOD_ROWS, 6 * D), F32),
        compiler_params=_cparams(("arbitrary", "arbitrary")),
        name="ada",
    )(cvec, w_ada, b_ada.reshape(depth, 1, 6 * D))
    return out.reshape(depth, MOD_ROWS, 6, D)


PROJ_CHUNK = 512


def _modulated(x, g_ref, mod_ref, shift_row):
    h = _rms(x) * g_ref[...]
    return h * (1.0 + mod_ref[0, shift_row + 1:shift_row + 2, :]) + mod_ref[0, shift_row:shift_row + 1, :]


IN_TM = 512
IN_CTX_TILES = T_CTX // IN_TM


def _even_in_kernel(xc_ref, xl_ref, mod_ref, g_ref, w_ref, z_ref, xbc_ref, hy_ref, dt_ref):
    x = jnp.where(pl.program_id(0) < IN_CTX_TILES, xc_ref[...], xl_ref[...])
    hb = _modulated(x, g_ref, mod_ref, 0).astype(BF16)
    col = 0
    for o_ref in (z_ref, xbc_ref, hy_ref, dt_ref):
        width = o_ref.shape[1]
        for c0 in range(0, width, PROJ_CHUNK):
            c1 = min(c0 + PROJ_CHUNK, width)
            o_ref[:, c0:c1] = jnp.dot(hb, w_ref[:, col + c0:col + c1], preferred_element_type=F32)
        col += width


def _pair_specs(width):
    return [pl.BlockSpec((TM, width), lambda i: (jnp.minimum(i, CTX_TILES - 1), 0)),
            pl.BlockSpec((TM, width), lambda i: (jnp.maximum(i - CTX_TILES, 0), 0))]


def even_in_proj(x_pair, mod, g, w_bf):
    widths = (D, SSD_XBC, 3 * D, LANES)
    mod_row = lambda i: jnp.where(i < IN_CTX_TILES, 0, 1 + (i - IN_CTX_TILES) // (L_LAT // IN_TM))
    return pl.pallas_call(
        _even_in_kernel,
        grid=(T_ALL // IN_TM,),
        in_specs=[
            pl.BlockSpec((IN_TM, D), lambda i: (jnp.minimum(i, IN_CTX_TILES - 1), 0)),
            pl.BlockSpec((IN_TM, D), lambda i: (jnp.maximum(i - IN_CTX_TILES, 0), 0)),
            pl.BlockSpec((1, 6, D), lambda i: (mod_row(i), 0, 0)),
            pl.BlockSpec((1, D), lambda i: (0, 0)),
            _const_spec(w_bf),
        ],
        out_specs=[pl.BlockSpec((IN_TM, w), lambda i: (i, 0)) for w in widths],
        out_shape=[jax.ShapeDtypeStruct((T_ALL, w), F32) for w in widths],
        compiler_params=_cparams(("arbitrary",)),
        name="even_in",
    )(*x_pair, mod, g, w_bf)


PAD = SUBLANES


def _ssd_kernel(*refs, L, has_init):
    if has_init:
        (xbc_ref, dt_ref, z_ref, init_ref, cw_ref, cb_ref, dtb_ref, alog_ref, dsk_ref, gs_ref,
         y_ref, xp_s, xc_s, ya_s, st_s) = refs
        fin_ref = None
    else:
        (xbc_ref, dt_ref, z_ref, cw_ref, cb_ref, dtb_ref, alog_ref, dsk_ref, gs_ref,
         y_ref, fin_ref, xp_s, xc_s, ya_s, st_s) = refs
        init_ref = None
    nc = L // CHUNK
    half = SSD_K // 2

    xp_s[0:PAD, :] = jnp.zeros((PAD, SSD_XBC), F32)
    xp_s[PAD + L:2 * PAD + L, :] = jnp.zeros((PAD, SSD_XBC), F32)
    xp_s[PAD:PAD + L, :] = xbc_ref[...]
    for c in range(nc):
        base = PAD + c * CHUNK - half
        for j in range(SSD_XBC // LANES):
            cols = slice(j * LANES, (j + 1) * LANES)
            acc = cb_ref[:, cols] + xp_s[base:base + CHUNK, cols] * cw_ref[0:1, cols]
            for k in range(1, SSD_K):
                acc = acc + xp_s[base + k:base + k + CHUNK, cols] * cw_ref[k:k + 1, cols]
            xc_s[c * CHUNK:(c + 1) * CHUNK, cols] = _silu(acc)

    row = lax.broadcasted_iota(jnp.int32, (CHUNK, CHUNK), 0)
    colm = lax.broadcasted_iota(jnp.int32, (CHUNK, CHUNK), 1)
    lane_lo = colm < SSD_P
    tri_lo = (colm <= row).astype(F32)
    tri_up = (colm >= row).astype(F32)

    for d in range(2):
        causal = (colm <= row) if d == 0 else (colm >= row)
        for j in range(SSD_H * SSD_P // CHUNK):
            if has_init:
                st_s[:, j * CHUNK:(j + 1) * CHUNK] = init_ref[0, d, j * CHUNK:(j + 1) * CHUNK, :].T
            else:
                st_s[:, j * CHUNK:(j + 1) * CHUNK] = jnp.zeros((CHUNK, CHUNK), F32)

        def chunk_body(ci, carry, d=d, causal=causal):
            c = ci if d == 0 else nc - 1 - ci
            r0 = pl.multiple_of(c * CHUNK, CHUNK)
            dt = jax.nn.softplus(dt_ref[pl.ds(r0, CHUNK), :] + dtb_ref[d:d + 1, :])
            a = dt * (-jnp.exp(alog_ref[d:d + 1, :]))
            tri = tri_lo if d == 0 else tri_up
            cs = jnp.dot(tri, a, precision=HIGHEST, preferred_element_type=F32)
            cs_t = jnp.dot(a.T, tri.T, precision=HIGHEST, preferred_element_type=F32)
            edge = cs[CHUNK - 1:CHUNK, :] if d == 0 else cs[0:1, :]
            for g in range(SSD_G):
                bm = xc_s[pl.ds(r0, CHUNK), D + g * SSD_N:D + (g + 1) * SSD_N]
                cm = xc_s[pl.ds(r0, CHUNK), D + (SSD_G + g) * SSD_N:D + (SSD_G + g + 1) * SSD_N]
                bm_b, cm_b = bm.astype(BF16), cm.astype(BF16)
                cb = lax.dot_general(cm_b, bm_b, (((1,), (1,)), ((), ())), preferred_element_type=F32)
                bm_t = bm.T.astype(BF16)
                pairs = SSD_H // SSD_G // 2
                for pp in range(pairs):
                    p = g * pairs + pp
                    h0, h1 = 2 * p, 2 * p + 1
                    cols = slice(p * CHUNK, (p + 1) * CHUNK)
                    xs = xc_s[pl.ds(r0, CHUNK), cols]
                    xdt = xs * jnp.where(lane_lo, dt[:, h0:h0 + 1], dt[:, h1:h1 + 1])
                    cs_b = [jnp.broadcast_to(cs[:, h:h + 1], (CHUNK, CHUNK)) for h in (h0, h1)]
                    ms = [cb * jnp.exp(jnp.where(causal, cs_b[e] - cs_t[h:h + 1, :], NEG)) for e, h in enumerate((h0, h1))]
                    cs_p = jnp.where(lane_lo, cs_b[0], cs_b[1])
                    edge_p = jnp.where(lane_lo[0:1, :], edge[:, h0:h0 + 1], edge[:, h1:h1 + 1])
                    mcat = jnp.concatenate(ms, axis=1).astype(BF16)
                    xbd = jnp.concatenate([jnp.where(lane_lo, xdt, 0.0), jnp.where(lane_lo, 0.0, xdt)],
                                          axis=0).astype(BF16)
                    y_diag = jnp.dot(mcat, xbd, preferred_element_type=F32)
                    st = st_s[:, cols]
                    y_off = jnp.dot(cm_b, st.astype(BF16), preferred_element_type=F32)
                    y_off = y_off * jnp.exp(cs_p)
                    y = y_diag + y_off
                    if d == 0:
                        ya_s[pl.ds(r0, CHUNK), cols] = y
                    else:
                        ya_s[pl.ds(r0, CHUNK), cols] = ya_s[pl.ds(r0, CHUNK), cols] + y
                    xdd = (xdt * jnp.exp(edge_p - cs_p)).astype(BF16)
                    snew = jnp.dot(bm_t, xdd, preferred_element_type=F32)
                    st_s[:, cols] = st * jnp.exp(edge_p) + snew
            return carry

        lax.fori_loop(0, nc, chunk_body, 0)
        if fin_ref is not None:
            for j in range(SSD_H * SSD_P // CHUNK):
                fin_ref[0, d, j * CHUNK:(j + 1) * CHUNK, :] = st_s[:, j * CHUNK:(j + 1) * CHUNK].T

    def out_body(c, carry):
        r0 = pl.multiple_of(c * CHUNK, CHUNK)
        y = ya_s[pl.ds(r0, CHUNK), :] + xc_s[pl.ds(r0, CHUNK), 0:D] * dsk_ref[...]
        y = y * _silu(z_ref[pl.ds(r0, CHUNK), :])
        y_ref[pl.ds(r0, CHUNK), :] = (_rms(y) * gs_ref[...]).astype(y_ref.dtype)
        return carry

    lax.fori_loop(0, nc, out_body, 0)


def ssd_mixer(xbc, dtr, z, init, cw, cb, dtb, alog, dsk, gs, *, L, n_seq, row_off):
    blk0 = row_off // L
    has_init = init is not None
    seq = lambda w: pl.BlockSpec((L, w), lambda b: (blk0 + b, 0))
    full = lambda arr: pl.BlockSpec(arr.shape, lambda b: (0,) * arr.ndim)
    in_specs = [seq(SSD_XBC), seq(LANES), seq(D)]
    args = [xbc, dtr, z]
    if has_init:
        in_specs.append(pl.BlockSpec((1, 2, SSD_H * SSD_P, SSD_N), lambda b: (b, 0, 0, 0)))
        args.append(init)
    small = [cw, cb, dtb, alog, dsk, gs]
    in_specs += [full(a) for a in small]
    args += small
    out_specs = [pl.BlockSpec((L, D), lambda b: (b, 0))]
    out_shape = [jax.ShapeDtypeStruct((n_seq * L, D), BF16)]
    if not has_init:
        out_specs.append(pl.BlockSpec((1, 2, SSD_H * SSD_P, SSD_N), lambda b: (b, 0, 0, 0)))
        out_shape.append(jax.ShapeDtypeStruct((n_seq, 2, SSD_H * SSD_P, SSD_N), F32))
    return pl.pallas_call(
        functools.partial(_ssd_kernel, L=L, has_init=has_init),
        grid=(n_seq,),
        in_specs=in_specs,
        out_specs=out_specs,
        out_shape=out_shape,
        scratch_shapes=[
            pltpu.VMEM((L + 2 * PAD, SSD_XBC), F32),
            pltpu.VMEM((L, SSD_XBC), F32),
            pltpu.VMEM((L, D), F32),
            pltpu.VMEM((SSD_N, SSD_H * SSD_P), F32),
        ],
        compiler_params=_cparams(("arbitrary",)),
        name=f"ssd_{L}",
    )(*args)


HY_CB = 256


def filter_dft_matrices(L):
    H = L // 2
    s = np.arange(L, dtype=np.int64)[None, :]
    k = np.arange(H, dtype=np.int64)[:, None]
    ang = lambda kk: ((kk * s) % (2 * L)).astype(np.float64) * (math.pi / L)
    ca, cb = np.cos(ang(k)), np.cos(ang(L - k))
    sa, sb = np.sin(ang(k)), np.sin(ang(L - k))
    cb[0] = np.where(s[0] % 2 == 0, 1.0, -1.0)
    sa[0], sb[0] = 0.0, 0.0
    fm = np.zeros((2 * SUBLANES, L))
    fm[0], fm[1] = np.cos(ang(H))[0], np.sin(ang(H))[0]
    mats = (np.concatenate([ca, cb], axis=0), np.concatenate([sa, sb], axis=0), fm)
    return tuple(jnp.asarray(m.astype(np.float32)).astype(BF16) for m in mats)


def _const_spec(arr):
    return pl.BlockSpec(arr.shape, lambda *_: (0,) * arr.ndim, pipeline_mode=pl.Buffered(1))


def _hy_filter_kernel(feat_ref, w1_ref, b1_ref, w2_ref, b2_ref, fr_ref, w3_ref, dl_ref, fs_ref, fd_ref, fm_ref,
                      h_ref, hm_ref, *, L):
    H = L // 2
    hp = functools.partial(jnp.dot, precision=HIGHEST, preferred_element_type=F32)
    hdn = jnp.sin(fr_ref[0:1, :] * (hp(feat_ref[...], w1_ref[...]) + b1_ref[...]))
    hdn = jnp.sin(fr_ref[1:2, :] * (hp(hdn, w2_ref[...]) + b2_ref[...]))
    rowi = lax.broadcasted_iota(jnp.int32, (L, 1), 0)
    t = rowi.astype(F32) * (1.0 / (L - 1))
    dec = jnp.exp(-t * dl_ref[...])
    first = rowi == 0
    for o in range(2):
        fwd = hp(hdn, w3_ref[2 * o]) * dec
        bwd = jnp.where(first, 0.0, hp(hdn, w3_ref[2 * o + 1]) * dec)
        hs, hd = (fwd + bwd).astype(BF16), (fwd - bwd).astype(BF16)
        ss = jnp.dot(fs_ref[...], hs, preferred_element_type=F32)
        sd = jnp.dot(fd_ref[...], hd, preferred_element_type=F32)
        h_ref[o, 0] = ss[0:H]
        h_ref[o, 1] = sd[0:H]
        h_ref[o, 2] = ss[H:L]
        h_ref[o, 3] = sd[H:L]
        mid_r = jnp.dot(fm_ref[...], hs, preferred_element_type=F32)
        mid_n = jnp.dot(fm_ref[...], hd, preferred_element_type=F32)
        hm_ref[o] = jnp.concatenate([mid_r[0:1], mid_n[1:2], jnp.zeros((SUBLANES - 2, mid_r.shape[1]), F32)], axis=0)


def hyena_filter_spectra(feat, w1, b1, w2, b2, freq, w3r, deltas, *, L):
    full = lambda arr: pl.BlockSpec(arr.shape, lambda j: (0,) * arr.ndim)
    mats = filter_dft_matrices(L)
    return pl.pallas_call(
        functools.partial(_hy_filter_kernel, L=L),
        grid=(D // HY_CB,),
        in_specs=[full(feat), full(w1), full(b1), full(w2), full(b2), full(freq),
                  pl.BlockSpec((4, HY_HID, HY_CB), lambda j: (0, 0, j)),
                  pl.BlockSpec((1, HY_CB), lambda j: (0, j))] + [_const_spec(m) for m in mats],
        out_specs=[pl.BlockSpec((2, 4, L // 2, HY_CB), lambda j: (0, 0, 0, j)),
                   pl.BlockSpec((2, SUBLANES, HY_CB), lambda j: (0, 0, j))],
        out_shape=[jax.ShapeDtypeStruct((2, 4, L // 2, D), F32), jax.ShapeDtypeStruct((2, SUBLANES, D), F32)],
        compiler_params=_cparams(("arbitrary",)),
        name=f"hy_filter_{L}",
    )(feat, w1, b1, w2, b2, freq, w3r, deltas, *mats)


def split_dft_matrices(L):
    H = L // 2
    k = np.arange(H, dtype=np.int64)[:, None]
    m = np.arange(H, dtype=np.int64)[None, :]
    alt = np.where(m % 2 == 0, 1.0, -1.0)
    ang_e = ((k * m) % L).astype(np.float64) * (2 * math.pi / L)
    ang_o = ((k * (2 * m + 1)) % (2 * L)).astype(np.float64) * (math.pi / L)
    ce, se, co, so = np.cos(ang_e), np.sin(ang_e), np.cos(ang_o), np.sin(ang_o)
    se[0], so[0] = alt[0], alt[0]
    w = np.where(k == 0, 1.0, 2.0) / (2 * L)
    fe = np.concatenate([ce, se], axis=0)
    fo = np.concatenate([co, so], axis=0)
    ge = np.concatenate([(ce * w).T, se.T / L], axis=1)
    go = np.concatenate([(co * w).T, so.T / L], axis=1)
    return tuple(jnp.asarray(a.astype(np.float32)).astype(BF16) for a in (fe, fo, ge, go))


def _store_lane_blocks(ref, val):
    for c in range(ref.shape[0]):
        ref[c] = val[:, c * LANES:(c + 1) * LANES]


def _load_parity(ref, parity, n):
    return jnp.concatenate([ref[c, pl.ds(parity, n, stride=2), :] for c in range(ref.shape[0])], axis=1)


def _hyena_kernel(p0_ref, p1_ref, p2_ref, w0_ref, w1_ref, w2_ref, b0_ref, b1_ref, b2_ref, h_ref, hm_ref, hb_ref,
                  fe_ref, fo_ref, ge_ref, go_ref, o_ref, xp_s, u_s, y_s, *, L, cb):
    H = L // 2
    xp_s[0:PAD, :] = jnp.zeros((PAD, cb), F32)
    xp_s[PAD + L:2 * PAD + L, :] = jnp.zeros((PAD, cb), F32)
    first = lax.broadcasted_iota(jnp.int32, (H, 1), 0) == 0

    def conv(p_ref, w_ref, b_ref):
        xp_s[PAD:PAD + L, :] = p_ref[...]
        acc = b_ref[...] + xp_s[PAD - 1:PAD - 1 + L, :] * w_ref[0:1, :]
        for k in range(1, HY_K):
            acc = acc + xp_s[PAD - 1 + k:PAD - 1 + k + L, :] * w_ref[k:k + 1, :]
        return acc

    u = conv(p0_ref, w0_ref, b0_ref)
    for o, (p_ref, w_ref, b_ref) in enumerate(((p1_ref, w1_ref, b1_ref), (p2_ref, w2_ref, b2_ref))):
        _store_lane_blocks(u_s, u)
        se = jnp.dot(fe_ref[...], _load_parity(u_s, 0, H).astype(BF16), preferred_element_type=F32)
        so = jnp.dot(fo_ref[...], _load_parity(u_s, 1, H).astype(BF16), preferred_element_type=F32)
        e, es, od, os_ = se[0:H], se[H:L], so[0:H], so[H:L]
        b0, b1 = e + od, e - od
        b2 = jnp.where(first, es, es + os_)
        b3 = jnp.where(first, os_, os_ - es)
        har, han, hbr, hbn = h_ref[o, 0], h_ref[o, 1], h_ref[o, 2], h_ref[o, 3]
        hmr, hmn = hm_ref[o, 0:1, :], hm_ref[o, 1:2, :]
        y0 = b0 * har - b2 * han
        y1 = b1 * hbr - b3 * hbn
        y2 = b0 * han + b2 * har
        y3 = b1 * hbn + b3 * hbr
        mid_r = b2[0:1] * hmr - b3[0:1] * hmn
        mid_n = b2[0:1] * hmn + b3[0:1] * hmr
        de = jnp.where(first, mid_r, y2 - y3)
        do = jnp.where(first, mid_n, y2 + y3)
        ye = jnp.dot(ge_ref[...], jnp.concatenate([y0 + y1, de], axis=0).astype(BF16), preferred_element_type=F32)
        yo = jnp.dot(go_ref[...], jnp.concatenate([y0 - y1, do], axis=0).astype(BF16), preferred_element_type=F32)
        for c in range(cb // LANES):
            y_s[c, pl.ds(0, H, stride=2), :] = ye[:, c * LANES:(c + 1) * LANES]
            y_s[c, pl.ds(1, H, stride=2), :] = yo[:, c * LANES:(c + 1) * LANES]
        y = jnp.concatenate([y_s[c] for c in range(cb // LANES)], axis=1)
        u = conv(p_ref, w_ref, b_ref) * (y + u * hb_ref[o:o + 1, :])
    o_ref[...] = u.astype(o_ref.dtype)


def hyena_mixer(hy, conv_w, conv_b, h4, hm, hy_bias, *, L, n_seq, row_off):
    blk0 = row_off // L
    cb = min(D, HY_CB * (L_LAT // L))
    nj = D // cb
    H = L // 2
    part = lambda q: pl.BlockSpec((L, cb), lambda j, b: (blk0 + b, q * nj + j))
    wpart = lambda q: pl.BlockSpec((HY_K, cb), lambda j, b: (0, q * nj + j))
    bpart = lambda q: pl.BlockSpec((1, cb), lambda j, b: (0, q * nj + j))
    mats = split_dft_matrices(L)
    return pl.pallas_call(
        functools.partial(_hyena_kernel, L=L, cb=cb),
        grid=(nj, n_seq),
        in_specs=[part(0), part(1), part(2), wpart(0), wpart(1), wpart(2), bpart(0), bpart(1), bpart(2),
                  pl.BlockSpec((2, 4, H, cb), lambda j, b: (0, 0, 0, j)),
                  pl.BlockSpec((2, SUBLANES, cb), lambda j, b: (0, 0, j)),
                  pl.BlockSpec((2, cb), lambda j, b: (0, j))]
                 + [_const_spec(m) for m in mats],
        out_specs=pl.BlockSpec((L, cb), lambda j, b: (b, j)),
        out_shape=jax.ShapeDtypeStruct((n_seq * L, D), BF16),
        scratch_shapes=[pltpu.VMEM((L + 2 * PAD, cb), F32), pltpu.VMEM((cb // LANES, L, LANES), F32),
                        pltpu.VMEM((cb // LANES, L, LANES), F32)],
        compiler_params=_cparams(("arbitrary", "arbitrary")),
        name=f"hyena_{L}",
    )(hy, hy, hy, conv_w, conv_w, conv_w, conv_b, conv_b, conv_b, h4, hm, hy_bias, *mats)


ROUTER_LANES = LANES
BIG_LANE = 1e9


ROW_GROUP = D // LANES


def _store_row_groups(ref, val):
    n = val.shape[0]
    for s in range(ROW_GROUP):
        ref[pl.ds(s, n, stride=ROW_GROUP), :] = val[:, s * LANES:(s + 1) * LANES]


def _load_row_groups(ref, n, s):
    return ref[pl.ds(s, n, stride=ROW_GROUP), :]


def _first_max_lane(v, lanef):
    m = jnp.max(v, axis=-1, keepdims=True)
    return m, jnp.min(jnp.where(v == m, lanef, BIG_LANE), axis=-1, keepdims=True)


def _out_router_kernel(*refs, n_in, x_is_pair):
    a_refs = refs[:2 * n_in]
    refs = refs[2 * n_in:]
    is_ctx = pl.program_id(0) < CTX_TILES
    if x_is_pair:
        x = jnp.where(is_ctx, refs[0][...], refs[1][...])
        refs = refs[2:]
    else:
        x = refs[0][...]
        refs = refs[1:]
    w_ref, mod_ref, gf_ref, wr_ref, br_ref, xo_ref, h2_ref, ids_ref, wts_ref, cnt_ref = refs
    acc, k0 = None, 0
    for ac_ref, al_ref in zip(a_refs[0::2], a_refs[1::2]):
        kk = ac_ref.shape[1]
        a = jnp.where(is_ctx, ac_ref[...], al_ref[...])
        part = jnp.dot(a, w_ref[k0:k0 + kk, :], preferred_element_type=F32)
        acc = part if acc is None else acc + part
        k0 += kk
    xn = x + mod_ref[0, 2:3, :] * acc
    xo_ref[...] = xn
    h2 = _modulated(xn, gf_ref, mod_ref, 3)
    h2_ref[...] = h2

    h_hi = h2.astype(BF16)
    h_lo = (h2 - h_hi.astype(F32)).astype(BF16)
    logits = (jnp.dot(h_hi, wr_ref[0], preferred_element_type=F32) + jnp.dot(h_lo, wr_ref[0], preferred_element_type=F32)
              + jnp.dot(h_hi, wr_ref[1], preferred_element_type=F32) + br_ref[...])
    lanef = lax.broadcasted_iota(jnp.int32, logits.shape, 1).astype(F32)
    gl = jnp.where(lanef < MOE_G, logits, NEG)
    gm, gi = _first_max_lane(gl, lanef)
    g_w = 1.0 / jnp.sum(jnp.exp(gl - gm), axis=-1, keepdims=True)
    lo = MOE_G + MOE_PG * gi
    el = jnp.where((lanef >= lo) & (lanef < lo + MOE_PG), logits, NEG)
    m1, e1 = _first_max_lane(el, lanef)
    m2, e2 = _first_max_lane(jnp.where(lanef == e1, NEG, el), lanef)
    p2 = jnp.exp(m2 - m1)
    w1 = g_w / (1.0 + p2)
    ids_ref[...] = jnp.where(lanef == 0, e1 - MOE_G, jnp.where(lanef == 1, e2 - MOE_G, 0.0)).astype(jnp.int32)
    wts_ref[...] = jnp.where(lanef == 0, w1, jnp.where(lanef == 1, w1 * p2, 0.0))
    chosen = ((lanef == e1 - MOE_G) | (lanef == e2 - MOE_G)).astype(F32)
    cnt_ref[0] = jnp.sum(chosen, axis=0, keepdims=True).astype(jnp.int32)


def out_proj_router(acts, w_bf, x, mod, gf, wr, br):
    tile = lambda w: pl.BlockSpec((TM, w), lambda i: (i, 0))
    full = lambda arr: pl.BlockSpec(arr.shape, lambda i: (0,) * arr.ndim)
    x_is_pair = isinstance(x, tuple)
    xs = x if x_is_pair else (x,)
    return pl.pallas_call(
        functools.partial(_out_router_kernel, n_in=len(acts), x_is_pair=x_is_pair),
        grid=(N_TILES,),
        in_specs=[s for a in acts for s in _pair_specs(a[0].shape[1])]
                 + (_pair_specs(D) if x_is_pair else [tile(D)])
                 + [full(w_bf), pl.BlockSpec((1, 6, D), lambda i: (_mod_row(i), 0, 0)), full(gf), full(wr), full(br)],
        out_specs=[tile(D), tile(D), tile(ROUTER_LANES), tile(ROUTER_LANES),
                   pl.BlockSpec((1, 1, ROUTER_LANES), lambda i: (i, 0, 0))],
        out_shape=[jax.ShapeDtypeStruct((T_ALL, D), F32), jax.ShapeDtypeStruct((T_ALL, D), F32),
                   jax.ShapeDtypeStruct((T_ALL, ROUTER_LANES), jnp.int32),
                   jax.ShapeDtypeStruct((T_ALL, ROUTER_LANES), F32),
                   jax.ShapeDtypeStruct((N_TILES, 1, ROUTER_LANES), jnp.int32)],
        compiler_params=_cparams(("arbitrary",)),
        name="out_router",
    )(*[part for a in acts for part in a], *xs, w_bf, mod, gf, wr, br)


N_ASSIGN = 2 * T_ALL
MOE_TILES = N_ASSIGN // TM + MOE_E
N_SLOTS = MOE_TILES * TM


def route_tables(cnt3):
    cnt = cnt3[:, 0, :MOE_E]
    total = jnp.sum(cnt, axis=0)
    padded = (total + TM - 1) // TM * TM
    ends = jnp.cumsum(padded)
    gdst = (ends - padded)[None, :] + jnp.cumsum(cnt, axis=0) - cnt
    loc = jnp.cumsum(cnt, axis=1) - cnt
    starts = jnp.arange(MOE_TILES, dtype=jnp.int32) * TM
    tile_expert = jnp.minimum(jnp.sum((ends[None, :] <= starts[:, None]).astype(jnp.int32), axis=1), MOE_E - 1)
    n_used = (ends[-1] // TM).astype(jnp.int32).reshape(1)
    return cnt, loc, gdst, ends, tile_expert, n_used


RUN_BITS = (2 * TM).bit_length()
RUN_SMALL_BITS = 6


def _dispatch_kernel(cnt_s, loc_s, gdst_s, ends_s, h_ref, ids_ref, gcol_ref, xs_ref, dest_ref, srt, zbuf, sem, zsem):
    i = pl.program_id(0)
    slot = i % 2
    n_rows = 2 * TM

    @pl.when(i == 0)
    def _():
        zbuf[...] = jnp.zeros(zbuf.shape, zbuf.dtype)
        n_used = ends_s[MOE_E - 1] // TM
        for phase in ("start", "wait"):
            def tail(t, c, phase=phase):
                dst = pl.multiple_of(t * (TM * ROW_GROUP), TM * ROW_GROUP)
                cp = pltpu.make_async_copy(zbuf, xs_ref.at[pl.ds(dst, TM * ROW_GROUP), :], zsem)
                cp.start() if phase == "start" else cp.wait()
                return c

            lax.fori_loop(n_used, MOE_TILES, tail, 0)
            for e in range(MOE_E):
                end = ends_s[e]
                prev = ends_s[e - 1] if e > 0 else 0

                @pl.when(end > prev)
                def _(end=end, phase=phase):
                    dst = pl.multiple_of((end - TM) * ROW_GROUP, TM * ROW_GROUP)
                    cp = pltpu.make_async_copy(zbuf, xs_ref.at[pl.ds(dst, TM * ROW_GROUP), :], zsem)
                    cp.start() if phase == "start" else cp.wait()

    idt = ids_ref[...].astype(F32).T
    sub = lax.broadcasted_iota(jnp.int32, (LANES, TM), 0).astype(F32)
    m0 = (sub == idt[0:1, :]).astype(F32)
    m1 = (sub == idt[1:2, :]).astype(F32)
    mt = (m0 + m1).astype(BF16)
    tr = lax.broadcasted_iota(jnp.int32, (TM, TM), 0)
    tc = lax.broadcasted_iota(jnp.int32, (TM, TM), 1)
    earlier = jnp.dot(mt, (tr < tc).astype(BF16), preferred_element_type=F32)
    er = lax.broadcasted_iota(jnp.int32, (LANES, LANES), 0)
    ec = lax.broadcasted_iota(jnp.int32, (LANES, LANES), 1)
    below = jnp.dot((ec < er).astype(BF16), mt, preferred_element_type=F32)
    local = jnp.sum(below, axis=1, keepdims=True) + earlier
    glob = gcol_ref[0] + earlier
    pos0 = jnp.sum(m0 * local, axis=0, keepdims=True)
    pos1 = jnp.sum(m1 * local, axis=0, keepdims=True)
    dest_ref[0] = jnp.concatenate([jnp.sum(m0 * glob, axis=0, keepdims=True),
                                   jnp.sum(m1 * glob, axis=0, keepdims=True)], axis=0).astype(jnp.int32)

    srow = lax.broadcasted_iota(jnp.int32, (n_rows, TM), 0).astype(F32)
    perm = jnp.where((srow == pos0) | (srow == pos1), 1.0, 0.0).astype(BF16)
    _store_row_groups(srt.at[slot], jnp.dot(perm, h_ref[...].astype(BF16), preferred_element_type=F32))

    def run_pieces(n, s0, d0, bits):
        for b in bits:
            size = 1 << b
            off = (n >> (b + 1)) << (b + 1)

            @pl.when(((n >> b) & 1) == 1)
            def _(size=size, off=off):
                src = pl.multiple_of((s0 + off) * ROW_GROUP, ROW_GROUP)
                dst = pl.multiple_of((d0 + off) * ROW_GROUP, ROW_GROUP)
                pltpu.make_async_copy(srt.at[slot, pl.ds(src, size * ROW_GROUP), :],
                                      xs_ref.at[pl.ds(dst, size * ROW_GROUP), :], sem.at[slot]).start()

    for e in range(MOE_E):
        n, s0, d0 = cnt_s[0, 0, e], loc_s[0, 0, e], gdst_s[0, 0, e]

        @pl.when(n >= (1 << RUN_SMALL_BITS))
        def _(n=n, s0=s0, d0=d0):
            run_pieces(n, s0, d0, reversed(range(RUN_SMALL_BITS, RUN_BITS)))

        run_pieces(n, s0, d0, reversed(range(RUN_SMALL_BITS)))

    def wait(s):
        pltpu.make_async_copy(srt.at[s], xs_ref.at[pl.ds(0, n_rows * ROW_GROUP), :], sem.at[s]).wait()

    @pl.when(i > 0)
    def _():
        wait(1 - slot)

    @pl.when(i == N_TILES - 1)
    def _():
        wait(slot)


def dispatch_rows(h2, ids, cnt, loc, gdst, ends):
    tab = lambda: pl.BlockSpec((1, 1, MOE_E), lambda i: (i, 0, 0), memory_space=pltpu.SMEM)
    gcol = jnp.pad(gdst.astype(F32), ((0, 0), (0, LANES - MOE_E)))[:, :, None]
    return pl.pallas_call(
        _dispatch_kernel,
        grid=(N_TILES,),
        in_specs=[tab(), tab(), tab(), pl.BlockSpec(memory_space=pltpu.SMEM),
                  pl.BlockSpec((TM, D), lambda i: (i, 0)), pl.BlockSpec((TM, ROUTER_LANES), lambda i: (i, 0)),
                  pl.BlockSpec((1, LANES, 1), lambda i: (i, 0, 0))],
        out_specs=[pl.BlockSpec(memory_space=pl.ANY), pl.BlockSpec((1, 2, TM), lambda i: (i, 0, 0))],
        out_shape=[jax.ShapeDtypeStruct((N_SLOTS * ROW_GROUP, LANES), F32),
                   jax.ShapeDtypeStruct((N_TILES, 2, TM), jnp.int32)],
        scratch_shapes=[pltpu.VMEM((2, 2 * TM * ROW_GROUP, LANES), F32), pltpu.VMEM((TM * ROW_GROUP, LANES), F32),
                        pltpu.SemaphoreType.DMA((2,)), pltpu.SemaphoreType.DMA(())],
        compiler_params=_cparams(("arbitrary",)),
        name="moe_dispatch",
    )(cnt.reshape(N_TILES, 1, MOE_E), loc.reshape(N_TILES, 1, MOE_E), gdst.reshape(N_TILES, 1, MOE_E),
      ends, h2, ids, gcol)


DMA_UNROLL = 8


def _start_group_gather(src_hbm, idx_ref, n, dst_ref, sem):
    def body(j, c):
        for u in range(DMA_UNROLL):
            r = j * DMA_UNROLL + u
            src = pl.multiple_of(idx_ref[0, 0, r] * ROW_GROUP, ROW_GROUP)
            dst = pl.multiple_of(r * ROW_GROUP, ROW_GROUP)
            pltpu.make_async_copy(src_hbm.at[pl.ds(src, ROW_GROUP), :], dst_ref.at[pl.ds(dst, ROW_GROUP), :],
                                  sem).start(priority=u % 2)
        return c

    lax.fori_loop(0, n // DMA_UNROLL, body, 0)


def _wait_group_gather(src_hbm, dst_ref, sem):
    pltpu.make_async_copy(src_hbm.at[pl.ds(0, dst_ref.shape[0]), :], dst_ref, sem).wait()


def _experts_kernel(te_ref, nu_ref, x_ref, wg_ref, wu_ref, wd_ref, o_ref, xcat):
    i = pl.program_id(0)

    @pl.when(i < nu_ref[0])
    def _():
        for s in range(ROW_GROUP):
            xcat[:, s * LANES:(s + 1) * LANES] = _load_row_groups(x_ref, TM, s).astype(BF16)
        x = xcat[...]
        g = jnp.dot(x, wg_ref[0, 0].astype(BF16), preferred_element_type=F32)
        u = jnp.dot(x, wu_ref[0, 0].astype(BF16), preferred_element_type=F32)
        hid = (_silu(g) * u).astype(BF16)
        _store_row_groups(o_ref, jnp.dot(hid, wd_ref[0, 0].astype(BF16), preferred_element_type=F32))

    @pl.when(i >= nu_ref[0])
    def _():
        o_ref[...] = jnp.zeros(o_ref.shape, o_ref.dtype)


def grouped_experts(xs, w_gate, w_up, w_down, tile_expert, n_used, layer):
    wspec = lambda a, b: pl.BlockSpec((1, 1, a, b), lambda i, te, nu: (layer, te[i], 0, 0))
    return pl.pallas_call(
        _experts_kernel,
        grid_spec=pltpu.PrefetchScalarGridSpec(
            num_scalar_prefetch=2,
            grid=(MOE_TILES,),
            in_specs=[pl.BlockSpec((TM * ROW_GROUP, LANES), lambda i, te, nu: (jnp.minimum(i, nu[0] - 1), 0)),
                      wspec(D, MOE_F), wspec(D, MOE_F), wspec(MOE_F, D)],
            out_specs=pl.BlockSpec((TM * ROW_GROUP, LANES), lambda i, te, nu: (i, 0)),
            scratch_shapes=[pltpu.VMEM((TM, D), BF16)],
        ),
        out_shape=jax.ShapeDtypeStruct((N_SLOTS * ROW_GROUP, LANES), F32),
        compiler_params=_cparams(("arbitrary",)),
        name="moe_experts",
    )(tile_expert, n_used, xs, w_gate, w_up, w_down)


def _combine_kernel(cur_ref, nxt_ref, ys_hbm, x_ref, wts_ref, mod_ref, gfin_ref, *rest, final):
    *o_refs, buf, sem = rest
    i = pl.program_id(0)
    slot = i % 2

    @pl.when(i == 0)
    def _():
        _start_group_gather(ys_hbm, cur_ref, 2 * TM, buf.at[0], sem.at[0])

    @pl.when(i + 1 < N_TILES)
    def _():
        _start_group_gather(ys_hbm, nxt_ref, 2 * TM, buf.at[1 - slot], sem.at[1 - slot])

    _wait_group_gather(ys_hbm, buf.at[slot], sem.at[slot])
    w0, w1 = wts_ref[:, 0:1], wts_ref[:, 1:2]

    def finish(o_ref):
        for s in range(ROW_GROUP):
            cols = slice(s * LANES, (s + 1) * LANES)
            y0 = buf[slot, pl.ds(s, TM, stride=ROW_GROUP), :]
            y1 = buf[slot, pl.ds(TM * ROW_GROUP + s, TM, stride=ROW_GROUP), :]
            o_ref[:, cols] = x_ref[:, cols] + mod_ref[0, 5:6, cols] * (w0 * y0 + w1 * y1)
        if final:
            o_ref[...] = _rms(o_ref[...]) * gfin_ref[...]

    if final:
        pl.when(i < CTX_TILES)(lambda: finish(o_refs[0]))
        pl.when(i >= CTX_TILES)(lambda: finish(o_refs[1]))
    else:
        finish(o_refs[0])


def moe_combine(ys, dest, x, wts, mod, gfin, *, final):
    tile = lambda w: pl.BlockSpec((TM, w), lambda i: (i, 0))
    if final:
        out_specs = _pair_specs(D)
        out_shape = [jax.ShapeDtypeStruct((T_CTX, D), F32), jax.ShapeDtypeStruct((T_LAT, D), F32)]
    else:
        out_specs = [tile(D)]
        out_shape = [jax.ShapeDtypeStruct((T_ALL, D), F32)]
    idx = lambda f: pl.BlockSpec((1, 1, 2 * TM), lambda i: (f(i), 0, 0), memory_space=pltpu.SMEM)
    dest3 = dest.reshape(N_TILES, 1, 2 * TM)
    out = pl.pallas_call(
        functools.partial(_combine_kernel, final=final),
        grid=(N_TILES,),
        in_specs=[idx(lambda i: i), idx(lambda i: jnp.minimum(i + 1, N_TILES - 1)),
                  pl.BlockSpec(memory_space=pl.ANY), tile(D), tile(ROUTER_LANES),
                  pl.BlockSpec((1, 6, D), lambda i: (_mod_row(i), 0, 0)),
                  pl.BlockSpec((1, D), lambda i: (0, 0))],
        out_specs=out_specs,
        out_shape=out_shape,
        scratch_shapes=[pltpu.VMEM((2, 2 * TM * ROW_GROUP, LANES), F32), pltpu.SemaphoreType.DMA((2,))],
        compiler_params=_cparams(("arbitrary",)),
        name="moe_combine",
    )(dest3, dest3, ys, x, wts, mod, gfin)
    return out if final else out[0]


ODD_COLS = 2048
ROPE_Q = MLA_H * MLA_ROPE
ROPE_SHIFT = ROPE_F


def rope_tables():
    t = np.arange(L_LAT)
    pos = np.stack([t // GRID_W, t % GRID_W], axis=1).astype(np.float64)
    inv = 10000.0 ** (-np.arange(ROPE_F, dtype=np.float64) / ROPE_F)
    lane = np.arange(ROPE_Q) % MLA_ROPE
    axis = lane // (2 * ROPE_F)
    first = (lane % (2 * ROPE_F)) < ROPE_F
    ang = pos[:, axis] * inv[lane % ROPE_F][None, :]
    cos, sin = np.cos(ang), np.sin(ang)
    tabs = [cos, np.where(first[None, :], -sin, 0.0), np.where(first[None, :], 0.0, sin)]
    ident = [np.ones((1, TM, ROPE_Q)), np.zeros((1, TM, ROPE_Q)), np.zeros((1, TM, ROPE_Q))]
    return [jnp.asarray(np.concatenate([i, tb.reshape(LAT_TILES_PER_SEQ, TM, ROPE_Q)], axis=0).astype(np.float32))
            for i, tb in zip(ident, tabs)]


def _rope(x, c, a, b):
    n = x.shape[1]
    return x * c[:, :n] + pltpu.roll(x, n - ROPE_SHIFT, 1) * a[:, :n] + pltpu.roll(x, ROPE_SHIFT, 1) * b[:, :n]


def _odd_in_kernel(x_ref, mod_ref, g_ref, w_ref, gq_ref, wuq_ref, gkv_ref, wukv_ref, rc_ref, ra_ref, rb_ref,
                   q_ref, kv_ref, qm_ref, ckv_ref, kvu_ref, kr_ref, knew_ref, vnew_ref):
    hb = _modulated(x_ref[...], g_ref, mod_ref, 0).astype(BF16)
    q_ref[...] = jnp.dot(hb, w_ref[:, 0:NA_W], preferred_element_type=F32)
    k = jnp.dot(hb, w_ref[:, NA_W:2 * NA_W], preferred_element_type=F32)
    v = jnp.dot(hb, w_ref[:, 2 * NA_W:3 * NA_W], preferred_element_type=F32)
    kv_ref[:, 0:NA_W] = k.astype(kv_ref.dtype)
    kv_ref[:, NA_W:2 * NA_W] = v.astype(kv_ref.dtype)

    @pl.when(pl.program_id(0) < CTX_TILES)
    def _():
        kt, vt = k.T, v.T
        for h in range(NA_H):
            knew_ref[0, h] = kt[h * NA_D:(h + 1) * NA_D, :]
            vnew_ref[0, h] = vt[h * NA_D:(h + 1) * NA_D, :]

    rest = jnp.dot(hb, w_ref[:, 3 * NA_W:ODD_COLS], preferred_element_type=F32)
    rc, ra, rb = rc_ref[0], ra_ref[0], rb_ref[0]
    qd = (_rms(rest[:, 0:MLA_QR]) * gq_ref[...]).astype(BF16)
    qm = jnp.dot(qd, wuq_ref[...], preferred_element_type=F32)
    qm_ref[:, 0:MLA_H * MLA_NOPE] = qm[:, 0:MLA_H * MLA_NOPE]
    qm_ref[:, MLA_H * MLA_NOPE:] = _rope(qm[:, MLA_H * MLA_NOPE:], rc, ra, rb)
    ckv = _rms(rest[:, MLA_QR:MLA_QR + MLA_KVR]) * gkv_ref[...]
    ckv_ref[...] = ckv
    kvu_ref[...] = jnp.dot(ckv.astype(BF16), wukv_ref[...],
                           preferred_element_type=F32).astype(kvu_ref.dtype)
    kr_ref[...] = _rope(rest[:, MLA_QR + MLA_KVR:], rc, ra, rb)


def odd_in_proj(x, mod, g, w_bf, gq, wuq_bf, gkv, wukv_bf, tabs):
    tile = lambda w: pl.BlockSpec((TM, w), lambda i: (i, 0))
    full = lambda arr: pl.BlockSpec(arr.shape, lambda i: (0,) * arr.ndim)
    tab = pl.BlockSpec((1, TM, ROPE_Q),
                       lambda i: (jnp.where(i < CTX_TILES, 0, 1 + (i - CTX_TILES) % LAT_TILES_PER_SEQ), 0, 0))
    outs = ((NA_W, F32), (2 * NA_W, BF16), (MLA_H * MLA_QK, F32), (MLA_KVR, F32),
            (MLA_H * (MLA_NOPE + MLA_V), BF16), (LANES, F32))
    cache = pl.BlockSpec((1, NA_H, NA_D, L_CTX), lambda i: (jnp.minimum(i, CTX_TILES - 1), 0, 0, 0))
    cache_shape = jax.ShapeDtypeStruct((N_CTX, NA_H, NA_D, L_CTX), F32)
    return pl.pallas_call(
        _odd_in_kernel,
        grid=(N_TILES,),
        in_specs=[tile(D), pl.BlockSpec((1, 6, D), lambda i: (_mod_row(i), 0, 0)), full(g), full(w_bf),
                  full(gq), full(wuq_bf), full(gkv), full(wukv_bf), tab, tab, tab],
        out_specs=[tile(w) for w, _ in outs] + [cache, cache],
        out_shape=[jax.ShapeDtypeStruct((T_ALL, w), dt) for w, dt in outs] + [cache_shape, cache_shape],
        compiler_params=_cparams(("arbitrary",)),
        name="odd_in",
    )(x, mod, g, w_bf, gq, wuq_bf, gkv, wukv_bf, *tabs)


LOG2E = math.log2(math.e)
NA_QSCALE = NA_D ** -0.5 * LOG2E
MLA_QSCALE = MLA_QK ** -0.5 * LOG2E
NT = (((1,), (1,)), ((), ()))


def _softmax_pv(scores, values):
    m = functools.reduce(jnp.maximum, [jnp.max(s, axis=-1, keepdims=True) for s in scores])
    ps = [jnp.exp2(s - m) for s in scores]
    den = functools.reduce(jnp.add, [jnp.sum(p, axis=-1, keepdims=True) for p in ps])
    acc = functools.reduce(jnp.add, [jnp.dot(p.astype(BF16), v, preferred_element_type=F32) for p, v in zip(ps, values)])
    return acc / den


def _pair(ref_or_val, p, base=0):
    return ref_or_val[:, base + p * LANES:base + (p + 1) * LANES]


def _low_half():
    return lax.broadcasted_iota(jnp.int32, (1, LANES), 1) < NA_D


def _rope_key_forms(kr):
    return kr.astype(BF16), pltpu.roll(kr, LANES // 2, 1).astype(BF16)


def _mla_pair(qm, p, sources, lo):
    outs = []
    for e in range(2):
        h = 2 * p + e
        qn = qm[:, h * MLA_NOPE:(h + 1) * MLA_NOPE] * MLA_QSCALE
        qr = qm[:, MLA_H * MLA_NOPE + h * MLA_ROPE:MLA_H * MLA_NOPE + (h + 1) * MLA_ROPE] * MLA_QSCALE
        z = jnp.zeros((qn.shape[0], LANES - MLA_QK), F32)
        qcat = jnp.concatenate([qn, qr, z] if e == 0 else [qr, z, qn], axis=1).astype(BF16)
        scores = []
        for kb, kr_lo, kr_hi, _ in sources:
            kcat = jnp.where(lo, kb, kr_hi) if e == 0 else jnp.where(lo, kr_lo, kb)
            scores.append(lax.dot_general(qcat, kcat, NT, preferred_element_type=F32))
        outs.append(_softmax_pv(scores, [src[3] for src in sources]))
    return jnp.where(lo, outs[0], outs[1])


def _attn_ctx_kernel(q_ref, kv_ref, qm_ref, kvu_ref, kr_ref, ona_ref, omla_ref):
    lo = _low_half()
    for p in range(NA_H // 2):
        qb = _pair(q_ref, p) * NA_QSCALE
        kb = _pair(kv_ref, p).astype(BF16)
        vb = _pair(kv_ref, p, NA_W).astype(BF16)
        outs = []
        for e in range(2):
            q = jnp.where(lo if e == 0 else jnp.logical_not(lo), qb, 0.0).astype(BF16)
            outs.append(_softmax_pv([lax.dot_general(q, kb, NT, preferred_element_type=F32)], [vb]))
        ona_ref[:, p * LANES:(p + 1) * LANES] = jnp.where(lo, outs[0], outs[1]).astype(ona_ref.dtype)
    kr_lo, kr_hi = _rope_key_forms(kr_ref[...])
    for p in range(MLA_H // 2):
        src = (_pair(kvu_ref, p).astype(BF16), kr_lo, kr_hi, _pair(kvu_ref, p, MLA_H * MLA_NOPE).astype(BF16))
        omla_ref[:, p * LANES:(p + 1) * LANES] = _mla_pair(qm_ref, p, [src], lo).astype(omla_ref.dtype)


def attn_context(q, kv, qm, kvu, kr):
    seq = lambda w: pl.BlockSpec((L_CTX, w), lambda b: (b, 0))
    return pl.pallas_call(
        _attn_ctx_kernel,
        grid=(N_CTX,),
        in_specs=[seq(NA_W), seq(2 * NA_W), seq(MLA_H * MLA_QK), seq(MLA_H * (MLA_NOPE + MLA_V)), seq(LANES)],
        out_specs=[seq(NA_W), seq(MLA_H * MLA_V)],
        out_shape=[jax.ShapeDtypeStruct((T_CTX, NA_W), BF16), jax.ShapeDtypeStruct((T_CTX, MLA_H * MLA_V), BF16)],
        compiler_params=_cparams(("arbitrary",)),
        name="attn_ctx",
    )(q, kv, qm, kvu, kr)


N_DR = 2 * NA_WIN_R - 1
GRID_ROWS = L_LAT // GRID_W


def _na_bias_kernel(t_ref, o_ref):
    neg = jnp.full((GRID_W, GRID_W), NEG, F32)
    for r in range(GRID_ROWS):
        r0 = min(max(r - NA_WIN_R // 2, 0), GRID_ROWS - NA_WIN_R)
        for kr in range(GRID_ROWS):
            in_window = r0 <= kr < r0 + NA_WIN_R
            blk = t_ref[0, kr - r + NA_WIN_R - 1] if in_window else neg
            o_ref[0, r * GRID_W:(r + 1) * GRID_W, kr * GRID_W:(kr + 1) * GRID_W] = blk


def neighbourhood_bias(rel_bias):
    c = np.arange(GRID_W)
    c0 = np.clip(c - NA_WIN_C // 2, 0, GRID_W - NA_WIN_C)
    col_ok = (c[None, :] >= c0[:, None]) & (c[None, :] < c0[:, None] + NA_WIN_C)
    dc = np.clip(c[None, :] - c[:, None], -(NA_WIN_C - 1), NA_WIN_C - 1) + NA_WIN_C - 1
    sel_c = (dc[:, :, None] == np.arange(2 * NA_WIN_C - 1)).astype(np.float32)
    t = jnp.einsum("hdj,qcj->hdqc", rel_bias.astype(F32), jnp.asarray(sel_c), precision=HIGHEST)
    t = jnp.where(jnp.asarray(col_ok)[None, None], t * LOG2E, NEG)
    return pl.pallas_call(
        _na_bias_kernel,
        grid=(NA_H,),
        in_specs=[pl.BlockSpec((1, N_DR, GRID_W, GRID_W), lambda h: (h, 0, 0, 0))],
        out_specs=pl.BlockSpec((1, L_LAT, L_LAT), lambda h: (h, 0, 0)),
        out_shape=jax.ShapeDtypeStruct((NA_H, L_LAT, L_LAT), F32),
        compiler_params=_cparams(("arbitrary",)),
        name="na_bias",
    )(t)


def _na_lat_kernel(q_ref, k_ref, v_ref, kc_ref, vc_ref, b_ref, o_ref):
    lo = _low_half()
    for p in range(NA_H // 2):
        qb = _pair(q_ref, p) * NA_QSCALE
        kb = _pair(k_ref, p).astype(BF16)
        vb = _pair(v_ref, p).astype(BF16)
        outs = []
        for e in range(2):
            h = 2 * p + e
            half = slice(e * NA_D, (e + 1) * NA_D)
            q = jnp.where(lo if e == 0 else jnp.logical_not(lo), qb, 0.0).astype(BF16)
            s1 = lax.dot_general(q, kb, NT, preferred_element_type=F32) + b_ref[h]
            s2 = jnp.dot(qb[:, half].astype(BF16), kc_ref[0, 0, h].astype(BF16), preferred_element_type=F32)
            m = jnp.maximum(jnp.max(s1, axis=-1, keepdims=True), jnp.max(s2, axis=-1, keepdims=True))
            p1, p2 = jnp.exp2(s1 - m), jnp.exp2(s2 - m)
            den = jnp.sum(p1, axis=-1, keepdims=True) + jnp.sum(p2, axis=-1, keepdims=True)
            a1 = jnp.dot(p1.astype(BF16), vb, preferred_element_type=F32)
            a2 = lax.dot_general(p2.astype(BF16), vc_ref[0, 0, h].astype(BF16), NT, preferred_element_type=F32)
            outs.append((a1[:, half] + a2) / den)
        o_ref[:, p * LANES:(p + 1) * LANES] = jnp.concatenate(outs, axis=1).astype(o_ref.dtype)


def attn_neighbourhood_latent(q, kv, cache_kt, cache_vt, bias):
    nq = L_LAT // TM
    t0 = T_CTX // TM
    s0 = T_CTX // L_LAT
    cache = pl.BlockSpec((1, 1, NA_H, NA_D, PAST), lambda qt, b: (b, 0, 0, 0, 0))
    return pl.pallas_call(
        _na_lat_kernel,
        grid=(nq, N_LAT),
        in_specs=[pl.BlockSpec((TM, NA_W), lambda qt, b: (t0 + b * nq + qt, 0)),
                  pl.BlockSpec((L_LAT, NA_W), lambda qt, b: (s0 + b, 0)),
                  pl.BlockSpec((L_LAT, NA_W), lambda qt, b: (s0 + b, 1)),
                  cache, cache,
                  pl.BlockSpec((NA_H, TM, L_LAT), lambda qt, b: (0, qt, 0))],
        out_specs=pl.BlockSpec((TM, NA_W), lambda qt, b: (b * nq + qt, 0)),
        out_shape=jax.ShapeDtypeStruct((T_LAT, NA_W), BF16),
        compiler_params=_cparams(("arbitrary", "arbitrary")),
        name="attn_na_lat",
    )(q, kv, kv, cache_kt, cache_vt, bias)


def _mla_lat_kernel(qm_ref, kvu_ref, kr_ref, ckv_ref, krc_ref, wukv_ref, o_ref):
    lo = _low_half()
    kvc = jnp.dot(ckv_ref[0, 0].astype(BF16), wukv_ref[...], preferred_element_type=F32)
    kr_lo, kr_hi = _rope_key_forms(kr_ref[...])
    krc = jnp.concatenate([krc_ref[0, 0], jnp.zeros((PAST, LANES - MLA_ROPE), F32)], axis=1)
    krc_lo, krc_hi = _rope_key_forms(krc)
    vbase = MLA_H * MLA_NOPE
    for p in range(MLA_H // 2):
        lat = (_pair(kvu_ref, p).astype(BF16), kr_lo, kr_hi, _pair(kvu_ref, p, vbase).astype(BF16))
        ctx = (_pair(kvc, p).astype(BF16), krc_lo, krc_hi, _pair(kvc, p, vbase).astype(BF16))
        o_ref[:, p * LANES:(p + 1) * LANES] = _mla_pair(qm_ref, p, [lat, ctx], lo).astype(o_ref.dtype)


def attn_mla_latent(qm, kvu, kr, cache_ckv, cache_krope, wukv_bf):
    nq = L_LAT // TM
    t0 = T_CTX // TM
    s0 = T_CTX // L_LAT
    return pl.pallas_call(
        _mla_lat_kernel,
        grid=(nq, N_LAT),
        in_specs=[pl.BlockSpec((TM, MLA_H * MLA_QK), lambda qt, b: (t0 + b * nq + qt, 0)),
                  pl.BlockSpec((L_LAT, MLA_H * (MLA_NOPE + MLA_V)), lambda qt, b: (s0 + b, 0)),
                  pl.BlockSpec((L_LAT, LANES), lambda qt, b: (s0 + b, 0)),
                  pl.BlockSpec((1, 1, PAST, MLA_KVR), lambda qt, b: (b, 0, 0, 0)),
                  pl.BlockSpec((1, 1, PAST, MLA_ROPE), lambda qt, b: (b, 0, 0, 0)),
                  pl.BlockSpec(wukv_bf.shape, lambda qt, b: (0, 0))],
        out_specs=pl.BlockSpec((TM, MLA_H * MLA_V), lambda qt, b: (b * nq + qt, 0)),
        out_shape=jax.ShapeDtypeStruct((T_LAT, MLA_H * MLA_V), BF16),
        compiler_params=_cparams(("arbitrary", "arbitrary")),
        name="attn_mla_lat",
    )(qm, kvu, kr, cache_ckv, cache_krope, wukv_bf)


def moe_block(h2, ids, wts, cnt3, x, mod, gfin, w_gate, w_up, w_down, layer, *, final):
    cnt, loc, gdst, ends, tile_expert, n_used = route_tables(cnt3)
    xs, dest = dispatch_rows(h2, ids, cnt, loc, gdst, ends)
    ys = grouped_experts(xs, w_gate, w_up, w_down, tile_expert, n_used, layer)
    return moe_combine(ys, dest, x, wts, mod, gfin, final=final)


def _pad_lanes(a):
    return jnp.pad(a, ((0, 0), (0, LANES - a.shape[1])))


def _hyena_features(L):
    t = np.linspace(0.0, 1.0, L)[:, None]
    w = 2.0 * math.pi * np.arange(L) / L
    bands = np.linspace(1e-4, HY_BANDS - 1, HY_BANDS)
    ang = w[:, None] * bands[None]
    feat = np.concatenate([t, np.cos(ang), -np.sin(ang)], axis=-1)
    return jnp.asarray(np.pad(feat, ((0, 0), (0, LANES - HY_FEAT))).astype(np.float32))


def _router_params(w_gr, b_gr, w_er, b_er):
    wr = _pad_lanes(jnp.concatenate([w_gr, w_er], axis=1))
    br = _pad_lanes(jnp.concatenate([b_gr, b_er])[None])
    wr_hi = wr.astype(BF16)
    wr_lo = (wr - wr_hi.astype(F32)).astype(BF16)
    return jnp.stack([wr_hi, wr_lo]), br


def _even_layer(x, mod, g_mix, state, w_in, conv_w, conv_b, a_log, dt_bias, d_skip, g_ssd, hy_conv_w, hy_conv_b,
                hy_w1, hy_b1, hy_w2, hy_b2, hy_w3, hy_freq, hy_bias):
    n0 = D + SSD_XBC
    w_bf = jnp.concatenate([w_in[:, :n0], w_in[:, n0 + SSD_H:], w_in[:, n0:n0 + SSD_H],
                            jnp.zeros((D, LANES - SSD_H), F32)], axis=1).astype(BF16)
    z, xbc, hy, dtr = even_in_proj(x, mod, g_mix, w_bf)
    small = (conv_w, conv_b[None], _pad_lanes(dt_bias), _pad_lanes(a_log), jnp.repeat(d_skip, SSD_P)[None], g_ssd[None])
    y_c, fin = ssd_mixer(xbc, dtr, z, None, *small, L=L_CTX, n_seq=N_CTX, row_off=0)
    (y_l,) = ssd_mixer(xbc, dtr, z, state.reshape(N_LAT, 2, SSD_H * SSD_P, SSD_N), *small,
                       L=L_LAT, n_seq=N_LAT, row_off=T_CTX)
    w1 = jnp.pad(hy_w1, ((0, LANES - HY_FEAT), (0, 0)))
    w3r = hy_w3.reshape(HY_HID, 4, D).transpose(1, 0, 2)
    deltas = jnp.asarray(np.linspace(HY_MIN_DECAY, HY_MAX_DECAY, D).astype(np.float32))[None]
    us = []
    for L, n_seq, off in ((L_CTX, N_CTX, 0), (L_LAT, N_LAT, T_CTX)):
        h4, hm = hyena_filter_spectra(_hyena_features(L), w1, hy_b1[None], hy_w2, hy_b2[None], hy_freq, w3r, deltas, L=L)
        us.append(hyena_mixer(hy, hy_conv_w, hy_conv_b[None], h4, hm, hy_bias, L=L, n_seq=n_seq, row_off=off))
    return (y_c, y_l), tuple(us), fin


def _odd_layer(x, mod, g_mix, cache_k, cache_v, cache_ckv, cache_kr, rel_bias, w_in, g_q, w_uq, g_kv, w_ukv):
    w_bf = jnp.pad(w_in, ((0, 0), (0, ODD_COLS - w_in.shape[1]))).astype(BF16)
    wuq = w_uq.reshape(MLA_QR, MLA_H, MLA_QK)
    wuq_bf = jnp.concatenate([wuq[:, :, :MLA_NOPE].reshape(MLA_QR, -1), wuq[:, :, MLA_NOPE:].reshape(MLA_QR, -1)],
                             axis=1).astype(BF16)
    wukv = w_ukv.reshape(MLA_KVR, MLA_H, MLA_NOPE + MLA_V)
    wukv_bf = jnp.concatenate([wukv[:, :, :MLA_NOPE].reshape(MLA_KVR, -1), wukv[:, :, MLA_NOPE:].reshape(MLA_KVR, -1)],
                              axis=1).astype(BF16)
    q, kv, qm, ckv, kvu, kr, k_new, v_new = odd_in_proj(x, mod, g_mix, w_bf, g_q[None], wuq_bf, g_kv[None], wukv_bf,
                                                        rope_tables())
    ona_c, omla_c = attn_context(q, kv, qm, kvu, kr)
    ona_l = attn_neighbourhood_latent(q, kv, jnp.swapaxes(cache_k, 3, 4), jnp.swapaxes(cache_v, 3, 4),
                                      neighbourhood_bias(rel_bias))
    omla_l = attn_mla_latent(qm, kvu, kr, cache_ckv, cache_kr, wukv_bf)
    return (ona_c, ona_l), (omla_c, omla_l), k_new, v_new, ckv, kr


def kernel(x_prompt, x_sample, state_ssd, cache_na_k, cache_na_v, cache_mla_ckv, cache_mla_krope, c, c_ctx, w_ada, b_ada, norm_mix, norm_ffn, norm_final, ev_w_in, ev_conv_w, ev_conv_b, ssd_A_log, ssd_dt_bias, ssd_d, ssd_norm, hy_conv_w, hy_conv_b, hy_w1, hy_b1, hy_w2, hy_b2, hy_w3, hy_freq, hy_bias, ev_w_out, od_w_in, mla_q_norm, mla_w_uq, mla_kv_norm, mla_w_ukv, na_rel_bias, od_w_out, moe_w_gr, moe_b_gr, moe_w_er, moe_b_er, moe_w_gate, moe_w_up, moe_w_down):
    x = (x_prompt.reshape(T_CTX, D), x_sample.reshape(T_LAT, D))
    cvec = jnp.zeros((MOD_ROWS, D), F32).at[0].set(c_ctx).at[1:1 + N_LAT].set(c)
    mod = ada_modulation(cvec, w_ada, b_ada)
    gfin = norm_final[None]

    y, u, fin = _even_layer(x, mod[0], norm_mix[0][None], state_ssd[:, 0], ev_w_in[0], ev_conv_w[0], ev_conv_b[0],
                            ssd_A_log[0], ssd_dt_bias[0], ssd_d[0], ssd_norm[0], hy_conv_w[0], hy_conv_b[0],
                            hy_w1[0], hy_b1[0], hy_w2[0], hy_b2[0], hy_w3[0], hy_freq[0], hy_bias[0])
    wr, br = _router_params(moe_w_gr[0], moe_b_gr[0], moe_w_er[0], moe_b_er[0])
    xn, h2, ids, wts, cnt3 = out_proj_router([y, u], ev_w_out[0].astype(BF16), x, mod[0], norm_ffn[0][None], wr, br)
    x = moe_block(h2, ids, wts, cnt3, xn, mod[0], gfin, moe_w_gate, moe_w_up, moe_w_down, 0, final=False)

    o_na, o_mla, k_new, v_new, ckv, kr = _odd_layer(x, mod[1], norm_mix[1][None], cache_na_k, cache_na_v, cache_mla_ckv,
                                           cache_mla_krope, na_rel_bias[0], od_w_in[0], mla_q_norm[0], mla_w_uq[0],
                                           mla_kv_norm[0], mla_w_ukv[0])
    wr, br = _router_params(moe_w_gr[1], moe_b_gr[1], moe_w_er[1], moe_b_er[1])
    xn, h2, ids, wts, cnt3 = out_proj_router([o_na, o_mla], od_w_out[0].astype(BF16), x, mod[1], norm_ffn[1][None], wr, br)
    y_c, y_l = moe_block(h2, ids, wts, cnt3, xn, mod[1], gfin, moe_w_gate, moe_w_up, moe_w_down, 1, final=True)

    return (y_c.reshape(N_CTX, L_CTX, D),
            y_l.reshape(N_LAT, L_LAT, D),
            fin.reshape(N_CTX, 1, 2, SSD_H, SSD_P, SSD_N),
            jnp.swapaxes(k_new, 2, 3)[:, None],
            jnp.swapaxes(v_new, 2, 3)[:, None],
            ckv[:T_CTX].reshape(N_CTX, 1, L_CTX, MLA_KVR),
            kr[:T_CTX, :MLA_ROPE].reshape(N_CTX, 1, L_CTX, MLA_ROPE))
```

```python
import functools
import math

import numpy as np
import jax
import jax.numpy as jnp
from jax import lax
from jax.experimental import pallas as pl
from jax.experimental.pallas import tpu as pltpu

F32 = jnp.float32
BF16 = jnp.bfloat16
HIGHEST = lax.Precision.HIGHEST

D = 1024
N_CTX, L_CTX = 16, 256
N_LAT, L_LAT = 8, 1024
T_CTX = N_CTX * L_CTX
T_LAT = N_LAT * L_LAT
T_ALL = T_CTX + T_LAT
PAST = 512
GRID_W = 64
EPS = 1e-6
NEG = -1e30

SSD_H, SSD_P, SSD_N, SSD_G = 16, 64, 128, 2
SSD_XBC = D + 2 * SSD_G * SSD_N
SSD_K = 5
CHUNK = 128

HY_K = 3
HY_BANDS = 16
HY_FEAT = 1 + 2 * HY_BANDS
HY_HID = 64
HY_MIN_DECAY = abs(math.log(1e-2) / 1.5)
HY_MAX_DECAY = abs(math.log(1e-2) / 0.3)

NA_H, NA_D = 8, 64
NA_W = NA_H * NA_D
NA_WIN_R, NA_WIN_C = 8, 16
MLA_H, MLA_QR, MLA_KVR = 8, 256, 128
MLA_NOPE, MLA_ROPE, MLA_V = 64, 32, 64
MLA_QK = MLA_NOPE + MLA_ROPE
ROPE_F = MLA_ROPE // 4

MOE_G, MOE_PG, MOE_E, MOE_F = 4, 8, 32, 256

LANES = 128
SUBLANES = 8
VMEM_LIMIT = 56 * 1024 * 1024

TM = 256
N_TILES = T_ALL // TM
CTX_TILES = T_CTX // TM
LAT_TILES_PER_SEQ = L_LAT // TM
MOD_ROWS = 16


def _cparams(sem):
    return pltpu.CompilerParams(dimension_semantics=sem, vmem_limit_bytes=VMEM_LIMIT)


def _mod_row(i):
    return jnp.where(i < CTX_TILES, 0, 1 + (i - CTX_TILES) // LAT_TILES_PER_SEQ)


def _silu(x):
    return x * jax.nn.sigmoid(x)


def _rms(x):
    return x * lax.rsqrt(jnp.mean(x * x, axis=-1, keepdims=True) + EPS)


def _ada_kernel(c_ref, w_ref, b_ref, o_ref):
    c = c_ref[...]
    o_ref[0] = jnp.dot(_silu(c), w_ref[0], precision=HIGHEST, preferred_element_type=F32) + b_ref[0]


def ada_modulation(cvec, w_ada, b_ada):
    depth = w_ada.shape[0]
    out = pl.pallas_call(
        _ada_kernel,
        grid=(depth, 6),
        in_specs=[
            pl.BlockSpec((MOD_ROWS, D), lambda l, j: (0, 0)),
            pl.BlockSpec((1, D, D), lambda l, j: (l, 0, j)),
            pl.BlockSpec((1, 1, D), lambda l, j: (l, 0, j)),
        ],
        out_specs=pl.BlockSpec((1, MOD_ROWS, D), lambda l, j: (l, 0, j)),
        out_shape=jax.ShapeDtypeStruct((depth, MOD_ROWS, 6 * D), F32),
        compiler_params=_cparams(("arbitrary", "arbitrary")),
        name="ada",
    )(cvec, w_ada, b_ada.reshape(depth, 1, 6 * D))
    return out.reshape(depth, MOD_ROWS, 6, D)


PROJ_CHUNK = 512


def _modulated(x, g_ref, mod_ref, shift_row):
    h = _rms(x) * g_ref[...]
    return h * (1.0 + mod_ref[0, shift_row + 1:shift_row + 2, :]) + mod_ref[0, shift_row:shift_row + 1, :]


IN_TM = 512
IN_CTX_TILES = T_CTX // IN_TM


def _even_in_kernel(xc_ref, xl_ref, mod_ref, g_ref, wa_ref, wh_ref, wd_ref, z_ref, xbc_ref, hy_ref, dt_ref):
    x = jnp.where(pl.program_id(0) < IN_CTX_TILES, xc_ref[...], xl_ref[...])
    hb = _modulated(x, g_ref, mod_ref, 0).astype(BF16)
    for o_ref, w_ref, col in ((z_ref, wa_ref, 0), (xbc_ref, wa_ref, D), (hy_ref, wh_ref, 0), (dt_ref, wd_ref, 0)):
        width = o_ref.shape[1]
        for c0 in range(0, width, PROJ_CHUNK):
            c1 = min(c0 + PROJ_CHUNK, width)
            o_ref[:, c0:c1] = jnp.dot(hb, w_ref[:, col + c0:col + c1], preferred_element_type=F32)


def _pair_specs(width):
    return [pl.BlockSpec((TM, width), lambda i: (jnp.minimum(i, CTX_TILES - 1), 0)),
            pl.BlockSpec((TM, width), lambda i: (jnp.maximum(i - CTX_TILES, 0), 0))]


def even_in_proj(x_pair, mod, g, w_parts):
    widths = (D, SSD_XBC, 3 * D, LANES)
    mod_row = lambda i: jnp.where(i < IN_CTX_TILES, 0, 1 + (i - IN_CTX_TILES) // (L_LAT // IN_TM))
    return pl.pallas_call(
        _even_in_kernel,
        grid=(T_ALL // IN_TM,),
        in_specs=[
            pl.BlockSpec((IN_TM, D), lambda i: (jnp.minimum(i, IN_CTX_TILES - 1), 0)),
            pl.BlockSpec((IN_TM, D), lambda i: (jnp.maximum(i - IN_CTX_TILES, 0), 0)),
            pl.BlockSpec((1, 6, D), lambda i: (mod_row(i), 0, 0)),
            pl.BlockSpec((1, D), lambda i: (0, 0)),
            *[_const_spec(w) for w in w_parts],
        ],
        out_specs=[pl.BlockSpec((IN_TM, w), lambda i: (i, 0)) for w in widths],
        out_shape=[jax.ShapeDtypeStruct((T_ALL, w), F32) for w in widths],
        compiler_params=_cparams(("arbitrary",)),
        name="even_in",
    )(*x_pair, mod, g, *w_parts)


PAD = SUBLANES


def _ssd_kernel(*refs, L, has_init):
    if has_init:
        (xbc_ref, dt_ref, z_ref, init_ref, cw_ref, cb_ref, dtb_ref, alog_ref, dsk_ref, gs_ref,
         y_ref, xp_s, xc_s, ya_s, st_s) = refs
        fin_ref = None
    else:
        (xbc_ref, dt_ref, z_ref, cw_ref, cb_ref, dtb_ref, alog_ref, dsk_ref, gs_ref,
         y_ref, fin_ref, xp_s, xc_s, ya_s, st_s) = refs
        init_ref = None
    nc = L // CHUNK
    half = SSD_K // 2

    xp_s[0:PAD, :] = jnp.zeros((PAD, SSD_XBC), F32)
    xp_s[PAD + L:2 * PAD + L, :] = jnp.zeros((PAD, SSD_XBC), F32)
    xp_s[PAD:PAD + L, :] = xbc_ref[...]
    for c in range(nc):
        base = PAD + c * CHUNK - half
        for j in range(SSD_XBC // LANES):
            cols = slice(j * LANES, (j + 1) * LANES)
            acc = cb_ref[:, cols] + xp_s[base:base + CHUNK, cols] * cw_ref[0:1, cols]
            for k in range(1, SSD_K):
                acc = acc + xp_s[base + k:base + k + CHUNK, cols] * cw_ref[k:k + 1, cols]
            xc_s[c * CHUNK:(c + 1) * CHUNK, cols] = _silu(acc)

    row = lax.broadcasted_iota(jnp.int32, (CHUNK, CHUNK), 0)
    colm = lax.broadcasted_iota(jnp.int32, (CHUNK, CHUNK), 1)
    lane_lo = colm < SSD_P
    tri_lo = (colm <= row).astype(F32)
    tri_up = (colm >= row).astype(F32)

    for d in range(2):
        causal = (colm <= row) if d == 0 else (colm >= row)
        for j in range(SSD_H * SSD_P // CHUNK):
            if has_init:
                st_s[:, j * CHUNK:(j + 1) * CHUNK] = init_ref[0, d, j * CHUNK:(j + 1) * CHUNK, :].T
            else:
                st_s[:, j * CHUNK:(j + 1) * CHUNK] = jnp.zeros((CHUNK, CHUNK), F32)

        def chunk_body(ci, carry, d=d, causal=causal):
            c = ci if d == 0 else nc - 1 - ci
            r0 = pl.multiple_of(c * CHUNK, CHUNK)
            dt = jax.nn.softplus(dt_ref[pl.ds(r0, CHUNK), :] + dtb_ref[d:d + 1, :])
            a = dt * (-jnp.exp(alog_ref[d:d + 1, :]))
            tri = tri_lo if d == 0 else tri_up
            cs = jnp.dot(tri, a, precision=HIGHEST, preferred_element_type=F32)
            cs_t = jnp.dot(a.T, tri.T, precision=HIGHEST, preferred_element_type=F32)
            edge = cs[CHUNK - 1:CHUNK, :] if d == 0 else cs[0:1, :]
            for g in range(SSD_G):
                bm = xc_s[pl.ds(r0, CHUNK), D + g * SSD_N:D + (g + 1) * SSD_N]
                cm = xc_s[pl.ds(r0, CHUNK), D + (SSD_G + g) * SSD_N:D + (SSD_G + g + 1) * SSD_N]
                bm_b, cm_b = bm.astype(BF16), cm.astype(BF16)
                cb = lax.dot_general(cm_b, bm_b, (((1,), (1,)), ((), ())), preferred_element_type=F32)
                bm_t = bm.T.astype(BF16)
                pairs = SSD_H // SSD_G // 2
                for pp in range(pairs):
                    p = g * pairs + pp
                    h0, h1 = 2 * p, 2 * p + 1
                    cols = slice(p * CHUNK, (p + 1) * CHUNK)
                    xs = xc_s[pl.ds(r0, CHUNK), cols]
                    xdt = xs * jnp.where(lane_lo, dt[:, h0:h0 + 1], dt[:, h1:h1 + 1])
                    cs_b = [jnp.broadcast_to(cs[:, h:h + 1], (CHUNK, CHUNK)) for h in (h0, h1)]
                    ms = [cb * jnp.exp(jnp.where(causal, cs_b[e] - cs_t[h:h + 1, :], NEG)) for e, h in enumerate((h0, h1))]
                    cs_p = jnp.where(lane_lo, cs_b[0], cs_b[1])
                    edge_p = jnp.where(lane_lo[0:1, :], edge[:, h0:h0 + 1], edge[:, h1:h1 + 1])
                    mcat = jnp.concatenate(ms, axis=1).astype(BF16)
                    xbd = jnp.concatenate([jnp.where(lane_lo, xdt, 0.0), jnp.where(lane_lo, 0.0, xdt)],
                                          axis=0).astype(BF16)
                    y_diag = jnp.dot(mcat, xbd, preferred_element_type=F32)
                    st = st_s[:, cols]
                    y_off = jnp.dot(cm_b, st.astype(BF16), preferred_element_type=F32)
                    y_off = y_off * jnp.exp(cs_p)
                    y = y_diag + y_off
                    if d == 0:
                        ya_s[pl.ds(r0, CHUNK), cols] = y
                    else:
                        ya_s[pl.ds(r0, CHUNK), cols] = ya_s[pl.ds(r0, CHUNK), cols] + y
                    xdd = (xdt * jnp.exp(edge_p - cs_p)).astype(BF16)
                    snew = jnp.dot(bm_t, xdd, preferred_element_type=F32)
                    st_s[:, cols] = st * jnp.exp(edge_p) + snew
            return carry

        lax.fori_loop(0, nc, chunk_body, 0)
        if fin_ref is not None:
            for j in range(SSD_H * SSD_P // CHUNK):
                fin_ref[0, d, j * CHUNK:(j + 1) * CHUNK, :] = st_s[:, j * CHUNK:(j + 1) * CHUNK].T

    def out_body(c, carry):
        r0 = pl.multiple_of(c * CHUNK, CHUNK)
        y = ya_s[pl.ds(r0, CHUNK), :] + xc_s[pl.ds(r0, CHUNK), 0:D] * dsk_ref[...]
        y = y * _silu(z_ref[pl.ds(r0, CHUNK), :])
        y_ref[pl.ds(r0, CHUNK), :] = (_rms(y) * gs_ref[...]).astype(y_ref.dtype)
        return carry

    lax.fori_loop(0, nc, out_body, 0)


def ssd_mixer(xbc, dtr, z, init, cw, cb, dtb, alog, dsk, gs, *, L, n_seq, row_off):
    blk0 = row_off // L
    has_init = init is not None
    seq = lambda w: pl.BlockSpec((L, w), lambda b: (blk0 + b, 0))
    full = lambda arr: pl.BlockSpec(arr.shape, lambda b: (0,) * arr.ndim)
    in_specs = [seq(SSD_XBC), seq(LANES), seq(D)]
    args = [xbc, dtr, z]
    if has_init:
        in_specs.append(pl.BlockSpec((1, 2, SSD_H * SSD_P, SSD_N), lambda b: (b, 0, 0, 0)))
        args.append(init)
    small = [cw, cb, dtb, alog, dsk, gs]
    in_specs += [full(a) for a in small]
    args += small
    out_specs = [pl.BlockSpec((L, D), lambda b: (b, 0))]
    out_shape = [jax.ShapeDtypeStruct((n_seq * L, D), BF16)]
    if not has_init:
        out_specs.append(pl.BlockSpec((1, 2, SSD_H * SSD_P, SSD_N), lambda b: (b, 0, 0, 0)))
        out_shape.append(jax.ShapeDtypeStruct((n_seq, 2, SSD_H * SSD_P, SSD_N), F32))
    return pl.pallas_call(
        functools.partial(_ssd_kernel, L=L, has_init=has_init),
        grid=(n_seq,),
        in_specs=in_specs,
        out_specs=out_specs,
        out_shape=out_shape,
        scratch_shapes=[
            pltpu.VMEM((L + 2 * PAD, SSD_XBC), F32),
            pltpu.VMEM((L, SSD_XBC), F32),
            pltpu.VMEM((L, D), F32),
            pltpu.VMEM((SSD_N, SSD_H * SSD_P), F32),
        ],
        compiler_params=_cparams(("arbitrary",)),
        name=f"ssd_{L}",
    )(*args)


HY_CB = 256


def filter_dft_matrices(L):
    H = L // 2
    s = np.arange(L, dtype=np.int64)[None, :]
    k = np.arange(H, dtype=np.int64)[:, None]
    ang = lambda kk: ((kk * s) % (2 * L)).astype(np.float64) * (math.pi / L)
    ca, cb = np.cos(ang(k)), np.cos(ang(L - k))
    sa, sb = np.sin(ang(k)), np.sin(ang(L - k))
    cb[0] = np.where(s[0] % 2 == 0, 1.0, -1.0)
    sa[0], sb[0] = 0.0, 0.0
    fm = np.zeros((2 * SUBLANES, L))
    fm[0], fm[1] = np.cos(ang(H))[0], np.sin(ang(H))[0]
    mats = (np.concatenate([ca, cb], axis=0), np.concatenate([sa, sb], axis=0), fm)
    return tuple(jnp.asarray(m.astype(np.float32)).astype(BF16) for m in mats)


def _const_spec(arr):
    return pl.BlockSpec(arr.shape, lambda *_: (0,) * arr.ndim, pipeline_mode=pl.Buffered(1))


def _hy_filter_kernel(feat_ref, w1_ref, b1_ref, w2_ref, b2_ref, fr_ref, w3_ref, dl_ref, fs_ref, fd_ref, fm_ref,
                      h_ref, hm_ref, *, L):
    H = L // 2
    hp = functools.partial(jnp.dot, precision=HIGHEST, preferred_element_type=F32)
    hdn = jnp.sin(fr_ref[0:1, :] * (hp(feat_ref[...], w1_ref[...]) + b1_ref[...]))
    hdn = jnp.sin(fr_ref[1:2, :] * (hp(hdn, w2_ref[...]) + b2_ref[...]))
    rowi = lax.broadcasted_iota(jnp.int32, (L, 1), 0)
    t = rowi.astype(F32) * (1.0 / (L - 1))
    dec = jnp.exp(-t * dl_ref[...])
    first = rowi == 0
    for o in range(2):
        fwd = hp(hdn, w3_ref[2 * o]) * dec
        bwd = jnp.where(first, 0.0, hp(hdn, w3_ref[2 * o + 1]) * dec)
        hs, hd = (fwd + bwd).astype(BF16), (fwd - bwd).astype(BF16)
        ss = jnp.dot(fs_ref[...], hs, preferred_element_type=F32)
        sd = jnp.dot(fd_ref[...], hd, preferred_element_type=F32)
        h_ref[o, 0] = ss[0:H]
        h_ref[o, 1] = sd[0:H]
        h_ref[o, 2] = ss[H:L]
        h_ref[o, 3] = sd[H:L]
        mid_r = jnp.dot(fm_ref[...], hs, preferred_element_type=F32)
        mid_n = jnp.dot(fm_ref[...], hd, preferred_element_type=F32)
        hm_ref[o] = jnp.concatenate([mid_r[0:1], mid_n[1:2], jnp.zeros((SUBLANES - 2, mid_r.shape[1]), F32)], axis=0)


def hyena_filter_spectra(feat, w1, b1, w2, b2, freq, w3r, deltas, *, L):
    full = lambda arr: pl.BlockSpec(arr.shape, lambda j: (0,) * arr.ndim)
    mats = filter_dft_matrices(L)
    return pl.pallas_call(
        functools.partial(_hy_filter_kernel, L=L),
        grid=(D // HY_CB,),
        in_specs=[full(feat), full(w1), full(b1), full(w2), full(b2), full(freq),
                  pl.BlockSpec((4, HY_HID, HY_CB), lambda j: (0, 0, j)),
                  pl.BlockSpec((1, HY_CB), lambda j: (0, j))] + [_const_spec(m) for m in mats],
        out_specs=[pl.BlockSpec((2, 4, L // 2, HY_CB), lambda j: (0, 0, 0, j)),
                   pl.BlockSpec((2, SUBLANES, HY_CB), lambda j: (0, 0, j))],
        out_shape=[jax.ShapeDtypeStruct((2, 4, L // 2, D), F32), jax.ShapeDtypeStruct((2, SUBLANES, D), F32)],
        compiler_params=_cparams(("arbitrary",)),
        name=f"hy_filter_{L}",
    )(feat, w1, b1, w2, b2, freq, w3r, deltas, *mats)


def split_dft_matrices(L):
    H = L // 2
    k = np.arange(H, dtype=np.int64)[:, None]
    m = np.arange(H, dtype=np.int64)[None, :]
    alt = np.where(m % 2 == 0, 1.0, -1.0)
    ang_e = ((k * m) % L).astype(np.float64) * (2 * math.pi / L)
    ang_o = ((k * (2 * m + 1)) % (2 * L)).astype(np.float64) * (math.pi / L)
    ce, se, co, so = np.cos(ang_e), np.sin(ang_e), np.cos(ang_o), np.sin(ang_o)
    se[0], so[0] = alt[0], alt[0]
    w = np.where(k == 0, 1.0, 2.0) / (2 * L)
    fe = np.concatenate([ce, se], axis=0)
    fo = np.concatenate([co, so], axis=0)
    ge = np.concatenate([(ce * w).T, se.T / L], axis=1)
    go = np.concatenate([(co * w).T, so.T / L], axis=1)
    return tuple(jnp.asarray(a.astype(np.float32)).astype(BF16) for a in (fe, fo, ge, go))


def _store_lane_blocks(ref, val):
    for c in range(ref.shape[0]):
        ref[c] = val[:, c * LANES:(c + 1) * LANES]


def _load_parity(ref, parity, n):
    return jnp.concatenate([ref[c, pl.ds(parity, n, stride=2), :] for c in range(ref.shape[0])], axis=1)


def _hyena_kernel(p0_ref, p1_ref, p2_ref, w0_ref, w1_ref, w2_ref, b0_ref, b1_ref, b2_ref, h_ref, hm_ref, hb_ref,
                  fe_ref, fo_ref, ge_ref, go_ref, o_ref, xp_s, u_s, y_s, *, L, cb):
    H = L // 2
    xp_s[0:PAD, :] = jnp.zeros((PAD, cb), F32)
    xp_s[PAD + L:2 * PAD + L, :] = jnp.zeros((PAD, cb), F32)
    first = lax.broadcasted_iota(jnp.int32, (H, 1), 0) == 0

    def conv(p_ref, w_ref, b_ref):
        xp_s[PAD:PAD + L, :] = p_ref[...]
        acc = b_ref[...] + xp_s[PAD - 1:PAD - 1 + L, :] * w_ref[0:1, :]
        for k in range(1, HY_K):
            acc = acc + xp_s[PAD - 1 + k:PAD - 1 + k + L, :] * w_ref[k:k + 1, :]
        return acc

    u = conv(p0_ref, w0_ref, b0_ref)
    for o, (p_ref, w_ref, b_ref) in enumerate(((p1_ref, w1_ref, b1_ref), (p2_ref, w2_ref, b2_ref))):
        _store_lane_blocks(u_s, u)
        se = jnp.dot(fe_ref[...], _load_parity(u_s, 0, H).astype(BF16), preferred_element_type=F32)
        so = jnp.dot(fo_ref[...], _load_parity(u_s, 1, H).astype(BF16), preferred_element_type=F32)
        e, es, od, os_ = se[0:H], se[H:L], so[0:H], so[H:L]
        b0, b1 = e + od, e - od
        b2 = jnp.where(first, es, es + os_)
        b3 = jnp.where(first, os_, os_ - es)
        har, han, hbr, hbn = h_ref[o, 0], h_ref[o, 1], h_ref[o, 2], h_ref[o, 3]
        hmr, hmn = hm_ref[o, 0:1, :], hm_ref[o, 1:2, :]
        y0 = b0 * har - b2 * han
        y1 = b1 * hbr - b3 * hbn
        y2 = b0 * han + b2 * har
        y3 = b1 * hbn + b3 * hbr
        mid_r = b2[0:1] * hmr - b3[0:1] * hmn
        mid_n = b2[0:1] * hmn + b3[0:1] * hmr
        de = jnp.where(first, mid_r, y2 - y3)
        do = jnp.where(first, mid_n, y2 + y3)
        ye = jnp.dot(ge_ref[...], jnp.concatenate([y0 + y1, de], axis=0).astype(BF16), preferred_element_type=F32)
        yo = jnp.dot(go_ref[...], jnp.concatenate([y0 - y1, do], axis=0).astype(BF16), preferred_element_type=F32)
        for c in range(cb // LANES):
            y_s[c, pl.ds(0, H, stride=2), :] = ye[:, c * LANES:(c + 1) * LANES]
            y_s[c, pl.ds(1, H, stride=2), :] = yo[:, c * LANES:(c + 1) * LANES]
        y = jnp.concatenate([y_s[c] for c in range(cb // LANES)], axis=1)
        u = conv(p_ref, w_ref, b_ref) * (y + u * hb_ref[o:o + 1, :])
    o_ref[...] = u.astype(o_ref.dtype)


def hyena_mixer(hy, conv_w, conv_b, h4, hm, hy_bias, *, L, n_seq, row_off):
    blk0 = row_off // L
    cb = min(D, HY_CB * (L_LAT // L))
    nj = D // cb
    H = L // 2
    part = lambda q: pl.BlockSpec((L, cb), lambda j, b: (blk0 + b, q * nj + j))
    wpart = lambda q: pl.BlockSpec((HY_K, cb), lambda j, b: (0, q * nj + j))
    bpart = lambda q: pl.BlockSpec((1, cb), lambda j, b: (0, q * nj + j))
    mats = split_dft_matrices(L)
    return pl.pallas_call(
        functools.partial(_hyena_kernel, L=L, cb=cb),
        grid=(nj, n_seq),
        in_specs=[part(0), part(1), part(2), wpart(0), wpart(1), wpart(2), bpart(0), bpart(1), bpart(2),
                  pl.BlockSpec((2, 4, H, cb), lambda j, b: (0, 0, 0, j)),
                  pl.BlockSpec((2, SUBLANES, cb), lambda j, b: (0, 0, j)),
                  pl.BlockSpec((2, cb), lambda j, b: (0, j))]
                 + [_const_spec(m) for m in mats],
        out_specs=pl.BlockSpec((L, cb), lambda j, b: (b, j)),
        out_shape=jax.ShapeDtypeStruct((n_seq * L, D), BF16),
        scratch_shapes=[pltpu.VMEM((L + 2 * PAD, cb), F32), pltpu.VMEM((cb // LANES, L, LANES), F32),
                        pltpu.VMEM((cb // LANES, L, LANES), F32)],
        compiler_params=_cparams(("arbitrary", "arbitrary")),
        name=f"hyena_{L}",
    )(hy, hy, hy, conv_w, conv_w, conv_w, conv_b, conv_b, conv_b, h4, hm, hy_bias, *mats)


ROUTER_LANES = LANES
BIG_LANE = 1e9


ROW_GROUP = D // LANES


def _store_row_groups(ref, val):
    n = val.shape[0]
    for s in range(ROW_GROUP):
        ref[pl.ds(s, n, stride=ROW_GROUP), :] = val[:, s * LANES:(s + 1) * LANES]


def _load_row_groups(ref, n, s):
    return ref[pl.ds(s, n, stride=ROW_GROUP), :]


def _first_max_lane(v, lanef):
    m = jnp.max(v, axis=-1, keepdims=True)
    return m, jnp.min(jnp.where(v == m, lanef, BIG_LANE), axis=-1, keepdims=True)


def _out_router_kernel(*refs, n_in, x_is_pair):
    a_refs = refs[:2 * n_in]
    refs = refs[2 * n_in:]
    is_ctx = pl.program_id(0) < CTX_TILES
    if x_is_pair:
        x = jnp.where(is_ctx, refs[0][...], refs[1][...])
        refs = refs[2:]
    else:
        x = refs[0][...]
        refs = refs[1:]
    w_ref, mod_ref, gf_ref, wr_ref, br_ref, xo_ref, h2_ref, ids_ref, wts_ref, cnt_ref = refs
    acc, k0 = None, 0
    for ac_ref, al_ref in zip(a_refs[0::2], a_refs[1::2]):
        kk = ac_ref.shape[1]
        a = jnp.where(is_ctx, ac_ref[...], al_ref[...])
        part = jnp.dot(a, w_ref[k0:k0 + kk, :], preferred_element_type=F32)
        acc = part if acc is None else acc + part
        k0 += kk
    xn = x + mod_ref[0, 2:3, :] * acc
    xo_ref[...] = xn
    h2 = _modulated(xn, gf_ref, mod_ref, 3)
    h2_ref[...] = h2

    h_hi = h2.astype(BF16)
    h_lo = (h2 - h_hi.astype(F32)).astype(BF16)
    logits = (jnp.dot(h_hi, wr_ref[0], preferred_element_type=F32) + jnp.dot(h_lo, wr_ref[0], preferred_element_type=F32)
              + jnp.dot(h_hi, wr_ref[1], preferred_element_type=F32) + br_ref[...])
    lanef = lax.broadcasted_iota(jnp.int32, logits.shape, 1).astype(F32)
    gl = jnp.where(lanef < MOE_G, logits, NEG)
    gm, gi = _first_max_lane(gl, lanef)
    g_w = 1.0 / jnp.sum(jnp.exp(gl - gm), axis=-1, keepdims=True)
    lo = MOE_G + MOE_PG * gi
    el = jnp.where((lanef >= lo) & (lanef < lo + MOE_PG), logits, NEG)
    m1, e1 = _first_max_lane(el, lanef)
    m2, e2 = _first_max_lane(jnp.where(lanef == e1, NEG, el), lanef)
    p2 = jnp.exp(m2 - m1)
    w1 = g_w / (1.0 + p2)
    ids_ref[...] = jnp.where(lanef == 0, e1 - MOE_G, jnp.where(lanef == 1, e2 - MOE_G, 0.0)).astype(jnp.int32)
    wts_ref[...] = jnp.where(lanef == 0, w1, jnp.where(lanef == 1, w1 * p2, 0.0))
    chosen = ((lanef == e1 - MOE_G) | (lanef == e2 - MOE_G)).astype(F32)
    cnt_ref[0] = jnp.sum(chosen, axis=0, keepdims=True).astype(jnp.int32)


def out_proj_router(acts, w_bf, x, mod, gf, wr, br):
    tile = lambda w: pl.BlockSpec((TM, w), lambda i: (i, 0))
    full = lambda arr: pl.BlockSpec(arr.shape, lambda i: (0,) * arr.ndim)
    x_is_pair = isinstance(x, tuple)
    xs = x if x_is_pair else (x,)
    return pl.pallas_call(
        functools.partial(_out_router_kernel, n_in=len(acts), x_is_pair=x_is_pair),
        grid=(N_TILES,),
        in_specs=[s for a in acts for s in _pair_specs(a[0].shape[1])]
                 + (_pair_specs(D) if x_is_pair else [tile(D)])
                 + [full(w_bf), pl.BlockSpec((1, 6, D), lambda i: (_mod_row(i), 0, 0)), full(gf), full(wr), full(br)],
        out_specs=[tile(D), tile(D), tile(ROUTER_LANES), tile(ROUTER_LANES),
                   pl.BlockSpec((1, 1, ROUTER_LANES), lambda i: (i, 0, 0))],
        out_shape=[jax.ShapeDtypeStruct((T_ALL, D), F32), jax.ShapeDtypeStruct((T_ALL, D), F32),
                   jax.ShapeDtypeStruct((T_ALL, ROUTER_LANES), jnp.int32),
                   jax.ShapeDtypeStruct((T_ALL, ROUTER_LANES), F32),
                   jax.ShapeDtypeStruct((N_TILES, 1, ROUTER_LANES), jnp.int32)],
        compiler_params=_cparams(("arbitrary",)),
        name="out_router",
    )(*[part for a in acts for part in a], *xs, w_bf, mod, gf, wr, br)


N_ASSIGN = 2 * T_ALL
MOE_TILES = N_ASSIGN // TM + MOE_E
N_SLOTS = MOE_TILES * TM


def route_tables(cnt3):
    cnt = cnt3[:, 0, :MOE_E]
    total = jnp.sum(cnt, axis=0)
    padded = (total + TM - 1) // TM * TM
    ends = jnp.cumsum(padded)
    gdst = (ends - padded)[None, :] + jnp.cumsum(cnt, axis=0) - cnt
    loc = jnp.cumsum(cnt, axis=1) - cnt
    starts = jnp.arange(MOE_TILES, dtype=jnp.int32) * TM
    tile_expert = jnp.minimum(jnp.sum((ends[None, :] <= starts[:, None]).astype(jnp.int32), axis=1), MOE_E - 1)
    n_used = (ends[-1] // TM).astype(jnp.int32).reshape(1)
    return cnt, loc, gdst, ends, tile_expert, n_used


RUN_BITS = (2 * TM).bit_length()
RUN_SMALL_BITS = 6


def _dispatch_kernel(cnt_s, loc_s, gdst_s, ends_s, h_ref, ids_ref, gcol_ref, xs_ref, dest_ref, srt, zbuf, sem, zsem):
    i = pl.program_id(0)
    slot = i % 2
    n_rows = 2 * TM

    @pl.when(i == 0)
    def _():
        zbuf[...] = jnp.zeros(zbuf.shape, zbuf.dtype)
        n_used = ends_s[MOE_E - 1] // TM
        for phase in ("start", "wait"):
            def tail(t, c, phase=phase):
                dst = pl.multiple_of(t * (TM * ROW_GROUP), TM * ROW_GROUP)
                cp = pltpu.make_async_copy(zbuf, xs_ref.at[pl.ds(dst, TM * ROW_GROUP), :], zsem)
                cp.start() if phase == "start" else cp.wait()
                return c

            lax.fori_loop(n_used, MOE_TILES, tail, 0)
            for e in range(MOE_E):
                end = ends_s[e]
                prev = ends_s[e - 1] if e > 0 else 0

                @pl.when(end > prev)
                def _(end=end, phase=phase):
                    dst = pl.multiple_of((end - TM) * ROW_GROUP, TM * ROW_GROUP)
                    cp = pltpu.make_async_copy(zbuf, xs_ref.at[pl.ds(dst, TM * ROW_GROUP), :], zsem)
                    cp.start() if phase == "start" else cp.wait()

    idt = ids_ref[...].astype(F32).T
    sub = lax.broadcasted_iota(jnp.int32, (LANES, TM), 0).astype(F32)
    m0 = (sub == idt[0:1, :]).astype(F32)
    m1 = (sub == idt[1:2, :]).astype(F32)
    mt = (m0 + m1).astype(BF16)
    tr = lax.broadcasted_iota(jnp.int32, (TM, TM), 0)
    tc = lax.broadcasted_iota(jnp.int32, (TM, TM), 1)
    earlier = jnp.dot(mt, (tr < tc).astype(BF16), preferred_element_type=F32)
    er = lax.broadcasted_iota(jnp.int32, (LANES, LANES), 0)
    ec = lax.broadcasted_iota(jnp.int32, (LANES, LANES), 1)
    below = jnp.dot((ec < er).astype(BF16), mt, preferred_element_type=F32)
    local = jnp.sum(below, axis=1, keepdims=True) + earlier
    glob = gcol_ref[0] + earlier
    pos0 = jnp.sum(m0 * local, axis=0, keepdims=True)
    pos1 = jnp.sum(m1 * local, axis=0, keepdims=True)
    dest_ref[0] = jnp.concatenate([jnp.sum(m0 * glob, axis=0, keepdims=True),
                                   jnp.sum(m1 * glob, axis=0, keepdims=True)], axis=0).astype(jnp.int32)

    srow = lax.broadcasted_iota(jnp.int32, (n_rows, TM), 0).astype(F32)
    perm = jnp.where((srow == pos0) | (srow == pos1), 1.0, 0.0).astype(BF16)
    _store_row_groups(srt.at[slot], jnp.dot(perm, h_ref[...].astype(BF16), preferred_element_type=F32))

    def run_pieces(n, s0, d0, bits):
        for b in bits:
            size = 1 << b
            off = (n >> (b + 1)) << (b + 1)

            @pl.when(((n >> b) & 1) == 1)
            def _(size=size, off=off):
                src = pl.multiple_of((s0 + off) * ROW_GROUP, ROW_GROUP)
                dst = pl.multiple_of((d0 + off) * ROW_GROUP, ROW_GROUP)
                pltpu.make_async_copy(srt.at[slot, pl.ds(src, size * ROW_GROUP), :],
                                      xs_ref.at[pl.ds(dst, size * ROW_GROUP), :], sem.at[slot]).start()

    for e in range(MOE_E):
        n, s0, d0 = cnt_s[0, 0, e], loc_s[0, 0, e], gdst_s[0, 0, e]

        @pl.when(n >= (1 << RUN_SMALL_BITS))
        def _(n=n, s0=s0, d0=d0):
            run_pieces(n, s0, d0, reversed(range(RUN_SMALL_BITS, RUN_BITS)))

        run_pieces(n, s0, d0, reversed(range(RUN_SMALL_BITS)))

    def wait(s):
        pltpu.make_async_copy(srt.at[s], xs_ref.at[pl.ds(0, n_rows * ROW_GROUP), :], sem.at[s]).wait()

    @pl.when(i > 0)
    def _():
        wait(1 - slot)

    @pl.when(i == N_TILES - 1)
    def _():
        wait(slot)


def dispatch_rows(h2, ids, cnt, loc, gdst, ends):
    tab = lambda: pl.BlockSpec((1, 1, MOE_E), lambda i: (i, 0, 0), memory_space=pltpu.SMEM)
    gcol = jnp.pad(gdst.astype(F32), ((0, 0), (0, LANES - MOE_E)))[:, :, None]
    return pl.pallas_call(
        _dispatch_kernel,
        grid=(N_TILES,),
        in_specs=[tab(), tab(), tab(), pl.BlockSpec(memory_space=pltpu.SMEM),
                  pl.BlockSpec((TM, D), lambda i: (i, 0)), pl.BlockSpec((TM, ROUTER_LANES), lambda i: (i, 0)),
                  pl.BlockSpec((1, LANES, 1), lambda i: (i, 0, 0))],
        out_specs=[pl.BlockSpec(memory_space=pl.ANY), pl.BlockSpec((1, 2, TM), lambda i: (i, 0, 0))],
        out_shape=[jax.ShapeDtypeStruct((N_SLOTS * ROW_GROUP, LANES), F32),
                   jax.ShapeDtypeStruct((N_TILES, 2, TM), jnp.int32)],
        scratch_shapes=[pltpu.VMEM((2, 2 * TM * ROW_GROUP, LANES), F32), pltpu.VMEM((TM * ROW_GROUP, LANES), F32),
                        pltpu.SemaphoreType.DMA((2,)), pltpu.SemaphoreType.DMA(())],
        compiler_params=_cparams(("arbitrary",)),
        name="moe_dispatch",
    )(cnt.reshape(N_TILES, 1, MOE_E), loc.reshape(N_TILES, 1, MOE_E), gdst.reshape(N_TILES, 1, MOE_E),
      ends, h2, ids, gcol)


DMA_UNROLL = 8


def _start_group_gather(src_hbm, idx_ref, n, dst_ref, sem):
    def body(j, c):
        for u in range(DMA_UNROLL):
            r = j * DMA_UNROLL + u
            src = pl.multiple_of(idx_ref[0, 0, r] * ROW_GROUP, ROW_GROUP)
            dst = pl.multiple_of(r * ROW_GROUP, ROW_GROUP)
            pltpu.make_async_copy(src_hbm.at[pl.ds(src, ROW_GROUP), :], dst_ref.at[pl.ds(dst, ROW_GROUP), :],
                                  sem).start(priority=u % 2)
        return c

    lax.fori_loop(0, n // DMA_UNROLL, body, 0)


def _wait_group_gather(src_hbm, dst_ref, sem):
    pltpu.make_async_copy(src_hbm.at[pl.ds(0, dst_ref.shape[0]), :], dst_ref, sem).wait()


def _experts_kernel(te_ref, nu_ref, x_ref, wg_ref, wu_ref, wd_ref, o_ref, xcat):
    i = pl.program_id(0)

    @pl.when(i < nu_ref[0])
    def _():
        for s in range(ROW_GROUP):
            xcat[:, s * LANES:(s + 1) * LANES] = _load_row_groups(x_ref, TM, s).astype(BF16)
        x = xcat[...]
        g = jnp.dot(x, wg_ref[0, 0].astype(BF16), preferred_element_type=F32)
        u = jnp.dot(x, wu_ref[0, 0].astype(BF16), preferred_element_type=F32)
        hid = (_silu(g) * u).astype(BF16)
        _store_row_groups(o_ref, jnp.dot(hid, wd_ref[0, 0].astype(BF16), preferred_element_type=F32))

    @pl.when(i >= nu_ref[0])
    def _():
        o_ref[...] = jnp.zeros(o_ref.shape, o_ref.dtype)


def grouped_experts(xs, w_gate, w_up, w_down, tile_expert, n_used, layer):
    wspec = lambda a, b: pl.BlockSpec((1, 1, a, b), lambda i, te, nu: (layer, te[i], 0, 0))
    return pl.pallas_call(
        _experts_kernel,
        grid_spec=pltpu.PrefetchScalarGridSpec(
            num_scalar_prefetch=2,
            grid=(MOE_TILES,),
            in_specs=[pl.BlockSpec((TM * ROW_GROUP, LANES), lambda i, te, nu: (jnp.minimum(i, nu[0] - 1), 0)),
                      wspec(D, MOE_F), wspec(D, MOE_F), wspec(MOE_F, D)],
            out_specs=pl.BlockSpec((TM * ROW_GROUP, LANES), lambda i, te, nu: (i, 0)),
            scratch_shapes=[pltpu.VMEM((TM, D), BF16)],
        ),
        out_shape=jax.ShapeDtypeStruct((N_SLOTS * ROW_GROUP, LANES), F32),
        compiler_params=_cparams(("arbitrary",)),
        name="moe_experts",
    )(tile_expert, n_used, xs, w_gate, w_up, w_down)


def _combine_kernel(cur_ref, nxt_ref, ys_hbm, x_ref, wts_ref, mod_ref, gfin_ref, *rest, final):
    *o_refs, buf, sem = rest
    i = pl.program_id(0)
    slot = i % 2

    @pl.when(i == 0)
    def _():
        _start_group_gather(ys_hbm, cur_ref, 2 * TM, buf.at[0], sem.at[0])

    @pl.when(i + 1 < N_TILES)
    def _():
        _start_group_gather(ys_hbm, nxt_ref, 2 * TM, buf.at[1 - slot], sem.at[1 - slot])

    _wait_group_gather(ys_hbm, buf.at[slot], sem.at[slot])
    w0, w1 = wts_ref[:, 0:1], wts_ref[:, 1:2]

    def finish(o_ref):
        for s in range(ROW_GROUP):
            cols = slice(s * LANES, (s + 1) * LANES)
            y0 = buf[slot, pl.ds(s, TM, stride=ROW_GROUP), :]
            y1 = buf[slot, pl.ds(TM * ROW_GROUP + s, TM, stride=ROW_GROUP), :]
            o_ref[:, cols] = x_ref[:, cols] + mod_ref[0, 5:6, cols] * (w0 * y0 + w1 * y1)
        if final:
            o_ref[...] = _rms(o_ref[...]) * gfin_ref[...]

    if final:
        pl.when(i < CTX_TILES)(lambda: finish(o_refs[0]))
        pl.when(i >= CTX_TILES)(lambda: finish(o_refs[1]))
    else:
        finish(o_refs[0])


def moe_combine(ys, dest, x, wts, mod, gfin, *, final):
    tile = lambda w: pl.BlockSpec((TM, w), lambda i: (i, 0))
    if final:
        out_specs = _pair_specs(D)
        out_shape = [jax.ShapeDtypeStruct((T_CTX, D), F32), jax.ShapeDtypeStruct((T_LAT, D), F32)]
    else:
        out_specs = [tile(D)]
        out_shape = [jax.ShapeDtypeStruct((T_ALL, D), F32)]
    idx = lambda f: pl.BlockSpec((1, 1, 2 * TM), lambda i: (f(i), 0, 0), memory_space=pltpu.SMEM)
    dest3 = dest.reshape(N_TILES, 1, 2 * TM)
    out = pl.pallas_call(
        functools.partial(_combine_kernel, final=final),
        grid=(N_TILES,),
        in_specs=[idx(lambda i: i), idx(lambda i: jnp.minimum(i + 1, N_TILES - 1)),
                  pl.BlockSpec(memory_space=pl.ANY), tile(D), tile(ROUTER_LANES),
                  pl.BlockSpec((1, 6, D), lambda i: (_mod_row(i), 0, 0)),
                  pl.BlockSpec((1, D), lambda i: (0, 0))],
        out_specs=out_specs,
        out_shape=out_shape,
        scratch_shapes=[pltpu.VMEM((2, 2 * TM * ROW_GROUP, LANES), F32), pltpu.SemaphoreType.DMA((2,))],
        compiler_params=_cparams(("arbitrary",)),
        name="moe_combine",
    )(dest3, dest3, ys, x, wts, mod, gfin)
    return out if final else out[0]


ODD_COLS = 2048
ROPE_Q = MLA_H * MLA_ROPE
ROPE_SHIFT = ROPE_F


def rope_tables():
    t = np.arange(L_LAT)
    pos = np.stack([t // GRID_W, t % GRID_W], axis=1).astype(np.float64)
    inv = 10000.0 ** (-np.arange(ROPE_F, dtype=np.float64) / ROPE_F)
    lane = np.arange(ROPE_Q) % MLA_ROPE
    axis = lane // (2 * ROPE_F)
    first = (lane % (2 * ROPE_F)) < ROPE_F
    ang = pos[:, axis] * inv[lane % ROPE_F][None, :]
    cos, sin = np.cos(ang), np.sin(ang)
    tabs = [cos, np.where(first[None, :], -sin, 0.0), np.where(first[None, :], 0.0, sin)]
    ident = [np.ones((1, TM, ROPE_Q)), np.zeros((1, TM, ROPE_Q)), np.zeros((1, TM, ROPE_Q))]
    return [jnp.asarray(np.concatenate([i, tb.reshape(LAT_TILES_PER_SEQ, TM, ROPE_Q)], axis=0).astype(np.float32))
            for i, tb in zip(ident, tabs)]


def _rope(x, c, a, b):
    n = x.shape[1]
    return x * c[:, :n] + pltpu.roll(x, n - ROPE_SHIFT, 1) * a[:, :n] + pltpu.roll(x, ROPE_SHIFT, 1) * b[:, :n]


def _odd_in_kernel(x_ref, mod_ref, g_ref, w_ref, gq_ref, wuq_ref, gkv_ref, wukv_ref, rc_ref, ra_ref, rb_ref,
                   q_ref, kv_ref, qm_ref, ckv_ref, kvu_ref, kr_ref, knew_ref, vnew_ref):
    hb = _modulated(x_ref[...], g_ref, mod_ref, 0).astype(BF16)
    q_ref[...] = jnp.dot(hb, w_ref[:, 0:NA_W], preferred_element_type=F32)
    k = jnp.dot(hb, w_ref[:, NA_W:2 * NA_W], preferred_element_type=F32)
    v = jnp.dot(hb, w_ref[:, 2 * NA_W:3 * NA_W], preferred_element_type=F32)
    kv_ref[:, 0:NA_W] = k.astype(kv_ref.dtype)
    kv_ref[:, NA_W:2 * NA_W] = v.astype(kv_ref.dtype)

    @pl.when(pl.program_id(0) < CTX_TILES)
    def _():
        kt, vt = k.T, v.T
        for h in range(NA_H):
            knew_ref[0, h] = kt[h * NA_D:(h + 1) * NA_D, :]
            vnew_ref[0, h] = vt[h * NA_D:(h + 1) * NA_D, :]

    rest = jnp.dot(hb, w_ref[:, 3 * NA_W:ODD_COLS], preferred_element_type=F32)
    rc, ra, rb = rc_ref[0], ra_ref[0], rb_ref[0]
    qd = (_rms(rest[:, 0:MLA_QR]) * gq_ref[...]).astype(BF16)
    qm = jnp.dot(qd, wuq_ref[...], preferred_element_type=F32)
    qm_ref[:, 0:MLA_H * MLA_NOPE] = qm[:, 0:MLA_H * MLA_NOPE]
    qm_ref[:, MLA_H * MLA_NOPE:] = _rope(qm[:, MLA_H * MLA_NOPE:], rc, ra, rb)
    ckv = _rms(rest[:, MLA_QR:MLA_QR + MLA_KVR]) * gkv_ref[...]
    ckv_ref[...] = ckv
    kvu_ref[...] = jnp.dot(ckv.astype(BF16), wukv_ref[...],
                           preferred_element_type=F32).astype(kvu_ref.dtype)
    kr_ref[...] = _rope(rest[:, MLA_QR + MLA_KVR:], rc, ra, rb)


def odd_in_proj(x, mod, g, w_bf, gq, wuq_bf, gkv, wukv_bf, tabs):
    tile = lambda w: pl.BlockSpec((TM, w), lambda i: (i, 0))
    full = lambda arr: pl.BlockSpec(arr.shape, lambda i: (0,) * arr.ndim)
    tab = pl.BlockSpec((1, TM, ROPE_Q),
                       lambda i: (jnp.where(i < CTX_TILES, 0, 1 + (i - CTX_TILES) % LAT_TILES_PER_SEQ), 0, 0))
    outs = ((NA_W, F32), (2 * NA_W, BF16), (MLA_H * MLA_QK, F32), (MLA_KVR, F32),
            (MLA_H * (MLA_NOPE + MLA_V), BF16), (LANES, F32))
    cache = pl.BlockSpec((1, NA_H, NA_D, L_CTX), lambda i: (jnp.minimum(i, CTX_TILES - 1), 0, 0, 0))
    cache_shape = jax.ShapeDtypeStruct((N_CTX, NA_H, NA_D, L_CTX), F32)
    return pl.pallas_call(
        _odd_in_kernel,
        grid=(N_TILES,),
        in_specs=[tile(D), pl.BlockSpec((1, 6, D), lambda i: (_mod_row(i), 0, 0)), full(g), full(w_bf),
                  full(gq), full(wuq_bf), full(gkv), full(wukv_bf), tab, tab, tab],
        out_specs=[tile(w) for w, _ in outs] + [cache, cache],
        out_shape=[jax.ShapeDtypeStruct((T_ALL, w), dt) for w, dt in outs] + [cache_shape, cache_shape],
        compiler_params=_cparams(("arbitrary",)),
        name="odd_in",
    )(x, mod, g, w_bf, gq, wuq_bf, gkv, wukv_bf, *tabs)


LOG2E = math.log2(math.e)
NA_QSCALE = NA_D ** -0.5 * LOG2E
MLA_QSCALE = MLA_QK ** -0.5 * LOG2E
NT = (((1,), (1,)), ((), ()))


def _softmax_pv(scores, values):
    m = functools.reduce(jnp.maximum, [jnp.max(s, axis=-1, keepdims=True) for s in scores])
    ps = [jnp.exp2(s - m) for s in scores]
    den = functools.reduce(jnp.add, [jnp.sum(p, axis=-1, keepdims=True) for p in ps])
    acc = functools.reduce(jnp.add, [jnp.dot(p.astype(BF16), v, preferred_element_type=F32) for p, v in zip(ps, values)])
    return acc / den


def _pair(ref_or_val, p, base=0):
    return ref_or_val[:, base + p * LANES:base + (p + 1) * LANES]


def _low_half():
    return lax.broadcasted_iota(jnp.int32, (1, LANES), 1) < NA_D


def _rope_key_forms(kr):
    return kr.astype(BF16), pltpu.roll(kr, LANES // 2, 1).astype(BF16)


def _mla_pair(qm, p, sources, lo):
    outs = []
    for e in range(2):
        h = 2 * p + e
        qn = qm[:, h * MLA_NOPE:(h + 1) * MLA_NOPE] * MLA_QSCALE
        qr = qm[:, MLA_H * MLA_NOPE + h * MLA_ROPE:MLA_H * MLA_NOPE + (h + 1) * MLA_ROPE] * MLA_QSCALE
        z = jnp.zeros((qn.shape[0], LANES - MLA_QK), F32)
        qcat = jnp.concatenate([qn, qr, z] if e == 0 else [qr, z, qn], axis=1).astype(BF16)
        scores = []
        for kb, kr_lo, kr_hi, _ in sources:
            kcat = jnp.where(lo, kb, kr_hi) if e == 0 else jnp.where(lo, kr_lo, kb)
            scores.append(lax.dot_general(qcat, kcat, NT, preferred_element_type=F32))
        outs.append(_softmax_pv(scores, [src[3] for src in sources]))
    return jnp.where(lo, outs[0], outs[1])


def _attn_ctx_kernel(q_ref, kv_ref, qm_ref, kvu_ref, kr_ref, ona_ref, omla_ref):
    lo = _low_half()
    for p in range(NA_H // 2):
        qb = _pair(q_ref, p) * NA_QSCALE
        kb = _pair(kv_ref, p).astype(BF16)
        vb = _pair(kv_ref, p, NA_W).astype(BF16)
        outs = []
        for e in range(2):
            q = jnp.where(lo if e == 0 else jnp.logical_not(lo), qb, 0.0).astype(BF16)
            outs.append(_softmax_pv([lax.dot_general(q, kb, NT, preferred_element_type=F32)], [vb]))
        ona_ref[:, p * LANES:(p + 1) * LANES] = jnp.where(lo, outs[0], outs[1]).astype(ona_ref.dtype)
    kr_lo, kr_hi = _rope_key_forms(kr_ref[...])
    for p in range(MLA_H // 2):
        src = (_pair(kvu_ref, p).astype(BF16), kr_lo, kr_hi, _pair(kvu_ref, p, MLA_H * MLA_NOPE).astype(BF16))
        omla_ref[:, p * LANES:(p + 1) * LANES] = _mla_pair(qm_ref, p, [src], lo).astype(omla_ref.dtype)


def attn_context(q, kv, qm, kvu, kr):
    seq = lambda w: pl.BlockSpec((L_CTX, w), lambda b: (b, 0))
    return pl.pallas_call(
        _attn_ctx_kernel,
        grid=(N_CTX,),
        in_specs=[seq(NA_W), seq(2 * NA_W), seq(MLA_H * MLA_QK), seq(MLA_H * (MLA_NOPE + MLA_V)), seq(LANES)],
        out_specs=[seq(NA_W), seq(MLA_H * MLA_V)],
        out_shape=[jax.ShapeDtypeStruct((T_CTX, NA_W), BF16), jax.ShapeDtypeStruct((T_CTX, MLA_H * MLA_V), BF16)],
        compiler_params=_cparams(("arbitrary",)),
        name="attn_ctx",
    )(q, kv, qm, kvu, kr)


N_DR = 2 * NA_WIN_R - 1
GRID_ROWS = L_LAT // GRID_W


def _na_bias_kernel(t_ref, o_ref):
    neg = jnp.full((GRID_W, GRID_W), NEG, F32)
    for r in range(GRID_ROWS):
        r0 = min(max(r - NA_WIN_R // 2, 0), GRID_ROWS - NA_WIN_R)
        for kr in range(GRID_ROWS):
            in_window = r0 <= kr < r0 + NA_WIN_R
            blk = t_ref[0, kr - r + NA_WIN_R - 1] if in_window else neg
            o_ref[0, r * GRID_W:(r + 1) * GRID_W, kr * GRID_W:(kr + 1) * GRID_W] = blk


def neighbourhood_bias(rel_bias):
    c = np.arange(GRID_W)
    c0 = np.clip(c - NA_WIN_C // 2, 0, GRID_W - NA_WIN_C)
    col_ok = (c[None, :] >= c0[:, None]) & (c[None, :] < c0[:, None] + NA_WIN_C)
    dc = np.clip(c[None, :] - c[:, None], -(NA_WIN_C - 1), NA_WIN_C - 1) + NA_WIN_C - 1
    sel_c = (dc[:, :, None] == np.arange(2 * NA_WIN_C - 1)).astype(np.float32)
    t = jnp.einsum("hdj,qcj->hdqc", rel_bias.astype(F32), jnp.asarray(sel_c), precision=HIGHEST)
    t = jnp.where(jnp.asarray(col_ok)[None, None], t * LOG2E, NEG)
    return pl.pallas_call(
        _na_bias_kernel,
        grid=(NA_H,),
        in_specs=[pl.BlockSpec((1, N_DR, GRID_W, GRID_W), lambda h: (h, 0, 0, 0))],
        out_specs=pl.BlockSpec((1, L_LAT, L_LAT), lambda h: (h, 0, 0)),
        out_shape=jax.ShapeDtypeStruct((NA_H, L_LAT, L_LAT), F32),
        compiler_params=_cparams(("arbitrary",)),
        name="na_bias",
    )(t)


def _na_lat_kernel(q_ref, k_ref, v_ref, kc_ref, vc_ref, b_ref, o_ref):
    lo = _low_half()
    for p in range(NA_H // 2):
        qb = _pair(q_ref, p) * NA_QSCALE
        kb = _pair(k_ref, p).astype(BF16)
        vb = _pair(v_ref, p).astype(BF16)
        outs = []
        for e in range(2):
            h = 2 * p + e
            half = slice(e * NA_D, (e + 1) * NA_D)
            q = jnp.where(lo if e == 0 else jnp.logical_not(lo), qb, 0.0).astype(BF16)
            s1 = lax.dot_general(q, kb, NT, preferred_element_type=F32) + b_ref[h]
            s2 = jnp.dot(qb[:, half].astype(BF16), kc_ref[0, 0, h].astype(BF16), preferred_element_type=F32)
            m = jnp.maximum(jnp.max(s1, axis=-1, keepdims=True), jnp.max(s2, axis=-1, keepdims=True))
            p1, p2 = jnp.exp2(s1 - m), jnp.exp2(s2 - m)
            den = jnp.sum(p1, axis=-1, keepdims=True) + jnp.sum(p2, axis=-1, keepdims=True)
            a1 = jnp.dot(p1.astype(BF16), vb, preferred_element_type=F32)
            a2 = lax.dot_general(p2.astype(BF16), vc_ref[0, 0, h].astype(BF16), NT, preferred_element_type=F32)
            outs.append((a1[:, half] + a2) / den)
        o_ref[:, p * LANES:(p + 1) * LANES] = jnp.concatenate(outs, axis=1).astype(o_ref.dtype)


def attn_neighbourhood_latent(q, kv, cache_kt, cache_vt, bias):
    nq = L_LAT // TM
    t0 = T_CTX // TM
    s0 = T_CTX // L_LAT
    cache = pl.BlockSpec((1, 1, NA_H, NA_D, PAST), lambda qt, b: (b, 0, 0, 0, 0))
    return pl.pallas_call(
        _na_lat_kernel,
        grid=(nq, N_LAT),
        in_specs=[pl.BlockSpec((TM, NA_W), lambda qt, b: (t0 + b * nq + qt, 0)),
                  pl.BlockSpec((L_LAT, NA_W), lambda qt, b: (s0 + b, 0)),
                  pl.BlockSpec((L_LAT, NA_W), lambda qt, b: (s0 + b, 1)),
                  cache, cache,
                  pl.BlockSpec((NA_H, TM, L_LAT), lambda qt, b: (0, qt, 0))],
        out_specs=pl.BlockSpec((TM, NA_W), lambda qt, b: (b * nq + qt, 0)),
        out_shape=jax.ShapeDtypeStruct((T_LAT, NA_W), BF16),
        compiler_params=_cparams(("arbitrary", "arbitrary")),
        name="attn_na_lat",
    )(q, kv, kv, cache_kt, cache_vt, bias)


def _mla_lat_kernel(qm_ref, kvu_ref, kr_ref, ckv_ref, krc_ref, wukv_ref, o_ref):
    lo = _low_half()
    kvc = jnp.dot(ckv_ref[0, 0].astype(BF16), wukv_ref[...], preferred_element_type=F32)
    kr_lo, kr_hi = _rope_key_forms(kr_ref[...])
    krc = jnp.concatenate([krc_ref[0, 0], jnp.zeros((PAST, LANES - MLA_ROPE), F32)], axis=1)
    krc_lo, krc_hi = _rope_key_forms(krc)
    vbase = MLA_H * MLA_NOPE
    for p in range(MLA_H // 2):
        lat = (_pair(kvu_ref, p).astype(BF16), kr_lo, kr_hi, _pair(kvu_ref, p, vbase).astype(BF16))
        ctx = (_pair(kvc, p).astype(BF16), krc_lo, krc_hi, _pair(kvc, p, vbase).astype(BF16))
        o_ref[:, p * LANES:(p + 1) * LANES] = _mla_pair(qm_ref, p, [lat, ctx], lo).astype(o_ref.dtype)


def attn_mla_latent(qm, kvu, kr, cache_ckv, cache_krope, wukv_bf):
    nq = L_LAT // TM
    t0 = T_CTX // TM
    s0 = T_CTX // L_LAT
    return pl.pallas_call(
        _mla_lat_kernel,
        grid=(nq, N_LAT),
        in_specs=[pl.BlockSpec((TM, MLA_H * MLA_QK), lambda qt, b: (t0 + b * nq + qt, 0)),
                  pl.BlockSpec((L_LAT, MLA_H * (MLA_NOPE + MLA_V)), lambda qt, b: (s0 + b, 0)),
                  pl.BlockSpec((L_LAT, LANES), lambda qt, b: (s0 + b, 0)),
                  pl.BlockSpec((1, 1, PAST, MLA_KVR), lambda qt, b: (b, 0, 0, 0)),
                  pl.BlockSpec((1, 1, PAST, MLA_ROPE), lambda qt, b: (b, 0, 0, 0)),
                  pl.BlockSpec(wukv_bf.shape, lambda qt, b: (0, 0))],
        out_specs=pl.BlockSpec((TM, MLA_H * MLA_V), lambda qt, b: (b * nq + qt, 0)),
        out_shape=jax.ShapeDtypeStruct((T_LAT, MLA_H * MLA_V), BF16),
        compiler_params=_cparams(("arbitrary", "arbitrary")),
        name="attn_mla_lat",
    )(qm, kvu, kr, cache_ckv, cache_krope, wukv_bf)


def moe_block(h2, ids, wts, cnt3, x, mod, gfin, w_gate, w_up, w_down, layer, *, final):
    cnt, loc, gdst, ends, tile_expert, n_used = route_tables(cnt3)
    xs, dest = dispatch_rows(h2, ids, cnt, loc, gdst, ends)
    ys = grouped_experts(xs, w_gate, w_up, w_down, tile_expert, n_used, layer)
    return moe_combine(ys, dest, x, wts, mod, gfin, final=final)


def _pad_lanes(a):
    return jnp.pad(a, ((0, 0), (0, LANES - a.shape[1])))


def _hyena_features(L):
    t = np.linspace(0.0, 1.0, L)[:, None]
    w = 2.0 * math.pi * np.arange(L) / L
    bands = np.linspace(1e-4, HY_BANDS - 1, HY_BANDS)
    ang = w[:, None] * bands[None]
    feat = np.concatenate([t, np.cos(ang), -np.sin(ang)], axis=-1)
    return jnp.asarray(np.pad(feat, ((0, 0), (0, LANES - HY_FEAT))).astype(np.float32))


def _router_params(w_gr, b_gr, w_er, b_er):
    wr = _pad_lanes(jnp.concatenate([w_gr, w_er], axis=1))
    br = _pad_lanes(jnp.concatenate([b_gr, b_er])[None])
    wr_hi = wr.astype(BF16)
    wr_lo = (wr - wr_hi.astype(F32)).astype(BF16)
    return jnp.stack([wr_hi, wr_lo]), br


def _even_layer(x, mod, g_mix, state, w_in, conv_w, conv_b, a_log, dt_bias, d_skip, g_ssd, hy_conv_w, hy_conv_b,
                hy_w1, hy_b1, hy_w2, hy_b2, hy_w3, hy_freq, hy_bias):
    n0 = D + SSD_XBC
    w_parts = (w_in[:, :n0].astype(BF16), w_in[:, n0 + SSD_H:].astype(BF16),
               _pad_lanes(w_in[:, n0:n0 + SSD_H]).astype(BF16))
    z, xbc, hy, dtr = even_in_proj(x, mod, g_mix, w_parts)
    small = (conv_w, conv_b[None], _pad_lanes(dt_bias), _pad_lanes(a_log), jnp.repeat(d_skip, SSD_P)[None], g_ssd[None])
    y_c, fin = ssd_mixer(xbc, dtr, z, None, *small, L=L_CTX, n_seq=N_CTX, row_off=0)
    (y_l,) = ssd_mixer(xbc, dtr, z, state.reshape(N_LAT, 2, SSD_H * SSD_P, SSD_N), *small,
                       L=L_LAT, n_seq=N_LAT, row_off=T_CTX)
    w1 = jnp.pad(hy_w1, ((0, LANES - HY_FEAT), (0, 0)))
    w3r = hy_w3.reshape(HY_HID, 4, D).transpose(1, 0, 2)
    deltas = jnp.asarray(np.linspace(HY_MIN_DECAY, HY_MAX_DECAY, D).astype(np.float32))[None]
    us = []
    for L, n_seq, off in ((L_CTX, N_CTX, 0), (L_LAT, N_LAT, T_CTX)):
        h4, hm = hyena_filter_spectra(_hyena_features(L), w1, hy_b1[None], hy_w2, hy_b2[None], hy_freq, w3r, deltas, L=L)
        us.append(hyena_mixer(hy, hy_conv_w, hy_conv_b[None], h4, hm, hy_bias, L=L, n_seq=n_seq, row_off=off))
    return (y_c, y_l), tuple(us), fin


def _odd_layer(x, mod, g_mix, cache_k, cache_v, cache_ckv, cache_kr, rel_bias, w_in, g_q, w_uq, g_kv, w_ukv):
    w_bf = jnp.pad(w_in, ((0, 0), (0, ODD_COLS - w_in.shape[1]))).astype(BF16)
    wuq = w_uq.reshape(MLA_QR, MLA_H, MLA_QK)
    wuq_bf = jnp.concatenate([wuq[:, :, :MLA_NOPE].reshape(MLA_QR, -1), wuq[:, :, MLA_NOPE:].reshape(MLA_QR, -1)],
                             axis=1).astype(BF16)
    wukv = w_ukv.reshape(MLA_KVR, MLA_H, MLA_NOPE + MLA_V)
    wukv_bf = jnp.concatenate([wukv[:, :, :MLA_NOPE].reshape(MLA_KVR, -1), wukv[:, :, MLA_NOPE:].reshape(MLA_KVR, -1)],
                              axis=1).astype(BF16)
    q, kv, qm, ckv, kvu, kr, k_new, v_new = odd_in_proj(x, mod, g_mix, w_bf, g_q[None], wuq_bf, g_kv[None], wukv_bf,
                                                        rope_tables())
    ona_c, omla_c = attn_context(q, kv, qm, kvu, kr)
    ona_l = attn_neighbourhood_latent(q, kv, jnp.swapaxes(cache_k, 3, 4), jnp.swapaxes(cache_v, 3, 4),
                                      neighbourhood_bias(rel_bias))
    omla_l = attn_mla_latent(qm, kvu, kr, cache_ckv, cache_kr, wukv_bf)
    return (ona_c, ona_l), (omla_c, omla_l), k_new, v_new, ckv, kr


def kernel(x_prompt, x_sample, state_ssd, cache_na_k, cache_na_v, cache_mla_ckv, cache_mla_krope, c, c_ctx, w_ada, b_ada, norm_mix, norm_ffn, norm_final, ev_w_in, ev_conv_w, ev_conv_b, ssd_A_log, ssd_dt_bias, ssd_d, ssd_norm, hy_conv_w, hy_conv_b, hy_w1, hy_b1, hy_w2, hy_b2, hy_w3, hy_freq, hy_bias, ev_w_out, od_w_in, mla_q_norm, mla_w_uq, mla_kv_norm, mla_w_ukv, na_rel_bias, od_w_out, moe_w_gr, moe_b_gr, moe_w_er, moe_b_er, moe_w_gate, moe_w_up, moe_w_down):
    x = (x_prompt.reshape(T_CTX, D), x_sample.reshape(T_LAT, D))
    cvec = jnp.zeros((MOD_ROWS, D), F32).at[0].set(c_ctx).at[1:1 + N_LAT].set(c)
    mod = ada_modulation(cvec, w_ada, b_ada)
    gfin = norm_final[None]

    y, u, fin = _even_layer(x, mod[0], norm_mix[0][None], state_ssd[:, 0], ev_w_in[0], ev_conv_w[0], ev_conv_b[0],
                            ssd_A_log[0], ssd_dt_bias[0], ssd_d[0], ssd_norm[0], hy_conv_w[0], hy_conv_b[0],
                            hy_w1[0], hy_b1[0], hy_w2[0], hy_b2[0], hy_w3[0], hy_freq[0], hy_bias[0])
    wr, br = _router_params(moe_w_gr[0], moe_b_gr[0], moe_w_er[0], moe_b_er[0])
    xn, h2, ids, wts, cnt3 = out_proj_router([y, u], ev_w_out[0].astype(BF16), x, mod[0], norm_ffn[0][None], wr, br)
    x = moe_block(h2, ids, wts, cnt3, xn, mod[0], gfin, moe_w_gate, moe_w_up, moe_w_down, 0, final=False)

    o_na, o_mla, k_new, v_new, ckv, kr = _odd_layer(x, mod[1], norm_mix[1][None], cache_na_k, cache_na_v, cache_mla_ckv,
                                           cache_mla_krope, na_rel_bias[0], od_w_in[0], mla_q_norm[0], mla_w_uq[0],
                                           mla_kv_norm[0], mla_w_ukv[0])
    wr, br = _router_params(moe_w_gr[1], moe_b_gr[1], moe_w_er[1], moe_b_er[1])
    xn, h2, ids, wts, cnt3 = out_proj_router([o_na, o_mla], od_w_out[0].astype(BF16), x, mod[1], norm_ffn[1][None], wr, br)
    y_c, y_l = moe_block(h2, ids, wts, cnt3, xn, mod[1], gfin, moe_w_gate, moe_w_up, moe_w_down, 1, final=True)

    return (y_c.reshape(N_CTX, L_CTX, D),
            y_l.reshape(N_LAT, L_LAT, D),
            fin.reshape(N_CTX, 1, 2, SSD_H, SSD_P, SSD_N),
            jnp.swapaxes(k_new, 2, 3)[:, None],
            jnp.swapaxes(v_new, 2, 3)[:, None],
            ckv[:T_CTX].reshape(N_CTX, 1, L_CTX, MLA_KVR),
            kr[:T_CTX, :MLA_ROPE].reshape(N_CTX, 1, L_CTX, MLA_ROPE))
```

```python
import functools
import math

import numpy as np
import jax
import jax.numpy as jnp
from jax import lax
from jax.experimental import pallas as pl
from jax.experimental.pallas import tpu as pltpu

F32 = jnp.float32
BF16 = jnp.bfloat16
HIGHEST = lax.Precision.HIGHEST

D = 1024
N_CTX, L_CTX = 16, 256
N_LAT, L_LAT = 8, 1024
T_CTX = N_CTX * L_CTX
T_LAT = N_LAT * L_LAT
T_ALL = T_CTX + T_LAT
PAST = 512
GRID_W = 64
EPS = 1e-6
NEG = -1e30

SSD_H, SSD_P, SSD_N, SSD_G = 16, 64, 128, 2
SSD_XBC = D + 2 * SSD_G * SSD_N
SSD_K = 5
CHUNK = 128

HY_K = 3
HY_BANDS = 16
HY_FEAT = 1 + 2 * HY_BANDS
HY_HID = 64
HY_MIN_DECAY = abs(math.log(1e-2) / 1.5)
HY_MAX_DECAY = abs(math.log(1e-2) / 0.3)

NA_H, NA_D = 8, 64
NA_W = NA_H * NA_D
NA_WIN_R, NA_WIN_C = 8, 16
MLA_H, MLA_QR, MLA_KVR = 8, 256, 128
MLA_NOPE, MLA_ROPE, MLA_V = 64, 32, 64
MLA_QK = MLA_NOPE + MLA_ROPE
ROPE_F = MLA_ROPE // 4

MOE_G, MOE_PG, MOE_E, MOE_F = 4, 8, 32, 256

LANES = 128
SUBLANES = 8
VMEM_LIMIT = 56 * 1024 * 1024

TM = 256
N_TILES = T_ALL // TM
CTX_TILES = T_CTX // TM
LAT_TILES_PER_SEQ = L_LAT // TM
MOD_ROWS = 16


def _cparams(sem):
    return pltpu.CompilerParams(dimension_semantics=sem, vmem_limit_bytes=VMEM_LIMIT)


def _mod_row(i):
    return jnp.where(i < CTX_TILES, 0, 1 + (i - CTX_TILES) // LAT_TILES_PER_SEQ)


def _silu(x):
    return x * jax.nn.sigmoid(x)


def _rms(x):
    return x * lax.rsqrt(jnp.mean(x * x, axis=-1, keepdims=True) + EPS)


def _ada_kernel(c_ref, w_ref, b_ref, o_ref):
    c = c_ref[...]
    o_ref[0] = jnp.dot(_silu(c), w_ref[0], precision=HIGHEST, preferred_element_type=F32) + b_ref[0]


def ada_modulation(cvec, w_ada, b_ada):
    depth = w_ada.shape[0]
    out = pl.pallas_call(
        _ada_kernel,
        grid=(depth, 6),
        in_specs=[
            pl.BlockSpec((MOD_ROWS, D), lambda l, j: (0, 0)),
            pl.BlockSpec((1, D, D), lambda l, j: (l, 0, j)),
            pl.BlockSpec((1, 1, D), lambda l, j: (l, 0, j)),
        ],
        out_specs=pl.BlockSpec((1, MOD_ROWS, D), lambda l, j: (l, 0, j)),
        out_shape=jax.ShapeDtypeStruct((depth, MOD_ROWS, 6 * D), F32),
        compiler_params=_cparams(("arbitrary", "arbitrary")),
        name="ada",
    )(cvec, w_ada, b_ada.reshape(depth, 1, 6 * D))
    return out.reshape(depth, MOD_ROWS, 6, D)


PROJ_CHUNK = 512


def _modulated(x, g_ref, mod_ref, shift_row):
    h = _rms(x) * g_ref[...]
    return h * (1.0 + mod_ref[0, shift_row + 1:shift_row + 2, :]) + mod_ref[0, shift_row:shift_row + 1, :]


IN_TM = 512
IN_CTX_TILES = T_CTX // IN_TM


def _even_in_kernel(xc_ref, xl_ref, mod_ref, g_ref, wa_ref, wh_ref, wd_ref, z_ref, xbc_ref, hy_ref, dt_ref):
    x = jnp.where(pl.program_id(0) < IN_CTX_TILES, xc_ref[...], xl_ref[...])
    hb = _modulated(x, g_ref, mod_ref, 0).astype(BF16)
    for o_ref, w_ref, col in ((z_ref, wa_ref, 0), (xbc_ref, wa_ref, D), (hy_ref, wh_ref, 0), (dt_ref, wd_ref, 0)):
        width = o_ref.shape[1]
        for c0 in range(0, width, PROJ_CHUNK):
            c1 = min(c0 + PROJ_CHUNK, width)
            o_ref[:, c0:c1] = jnp.dot(hb, w_ref[:, col + c0:col + c1], preferred_element_type=F32)


def _pair_specs(width):
    return [pl.BlockSpec((TM, width), lambda i: (jnp.minimum(i, CTX_TILES - 1), 0)),
            pl.BlockSpec((TM, width), lambda i: (jnp.maximum(i - CTX_TILES, 0), 0))]


def even_in_proj(x_pair, mod, g, w_parts):
    widths = (D, SSD_XBC, 3 * D, LANES)
    mod_row = lambda i: jnp.where(i < IN_CTX_TILES, 0, 1 + (i - IN_CTX_TILES) // (L_LAT // IN_TM))
    return pl.pallas_call(
        _even_in_kernel,
        grid=(T_ALL // IN_TM,),
        in_specs=[
            pl.BlockSpec((IN_TM, D), lambda i: (jnp.minimum(i, IN_CTX_TILES - 1), 0)),
            pl.BlockSpec((IN_TM, D), lambda i: (jnp.maximum(i - IN_CTX_TILES, 0), 0)),
            pl.BlockSpec((1, 6, D), lambda i: (mod_row(i), 0, 0)),
            pl.BlockSpec((1, D), lambda i: (0, 0)),
            *[_const_spec(w) for w in w_parts],
        ],
        out_specs=[pl.BlockSpec((IN_TM, w), lambda i: (i, 0)) for w in widths],
        out_shape=[jax.ShapeDtypeStruct((T_ALL, w), F32) for w in widths],
        compiler_params=_cparams(("arbitrary",)),
        name="even_in",
    )(*x_pair, mod, g, *w_parts)


PAD = SUBLANES


def _ssd_kernel(*refs, L, has_init):
    if has_init:
        (xbc_ref, dt_ref, z_ref, init_ref, cw_ref, cb_ref, dtb_ref, alog_ref, dsk_ref, gs_ref,
         y_ref, xp_s, xc_s, ya_s, st_s) = refs
        fin_ref = None
    else:
        (xbc_ref, dt_ref, z_ref, cw_ref, cb_ref, dtb_ref, alog_ref, dsk_ref, gs_ref,
         y_ref, fin_ref, xp_s, xc_s, ya_s, st_s) = refs
        init_ref = None
    nc = L // CHUNK
    half = SSD_K // 2

    xp_s[0:PAD, :] = jnp.zeros((PAD, SSD_XBC), F32)
    xp_s[PAD + L:2 * PAD + L, :] = jnp.zeros((PAD, SSD_XBC), F32)
    xp_s[PAD:PAD + L, :] = xbc_ref[...]
    for c in range(nc):
        base = PAD + c * CHUNK - half
        for j in range(SSD_XBC // LANES):
            cols = slice(j * LANES, (j + 1) * LANES)
            acc = cb_ref[:, cols] + xp_s[base:base + CHUNK, cols] * cw_ref[0:1, cols]
            for k in range(1, SSD_K):
                acc = acc + xp_s[base + k:base + k + CHUNK, cols] * cw_ref[k:k + 1, cols]
            xc_s[c * CHUNK:(c + 1) * CHUNK, cols] = _silu(acc)

    row = lax.broadcasted_iota(jnp.int32, (CHUNK, CHUNK), 0)
    colm = lax.broadcasted_iota(jnp.int32, (CHUNK, CHUNK), 1)
    lane_lo = colm < SSD_P
    tri_lo = (colm <= row).astype(F32)
    tri_up = (colm >= row).astype(F32)

    for d in range(2):
        causal = (colm <= row) if d == 0 else (colm >= row)
        for j in range(SSD_H * SSD_P // CHUNK):
            if has_init:
                st_s[:, j * CHUNK:(j + 1) * CHUNK] = init_ref[0, d, j * CHUNK:(j + 1) * CHUNK, :].T
            else:
                st_s[:, j * CHUNK:(j + 1) * CHUNK] = jnp.zeros((CHUNK, CHUNK), F32)

        def chunk_body(ci, carry, d=d, causal=causal):
            c = ci if d == 0 else nc - 1 - ci
            r0 = pl.multiple_of(c * CHUNK, CHUNK)
            dt = jax.nn.softplus(dt_ref[pl.ds(r0, CHUNK), :] + dtb_ref[d:d + 1, :])
            a = dt * (-jnp.exp(alog_ref[d:d + 1, :]))
            tri = tri_lo if d == 0 else tri_up
            cs = jnp.dot(tri, a, precision=HIGHEST, preferred_element_type=F32)
            cs_t = jnp.dot(a.T, tri.T, precision=HIGHEST, preferred_element_type=F32)
            edge = cs[CHUNK - 1:CHUNK, :] if d == 0 else cs[0:1, :]
            for g in range(SSD_G):
                bm = xc_s[pl.ds(r0, CHUNK), D + g * SSD_N:D + (g + 1) * SSD_N]
                cm = xc_s[pl.ds(r0, CHUNK), D + (SSD_G + g) * SSD_N:D + (SSD_G + g + 1) * SSD_N]
                bm_b, cm_b = bm.astype(BF16), cm.astype(BF16)
                cb = lax.dot_general(cm_b, bm_b, (((1,), (1,)), ((), ())), preferred_element_type=F32)
                bm_t = bm.T.astype(BF16)
                pairs = SSD_H // SSD_G // 2
                for pp in range(pairs):
                    p = g * pairs + pp
                    h0, h1 = 2 * p, 2 * p + 1
                    cols = slice(p * CHUNK, (p + 1) * CHUNK)
                    xs = xc_s[pl.ds(r0, CHUNK), cols]
                    xdt = xs * jnp.where(lane_lo, dt[:, h0:h0 + 1], dt[:, h1:h1 + 1])
                    cs_b = [jnp.broadcast_to(cs[:, h:h + 1], (CHUNK, CHUNK)) for h in (h0, h1)]
                    ms = [cb * jnp.exp(jnp.where(causal, cs_b[e] - cs_t[h:h + 1, :], NEG)) for e, h in enumerate((h0, h1))]
                    cs_p = jnp.where(lane_lo, cs_b[0], cs_b[1])
                    edge_p = jnp.where(lane_lo[0:1, :], edge[:, h0:h0 + 1], edge[:, h1:h1 + 1])
                    mcat = jnp.concatenate(ms, axis=1).astype(BF16)
                    xbd = jnp.concatenate([jnp.where(lane_lo, xdt, 0.0), jnp.where(lane_lo, 0.0, xdt)],
                                          axis=0).astype(BF16)
                    y_diag = jnp.dot(mcat, xbd, preferred_element_type=F32)
                    st = st_s[:, cols]
                    y_off = jnp.dot(cm_b, st.astype(BF16), preferred_element_type=F32)
                    y_off = y_off * jnp.exp(cs_p)
                    y = y_diag + y_off
                    if d == 0:
                        ya_s[pl.ds(r0, CHUNK), cols] = y
                    else:
                        ya_s[pl.ds(r0, CHUNK), cols] = ya_s[pl.ds(r0, CHUNK), cols] + y
                    xdd = (xdt * jnp.exp(edge_p - cs_p)).astype(BF16)
                    snew = jnp.dot(bm_t, xdd, preferred_element_type=F32)
                    st_s[:, cols] = st * jnp.exp(edge_p) + snew
            return carry

        lax.fori_loop(0, nc, chunk_body, 0)
        if fin_ref is not None:
            for j in range(SSD_H * SSD_P // CHUNK):
                fin_ref[0, d, j * CHUNK:(j + 1) * CHUNK, :] = st_s[:, j * CHUNK:(j + 1) * CHUNK].T

    def out_body(c, carry):
        r0 = pl.multiple_of(c * CHUNK, CHUNK)
        y = ya_s[pl.ds(r0, CHUNK), :] + xc_s[pl.ds(r0, CHUNK), 0:D] * dsk_ref[...]
        y = y * _silu(z_ref[pl.ds(r0, CHUNK), :])
        y_ref[pl.ds(r0, CHUNK), :] = (_rms(y) * gs_ref[...]).astype(y_ref.dtype)
        return carry

    lax.fori_loop(0, nc, out_body, 0)


def ssd_mixer(xbc, dtr, z, init, cw, cb, dtb, alog, dsk, gs, *, L, n_seq, row_off):
    blk0 = row_off // L
    has_init = init is not None
    seq = lambda w: pl.BlockSpec((L, w), lambda b: (blk0 + b, 0))
    full = lambda arr: pl.BlockSpec(arr.shape, lambda b: (0,) * arr.ndim)
    in_specs = [seq(SSD_XBC), seq(LANES), seq(D)]
    args = [xbc, dtr, z]
    if has_init:
        in_specs.append(pl.BlockSpec((1, 2, SSD_H * SSD_P, SSD_N), lambda b: (b, 0, 0, 0)))
        args.append(init)
    small = [cw, cb, dtb, alog, dsk, gs]
    in_specs += [full(a) for a in small]
    args += small
    out_specs = [pl.BlockSpec((L, D), lambda b: (b, 0))]
    out_shape = [jax.ShapeDtypeStruct((n_seq * L, D), BF16)]
    if not has_init:
        out_specs.append(pl.BlockSpec((1, 2, SSD_H * SSD_P, SSD_N), lambda b: (b, 0, 0, 0)))
        out_shape.append(jax.ShapeDtypeStruct((n_seq, 2, SSD_H * SSD_P, SSD_N), F32))
    return pl.pallas_call(
        functools.partial(_ssd_kernel, L=L, has_init=has_init),
        grid=(n_seq,),
        in_specs=in_specs,
        out_specs=out_specs,
        out_shape=out_shape,
        scratch_shapes=[
            pltpu.VMEM((L + 2 * PAD, SSD_XBC), F32),
            pltpu.VMEM((L, SSD_XBC), F32),
            pltpu.VMEM((L, D), F32),
            pltpu.VMEM((SSD_N, SSD_H * SSD_P), F32),
        ],
        compiler_params=_cparams(("arbitrary",)),
        name=f"ssd_{L}",
    )(*args)


HY_CB = 256


def filter_dft_matrices(L):
    H = L // 2
    s = np.arange(L, dtype=np.int64)[None, :]
    k = np.arange(H, dtype=np.int64)[:, None]
    ang = lambda kk: ((kk * s) % (2 * L)).astype(np.float64) * (math.pi / L)
    ca, cb = np.cos(ang(k)), np.cos(ang(L - k))
    sa, sb = np.sin(ang(k)), np.sin(ang(L - k))
    cb[0] = np.where(s[0] % 2 == 0, 1.0, -1.0)
    sa[0], sb[0] = 0.0, 0.0
    fm = np.zeros((2 * SUBLANES, L))
    fm[0], fm[1] = np.cos(ang(H))[0], np.sin(ang(H))[0]
    mats = (np.concatenate([ca, cb], axis=0), np.concatenate([sa, sb], axis=0), fm)
    return tuple(jnp.asarray(m.astype(np.float32)).astype(BF16) for m in mats)


def _const_spec(arr):
    return pl.BlockSpec(arr.shape, lambda *_: (0,) * arr.ndim, pipeline_mode=pl.Buffered(1))


def _hy_filter_kernel(feat_ref, w1_ref, b1_ref, w2_ref, b2_ref, fr_ref, w3_ref, dl_ref, fs_ref, fd_ref, fm_ref,
                      h_ref, hm_ref, *, L):
    H = L // 2
    hp = functools.partial(jnp.dot, precision=HIGHEST, preferred_element_type=F32)
    hdn = jnp.sin(fr_ref[0:1, :] * (hp(feat_ref[...], w1_ref[...]) + b1_ref[...]))
    hdn = jnp.sin(fr_ref[1:2, :] * (hp(hdn, w2_ref[...]) + b2_ref[...]))
    rowi = lax.broadcasted_iota(jnp.int32, (L, 1), 0)
    t = rowi.astype(F32) * (1.0 / (L - 1))
    dec = jnp.exp(-t * dl_ref[...])
    first = rowi == 0
    for o in range(2):
        fwd = hp(hdn, w3_ref[2 * o]) * dec
        bwd = jnp.where(first, 0.0, hp(hdn, w3_ref[2 * o + 1]) * dec)
        hs, hd = (fwd + bwd).astype(BF16), (fwd - bwd).astype(BF16)
        ss = jnp.dot(fs_ref[...], hs, preferred_element_type=F32)
        sd = jnp.dot(fd_ref[...], hd, preferred_element_type=F32)
        h_ref[o, 0] = ss[0:H]
        h_ref[o, 1] = sd[0:H]
        h_ref[o, 2] = ss[H:L]
        h_ref[o, 3] = sd[H:L]
        mid_r = jnp.dot(fm_ref[...], hs, preferred_element_type=F32)
        mid_n = jnp.dot(fm_ref[...], hd, preferred_element_type=F32)
        hm_ref[o] = jnp.concatenate([mid_r[0:1], mid_n[1:2], jnp.zeros((SUBLANES - 2, mid_r.shape[1]), F32)], axis=0)


def hyena_filter_spectra(feat, w1, b1, w2, b2, freq, w3r, deltas, *, L):
    full = lambda arr: pl.BlockSpec(arr.shape, lambda j: (0,) * arr.ndim)
    mats = filter_dft_matrices(L)
    return pl.pallas_call(
        functools.partial(_hy_filter_kernel, L=L),
        grid=(D // HY_CB,),
        in_specs=[full(feat), full(w1), full(b1), full(w2), full(b2), full(freq),
                  pl.BlockSpec((4, HY_HID, HY_CB), lambda j: (0, 0, j)),
                  pl.BlockSpec((1, HY_CB), lambda j: (0, j))] + [_const_spec(m) for m in mats],
        out_specs=[pl.BlockSpec((2, 4, L // 2, HY_CB), lambda j: (0, 0, 0, j)),
                   pl.BlockSpec((2, SUBLANES, HY_CB), lambda j: (0, 0, j))],
        out_shape=[jax.ShapeDtypeStruct((2, 4, L // 2, D), F32), jax.ShapeDtypeStruct((2, SUBLANES, D), F32)],
        compiler_params=_cparams(("arbitrary",)),
        name=f"hy_filter_{L}",
    )(feat, w1, b1, w2, b2, freq, w3r, deltas, *mats)


def split_dft_matrices(L):
    H = L // 2
    k = np.arange(H, dtype=np.int64)[:, None]
    m = np.arange(H, dtype=np.int64)[None, :]
    alt = np.where(m % 2 == 0, 1.0, -1.0)
    ang_e = ((k * m) % L).astype(np.float64) * (2 * math.pi / L)
    ang_o = ((k * (2 * m + 1)) % (2 * L)).astype(np.float64) * (math.pi / L)
    ce, se, co, so = np.cos(ang_e), np.sin(ang_e), np.cos(ang_o), np.sin(ang_o)
    se[0], so[0] = alt[0], alt[0]
    w = np.where(k == 0, 1.0, 2.0) / (2 * L)
    fe = np.concatenate([ce, se], axis=0)
    fo = np.concatenate([co, so], axis=0)
    ge = np.concatenate([(ce * w).T, se.T / L], axis=1)
    go = np.concatenate([(co * w).T, so.T / L], axis=1)
    return tuple(jnp.asarray(a.astype(np.float32)).astype(BF16) for a in (fe, fo, ge, go))


def _store_lane_blocks(ref, val):
    for c in range(ref.shape[0]):
        ref[c] = val[:, c * LANES:(c + 1) * LANES]


def _load_parity(ref, parity, n):
    return jnp.concatenate([ref[c, pl.ds(parity, n, stride=2), :] for c in range(ref.shape[0])], axis=1)


def _hyena_kernel(p0_ref, p1_ref, p2_ref, w0_ref, w1_ref, w2_ref, b0_ref, b1_ref, b2_ref, h_ref, hm_ref, hb_ref,
                  fe_ref, fo_ref, ge_ref, go_ref, o_ref, xp_s, u_s, y_s, *, L, cb):
    H = L // 2
    xp_s[0:PAD, :] = jnp.zeros((PAD, cb), F32)
    xp_s[PAD + L:2 * PAD + L, :] = jnp.zeros((PAD, cb), F32)
    first = lax.broadcasted_iota(jnp.int32, (H, 1), 0) == 0

    def conv(p_ref, w_ref, b_ref):
        xp_s[PAD:PAD + L, :] = p_ref[...]
        acc = b_ref[...] + xp_s[PAD - 1:PAD - 1 + L, :] * w_ref[0:1, :]
        for k in range(1, HY_K):
            acc = acc + xp_s[PAD - 1 + k:PAD - 1 + k + L, :] * w_ref[k:k + 1, :]
        return acc

    u = conv(p0_ref, w0_ref, b0_ref)
    for o, (p_ref, w_ref, b_ref) in enumerate(((p1_ref, w1_ref, b1_ref), (p2_ref, w2_ref, b2_ref))):
        _store_lane_blocks(u_s, u)
        se = jnp.dot(fe_ref[...], _load_parity(u_s, 0, H).astype(BF16), preferred_element_type=F32)
        so = jnp.dot(fo_ref[...], _load_parity(u_s, 1, H).astype(BF16), preferred_element_type=F32)
        e, es, od, os_ = se[0:H], se[H:L], so[0:H], so[H:L]
        b0, b1 = e + od, e - od
        b2 = jnp.where(first, es, es + os_)
        b3 = jnp.where(first, os_, os_ - es)
        har, han, hbr, hbn = h_ref[o, 0], h_ref[o, 1], h_ref[o, 2], h_ref[o, 3]
        hmr, hmn = hm_ref[o, 0:1, :], hm_ref[o, 1:2, :]
        y0 = b0 * har - b2 * han
        y1 = b1 * hbr - b3 * hbn
        y2 = b0 * han + b2 * har
        y3 = b1 * hbn + b3 * hbr
        mid_r = b2[0:1] * hmr - b3[0:1] * hmn
        mid_n = b2[0:1] * hmn + b3[0:1] * hmr
        de = jnp.where(first, mid_r, y2 - y3)
        do = jnp.where(first, mid_n, y2 + y3)
        ye = jnp.dot(ge_ref[...], jnp.concatenate([y0 + y1, de], axis=0).astype(BF16), preferred_element_type=F32)
        yo = jnp.dot(go_ref[...], jnp.concatenate([y0 - y1, do], axis=0).astype(BF16), preferred_element_type=F32)
        for c in range(cb // LANES):
            y_s[c, pl.ds(0, H, stride=2), :] = ye[:, c * LANES:(c + 1) * LANES]
            y_s[c, pl.ds(1, H, stride=2), :] = yo[:, c * LANES:(c + 1) * LANES]
        y = jnp.concatenate([y_s[c] for c in range(cb // LANES)], axis=1)
        u = conv(p_ref, w_ref, b_ref) * (y + u * hb_ref[o:o + 1, :])
    o_ref[...] = u.astype(o_ref.dtype)


def hyena_mixer(hy, conv_w, conv_b, h4, hm, hy_bias, *, L, n_seq, row_off):
    blk0 = row_off // L
    cb = min(D, HY_CB * (L_LAT // L))
    nj = D // cb
    H = L // 2
    part = lambda q: pl.BlockSpec((L, cb), lambda j, b: (blk0 + b, q * nj + j))
    wpart = lambda q: pl.BlockSpec((HY_K, cb), lambda j, b: (0, q * nj + j))
    bpart = lambda q: pl.BlockSpec((1, cb), lambda j, b: (0, q * nj + j))
    mats = split_dft_matrices(L)
    return pl.pallas_call(
        functools.partial(_hyena_kernel, L=L, cb=cb),
        grid=(nj, n_seq),
        in_specs=[part(0), part(1), part(2), wpart(0), wpart(1), wpart(2), bpart(0), bpart(1), bpart(2),
                  pl.BlockSpec((2, 4, H, cb), lambda j, b: (0, 0, 0, j)),
                  pl.BlockSpec((2, SUBLANES, cb), lambda j, b: (0, 0, j)),
                  pl.BlockSpec((2, cb), lambda j, b: (0, j))]
                 + [_const_spec(m) for m in mats],
        out_specs=pl.BlockSpec((L, cb), lambda j, b: (b, j)),
        out_shape=jax.ShapeDtypeStruct((n_seq * L, D), BF16),
        scratch_shapes=[pltpu.VMEM((L + 2 * PAD, cb), F32), pltpu.VMEM((cb // LANES, L, LANES), F32),
                        pltpu.VMEM((cb // LANES, L, LANES), F32)],
        compiler_params=_cparams(("arbitrary", "arbitrary")),
        name=f"hyena_{L}",
    )(hy, hy, hy, conv_w, conv_w, conv_w, conv_b, conv_b, conv_b, h4, hm, hy_bias, *mats)


ROUTER_LANES = LANES
BIG_LANE = 1e9


ROW_GROUP = D // LANES


def _store_row_groups(ref, val):
    n = val.shape[0]
    for s in range(ROW_GROUP):
        ref[pl.ds(s, n, stride=ROW_GROUP), :] = val[:, s * LANES:(s + 1) * LANES]


def _load_row_groups(ref, n, s):
    return ref[pl.ds(s, n, stride=ROW_GROUP), :]


def _first_max_lane(v, lanef):
    m = jnp.max(v, axis=-1, keepdims=True)
    return m, jnp.min(jnp.where(v == m, lanef, BIG_LANE), axis=-1, keepdims=True)


def _out_router_kernel(*refs, n_in, x_is_pair):
    a_refs = refs[:2 * n_in]
    refs = refs[2 * n_in:]
    is_ctx = pl.program_id(0) < CTX_TILES
    if x_is_pair:
        x = jnp.where(is_ctx, refs[0][...], refs[1][...])
        refs = refs[2:]
    else:
        x = refs[0][...]
        refs = refs[1:]
    w_ref, mod_ref, gf_ref, wr_ref, br_ref, xo_ref, h2_ref, ids_ref, wts_ref, cnt_ref = refs
    acc, k0 = None, 0
    for ac_ref, al_ref in zip(a_refs[0::2], a_refs[1::2]):
        kk = ac_ref.shape[1]
        a = jnp.where(is_ctx, ac_ref[...], al_ref[...])
        part = jnp.dot(a, w_ref[k0:k0 + kk, :], preferred_element_type=F32)
        acc = part if acc is None else acc + part
        k0 += kk
    xn = x + mod_ref[0, 2:3, :] * acc
    xo_ref[...] = xn
    h2 = _modulated(xn, gf_ref, mod_ref, 3)
    h2_ref[...] = h2

    h_hi = h2.astype(BF16)
    h_lo = (h2 - h_hi.astype(F32)).astype(BF16)
    logits = (jnp.dot(h_hi, wr_ref[0], preferred_element_type=F32) + jnp.dot(h_lo, wr_ref[0], preferred_element_type=F32)
              + jnp.dot(h_hi, wr_ref[1], preferred_element_type=F32) + br_ref[...])
    lanef = lax.broadcasted_iota(jnp.int32, logits.shape, 1).astype(F32)
    gl = jnp.where(lanef < MOE_G, logits, NEG)
    gm, gi = _first_max_lane(gl, lanef)
    g_w = 1.0 / jnp.sum(jnp.exp(gl - gm), axis=-1, keepdims=True)
    lo = MOE_G + MOE_PG * gi
    el = jnp.where((lanef >= lo) & (lanef < lo + MOE_PG), logits, NEG)
    m1, e1 = _first_max_lane(el, lanef)
    m2, e2 = _first_max_lane(jnp.where(lanef == e1, NEG, el), lanef)
    p2 = jnp.exp(m2 - m1)
    w1 = g_w / (1.0 + p2)
    ids_ref[...] = jnp.where(lanef == 0, e1 - MOE_G, jnp.where(lanef == 1, e2 - MOE_G, 0.0)).astype(jnp.int32)
    wts_ref[...] = jnp.where(lanef == 0, w1, jnp.where(lanef == 1, w1 * p2, 0.0))
    chosen = ((lanef == e1 - MOE_G) | (lanef == e2 - MOE_G)).astype(F32)
    cnt_ref[0] = jnp.sum(chosen, axis=0, keepdims=True).astype(jnp.int32)


def out_proj_router(acts, w_bf, x, mod, gf, wr, br):
    tile = lambda w: pl.BlockSpec((TM, w), lambda i: (i, 0))
    full = lambda arr: pl.BlockSpec(arr.shape, lambda i: (0,) * arr.ndim)
    x_is_pair = isinstance(x, tuple)
    xs = x if x_is_pair else (x,)
    return pl.pallas_call(
        functools.partial(_out_router_kernel, n_in=len(acts), x_is_pair=x_is_pair),
        grid=(N_TILES,),
        in_specs=[s for a in acts for s in _pair_specs(a[0].shape[1])]
                 + (_pair_specs(D) if x_is_pair else [tile(D)])
                 + [full(w_bf), pl.BlockSpec((1, 6, D), lambda i: (_mod_row(i), 0, 0)), full(gf), full(wr), full(br)],
        out_specs=[tile(D), tile(D), tile(ROUTER_LANES), tile(ROUTER_LANES),
                   pl.BlockSpec((1, 1, ROUTER_LANES), lambda i: (i, 0, 0))],
        out_shape=[jax.ShapeDtypeStruct((T_ALL, D), F32), jax.ShapeDtypeStruct((T_ALL, D), F32),
                   jax.ShapeDtypeStruct((T_ALL, ROUTER_LANES), jnp.int32),
                   jax.ShapeDtypeStruct((T_ALL, ROUTER_LANES), F32),
                   jax.ShapeDtypeStruct((N_TILES, 1, ROUTER_LANES), jnp.int32)],
        compiler_params=_cparams(("arbitrary",)),
        name="out_router",
    )(*[part for a in acts for part in a], *xs, w_bf, mod, gf, wr, br)


N_ASSIGN = 2 * T_ALL
MOE_TILES = N_ASSIGN // TM + MOE_E
N_SLOTS = MOE_TILES * TM


def route_tables(cnt3):
    cnt = cnt3[:, 0, :MOE_E]
    total = jnp.sum(cnt, axis=0)
    padded = (total + TM - 1) // TM * TM
    ends = jnp.cumsum(padded)
    gdst = (ends - padded)[None, :] + jnp.cumsum(cnt, axis=0) - cnt
    loc = jnp.cumsum(cnt, axis=1) - cnt
    starts = jnp.arange(MOE_TILES, dtype=jnp.int32) * TM
    tile_expert = jnp.minimum(jnp.sum((ends[None, :] <= starts[:, None]).astype(jnp.int32), axis=1), MOE_E - 1)
    n_used = (ends[-1] // TM).astype(jnp.int32).reshape(1)
    return cnt, loc, gdst, ends, tile_expert, n_used


RUN_BITS = (2 * TM).bit_length()
RUN_SMALL_BITS = 6


def _dispatch_kernel(cnt_s, loc_s, gdst_s, ends_s, h_ref, ids_ref, gcol_ref, xs_ref, dest_ref, srt, zbuf, sem, zsem):
    i = pl.program_id(0)
    slot = i % 2
    n_rows = 2 * TM

    @pl.when(i == 0)
    def _():
        zbuf[...] = jnp.zeros(zbuf.shape, zbuf.dtype)
        n_used = ends_s[MOE_E - 1] // TM
        for phase in ("start", "wait"):
            def tail(t, c, phase=phase):
                dst = pl.multiple_of(t * (TM * ROW_GROUP), TM * ROW_GROUP)
                cp = pltpu.make_async_copy(zbuf, xs_ref.at[pl.ds(dst, TM * ROW_GROUP), :], zsem)
                cp.start() if phase == "start" else cp.wait()
                return c

            lax.fori_loop(n_used, MOE_TILES, tail, 0)
            for e in range(MOE_E):
                end = ends_s[e]
                prev = ends_s[e - 1] if e > 0 else 0

                @pl.when(end > prev)
                def _(end=end, phase=phase):
                    dst = pl.multiple_of((end - TM) * ROW_GROUP, TM * ROW_GROUP)
                    cp = pltpu.make_async_copy(zbuf, xs_ref.at[pl.ds(dst, TM * ROW_GROUP), :], zsem)
                    cp.start() if phase == "start" else cp.wait()

    idt = ids_ref[...].astype(F32).T
    sub = lax.broadcasted_iota(jnp.int32, (LANES, TM), 0).astype(F32)
    m0 = (sub == idt[0:1, :]).astype(F32)
    m1 = (sub == idt[1:2, :]).astype(F32)
    mt = (m0 + m1).astype(BF16)
    tr = lax.broadcasted_iota(jnp.int32, (TM, TM), 0)
    tc = lax.broadcasted_iota(jnp.int32, (TM, TM), 1)
    earlier = jnp.dot(mt, (tr < tc).astype(BF16), preferred_element_type=F32)
    er = lax.broadcasted_iota(jnp.int32, (LANES, LANES), 0)
    ec = lax.broadcasted_iota(jnp.int32, (LANES, LANES), 1)
    below = jnp.dot((ec < er).astype(BF16), mt, preferred_element_type=F32)
    local = jnp.sum(below, axis=1, keepdims=True) + earlier
    glob = gcol_ref[0] + earlier
    pos0 = jnp.sum(m0 * local, axis=0, keepdims=True)
    pos1 = jnp.sum(m1 * local, axis=0, keepdims=True)
    dest_ref[0] = jnp.concatenate([jnp.sum(m0 * glob, axis=0, keepdims=True),
                                   jnp.sum(m1 * glob, axis=0, keepdims=True)], axis=0).astype(jnp.int32)

    srow = lax.broadcasted_iota(jnp.int32, (n_rows, TM), 0).astype(F32)
    perm = jnp.where((srow == pos0) | (srow == pos1), 1.0, 0.0).astype(BF16)
    _store_row_groups(srt.at[slot], jnp.dot(perm, h_ref[...].astype(BF16), preferred_element_type=F32))

    def run_pieces(n, s0, d0, bits):
        for b in bits:
            size = 1 << b
            off = (n >> (b + 1)) << (b + 1)

            @pl.when(((n >> b) & 1) == 1)
            def _(size=size, off=off):
                src = pl.multiple_of((s0 + off) * ROW_GROUP, ROW_GROUP)
                dst = pl.multiple_of((d0 + off) * ROW_GROUP, ROW_GROUP)
                pltpu.make_async_copy(srt.at[slot, pl.ds(src, size * ROW_GROUP), :],
                                      xs_ref.at[pl.ds(dst, size * ROW_GROUP), :], sem.at[slot]).start()

    for e in range(MOE_E):
        n, s0, d0 = cnt_s[0, 0, e], loc_s[0, 0, e], gdst_s[0, 0, e]

        @pl.when(n >= (1 << RUN_SMALL_BITS))
        def _(n=n, s0=s0, d0=d0):
            run_pieces(n, s0, d0, reversed(range(RUN_SMALL_BITS, RUN_BITS)))

        run_pieces(n, s0, d0, reversed(range(RUN_SMALL_BITS)))

    def wait(s):
        pltpu.make_async_copy(srt.at[s], xs_ref.at[pl.ds(0, n_rows * ROW_GROUP), :], sem.at[s]).wait()

    @pl.when(i > 0)
    def _():
        wait(1 - slot)

    @pl.when(i == N_TILES - 1)
    def _():
        wait(slot)


def dispatch_rows(h2, ids, cnt, loc, gdst, ends):
    tab = lambda: pl.BlockSpec((1, 1, MOE_E), lambda i: (i, 0, 0), memory_space=pltpu.SMEM)
    gcol = jnp.pad(gdst.astype(F32), ((0, 0), (0, LANES - MOE_E)))[:, :, None]
    return pl.pallas_call(
        _dispatch_kernel,
        grid=(N_TILES,),
        in_specs=[tab(), tab(), tab(), pl.BlockSpec(memory_space=pltpu.SMEM),
                  pl.BlockSpec((TM, D), lambda i: (i, 0)), pl.BlockSpec((TM, ROUTER_LANES), lambda i: (i, 0)),
                  pl.BlockSpec((1, LANES, 1), lambda i: (i, 0, 0))],
        out_specs=[pl.BlockSpec(memory_space=pl.ANY), pl.BlockSpec((1, 2, TM), lambda i: (i, 0, 0))],
        out_shape=[jax.ShapeDtypeStruct((N_SLOTS * ROW_GROUP, LANES), F32),
                   jax.ShapeDtypeStruct((N_TILES, 2, TM), jnp.int32)],
        scratch_shapes=[pltpu.VMEM((2, 2 * TM * ROW_GROUP, LANES), F32), pltpu.VMEM((TM * ROW_GROUP, LANES), F32),
                        pltpu.SemaphoreType.DMA((2,)), pltpu.SemaphoreType.DMA(())],
        compiler_params=_cparams(("arbitrary",)),
        name="moe_dispatch",
    )(cnt.reshape(N_TILES, 1, MOE_E), loc.reshape(N_TILES, 1, MOE_E), gdst.reshape(N_TILES, 1, MOE_E),
      ends, h2, ids, gcol)


DMA_UNROLL = 8


def _start_group_gather(src_hbm, idx_ref, n, dst_ref, sem):
    def body(j, c):
        for u in range(DMA_UNROLL):
            r = j * DMA_UNROLL + u
            src = pl.multiple_of(idx_ref[0, 0, r] * ROW_GROUP, ROW_GROUP)
            dst = pl.multiple_of(r * ROW_GROUP, ROW_GROUP)
            pltpu.make_async_copy(src_hbm.at[pl.ds(src, ROW_GROUP), :], dst_ref.at[pl.ds(dst, ROW_GROUP), :],
                                  sem).start(priority=u % 2)
        return c

    lax.fori_loop(0, n // DMA_UNROLL, body, 0)


def _wait_group_gather(src_hbm, dst_ref, sem):
    pltpu.make_async_copy(src_hbm.at[pl.ds(0, dst_ref.shape[0]), :], dst_ref, sem).wait()


def _experts_kernel(te_ref, nu_ref, x_ref, wg_ref, wu_ref, wd_ref, o_ref, xcat):
    i = pl.program_id(0)

    @pl.when(i < nu_ref[0])
    def _():
        for s in range(ROW_GROUP):
            xcat[:, s * LANES:(s + 1) * LANES] = _load_row_groups(x_ref, TM, s).astype(BF16)
        x = xcat[...]
        g = jnp.dot(x, wg_ref[0, 0].astype(BF16), preferred_element_type=F32)
        u = jnp.dot(x, wu_ref[0, 0].astype(BF16), preferred_element_type=F32)
        hid = (_silu(g) * u).astype(BF16)
        _store_row_groups(o_ref, jnp.dot(hid, wd_ref[0, 0].astype(BF16), preferred_element_type=F32))

    @pl.when(i >= nu_ref[0])
    def _():
        o_ref[...] = jnp.zeros(o_ref.shape, o_ref.dtype)


def grouped_experts(xs, w_gate, w_up, w_down, tile_expert, n_used, layer):
    wspec = lambda a, b: pl.BlockSpec((1, 1, a, b), lambda i, te, nu: (layer, te[i], 0, 0))
    return pl.pallas_call(
        _experts_kernel,
        grid_spec=pltpu.PrefetchScalarGridSpec(
            num_scalar_prefetch=2,
            grid=(MOE_TILES,),
            in_specs=[pl.BlockSpec((TM * ROW_GROUP, LANES), lambda i, te, nu: (jnp.minimum(i, nu[0] - 1), 0)),
                      wspec(D, MOE_F), wspec(D, MOE_F), wspec(MOE_F, D)],
            out_specs=pl.BlockSpec((TM * ROW_GROUP, LANES), lambda i, te, nu: (i, 0)),
            scratch_shapes=[pltpu.VMEM((TM, D), BF16)],
        ),
        out_shape=jax.ShapeDtypeStruct((N_SLOTS * ROW_GROUP, LANES), F32),
        compiler_params=_cparams(("arbitrary",)),
        name="moe_experts",
    )(tile_expert, n_used, xs, w_gate, w_up, w_down)


def _combine_kernel(cur_ref, nxt_ref, ys_hbm, x_ref, wts_ref, mod_ref, gfin_ref, *rest, final):
    *o_refs, buf, sem = rest
    i = pl.program_id(0)
    slot = i % 2

    @pl.when(i == 0)
    def _():
        _start_group_gather(ys_hbm, cur_ref, 2 * TM, buf.at[0], sem.at[0])

    @pl.when(i + 1 < N_TILES)
    def _():
        _start_group_gather(ys_hbm, nxt_ref, 2 * TM, buf.at[1 - slot], sem.at[1 - slot])

    _wait_group_gather(ys_hbm, buf.at[slot], sem.at[slot])
    w0, w1 = wts_ref[:, 0:1], wts_ref[:, 1:2]

    def finish(o_ref):
        for s in range(ROW_GROUP):
            cols = slice(s * LANES, (s + 1) * LANES)
            y0 = buf[slot, pl.ds(s, TM, stride=ROW_GROUP), :]
            y1 = buf[slot, pl.ds(TM * ROW_GROUP + s, TM, stride=ROW_GROUP), :]
            o_ref[:, cols] = x_ref[:, cols] + mod_ref[0, 5:6, cols] * (w0 * y0 + w1 * y1)
        if final:
            o_ref[...] = _rms(o_ref[...]) * gfin_ref[...]

    if final:
        pl.when(i < CTX_TILES)(lambda: finish(o_refs[0]))
        pl.when(i >= CTX_TILES)(lambda: finish(o_refs[1]))
    else:
        finish(o_refs[0])


def moe_combine(ys, dest, x, wts, mod, gfin, *, final):
    tile = lambda w: pl.BlockSpec((TM, w), lambda i: (i, 0))
    if final:
        out_specs = _pair_specs(D)
        out_shape = [jax.ShapeDtypeStruct((T_CTX, D), F32), jax.ShapeDtypeStruct((T_LAT, D), F32)]
    else:
        out_specs = [tile(D)]
        out_shape = [jax.ShapeDtypeStruct((T_ALL, D), F32)]
    idx = lambda f: pl.BlockSpec((1, 1, 2 * TM), lambda i: (f(i), 0, 0), memory_space=pltpu.SMEM)
    dest3 = dest.reshape(N_TILES, 1, 2 * TM)
    out = pl.pallas_call(
        functools.partial(_combine_kernel, final=final),
        grid=(N_TILES,),
        in_specs=[idx(lambda i: i), idx(lambda i: jnp.minimum(i + 1, N_TILES - 1)),
                  pl.BlockSpec(memory_space=pl.ANY), tile(D), tile(ROUTER_LANES),
                  pl.BlockSpec((1, 6, D), lambda i: (_mod_row(i), 0, 0)),
                  pl.BlockSpec((1, D), lambda i: (0, 0))],
        out_specs=out_specs,
        out_shape=out_shape,
        scratch_shapes=[pltpu.VMEM((2, 2 * TM * ROW_GROUP, LANES), F32), pltpu.SemaphoreType.DMA((2,))],
        compiler_params=_cparams(("arbitrary",)),
        name="moe_combine",
    )(dest3, dest3, ys, x, wts, mod, gfin)
    return out if final else out[0]


ODD_COLS = 2048
ROPE_Q = MLA_H * MLA_ROPE
ROPE_SHIFT = ROPE_F


def rope_tables():
    t = np.arange(L_LAT)
    pos = np.stack([t // GRID_W, t % GRID_W], axis=1).astype(np.float64)
    inv = 10000.0 ** (-np.arange(ROPE_F, dtype=np.float64) / ROPE_F)
    lane = np.arange(ROPE_Q) % MLA_ROPE
    axis = lane // (2 * ROPE_F)
    first = (lane % (2 * ROPE_F)) < ROPE_F
    ang = pos[:, axis] * inv[lane % ROPE_F][None, :]
    cos, sin = np.cos(ang), np.sin(ang)
    tabs = [cos, np.where(first[None, :], -sin, 0.0), np.where(first[None, :], 0.0, sin)]
    ident = [np.ones((1, TM, ROPE_Q)), np.zeros((1, TM, ROPE_Q)), np.zeros((1, TM, ROPE_Q))]
    return [jnp.asarray(np.concatenate([i, tb.reshape(LAT_TILES_PER_SEQ, TM, ROPE_Q)], axis=0).astype(np.float32))
            for i, tb in zip(ident, tabs)]


def _rope(x, c, a, b):
    n = x.shape[1]
    return x * c[:, :n] + pltpu.roll(x, n - ROPE_SHIFT, 1) * a[:, :n] + pltpu.roll(x, ROPE_SHIFT, 1) * b[:, :n]


def _odd_in_kernel(x_ref, mod_ref, g_ref, w_ref, gq_ref, wuq_ref, gkv_ref, wukv_ref, rc_ref, ra_ref, rb_ref,
                   q_ref, kv_ref, qm_ref, ckv_ref, kvu_ref, kr_ref, knew_ref, vnew_ref):
    hb = _modulated(x_ref[...], g_ref, mod_ref, 0).astype(BF16)
    q_ref[...] = jnp.dot(hb, w_ref[:, 0:NA_W], preferred_element_type=F32)
    k = jnp.dot(hb, w_ref[:, NA_W:2 * NA_W], preferred_element_type=F32)
    v = jnp.dot(hb, w_ref[:, 2 * NA_W:3 * NA_W], preferred_element_type=F32)
    kv_ref[:, 0:NA_W] = k.astype(kv_ref.dtype)
    kv_ref[:, NA_W:2 * NA_W] = v.astype(kv_ref.dtype)

    @pl.when(pl.program_id(0) < CTX_TILES)
    def _():
        kt, vt = k.T, v.T
        for h in range(NA_H):
            knew_ref[0, h] = kt[h * NA_D:(h + 1) * NA_D, :]
            vnew_ref[0, h] = vt[h * NA_D:(h + 1) * NA_D, :]

    rest = jnp.dot(hb, w_ref[:, 3 * NA_W:ODD_COLS], preferred_element_type=F32)
    rc, ra, rb = rc_ref[0], ra_ref[0], rb_ref[0]
    qd = (_rms(rest[:, 0:MLA_QR]) * gq_ref[...]).astype(BF16)
    qm = jnp.dot(qd, wuq_ref[...], preferred_element_type=F32)
    qm_ref[:, 0:MLA_H * MLA_NOPE] = qm[:, 0:MLA_H * MLA_NOPE]
    qm_ref[:, MLA_H * MLA_NOPE:] = _rope(qm[:, MLA_H * MLA_NOPE:], rc, ra, rb)
    ckv = _rms(rest[:, MLA_QR:MLA_QR + MLA_KVR]) * gkv_ref[...]
    ckv_ref[...] = ckv
    kvu_ref[...] = jnp.dot(ckv.astype(BF16), wukv_ref[...],
                           preferred_element_type=F32).astype(kvu_ref.dtype)
    kr_ref[...] = _rope(rest[:, MLA_QR + MLA_KVR:], rc, ra, rb)


def odd_in_proj(x, mod, g, w_bf, gq, wuq_bf, gkv, wukv_bf, tabs):
    tile = lambda w: pl.BlockSpec((TM, w), lambda i: (i, 0))
    full = lambda arr: pl.BlockSpec(arr.shape, lambda i: (0,) * arr.ndim)
    tab = pl.BlockSpec((1, TM, ROPE_Q),
                       lambda i: (jnp.where(i < CTX_TILES, 0, 1 + (i - CTX_TILES) % LAT_TILES_PER_SEQ), 0, 0))
    outs = ((NA_W, F32), (2 * NA_W, BF16), (MLA_H * MLA_QK, F32), (MLA_KVR, F32),
            (MLA_H * (MLA_NOPE + MLA_V), BF16), (LANES, F32))
    cache = pl.BlockSpec((1, NA_H, NA_D, L_CTX), lambda i: (jnp.minimum(i, CTX_TILES - 1), 0, 0, 0))
    cache_shape = jax.ShapeDtypeStruct((N_CTX, NA_H, NA_D, L_CTX), F32)
    return pl.pallas_call(
        _odd_in_kernel,
        grid=(N_TILES,),
        in_specs=[tile(D), pl.BlockSpec((1, 6, D), lambda i: (_mod_row(i), 0, 0)), full(g), full(w_bf),
                  full(gq), full(wuq_bf), full(gkv), full(wukv_bf), tab, tab, tab],
        out_specs=[tile(w) for w, _ in outs] + [cache, cache],
        out_shape=[jax.ShapeDtypeStruct((T_ALL, w), dt) for w, dt in outs] + [cache_shape, cache_shape],
        compiler_params=_cparams(("arbitrary",)),
        name="odd_in",
    )(x, mod, g, w_bf, gq, wuq_bf, gkv, wukv_bf, *tabs)


LOG2E = math.log2(math.e)
NA_QSCALE = NA_D ** -0.5 * LOG2E
MLA_QSCALE = MLA_QK ** -0.5 * LOG2E
NT = (((1,), (1,)), ((), ()))


def _softmax_pv(scores, values):
    m = functools.reduce(jnp.maximum, [jnp.max(s, axis=-1, keepdims=True) for s in scores])
    ps = [jnp.exp2(s - m) for s in scores]
    den = functools.reduce(jnp.add, [jnp.sum(p, axis=-1, keepdims=True) for p in ps])
    acc = functools.reduce(jnp.add, [jnp.dot(p.astype(BF16), v, preferred_element_type=F32) for p, v in zip(ps, values)])
    return acc / den


def _pair(ref_or_val, p, base=0):
    return ref_or_val[:, base + p * LANES:base + (p + 1) * LANES]


def _low_half():
    return lax.broadcasted_iota(jnp.int32, (1, LANES), 1) < NA_D


def _rope_key_forms(kr):
    return kr.astype(BF16), pltpu.roll(kr, LANES // 2, 1).astype(BF16)


def _mla_pair(qm, p, sources, lo):
    outs = []
    for e in range(2):
        h = 2 * p + e
        qn = qm[:, h * MLA_NOPE:(h + 1) * MLA_NOPE] * MLA_QSCALE
        qr = qm[:, MLA_H * MLA_NOPE + h * MLA_ROPE:MLA_H * MLA_NOPE + (h + 1) * MLA_ROPE] * MLA_QSCALE
        z = jnp.zeros((qn.shape[0], LANES - MLA_QK), F32)
        qcat = jnp.concatenate([qn, qr, z] if e == 0 else [qr, z, qn], axis=1).astype(BF16)
        scores = []
        for kb, kr_lo, kr_hi, _ in sources:
            kcat = jnp.where(lo, kb, kr_hi) if e == 0 else jnp.where(lo, kr_lo, kb)
            scores.append(lax.dot_general(qcat, kcat, NT, preferred_element_type=F32))
        outs.append(_softmax_pv(scores, [src[3] for src in sources]))
    return jnp.where(lo, outs[0], outs[1])


def _attn_ctx_kernel(q_ref, kv_ref, qm_ref, kvu_ref, kr_ref, ona_ref, omla_ref):
    lo = _low_half()
    for p in range(NA_H // 2):
        qb = _pair(q_ref, p) * NA_QSCALE
        kb = _pair(kv_ref, p).astype(BF16)
        vb = _pair(kv_ref, p, NA_W).astype(BF16)
        outs = []
        for e in range(2):
            q = jnp.where(lo if e == 0 else jnp.logical_not(lo), qb, 0.0).astype(BF16)
            outs.append(_softmax_pv([lax.dot_general(q, kb, NT, preferred_element_type=F32)], [vb]))
        ona_ref[:, p * LANES:(p + 1) * LANES] = jnp.where(lo, outs[0], outs[1]).astype(ona_ref.dtype)
    kr_lo, kr_hi = _rope_key_forms(kr_ref[...])
    for p in range(MLA_H // 2):
        src = (_pair(kvu_ref, p).astype(BF16), kr_lo, kr_hi, _pair(kvu_ref, p, MLA_H * MLA_NOPE).astype(BF16))
        omla_ref[:, p * LANES:(p + 1) * LANES] = _mla_pair(qm_ref, p, [src], lo).astype(omla_ref.dtype)


def attn_context(q, kv, qm, kvu, kr):
    seq = lambda w: pl.BlockSpec((L_CTX, w), lambda b: (b, 0))
    return pl.pallas_call(
        _attn_ctx_kernel,
        grid=(N_CTX,),
        in_specs=[seq(NA_W), seq(2 * NA_W), seq(MLA_H * MLA_QK), seq(MLA_H * (MLA_NOPE + MLA_V)), seq(LANES)],
        out_specs=[seq(NA_W), seq(MLA_H * MLA_V)],
        out_shape=[jax.ShapeDtypeStruct((T_CTX, NA_W), BF16), jax.ShapeDtypeStruct((T_CTX, MLA_H * MLA_V), BF16)],
        compiler_params=_cparams(("arbitrary",)),
        name="attn_ctx",
    )(q, kv, qm, kvu, kr)


N_DR = 2 * NA_WIN_R - 1
GRID_ROWS = L_LAT // GRID_W


def _na_bias_kernel(t_ref, o_ref):
    neg = jnp.full((GRID_W, GRID_W), NEG, F32)
    for r in range(GRID_ROWS):
        r0 = min(max(r - NA_WIN_R // 2, 0), GRID_ROWS - NA_WIN_R)
        for kr in range(GRID_ROWS):
            in_window = r0 <= kr < r0 + NA_WIN_R
            blk = t_ref[0, kr - r + NA_WIN_R - 1] if in_window else neg
            o_ref[0, r * GRID_W:(r + 1) * GRID_W, kr * GRID_W:(kr + 1) * GRID_W] = blk


def neighbourhood_bias(rel_bias):
    c = np.arange(GRID_W)
    c0 = np.clip(c - NA_WIN_C // 2, 0, GRID_W - NA_WIN_C)
    col_ok = (c[None, :] >= c0[:, None]) & (c[None, :] < c0[:, None] + NA_WIN_C)
    dc = np.clip(c[None, :] - c[:, None], -(NA_WIN_C - 1), NA_WIN_C - 1) + NA_WIN_C - 1
    sel_c = (dc[:, :, None] == np.arange(2 * NA_WIN_C - 1)).astype(np.float32)
    t = jnp.einsum("hdj,qcj->hdqc", rel_bias.astype(F32), jnp.asarray(sel_c), precision=HIGHEST)
    t = jnp.where(jnp.asarray(col_ok)[None, None], t * LOG2E, NEG)
    return pl.pallas_call(
        _na_bias_kernel,
        grid=(NA_H,),
        in_specs=[pl.BlockSpec((1, N_DR, GRID_W, GRID_W), lambda h: (h, 0, 0, 0))],
        out_specs=pl.BlockSpec((1, L_LAT, L_LAT), lambda h: (h, 0, 0)),
        out_shape=jax.ShapeDtypeStruct((NA_H, L_LAT, L_LAT), F32),
        compiler_params=_cparams(("arbitrary",)),
        name="na_bias",
    )(t)


def _na_lat_kernel(q_ref, k_ref, v_ref, kc_ref, vc_ref, b_ref, o_ref):
    lo = _low_half()
    for p in range(NA_H // 2):
        qb = _pair(q_ref, p) * NA_QSCALE
        kb = _pair(k_ref, p).astype(BF16)
        vb = _pair(v_ref, p).astype(BF16)
        outs = []
        for e in range(2):
            h = 2 * p + e
            half = slice(e * NA_D, (e + 1) * NA_D)
            q = jnp.where(lo if e == 0 else jnp.logical_not(lo), qb, 0.0).astype(BF16)
            s1 = lax.dot_general(q, kb, NT, preferred_element_type=F32) + b_ref[h]
            s2 = jnp.dot(qb[:, half].astype(BF16), kc_ref[0, 0, h].astype(BF16), preferred_element_type=F32)
            m = jnp.maximum(jnp.max(s1, axis=-1, keepdims=True), jnp.max(s2, axis=-1, keepdims=True))
            p1, p2 = jnp.exp2(s1 - m), jnp.exp2(s2 - m)
            den = jnp.sum(p1, axis=-1, keepdims=True) + jnp.sum(p2, axis=-1, keepdims=True)
            a1 = jnp.dot(p1.astype(BF16), vb, preferred_element_type=F32)
            a2 = lax.dot_general(p2.astype(BF16), vc_ref[0, 0, h].astype(BF16), NT, preferred_element_type=F32)
            outs.append((a1[:, half] + a2) / den)
        o_ref[:, p * LANES:(p + 1) * LANES] = jnp.concatenate(outs, axis=1).astype(o_ref.dtype)


def attn_neighbourhood_latent(q, kv, cache_kt, cache_vt, bias):
    nq = L_LAT // TM
    t0 = T_CTX // TM
    s0 = T_CTX // L_LAT
    cache = pl.BlockSpec((1, 1, NA_H, NA_D, PAST), lambda qt, b: (b, 0, 0, 0, 0))
    return pl.pallas_call(
        _na_lat_kernel,
        grid=(nq, N_LAT),
        in_specs=[pl.BlockSpec((TM, NA_W), lambda qt, b: (t0 + b * nq + qt, 0)),
                  pl.BlockSpec((L_LAT, NA_W), lambda qt, b: (s0 + b, 0)),
                  pl.BlockSpec((L_LAT, NA_W), lambda qt, b: (s0 + b, 1)),
                  cache, cache,
                  pl.BlockSpec((NA_H, TM, L_LAT), lambda qt, b: (0, qt, 0))],
        out_specs=pl.BlockSpec((TM, NA_W), lambda qt, b: (b * nq + qt, 0)),
        out_shape=jax.ShapeDtypeStruct((T_LAT, NA_W), BF16),
        compiler_params=_cparams(("arbitrary", "arbitrary")),
        name="attn_na_lat",
    )(q, kv, kv, cache_kt, cache_vt, bias)


def _mla_lat_kernel(qm_ref, kvu_ref, kr_ref, ckv_ref, krc_ref, wukv_ref, o_ref):
    lo = _low_half()
    kvc = jnp.dot(ckv_ref[0, 0].astype(BF16), wukv_ref[...], preferred_element_type=F32)
    kr_lo, kr_hi = _rope_key_forms(kr_ref[...])
    krc = jnp.concatenate([krc_ref[0, 0], jnp.zeros((PAST, LANES - MLA_ROPE), F32)], axis=1)
    krc_lo, krc_hi = _rope_key_forms(krc)
    vbase = MLA_H * MLA_NOPE
    for p in range(MLA_H // 2):
        lat = (_pair(kvu_ref, p).astype(BF16), kr_lo, kr_hi, _pair(kvu_ref, p, vbase).astype(BF16))
        ctx = (_pair(kvc, p).astype(BF16), krc_lo, krc_hi, _pair(kvc, p, vbase).astype(BF16))
        o_ref[:, p * LANES:(p + 1) * LANES] = _mla_pair(qm_ref, p, [lat, ctx], lo).astype(o_ref.dtype)


def attn_mla_latent(qm, kvu, kr, cache_ckv, cache_krope, wukv_bf):
    tq = 2 * TM
    nq = L_LAT // tq
    t0 = T_CTX // tq
    s0 = T_CTX // L_LAT
    return pl.pallas_call(
        _mla_lat_kernel,
        grid=(nq, N_LAT),
        in_specs=[pl.BlockSpec((tq, MLA_H * MLA_QK), lambda qt, b: (t0 + b * nq + qt, 0)),
                  pl.BlockSpec((L_LAT, MLA_H * (MLA_NOPE + MLA_V)), lambda qt, b: (s0 + b, 0)),
                  pl.BlockSpec((L_LAT, LANES), lambda qt, b: (s0 + b, 0)),
                  pl.BlockSpec((1, 1, PAST, MLA_KVR), lambda qt, b: (b, 0, 0, 0)),
                  pl.BlockSpec((1, 1, PAST, MLA_ROPE), lambda qt, b: (b, 0, 0, 0)),
                  pl.BlockSpec(wukv_bf.shape, lambda qt, b: (0, 0))],
        out_specs=pl.BlockSpec((tq, MLA_H * MLA_V), lambda qt, b: (b * nq + qt, 0)),
        out_shape=jax.ShapeDtypeStruct((T_LAT, MLA_H * MLA_V), BF16),
        compiler_params=_cparams(("arbitrary", "arbitrary")),
        name="attn_mla_lat",
    )(qm, kvu, kr, cache_ckv, cache_krope, wukv_bf)


def moe_block(h2, ids, wts, cnt3, x, mod, gfin, w_gate, w_up, w_down, layer, *, final):
    cnt, loc, gdst, ends, tile_expert, n_used = route_tables(cnt3)
    xs, dest = dispatch_rows(h2, ids, cnt, loc, gdst, ends)
    ys = grouped_experts(xs, w_gate, w_up, w_down, tile_expert, n_used, layer)
    return moe_combine(ys, dest, x, wts, mod, gfin, final=final)


def _pad_lanes(a):
    return jnp.pad(a, ((0, 0), (0, LANES - a.shape[1])))


def _hyena_features(L):
    t = np.linspace(0.0, 1.0, L)[:, None]
    w = 2.0 * math.pi * np.arange(L) / L
    bands = np.linspace(1e-4, HY_BANDS - 1, HY_BANDS)
    ang = w[:, None] * bands[None]
    feat = np.concatenate([t, np.cos(ang), -np.sin(ang)], axis=-1)
    return jnp.asarray(np.pad(feat, ((0, 0), (0, LANES - HY_FEAT))).astype(np.float32))


def _router_params(w_gr, b_gr, w_er, b_er):
    wr = _pad_lanes(jnp.concatenate([w_gr, w_er], axis=1))
    br = _pad_lanes(jnp.concatenate([b_gr, b_er])[None])
    wr_hi = wr.astype(BF16)
    wr_lo = (wr - wr_hi.astype(F32)).astype(BF16)
    return jnp.stack([wr_hi, wr_lo]), br


def _even_layer(x, mod, g_mix, state, w_in, conv_w, conv_b, a_log, dt_bias, d_skip, g_ssd, hy_conv_w, hy_conv_b,
                hy_w1, hy_b1, hy_w2, hy_b2, hy_w3, hy_freq, hy_bias):
    n0 = D + SSD_XBC
    w_parts = (w_in[:, :n0].astype(BF16), w_in[:, n0 + SSD_H:].astype(BF16),
               _pad_lanes(w_in[:, n0:n0 + SSD_H]).astype(BF16))
    z, xbc, hy, dtr = even_in_proj(x, mod, g_mix, w_parts)
    small = (conv_w, conv_b[None], _pad_lanes(dt_bias), _pad_lanes(a_log), jnp.repeat(d_skip, SSD_P)[None], g_ssd[None])
    y_c, fin = ssd_mixer(xbc, dtr, z, None, *small, L=L_CTX, n_seq=N_CTX, row_off=0)
    (y_l,) = ssd_mixer(xbc, dtr, z, state.reshape(N_LAT, 2, SSD_H * SSD_P, SSD_N), *small,
                       L=L_LAT, n_seq=N_LAT, row_off=T_CTX)
    w1 = jnp.pad(hy_w1, ((0, LANES - HY_FEAT), (0, 0)))
    w3r = hy_w3.reshape(HY_HID, 4, D).transpose(1, 0, 2)
    deltas = jnp.asarray(np.linspace(HY_MIN_DECAY, HY_MAX_DECAY, D).astype(np.float32))[None]
    us = []
    for L, n_seq, off in ((L_CTX, N_CTX, 0), (L_LAT, N_LAT, T_CTX)):
        h4, hm = hyena_filter_spectra(_hyena_features(L), w1, hy_b1[None], hy_w2, hy_b2[None], hy_freq, w3r, deltas, L=L)
        us.append(hyena_mixer(hy, hy_conv_w, hy_conv_b[None], h4, hm, hy_bias, L=L, n_seq=n_seq, row_off=off))
    return (y_c, y_l), tuple(us), fin


def _odd_layer(x, mod, g_mix, cache_k, cache_v, cache_ckv, cache_kr, rel_bias, w_in, g_q, w_uq, g_kv, w_ukv):
    w_bf = jnp.pad(w_in, ((0, 0), (0, ODD_COLS - w_in.shape[1]))).astype(BF16)
    wuq = w_uq.reshape(MLA_QR, MLA_H, MLA_QK)
    wuq_bf = jnp.concatenate([wuq[:, :, :MLA_NOPE].reshape(MLA_QR, -1), wuq[:, :, MLA_NOPE:].reshape(MLA_QR, -1)],
                             axis=1).astype(BF16)
    wukv = w_ukv.reshape(MLA_KVR, MLA_H, MLA_NOPE + MLA_V)
    wukv_bf = jnp.concatenate([wukv[:, :, :MLA_NOPE].reshape(MLA_KVR, -1), wukv[:, :, MLA_NOPE:].reshape(MLA_KVR, -1)],
                              axis=1).astype(BF16)
    q, kv, qm, ckv, kvu, kr, k_new, v_new = odd_in_proj(x, mod, g_mix, w_bf, g_q[None], wuq_bf, g_kv[None], wukv_bf,
                                                        rope_tables())
    ona_c, omla_c = attn_context(q, kv, qm, kvu, kr)
    ona_l = attn_neighbourhood_latent(q, kv, jnp.swapaxes(cache_k, 3, 4), jnp.swapaxes(cache_v, 3, 4),
                                      neighbourhood_bias(rel_bias))
    omla_l = attn_mla_latent(qm, kvu, kr, cache_ckv, cache_kr, wukv_bf)
    return (ona_c, ona_l), (omla_c, omla_l), k_new, v_new, ckv, kr


def kernel(x_prompt, x_sample, state_ssd, cache_na_k, cache_na_v, cache_mla_ckv, cache_mla_krope, c, c_ctx, w_ada, b_ada, norm_mix, norm_ffn, norm_final, ev_w_in, ev_conv_w, ev_conv_b, ssd_A_log, ssd_dt_bias, ssd_d, ssd_norm, hy_conv_w, hy_conv_b, hy_w1, hy_b1, hy_w2, hy_b2, hy_w3, hy_freq, hy_bias, ev_w_out, od_w_in, mla_q_norm, mla_w_uq, mla_kv_norm, mla_w_ukv, na_rel_bias, od_w_out, moe_w_gr, moe_b_gr, moe_w_er, moe_b_er, moe_w_gate, moe_w_up, moe_w_down):
    x = (x_prompt.reshape(T_CTX, D), x_sample.reshape(T_LAT, D))
    cvec = jnp.zeros((MOD_ROWS, D), F32).at[0].set(c_ctx).at[1:1 + N_LAT].set(c)
    mod = ada_modulation(cvec, w_ada, b_ada)
    gfin = norm_final[None]

    y, u, fin = _even_layer(x, mod[0], norm_mix[0][None], state_ssd[:, 0], ev_w_in[0], ev_conv_w[0], ev_conv_b[0],
                            ssd_A_log[0], ssd_dt_bias[0], ssd_d[0], ssd_norm[0], hy_conv_w[0], hy_conv_b[0],
                            hy_w1[0], hy_b1[0], hy_w2[0], hy_b2[0], hy_w3[0], hy_freq[0], hy_bias[0])
    wr, br = _router_params(moe_w_gr[0], moe_b_gr[0], moe_w_er[0], moe_b_er[0])
    xn, h2, ids, wts, cnt3 = out_proj_router([y, u], ev_w_out[0].astype(BF16), x, mod[0], norm_ffn[0][None], wr, br)
    x = moe_block(h2, ids, wts, cnt3, xn, mod[0], gfin, moe_w_gate, moe_w_up, moe_w_down, 0, final=False)

    o_na, o_mla, k_new, v_new, ckv, kr = _odd_layer(x, mod[1], norm_mix[1][None], cache_na_k, cache_na_v, cache_mla_ckv,
                                           cache_mla_krope, na_rel_bias[0], od_w_in[0], mla_q_norm[0], mla_w_uq[0],
                                           mla_kv_norm[0], mla_w_ukv[0])
    wr, br = _router_params(moe_w_gr[1], moe_b_gr[1], moe_w_er[1], moe_b_er[1])
    xn, h2, ids, wts, cnt3 = out_proj_router([o_na, o_mla], od_w_out[0].astype(BF16), x, mod[1], norm_ffn[1][None], wr, br)
    y_c, y_l = moe_block(h2, ids, wts, cnt3, xn, mod[1], gfin, moe_w_gate, moe_w_up, moe_w_down, 1, final=True)

    return (y_c.reshape(N_CTX, L_CTX, D),
            y_l.reshape(N_LAT, L_LAT, D),
            fin.reshape(N_CTX, 1, 2, SSD_H, SSD_P, SSD_N),
            jnp.swapaxes(k_new, 2, 3)[:, None],
            jnp.swapaxes(v_new, 2, 3)[:, None],
            ckv[:T_CTX].reshape(N_CTX, 1, L_CTX, MLA_KVR),
            kr[:T_CTX, :MLA_ROPE].reshape(N_CTX, 1, L_CTX, MLA_ROPE))
```

```python
import functools
import math

import numpy as np
import jax
import jax.numpy as jnp
from jax import lax
from jax.experimental import pallas as pl
from jax.experimental.pallas import tpu as pltpu

F32 = jnp.float32
BF16 = jnp.bfloat16
HIGHEST = lax.Precision.HIGHEST

D = 1024
N_CTX, L_CTX = 16, 256
N_LAT, L_LAT = 8, 1024
T_CTX = N_CTX * L_CTX
T_LAT = N_LAT * L_LAT
T_ALL = T_CTX + T_LAT
PAST = 512
GRID_W = 64
EPS = 1e-6
NEG = -1e30

SSD_H, SSD_P, SSD_N, SSD_G = 16, 64, 128, 2
SSD_XBC = D + 2 * SSD_G * SSD_N
SSD_K = 5
CHUNK = 128

HY_K = 3
HY_BANDS = 16
HY_FEAT = 1 + 2 * HY_BANDS
HY_HID = 64
HY_MIN_DECAY = abs(math.log(1e-2) / 1.5)
HY_MAX_DECAY = abs(math.log(1e-2) / 0.3)

NA_H, NA_D = 8, 64
NA_W = NA_H * NA_D
NA_WIN_R, NA_WIN_C = 8, 16
MLA_H, MLA_QR, MLA_KVR = 8, 256, 128
MLA_NOPE, MLA_ROPE, MLA_V = 64, 32, 64
MLA_QK = MLA_NOPE + MLA_ROPE
ROPE_F = MLA_ROPE // 4

MOE_G, MOE_PG, MOE_E, MOE_F = 4, 8, 32, 256

LANES = 128
SUBLANES = 8
VMEM_LIMIT = 56 * 1024 * 1024

TM = 256
N_TILES = T_ALL // TM
CTX_TILES = T_CTX // TM
LAT_TILES_PER_SEQ = L_LAT // TM
MOD_ROWS = 16


def _cparams(sem):
    return pltpu.CompilerParams(dimension_semantics=sem, vmem_limit_bytes=VMEM_LIMIT)


def _mod_row(i):
    return jnp.where(i < CTX_TILES, 0, 1 + (i - CTX_TILES) // LAT_TILES_PER_SEQ)


def _silu(x):
    return x * jax.nn.sigmoid(x)


def _rms(x):
    return x * lax.rsqrt(jnp.mean(x * x, axis=-1, keepdims=True) + EPS)


def _ada_kernel(c_ref, w_ref, b_ref, o_ref):
    c = c_ref[...]
    o_ref[0] = jnp.dot(_silu(c), w_ref[0], precision=HIGHEST, preferred_element_type=F32) + b_ref[0]


def ada_modulation(cvec, w_ada, b_ada):
    depth = w_ada.shape[0]
    out = pl.pallas_call(
        _ada_kernel,
        grid=(depth, 6),
        in_specs=[
            pl.BlockSpec((MOD_ROWS, D), lambda l, j: (0, 0)),
            pl.BlockSpec((1, D, D), lambda l, j: (l, 0, j)),
            pl.BlockSpec((1, 1, D), lambda l, j: (l, 0, j)),
        ],
        out_specs=pl.BlockSpec((1, MOD_ROWS, D), lambda l, j: (l, 0, j)),
        out_shape=jax.ShapeDtypeStruct((depth, MOD_ROWS, 6 * D), F32),
        compiler_params=_cparams(("arbitrary", "arbitrary")),
        name="ada",
    )(cvec, w_ada, b_ada.reshape(depth, 1, 6 * D))
    return out.reshape(depth, MOD_ROWS, 6, D)


PROJ_CHUNK = 512


def _modulated(x, g_ref, mod_ref, shift_row):
    h = _rms(x) * g_ref[...]
    return h * (1.0 + mod_ref[0, shift_row + 1:shift_row + 2, :]) + mod_ref[0, shift_row:shift_row + 1, :]


IN_TM = 512
IN_CTX_TILES = T_CTX // IN_TM


def _even_in_kernel(xc_ref, xl_ref, mod_ref, g_ref, wa_ref, wh_ref, wd_ref, z_ref, xbc_ref, hy_ref, dt_ref):
    x = jnp.where(pl.program_id(0) < IN_CTX_TILES, xc_ref[...], xl_ref[...])
    hb = _modulated(x, g_ref, mod_ref, 0).astype(BF16)
    for o_ref, w_ref, col in ((z_ref, wa_ref, 0), (xbc_ref, wa_ref, D), (hy_ref, wh_ref, 0), (dt_ref, wd_ref, 0)):
        width = o_ref.shape[1]
        for c0 in range(0, width, PROJ_CHUNK):
            c1 = min(c0 + PROJ_CHUNK, width)
            o_ref[:, c0:c1] = jnp.dot(hb, w_ref[:, col + c0:col + c1], preferred_element_type=F32)


def _pair_specs(width):
    return [pl.BlockSpec((TM, width), lambda i: (jnp.minimum(i, CTX_TILES - 1), 0)),
            pl.BlockSpec((TM, width), lambda i: (jnp.maximum(i - CTX_TILES, 0), 0))]


def even_in_proj(x_pair, mod, g, w_parts):
    widths = (D, SSD_XBC, 3 * D, LANES)
    mod_row = lambda i: jnp.where(i < IN_CTX_TILES, 0, 1 + (i - IN_CTX_TILES) // (L_LAT // IN_TM))
    return pl.pallas_call(
        _even_in_kernel,
        grid=(T_ALL // IN_TM,),
        in_specs=[
            pl.BlockSpec((IN_TM, D), lambda i: (jnp.minimum(i, IN_CTX_TILES - 1), 0)),
            pl.BlockSpec((IN_TM, D), lambda i: (jnp.maximum(i - IN_CTX_TILES, 0), 0)),
            pl.BlockSpec((1, 6, D), lambda i: (mod_row(i), 0, 0)),
            pl.BlockSpec((1, D), lambda i: (0, 0)),
            *[_const_spec(w) for w in w_parts],
        ],
        out_specs=[pl.BlockSpec((IN_TM, w), lambda i: (i, 0)) for w in widths],
        out_shape=[jax.ShapeDtypeStruct((T_ALL, w), F32) for w in widths],
        compiler_params=_cparams(("arbitrary",)),
        name="even_in",
    )(*x_pair, mod, g, *w_parts)


PAD = SUBLANES


def _ssd_kernel(*refs, L, has_init):
    if has_init:
        (xbc_ref, dt_ref, z_ref, init_ref, cw_ref, cb_ref, dtb_ref, alog_ref, dsk_ref, gs_ref,
         y_ref, xp_s, xc_s, ya_s, st_s) = refs
        fin_ref = None
    else:
        (xbc_ref, dt_ref, z_ref, cw_ref, cb_ref, dtb_ref, alog_ref, dsk_ref, gs_ref,
         y_ref, fin_ref, xp_s, xc_s, ya_s, st_s) = refs
        init_ref = None
    nc = L // CHUNK
    half = SSD_K // 2

    xp_s[0:PAD, :] = jnp.zeros((PAD, SSD_XBC), F32)
    xp_s[PAD + L:2 * PAD + L, :] = jnp.zeros((PAD, SSD_XBC), F32)
    xp_s[PAD:PAD + L, :] = xbc_ref[...]
    for c in range(nc):
        base = PAD + c * CHUNK - half
        for j in range(SSD_XBC // LANES):
            cols = slice(j * LANES, (j + 1) * LANES)
            acc = cb_ref[:, cols] + xp_s[base:base + CHUNK, cols] * cw_ref[0:1, cols]
            for k in range(1, SSD_K):
                acc = acc + xp_s[base + k:base + k + CHUNK, cols] * cw_ref[k:k + 1, cols]
            xc_s[c * CHUNK:(c + 1) * CHUNK, cols] = _silu(acc)

    row = lax.broadcasted_iota(jnp.int32, (CHUNK, CHUNK), 0)
    colm = lax.broadcasted_iota(jnp.int32, (CHUNK, CHUNK), 1)
    lane_lo = colm < SSD_P
    tri_lo = (colm <= row).astype(F32)
    tri_up = (colm >= row).astype(F32)

    for d in range(2):
        causal = (colm <= row) if d == 0 else (colm >= row)
        for j in range(SSD_H * SSD_P // CHUNK):
            if has_init:
                st_s[:, j * CHUNK:(j + 1) * CHUNK] = init_ref[0, d, j * CHUNK:(j + 1) * CHUNK, :].T
            else:
                st_s[:, j * CHUNK:(j + 1) * CHUNK] = jnp.zeros((CHUNK, CHUNK), F32)

        def chunk_body(ci, carry, d=d, causal=causal):
            c = ci if d == 0 else nc - 1 - ci
            r0 = pl.multiple_of(c * CHUNK, CHUNK)
            dt = jax.nn.softplus(dt_ref[pl.ds(r0, CHUNK), :] + dtb_ref[d:d + 1, :])
            a = dt * (-jnp.exp(alog_ref[d:d + 1, :]))
            tri = tri_lo if d == 0 else tri_up
            cs = jnp.dot(tri, a, precision=HIGHEST, preferred_element_type=F32)
            cs_t = jnp.dot(a.T, tri.T, precision=HIGHEST, preferred_element_type=F32)
            edge = cs[CHUNK - 1:CHUNK, :] if d == 0 else cs[0:1, :]
            for g in range(SSD_G):
                bm = xc_s[pl.ds(r0, CHUNK), D + g * SSD_N:D + (g + 1) * SSD_N]
                cm = xc_s[pl.ds(r0, CHUNK), D + (SSD_G + g) * SSD_N:D + (SSD_G + g + 1) * SSD_N]
                bm_b, cm_b = bm.astype(BF16), cm.astype(BF16)
                cb = lax.dot_general(cm_b, bm_b, (((1,), (1,)), ((), ())), preferred_element_type=F32)
                bm_t = bm.T.astype(BF16)
                pairs = SSD_H // SSD_G // 2
                for pp in range(pairs):
                    p = g * pairs + pp
                    h0, h1 = 2 * p, 2 * p + 1
                    cols = slice(p * CHUNK, (p + 1) * CHUNK)
                    xs = xc_s[pl.ds(r0, CHUNK), cols]
                    xdt = xs * jnp.where(lane_lo, dt[:, h0:h0 + 1], dt[:, h1:h1 + 1])
                    cs_b = [jnp.broadcast_to(cs[:, h:h + 1], (CHUNK, CHUNK)) for h in (h0, h1)]
                    ms = [cb * jnp.exp(jnp.where(causal, cs_b[e] - cs_t[h:h + 1, :], NEG)) for e, h in enumerate((h0, h1))]
                    cs_p = jnp.where(lane_lo, cs_b[0], cs_b[1])
                    edge_p = jnp.where(lane_lo[0:1, :], edge[:, h0:h0 + 1], edge[:, h1:h1 + 1])
                    mcat = jnp.concatenate(ms, axis=1).astype(BF16)
                    xbd = jnp.concatenate([jnp.where(lane_lo, xdt, 0.0), jnp.where(lane_lo, 0.0, xdt)],
                                          axis=0).astype(BF16)
                    y_diag = jnp.dot(mcat, xbd, preferred_element_type=F32)
                    st = st_s[:, cols]
                    y_off = jnp.dot(cm_b, st.astype(BF16), preferred_element_type=F32)
                    y_off = y_off * jnp.exp(cs_p)
                    y = y_diag + y_off
                    if d == 0:
                        ya_s[pl.ds(r0, CHUNK), cols] = y
                    else:
                        ya_s[pl.ds(r0, CHUNK), cols] = ya_s[pl.ds(r0, CHUNK), cols] + y
                    xdd = (xdt * jnp.exp(edge_p - cs_p)).astype(BF16)
                    snew = jnp.dot(bm_t, xdd, preferred_element_type=F32)
                    st_s[:, cols] = st * jnp.exp(edge_p) + snew
            return carry

        lax.fori_loop(0, nc, chunk_body, 0)
        if fin_ref is not None:
            for j in range(SSD_H * SSD_P // CHUNK):
                fin_ref[0, d, j * CHUNK:(j + 1) * CHUNK, :] = st_s[:, j * CHUNK:(j + 1) * CHUNK].T

    def out_body(c, carry):
        r0 = pl.multiple_of(c * CHUNK, CHUNK)
        y = ya_s[pl.ds(r0, CHUNK), :] + xc_s[pl.ds(r0, CHUNK), 0:D] * dsk_ref[...]
        y = y * _silu(z_ref[pl.ds(r0, CHUNK), :])
        y_ref[pl.ds(r0, CHUNK), :] = (_rms(y) * gs_ref[...]).astype(y_ref.dtype)
        return carry

    lax.fori_loop(0, nc, out_body, 0)


def ssd_mixer(xbc, dtr, z, init, cw, cb, dtb, alog, dsk, gs, *, L, n_seq, row_off):
    blk0 = row_off // L
    has_init = init is not None
    seq = lambda w: pl.BlockSpec((L, w), lambda b: (blk0 + b, 0))
    full = lambda arr: pl.BlockSpec(arr.shape, lambda b: (0,) * arr.ndim)
    in_specs = [seq(SSD_XBC), seq(LANES), seq(D)]
    args = [xbc, dtr, z]
    if has_init:
        in_specs.append(pl.BlockSpec((1, 2, SSD_H * SSD_P, SSD_N), lambda b: (b, 0, 0, 0)))
        args.append(init)
    small = [cw, cb, dtb, alog, dsk, gs]
    in_specs += [full(a) for a in small]
    args += small
    out_specs = [pl.BlockSpec((L, D), lambda b: (b, 0))]
    out_shape = [jax.ShapeDtypeStruct((n_seq * L, D), BF16)]
    if not has_init:
        out_specs.append(pl.BlockSpec((1, 2, SSD_H * SSD_P, SSD_N), lambda b: (b, 0, 0, 0)))
        out_shape.append(jax.ShapeDtypeStruct((n_seq, 2, SSD_H * SSD_P, SSD_N), F32))
    return pl.pallas_call(
        functools.partial(_ssd_kernel, L=L, has_init=has_init),
        grid=(n_seq,),
        in_specs=in_specs,
        out_specs=out_specs,
        out_shape=out_shape,
        scratch_shapes=[
            pltpu.VMEM((L + 2 * PAD, SSD_XBC), F32),
            pltpu.VMEM((L, SSD_XBC), F32),
            pltpu.VMEM((L, D), F32),
            pltpu.VMEM((SSD_N, SSD_H * SSD_P), F32),
        ],
        compiler_params=_cparams(("arbitrary",)),
        name=f"ssd_{L}",
    )(*args)


HY_CB = 256


def filter_dft_matrices(L):
    H = L // 2
    s = np.arange(L, dtype=np.int64)[None, :]
    k = np.arange(H, dtype=np.int64)[:, None]
    ang = lambda kk: ((kk * s) % (2 * L)).astype(np.float64) * (math.pi / L)
    ca, cb = np.cos(ang(k)), np.cos(ang(L - k))
    sa, sb = np.sin(ang(k)), np.sin(ang(L - k))
    cb[0] = np.where(s[0] % 2 == 0, 1.0, -1.0)
    sa[0], sb[0] = 0.0, 0.0
    fm = np.zeros((2 * SUBLANES, L))
    fm[0], fm[1] = np.cos(ang(H))[0], np.sin(ang(H))[0]
    mats = (np.concatenate([ca, cb], axis=0), np.concatenate([sa, sb], axis=0), fm)
    return tuple(jnp.asarray(m.astype(np.float32)).astype(BF16) for m in mats)


def _const_spec(arr):
    return pl.BlockSpec(arr.shape, lambda *_: (0,) * arr.ndim, pipeline_mode=pl.Buffered(1))


def _hy_filter_kernel(feat_ref, w1_ref, b1_ref, w2_ref, b2_ref, fr_ref, w3_ref, dl_ref, fs_ref, fd_ref, fm_ref,
                      h_ref, hm_ref, *, L):
    H = L // 2
    hp = functools.partial(jnp.dot, precision=HIGHEST, preferred_element_type=F32)
    hdn = jnp.sin(fr_ref[0:1, :] * (hp(feat_ref[...], w1_ref[...]) + b1_ref[...]))
    hdn = jnp.sin(fr_ref[1:2, :] * (hp(hdn, w2_ref[...]) + b2_ref[...]))
    rowi = lax.broadcasted_iota(jnp.int32, (L, 1), 0)
    t = rowi.astype(F32) * (1.0 / (L - 1))
    dec = jnp.exp(-t * dl_ref[...])
    first = rowi == 0
    for o in range(2):
        fwd = hp(hdn, w3_ref[2 * o]) * dec
        bwd = jnp.where(first, 0.0, hp(hdn, w3_ref[2 * o + 1]) * dec)
        hs, hd = (fwd + bwd).astype(BF16), (fwd - bwd).astype(BF16)
        ss = jnp.dot(fs_ref[...], hs, preferred_element_type=F32)
        sd = jnp.dot(fd_ref[...], hd, preferred_element_type=F32)
        h_ref[o, 0] = ss[0:H]
        h_ref[o, 1] = sd[0:H]
        h_ref[o, 2] = ss[H:L]
        h_ref[o, 3] = sd[H:L]
        mid_r = jnp.dot(fm_ref[...], hs, preferred_element_type=F32)
        mid_n = jnp.dot(fm_ref[...], hd, preferred_element_type=F32)
        hm_ref[o] = jnp.concatenate([mid_r[0:1], mid_n[1:2], jnp.zeros((SUBLANES - 2, mid_r.shape[1]), F32)], axis=0)


def hyena_filter_spectra(feat, w1, b1, w2, b2, freq, w3r, deltas, *, L):
    full = lambda arr: pl.BlockSpec(arr.shape, lambda j: (0,) * arr.ndim)
    mats = filter_dft_matrices(L)
    return pl.pallas_call(
        functools.partial(_hy_filter_kernel, L=L),
        grid=(D // HY_CB,),
        in_specs=[full(feat), full(w1), full(b1), full(w2), full(b2), full(freq),
                  pl.BlockSpec((4, HY_HID, HY_CB), lambda j: (0, 0, j)),
                  pl.BlockSpec((1, HY_CB), lambda j: (0, j))] + [_const_spec(m) for m in mats],
        out_specs=[pl.BlockSpec((2, 4, L // 2, HY_CB), lambda j: (0, 0, 0, j)),
                   pl.BlockSpec((2, SUBLANES, HY_CB), lambda j: (0, 0, j))],
        out_shape=[jax.ShapeDtypeStruct((2, 4, L // 2, D), F32), jax.ShapeDtypeStruct((2, SUBLANES, D), F32)],
        compiler_params=_cparams(("arbitrary",)),
        name=f"hy_filter_{L}",
    )(feat, w1, b1, w2, b2, freq, w3r, deltas, *mats)


def split_dft_matrices(L):
    H = L // 2
    k = np.arange(H, dtype=np.int64)[:, None]
    m = np.arange(H, dtype=np.int64)[None, :]
    alt = np.where(m % 2 == 0, 1.0, -1.0)
    ang_e = ((k * m) % L).astype(np.float64) * (2 * math.pi / L)
    ang_o = ((k * (2 * m + 1)) % (2 * L)).astype(np.float64) * (math.pi / L)
    ce, se, co, so = np.cos(ang_e), np.sin(ang_e), np.cos(ang_o), np.sin(ang_o)
    se[0], so[0] = alt[0], alt[0]
    w = np.where(k == 0, 1.0, 2.0) / (2 * L)
    fe = np.concatenate([ce, se], axis=0)
    fo = np.concatenate([co, so], axis=0)
    ge = np.concatenate([(ce * w).T, se.T / L], axis=1)
    go = np.concatenate([(co * w).T, so.T / L], axis=1)
    return tuple(jnp.asarray(a.astype(np.float32)).astype(BF16) for a in (fe, fo, ge, go))


def _store_lane_blocks(ref, val):
    for c in range(ref.shape[0]):
        ref[c] = val[:, c * LANES:(c + 1) * LANES]


def _load_parity(ref, parity, n):
    return jnp.concatenate([ref[c, pl.ds(parity, n, stride=2), :] for c in range(ref.shape[0])], axis=1)


def _hyena_kernel(p0_ref, p1_ref, p2_ref, w0_ref, w1_ref, w2_ref, b0_ref, b1_ref, b2_ref, h_ref, hm_ref, hb_ref,
                  fe_ref, fo_ref, ge_ref, go_ref, o_ref, xp_s, u_s, y_s, *, L, cb):
    H = L // 2
    xp_s[0:PAD, :] = jnp.zeros((PAD, cb), F32)
    xp_s[PAD + L:2 * PAD + L, :] = jnp.zeros((PAD, cb), F32)
    first = lax.broadcasted_iota(jnp.int32, (H, 1), 0) == 0

    def conv(p_ref, w_ref, b_ref):
        xp_s[PAD:PAD + L, :] = p_ref[...]
        acc = b_ref[...] + xp_s[PAD - 1:PAD - 1 + L, :] * w_ref[0:1, :]
        for k in range(1, HY_K):
            acc = acc + xp_s[PAD - 1 + k:PAD - 1 + k + L, :] * w_ref[k:k + 1, :]
        return acc

    u = conv(p0_ref, w0_ref, b0_ref)
    for o, (p_ref, w_ref, b_ref) in enumerate(((p1_ref, w1_ref, b1_ref), (p2_ref, w2_ref, b2_ref))):
        _store_lane_blocks(u_s, u)
        se = jnp.dot(fe_ref[...], _load_parity(u_s, 0, H).astype(BF16), preferred_element_type=F32)
        so = jnp.dot(fo_ref[...], _load_parity(u_s, 1, H).astype(BF16), preferred_element_type=F32)
        e, es, od, os_ = se[0:H], se[H:L], so[0:H], so[H:L]
        b0, b1 = e + od, e - od
        b2 = jnp.where(first, es, es + os_)
        b3 = jnp.where(first, os_, os_ - es)
        har, han, hbr, hbn = h_ref[o, 0], h_ref[o, 1], h_ref[o, 2], h_ref[o, 3]
        hmr, hmn = hm_ref[o, 0:1, :], hm_ref[o, 1:2, :]
        y0 = b0 * har - b2 * han
        y1 = b1 * hbr - b3 * hbn
        y2 = b0 * han + b2 * har
        y3 = b1 * hbn + b3 * hbr
        mid_r = b2[0:1] * hmr - b3[0:1] * hmn
        mid_n = b2[0:1] * hmn + b3[0:1] * hmr
        de = jnp.where(first, mid_r, y2 - y3)
        do = jnp.where(first, mid_n, y2 + y3)
        ye = jnp.dot(ge_ref[...], jnp.concatenate([y0 + y1, de], axis=0).astype(BF16), preferred_element_type=F32)
        yo = jnp.dot(go_ref[...], jnp.concatenate([y0 - y1, do], axis=0).astype(BF16), preferred_element_type=F32)
        for c in range(cb // LANES):
            y_s[c, pl.ds(0, H, stride=2), :] = ye[:, c * LANES:(c + 1) * LANES]
            y_s[c, pl.ds(1, H, stride=2), :] = yo[:, c * LANES:(c + 1) * LANES]
        y = jnp.concatenate([y_s[c] for c in range(cb // LANES)], axis=1)
        u = conv(p_ref, w_ref, b_ref) * (y + u * hb_ref[o:o + 1, :])
    o_ref[...] = u.astype(o_ref.dtype)


def hyena_mixer(hy, conv_w, conv_b, h4, hm, hy_bias, *, L, n_seq, row_off):
    blk0 = row_off // L
    cb = min(D, HY_CB * (L_LAT // L))
    nj = D // cb
    H = L // 2
    part = lambda q: pl.BlockSpec((L, cb), lambda j, b: (blk0 + b, q * nj + j))
    wpart = lambda q: pl.BlockSpec((HY_K, cb), lambda j, b: (0, q * nj + j))
    bpart = lambda q: pl.BlockSpec((1, cb), lambda j, b: (0, q * nj + j))
    mats = split_dft_matrices(L)
    return pl.pallas_call(
        functools.partial(_hyena_kernel, L=L, cb=cb),
        grid=(nj, n_seq),
        in_specs=[part(0), part(1), part(2), wpart(0), wpart(1), wpart(2), bpart(0), bpart(1), bpart(2),
                  pl.BlockSpec((2, 4, H, cb), lambda j, b: (0, 0, 0, j)),
                  pl.BlockSpec((2, SUBLANES, cb), lambda j, b: (0, 0, j)),
                  pl.BlockSpec((2, cb), lambda j, b: (0, j))]
                 + [_const_spec(m) for m in mats],
        out_specs=pl.BlockSpec((L, cb), lambda j, b: (b, j)),
        out_shape=jax.ShapeDtypeStruct((n_seq * L, D), BF16),
        scratch_shapes=[pltpu.VMEM((L + 2 * PAD, cb), F32), pltpu.VMEM((cb // LANES, L, LANES), F32),
                        pltpu.VMEM((cb // LANES, L, LANES), F32)],
        compiler_params=_cparams(("arbitrary", "arbitrary")),
        name=f"hyena_{L}",
    )(hy, hy, hy, conv_w, conv_w, conv_w, conv_b, conv_b, conv_b, h4, hm, hy_bias, *mats)


ROUTER_LANES = LANES
BIG_LANE = 1e9


ROW_GROUP = D // LANES


def _store_row_groups(ref, val):
    n = val.shape[0]
    for s in range(ROW_GROUP):
        ref[pl.ds(s, n, stride=ROW_GROUP), :] = val[:, s * LANES:(s + 1) * LANES]


def _load_row_groups(ref, n, s):
    return ref[pl.ds(s, n, stride=ROW_GROUP), :]


def _first_max_lane(v, lanef):
    m = jnp.max(v, axis=-1, keepdims=True)
    return m, jnp.min(jnp.where(v == m, lanef, BIG_LANE), axis=-1, keepdims=True)


def _out_router_kernel(*refs, n_in, x_is_pair):
    a_refs = refs[:2 * n_in]
    refs = refs[2 * n_in:]
    is_ctx = pl.program_id(0) < CTX_TILES
    if x_is_pair:
        x = jnp.where(is_ctx, refs[0][...], refs[1][...])
        refs = refs[2:]
    else:
        x = refs[0][...]
        refs = refs[1:]
    w_ref, mod_ref, gf_ref, wr_ref, br_ref, xo_ref, h2_ref, ids_ref, wts_ref, cnt_ref = refs
    acc, k0 = None, 0
    for ac_ref, al_ref in zip(a_refs[0::2], a_refs[1::2]):
        kk = ac_ref.shape[1]
        a = jnp.where(is_ctx, ac_ref[...], al_ref[...])
        part = jnp.dot(a, w_ref[k0:k0 + kk, :], preferred_element_type=F32)
        acc = part if acc is None else acc + part
        k0 += kk
    xn = x + mod_ref[0, 2:3, :] * acc
    xo_ref[...] = xn
    h2 = _modulated(xn, gf_ref, mod_ref, 3)
    h2_ref[...] = h2

    h_hi = h2.astype(BF16)
    h_lo = (h2 - h_hi.astype(F32)).astype(BF16)
    logits = (jnp.dot(h_hi, wr_ref[0], preferred_element_type=F32) + jnp.dot(h_lo, wr_ref[0], preferred_element_type=F32)
              + jnp.dot(h_hi, wr_ref[1], preferred_element_type=F32) + br_ref[...])
    lanef = lax.broadcasted_iota(jnp.int32, logits.shape, 1).astype(F32)
    gl = jnp.where(lanef < MOE_G, logits, NEG)
    gm, gi = _first_max_lane(gl, lanef)
    g_w = 1.0 / jnp.sum(jnp.exp(gl - gm), axis=-1, keepdims=True)
    lo = MOE_G + MOE_PG * gi
    el = jnp.where((lanef >= lo) & (lanef < lo + MOE_PG), logits, NEG)
    m1, e1 = _first_max_lane(el, lanef)
    m2, e2 = _first_max_lane(jnp.where(lanef == e1, NEG, el), lanef)
    p2 = jnp.exp(m2 - m1)
    w1 = g_w / (1.0 + p2)
    ids_ref[...] = jnp.where(lanef == 0, e1 - MOE_G, jnp.where(lanef == 1, e2 - MOE_G, 0.0)).astype(jnp.int32)
    wts_ref[...] = jnp.where(lanef == 0, w1, jnp.where(lanef == 1, w1 * p2, 0.0))
    chosen = ((lanef == e1 - MOE_G) | (lanef == e2 - MOE_G)).astype(F32)
    cnt_ref[0] = jnp.sum(chosen, axis=0, keepdims=True).astype(jnp.int32)


def out_proj_router(acts, w_bf, x, mod, gf, wr, br):
    tile = lambda w: pl.BlockSpec((TM, w), lambda i: (i, 0))
    full = lambda arr: pl.BlockSpec(arr.shape, lambda i: (0,) * arr.ndim)
    x_is_pair = isinstance(x, tuple)
    xs = x if x_is_pair else (x,)
    return pl.pallas_call(
        functools.partial(_out_router_kernel, n_in=len(acts), x_is_pair=x_is_pair),
        grid=(N_TILES,),
        in_specs=[s for a in acts for s in _pair_specs(a[0].shape[1])]
                 + (_pair_specs(D) if x_is_pair else [tile(D)])
                 + [full(w_bf), pl.BlockSpec((1, 6, D), lambda i: (_mod_row(i), 0, 0)), full(gf), full(wr), full(br)],
        out_specs=[tile(D), tile(D), tile(ROUTER_LANES), tile(ROUTER_LANES),
                   pl.BlockSpec((1, 1, ROUTER_LANES), lambda i: (i, 0, 0))],
        out_shape=[jax.ShapeDtypeStruct((T_ALL, D), F32), jax.ShapeDtypeStruct((T_ALL, D), F32),
                   jax.ShapeDtypeStruct((T_ALL, ROUTER_LANES), jnp.int32),
                   jax.ShapeDtypeStruct((T_ALL, ROUTER_LANES), F32),
                   jax.ShapeDtypeStruct((N_TILES, 1, ROUTER_LANES), jnp.int32)],
        compiler_params=_cparams(("arbitrary",)),
        name="out_router",
    )(*[part for a in acts for part in a], *xs, w_bf, mod, gf, wr, br)


N_ASSIGN = 2 * T_ALL
MOE_TILES = N_ASSIGN // TM + MOE_E
N_SLOTS = MOE_TILES * TM


def route_tables(cnt3):
    cnt = cnt3[:, 0, :MOE_E]
    total = jnp.sum(cnt, axis=0)
    padded = (total + TM - 1) // TM * TM
    ends = jnp.cumsum(padded)
    gdst = (ends - padded)[None, :] + jnp.cumsum(cnt, axis=0) - cnt
    loc = jnp.cumsum(cnt, axis=1) - cnt
    starts = jnp.arange(MOE_TILES, dtype=jnp.int32) * TM
    tile_expert = jnp.minimum(jnp.sum((ends[None, :] <= starts[:, None]).astype(jnp.int32), axis=1), MOE_E - 1)
    n_used = (ends[-1] // TM).astype(jnp.int32).reshape(1)
    return cnt, loc, gdst, ends, tile_expert, n_used


RUN_BITS = (2 * TM).bit_length()
RUN_SMALL_BITS = 6


def _dispatch_kernel(cnt_s, loc_s, gdst_s, ends_s, h_ref, ids_ref, gcol_ref, xs_ref, dest_ref, srt, zbuf, sem, zsem):
    i = pl.program_id(0)
    slot = i % 2
    n_rows = 2 * TM

    @pl.when(i == 0)
    def _():
        zbuf[...] = jnp.zeros(zbuf.shape, zbuf.dtype)
        n_used = ends_s[MOE_E - 1] // TM
        for phase in ("start", "wait"):
            def tail(t, c, phase=phase):
                dst = pl.multiple_of(t * (TM * ROW_GROUP), TM * ROW_GROUP)
                cp = pltpu.make_async_copy(zbuf, xs_ref.at[pl.ds(dst, TM * ROW_GROUP), :], zsem)
                cp.start() if phase == "start" else cp.wait()
                return c

            lax.fori_loop(n_used, MOE_TILES, tail, 0)
            for e in range(MOE_E):
                end = ends_s[e]
                prev = ends_s[e - 1] if e > 0 else 0

                @pl.when(end > prev)
                def _(end=end, phase=phase):
                    dst = pl.multiple_of((end - TM) * ROW_GROUP, TM * ROW_GROUP)
                    cp = pltpu.make_async_copy(zbuf, xs_ref.at[pl.ds(dst, TM * ROW_GROUP), :], zsem)
                    cp.start() if phase == "start" else cp.wait()

    idt = ids_ref[...].astype(F32).T
    sub = lax.broadcasted_iota(jnp.int32, (LANES, TM), 0).astype(F32)
    m0 = (sub == idt[0:1, :]).astype(F32)
    m1 = (sub == idt[1:2, :]).astype(F32)
    mt = (m0 + m1).astype(BF16)
    tr = lax.broadcasted_iota(jnp.int32, (TM, TM), 0)
    tc = lax.broadcasted_iota(jnp.int32, (TM, TM), 1)
    earlier = jnp.dot(mt, (tr < tc).astype(BF16), preferred_element_type=F32)
    er = lax.broadcasted_iota(jnp.int32, (LANES, LANES), 0)
    ec = lax.broadcasted_iota(jnp.int32, (LANES, LANES), 1)
    below = jnp.dot((ec < er).astype(BF16), mt, preferred_element_type=F32)
    local = jnp.sum(below, axis=1, keepdims=True) + earlier
    glob = gcol_ref[0] + earlier
    pos0 = jnp.sum(m0 * local, axis=0, keepdims=True)
    pos1 = jnp.sum(m1 * local, axis=0, keepdims=True)
    dest_ref[0] = jnp.concatenate([jnp.sum(m0 * glob, axis=0, keepdims=True),
                                   jnp.sum(m1 * glob, axis=0, keepdims=True)], axis=0).astype(jnp.int32)

    srow = lax.broadcasted_iota(jnp.int32, (n_rows, TM), 0).astype(F32)
    perm = jnp.where((srow == pos0) | (srow == pos1), 1.0, 0.0).astype(BF16)
    _store_row_groups(srt.at[slot], jnp.dot(perm, h_ref[...].astype(BF16), preferred_element_type=F32))

    def run_pieces(n, s0, d0, bits):
        for b in bits:
            size = 1 << b
            off = (n >> (b + 1)) << (b + 1)

            @pl.when(((n >> b) & 1) == 1)
            def _(size=size, off=off):
                src = pl.multiple_of((s0 + off) * ROW_GROUP, ROW_GROUP)
                dst = pl.multiple_of((d0 + off) * ROW_GROUP, ROW_GROUP)
                pltpu.make_async_copy(srt.at[slot, pl.ds(src, size * ROW_GROUP), :],
                                      xs_ref.at[pl.ds(dst, size * ROW_GROUP), :], sem.at[slot]).start()

    for e in range(MOE_E):
        n, s0, d0 = cnt_s[0, 0, e], loc_s[0, 0, e], gdst_s[0, 0, e]

        @pl.when(n >= (1 << RUN_SMALL_BITS))
        def _(n=n, s0=s0, d0=d0):
            run_pieces(n, s0, d0, reversed(range(RUN_SMALL_BITS, RUN_BITS)))

        run_pieces(n, s0, d0, reversed(range(RUN_SMALL_BITS)))

    def wait(s):
        pltpu.make_async_copy(srt.at[s], xs_ref.at[pl.ds(0, n_rows * ROW_GROUP), :], sem.at[s]).wait()

    @pl.when(i > 0)
    def _():
        wait(1 - slot)

    @pl.when(i == N_TILES - 1)
    def _():
        wait(slot)


def dispatch_rows(h2, ids, cnt, loc, gdst, ends):
    tab = lambda: pl.BlockSpec((1, 1, MOE_E), lambda i: (i, 0, 0), memory_space=pltpu.SMEM)
    gcol = jnp.pad(gdst.astype(F32), ((0, 0), (0, LANES - MOE_E)))[:, :, None]
    return pl.pallas_call(
        _dispatch_kernel,
        grid=(N_TILES,),
        in_specs=[tab(), tab(), tab(), pl.BlockSpec(memory_space=pltpu.SMEM),
                  pl.BlockSpec((TM, D), lambda i: (i, 0)), pl.BlockSpec((TM, ROUTER_LANES), lambda i: (i, 0)),
                  pl.BlockSpec((1, LANES, 1), lambda i: (i, 0, 0))],
        out_specs=[pl.BlockSpec(memory_space=pl.ANY), pl.BlockSpec((1, 2, TM), lambda i: (i, 0, 0))],
        out_shape=[jax.ShapeDtypeStruct((N_SLOTS * ROW_GROUP, LANES), F32),
                   jax.ShapeDtypeStruct((N_TILES, 2, TM), jnp.int32)],
        scratch_shapes=[pltpu.VMEM((2, 2 * TM * ROW_GROUP, LANES), F32), pltpu.VMEM((TM * ROW_GROUP, LANES), F32),
                        pltpu.SemaphoreType.DMA((2,)), pltpu.SemaphoreType.DMA(())],
        compiler_params=_cparams(("arbitrary",)),
        name="moe_dispatch",
    )(cnt.reshape(N_TILES, 1, MOE_E), loc.reshape(N_TILES, 1, MOE_E), gdst.reshape(N_TILES, 1, MOE_E),
      ends, h2, ids, gcol)


DMA_UNROLL = 8


def _start_group_gather(src_hbm, idx_ref, n, dst_ref, sem):
    def body(j, c):
        for u in range(DMA_UNROLL):
            r = j * DMA_UNROLL + u
            src = pl.multiple_of(idx_ref[0, 0, r] * ROW_GROUP, ROW_GROUP)
            dst = pl.multiple_of(r * ROW_GROUP, ROW_GROUP)
            pltpu.make_async_copy(src_hbm.at[pl.ds(src, ROW_GROUP), :], dst_ref.at[pl.ds(dst, ROW_GROUP), :],
                                  sem).start(priority=u % 2)
        return c

    lax.fori_loop(0, n // DMA_UNROLL, body, 0)


def _wait_group_gather(src_hbm, dst_ref, sem):
    pltpu.make_async_copy(src_hbm.at[pl.ds(0, dst_ref.shape[0]), :], dst_ref, sem).wait()


def _experts_kernel(te_ref, nu_ref, x_ref, wg_ref, wu_ref, wd_ref, o_ref, xcat):
    i = pl.program_id(0)

    @pl.when(i < nu_ref[0])
    def _():
        for s in range(ROW_GROUP):
            xcat[:, s * LANES:(s + 1) * LANES] = _load_row_groups(x_ref, TM, s).astype(BF16)
        x = xcat[...]
        g = jnp.dot(x, wg_ref[0, 0].astype(BF16), preferred_element_type=F32)
        u = jnp.dot(x, wu_ref[0, 0].astype(BF16), preferred_element_type=F32)
        hid = (_silu(g) * u).astype(BF16)
        _store_row_groups(o_ref, jnp.dot(hid, wd_ref[0, 0].astype(BF16), preferred_element_type=F32))

    @pl.when(i >= nu_ref[0])
    def _():
        o_ref[...] = jnp.zeros(o_ref.shape, o_ref.dtype)


def grouped_experts(xs, w_gate, w_up, w_down, tile_expert, n_used, layer):
    wspec = lambda a, b: pl.BlockSpec((1, 1, a, b), lambda i, te, nu: (layer, te[i], 0, 0))
    return pl.pallas_call(
        _experts_kernel,
        grid_spec=pltpu.PrefetchScalarGridSpec(
            num_scalar_prefetch=2,
            grid=(MOE_TILES,),
            in_specs=[pl.BlockSpec((TM * ROW_GROUP, LANES), lambda i, te, nu: (jnp.minimum(i, nu[0] - 1), 0)),
                      wspec(D, MOE_F), wspec(D, MOE_F), wspec(MOE_F, D)],
            out_specs=pl.BlockSpec((TM * ROW_GROUP, LANES), lambda i, te, nu: (i, 0)),
            scratch_shapes=[pltpu.VMEM((TM, D), BF16)],
        ),
        out_shape=jax.ShapeDtypeStruct((N_SLOTS * ROW_GROUP, LANES), F32),
        compiler_params=_cparams(("arbitrary",)),
        name="moe_experts",
    )(tile_expert, n_used, xs, w_gate, w_up, w_down)


def _combine_kernel(cur_ref, nxt_ref, ys_hbm, x_ref, wts_ref, mod_ref, gfin_ref, *rest, final):
    *o_refs, buf, sem = rest
    i = pl.program_id(0)
    slot = i % 2

    @pl.when(i == 0)
    def _():
        _start_group_gather(ys_hbm, cur_ref, 2 * TM, buf.at[0], sem.at[0])

    @pl.when(i + 1 < N_TILES)
    def _():
        _start_group_gather(ys_hbm, nxt_ref, 2 * TM, buf.at[1 - slot], sem.at[1 - slot])

    _wait_group_gather(ys_hbm, buf.at[slot], sem.at[slot])
    w0, w1 = wts_ref[:, 0:1], wts_ref[:, 1:2]

    def finish(o_ref):
        for s in range(ROW_GROUP):
            cols = slice(s * LANES, (s + 1) * LANES)
            y0 = buf[slot, pl.ds(s, TM, stride=ROW_GROUP), :]
            y1 = buf[slot, pl.ds(TM * ROW_GROUP + s, TM, stride=ROW_GROUP), :]
            o_ref[:, cols] = x_ref[:, cols] + mod_ref[0, 5:6, cols] * (w0 * y0 + w1 * y1)
        if final:
            o_ref[...] = _rms(o_ref[...]) * gfin_ref[...]

    if final:
        pl.when(i < CTX_TILES)(lambda: finish(o_refs[0]))
        pl.when(i >= CTX_TILES)(lambda: finish(o_refs[1]))
    else:
        finish(o_refs[0])


def moe_combine(ys, dest, x, wts, mod, gfin, *, final):
    tile = lambda w: pl.BlockSpec((TM, w), lambda i: (i, 0))
    if final:
        out_specs = _pair_specs(D)
        out_shape = [jax.ShapeDtypeStruct((T_CTX, D), F32), jax.ShapeDtypeStruct((T_LAT, D), F32)]
    else:
        out_specs = [tile(D)]
        out_shape = [jax.ShapeDtypeStruct((T_ALL, D), F32)]
    idx = lambda f: pl.BlockSpec((1, 1, 2 * TM), lambda i: (f(i), 0, 0), memory_space=pltpu.SMEM)
    dest3 = dest.reshape(N_TILES, 1, 2 * TM)
    out = pl.pallas_call(
        functools.partial(_combine_kernel, final=final),
        grid=(N_TILES,),
        in_specs=[idx(lambda i: i), idx(lambda i: jnp.minimum(i + 1, N_TILES - 1)),
                  pl.BlockSpec(memory_space=pl.ANY), tile(D), tile(ROUTER_LANES),
                  pl.BlockSpec((1, 6, D), lambda i: (_mod_row(i), 0, 0)),
                  pl.BlockSpec((1, D), lambda i: (0, 0))],
        out_specs=out_specs,
        out_shape=out_shape,
        scratch_shapes=[pltpu.VMEM((2, 2 * TM * ROW_GROUP, LANES), F32), pltpu.SemaphoreType.DMA((2,))],
        compiler_params=_cparams(("arbitrary",)),
        name="moe_combine",
    )(dest3, dest3, ys, x, wts, mod, gfin)
    return out if final else out[0]


ODD_COLS = 2048
ROPE_Q = MLA_H * MLA_ROPE
ROPE_SHIFT = ROPE_F


def rope_tables():
    t = np.arange(L_LAT)
    pos = np.stack([t // GRID_W, t % GRID_W], axis=1).astype(np.float64)
    inv = 10000.0 ** (-np.arange(ROPE_F, dtype=np.float64) / ROPE_F)
    lane = np.arange(ROPE_Q) % MLA_ROPE
    axis = lane // (2 * ROPE_F)
    first = (lane % (2 * ROPE_F)) < ROPE_F
    ang = pos[:, axis] * inv[lane % ROPE_F][None, :]
    cos, sin = np.cos(ang), np.sin(ang)
    tabs = [cos, np.where(first[None, :], -sin, 0.0), np.where(first[None, :], 0.0, sin)]
    ident = [np.ones((1, TM, ROPE_Q)), np.zeros((1, TM, ROPE_Q)), np.zeros((1, TM, ROPE_Q))]
    return [jnp.asarray(np.concatenate([i, tb.reshape(LAT_TILES_PER_SEQ, TM, ROPE_Q)], axis=0).astype(np.float32))
            for i, tb in zip(ident, tabs)]


def _rope(x, c, a, b):
    n = x.shape[1]
    return x * c[:, :n] + pltpu.roll(x, n - ROPE_SHIFT, 1) * a[:, :n] + pltpu.roll(x, ROPE_SHIFT, 1) * b[:, :n]


def _odd_in_kernel(x_ref, mod_ref, g_ref, w_ref, gq_ref, wuq_ref, gkv_ref, wukv_ref, rc_ref, ra_ref, rb_ref,
                   q_ref, kv_ref, qm_ref, ckv_ref, kvu_ref, kr_ref, knew_ref, vnew_ref):
    hb = _modulated(x_ref[...], g_ref, mod_ref, 0).astype(BF16)
    q_ref[...] = jnp.dot(hb, w_ref[:, 0:NA_W], preferred_element_type=F32)
    k = jnp.dot(hb, w_ref[:, NA_W:2 * NA_W], preferred_element_type=F32)
    v = jnp.dot(hb, w_ref[:, 2 * NA_W:3 * NA_W], preferred_element_type=F32)
    kv_ref[:, 0:NA_W] = k.astype(kv_ref.dtype)
    kv_ref[:, NA_W:2 * NA_W] = v.astype(kv_ref.dtype)

    @pl.when(pl.program_id(0) < CTX_TILES)
    def _():
        kt, vt = k.T, v.T
        for h in range(NA_H):
            knew_ref[0, h] = kt[h * NA_D:(h + 1) * NA_D, :]
            vnew_ref[0, h] = vt[h * NA_D:(h + 1) * NA_D, :]

    rest = jnp.dot(hb, w_ref[:, 3 * NA_W:ODD_COLS], preferred_element_type=F32)
    rc, ra, rb = rc_ref[0], ra_ref[0], rb_ref[0]
    qd = (_rms(rest[:, 0:MLA_QR]) * gq_ref[...]).astype(BF16)
    qm = jnp.dot(qd, wuq_ref[...], preferred_element_type=F32)
    qm_ref[:, 0:MLA_H * MLA_NOPE] = qm[:, 0:MLA_H * MLA_NOPE]
    qm_ref[:, MLA_H * MLA_NOPE:] = _rope(qm[:, MLA_H * MLA_NOPE:], rc, ra, rb)
    ckv = _rms(rest[:, MLA_QR:MLA_QR + MLA_KVR]) * gkv_ref[...]
    ckv_ref[...] = ckv
    kvu_ref[...] = jnp.dot(ckv.astype(BF16), wukv_ref[...],
                           preferred_element_type=F32).astype(kvu_ref.dtype)
    kr_ref[...] = _rope(rest[:, MLA_QR + MLA_KVR:], rc, ra, rb)


def odd_in_proj(x, mod, g, w_bf, gq, wuq_bf, gkv, wukv_bf, tabs):
    tile = lambda w: pl.BlockSpec((TM, w), lambda i: (i, 0))
    full = lambda arr: pl.BlockSpec(arr.shape, lambda i: (0,) * arr.ndim)
    tab = pl.BlockSpec((1, TM, ROPE_Q),
                       lambda i: (jnp.where(i < CTX_TILES, 0, 1 + (i - CTX_TILES) % LAT_TILES_PER_SEQ), 0, 0))
    outs = ((NA_W, F32), (2 * NA_W, BF16), (MLA_H * MLA_QK, F32), (MLA_KVR, F32),
            (MLA_H * (MLA_NOPE + MLA_V), BF16), (LANES, F32))
    cache = pl.BlockSpec((1, NA_H, NA_D, L_CTX), lambda i: (jnp.minimum(i, CTX_TILES - 1), 0, 0, 0))
    cache_shape = jax.ShapeDtypeStruct((N_CTX, NA_H, NA_D, L_CTX), F32)
    return pl.pallas_call(
        _odd_in_kernel,
        grid=(N_TILES,),
        in_specs=[tile(D), pl.BlockSpec((1, 6, D), lambda i: (_mod_row(i), 0, 0)), full(g), full(w_bf),
                  full(gq), full(wuq_bf), full(gkv), full(wukv_bf), tab, tab, tab],
        out_specs=[tile(w) for w, _ in outs] + [cache, cache],
        out_shape=[jax.ShapeDtypeStruct((T_ALL, w), dt) for w, dt in outs] + [cache_shape, cache_shape],
        compiler_params=_cparams(("arbitrary",)),
        name="odd_in",
    )(x, mod, g, w_bf, gq, wuq_bf, gkv, wukv_bf, *tabs)


LOG2E = math.log2(math.e)
NA_QSCALE = NA_D ** -0.5 * LOG2E
MLA_QSCALE = MLA_QK ** -0.5 * LOG2E
NT = (((1,), (1,)), ((), ()))


def _softmax_pv(scores, values):
    m = functools.reduce(jnp.maximum, [jnp.max(s, axis=-1, keepdims=True) for s in scores])
    ps = [jnp.exp2(s - m) for s in scores]
    den = functools.reduce(jnp.add, [jnp.sum(p, axis=-1, keepdims=True) for p in ps])
    acc = functools.reduce(jnp.add, [jnp.dot(p.astype(BF16), v, preferred_element_type=F32) for p, v in zip(ps, values)])
    return acc / den


def _pair(ref_or_val, p, base=0):
    return ref_or_val[:, base + p * LANES:base + (p + 1) * LANES]


def _low_half():
    return lax.broadcasted_iota(jnp.int32, (1, LANES), 1) < NA_D


def _rope_key_forms(kr):
    return kr.astype(BF16), pltpu.roll(kr, LANES // 2, 1).astype(BF16)


def _mla_pair(qm, p, sources, lo):
    outs = []
    for e in range(2):
        h = 2 * p + e
        qn = qm[:, h * MLA_NOPE:(h + 1) * MLA_NOPE] * MLA_QSCALE
        qr = qm[:, MLA_H * MLA_NOPE + h * MLA_ROPE:MLA_H * MLA_NOPE + (h + 1) * MLA_ROPE] * MLA_QSCALE
        z = jnp.zeros((qn.shape[0], LANES - MLA_QK), F32)
        qcat = jnp.concatenate([qn, qr, z] if e == 0 else [qr, z, qn], axis=1).astype(BF16)
        scores = []
        for kb, kr_lo, kr_hi, _ in sources:
            kcat = jnp.where(lo, kb, kr_hi) if e == 0 else jnp.where(lo, kr_lo, kb)
            scores.append(lax.dot_general(qcat, kcat, NT, preferred_element_type=F32))
        outs.append(_softmax_pv(scores, [src[3] for src in sources]))
    return jnp.where(lo, outs[0], outs[1])


def _attn_ctx_kernel(q_ref, kv_ref, qm_ref, kvu_ref, kr_ref, ona_ref, omla_ref):
    lo = _low_half()
    for p in range(NA_H // 2):
        qb = _pair(q_ref, p) * NA_QSCALE
        kb = _pair(kv_ref, p).astype(BF16)
        vb = _pair(kv_ref, p, NA_W).astype(BF16)
        outs = []
        for e in range(2):
            q = jnp.where(lo if e == 0 else jnp.logical_not(lo), qb, 0.0).astype(BF16)
            outs.append(_softmax_pv([lax.dot_general(q, kb, NT, preferred_element_type=F32)], [vb]))
        ona_ref[:, p * LANES:(p + 1) * LANES] = jnp.where(lo, outs[0], outs[1]).astype(ona_ref.dtype)
    kr_lo, kr_hi = _rope_key_forms(kr_ref[...])
    for p in range(MLA_H // 2):
        src = (_pair(kvu_ref, p).astype(BF16), kr_lo, kr_hi, _pair(kvu_ref, p, MLA_H * MLA_NOPE).astype(BF16))
        omla_ref[:, p * LANES:(p + 1) * LANES] = _mla_pair(qm_ref, p, [src], lo).astype(omla_ref.dtype)


def attn_context(q, kv, qm, kvu, kr):
    seq = lambda w: pl.BlockSpec((L_CTX, w), lambda b: (b, 0))
    return pl.pallas_call(
        _attn_ctx_kernel,
        grid=(N_CTX,),
        in_specs=[seq(NA_W), seq(2 * NA_W), seq(MLA_H * MLA_QK), seq(MLA_H * (MLA_NOPE + MLA_V)), seq(LANES)],
        out_specs=[seq(NA_W), seq(MLA_H * MLA_V)],
        out_shape=[jax.ShapeDtypeStruct((T_CTX, NA_W), BF16), jax.ShapeDtypeStruct((T_CTX, MLA_H * MLA_V), BF16)],
        compiler_params=_cparams(("arbitrary",)),
        name="attn_ctx",
    )(q, kv, qm, kvu, kr)


N_DR = 2 * NA_WIN_R - 1
GRID_ROWS = L_LAT // GRID_W


def _na_bias_kernel(t_ref, o_ref):
    neg = jnp.full((GRID_W, GRID_W), NEG, F32)
    for r in range(GRID_ROWS):
        r0 = min(max(r - NA_WIN_R // 2, 0), GRID_ROWS - NA_WIN_R)
        for kr in range(GRID_ROWS):
            in_window = r0 <= kr < r0 + NA_WIN_R
            blk = t_ref[0, kr - r + NA_WIN_R - 1] if in_window else neg
            o_ref[0, r * GRID_W:(r + 1) * GRID_W, kr * GRID_W:(kr + 1) * GRID_W] = blk


def neighbourhood_bias(rel_bias):
    c = np.arange(GRID_W)
    c0 = np.clip(c - NA_WIN_C // 2, 0, GRID_W - NA_WIN_C)
    col_ok = (c[None, :] >= c0[:, None]) & (c[None, :] < c0[:, None] + NA_WIN_C)
    dc = np.clip(c[None, :] - c[:, None], -(NA_WIN_C - 1), NA_WIN_C - 1) + NA_WIN_C - 1
    sel_c = (dc[:, :, None] == np.arange(2 * NA_WIN_C - 1)).astype(np.float32)
    t = jnp.einsum("hdj,qcj->hdqc", rel_bias.astype(F32), jnp.asarray(sel_c), precision=HIGHEST)
    t = jnp.where(jnp.asarray(col_ok)[None, None], t * LOG2E, NEG)
    return pl.pallas_call(
        _na_bias_kernel,
        grid=(NA_H,),
        in_specs=[pl.BlockSpec((1, N_DR, GRID_W, GRID_W), lambda h: (h, 0, 0, 0))],
        out_specs=pl.BlockSpec((1, L_LAT, L_LAT), lambda h: (h, 0, 0)),
        out_shape=jax.ShapeDtypeStruct((NA_H, L_LAT, L_LAT), F32),
        compiler_params=_cparams(("arbitrary",)),
        name="na_bias",
    )(t)


def _na_lat_kernel(q_ref, k_ref, v_ref, kc_ref, vc_ref, b_ref, o_ref):
    lo = _low_half()
    for p in range(NA_H // 2):
        qb = _pair(q_ref, p) * NA_QSCALE
        kb = _pair(k_ref, p).astype(BF16)
        vb = _pair(v_ref, p).astype(BF16)
        outs = []
        for e in range(2):
            h = 2 * p + e
            half = slice(e * NA_D, (e + 1) * NA_D)
            q = jnp.where(lo if e == 0 else jnp.logical_not(lo), qb, 0.0).astype(BF16)
            s1 = lax.dot_general(q, kb, NT, preferred_element_type=F32) + b_ref[h]
            s2 = jnp.dot(qb[:, half].astype(BF16), kc_ref[0, 0, h].astype(BF16), preferred_element_type=F32)
            m = jnp.maximum(jnp.max(s1, axis=-1, keepdims=True), jnp.max(s2, axis=-1, keepdims=True))
            p1, p2 = jnp.exp2(s1 - m), jnp.exp2(s2 - m)
            den = jnp.sum(p1, axis=-1, keepdims=True) + jnp.sum(p2, axis=-1, keepdims=True)
            a1 = jnp.dot(p1.astype(BF16), vb, preferred_element_type=F32)
            a2 = lax.dot_general(p2.astype(BF16), vc_ref[0, 0, h].astype(BF16), NT, preferred_element_type=F32)
            outs.append((a1[:, half] + a2) / den)
        o_ref[:, p * LANES:(p + 1) * LANES] = jnp.concatenate(outs, axis=1).astype(o_ref.dtype)


def attn_neighbourhood_latent(q, kv, cache_kt, cache_vt, bias):
    tq = 2 * TM
    nq = L_LAT // tq
    t0 = T_CTX // tq
    s0 = T_CTX // L_LAT
    cache = pl.BlockSpec((1, 1, NA_H, NA_D, PAST), lambda qt, b: (b, 0, 0, 0, 0))
    return pl.pallas_call(
        _na_lat_kernel,
        grid=(nq, N_LAT),
        in_specs=[pl.BlockSpec((tq, NA_W), lambda qt, b: (t0 + b * nq + qt, 0)),
                  pl.BlockSpec((L_LAT, NA_W), lambda qt, b: (s0 + b, 0)),
                  pl.BlockSpec((L_LAT, NA_W), lambda qt, b: (s0 + b, 1)),
                  cache, cache,
                  pl.BlockSpec((NA_H, tq, L_LAT), lambda qt, b: (0, qt, 0), pipeline_mode=pl.Buffered(1))],
        out_specs=pl.BlockSpec((tq, NA_W), lambda qt, b: (b * nq + qt, 0)),
        out_shape=jax.ShapeDtypeStruct((T_LAT, NA_W), BF16),
        compiler_params=_cparams(("arbitrary", "arbitrary")),
        name="attn_na_lat",
    )(q, kv, kv, cache_kt, cache_vt, bias)


def _mla_lat_kernel(qm_ref, kvu_ref, kr_ref, ckv_ref, krc_ref, wukv_ref, o_ref):
    lo = _low_half()
    kvc = jnp.dot(ckv_ref[0, 0].astype(BF16), wukv_ref[...], preferred_element_type=F32)
    kr_lo, kr_hi = _rope_key_forms(kr_ref[...])
    krc = jnp.concatenate([krc_ref[0, 0], jnp.zeros((PAST, LANES - MLA_ROPE), F32)], axis=1)
    krc_lo, krc_hi = _rope_key_forms(krc)
    vbase = MLA_H * MLA_NOPE
    for p in range(MLA_H // 2):
        lat = (_pair(kvu_ref, p).astype(BF16), kr_lo, kr_hi, _pair(kvu_ref, p, vbase).astype(BF16))
        ctx = (_pair(kvc, p).astype(BF16), krc_lo, krc_hi, _pair(kvc, p, vbase).astype(BF16))
        o_ref[:, p * LANES:(p + 1) * LANES] = _mla_pair(qm_ref, p, [lat, ctx], lo).astype(o_ref.dtype)


def attn_mla_latent(qm, kvu, kr, cache_ckv, cache_krope, wukv_bf):
    tq = 2 * TM
    nq = L_LAT // tq
    t0 = T_CTX // tq
    s0 = T_CTX // L_LAT
    return pl.pallas_call(
        _mla_lat_kernel,
        grid=(nq, N_LAT),
        in_specs=[pl.BlockSpec((tq, MLA_H * MLA_QK), lambda qt, b: (t0 + b * nq + qt, 0)),
                  pl.BlockSpec((L_LAT, MLA_H * (MLA_NOPE + MLA_V)), lambda qt, b: (s0 + b, 0)),
                  pl.BlockSpec((L_LAT, LANES), lambda qt, b: (s0 + b, 0)),
                  pl.BlockSpec((1, 1, PAST, MLA_KVR), lambda qt, b: (b, 0, 0, 0)),
                  pl.BlockSpec((1, 1, PAST, MLA_ROPE), lambda qt, b: (b, 0, 0, 0)),
                  pl.BlockSpec(wukv_bf.shape, lambda qt, b: (0, 0))],
        out_specs=pl.BlockSpec((tq, MLA_H * MLA_V), lambda qt, b: (b * nq + qt, 0)),
        out_shape=jax.ShapeDtypeStruct((T_LAT, MLA_H * MLA_V), BF16),
        compiler_params=_cparams(("arbitrary", "arbitrary")),
        name="attn_mla_lat",
    )(qm, kvu, kr, cache_ckv, cache_krope, wukv_bf)


def moe_block(h2, ids, wts, cnt3, x, mod, gfin, w_gate, w_up, w_down, layer, *, final):
    cnt, loc, gdst, ends, tile_expert, n_used = route_tables(cnt3)
    xs, dest = dispatch_rows(h2, ids, cnt, loc, gdst, ends)
    ys = grouped_experts(xs, w_gate, w_up, w_down, tile_expert, n_used, layer)
    return moe_combine(ys, dest, x, wts, mod, gfin, final=final)


def _pad_lanes(a):
    return jnp.pad(a, ((0, 0), (0, LANES - a.shape[1])))


def _hyena_features(L):
    t = np.linspace(0.0, 1.0, L)[:, None]
    w = 2.0 * math.pi * np.arange(L) / L
    bands = np.linspace(1e-4, HY_BANDS - 1, HY_BANDS)
    ang = w[:, None] * bands[None]
    feat = np.concatenate([t, np.cos(ang), -np.sin(ang)], axis=-1)
    return jnp.asarray(np.pad(feat, ((0, 0), (0, LANES - HY_FEAT))).astype(np.float32))


def _router_params(w_gr, b_gr, w_er, b_er):
    wr = _pad_lanes(jnp.concatenate([w_gr, w_er], axis=1))
    br = _pad_lanes(jnp.concatenate([b_gr, b_er])[None])
    wr_hi = wr.astype(BF16)
    wr_lo = (wr - wr_hi.astype(F32)).astype(BF16)
    return jnp.stack([wr_hi, wr_lo]), br


def _even_layer(x, mod, g_mix, state, w_in, conv_w, conv_b, a_log, dt_bias, d_skip, g_ssd, hy_conv_w, hy_conv_b,
                hy_w1, hy_b1, hy_w2, hy_b2, hy_w3, hy_freq, hy_bias):
    n0 = D + SSD_XBC
    w_parts = (w_in[:, :n0].astype(BF16), w_in[:, n0 + SSD_H:].astype(BF16),
               _pad_lanes(w_in[:, n0:n0 + SSD_H]).astype(BF16))
    z, xbc, hy, dtr = even_in_proj(x, mod, g_mix, w_parts)
    small = (conv_w, conv_b[None], _pad_lanes(dt_bias), _pad_lanes(a_log), jnp.repeat(d_skip, SSD_P)[None], g_ssd[None])
    y_c, fin = ssd_mixer(xbc, dtr, z, None, *small, L=L_CTX, n_seq=N_CTX, row_off=0)
    (y_l,) = ssd_mixer(xbc, dtr, z, state.reshape(N_LAT, 2, SSD_H * SSD_P, SSD_N), *small,
                       L=L_LAT, n_seq=N_LAT, row_off=T_CTX)
    w1 = jnp.pad(hy_w1, ((0, LANES - HY_FEAT), (0, 0)))
    w3r = hy_w3.reshape(HY_HID, 4, D).transpose(1, 0, 2)
    deltas = jnp.asarray(np.linspace(HY_MIN_DECAY, HY_MAX_DECAY, D).astype(np.float32))[None]
    us = []
    for L, n_seq, off in ((L_CTX, N_CTX, 0), (L_LAT, N_LAT, T_CTX)):
        h4, hm = hyena_filter_spectra(_hyena_features(L), w1, hy_b1[None], hy_w2, hy_b2[None], hy_freq, w3r, deltas, L=L)
        us.append(hyena_mixer(hy, hy_conv_w, hy_conv_b[None], h4, hm, hy_bias, L=L, n_seq=n_seq, row_off=off))
    return (y_c, y_l), tuple(us), fin


def _odd_layer(x, mod, g_mix, cache_k, cache_v, cache_ckv, cache_kr, rel_bias, w_in, g_q, w_uq, g_kv, w_ukv):
    w_bf = jnp.pad(w_in, ((0, 0), (0, ODD_COLS - w_in.shape[1]))).astype(BF16)
    wuq = w_uq.reshape(MLA_QR, MLA_H, MLA_QK)
    wuq_bf = jnp.concatenate([wuq[:, :, :MLA_NOPE].reshape(MLA_QR, -1), wuq[:, :, MLA_NOPE:].reshape(MLA_QR, -1)],
                             axis=1).astype(BF16)
    wukv = w_ukv.reshape(MLA_KVR, MLA_H, MLA_NOPE + MLA_V)
    wukv_bf = jnp.concatenate([wukv[:, :, :MLA_NOPE].reshape(MLA_KVR, -1), wukv[:, :, MLA_NOPE:].reshape(MLA_KVR, -1)],
                              axis=1).astype(BF16)
    q, kv, qm, ckv, kvu, kr, k_new, v_new = odd_in_proj(x, mod, g_mix, w_bf, g_q[None], wuq_bf, g_kv[None], wukv_bf,
                                                        rope_tables())
    ona_c, omla_c = attn_context(q, kv, qm, kvu, kr)
    ona_l = attn_neighbourhood_latent(q, kv, jnp.swapaxes(cache_k, 3, 4), jnp.swapaxes(cache_v, 3, 4),
                                      neighbourhood_bias(rel_bias))
    omla_l = attn_mla_latent(qm, kvu, kr, cache_ckv, cache_kr, wukv_bf)
    return (ona_c, ona_l), (omla_c, omla_l), k_new, v_new, ckv, kr


def kernel(x_prompt, x_sample, state_ssd, cache_na_k, cache_na_v, cache_mla_ckv, cache_mla_krope, c, c_ctx, w_ada, b_ada, norm_mix, norm_ffn, norm_final, ev_w_in, ev_conv_w, ev_conv_b, ssd_A_log, ssd_dt_bias, ssd_d, ssd_norm, hy_conv_w, hy_conv_b, hy_w1, hy_b1, hy_w2, hy_b2, hy_w3, hy_freq, hy_bias, ev_w_out, od_w_in, mla_q_norm, mla_w_uq, mla_kv_norm, mla_w_ukv, na_rel_bias, od_w_out, moe_w_gr, moe_b_gr, moe_w_er, moe_b_er, moe_w_gate, moe_w_up, moe_w_down):
    x = (x_prompt.reshape(T_CTX, D), x_sample.reshape(T_LAT, D))
    cvec = jnp.zeros((MOD_ROWS, D), F32).at[0].set(c_ctx).at[1:1 + N_LAT].set(c)
    mod = ada_modulation(cvec, w_ada, b_ada)
    gfin = norm_final[None]

    y, u, fin = _even_layer(x, mod[0], norm_mix[0][None], state_ssd[:, 0], ev_w_in[0], ev_conv_w[0], ev_conv_b[0],
                            ssd_A_log[0], ssd_dt_bias[0], ssd_d[0], ssd_norm[0], hy_conv_w[0], hy_conv_b[0],
                            hy_w1[0], hy_b1[0], hy_w2[0], hy_b2[0], hy_w3[0], hy_freq[0], hy_bias[0])
    wr, br = _router_params(moe_w_gr[0], moe_b_gr[0], moe_w_er[0], moe_b_er[0])
    xn, h2, ids, wts, cnt3 = out_proj_router([y, u], ev_w_out[0].astype(BF16), x, mod[0], norm_ffn[0][None], wr, br)
    x = moe_block(h2, ids, wts, cnt3, xn, mod[0], gfin, moe_w_gate, moe_w_up, moe_w_down, 0, final=False)

    o_na, o_mla, k_new, v_new, ckv, kr = _odd_layer(x, mod[1], norm_mix[1][None], cache_na_k, cache_na_v, cache_mla_ckv,
                                           cache_mla_krope, na_rel_bias[0], od_w_in[0], mla_q_norm[0], mla_w_uq[0],
                                           mla_kv_norm[0], mla_w_ukv[0])
    wr, br = _router_params(moe_w_gr[1], moe_b_gr[1], moe_w_er[1], moe_b_er[1])
    xn, h2, ids, wts, cnt3 = out_proj_router([o_na, o_mla], od_w_out[0].astype(BF16), x, mod[1], norm_ffn[1][None], wr, br)
    y_c, y_l = moe_block(h2, ids, wts, cnt3, xn, mod[1], gfin, moe_w_gate, moe_w_up, moe_w_down, 1, final=True)

    return (y_c.reshape(N_CTX, L_CTX, D),
            y_l.reshape(N_LAT, L_LAT, D),
            fin.reshape(N_CTX, 1, 2, SSD_H, SSD_P, SSD_N),
            jnp.swapaxes(k_new, 2, 3)[:, None],
            jnp.swapaxes(v_new, 2, 3)[:, None],
            ckv[:T_CTX].reshape(N_CTX, 1, L_CTX, MLA_KVR),
            kr[:T_CTX, :MLA_ROPE].reshape(N_CTX, 1, L_CTX, MLA_ROPE))
```
